```python
import math
import jax, jax.numpy as jnp
from jax import lax
import numpy as np

D_MODEL = 2048
BATCH = 8
SEQ = 4096
DEPTH = 1

MLA_NOPE = 128
MLA_ROPE = 64
MLA_V = 128
MLA_HEADS = D_MODEL // MLA_V
MLA_Q_RANK = 768
MLA_KV_RANK = 512
MLA_QK = MLA_NOPE + MLA_ROPE
ROPE_THETA = 10000.0
SWA_HEAD_DIM = 64
SWA_HEADS = D_MODEL // SWA_HEAD_DIM
SWA_KV_HEADS = 4
SWA_GROUP = SWA_HEADS // SWA_KV_HEADS
WINDOW = 128
BLOCK = 128
REL_BUCKETS = 32
REL_MAX_DIST = 128
D_FF = 5632
CONV_WIDTH = 3
EPS = 1e-6
NEG = -1e30

MLA_IN = MLA_Q_RANK + MLA_KV_RANK + MLA_ROPE
SWA_Q = SWA_HEADS * SWA_HEAD_DIM
SWA_KV = SWA_KV_HEADS * SWA_HEAD_DIM
N_BRANCH = 2
IN_COLS = MLA_IN + SWA_Q + 2 * SWA_KV + N_BRANCH * D_MODEL

kernel_name = "hybrid_mla_swa_convffn_block"


def rms_norm(x, g):
    xf = x.astype(jnp.float32)
    y = xf * lax.rsqrt(jnp.mean(xf * xf, axis=-1, keepdims=True) + EPS)
    return (y * g.astype(jnp.float32)).astype(x.dtype)


def rope_tables(seq):
    pos = jnp.arange(seq, dtype=jnp.float32)
    inv = ROPE_THETA ** (-jnp.arange(0, MLA_ROPE, 2, dtype=jnp.float32) / MLA_ROPE)
    ang = pos[:, None] * inv[None, :]
    ang = jnp.concatenate([ang, ang], axis=-1)
    return jnp.cos(ang), jnp.sin(ang)


def apply_rope(x, cos, sin):
    half = x.shape[-1] // 2
    x1, x2 = x[..., :half], x[..., half:]
    rot = jnp.concatenate([-x2, x1], axis=-1)
    return x * cos.astype(x.dtype) + rot * sin.astype(x.dtype)


def t5_bucket(dist):
    max_exact = REL_BUCKETS // 2
    n = jnp.maximum(dist, 0)
    large = max_exact + (jnp.log(jnp.maximum(n, 1).astype(jnp.float32) / max_exact)
                         / math.log(REL_MAX_DIST / max_exact)
                         * (REL_BUCKETS - max_exact)).astype(jnp.int32)
    large = jnp.minimum(large, REL_BUCKETS - 1)
    return jnp.where(n < max_exact, n, large)


def mla_branch(cq, ckv, k_rope, g_q, w_uq, g_kv, w_ukv):
    B, S, _ = cq.shape
    nb = S // BLOCK
    cos, sin = rope_tables(S)
    q = (rms_norm(cq, g_q) @ w_uq).reshape(B, S, MLA_HEADS, MLA_QK)
    q_nope = q[..., :MLA_NOPE]
    q_rope = apply_rope(q[..., MLA_NOPE:], cos[:, None, :], sin[:, None, :])
    kv = (rms_norm(ckv, g_kv) @ w_ukv).reshape(B, S, MLA_HEADS, MLA_NOPE + MLA_V)
    k_nope, v = kv[..., :MLA_NOPE], kv[..., MLA_NOPE:]
    k_rope = apply_rope(k_rope, cos, sin)
    scale = MLA_QK ** -0.5
    qn_blocks = q_nope.reshape(B, nb, BLOCK, MLA_HEADS, MLA_NOPE).transpose(1, 0, 2, 3, 4)
    qr_blocks = q_rope.reshape(B, nb, BLOCK, MLA_HEADS, MLA_ROPE).transpose(1, 0, 2, 3, 4)
    kpos = jnp.arange(S)

    def one_block(args):
        qn, qr, i = args
        s = (jnp.einsum('bqhd,bkhd->bhqk', qn, k_nope)
             + jnp.einsum('bqhd,bkd->bhqk', qr, k_rope)).astype(jnp.float32) * scale
        qpos = i * BLOCK + jnp.arange(BLOCK)
        s = jnp.where(kpos[None, :] <= qpos[:, None], s, NEG)
        p = jax.nn.softmax(s, axis=-1).astype(v.dtype)
        return jnp.einsum('bhqk,bkhd->bqhd', p, v)

    out = lax.map(one_block, (qn_blocks, qr_blocks, jnp.arange(nb)))
    return out.transpose(1, 0, 2, 3, 4).reshape(B, S, MLA_HEADS * MLA_V)


def swa_branch(q, k, v, rel_bias, sinks):
    B, S, _ = q.shape
    nb = S // BLOCK
    q = q.reshape(B, nb, BLOCK, SWA_KV_HEADS, SWA_GROUP, SWA_HEAD_DIM)

    def band(t):
        t = t.reshape(B, S, SWA_KV_HEADS, SWA_HEAD_DIM)
        t = jnp.pad(t, ((0, 0), (BLOCK, 0), (0, 0), (0, 0)))
        t = t.reshape(B, nb + 1, BLOCK, SWA_KV_HEADS, SWA_HEAD_DIM)
        return jnp.concatenate([t[:, :-1], t[:, 1:]], axis=2)

    kb, vb = band(k), band(v)
    s = jnp.einsum('bnqhgd,bnshd->bhgnqs', q, kb).astype(jnp.float32) * (SWA_HEAD_DIM ** -0.5)
    a = jnp.arange(BLOCK)
    bidx = jnp.arange(2 * BLOCK)
    dist = BLOCK + a[:, None] - bidx[None, :]
    bias = rel_bias[t5_bucket(dist)].astype(jnp.float32)
    bias = bias.transpose(2, 0, 1).reshape(SWA_KV_HEADS, SWA_GROUP, 1, BLOCK, 2 * BLOCK)
    kpos = jnp.arange(nb)[:, None, None] * BLOCK - BLOCK + bidx[None, None, :]
    mask = (dist >= 0)[None] & (dist < WINDOW)[None] & (kpos >= 0)
    s = jnp.where(mask, s + bias, NEG)
    sink = sinks.astype(jnp.float32).reshape(SWA_KV_HEADS, SWA_GROUP, 1, 1, 1)
    m = jnp.maximum(jnp.max(s, axis=-1, keepdims=True), sink)
    e = jnp.exp(s - m)
    p = e / (jnp.sum(e, axis=-1, keepdims=True) + jnp.exp(sink - m))
    o = jnp.einsum('bhgnqs,bnshd->bnqhgd', p.astype(vb.dtype), vb)
    return o.reshape(B, S, SWA_Q)


def causal_dwconv(u, w, b):
    S = u.shape[1]
    up = jnp.pad(u, ((0, 0), (CONV_WIDTH - 1, 0), (0, 0)))
    y = b
    for j in range(CONV_WIDTH):
        y = y + w[j] * up[:, j:j + S]
    return y


def _fwd_setup_inputs(seed: int = 0) -> dict:
    key = jax.random.key(seed)
    ks = jax.random.split(key, 24)
    f32 = jnp.float32
    D, L = D_MODEL, DEPTH

    def nrm(k, shape, scale):
        return jax.random.normal(k, shape, f32) * scale

    def gain(k, shape):
        return 1.0 + 0.1 * jax.random.normal(k, shape, f32)

    return {
        "x": nrm(ks[0], (BATCH, SEQ, D), 1.0),
        "c": nrm(ks[1], (BATCH, D), 1.0),
        "w_ada": nrm(ks[2], (L, D, 6 * D), 0.5 * D ** -0.5),
        "b_ada": nrm(ks[3], (L, 6 * D), 0.02),
        "g_pre_mix": gain(ks[4], (L, D)),
        "g_post_mix": gain(ks[5], (L, D)),
        "w_in": nrm(ks[6], (L, D, IN_COLS), D ** -0.5),
        "g_q_lat": gain(ks[7], (L, MLA_Q_RANK)),
        "w_uq": nrm(ks[8], (L, MLA_Q_RANK, MLA_HEADS * MLA_QK), MLA_Q_RANK ** -0.5),
        "g_kv_lat": gain(ks[9], (L, MLA_KV_RANK)),
        "w_ukv": nrm(ks[10], (L, MLA_KV_RANK, MLA_HEADS * (MLA_NOPE + MLA_V)), MLA_KV_RANK ** -0.5),
        "rel_bias": nrm(ks[11], (REL_BUCKETS, SWA_HEADS), 0.5),
        "sinks": nrm(ks[12], (L, SWA_HEADS), 1.0),
        "w_o": nrm(ks[13], (L, D, D), D ** -0.5),
        "g_pre_ffn": gain(ks[14], (L, D)),
        "g_post_ffn": gain(ks[15], (L, D)),
        "w_up": nrm(ks[16], (L, D, 2 * D_FF), D ** -0.5),
        "conv_w": nrm(ks[17], (L, CONV_WIDTH, 2 * D_FF), CONV_WIDTH ** -0.5),
        "conv_b": nrm(ks[18], (L, 2 * D_FF), 0.02),
        "w_down": nrm(ks[19], (L, D_FF, D), D_FF ** -0.5),
    }


def _fwd_reference(x, c, w_ada, b_ada, g_pre_mix, g_post_mix, w_in, g_q_lat, w_uq, g_kv_lat,
              w_ukv, rel_bias, sinks, w_o, g_pre_ffn, g_post_ffn, w_up, conv_w, conv_b, w_down):
    D = D_MODEL
    c_act = jax.nn.silu(c)
    for l in range(DEPTH):
        mod = (c_act @ w_ada[l] + b_ada[l])[:, None, :]
        sh1, sc1, gt1, sh2, sc2, gt2 = jnp.split(mod, 6, axis=-1)

        h = rms_norm(x, g_pre_mix[l]) * (1.0 + sc1) + sh1
        z = h @ w_in[l]
        o0 = 0
        cq = z[..., o0:o0 + MLA_Q_RANK]; o0 += MLA_Q_RANK
        ckv = z[..., o0:o0 + MLA_KV_RANK]; o0 += MLA_KV_RANK
        kr = z[..., o0:o0 + MLA_ROPE]; o0 += MLA_ROPE
        qs = z[..., o0:o0 + SWA_Q]; o0 += SWA_Q
        ks_ = z[..., o0:o0 + SWA_KV]; o0 += SWA_KV
        vs = z[..., o0:o0 + SWA_KV]; o0 += SWA_KV
        gates = jax.nn.sigmoid(z[..., o0:o0 + N_BRANCH * D])
        g_a, g_b = gates[..., :D], gates[..., D:]

        o_a = mla_branch(cq, ckv, kr, g_q_lat[l], w_uq[l], g_kv_lat[l], w_ukv[l])
        o_b = swa_branch(qs, ks_, vs, rel_bias, sinks[l])
        mix = (g_a * o_a + g_b * o_b) @ w_o[l]
        x = x + gt1 * rms_norm(mix, g_post_mix[l])

        h = rms_norm(x, g_pre_ffn[l]) * (1.0 + sc2) + sh2
        u = causal_dwconv(h @ w_up[l], conv_w[l], conv_b[l])
        y = (jax.nn.silu(u[..., :D_FF]) * u[..., D_FF:]) @ w_down[l]
        x = x + gt2 * rms_norm(y, g_post_ffn[l])
    return x


import jax as _jax
import jax.numpy as _jnp

TWIN_FORMAT = 'train_step'
FWD_PARAMS = ['x', 'c', 'w_ada', 'b_ada', 'g_pre_mix', 'g_post_mix', 'w_in', 'g_q_lat', 'w_uq', 'g_kv_lat', 'w_ukv', 'rel_bias', 'sinks', 'w_o', 'g_pre_ffn', 'g_post_ffn', 'w_up', 'conv_w', 'conv_b', 'w_down']
TWIN_WEIGHTS = ['w_ada', 'b_ada', 'g_pre_mix', 'g_post_mix', 'w_in', 'g_q_lat', 'w_uq', 'g_kv_lat', 'w_ukv', 'rel_bias', 'sinks', 'w_o', 'g_pre_ffn', 'g_post_ffn', 'w_up', 'conv_w', 'conv_b', 'w_down']
TWIN_DIFF_INPUT = 'x'
TWIN_INPUTS = ['x', 'c', 'w_ada', 'b_ada', 'g_pre_mix', 'g_post_mix', 'w_in', 'g_q_lat', 'w_uq', 'g_kv_lat', 'w_ukv', 'rel_bias', 'sinks', 'w_o', 'g_pre_ffn', 'g_post_ffn', 'w_up', 'conv_w', 'conv_b', 'w_down', 'loss_target', 'm_w_ada', 'm_b_ada', 'm_g_pre_mix', 'm_g_post_mix', 'm_w_in', 'm_g_q_lat', 'm_w_uq', 'm_g_kv_lat', 'm_w_ukv', 'm_rel_bias', 'm_sinks', 'm_w_o', 'm_g_pre_ffn', 'm_g_post_ffn', 'm_w_up', 'm_conv_w', 'm_conv_b', 'm_w_down', 'v_w_ada', 'v_b_ada', 'v_g_pre_mix', 'v_g_post_mix', 'v_w_in', 'v_g_q_lat', 'v_w_uq', 'v_g_kv_lat', 'v_w_ukv', 'v_rel_bias', 'v_sinks', 'v_w_o', 'v_g_pre_ffn', 'v_g_post_ffn', 'v_w_up', 'v_conv_w', 'v_conv_b', 'v_w_down']
TWIN_OUTPUTS = ['loss', 'grad_x', 'grad_w_ada', 'grad_b_ada', 'grad_g_pre_mix', 'grad_g_post_mix', 'grad_w_in', 'grad_g_q_lat', 'grad_w_uq', 'grad_g_kv_lat', 'grad_w_ukv', 'grad_rel_bias', 'grad_sinks', 'grad_w_o', 'grad_g_pre_ffn', 'grad_g_post_ffn', 'grad_w_up', 'grad_conv_w', 'grad_conv_b', 'grad_w_down', 'delta_w_ada', 'delta_b_ada', 'delta_g_pre_mix', 'delta_g_post_mix', 'delta_w_in', 'delta_g_q_lat', 'delta_w_uq', 'delta_g_kv_lat', 'delta_w_ukv', 'delta_rel_bias', 'delta_sinks', 'delta_w_o', 'delta_g_pre_ffn', 'delta_g_post_ffn', 'delta_w_up', 'delta_conv_w', 'delta_conv_b', 'delta_w_down', 'new_m_w_ada', 'new_m_b_ada', 'new_m_g_pre_mix', 'new_m_g_post_mix', 'new_m_w_in', 'new_m_g_q_lat', 'new_m_w_uq', 'new_m_g_kv_lat', 'new_m_w_ukv', 'new_m_rel_bias', 'new_m_sinks', 'new_m_w_o', 'new_m_g_pre_ffn', 'new_m_g_post_ffn', 'new_m_w_up', 'new_m_conv_w', 'new_m_conv_b', 'new_m_w_down', 'new_v_w_ada', 'new_v_b_ada', 'new_v_g_pre_mix', 'new_v_g_post_mix', 'new_v_w_in', 'new_v_g_q_lat', 'new_v_w_uq', 'new_v_g_kv_lat', 'new_v_w_ukv', 'new_v_rel_bias', 'new_v_sinks', 'new_v_w_o', 'new_v_g_pre_ffn', 'new_v_g_post_ffn', 'new_v_w_up', 'new_v_conv_w', 'new_v_conv_b', 'new_v_w_down']
TWIN_LEAF_KINDS = {'loss': 'loss', 'grad_x': 'grad_x', 'grad_w_ada': 'grad_w', 'grad_b_ada': 'grad_w', 'grad_g_pre_mix': 'grad_w', 'grad_g_post_mix': 'grad_w', 'grad_w_in': 'grad_w', 'grad_g_q_lat': 'grad_w', 'grad_w_uq': 'grad_w', 'grad_g_kv_lat': 'grad_w', 'grad_w_ukv': 'grad_w', 'grad_rel_bias': 'grad_w', 'grad_sinks': 'grad_w', 'grad_w_o': 'grad_w', 'grad_g_pre_ffn': 'grad_w', 'grad_g_post_ffn': 'grad_w', 'grad_w_up': 'grad_w', 'grad_conv_w': 'grad_w', 'grad_conv_b': 'grad_w', 'grad_w_down': 'grad_w', 'delta_w_ada': 'delta_w', 'delta_b_ada': 'delta_w', 'delta_g_pre_mix': 'delta_w', 'delta_g_post_mix': 'delta_w', 'delta_w_in': 'delta_w', 'delta_g_q_lat': 'delta_w', 'delta_w_uq': 'delta_w', 'delta_g_kv_lat': 'delta_w', 'delta_w_ukv': 'delta_w', 'delta_rel_bias': 'delta_w', 'delta_sinks': 'delta_w', 'delta_w_o': 'delta_w', 'delta_g_pre_ffn': 'delta_w', 'delta_g_post_ffn': 'delta_w', 'delta_w_up': 'delta_w', 'delta_conv_w': 'delta_w', 'delta_conv_b': 'delta_w', 'delta_w_down': 'delta_w', 'new_m_w_ada': 'new_m', 'new_m_b_ada': 'new_m', 'new_m_g_pre_mix': 'new_m', 'new_m_g_post_mix': 'new_m', 'new_m_w_in': 'new_m', 'new_m_g_q_lat': 'new_m', 'new_m_w_uq': 'new_m', 'new_m_g_kv_lat': 'new_m', 'new_m_w_ukv': 'new_m', 'new_m_rel_bias': 'new_m', 'new_m_sinks': 'new_m', 'new_m_w_o': 'new_m', 'new_m_g_pre_ffn': 'new_m', 'new_m_g_post_ffn': 'new_m', 'new_m_w_up': 'new_m', 'new_m_conv_w': 'new_m', 'new_m_conv_b': 'new_m', 'new_m_w_down': 'new_m', 'new_v_w_ada': 'new_v', 'new_v_b_ada': 'new_v', 'new_v_g_pre_mix': 'new_v', 'new_v_g_post_mix': 'new_v', 'new_v_w_in': 'new_v', 'new_v_g_q_lat': 'new_v', 'new_v_w_uq': 'new_v', 'new_v_g_kv_lat': 'new_v', 'new_v_w_ukv': 'new_v', 'new_v_rel_bias': 'new_v', 'new_v_sinks': 'new_v', 'new_v_w_o': 'new_v', 'new_v_g_pre_ffn': 'new_v', 'new_v_g_post_ffn': 'new_v', 'new_v_w_up': 'new_v', 'new_v_conv_w': 'new_v', 'new_v_conv_b': 'new_v', 'new_v_w_down': 'new_v'}


def _forward(args):
    return _fwd_reference(*[args[k] for k in FWD_PARAMS])


def _output_shape():
    def fwd():
        inp = _fwd_setup_inputs(0)
        return _fwd_reference(*[inp[k] for k in FWD_PARAMS])
    out = _jax.eval_shape(fwd)
    return out.shape, out.dtype

N_MICROBATCH = 1
ADAM_LR = 0.001
ADAM_B1 = 0.9
ADAM_B2 = 0.999
ADAM_EPS = 1e-08
ADAM_WD = 0.01
ADAM_STEP = 10
PER_EXAMPLE_BATCH_AXIS = {'x': 0, 'c': 0, 'loss_target': 0}
SHARED_INPUTS = []
_WEIGHT_DTYPES = {'w_ada': _jnp.float32, 'b_ada': _jnp.float32, 'g_pre_mix': _jnp.float32, 'g_post_mix': _jnp.float32, 'w_in': _jnp.float32, 'g_q_lat': _jnp.float32, 'w_uq': _jnp.float32, 'g_kv_lat': _jnp.float32, 'w_ukv': _jnp.float32, 'rel_bias': _jnp.float32, 'sinks': _jnp.float32, 'w_o': _jnp.float32, 'g_pre_ffn': _jnp.float32, 'g_post_ffn': _jnp.float32, 'w_up': _jnp.float32, 'conv_w': _jnp.float32, 'conv_b': _jnp.float32, 'w_down': _jnp.float32}
MOMENT_SCALE = {'w_ada': 8.652908e-01, 'b_ada': 1.693259e+00, 'g_pre_mix': 5.610645e-02, 'g_post_mix': 1.822707e+00, 'w_in': 2.652674e-01, 'g_q_lat': 2.523852e-02, 'w_uq': 1.245823e-02, 'g_kv_lat': 6.337161e-01, 'w_ukv': 2.410437e-01, 'rel_bias': 2.360087e-02, 'sinks': 1.561321e-02, 'w_o': 5.012553e-01, 'g_pre_ffn': 7.245424e-02, 'g_post_ffn': 1.648437e+00, 'w_up': 3.933905e-02, 'conv_w': 4.094540e-02, 'conv_b': 9.086490e-02, 'w_down': 7.857814e-02}


def _to_microbatches(a, axis):
    t = _jnp.moveaxis(a, axis, 0)
    t = t.reshape((N_MICROBATCH, t.shape[0] // N_MICROBATCH) + t.shape[1:])
    return _jnp.moveaxis(t, 1, axis + 1)


def setup_inputs(seed: int = 0) -> dict:
    inp = _fwd_setup_inputs(seed)
    key = _jax.random.fold_in(_jax.random.key(seed), 7919)
    shape, _ = _output_shape()
    out = dict(inp)
    out["loss_target"] = _jax.random.normal(_jax.random.fold_in(key, 0), shape, _jnp.float32)
    for i, name in enumerate(TWIN_WEIGHTS):
        w = inp[name].astype(_jnp.float32)
        if MOMENT_SCALE is None:
            s = _jnp.sqrt(_jnp.mean(_jnp.square(w)) + 1e-30)
        else:
            s = MOMENT_SCALE[name]
        km, kv = _jax.random.split(_jax.random.fold_in(key, i + 1))
        out[name] = w
        out["m_" + name] = s * _jax.random.normal(km, w.shape, _jnp.float32)
        out["v_" + name] = (s * s) * _jax.random.uniform(kv, w.shape, _jnp.float32, 0.5, 1.5)
    if N_MICROBATCH > 1:
        for name, axis in PER_EXAMPLE_BATCH_AXIS.items():
            out[name] = _to_microbatches(out[name], axis)
    return {'x': out['x'], 'c': out['c'], 'w_ada': out['w_ada'], 'b_ada': out['b_ada'], 'g_pre_mix': out['g_pre_mix'], 'g_post_mix': out['g_post_mix'], 'w_in': out['w_in'], 'g_q_lat': out['g_q_lat'], 'w_uq': out['w_uq'], 'g_kv_lat': out['g_kv_lat'], 'w_ukv': out['w_ukv'], 'rel_bias': out['rel_bias'], 'sinks': out['sinks'], 'w_o': out['w_o'], 'g_pre_ffn': out['g_pre_ffn'], 'g_post_ffn': out['g_post_ffn'], 'w_up': out['w_up'], 'conv_w': out['conv_w'], 'conv_b': out['conv_b'], 'w_down': out['w_down'], 'loss_target': out['loss_target'], 'm_w_ada': out['m_w_ada'], 'm_b_ada': out['m_b_ada'], 'm_g_pre_mix': out['m_g_pre_mix'], 'm_g_post_mix': out['m_g_post_mix'], 'm_w_in': out['m_w_in'], 'm_g_q_lat': out['m_g_q_lat'], 'm_w_uq': out['m_w_uq'], 'm_g_kv_lat': out['m_g_kv_lat'], 'm_w_ukv': out['m_w_ukv'], 'm_rel_bias': out['m_rel_bias'], 'm_sinks': out['m_sinks'], 'm_w_o': out['m_w_o'], 'm_g_pre_ffn': out['m_g_pre_ffn'], 'm_g_post_ffn': out['m_g_post_ffn'], 'm_w_up': out['m_w_up'], 'm_conv_w': out['m_conv_w'], 'm_conv_b': out['m_conv_b'], 'm_w_down': out['m_w_down'], 'v_w_ada': out['v_w_ada'], 'v_b_ada': out['v_b_ada'], 'v_g_pre_mix': out['v_g_pre_mix'], 'v_g_post_mix': out['v_g_post_mix'], 'v_w_in': out['v_w_in'], 'v_g_q_lat': out['v_g_q_lat'], 'v_w_uq': out['v_w_uq'], 'v_g_kv_lat': out['v_g_kv_lat'], 'v_w_ukv': out['v_w_ukv'], 'v_rel_bias': out['v_rel_bias'], 'v_sinks': out['v_sinks'], 'v_w_o': out['v_w_o'], 'v_g_pre_ffn': out['v_g_pre_ffn'], 'v_g_post_ffn': out['v_g_post_ffn'], 'v_w_up': out['v_w_up'], 'v_conv_w': out['v_conv_w'], 'v_conv_b': out['v_conv_b'], 'v_w_down': out['v_w_down']}


def _loss(weights, diff, rest, loss_target):
    with _jax.named_scope("forward"):
        args = {**rest, TWIN_DIFF_INPUT: diff, **{k: w.astype(_WEIGHT_DTYPES[k]) for k, w in weights.items()}}
        y = _forward(args)
    with _jax.named_scope("loss_head"):
        err = _jnp.square(y.astype(_jnp.float32) - loss_target)
        return 0.5 * _jnp.sum(_jnp.mean(err, axis=-1)) if err.ndim else 0.5 * err


def _adamw(w, g, m, v):
    m = ADAM_B1 * m + (1.0 - ADAM_B1) * g
    v = ADAM_B2 * v + (1.0 - ADAM_B2) * _jnp.square(g)
    m_hat = m / (1.0 - ADAM_B1 ** ADAM_STEP)
    v_hat = v / (1.0 - ADAM_B2 ** ADAM_STEP)
    delta = -ADAM_LR * (m_hat / (_jnp.sqrt(v_hat) + ADAM_EPS) + ADAM_WD * w)
    return delta, m, v


def reference(x, c, w_ada, b_ada, g_pre_mix, g_post_mix, w_in, g_q_lat, w_uq, g_kv_lat, w_ukv, rel_bias, sinks, w_o, g_pre_ffn, g_post_ffn, w_up, conv_w, conv_b, w_down, loss_target, m_w_ada, m_b_ada, m_g_pre_mix, m_g_post_mix, m_w_in, m_g_q_lat, m_w_uq, m_g_kv_lat, m_w_ukv, m_rel_bias, m_sinks, m_w_o, m_g_pre_ffn, m_g_post_ffn, m_w_up, m_conv_w, m_conv_b, m_w_down, v_w_ada, v_b_ada, v_g_pre_mix, v_g_post_mix, v_w_in, v_g_q_lat, v_w_uq, v_g_kv_lat, v_w_ukv, v_rel_bias, v_sinks, v_w_o, v_g_pre_ffn, v_g_post_ffn, v_w_up, v_conv_w, v_conv_b, v_w_down):
    given = dict(x=x, c=c, w_ada=w_ada, b_ada=b_ada, g_pre_mix=g_pre_mix, g_post_mix=g_post_mix, w_in=w_in, g_q_lat=g_q_lat, w_uq=w_uq, g_kv_lat=g_kv_lat, w_ukv=w_ukv, rel_bias=rel_bias, sinks=sinks, w_o=w_o, g_pre_ffn=g_pre_ffn, g_post_ffn=g_post_ffn, w_up=w_up, conv_w=conv_w, conv_b=conv_b, w_down=w_down, loss_target=loss_target, m_w_ada=m_w_ada, m_b_ada=m_b_ada, m_g_pre_mix=m_g_pre_mix, m_g_post_mix=m_g_post_mix, m_w_in=m_w_in, m_g_q_lat=m_g_q_lat, m_w_uq=m_w_uq, m_g_kv_lat=m_g_kv_lat, m_w_ukv=m_w_ukv, m_rel_bias=m_rel_bias, m_sinks=m_sinks, m_w_o=m_w_o, m_g_pre_ffn=m_g_pre_ffn, m_g_post_ffn=m_g_post_ffn, m_w_up=m_w_up, m_conv_w=m_conv_w, m_conv_b=m_conv_b, m_w_down=m_w_down, v_w_ada=v_w_ada, v_b_ada=v_b_ada, v_g_pre_mix=v_g_pre_mix, v_g_post_mix=v_g_post_mix, v_w_in=v_w_in, v_g_q_lat=v_g_q_lat, v_w_uq=v_w_uq, v_g_kv_lat=v_g_kv_lat, v_w_ukv=v_w_ukv, v_rel_bias=v_rel_bias, v_sinks=v_sinks, v_w_o=v_w_o, v_g_pre_ffn=v_g_pre_ffn, v_g_post_ffn=v_g_post_ffn, v_w_up=v_w_up, v_conv_w=v_conv_w, v_conv_b=v_conv_b, v_w_down=v_w_down)
    weights = {n: given[n] for n in TWIN_WEIGHTS}
    shared = {n: given[n] for n in SHARED_INPUTS}
    per_example = {n: given[n] for n in ['x', 'c']}
    grad_fn = _jax.value_and_grad(_loss, argnums=(0, 1))

    def one_microbatch(ex, loss_target):
        ex = dict(ex)
        diff = ex.pop(TWIN_DIFF_INPUT)
        return grad_fn(weights, diff, {**shared, **ex}, loss_target)

    if N_MICROBATCH == 1:
        loss, (grad_w, grad_x) = one_microbatch(per_example, given["loss_target"])
    else:
        def body(carry, xs):
            loss_sum, grad_sum = carry
            l_k, (gw_k, gx_k) = one_microbatch(xs[0], xs[1])
            with _jax.named_scope("update"):
                return (loss_sum + l_k, _jax.tree.map(_jnp.add, grad_sum, gw_k)), gx_k

        init = (_jnp.zeros((), _jnp.float32), _jax.tree.map(_jnp.zeros_like, weights))
        (loss, grad_w), grad_x = _jax.lax.scan(body, init, (per_example, given["loss_target"]))
    with _jax.named_scope("update"):
        delta_w, new_m, new_v = {}, {}, {}
        for n in TWIN_WEIGHTS:
            delta_w[n], new_m[n], new_v[n] = _adamw(weights[n], grad_w[n], given["m_" + n], given["v_" + n])
    return (loss, grad_x, *[grad_w[n] for n in TWIN_WEIGHTS], *[delta_w[n] for n in TWIN_WEIGHTS],
            *[new_m[n] for n in TWIN_WEIGHTS], *[new_v[n] for n in TWIN_WEIGHTS])
```

```python
import functools
import itertools
import math

import numpy as np

import jax
import jax.numpy as jnp
from jax import lax
from jax.experimental import pallas as pl
from jax.experimental.pallas import tpu as pltpu

F32 = jnp.float32
BF16 = jnp.bfloat16

N_DEV = 8
MLA_NOPE = 128
MLA_ROPE = 64
MLA_V = 128
MLA_QK = MLA_NOPE + MLA_ROPE
MLA_QK_PAD = 256
ROPE_HALF = MLA_ROPE // 2
ROPE_THETA = 10000.0
SWA_HD = 64
SWA_KVH = 4
WINDOW = 128
BLOCK = 128
REL_BUCKETS = 32
REL_MAX_DIST = 128
CONV_WIDTH = 3
EPS = 1e-6
NEG = -1e30
ADAM_LR = 0.001
ADAM_B1 = 0.9
ADAM_B2 = 0.999
ADAM_EPS = 1e-08
ADAM_WD = 0.01
ADAM_STEP = 10
LANES = 128
HALO = 16
MESH = pl.DeviceIdType.MESH
HIGHEST = lax.Precision.HIGHEST

NN = (((1,), (0,)), ((), ()))
NT = (((1,), (1,)), ((), ()))
TN = (((0,), (0,)), ((), ()))


def _tile(n, pref, align=LANES):
    if n <= pref:
        return n
    t = (pref // align) * align
    while t >= align:
        if n % t == 0:
            return t
        t -= align
    return n


def _round_up(n, m):
    return (n + m - 1) // m * m


def _params(*sem):
    return pltpu.CompilerParams(dimension_semantics=sem)


def _sigmoid(x):
    return 1.0 / (1.0 + jnp.exp(-x))


def _my_place():
    return lax.axis_index("x"), lax.axis_index("y"), lax.axis_index("c")


def _all_gather(x, *, name, in_vmem):
    space = pltpu.VMEM if in_vmem else pl.ANY

    def body(x_ref, out_ref, send_sems, recv_sems, local_sem):
        x_, y_, c_ = _my_place()
        me, sibling = (x_, y_, c_), (x_, y_, 1 - c_)
        chips = [(1 - x_, y_), (x_, 1 - y_), (1 - x_, 1 - y_)]

        def slot(px, py, pc):
            return out_ref.at[4 * px + 2 * py + pc]

        def copy(k, block, to, src=None):
            return pltpu.make_async_remote_copy(
                src_ref=slot(*block) if src is None else src,
                dst_ref=slot(*block),
                send_sem=send_sems.at[k],
                recv_sem=recv_sems.at[k],
                device_id=to,
                device_id_type=MESH,
            )

        mine = pltpu.make_async_copy(x_ref, slot(*me), local_sem)
        mine.start()
        first = [copy(0, me, sibling, src=x_ref)]
        first += [copy(1 + j, me, (*chip, c_), src=x_ref) for j, chip in enumerate(chips)]
        for cp in first:
            cp.start()
        passed = [copy(4 + j, (*chip, c_), sibling) for j, chip in enumerate(chips)]
        for j, chip in enumerate(chips):
            copy(1 + j, (*chip, c_), me).wait_recv()
            passed[j].start()
        copy(0, sibling, me).wait_recv()
        for j, chip in enumerate(chips):
            copy(4 + j, (*chip, 1 - c_), me).wait_recv()
        for cp in first + passed:
            cp.wait_send()
        mine.wait()

    return pl.pallas_call(
        body,
        name=name,
        out_shape=jax.ShapeDtypeStruct((N_DEV,) + x.shape, x.dtype),
        in_specs=[pl.BlockSpec(memory_space=space)],
        out_specs=pl.BlockSpec(memory_space=space),
        scratch_shapes=[
            pltpu.SemaphoreType.DMA((7,)),
            pltpu.SemaphoreType.DMA((7,)),
            pltpu.SemaphoreType.DMA,
        ],
    )(x)


def _all_to_all(x, *, name):
    def body(x_ref, out_ref, send_sems, recv_sems, local_sem):
        x_, y_, c_ = _my_place()
        me = 4 * x_ + 2 * y_ + c_

        def peer(r):
            return (x_ ^ ((r >> 2) & 1), y_ ^ ((r >> 1) & 1), c_ ^ (r & 1))

        def copy(r):
            px, py, pc = peer(r)
            return pltpu.make_async_remote_copy(
                src_ref=x_ref.at[4 * px + 2 * py + pc],
                dst_ref=out_ref.at[me],
                send_sem=send_sems.at[r - 1],
                recv_sem=recv_sems.at[r - 1],
                device_id=(px, py, pc),
                device_id_type=MESH,
            )

        mine = pltpu.make_async_copy(x_ref.at[me], out_ref.at[me], local_sem)
        mine.start()
        copies = [copy(r) for r in range(1, N_DEV)]
        for cp in copies:
            cp.start()
        for cp in copies:
            cp.wait_recv()
        for cp in copies:
            cp.wait_send()
        mine.wait()

    return pl.pallas_call(
        body,
        name=name,
        out_shape=jax.ShapeDtypeStruct(x.shape, x.dtype),
        in_specs=[pl.BlockSpec(memory_space=pl.ANY)],
        out_specs=pl.BlockSpec(memory_space=pl.ANY),
        scratch_shapes=[
            pltpu.SemaphoreType.DMA((7,)),
            pltpu.SemaphoreType.DMA((7,)),
            pltpu.SemaphoreType.DMA,
        ],
    )(x)


def _matmul(a, b, *, mode, out_dtype, name, tm=1024, tn=1024, tk=512, precision=None):
    if mode == "nn":
        (M, K), (K2, N) = a.shape, b.shape
    elif mode == "nt":
        (M, K), (N, K2) = a.shape, b.shape
    else:
        (K, M), (K2, N) = a.shape, b.shape
    assert K == K2, (a.shape, b.shape, mode)
    tm, tn, tk = _tile(M, tm, LANES if mode == "tn" else 16), _tile(N, tn), _tile(K, tk)
    nk = K // tk
    if mode == "tn":
        a_spec = pl.BlockSpec((tk, tm), lambda i, j, k: (k, i))
    else:
        a_spec = pl.BlockSpec((tm, tk), lambda i, j, k: (i, k))
    if mode == "nt":
        b_spec = pl.BlockSpec((tn, tk), lambda i, j, k: (j, k))
    else:
        b_spec = pl.BlockSpec((tk, tn), lambda i, j, k: (k, j))
    dn = {"nn": NN, "nt": NT, "tn": TN}[mode]

    def body(a_ref, b_ref, o_ref, acc_ref):
        k = pl.program_id(2)

        @pl.when(k == 0)
        def _():
            acc_ref[...] = jnp.zeros_like(acc_ref)

        acc_ref[...] += lax.dot_general(a_ref[...], b_ref[...], dn, preferred_element_type=F32,
                                        precision=precision)

        @pl.when(k == nk - 1)
        def _():
            o_ref[...] = acc_ref[...].astype(o_ref.dtype)

    return pl.pallas_call(
        body,
        name=name,
        grid=(M // tm, N // tn, nk),
        in_specs=[a_spec, b_spec],
        out_specs=pl.BlockSpec((tm, tn), lambda i, j, k: (i, j)),
        out_shape=jax.ShapeDtypeStruct((M, N), out_dtype),
        scratch_shapes=[pltpu.VMEM((tm, tn), F32)],
        compiler_params=_params("parallel", "parallel", "arbitrary"),
    )(a, b)


def _rstd(xf):
    return lax.rsqrt(jnp.mean(xf * xf, axis=-1, keepdims=True) + EPS)


def _col_view(width, off):
    assert off % width == 0
    return off // width


def _prenorm(x, g, sc, sh, *, name, off=0, width=None):
    S = x.shape[0]
    W = x.shape[1] if width is None else width
    cb = _col_view(W, off)
    tr = _tile(S, 512, 16)
    mod = sc is not None
    vec = pl.BlockSpec((1, W), lambda i: (0, 0))

    def body(*refs):
        if mod:
            x_ref, g_ref, sc_ref, sh_ref, o_ref = refs
        else:
            x_ref, g_ref, o_ref = refs
        xf = x_ref[...].astype(F32)
        y = xf * _rstd(xf) * g_ref[...]
        if mod:
            y = y * (1.0 + sc_ref[...]) + sh_ref[...]
        o_ref[...] = y.astype(o_ref.dtype)

    args = (x, g, sc, sh) if mod else (x, g)
    return pl.pallas_call(
        body,
        name=name,
        grid=(S // tr,),
        in_specs=[pl.BlockSpec((tr, W), lambda i: (i, cb))] + [vec] * (len(args) - 1),
        out_specs=pl.BlockSpec((tr, W), lambda i: (i, 0)),
        out_shape=jax.ShapeDtypeStruct((S, W), BF16),
        compiler_params=_params("parallel"),
    )(*args)


def _prenorm_bwd(x, dh, dres, g, sc, *, name, out_dtype, off=0, width=None):
    S = x.shape[0]
    W = x.shape[1] if width is None else width
    cb = _col_view(W, off)
    tr = _tile(S, 256, 16)
    mod = sc is not None
    res = dres is not None
    vec = pl.BlockSpec((1, W), lambda i: (0, 0))
    row = pl.BlockSpec((tr, W), lambda i: (i, 0))

    def body(*refs):
        it = iter(refs)
        x_ref, dh_ref = next(it), next(it)
        dres_ref = next(it) if res else None
        g_ref = next(it)
        sc_ref = next(it) if mod else None
        dx_ref, dg_ref = next(it), next(it)
        dsc_ref, dsh_ref = (next(it), next(it)) if mod else (None, None)
        i = pl.program_id(0)

        @pl.when(i == 0)
        def _():
            dg_ref[...] = jnp.zeros_like(dg_ref)
            if mod:
                dsc_ref[...] = jnp.zeros_like(dsc_ref)
                dsh_ref[...] = jnp.zeros_like(dsh_ref)

        xf = x_ref[...].astype(F32)
        r = _rstd(xf)
        xn = xf * r
        dhf = dh_ref[...].astype(F32)
        gv = g_ref[...]
        if mod:
            one_sc = 1.0 + sc_ref[...]
            dsh_ref[...] += jnp.sum(dhf, axis=0, keepdims=True)
            dsc_ref[...] += jnp.sum(dhf * (xn * gv), axis=0, keepdims=True)
            dg_ref[...] += jnp.sum(dhf * xn * one_sc, axis=0, keepdims=True)
            dxn = dhf * (gv * one_sc)
        else:
            dg_ref[...] += jnp.sum(dhf * xn, axis=0, keepdims=True)
            dxn = dhf * gv
        dx = r * (dxn - xn * jnp.mean(dxn * xn, axis=-1, keepdims=True))
        if res:
            dx = dx + dres_ref[...]
        dx_ref[...] = dx.astype(dx_ref.dtype)

    args = [x, dh] + ([dres] if res else []) + [g] + ([sc] if mod else [])
    in_specs = [pl.BlockSpec((tr, W), lambda i: (i, cb)), row] + ([row] if res else []) + [vec] + ([vec] if mod else [])
    n_vec = 3 if mod else 1
    outs = pl.pallas_call(
        body,
        name=name,
        grid=(S // tr,),
        in_specs=in_specs,
        out_specs=[row] + [vec] * n_vec,
        out_shape=[jax.ShapeDtypeStruct((S, W), out_dtype)] + [jax.ShapeDtypeStruct((1, W), F32)] * n_vec,
        compiler_params=_params("arbitrary"),
    )(*args)
    return outs


def _postnorm_res(x, y, gt, g, *, name):
    S, D = x.shape
    tr = _tile(S, 512, 8)
    row = pl.BlockSpec((tr, D), lambda i: (i, 0))
    vec = pl.BlockSpec((1, D), lambda i: (0, 0))

    def body(x_ref, y_ref, gt_ref, g_ref, o_ref):
        yf = y_ref[...]
        o_ref[...] = x_ref[...] + gt_ref[...] * (yf * _rstd(yf) * g_ref[...])

    return pl.pallas_call(
        body,
        name=name,
        grid=(S // tr,),
        in_specs=[row, row, vec, vec],
        out_specs=row,
        out_shape=jax.ShapeDtypeStruct((S, D), F32),
        compiler_params=_params("parallel"),
    )(x, y, gt, g)


def _postnorm_bwd(dx1, y, gt, g, *, name):
    S, D = y.shape
    tr = _tile(S, 256, 16)
    row = pl.BlockSpec((tr, D), lambda i: (i, 0))
    vec = pl.BlockSpec((1, D), lambda i: (0, 0))

    def body(dx_ref, y_ref, gt_ref, g_ref, dy_ref, dgt_ref, dg_ref):
        @pl.when(pl.program_id(0) == 0)
        def _():
            dgt_ref[...] = jnp.zeros_like(dgt_ref)
            dg_ref[...] = jnp.zeros_like(dg_ref)

        yf = y_ref[...]
        r = _rstd(yf)
        yn = yf * r
        d = dx_ref[...]
        gtv, gv = gt_ref[...], g_ref[...]
        dgt_ref[...] += jnp.sum(d * (yn * gv), axis=0, keepdims=True)
        dg_ref[...] += jnp.sum(d * gtv * yn, axis=0, keepdims=True)
        dyn = d * (gtv * gv)
        dy_ref[...] = (r * (dyn - yn * jnp.mean(dyn * yn, axis=-1, keepdims=True))).astype(dy_ref.dtype)

    return pl.pallas_call(
        body,
        name=name,
        grid=(S // tr,),
        in_specs=[row, row, vec, vec],
        out_specs=[row, vec, vec],
        out_shape=[jax.ShapeDtypeStruct((S, D), BF16), jax.ShapeDtypeStruct((1, D), F32),
                   jax.ShapeDtypeStruct((1, D), F32)],
        compiler_params=_params("arbitrary"),
    )(dx1, y, gt, g)


def _final_loss(x1, y, target, gt, g, *, name):
    S, D = y.shape
    tr = _tile(S, 256, 16)
    row = pl.BlockSpec((tr, D), lambda i: (i, 0))
    vec = pl.BlockSpec((1, D), lambda i: (0, 0))
    one = pl.BlockSpec((1, LANES), lambda i: (0, 0))

    def body(x_ref, y_ref, t_ref, gt_ref, g_ref, loss_ref, dout_ref, dy_ref, dgt_ref, dg_ref):
        @pl.when(pl.program_id(0) == 0)
        def _():
            loss_ref[...] = jnp.zeros_like(loss_ref)
            dgt_ref[...] = jnp.zeros_like(dgt_ref)
            dg_ref[...] = jnp.zeros_like(dg_ref)

        yf = y_ref[...]
        r = _rstd(yf)
        yn = yf * r
        gtv, gv = gt_ref[...], g_ref[...]
        out = x_ref[...] + gtv * (yn * gv)
        diff = out - t_ref[...]
        per_tok = jnp.mean(diff * diff, axis=-1, keepdims=True)
        loss_ref[...] += 0.5 * jnp.sum(per_tok, axis=0, keepdims=True)
        d = diff / D
        dout_ref[...] = d
        dgt_ref[...] += jnp.sum(d * (yn * gv), axis=0, keepdims=True)
        dg_ref[...] += jnp.sum(d * gtv * yn, axis=0, keepdims=True)
        dyn = d * (gtv * gv)
        dy_ref[...] = (r * (dyn - yn * jnp.mean(dyn * yn, axis=-1, keepdims=True))).astype(dy_ref.dtype)

    return pl.pallas_call(
        body,
        name=name,
        grid=(S // tr,),
        in_specs=[row, row, row, vec, vec],
        out_specs=[one, row, row, vec, vec],
        out_shape=[jax.ShapeDtypeStruct((1, LANES), F32), jax.ShapeDtypeStruct((S, D), F32),
                   jax.ShapeDtypeStruct((S, D), BF16), jax.ShapeDtypeStruct((1, D), F32),
                   jax.ShapeDtypeStruct((1, D), F32)],
        compiler_params=_params("arbitrary"),
    )(x1, y, target, gt, g)


def _ada_fwd(c_all, w_local, b_cols, *, name):
    B, D = c_all.shape
    N = w_local.shape[1]
    tn = _tile(N, 512)

    def body(c_ref, w_ref, b_ref, ca_ref, mod_ref):
        cv = c_ref[...]
        ca = cv * _sigmoid(cv)
        ca_ref[...] = ca
        mod_ref[...] = jnp.dot(ca, w_ref[...], preferred_element_type=F32, precision=HIGHEST) + b_ref[...]

    return pl.pallas_call(
        body,
        name=name,
        grid=(N // tn,),
        in_specs=[pl.BlockSpec((B, D), lambda j: (0, 0)), pl.BlockSpec((D, tn), lambda j: (0, j)),
                  pl.BlockSpec((1, tn), lambda j: (0, j))],
        out_specs=[pl.BlockSpec((B, D), lambda j: (0, 0)), pl.BlockSpec((B, tn), lambda j: (0, j))],
        out_shape=[jax.ShapeDtypeStruct((B, D), F32), jax.ShapeDtypeStruct((B, N), F32)],
        compiler_params=_params("arbitrary"),
    )(c_all, w_local, b_cols)


def _rope_tables(S, width, lane_off):
    pos = jnp.arange(S, dtype=F32)
    inv = ROPE_THETA ** (-jnp.arange(0, MLA_ROPE, 2, dtype=F32) / MLA_ROPE)
    ang = pos[:, None] * inv[None, :]
    ang = jnp.concatenate([ang, ang], axis=-1)
    cos, sin = jnp.cos(ang), jnp.sin(ang)
    first = (jnp.arange(MLA_ROPE) < ROPE_HALF)[None, :]
    sa = jnp.where(first, -sin, 0.0)
    sb = jnp.where(first, 0.0, sin)

    def place(t, fill):
        return jnp.pad(t, ((0, 0), (lane_off, width - lane_off - MLA_ROPE)), constant_values=fill)

    return place(cos, 1.0), place(sa, 0.0), place(sb, 0.0)


def _rope_apply(x, cos, sa, sb, width, transpose):
    if transpose:
        return x * cos + pltpu.roll(x * sa, ROPE_HALF, 1) + pltpu.roll(x * sb, width - ROPE_HALF, 1)
    return x * cos + pltpu.roll(x, width - ROPE_HALF, 1) * sa + pltpu.roll(x, ROPE_HALF, 1) * sb


def _rope(x, tables, *, heads, width, transpose, name, off=0):
    S = x.shape[0]
    cb = _col_view(width, off)
    tr = _tile(S, 512, 16)
    tab = pl.BlockSpec((tr, width), lambda i, h: (i, 0))

    def body(x_ref, c_ref, sa_ref, sb_ref, o_ref):
        o_ref[...] = _rope_apply(x_ref[...].astype(F32), c_ref[...], sa_ref[...], sb_ref[...], width,
                                 transpose).astype(o_ref.dtype)

    return pl.pallas_call(
        body,
        name=name,
        grid=(S // tr, heads),
        in_specs=[pl.BlockSpec((tr, width), lambda i, h: (i, cb + h)), tab, tab, tab],
        out_specs=pl.BlockSpec((tr, width), lambda i, h: (i, h)),
        out_shape=jax.ShapeDtypeStruct((S, heads * width), BF16),
        compiler_params=_params("parallel", "parallel"),
    )(x, *tables)


def _assemble_k(kv, krr, *, heads, name):
    S = kv.shape[0]
    tr = _tile(S, 512, 16)

    def body(kn_ref, kr_ref, o_ref):
        o_ref[:, :MLA_NOPE] = kn_ref[...]
        o_ref[:, MLA_NOPE:] = kr_ref[...]

    return pl.pallas_call(
        body,
        name=name,
        grid=(S // tr, heads),
        in_specs=[pl.BlockSpec((tr, MLA_NOPE), lambda i, h: (i, 2 * h)),
                  pl.BlockSpec((tr, LANES), lambda i, h: (i, 0))],
        out_specs=pl.BlockSpec((tr, MLA_QK_PAD), lambda i, h: (i, h)),
        out_shape=jax.ShapeDtypeStruct((S, heads * MLA_QK_PAD), BF16),
        compiler_params=_params("parallel", "parallel"),
    )(kv, krr)


def _assemble_k_bwd(dK, dV, tables, *, heads, name):
    S = dK.shape[0]
    tr = _tile(S, 512, 16)
    tab = pl.BlockSpec((tr, LANES), lambda i, h: (i, 0))

    def body(dk_ref, dv_ref, c_ref, sa_ref, sb_ref, dkv_ref, dkr_ref, acc_ref):
        h = pl.program_id(1)

        @pl.when(h == 0)
        def _():
            acc_ref[...] = jnp.zeros_like(acc_ref)

        dkv_ref[:, :MLA_NOPE] = dk_ref[:, :MLA_NOPE]
        dkv_ref[:, MLA_NOPE:] = dv_ref[...]
        acc_ref[...] += dk_ref[:, MLA_NOPE:].astype(F32)

        @pl.when(h == heads - 1)
        def _():
            dkr_ref[...] = _rope_apply(acc_ref[...], c_ref[...], sa_ref[...], sb_ref[...], LANES,
                                       True).astype(dkr_ref.dtype)

    return pl.pallas_call(
        body,
        name=name,
        grid=(S // tr, heads),
        in_specs=[pl.BlockSpec((tr, MLA_QK_PAD), lambda i, h: (i, h)),
                  pl.BlockSpec((tr, MLA_V), lambda i, h: (i, h)), tab, tab, tab],
        out_specs=[pl.BlockSpec((tr, MLA_QK_PAD), lambda i, h: (i, h)),
                   pl.BlockSpec((tr, LANES), lambda i, h: (i, 0))],
        out_shape=[jax.ShapeDtypeStruct((S, heads * MLA_QK_PAD), BF16), jax.ShapeDtypeStruct((S, LANES), BF16)],
        scratch_shapes=[pltpu.VMEM((tr, LANES), F32)],
        compiler_params=_params("parallel", "arbitrary"),
    )(dK, dV, *tables)


MLA_SCALE = MLA_QK ** -0.5


def _lane_tile(v, n):
    return v if n == LANES else jnp.tile(v, (1, n // LANES))


def _causal_mask(s):
    rows = lax.broadcasted_iota(jnp.int32, s.shape, 0)
    cols = lax.broadcasted_iota(jnp.int32, s.shape, 1)
    return jnp.where(cols <= rows, s, NEG)


def _flash_fwd(Q, K, KV, *, heads, name):
    S = Q.shape[0]
    t = _tile(S, 512)
    nb = S // t

    def body(q_ref, k_ref, v_ref, o_ref, lse_ref, m_scr, l_scr, acc_scr):
        qi, ki = pl.program_id(1), pl.program_id(2)

        @pl.when(ki == 0)
        def _():
            m_scr[...] = jnp.full_like(m_scr, NEG)
            l_scr[...] = jnp.zeros_like(l_scr)
            acc_scr[...] = jnp.zeros_like(acc_scr)

        def step(masked):
            s = lax.dot_general(q_ref[...], k_ref[...], NT, preferred_element_type=F32) * MLA_SCALE
            if masked:
                s = _causal_mask(s)
            m_prev = m_scr[...]
            m_new = jnp.maximum(m_prev, jnp.max(s, axis=1, keepdims=True))
            alpha = jnp.exp(m_prev - m_new)
            p = jnp.exp(s - _lane_tile(m_new, t))
            l_scr[...] = alpha * l_scr[...] + jnp.sum(p, axis=1, keepdims=True)
            acc_scr[...] = alpha * acc_scr[...] + jnp.dot(p.astype(BF16), v_ref[...], preferred_element_type=F32)
            m_scr[...] = m_new

        pl.when(ki < qi)(lambda: step(False))
        pl.when(ki == qi)(lambda: step(True))

        @pl.when(ki == nb - 1)
        def _():
            o_ref[...] = (acc_scr[...] / l_scr[...]).astype(o_ref.dtype)
            lse_ref[0] = m_scr[...] + jnp.log(l_scr[...])

    return pl.pallas_call(
        body,
        name=name,
        grid=(heads, nb, nb),
        in_specs=[pl.BlockSpec((t, MLA_QK_PAD), lambda h, qi, ki: (qi, h)),
                  pl.BlockSpec((t, MLA_QK_PAD), lambda h, qi, ki: (jnp.minimum(ki, qi), h)),
                  pl.BlockSpec((t, MLA_V), lambda h, qi, ki: (jnp.minimum(ki, qi), 2 * h + 1))],
        out_specs=[pl.BlockSpec((t, MLA_V), lambda h, qi, ki: (qi, h)),
                   pl.BlockSpec((1, t, LANES), lambda h, qi, ki: (h, qi, 0))],
        out_shape=[jax.ShapeDtypeStruct((S, heads * MLA_V), BF16),
                   jax.ShapeDtypeStruct((heads, S, LANES), F32)],
        scratch_shapes=[pltpu.VMEM((t, LANES), F32), pltpu.VMEM((t, LANES), F32), pltpu.VMEM((t, MLA_V), F32)],
        compiler_params=_params("parallel", "parallel", "arbitrary"),
    )(Q, K, KV)


def _flash_bwd_dq(Q, K, KV, dO, O, lse, *, heads, name):
    S = Q.shape[0]
    t = _tile(S, 512)
    nb = S // t

    def body(q_ref, k_ref, v_ref, do_ref, o_ref, lse_ref, dq_ref, delta_ref, acc_scr, delta_scr):
        qi, ki = pl.program_id(1), pl.program_id(2)

        @pl.when(ki == 0)
        def _():
            acc_scr[...] = jnp.zeros_like(acc_scr)
            d = jnp.sum(do_ref[...].astype(F32) * o_ref[...].astype(F32), axis=1, keepdims=True)
            delta_scr[...] = jnp.broadcast_to(d, delta_scr.shape)

        def step(masked):
            s = lax.dot_general(q_ref[...], k_ref[...], NT, preferred_element_type=F32) * MLA_SCALE
            if masked:
                s = _causal_mask(s)
            p = jnp.exp(s - _lane_tile(lse_ref[0], t))
            dp = lax.dot_general(do_ref[...], v_ref[...], NT, preferred_element_type=F32)
            ds = p * (dp - _lane_tile(delta_scr[...], t)) * MLA_SCALE
            acc_scr[...] += jnp.dot(ds.astype(BF16), k_ref[...], preferred_element_type=F32)

        pl.when(ki < qi)(lambda: step(False))
        pl.when(ki == qi)(lambda: step(True))

        @pl.when(ki == nb - 1)
        def _():
            dq_ref[...] = acc_scr[...].astype(dq_ref.dtype)
            delta_ref[0] = delta_scr[...]

    qrow = lambda h, qi, ki: (qi, h)
    return pl.pallas_call(
        body,
        name=name,
        grid=(heads, nb, nb),
        in_specs=[pl.BlockSpec((t, MLA_QK_PAD), qrow),
                  pl.BlockSpec((t, MLA_QK_PAD), lambda h, qi, ki: (jnp.minimum(ki, qi), h)),
                  pl.BlockSpec((t, MLA_V), lambda h, qi, ki: (jnp.minimum(ki, qi), 2 * h + 1)),
                  pl.BlockSpec((t, MLA_V), qrow),
                  pl.BlockSpec((t, MLA_V), qrow),
                  pl.BlockSpec((1, t, LANES), lambda h, qi, ki: (h, qi, 0))],
        out_specs=[pl.BlockSpec((t, MLA_QK_PAD), qrow),
                   pl.BlockSpec((1, t, LANES), lambda h, qi, ki: (h, qi, 0))],
        out_shape=[jax.ShapeDtypeStruct((S, heads * MLA_QK_PAD), BF16),
                   jax.ShapeDtypeStruct((heads, S, LANES), F32)],
        scratch_shapes=[pltpu.VMEM((t, MLA_QK_PAD), F32), pltpu.VMEM((t, LANES), F32)],
        compiler_params=_params("parallel", "parallel", "arbitrary"),
    )(Q, K, KV, dO, O, lse)


def _flash_bwd_dkv(Q, K, KV, dO, lse, delta, *, heads, name):
    S = Q.shape[0]
    t = _tile(S, 512)
    nb = S // t

    def body(q_ref, k_ref, v_ref, do_ref, lse_ref, delta_ref, dk_ref, dv_ref, dk_scr, dv_scr):
        ki, qi = pl.program_id(1), pl.program_id(2)

        @pl.when(qi == 0)
        def _():
            dk_scr[...] = jnp.zeros_like(dk_scr)
            dv_scr[...] = jnp.zeros_like(dv_scr)

        def step(masked):
            s = lax.dot_general(q_ref[...], k_ref[...], NT, preferred_element_type=F32) * MLA_SCALE
            if masked:
                s = _causal_mask(s)
            p = jnp.exp(s - _lane_tile(lse_ref[0], t))
            dv_scr[...] += lax.dot_general(p.astype(BF16), do_ref[...], TN, preferred_element_type=F32)
            dp = lax.dot_general(do_ref[...], v_ref[...], NT, preferred_element_type=F32)
            ds = p * (dp - _lane_tile(delta_ref[0], t)) * MLA_SCALE
            dk_scr[...] += lax.dot_general(ds.astype(BF16), q_ref[...], TN, preferred_element_type=F32)

        pl.when(qi > ki)(lambda: step(False))
        pl.when(qi == ki)(lambda: step(True))

        @pl.when(qi == nb - 1)
        def _():
            dk_ref[...] = dk_scr[...].astype(dk_ref.dtype)
            dv_ref[...] = dv_scr[...].astype(dv_ref.dtype)

    qrow = lambda h, ki, qi: (jnp.maximum(qi, ki), h)
    qstat = lambda h, ki, qi: (h, jnp.maximum(qi, ki), 0)
    return pl.pallas_call(
        body,
        name=name,
        grid=(heads, nb, nb),
        in_specs=[pl.BlockSpec((t, MLA_QK_PAD), qrow),
                  pl.BlockSpec((t, MLA_QK_PAD), lambda h, ki, qi: (ki, h)),
                  pl.BlockSpec((t, MLA_V), lambda h, ki, qi: (ki, 2 * h + 1)),
                  pl.BlockSpec((t, MLA_V), qrow),
                  pl.BlockSpec((1, t, LANES), qstat),
                  pl.BlockSpec((1, t, LANES), qstat)],
        out_specs=[pl.BlockSpec((t, MLA_QK_PAD), lambda h, ki, qi: (ki, h)),
                   pl.BlockSpec((t, MLA_V), lambda h, ki, qi: (ki, h))],
        out_shape=[jax.ShapeDtypeStruct((S, heads * MLA_QK_PAD), BF16),
                   jax.ShapeDtypeStruct((S, heads * MLA_V), BF16)],
        scratch_shapes=[pltpu.VMEM((t, MLA_QK_PAD), F32), pltpu.VMEM((t, MLA_V), F32)],
        compiler_params=_params("parallel", "parallel", "arbitrary"),
    )(Q, K, KV, dO, lse, delta)


SWA_SCALE = SWA_HD ** -0.5


def _t5_bucket_table():
    a = np.arange(BLOCK)[:, None]
    j = np.arange(2 * BLOCK)[None, :]
    dist = BLOCK + a - j
    max_exact = REL_BUCKETS // 2
    n = np.maximum(dist, 0)
    large = max_exact + (np.log(np.maximum(n, 1).astype(np.float32) / np.float32(max_exact))
                         / np.float32(math.log(REL_MAX_DIST / max_exact))
                         * np.float32(REL_BUCKETS - max_exact)).astype(np.int32)
    large = np.minimum(large, REL_BUCKETS - 1)
    bucket = np.where(n < max_exact, n, large)
    valid = (dist >= 0) & (dist < WINDOW)
    return bucket.astype(np.int32), valid


def _swa_scores(q_ref, kp_ref, kc_ref, bias_ref, qb, G):
    q2 = q_ref[...].reshape(G * BLOCK, SWA_HD)
    kb = jnp.concatenate([kp_ref[0], kc_ref[0]], axis=0)
    s = lax.dot_general(q2, kb, NT, preferred_element_type=F32) * SWA_SCALE
    s = s + bias_ref[...].reshape(G * BLOCK, 2 * BLOCK)
    cols = lax.broadcasted_iota(jnp.int32, s.shape, 1)
    s = jnp.where((cols >= BLOCK) | (qb > 0), s, NEG)
    return q2, kb, s


def _swa_probs(s, sink):
    m = jnp.maximum(jnp.max(s, axis=1, keepdims=True), sink)
    e = jnp.exp(s - m)
    es = jnp.exp(sink - m)
    den = jnp.sum(e, axis=1, keepdims=True) + es
    return e / den, es / den


def _swa_fwd(q, k, v, bias, sink, *, name):
    H, S, _ = q.shape
    G = H // SWA_KVH
    nb = S // BLOCK
    cur = lambda kh, qb: (kh, qb, 0)
    prev = lambda kh, qb: (kh, jnp.maximum(qb - 1, 0), 0)
    kvspec = lambda im: pl.BlockSpec((1, BLOCK, SWA_HD), im)

    def body(q_ref, kc_ref, kp_ref, vc_ref, vp_ref, bias_ref, sink_ref, o_ref):
        qb = pl.program_id(1)
        _, _, s = _swa_scores(q_ref, kp_ref, kc_ref, bias_ref, qb, G)
        p, _ = _swa_probs(s, sink_ref[0][:, :1])
        vb = jnp.concatenate([vp_ref[0], vc_ref[0]], axis=0)
        o = jnp.dot(p.astype(BF16), vb, preferred_element_type=F32)
        o_ref[...] = o.reshape(G, BLOCK, SWA_HD).astype(o_ref.dtype)

    return pl.pallas_call(
        body,
        name=name,
        grid=(SWA_KVH, nb),
        in_specs=[pl.BlockSpec((G, BLOCK, SWA_HD), cur), kvspec(cur), kvspec(prev), kvspec(cur), kvspec(prev),
                  pl.BlockSpec((G, BLOCK, 2 * BLOCK), lambda kh, qb: (kh, 0, 0)),
                  pl.BlockSpec((1, G * BLOCK, LANES), lambda kh, qb: (kh, 0, 0))],
        out_specs=pl.BlockSpec((G, BLOCK, SWA_HD), cur),
        out_shape=jax.ShapeDtypeStruct((H, S, SWA_HD), BF16),
        compiler_params=_params("parallel", "parallel"),
    )(q, k, k, v, v, bias, sink)


def _swa_bwd(q, k, v, bias, sink, do, *, name):
    H, S, _ = q.shape
    G = H // SWA_KVH
    nb = S // BLOCK
    cur = lambda kh, qb: (kh, jnp.minimum(qb, nb - 1), 0)
    prev = lambda kh, qb: (kh, jnp.maximum(jnp.minimum(qb, nb - 1) - 1, 0), 0)
    lag = lambda kh, qb: (kh, jnp.maximum(qb - 1, 0), 0)
    kvspec = lambda im: pl.BlockSpec((1, BLOCK, SWA_HD), im)

    def body(q_ref, kc_ref, kp_ref, vc_ref, vp_ref, bias_ref, sink_ref, do_ref,
             dq_ref, dk_ref, dv_ref, dbias_ref, dsink_ref, ck_scr, cv_scr):
        qb = pl.program_id(1)

        @pl.when(qb == 0)
        def _():
            dbias_ref[...] = jnp.zeros_like(dbias_ref)
            dsink_ref[...] = jnp.zeros_like(dsink_ref)
            ck_scr[...] = jnp.zeros_like(ck_scr)
            cv_scr[...] = jnp.zeros_like(cv_scr)

        @pl.when(qb < nb)
        def _():
            q2, kb, s = _swa_scores(q_ref, kp_ref, kc_ref, bias_ref, qb, G)
            p, ps = _swa_probs(s, sink_ref[0][:, :1])
            vb = jnp.concatenate([vp_ref[0], vc_ref[0]], axis=0)
            do2 = do_ref[...].reshape(G * BLOCK, SWA_HD)
            dp = lax.dot_general(do2, vb, NT, preferred_element_type=F32)
            delta = jnp.sum(dp * p, axis=1, keepdims=True)
            ds = p * (dp - delta)
            dbias_ref[...] += ds.reshape(G, BLOCK, 2 * BLOCK)
            dsk = jnp.sum((-ps * delta).reshape(G, BLOCK, 1), axis=1)
            dsink_ref[0] += jnp.broadcast_to(dsk, (G, LANES))
            dsb = (ds * SWA_SCALE).astype(BF16)
            dq_ref[...] = jnp.dot(dsb, kb, preferred_element_type=F32).reshape(G, BLOCK, SWA_HD).astype(dq_ref.dtype)
            dkb = lax.dot_general(dsb, q2, TN, preferred_element_type=F32)
            dvb = lax.dot_general(p.astype(BF16), do2, TN, preferred_element_type=F32)
            dk_ref[0] = (ck_scr[...] + dkb[:BLOCK]).astype(dk_ref.dtype)
            dv_ref[0] = (cv_scr[...] + dvb[:BLOCK]).astype(dv_ref.dtype)
            ck_scr[...] = dkb[BLOCK:]
            cv_scr[...] = dvb[BLOCK:]

        @pl.when(qb == nb)
        def _():
            dk_ref[0] = ck_scr[...].astype(dk_ref.dtype)
            dv_ref[0] = cv_scr[...].astype(dv_ref.dtype)

    return pl.pallas_call(
        body,
        name=name,
        grid=(SWA_KVH, nb + 1),
        in_specs=[pl.BlockSpec((G, BLOCK, SWA_HD), cur), kvspec(cur), kvspec(prev), kvspec(cur), kvspec(prev),
                  pl.BlockSpec((G, BLOCK, 2 * BLOCK), lambda kh, qb: (kh, 0, 0)),
                  pl.BlockSpec((1, G * BLOCK, LANES), lambda kh, qb: (kh, 0, 0)),
                  pl.BlockSpec((G, BLOCK, SWA_HD), cur)],
        out_specs=[pl.BlockSpec((G, BLOCK, SWA_HD), cur), kvspec(lag), kvspec(lag),
                   pl.BlockSpec((G, BLOCK, 2 * BLOCK), lambda kh, qb: (kh, 0, 0)),
                   pl.BlockSpec((1, G, LANES), lambda kh, qb: (kh, 0, 0))],
        out_shape=[jax.ShapeDtypeStruct((H, S, SWA_HD), BF16),
                   jax.ShapeDtypeStruct((SWA_KVH, S, SWA_HD), BF16),
                   jax.ShapeDtypeStruct((SWA_KVH, S, SWA_HD), BF16),
                   jax.ShapeDtypeStruct((H, BLOCK, 2 * BLOCK), F32),
                   jax.ShapeDtypeStruct((SWA_KVH, G, LANES), F32)],
        scratch_shapes=[pltpu.VMEM((BLOCK, SWA_HD), F32), pltpu.VMEM((BLOCK, SWA_HD), F32)],
        compiler_params=_params("parallel", "arbitrary"),
    )(q, k, k, v, v, bias, sink, do)


def _gate_mix(z, o_a, o_b, *, D, off_a, off_b, name):
    S = z.shape[0]
    tr = _tile(S, 256, 16)
    row = pl.BlockSpec((tr, D), lambda i: (i, 0))
    ca, cb = _col_view(D, off_a), _col_view(D, off_b)

    def body(ga_ref, gb_ref, oa_ref, ob_ref, m_ref):
        m = (_sigmoid(ga_ref[...].astype(F32)) * oa_ref[...].astype(F32)
             + _sigmoid(gb_ref[...].astype(F32)) * ob_ref[...].astype(F32))
        m_ref[...] = m.astype(m_ref.dtype)

    return pl.pallas_call(
        body,
        name=name,
        grid=(S // tr,),
        in_specs=[pl.BlockSpec((tr, D), lambda i: (i, ca)), pl.BlockSpec((tr, D), lambda i: (i, cb)), row, row],
        out_specs=row,
        out_shape=jax.ShapeDtypeStruct((S, D), BF16),
        compiler_params=_params("parallel"),
    )(z, z, o_a, o_b)


def _gate_mix_bwd(dm, z, o_a, o_b, *, D, off_a, off_b, name):
    S = z.shape[0]
    tr = _tile(S, 256, 16)
    row = pl.BlockSpec((tr, D), lambda i: (i, 0))
    ca, cb = _col_view(D, off_a), _col_view(D, off_b)

    def body(dm_ref, ga_ref, gb_ref, oa_ref, ob_ref, dga_ref, dgb_ref, doa_ref, dob_ref):
        d = dm_ref[...].astype(F32)
        for g_ref, o_ref, dg_ref, do_ref in ((ga_ref, oa_ref, dga_ref, doa_ref), (gb_ref, ob_ref, dgb_ref, dob_ref)):
            sg = _sigmoid(g_ref[...].astype(F32))
            dg_ref[...] = (d * o_ref[...].astype(F32) * (sg * (1.0 - sg))).astype(dg_ref.dtype)
            do_ref[...] = (d * sg).astype(do_ref.dtype)

    return pl.pallas_call(
        body,
        name=name,
        grid=(S // tr,),
        in_specs=[row, pl.BlockSpec((tr, D), lambda i: (i, ca)), pl.BlockSpec((tr, D), lambda i: (i, cb)), row, row],
        out_specs=[row] * 4,
        out_shape=[jax.ShapeDtypeStruct((S, D), BF16)] * 4,
        compiler_params=_params("parallel"),
    )(dm, z, z, o_a, o_b)


def _conv_taps(buf, cw_ref, cb_ref, rows):
    y = cb_ref[...] + cw_ref[0:1, :] * buf[pl.ds(HALO - 2, rows), :]
    y = y + cw_ref[1:2, :] * buf[pl.ds(HALO - 1, rows), :]
    return y + cw_ref[2:3, :] * buf[pl.ds(HALO, rows), :]


def _conv_gate(up, cw, cb, *, name):
    S, F2 = up.shape
    F = F2 // 2
    tr = _tile(S, 512, HALO)
    tc = _tile(F, 512)
    nc = F // tc
    hb = tr // HALO

    def halo_map(shift):
        return lambda i, j: (jnp.maximum(i * hb - 1, 0), j + shift)

    def body(x1_ref, h1_ref, x2_ref, h2_ref, cw1_ref, cw2_ref, cb1_ref, cb2_ref, a_ref, b1, b2):
        first = pl.program_id(0) == 0
        us = []
        for x_ref, h_ref, cw_ref, cb_ref, buf in ((x1_ref, h1_ref, cw1_ref, cb1_ref, b1),
                                                  (x2_ref, h2_ref, cw2_ref, cb2_ref, b2)):
            buf[0:HALO, :] = jnp.where(first, 0.0, h_ref[...].astype(F32))
            buf[HALO:, :] = x_ref[...].astype(F32)
            us.append(_conv_taps(buf, cw_ref, cb_ref, tr))
        u1, u2 = us
        a_ref[...] = (u1 * _sigmoid(u1) * u2).astype(a_ref.dtype)

    return pl.pallas_call(
        body,
        name=name,
        grid=(S // tr, nc),
        in_specs=[pl.BlockSpec((tr, tc), lambda i, j: (i, j)), pl.BlockSpec((HALO, tc), halo_map(0)),
                  pl.BlockSpec((tr, tc), lambda i, j: (i, j + nc)), pl.BlockSpec((HALO, tc), halo_map(nc)),
                  pl.BlockSpec((CONV_WIDTH, tc), lambda i, j: (0, j)),
                  pl.BlockSpec((CONV_WIDTH, tc), lambda i, j: (0, j + nc)),
                  pl.BlockSpec((1, tc), lambda i, j: (0, j)), pl.BlockSpec((1, tc), lambda i, j: (0, j + nc))],
        out_specs=pl.BlockSpec((tr, tc), lambda i, j: (i, j)),
        out_shape=jax.ShapeDtypeStruct((S, F), BF16),
        scratch_shapes=[pltpu.VMEM((tr + HALO, tc), F32), pltpu.VMEM((tr + HALO, tc), F32)],
        compiler_params=_params("parallel", "parallel"),
    )(up, up, up, up, cw, cw, cb, cb)


def _conv_gate_bwd(up, da, cw, cb, *, name):
    S, F2 = up.shape
    F = F2 // 2
    tr = _tile(S, 256, HALO)
    tc = _tile(F, 512)
    nc = F // tc
    hb = tr // HALO
    n_halo = S // HALO
    ni = S // tr
    ext = tr + 8

    def cur(shift):
        return lambda j, i: (i, j % nc + shift)

    def before(shift):
        return lambda j, i: (jnp.maximum(i * hb - 1, 0), j % nc + shift)

    def after(shift):
        return lambda j, i: (jnp.minimum((i + 1) * hb, n_halo - 1), j % nc + shift)

    def vec(rows, shift):
        return pl.BlockSpec((rows, tc), lambda j, i: (0, j % nc + shift))

    def body(x1_ref, p1_ref, n1_ref, x2_ref, p2_ref, n2_ref, da_ref, dan_ref,
             cw1_ref, cw2_ref, cb1_ref, cb2_ref, cwo_ref,
             dup_ref, dcw_ref, dcb_ref, b1, b2, bda, bdu):
        j, i = pl.program_id(0), pl.program_id(1)
        first, last = i == 0, i == ni - 1

        @pl.when(i == 0)
        def _():
            dcw_ref[...] = jnp.zeros_like(dcw_ref)
            dcb_ref[...] = jnp.zeros_like(dcb_ref)

        for x_ref, p_ref, n_ref, buf in ((x1_ref, p1_ref, n1_ref, b1), (x2_ref, p2_ref, n2_ref, b2)):
            buf[0:HALO, :] = jnp.where(first, 0.0, p_ref[...].astype(F32))
            buf[HALO:HALO + tr, :] = x_ref[...].astype(F32)
            buf[HALO + tr:, :] = n_ref[...].astype(F32)
        bda[0:tr, :] = da_ref[...].astype(F32)
        bda[tr:, :] = jnp.where(last, 0.0, dan_ref[...].astype(F32))
        dae = bda[pl.ds(0, ext), :]
        u1 = _conv_taps(b1, cw1_ref, cb1_ref, ext)
        sg = _sigmoid(u1)

        def finish(du, own):
            bdu[...] = du
            d0 = bdu[pl.ds(0, tr), :]
            dup = cwo_ref[2:3, :] * d0 + cwo_ref[1:2, :] * bdu[pl.ds(1, tr), :] + cwo_ref[0:1, :] * bdu[pl.ds(2, tr), :]
            dup_ref[...] = dup.astype(dup_ref.dtype)
            dcb_ref[...] += jnp.sum(d0, axis=0, keepdims=True)
            for tap in range(CONV_WIDTH):
                dcw_ref[tap:tap + 1, :] += jnp.sum(d0 * own[pl.ds(HALO - 2 + tap, tr), :], axis=0, keepdims=True)

        @pl.when(j < nc)
        def _():
            u2 = _conv_taps(b2, cw2_ref, cb2_ref, ext)
            finish(dae * u2 * (sg * (1.0 + u1 * (1.0 - sg))), b1)

        @pl.when(j >= nc)
        def _():
            finish(dae * (u1 * sg), b2)

    return pl.pallas_call(
        body,
        name=name,
        grid=(2 * nc, ni),
        in_specs=[pl.BlockSpec((tr, tc), cur(0)), pl.BlockSpec((HALO, tc), before(0)), pl.BlockSpec((HALO, tc), after(0)),
                  pl.BlockSpec((tr, tc), cur(nc)), pl.BlockSpec((HALO, tc), before(nc)), pl.BlockSpec((HALO, tc), after(nc)),
                  pl.BlockSpec((tr, tc), cur(0)), pl.BlockSpec((HALO, tc), after(0)),
                  vec(CONV_WIDTH, 0), vec(CONV_WIDTH, nc), vec(1, 0), vec(1, nc),
                  pl.BlockSpec((CONV_WIDTH, tc), lambda j, i: (0, j))],
        out_specs=[pl.BlockSpec((tr, tc), lambda j, i: (i, j)),
                   pl.BlockSpec((CONV_WIDTH, tc), lambda j, i: (0, j)),
                   pl.BlockSpec((1, tc), lambda j, i: (0, j))],
        out_shape=[jax.ShapeDtypeStruct((S, F2), BF16), jax.ShapeDtypeStruct((CONV_WIDTH, F2), F32),
                   jax.ShapeDtypeStruct((1, F2), F32)],
        scratch_shapes=[pltpu.VMEM((tr + 2 * HALO, tc), F32), pltpu.VMEM((tr + 2 * HALO, tc), F32),
                        pltpu.VMEM((tr + HALO, tc), F32), pltpu.VMEM((ext, tc), F32)],
        compiler_params=_params("parallel", "arbitrary"),
    )(up, up, up, up, up, up, da, da, cw, cw, cb, cb, cw)


def _adam_math(w, g, m, v):
    m = ADAM_B1 * m + (1.0 - ADAM_B1) * g
    v = ADAM_B2 * v + (1.0 - ADAM_B2) * (g * g)
    m_hat = m / (1.0 - ADAM_B1 ** ADAM_STEP)
    v_hat = v / (1.0 - ADAM_B2 ** ADAM_STEP)
    delta = -ADAM_LR * (m_hat / (jnp.sqrt(v_hat) + ADAM_EPS) + ADAM_WD * w)
    return delta, m, v


def _adamw(w, m, v, parts, *, name):
    R, C = w.shape
    tr = _tile(R, 256, 16)
    row = pl.BlockSpec((tr, C), lambda i: (i, 0))

    def body(w_ref, m_ref, v_ref, p_ref, g_ref, d_ref, m2_ref, v2_ref):
        g = p_ref[0].astype(F32)
        for k in range(1, N_DEV):
            g = g + p_ref[k].astype(F32)
        g_ref[...] = g
        d_ref[...], m2_ref[...], v2_ref[...] = _adam_math(w_ref[...], g, m_ref[...], v_ref[...])

    return pl.pallas_call(
        body,
        name=name,
        grid=(R // tr,),
        in_specs=[row, row, row, pl.BlockSpec((N_DEV, tr, C), lambda i: (0, i, 0))],
        out_specs=[row] * 4,
        out_shape=[jax.ShapeDtypeStruct((R, C), F32)] * 4,
        compiler_params=_params("parallel"),
    )(w, m, v, parts)


def _adamw_ada(w, m, v, cact_t, dmod_cols, *, name):
    R, C = w.shape
    B = cact_t.shape[1]
    tr = _tile(R, 256, 8)
    row = pl.BlockSpec((tr, C), lambda i: (i, 0))

    def body(w_ref, m_ref, v_ref, c_ref, d_ref, g_ref, dl_ref, m2_ref, v2_ref):
        g = c_ref[:, 0:1] * d_ref[0:1, :]
        for b in range(1, B):
            g = g + c_ref[:, b:b + 1] * d_ref[b:b + 1, :]
        g_ref[...] = g
        dl_ref[...], m2_ref[...], v2_ref[...] = _adam_math(w_ref[...], g, m_ref[...], v_ref[...])

    return pl.pallas_call(
        body,
        name=name,
        grid=(R // tr,),
        in_specs=[row, row, row, pl.BlockSpec((tr, B), lambda i: (i, 0)), pl.BlockSpec((B, C), lambda i: (0, 0))],
        out_specs=[row] * 4,
        out_shape=[jax.ShapeDtypeStruct((R, C), F32)] * 4,
        compiler_params=_params("parallel"),
    )(w, m, v, cact_t, dmod_cols)


def _z_layout(D, q_rank, kv_rank):
    kv = SWA_KVH * SWA_HD
    orig = {}
    o = 0
    for nm, w in (("cq", q_rank), ("ckv", kv_rank), ("kr", MLA_ROPE), ("qs", D), ("ks", kv), ("vs", kv),
                  ("ga", D), ("gb", D)):
        orig[nm] = (o, w)
        o += w
    blockw = {"cq": q_rank, "ckv": kv_rank, "kr": LANES, "qs": D, "ks": kv, "vs": kv, "ga": D, "gb": D}
    best = None
    for perm in itertools.permutations(("cq", "ckv", "ks", "vs", "kr")):
        off, new = 0, {}
        for nm in ("ga", "gb", "qs") + perm:
            off = _round_up(off, blockw[nm])
            new[nm] = off
            off += blockw[nm]
        if best is None or off < best[0]:
            best = (off, new)
    total = _round_up(best[0], 1024 if best[0] > 4096 else 512)
    return orig, best[1], blockw, total, o


def _permute_w_in(w, lay):
    orig, new, blockw, total, _ = lay
    parts, at = [], 0
    for nm in sorted(new, key=new.get):
        if new[nm] > at:
            parts.append(jnp.zeros((w.shape[0], new[nm] - at), w.dtype))
        o, wd = orig[nm]
        parts.append(w[:, o:o + wd])
        if blockw[nm] > wd:
            parts.append(jnp.zeros((w.shape[0], blockw[nm] - wd), w.dtype))
        at = new[nm] + blockw[nm]
    if total > at:
        parts.append(jnp.zeros((w.shape[0], total - at), w.dtype))
    return jnp.concatenate(parts, axis=1)


def _unpermute_w_in(wp, lay):
    orig, new, _, _, _ = lay
    return jnp.concatenate([wp[:, new[nm]:new[nm] + orig[nm][1]] for nm in sorted(orig, key=lambda n: orig[n][0])],
                           axis=1)


def _assemble_dz(parts, lay, S):
    _, new, blockw, total, _ = lay
    cols, at = [], 0
    for nm in sorted(new, key=new.get):
        if new[nm] > at:
            cols.append(jnp.zeros((S, new[nm] - at), BF16))
        cols.append(parts[nm])
        at = new[nm] + blockw[nm]
    if total > at:
        cols.append(jnp.zeros((S, total - at), BF16))
    return jnp.concatenate(cols, axis=1)


def _unshard_cols(g):
    return jnp.transpose(g, (1, 0, 2)).reshape(g.shape[1], N_DEV * g.shape[2])


def _shard_cols(w):
    K, N = w.shape
    return jnp.transpose(w.reshape(K, N_DEV, N // N_DEV), (1, 0, 2))


def _pack(vecs, rows):
    flat = jnp.concatenate([v.reshape(-1) for v in vecs])
    return jnp.pad(flat, (0, rows * LANES - flat.shape[0])).reshape(rows, LANES)


def kernel(x, c, w_ada, b_ada, g_pre_mix, g_post_mix, w_in, g_q_lat, w_uq, g_kv_lat, w_ukv, rel_bias, sinks, w_o, g_pre_ffn, g_post_ffn, w_up, conv_w, conv_b, w_down, loss_target, m_w_ada, m_b_ada, m_g_pre_mix, m_g_post_mix, m_w_in, m_g_q_lat, m_w_uq, m_g_kv_lat, m_w_ukv, m_rel_bias, m_sinks, m_w_o, m_g_pre_ffn, m_g_post_ffn, m_w_up, m_conv_w, m_conv_b, m_w_down, v_w_ada, v_b_ada, v_g_pre_mix, v_g_post_mix, v_w_in, v_g_q_lat, v_w_uq, v_g_kv_lat, v_w_ukv, v_rel_bias, v_sinks, v_w_o, v_g_pre_ffn, v_g_post_ffn, v_w_up, v_conv_w, v_conv_b, v_w_down):
    S, D = x.shape[1], x.shape[2]
    Q_RANK, KV_RANK = g_q_lat.shape[1], g_kv_lat.shape[1]
    H_MLA = D // MLA_V
    H_SWA = D // SWA_HD
    G_SWA = H_SWA // SWA_KVH
    F2 = w_up.shape[2] * N_DEV
    F = F2 // 2
    ada_n = w_ada.shape[2]
    me = 4 * lax.axis_index("x") + 2 * lax.axis_index("y") + lax.axis_index("c")
    lay = _z_layout(D, Q_RANK, KV_RANK)
    _, zoff, _, NZ, in_cols = lay
    assert in_cols == w_in.shape[2] * N_DEV

    x2, tgt = x[0], loss_target[0]

    cw_n = conv_w.shape[2]
    small = jnp.concatenate([jnp.pad(c, ((0, 7), (0, 0))), jnp.pad(conv_w[0], ((0, 8 - CONV_WIDTH), (0, 0)))], axis=1)
    small_all = _all_gather(small, name="ag_cond", in_vmem=True)
    c_all = small_all[:, 0, :D]
    cw_full = _unshard_cols(small_all[:, :CONV_WIDTH, D:])
    b_cols = lax.dynamic_slice_in_dim(b_ada, me * ada_n, ada_n, axis=1)
    c_act, mod_cols = _ada_fwd(c_all, w_ada[0], b_cols, name="ada_fwd")
    mod_all = _all_gather(mod_cols, name="ag_mod", in_vmem=True)
    mod_me = lax.dynamic_index_in_dim(mod_all, me, axis=1, keepdims=False).reshape(1, N_DEV * ada_n)
    sh1, sc1, gt1, sh2, sc2, gt2 = [mod_me[:, k * D:(k + 1) * D] for k in range(6)]

    w_in_p = _permute_w_in(_unshard_cols(_all_gather(w_in[0].astype(BF16), name="ag_w_in", in_vmem=False)), lay)
    uq = _unshard_cols(_all_gather(w_uq[0].astype(BF16), name="ag_w_uq", in_vmem=False))
    w_uq_p = jnp.pad(uq.reshape(Q_RANK, H_MLA, MLA_QK), ((0, 0), (0, 0), (0, MLA_QK_PAD - MLA_QK))
                     ).reshape(Q_RANK, H_MLA * MLA_QK_PAD)
    w_ukv_f = _unshard_cols(_all_gather(w_ukv[0].astype(BF16), name="ag_w_ukv", in_vmem=False))
    w_o_f = _all_gather(w_o[0].astype(BF16), name="ag_w_o", in_vmem=False).reshape(D, D)
    w_up_f = _unshard_cols(_all_gather(w_up[0].astype(BF16), name="ag_w_up", in_vmem=False))
    w_down_f = _all_gather(w_down[0].astype(BF16), name="ag_w_down", in_vmem=False).reshape(F, D)

    h1 = _prenorm(x2, g_pre_mix, sc1, sh1, name="prenorm_mix")
    z = _matmul(h1, w_in_p, mode="nn", out_dtype=BF16, name="mm_in")
    cqn = _prenorm(z, g_q_lat, None, None, name="norm_cq", off=zoff["cq"], width=Q_RANK)
    ckvn = _prenorm(z, g_kv_lat, None, None, name="norm_ckv", off=zoff["ckv"], width=KV_RANK)
    q_raw = _matmul(cqn, w_uq_p, mode="nn", out_dtype=BF16, name="mm_uq")
    kv = _matmul(ckvn, w_ukv_f, mode="nn", out_dtype=BF16, name="mm_ukv")
    tab_q = _rope_tables(S, MLA_QK_PAD, MLA_NOPE)
    tab_k = _rope_tables(S, LANES, 0)
    Qr = _rope(q_raw, tab_q, heads=H_MLA, width=MLA_QK_PAD, transpose=False, name="rope_q")
    krr = _rope(z, tab_k, heads=1, width=LANES, transpose=False, name="rope_k", off=zoff["kr"])
    Kc = _assemble_k(kv, krr, heads=H_MLA, name="assemble_k")
    o_a, lse = _flash_fwd(Qr, Kc, kv, heads=H_MLA, name="mla_fwd")

    bucket, valid = _t5_bucket_table()
    onehot = (jnp.asarray(bucket).reshape(-1, 1) == jnp.arange(LANES)[None, :]).astype(F32)
    rb_pad = jnp.pad(rel_bias, ((0, LANES - REL_BUCKETS), (0, LANES - H_SWA)))
    bias_t = _matmul(onehot, rb_pad, mode="nn", out_dtype=F32, name="bias_table", tm=2048, precision=HIGHEST)
    bias_full = jnp.transpose(bias_t[:, :H_SWA].reshape(BLOCK, 2 * BLOCK, H_SWA), (2, 0, 1))
    bias_full = jnp.where(jnp.asarray(valid)[None], bias_full, NEG)
    sink_rows = jnp.broadcast_to(sinks.reshape(SWA_KVH, G_SWA, 1, 1), (SWA_KVH, G_SWA, BLOCK, LANES)
                                 ).reshape(SWA_KVH, G_SWA * BLOCK, LANES)
    kvw = SWA_KVH * SWA_HD

    def heads_first(t, n):
        return jnp.transpose(t.reshape(S, n, SWA_HD), (1, 0, 2))

    def heads_last(t):
        return jnp.transpose(t, (1, 0, 2)).reshape(S, t.shape[0] * SWA_HD)

    qs_h = heads_first(z[:, zoff["qs"]:zoff["qs"] + D], H_SWA)
    ks_h = heads_first(z[:, zoff["ks"]:zoff["ks"] + kvw], SWA_KVH)
    vs_h = heads_first(z[:, zoff["vs"]:zoff["vs"] + kvw], SWA_KVH)
    o_b = heads_last(_swa_fwd(qs_h, ks_h, vs_h, bias_full, sink_rows, name="swa_fwd"))

    mixin = _gate_mix(z, o_a, o_b, D=D, off_a=zoff["ga"], off_b=zoff["gb"], name="gate_mix")
    mix = _matmul(mixin, w_o_f, mode="nn", out_dtype=F32, name="mm_o")
    x1 = _postnorm_res(x2, mix, gt1, g_post_mix, name="postnorm_mix")

    h2 = _prenorm(x1, g_pre_ffn, sc2, sh2, name="prenorm_ffn")
    up = _matmul(h2, w_up_f, mode="nn", out_dtype=BF16, name="mm_up")
    act = _conv_gate(up, cw_full, conv_b, name="conv_gate")
    y = _matmul(act, w_down_f, mode="nn", out_dtype=F32, name="mm_down")
    loss_part, dout, dy, dgt2, dg_post_ffn = _final_loss(x1, y, tgt, gt2, g_post_ffn, name="final_loss")
    loss = lax.psum(loss_part[0, 0], ("x", "y", "c"))

    dact = _matmul(dy, w_down_f, mode="nt", out_dtype=BF16, name="mm_down_dx")
    dw_down = _matmul(act, dy, mode="tn", out_dtype=BF16, name="mm_down_dw")
    dup, dcw, dcb = _conv_gate_bwd(up, dact, cw_full, conv_b, name="conv_gate_bwd")
    dh2 = _matmul(dup, w_up_f, mode="nt", out_dtype=F32, name="mm_up_dx")
    dw_up = _matmul(h2, dup, mode="tn", out_dtype=BF16, name="mm_up_dw")
    dx1, dg_pre_ffn, dsc2, dsh2 = _prenorm_bwd(x1, dh2, dout, g_pre_ffn, sc2, name="prenorm_ffn_bwd", out_dtype=F32)

    dmix, dgt1, dg_post_mix = _postnorm_bwd(dx1, mix, gt1, g_post_mix, name="postnorm_mix_bwd")
    dmixin = _matmul(dmix, w_o_f, mode="nt", out_dtype=BF16, name="mm_o_dx")
    dw_o = _matmul(mixin, dmix, mode="tn", out_dtype=BF16, name="mm_o_dw")
    dga, dgb, do_a, do_b = _gate_mix_bwd(dmixin, z, o_a, o_b, D=D, off_a=zoff["ga"], off_b=zoff["gb"],
                                         name="gate_mix_bwd")
    dqs_h, dks_h, dvs_h, dbias, dsink = _swa_bwd(qs_h, ks_h, vs_h, bias_full, sink_rows, heads_first(do_b, H_SWA),
                                                 name="swa_bwd")
    drel_t = _matmul(dbias.reshape(H_SWA, BLOCK * 2 * BLOCK), onehot, mode="nn", out_dtype=F32, name="bias_grad",
                     tk=4096, precision=HIGHEST)
    d_rel_bias = jnp.transpose(drel_t[:, :REL_BUCKETS])
    d_sinks = dsink[:, :, 0].reshape(1, H_SWA)

    dQ, delta = _flash_bwd_dq(Qr, Kc, kv, do_a, o_a, lse, heads=H_MLA, name="mla_bwd_dq")
    dK, dV = _flash_bwd_dkv(Qr, Kc, kv, do_a, lse, delta, heads=H_MLA, name="mla_bwd_dkv")
    dq_raw = _rope(dQ, tab_q, heads=H_MLA, width=MLA_QK_PAD, transpose=True, name="rope_q_bwd")
    dkv, dkr = _assemble_k_bwd(dK, dV, tab_k, heads=H_MLA, name="assemble_k_bwd")
    dcqn = _matmul(dq_raw, w_uq_p, mode="nt", out_dtype=F32, name="mm_uq_dx")
    dw_uq_p = _matmul(cqn, dq_raw, mode="tn", out_dtype=BF16, name="mm_uq_dw")
    dckvn = _matmul(dkv, w_ukv_f, mode="nt", out_dtype=F32, name="mm_ukv_dx")
    dw_ukv = _matmul(ckvn, dkv, mode="tn", out_dtype=BF16, name="mm_ukv_dw")
    dcq, dg_q = _prenorm_bwd(z, dcqn, None, g_q_lat, None, name="norm_cq_bwd", out_dtype=BF16,
                             off=zoff["cq"], width=Q_RANK)
    dckv, dg_kv = _prenorm_bwd(z, dckvn, None, g_kv_lat, None, name="norm_ckv_bwd", out_dtype=BF16,
                               off=zoff["ckv"], width=KV_RANK)
    dz = _assemble_dz({"ga": dga, "gb": dgb, "qs": heads_last(dqs_h), "cq": dcq, "ckv": dckv,
                       "ks": heads_last(dks_h), "vs": heads_last(dvs_h), "kr": dkr}, lay, S)
    dh1 = _matmul(dz, w_in_p, mode="nt", out_dtype=F32, name="mm_in_dx")
    dw_in_p = _matmul(h1, dz, mode="tn", out_dtype=BF16, name="mm_in_dw")
    grad_x, dg_pre_mix, dsc1, dsh1 = _prenorm_bwd(x2, dh1, dx1, g_pre_mix, sc1, name="prenorm_mix_bwd",
                                                  out_dtype=F32)
    dmod = jnp.concatenate([dsh1, dsc1, dgt1, dsh2, dsc2, dgt2], axis=1)

    small_names = ["b_ada", "g_pre_mix", "g_post_mix", "g_q_lat", "g_kv_lat", "rel_bias", "sinks", "g_pre_ffn",
                   "g_post_ffn", "conv_b"]
    small_w = [b_ada, g_pre_mix, g_post_mix, g_q_lat, g_kv_lat, rel_bias, sinks, g_pre_ffn, g_post_ffn, conv_b]
    small_m = [m_b_ada, m_g_pre_mix, m_g_post_mix, m_g_q_lat, m_g_kv_lat, m_rel_bias, m_sinks, m_g_pre_ffn,
               m_g_post_ffn, m_conv_b]
    small_v = [v_b_ada, v_g_pre_mix, v_g_post_mix, v_g_q_lat, v_g_kv_lat, v_rel_bias, v_sinks, v_g_pre_ffn,
               v_g_post_ffn, v_conv_b]
    small_g = [dmod, dg_pre_mix, dg_post_mix, dg_q, dg_kv, d_rel_bias, d_sinks, dg_pre_ffn, dg_post_ffn, dcb]
    n_small = sum(int(np.prod(w.shape)) for w in small_w)
    rows = _round_up(-(-n_small // LANES), 16)
    parts_small = _all_gather(_pack(small_g, rows), name="ag_small_grads", in_vmem=True)
    sg, sd, sm, sv = _adamw(_pack(small_w, rows), _pack(small_m, rows), _pack(small_v, rows), parts_small,
                            name="adamw_small")

    def unpack(packed):
        flat, out, at = packed.reshape(-1), {}, 0
        for nm, w in zip(small_names, small_w):
            n = int(np.prod(w.shape))
            out[nm] = flat[at:at + n].reshape(w.shape)
            at += n
        return out

    small_out = [unpack(t) for t in (sg, sd, sm, sv)]

    dmod_all = parts_small.reshape(N_DEV, rows * LANES)[:, :6 * D]
    dmod_cols = lax.dynamic_slice_in_dim(dmod_all, me * ada_n, ada_n, axis=1)
    ada_out = _adamw_ada(w_ada[0], m_w_ada[0], v_w_ada[0], jnp.transpose(c_act), dmod_cols, name="adamw_w_ada")

    def owner_update(parts, w, m, v, name):
        got = _all_to_all(parts, name="a2a_" + name)
        shp = w.shape
        w2, m2, v2 = (t.reshape(shp[-2], shp[-1]) for t in (w, m, v))
        return [t.reshape(shp) for t in _adamw(w2, m2, v2, got, name="adamw_" + name)]

    dw_in = _unpermute_w_in(dw_in_p, lay)
    dw_uq = dw_uq_p.reshape(Q_RANK, H_MLA, MLA_QK_PAD)[:, :, :MLA_QK].reshape(Q_RANK, H_MLA * MLA_QK)
    dcw_parts = jnp.pad(_shard_cols(dcw), ((0, 0), (0, 16 - CONV_WIDTH), (0, 0)))

    def pad_rows(t):
        return jnp.pad(t[0], ((0, 16 - CONV_WIDTH), (0, 0)))

    big = {
        "w_in": owner_update(_shard_cols(dw_in), w_in, m_w_in, v_w_in, "w_in"),
        "w_uq": owner_update(_shard_cols(dw_uq), w_uq, m_w_uq, v_w_uq, "w_uq"),
        "w_ukv": owner_update(_shard_cols(dw_ukv), w_ukv, m_w_ukv, v_w_ukv, "w_ukv"),
        "w_o": owner_update(dw_o.reshape(N_DEV, D // N_DEV, D), w_o, m_w_o, v_w_o, "w_o"),
        "w_up": owner_update(_shard_cols(dw_up), w_up, m_w_up, v_w_up, "w_up"),
        "w_down": owner_update(dw_down.reshape(N_DEV, F // N_DEV, D), w_down, m_w_down, v_w_down, "w_down"),
    }
    cw_upd = _adamw(pad_rows(conv_w), pad_rows(m_conv_w), pad_rows(v_conv_w),
                    _all_to_all(dcw_parts, name="a2a_conv_w"), name="adamw_conv_w")
    big["conv_w"] = [t[:CONV_WIDTH].reshape(conv_w.shape) for t in cw_upd]
    big["w_ada"] = [t.reshape(w_ada.shape) for t in ada_out]

    order = ["w_ada", "b_ada", "g_pre_mix", "g_post_mix", "w_in", "g_q_lat", "w_uq", "g_kv_lat", "w_ukv", "rel_bias",
             "sinks", "w_o", "g_pre_ffn", "g_post_ffn", "w_up", "conv_w", "conv_b", "w_down"]
    outs = [loss, grad_x.reshape(x.shape)]
    for kind in range(4):
        for nm in order:
            outs.append(big[nm][kind] if nm in big else small_out[kind][nm])
    return tuple(outs)
```

```python
import functools
import itertools
import math

import numpy as np

import jax
import jax.numpy as jnp
from jax import lax
from jax.experimental import pallas as pl
from jax.experimental.pallas import tpu as pltpu

F32 = jnp.float32
BF16 = jnp.bfloat16

N_DEV = 8
MLA_NOPE = 128
MLA_ROPE = 64
MLA_V = 128
MLA_QK = MLA_NOPE + MLA_ROPE
MLA_QK_PAD = 256
ROPE_HALF = MLA_ROPE // 2
ROPE_THETA = 10000.0
SWA_HD = 64
SWA_KVH = 4
WINDOW = 128
BLOCK = 128
REL_BUCKETS = 32
REL_MAX_DIST = 128
CONV_WIDTH = 3
EPS = 1e-6
NEG = -1e30
ADAM_LR = 0.001
ADAM_B1 = 0.9
ADAM_B2 = 0.999
ADAM_EPS = 1e-08
ADAM_WD = 0.01
ADAM_STEP = 10
LANES = 128
HALO = 16
MESH = pl.DeviceIdType.MESH
HIGHEST = lax.Precision.HIGHEST

NN = (((1,), (0,)), ((), ()))
NT = (((1,), (1,)), ((), ()))
TN = (((0,), (0,)), ((), ()))


def _tile(n, pref, align=LANES):
    if n <= pref:
        return n
    t = (pref // align) * align
    while t >= align:
        if n % t == 0:
            return t
        t -= align
    return n


def _round_up(n, m):
    return (n + m - 1) // m * m


def _params(*sem):
    return pltpu.CompilerParams(dimension_semantics=sem)


def _sigmoid(x):
    return 1.0 / (1.0 + jnp.exp(-x))


def _my_place():
    return lax.axis_index("x"), lax.axis_index("y"), lax.axis_index("c")


def _all_gather(x, *, name, in_vmem):
    space = pltpu.VMEM if in_vmem else pl.ANY

    def body(x_ref, out_ref, send_sems, recv_sems, local_sem):
        x_, y_, c_ = _my_place()
        me, sibling = (x_, y_, c_), (x_, y_, 1 - c_)
        chips = [(1 - x_, y_), (x_, 1 - y_), (1 - x_, 1 - y_)]

        def slot(px, py, pc):
            return out_ref.at[4 * px + 2 * py + pc]

        def copy(k, block, to, src=None):
            return pltpu.make_async_remote_copy(
                src_ref=slot(*block) if src is None else src,
                dst_ref=slot(*block),
                send_sem=send_sems.at[k],
                recv_sem=recv_sems.at[k],
                device_id=to,
                device_id_type=MESH,
            )

        mine = pltpu.make_async_copy(x_ref, slot(*me), local_sem)
        mine.start()
        first = [copy(0, me, sibling, src=x_ref)]
        first += [copy(1 + j, me, (*chip, c_), src=x_ref) for j, chip in enumerate(chips)]
        for cp in first:
            cp.start()
        passed = [copy(4 + j, (*chip, c_), sibling) for j, chip in enumerate(chips)]
        for j, chip in enumerate(chips):
            copy(1 + j, (*chip, c_), me).wait_recv()
            passed[j].start()
        copy(0, sibling, me).wait_recv()
        for j, chip in enumerate(chips):
            copy(4 + j, (*chip, 1 - c_), me).wait_recv()
        for cp in first + passed:
            cp.wait_send()
        mine.wait()

    return pl.pallas_call(
        body,
        name=name,
        out_shape=jax.ShapeDtypeStruct((N_DEV,) + x.shape, x.dtype),
        in_specs=[pl.BlockSpec(memory_space=space)],
        out_specs=pl.BlockSpec(memory_space=space),
        scratch_shapes=[
            pltpu.SemaphoreType.DMA((7,)),
            pltpu.SemaphoreType.DMA((7,)),
            pltpu.SemaphoreType.DMA,
        ],
    )(x)


def _carried_copies(kind, x_ref, out_ref, send_sems, recv_sems, local_sems, t):
    x_, y_, c_ = _my_place()
    me = 4 * x_ + 2 * y_ + c_
    remote = []
    for r in range(1, N_DEV):
        px, py, pc = x_ ^ ((r >> 2) & 1), y_ ^ ((r >> 1) & 1), c_ ^ (r & 1)
        src = x_ref if kind == "gather" else x_ref.at[4 * px + 2 * py + pc]
        remote.append(pltpu.make_async_remote_copy(
            src_ref=src, dst_ref=out_ref.at[me],
            send_sem=send_sems.at[7 * t + r - 1], recv_sem=recv_sems.at[7 * t + r - 1],
            device_id=(px, py, pc), device_id_type=MESH))
    own = x_ref if kind == "gather" else x_ref.at[me]
    return remote, pltpu.make_async_copy(own, out_ref.at[me], local_sems.at[t])


def _call(body, *, name, grid, in_specs, out_specs, out_shape, args, scratch_shapes=(), sem=(), comm=()):
    if not comm:
        outs = pl.pallas_call(body, name=name, grid=grid, in_specs=list(in_specs), out_specs=list(out_specs),
                              out_shape=list(out_shape), scratch_shapes=list(scratch_shapes),
                              compiler_params=_params(*sem))(*args)
        return list(outs), []
    n_in, n_out, n_c, n_s = len(in_specs), len(out_specs), len(comm), len(scratch_shapes)
    kinds = [kind for kind, _ in comm]
    hbm = pl.BlockSpec(memory_space=pl.ANY)

    def wrapped(*refs):
        ins, cin = refs[:n_in], refs[n_in:n_in + n_c]
        at = n_in + n_c
        outs, cout = refs[at:at + n_out], refs[at + n_out:at + n_out + n_c]
        scr = refs[at + n_out + n_c:at + n_out + n_c + n_s]
        send, recv, local = refs[-3:]
        ids = [pl.program_id(a) for a in range(len(grid))]
        first = functools.reduce(jnp.logical_and, [i == 0 for i in ids])
        last = functools.reduce(jnp.logical_and, [i == g - 1 for i, g in zip(ids, grid)])

        def copies():
            return [_carried_copies(kinds[t], cin[t], cout[t], send, recv, local, t) for t in range(n_c)]

        @pl.when(first)
        def _():
            for remote, own in copies():
                own.start()
                for cp in remote:
                    cp.start()

        body(*ins, *outs, *scr)

        @pl.when(last)
        def _():
            for remote, own in copies():
                for cp in remote:
                    cp.wait_recv()
                for cp in remote:
                    cp.wait_send()
                own.wait()

    c_shapes = [jax.ShapeDtypeStruct(((N_DEV,) + a.shape) if kind == "gather" else a.shape, a.dtype)
                for kind, a in comm]
    res = pl.pallas_call(
        wrapped, name=name, grid=grid,
        in_specs=list(in_specs) + [hbm] * n_c, out_specs=list(out_specs) + [hbm] * n_c,
        out_shape=list(out_shape) + c_shapes,
        scratch_shapes=list(scratch_shapes) + [pltpu.SemaphoreType.DMA((7 * n_c,)), pltpu.SemaphoreType.DMA((7 * n_c,)),
                                               pltpu.SemaphoreType.DMA((n_c,))],
        compiler_params=_params(*(["arbitrary"] * len(grid))),
    )(*args, *[a for _, a in comm])
    return list(res[:n_out]), list(res[n_out:])


def _matmul(a, b, *, mode, out_dtype, name, tm=1024, tn=1024, tk=2816, precision=None, comm=()):
    if mode == "nn":
        (M, K), (K2, N) = a.shape, b.shape
    elif mode == "nt":
        (M, K), (N, K2) = a.shape, b.shape
    else:
        (K, M), (K2, N) = a.shape, b.shape
    assert K == K2, (a.shape, b.shape, mode)
    tm, tn, tk = _tile(M, tm, LANES if mode == "tn" else 16), _tile(N, tn), _tile(K, tk)
    nk = K // tk
    if mode == "tn":
        a_spec = pl.BlockSpec((tk, tm), lambda i, j, k: (k, i))
    else:
        a_spec = pl.BlockSpec((tm, tk), lambda i, j, k: (i, k))
    if mode == "nt":
        b_spec = pl.BlockSpec((tn, tk), lambda i, j, k: (j, k))
    else:
        b_spec = pl.BlockSpec((tk, tn), lambda i, j, k: (k, j))
    dn = {"nn": NN, "nt": NT, "tn": TN}[mode]

    def product(a_ref, b_ref):
        return lax.dot_general(a_ref[...], b_ref[...], dn, preferred_element_type=F32, precision=precision)

    def body_one(a_ref, b_ref, o_ref):
        o_ref[...] = product(a_ref, b_ref).astype(o_ref.dtype)

    def body_acc(a_ref, b_ref, o_ref, acc_ref):
        k = pl.program_id(2)

        @pl.when(k == 0)
        def _():
            acc_ref[...] = product(a_ref, b_ref)

        @pl.when(k > 0)
        def _():
            acc_ref[...] += product(a_ref, b_ref)

        @pl.when(k == nk - 1)
        def _():
            o_ref[...] = acc_ref[...].astype(o_ref.dtype)

    outs, moved = _call(
        body_one if nk == 1 else body_acc,
        name=name,
        grid=(M // tm, N // tn, nk),
        in_specs=[a_spec, b_spec],
        out_specs=[pl.BlockSpec((tm, tn), lambda i, j, k: (i, j))],
        out_shape=[jax.ShapeDtypeStruct((M, N), out_dtype)],
        scratch_shapes=[] if nk == 1 else [pltpu.VMEM((tm, tn), F32)],
        sem=("parallel", "parallel", "arbitrary"),
        args=(a, b),
        comm=comm,
    )
    return (outs[0], moved) if comm else outs[0]


def _rstd(xf):
    return lax.rsqrt(jnp.mean(xf * xf, axis=-1, keepdims=True) + EPS)


def _col_view(width, off):
    assert off % width == 0
    return off // width


def _prenorm(x, g, sc, sh, *, name, off=0, width=None):
    S = x.shape[0]
    W = x.shape[1] if width is None else width
    cb = _col_view(W, off)
    tr = _tile(S, 512, 16)
    mod = sc is not None
    vec = pl.BlockSpec((1, W), lambda i: (0, 0))

    def body(*refs):
        if mod:
            x_ref, g_ref, sc_ref, sh_ref, o_ref = refs
        else:
            x_ref, g_ref, o_ref = refs
        xf = x_ref[...].astype(F32)
        y = xf * _rstd(xf) * g_ref[...]
        if mod:
            y = y * (1.0 + sc_ref[...]) + sh_ref[...]
        o_ref[...] = y.astype(o_ref.dtype)

    args = (x, g, sc, sh) if mod else (x, g)
    return pl.pallas_call(
        body,
        name=name,
        grid=(S // tr,),
        in_specs=[pl.BlockSpec((tr, W), lambda i: (i, cb))] + [vec] * (len(args) - 1),
        out_specs=pl.BlockSpec((tr, W), lambda i: (i, 0)),
        out_shape=jax.ShapeDtypeStruct((S, W), BF16),
        compiler_params=_params("parallel"),
    )(*args)


def _prenorm_bwd(x, dh, dres, g, sc, *, name, out_dtype, off=0, width=None):
    S = x.shape[0]
    W = x.shape[1] if width is None else width
    cb = _col_view(W, off)
    tr = _tile(S, 256, 16)
    mod = sc is not None
    res = dres is not None
    vec = pl.BlockSpec((1, W), lambda i: (0, 0))
    row = pl.BlockSpec((tr, W), lambda i: (i, 0))

    def body(*refs):
        it = iter(refs)
        x_ref, dh_ref = next(it), next(it)
        dres_ref = next(it) if res else None
        g_ref = next(it)
        sc_ref = next(it) if mod else None
        dx_ref, dg_ref = next(it), next(it)
        dsc_ref, dsh_ref = (next(it), next(it)) if mod else (None, None)
        i = pl.program_id(0)

        @pl.when(i == 0)
        def _():
            dg_ref[...] = jnp.zeros_like(dg_ref)
            if mod:
                dsc_ref[...] = jnp.zeros_like(dsc_ref)
                dsh_ref[...] = jnp.zeros_like(dsh_ref)

        xf = x_ref[...].astype(F32)
        r = _rstd(xf)
        xn = xf * r
        dhf = dh_ref[...].astype(F32)
        gv = g_ref[...]
        if mod:
            one_sc = 1.0 + sc_ref[...]
            dsh_ref[...] += jnp.sum(dhf, axis=0, keepdims=True)
            dsc_ref[...] += jnp.sum(dhf * (xn * gv), axis=0, keepdims=True)
            dg_ref[...] += jnp.sum(dhf * xn * one_sc, axis=0, keepdims=True)
            dxn = dhf * (gv * one_sc)
        else:
            dg_ref[...] += jnp.sum(dhf * xn, axis=0, keepdims=True)
            dxn = dhf * gv
        dx = r * (dxn - xn * jnp.mean(dxn * xn, axis=-1, keepdims=True))
        if res:
            dx = dx + dres_ref[...]
        dx_ref[...] = dx.astype(dx_ref.dtype)

    args = [x, dh] + ([dres] if res else []) + [g] + ([sc] if mod else [])
    in_specs = [pl.BlockSpec((tr, W), lambda i: (i, cb)), row] + ([row] if res else []) + [vec] + ([vec] if mod else [])
    n_vec = 3 if mod else 1
    outs = pl.pallas_call(
        body,
        name=name,
        grid=(S // tr,),
        in_specs=in_specs,
        out_specs=[row] + [vec] * n_vec,
        out_shape=[jax.ShapeDtypeStruct((S, W), out_dtype)] + [jax.ShapeDtypeStruct((1, W), F32)] * n_vec,
        compiler_params=_params("arbitrary"),
    )(*args)
    return outs


def _postnorm_res(x, y, gt, g, *, name):
    S, D = x.shape
    tr = _tile(S, 512, 8)
    row = pl.BlockSpec((tr, D), lambda i: (i, 0))
    vec = pl.BlockSpec((1, D), lambda i: (0, 0))

    def body(x_ref, y_ref, gt_ref, g_ref, o_ref):
        yf = y_ref[...]
        o_ref[...] = x_ref[...] + gt_ref[...] * (yf * _rstd(yf) * g_ref[...])

    return pl.pallas_call(
        body,
        name=name,
        grid=(S // tr,),
        in_specs=[row, row, vec, vec],
        out_specs=row,
        out_shape=jax.ShapeDtypeStruct((S, D), F32),
        compiler_params=_params("parallel"),
    )(x, y, gt, g)


def _postnorm_bwd(dx1, y, gt, g, *, name):
    S, D = y.shape
    tr = _tile(S, 256, 16)
    row = pl.BlockSpec((tr, D), lambda i: (i, 0))
    vec = pl.BlockSpec((1, D), lambda i: (0, 0))

    def body(dx_ref, y_ref, gt_ref, g_ref, dy_ref, dgt_ref, dg_ref):
        @pl.when(pl.program_id(0) == 0)
        def _():
            dgt_ref[...] = jnp.zeros_like(dgt_ref)
            dg_ref[...] = jnp.zeros_like(dg_ref)

        yf = y_ref[...]
        r = _rstd(yf)
        yn = yf * r
        d = dx_ref[...]
        gtv, gv = gt_ref[...], g_ref[...]
        dgt_ref[...] += jnp.sum(d * (yn * gv), axis=0, keepdims=True)
        dg_ref[...] += jnp.sum(d * gtv * yn, axis=0, keepdims=True)
        dyn = d * (gtv * gv)
        dy_ref[...] = (r * (dyn - yn * jnp.mean(dyn * yn, axis=-1, keepdims=True))).astype(dy_ref.dtype)

    return pl.pallas_call(
        body,
        name=name,
        grid=(S // tr,),
        in_specs=[row, row, vec, vec],
        out_specs=[row, vec, vec],
        out_shape=[jax.ShapeDtypeStruct((S, D), BF16), jax.ShapeDtypeStruct((1, D), F32),
                   jax.ShapeDtypeStruct((1, D), F32)],
        compiler_params=_params("arbitrary"),
    )(dx1, y, gt, g)


def _final_loss(x1, y, target, gt, g, *, name):
    S, D = y.shape
    tr = _tile(S, 256, 16)
    row = pl.BlockSpec((tr, D), lambda i: (i, 0))
    vec = pl.BlockSpec((1, D), lambda i: (0, 0))
    one = pl.BlockSpec((1, LANES), lambda i: (0, 0))

    def body(x_ref, y_ref, t_ref, gt_ref, g_ref, loss_ref, dout_ref, dy_ref, dgt_ref, dg_ref):
        @pl.when(pl.program_id(0) == 0)
        def _():
            loss_ref[...] = jnp.zeros_like(loss_ref)
            dgt_ref[...] = jnp.zeros_like(dgt_ref)
            dg_ref[...] = jnp.zeros_like(dg_ref)

        yf = y_ref[...]
        r = _rstd(yf)
        yn = yf * r
        gtv, gv = gt_ref[...], g_ref[...]
        out = x_ref[...] + gtv * (yn * gv)
        diff = out - t_ref[...]
        per_tok = jnp.mean(diff * diff, axis=-1, keepdims=True)
        loss_ref[...] += 0.5 * jnp.sum(per_tok, axis=0, keepdims=True)
        d = diff / D
        dout_ref[...] = d
        dgt_ref[...] += jnp.sum(d * (yn * gv), axis=0, keepdims=True)
        dg_ref[...] += jnp.sum(d * gtv * yn, axis=0, keepdims=True)
        dyn = d * (gtv * gv)
        dy_ref[...] = (r * (dyn - yn * jnp.mean(dyn * yn, axis=-1, keepdims=True))).astype(dy_ref.dtype)

    return pl.pallas_call(
        body,
        name=name,
        grid=(S // tr,),
        in_specs=[row, row, row, vec, vec],
        out_specs=[one, row, row, vec, vec],
        out_shape=[jax.ShapeDtypeStruct((1, LANES), F32), jax.ShapeDtypeStruct((S, D), F32),
                   jax.ShapeDtypeStruct((S, D), BF16), jax.ShapeDtypeStruct((1, D), F32),
                   jax.ShapeDtypeStruct((1, D), F32)],
        compiler_params=_params("arbitrary"),
    )(x1, y, target, gt, g)


def _ada_fwd(c_all, w_local, b_cols, *, name):
    B, D = c_all.shape
    N = w_local.shape[1]
    tn = _tile(N, 512)

    def body(c_ref, w_ref, b_ref, ca_ref, mod_ref):
        cv = c_ref[...]
        ca = cv * _sigmoid(cv)
        ca_ref[...] = ca
        mod_ref[...] = jnp.dot(ca, w_ref[...], preferred_element_type=F32, precision=HIGHEST) + b_ref[...]

    return pl.pallas_call(
        body,
        name=name,
        grid=(N // tn,),
        in_specs=[pl.BlockSpec((B, D), lambda j: (0, 0)), pl.BlockSpec((D, tn), lambda j: (0, j)),
                  pl.BlockSpec((1, tn), lambda j: (0, j))],
        out_specs=[pl.BlockSpec((B, D), lambda j: (0, 0)), pl.BlockSpec((B, tn), lambda j: (0, j))],
        out_shape=[jax.ShapeDtypeStruct((B, D), F32), jax.ShapeDtypeStruct((B, N), F32)],
        compiler_params=_params("arbitrary"),
    )(c_all, w_local, b_cols)


def _rope_tables(S, width, lane_off):
    pos = jnp.arange(S, dtype=F32)
    inv = ROPE_THETA ** (-jnp.arange(0, MLA_ROPE, 2, dtype=F32) / MLA_ROPE)
    ang = pos[:, None] * inv[None, :]
    ang = jnp.concatenate([ang, ang], axis=-1)
    cos, sin = jnp.cos(ang), jnp.sin(ang)
    first = (jnp.arange(MLA_ROPE) < ROPE_HALF)[None, :]
    sa = jnp.where(first, -sin, 0.0)
    sb = jnp.where(first, 0.0, sin)

    def place(t, fill):
        return jnp.pad(t, ((0, 0), (lane_off, width - lane_off - MLA_ROPE)), constant_values=fill)

    return place(cos, 1.0), place(sa, 0.0), place(sb, 0.0)


def _rope_apply(x, cos, sa, sb, width, transpose):
    if transpose:
        return x * cos + pltpu.roll(x * sa, ROPE_HALF, 1) + pltpu.roll(x * sb, width - ROPE_HALF, 1)
    return x * cos + pltpu.roll(x, width - ROPE_HALF, 1) * sa + pltpu.roll(x, ROPE_HALF, 1) * sb


def _rope(x, tables, *, heads, width, transpose, name, off=0):
    S = x.shape[0]
    cb = _col_view(width, off)
    tr = _tile(S, 512, 16)
    tab = pl.BlockSpec((tr, width), lambda i, h: (i, 0))

    def body(x_ref, c_ref, sa_ref, sb_ref, o_ref):
        o_ref[...] = _rope_apply(x_ref[...].astype(F32), c_ref[...], sa_ref[...], sb_ref[...], width,
                                 transpose).astype(o_ref.dtype)

    return pl.pallas_call(
        body,
        name=name,
        grid=(S // tr, heads),
        in_specs=[pl.BlockSpec((tr, width), lambda i, h: (i, cb + h)), tab, tab, tab],
        out_specs=pl.BlockSpec((tr, width), lambda i, h: (i, h)),
        out_shape=jax.ShapeDtypeStruct((S, heads * width), BF16),
        compiler_params=_params("parallel", "parallel"),
    )(x, *tables)


def _assemble_k(kv, krr, *, heads, name):
    S = kv.shape[0]
    tr = _tile(S, 512, 16)

    def body(kn_ref, kr_ref, o_ref):
        o_ref[:, :MLA_NOPE] = kn_ref[...]
        o_ref[:, MLA_NOPE:] = kr_ref[...]

    return pl.pallas_call(
        body,
        name=name,
        grid=(S // tr, heads),
        in_specs=[pl.BlockSpec((tr, MLA_NOPE), lambda i, h: (i, 2 * h)),
                  pl.BlockSpec((tr, LANES), lambda i, h: (i, 0))],
        out_specs=pl.BlockSpec((tr, MLA_QK_PAD), lambda i, h: (i, h)),
        out_shape=jax.ShapeDtypeStruct((S, heads * MLA_QK_PAD), BF16),
        compiler_params=_params("parallel", "parallel"),
    )(kv, krr)


def _assemble_k_bwd(dK, dV, tables, *, heads, name):
    S = dK.shape[0]
    tr = _tile(S, 512, 16)
    tab = pl.BlockSpec((tr, LANES), lambda i, h: (i, 0))

    def body(dk_ref, dv_ref, c_ref, sa_ref, sb_ref, dkv_ref, dkr_ref, acc_ref):
        h = pl.program_id(1)

        @pl.when(h == 0)
        def _():
            acc_ref[...] = jnp.zeros_like(acc_ref)

        dkv_ref[:, :MLA_NOPE] = dk_ref[:, :MLA_NOPE]
        dkv_ref[:, MLA_NOPE:] = dv_ref[...]
        acc_ref[...] += dk_ref[:, MLA_NOPE:].astype(F32)

        @pl.when(h == heads - 1)
        def _():
            dkr_ref[...] = _rope_apply(acc_ref[...], c_ref[...], sa_ref[...], sb_ref[...], LANES,
                                       True).astype(dkr_ref.dtype)

    return pl.pallas_call(
        body,
        name=name,
        grid=(S // tr, heads),
        in_specs=[pl.BlockSpec((tr, MLA_QK_PAD), lambda i, h: (i, h)),
                  pl.BlockSpec((tr, MLA_V), lambda i, h: (i, h)), tab, tab, tab],
        out_specs=[pl.BlockSpec((tr, MLA_QK_PAD), lambda i, h: (i, h)),
                   pl.BlockSpec((tr, LANES), lambda i, h: (i, 0))],
        out_shape=[jax.ShapeDtypeStruct((S, heads * MLA_QK_PAD), BF16), jax.ShapeDtypeStruct((S, LANES), BF16)],
        scratch_shapes=[pltpu.VMEM((tr, LANES), F32)],
        compiler_params=_params("parallel", "arbitrary"),
    )(dK, dV, *tables)


MLA_SCALE = MLA_QK ** -0.5


def _lane_tile(v, n):
    return v if n == LANES else jnp.tile(v, (1, n // LANES))


def _causal_mask(s):
    rows = lax.broadcasted_iota(jnp.int32, s.shape, 0)
    cols = lax.broadcasted_iota(jnp.int32, s.shape, 1)
    return jnp.where(cols <= rows, s, NEG)


def _flash_fwd(Q, K, KV, *, heads, name, comm=()):
    S = Q.shape[0]
    t = _tile(S, 512)
    nb = S // t

    def body(q_ref, k_ref, v_ref, o_ref, lse_ref, m_scr, l_scr, acc_scr):
        qi, ki = pl.program_id(1), pl.program_id(2)

        @pl.when(ki == 0)
        def _():
            m_scr[...] = jnp.full_like(m_scr, NEG)
            l_scr[...] = jnp.zeros_like(l_scr)
            acc_scr[...] = jnp.zeros_like(acc_scr)

        def step(masked):
            s = lax.dot_general(q_ref[...], k_ref[...], NT, preferred_element_type=F32) * MLA_SCALE
            if masked:
                s = _causal_mask(s)
            m_prev = m_scr[...]
            m_new = jnp.maximum(m_prev, jnp.max(s, axis=1, keepdims=True))
            alpha = jnp.exp(m_prev - m_new)
            p = jnp.exp(s - _lane_tile(m_new, t))
            l_scr[...] = alpha * l_scr[...] + jnp.sum(p, axis=1, keepdims=True)
            acc_scr[...] = alpha * acc_scr[...] + jnp.dot(p.astype(BF16), v_ref[...], preferred_element_type=F32)
            m_scr[...] = m_new

        pl.when(ki < qi)(lambda: step(False))
        pl.when(ki == qi)(lambda: step(True))

        @pl.when(ki == nb - 1)
        def _():
            o_ref[...] = (acc_scr[...] / l_scr[...]).astype(o_ref.dtype)
            lse_ref[0] = m_scr[...] + jnp.log(l_scr[...])

    return _call(
        body,
        name=name,
        grid=(heads, nb, nb),
        in_specs=[pl.BlockSpec((t, MLA_QK_PAD), lambda h, qi, ki: (qi, h)),
                  pl.BlockSpec((t, MLA_QK_PAD), lambda h, qi, ki: (jnp.minimum(ki, qi), h)),
                  pl.BlockSpec((t, MLA_V), lambda h, qi, ki: (jnp.minimum(ki, qi), 2 * h + 1))],
        out_specs=[pl.BlockSpec((t, MLA_V), lambda h, qi, ki: (qi, h)),
                   pl.BlockSpec((1, t, LANES), lambda h, qi, ki: (h, qi, 0))],
        out_shape=[jax.ShapeDtypeStruct((S, heads * MLA_V), BF16),
                   jax.ShapeDtypeStruct((heads, S, LANES), F32)],
        scratch_shapes=[pltpu.VMEM((t, LANES), F32), pltpu.VMEM((t, LANES), F32), pltpu.VMEM((t, MLA_V), F32)],
        sem=("parallel", "parallel", "arbitrary"),
        args=(Q, K, KV),
        comm=comm,
    )


def _flash_bwd_dq(Q, K, KV, dO, O, lse, *, heads, name, comm=()):
    S = Q.shape[0]
    t = _tile(S, 512)
    nb = S // t

    def body(q_ref, k_ref, v_ref, do_ref, o_ref, lse_ref, dq_ref, delta_ref, acc_scr, delta_scr):
        qi, ki = pl.program_id(1), pl.program_id(2)

        @pl.when(ki == 0)
        def _():
            acc_scr[...] = jnp.zeros_like(acc_scr)
            d = jnp.sum(do_ref[...].astype(F32) * o_ref[...].astype(F32), axis=1, keepdims=True)
            delta_scr[...] = jnp.broadcast_to(d, delta_scr.shape)

        def step(masked):
            s = lax.dot_general(q_ref[...], k_ref[...], NT, preferred_element_type=F32) * MLA_SCALE
            if masked:
                s = _causal_mask(s)
            p = jnp.exp(s - _lane_tile(lse_ref[0], t))
            dp = lax.dot_general(do_ref[...], v_ref[...], NT, preferred_element_type=F32)
            ds = p * (dp - _lane_tile(delta_scr[...], t)) * MLA_SCALE
            acc_scr[...] += jnp.dot(ds.astype(BF16), k_ref[...], preferred_element_type=F32)

        pl.when(ki < qi)(lambda: step(False))
        pl.when(ki == qi)(lambda: step(True))

        @pl.when(ki == nb - 1)
        def _():
            dq_ref[...] = acc_scr[...].astype(dq_ref.dtype)
            delta_ref[0] = delta_scr[...]

    qrow = lambda h, qi, ki: (qi, h)
    return _call(
        body,
        name=name,
        grid=(heads, nb, nb),
        in_specs=[pl.BlockSpec((t, MLA_QK_PAD), qrow),
                  pl.BlockSpec((t, MLA_QK_PAD), lambda h, qi, ki: (jnp.minimum(ki, qi), h)),
                  pl.BlockSpec((t, MLA_V), lambda h, qi, ki: (jnp.minimum(ki, qi), 2 * h + 1)),
                  pl.BlockSpec((t, MLA_V), qrow),
                  pl.BlockSpec((t, MLA_V), qrow),
                  pl.BlockSpec((1, t, LANES), lambda h, qi, ki: (h, qi, 0))],
        out_specs=[pl.BlockSpec((t, MLA_QK_PAD), qrow),
                   pl.BlockSpec((1, t, LANES), lambda h, qi, ki: (h, qi, 0))],
        out_shape=[jax.ShapeDtypeStruct((S, heads * MLA_QK_PAD), BF16),
                   jax.ShapeDtypeStruct((heads, S, LANES), F32)],
        scratch_shapes=[pltpu.VMEM((t, MLA_QK_PAD), F32), pltpu.VMEM((t, LANES), F32)],
        sem=("parallel", "parallel", "arbitrary"),
        args=(Q, K, KV, dO, O, lse),
        comm=comm,
    )


def _flash_bwd_dkv(Q, K, KV, dO, lse, delta, *, heads, name, comm=()):
    S = Q.shape[0]
    t = _tile(S, 512)
    nb = S // t

    def body(q_ref, k_ref, v_ref, do_ref, lse_ref, delta_ref, dk_ref, dv_ref, dk_scr, dv_scr):
        ki, qi = pl.program_id(1), pl.program_id(2)

        @pl.when(qi == 0)
        def _():
            dk_scr[...] = jnp.zeros_like(dk_scr)
            dv_scr[...] = jnp.zeros_like(dv_scr)

        def step(masked):
            s = lax.dot_general(q_ref[...], k_ref[...], NT, preferred_element_type=F32) * MLA_SCALE
            if masked:
                s = _causal_mask(s)
            p = jnp.exp(s - _lane_tile(lse_ref[0], t))
            dv_scr[...] += lax.dot_general(p.astype(BF16), do_ref[...], TN, preferred_element_type=F32)
            dp = lax.dot_general(do_ref[...], v_ref[...], NT, preferred_element_type=F32)
            ds = p * (dp - _lane_tile(delta_ref[0], t)) * MLA_SCALE
            dk_scr[...] += lax.dot_general(ds.astype(BF16), q_ref[...], TN, preferred_element_type=F32)

        pl.when(qi > ki)(lambda: step(False))
        pl.when(qi == ki)(lambda: step(True))

        @pl.when(qi == nb - 1)
        def _():
            dk_ref[...] = dk_scr[...].astype(dk_ref.dtype)
            dv_ref[...] = dv_scr[...].astype(dv_ref.dtype)

    qrow = lambda h, ki, qi: (jnp.maximum(qi, ki), h)
    qstat = lambda h, ki, qi: (h, jnp.maximum(qi, ki), 0)
    return _call(
        body,
        name=name,
        grid=(heads, nb, nb),
        in_specs=[pl.BlockSpec((t, MLA_QK_PAD), qrow),
                  pl.BlockSpec((t, MLA_QK_PAD), lambda h, ki, qi: (ki, h)),
                  pl.BlockSpec((t, MLA_V), lambda h, ki, qi: (ki, 2 * h + 1)),
                  pl.BlockSpec((t, MLA_V), qrow),
                  pl.BlockSpec((1, t, LANES), qstat),
                  pl.BlockSpec((1, t, LANES), qstat)],
        out_specs=[pl.BlockSpec((t, MLA_QK_PAD), lambda h, ki, qi: (ki, h)),
                   pl.BlockSpec((t, MLA_V), lambda h, ki, qi: (ki, h))],
        out_shape=[jax.ShapeDtypeStruct((S, heads * MLA_QK_PAD), BF16),
                   jax.ShapeDtypeStruct((S, heads * MLA_V), BF16)],
        scratch_shapes=[pltpu.VMEM((t, MLA_QK_PAD), F32), pltpu.VMEM((t, MLA_V), F32)],
        sem=("parallel", "parallel", "arbitrary"),
        args=(Q, K, KV, dO, lse, delta),
        comm=comm,
    )


SWA_SCALE = SWA_HD ** -0.5


def _t5_bucket_table():
    a = np.arange(BLOCK)[:, None]
    j = np.arange(2 * BLOCK)[None, :]
    dist = BLOCK + a - j
    max_exact = REL_BUCKETS // 2
    n = np.maximum(dist, 0)
    large = max_exact + (np.log(np.maximum(n, 1).astype(np.float32) / np.float32(max_exact))
                         / np.float32(math.log(REL_MAX_DIST / max_exact))
                         * np.float32(REL_BUCKETS - max_exact)).astype(np.int32)
    large = np.minimum(large, REL_BUCKETS - 1)
    bucket = np.where(n < max_exact, n, large)
    valid = (dist >= 0) & (dist < WINDOW)
    return bucket.astype(np.int32), valid


def _swa_scores(q_ref, kp_ref, kc_ref, bias_ref, qb, G):
    q2 = q_ref[...].reshape(G * BLOCK, SWA_HD)
    kb = jnp.concatenate([kp_ref[0], kc_ref[0]], axis=0)
    s = lax.dot_general(q2, kb, NT, preferred_element_type=F32) * SWA_SCALE
    s = s + bias_ref[...].reshape(G * BLOCK, 2 * BLOCK)
    cols = lax.broadcasted_iota(jnp.int32, s.shape, 1)
    s = jnp.where((cols >= BLOCK) | (qb > 0), s, NEG)
    return q2, kb, s


def _swa_probs(s, sink):
    m = jnp.maximum(jnp.max(s, axis=1, keepdims=True), sink)
    e = jnp.exp(s - m)
    es = jnp.exp(sink - m)
    den = jnp.sum(e, axis=1, keepdims=True) + es
    return e / den, es / den


def _swa_fwd(q, k, v, bias, sink, *, name, comm=()):
    H, S, _ = q.shape
    G = H // SWA_KVH
    nb = S // BLOCK
    cur = lambda kh, qb: (kh, qb, 0)
    prev = lambda kh, qb: (kh, jnp.maximum(qb - 1, 0), 0)
    kvspec = lambda im: pl.BlockSpec((1, BLOCK, SWA_HD), im)

    def body(q_ref, kc_ref, kp_ref, vc_ref, vp_ref, bias_ref, sink_ref, o_ref):
        qb = pl.program_id(1)
        _, _, s = _swa_scores(q_ref, kp_ref, kc_ref, bias_ref, qb, G)
        p, _ = _swa_probs(s, sink_ref[0][:, :1])
        vb = jnp.concatenate([vp_ref[0], vc_ref[0]], axis=0)
        o = jnp.dot(p.astype(BF16), vb, preferred_element_type=F32)
        o_ref[...] = o.reshape(G, BLOCK, SWA_HD).astype(o_ref.dtype)

    outs, moved = _call(
        body,
        name=name,
        grid=(SWA_KVH, nb),
        in_specs=[pl.BlockSpec((G, BLOCK, SWA_HD), cur), kvspec(cur), kvspec(prev), kvspec(cur), kvspec(prev),
                  pl.BlockSpec((G, BLOCK, 2 * BLOCK), lambda kh, qb: (kh, 0, 0)),
                  pl.BlockSpec((1, G * BLOCK, LANES), lambda kh, qb: (kh, 0, 0))],
        out_specs=[pl.BlockSpec((G, BLOCK, SWA_HD), cur)],
        out_shape=[jax.ShapeDtypeStruct((H, S, SWA_HD), BF16)],
        sem=("parallel", "parallel"),
        args=(q, k, k, v, v, bias, sink),
        comm=comm,
    )
    return outs[0], moved


def _swa_bwd(q, k, v, bias, sink, do, *, name, comm=()):
    H, S, _ = q.shape
    G = H // SWA_KVH
    nb = S // BLOCK
    cur = lambda kh, qb: (kh, jnp.minimum(qb, nb - 1), 0)
    prev = lambda kh, qb: (kh, jnp.maximum(jnp.minimum(qb, nb - 1) - 1, 0), 0)
    lag = lambda kh, qb: (kh, jnp.maximum(qb - 1, 0), 0)
    kvspec = lambda im: pl.BlockSpec((1, BLOCK, SWA_HD), im)

    def body(q_ref, kc_ref, kp_ref, vc_ref, vp_ref, bias_ref, sink_ref, do_ref,
             dq_ref, dk_ref, dv_ref, dbias_ref, dsink_ref, ck_scr, cv_scr):
        qb = pl.program_id(1)

        @pl.when(qb == 0)
        def _():
            dbias_ref[...] = jnp.zeros_like(dbias_ref)
            dsink_ref[...] = jnp.zeros_like(dsink_ref)
            ck_scr[...] = jnp.zeros_like(ck_scr)
            cv_scr[...] = jnp.zeros_like(cv_scr)

        @pl.when(qb < nb)
        def _():
            q2, kb, s = _swa_scores(q_ref, kp_ref, kc_ref, bias_ref, qb, G)
            p, ps = _swa_probs(s, sink_ref[0][:, :1])
            vb = jnp.concatenate([vp_ref[0], vc_ref[0]], axis=0)
            do2 = do_ref[...].reshape(G * BLOCK, SWA_HD)
            dp = lax.dot_general(do2, vb, NT, preferred_element_type=F32)
            delta = jnp.sum(dp * p, axis=1, keepdims=True)
            ds = p * (dp - delta)
            dbias_ref[...] += ds.reshape(G, BLOCK, 2 * BLOCK)
            dsk = jnp.sum((-ps * delta).reshape(G, BLOCK, 1), axis=1)
            dsink_ref[0] += jnp.broadcast_to(dsk, (G, LANES))
            dsb = (ds * SWA_SCALE).astype(BF16)
            dq_ref[...] = jnp.dot(dsb, kb, preferred_element_type=F32).reshape(G, BLOCK, SWA_HD).astype(dq_ref.dtype)
            dkb = lax.dot_general(dsb, q2, TN, preferred_element_type=F32)
            dvb = lax.dot_general(p.astype(BF16), do2, TN, preferred_element_type=F32)
            dk_ref[0] = (ck_scr[...] + dkb[:BLOCK]).astype(dk_ref.dtype)
            dv_ref[0] = (cv_scr[...] + dvb[:BLOCK]).astype(dv_ref.dtype)
            ck_scr[...] = dkb[BLOCK:]
            cv_scr[...] = dvb[BLOCK:]

        @pl.when(qb == nb)
        def _():
            dk_ref[0] = ck_scr[...].astype(dk_ref.dtype)
            dv_ref[0] = cv_scr[...].astype(dv_ref.dtype)

    return _call(
        body,
        name=name,
        grid=(SWA_KVH, nb + 1),
        in_specs=[pl.BlockSpec((G, BLOCK, SWA_HD), cur), kvspec(cur), kvspec(prev), kvspec(cur), kvspec(prev),
                  pl.BlockSpec((G, BLOCK, 2 * BLOCK), lambda kh, qb: (kh, 0, 0)),
                  pl.BlockSpec((1, G * BLOCK, LANES), lambda kh, qb: (kh, 0, 0)),
                  pl.BlockSpec((G, BLOCK, SWA_HD), cur)],
        out_specs=[pl.BlockSpec((G, BLOCK, SWA_HD), cur), kvspec(lag), kvspec(lag),
                   pl.BlockSpec((G, BLOCK, 2 * BLOCK), lambda kh, qb: (kh, 0, 0)),
                   pl.BlockSpec((1, G, LANES), lambda kh, qb: (kh, 0, 0))],
        out_shape=[jax.ShapeDtypeStruct((H, S, SWA_HD), BF16),
                   jax.ShapeDtypeStruct((SWA_KVH, S, SWA_HD), BF16),
                   jax.ShapeDtypeStruct((SWA_KVH, S, SWA_HD), BF16),
                   jax.ShapeDtypeStruct((H, BLOCK, 2 * BLOCK), F32),
                   jax.ShapeDtypeStruct((SWA_KVH, G, LANES), F32)],
        scratch_shapes=[pltpu.VMEM((BLOCK, SWA_HD), F32), pltpu.VMEM((BLOCK, SWA_HD), F32)],
        sem=("parallel", "arbitrary"),
        args=(q, k, k, v, v, bias, sink, do),
        comm=comm,
    )


def _gate_mix(z, o_a, o_b, *, D, off_a, off_b, name):
    S = z.shape[0]
    tr = _tile(S, 256, 16)
    row = pl.BlockSpec((tr, D), lambda i: (i, 0))
    ca, cb = _col_view(D, off_a), _col_view(D, off_b)

    def body(ga_ref, gb_ref, oa_ref, ob_ref, m_ref):
        m = (_sigmoid(ga_ref[...].astype(F32)) * oa_ref[...].astype(F32)
             + _sigmoid(gb_ref[...].astype(F32)) * ob_ref[...].astype(F32))
        m_ref[...] = m.astype(m_ref.dtype)

    return pl.pallas_call(
        body,
        name=name,
        grid=(S // tr,),
        in_specs=[pl.BlockSpec((tr, D), lambda i: (i, ca)), pl.BlockSpec((tr, D), lambda i: (i, cb)), row, row],
        out_specs=row,
        out_shape=jax.ShapeDtypeStruct((S, D), BF16),
        compiler_params=_params("parallel"),
    )(z, z, o_a, o_b)


def _gate_mix_bwd(dm, z, o_a, o_b, *, D, off_a, off_b, name):
    S = z.shape[0]
    tr = _tile(S, 256, 16)
    row = pl.BlockSpec((tr, D), lambda i: (i, 0))
    ca, cb = _col_view(D, off_a), _col_view(D, off_b)

    def body(dm_ref, ga_ref, gb_ref, oa_ref, ob_ref, dga_ref, dgb_ref, doa_ref, dob_ref):
        d = dm_ref[...].astype(F32)
        for g_ref, o_ref, dg_ref, do_ref in ((ga_ref, oa_ref, dga_ref, doa_ref), (gb_ref, ob_ref, dgb_ref, dob_ref)):
            sg = _sigmoid(g_ref[...].astype(F32))
            dg_ref[...] = (d * o_ref[...].astype(F32) * (sg * (1.0 - sg))).astype(dg_ref.dtype)
            do_ref[...] = (d * sg).astype(do_ref.dtype)

    return pl.pallas_call(
        body,
        name=name,
        grid=(S // tr,),
        in_specs=[row, pl.BlockSpec((tr, D), lambda i: (i, ca)), pl.BlockSpec((tr, D), lambda i: (i, cb)), row, row],
        out_specs=[row] * 4,
        out_shape=[jax.ShapeDtypeStruct((S, D), BF16)] * 4,
        compiler_params=_params("parallel"),
    )(dm, z, z, o_a, o_b)


def _conv_taps(buf, cw_ref, cb_ref, rows):
    y = cb_ref[...] + cw_ref[0:1, :] * buf[pl.ds(HALO - 2, rows), :]
    y = y + cw_ref[1:2, :] * buf[pl.ds(HALO - 1, rows), :]
    return y + cw_ref[2:3, :] * buf[pl.ds(HALO, rows), :]


def _conv_gate(up, cw, cb, *, name):
    S, F2 = up.shape
    F = F2 // 2
    tr = _tile(S, 512, HALO)
    tc = _tile(F, 512)
    nc = F // tc
    hb = tr // HALO

    def halo_map(shift):
        return lambda i, j: (jnp.maximum(i * hb - 1, 0), j + shift)

    def body(x1_ref, h1_ref, x2_ref, h2_ref, cw1_ref, cw2_ref, cb1_ref, cb2_ref, a_ref, b1, b2):
        first = pl.program_id(0) == 0
        us = []
        for x_ref, h_ref, cw_ref, cb_ref, buf in ((x1_ref, h1_ref, cw1_ref, cb1_ref, b1),
                                                  (x2_ref, h2_ref, cw2_ref, cb2_ref, b2)):
            buf[0:HALO, :] = jnp.where(first, 0.0, h_ref[...].astype(F32))
            buf[HALO:, :] = x_ref[...].astype(F32)
            us.append(_conv_taps(buf, cw_ref, cb_ref, tr))
        u1, u2 = us
        a_ref[...] = (u1 * _sigmoid(u1) * u2).astype(a_ref.dtype)

    return pl.pallas_call(
        body,
        name=name,
        grid=(S // tr, nc),
        in_specs=[pl.BlockSpec((tr, tc), lambda i, j: (i, j)), pl.BlockSpec((HALO, tc), halo_map(0)),
                  pl.BlockSpec((tr, tc), lambda i, j: (i, j + nc)), pl.BlockSpec((HALO, tc), halo_map(nc)),
                  pl.BlockSpec((CONV_WIDTH, tc), lambda i, j: (0, j)),
                  pl.BlockSpec((CONV_WIDTH, tc), lambda i, j: (0, j + nc)),
                  pl.BlockSpec((1, tc), lambda i, j: (0, j)), pl.BlockSpec((1, tc), lambda i, j: (0, j + nc))],
        out_specs=pl.BlockSpec((tr, tc), lambda i, j: (i, j)),
        out_shape=jax.ShapeDtypeStruct((S, F), BF16),
        scratch_shapes=[pltpu.VMEM((tr + HALO, tc), F32), pltpu.VMEM((tr + HALO, tc), F32)],
        compiler_params=_params("parallel", "parallel"),
    )(up, up, up, up, cw, cw, cb, cb)


def _conv_gate_bwd(up, da, cw, cb, *, name, comm=()):
    S, F2 = up.shape
    F = F2 // 2
    tr = _tile(S, 256, HALO)
    tc = _tile(F, 512)
    nc = F // tc
    hb = tr // HALO
    n_halo = S // HALO
    ni = S // tr
    ext = tr + 8

    def cur(shift):
        return lambda j, i: (i, j % nc + shift)

    def before(shift):
        return lambda j, i: (jnp.maximum(i * hb - 1, 0), j % nc + shift)

    def after(shift):
        return lambda j, i: (jnp.minimum((i + 1) * hb, n_halo - 1), j % nc + shift)

    def vec(rows, shift):
        return pl.BlockSpec((rows, tc), lambda j, i: (0, j % nc + shift))

    def body(x1_ref, p1_ref, n1_ref, x2_ref, p2_ref, n2_ref, da_ref, dan_ref,
             cw1_ref, cw2_ref, cb1_ref, cb2_ref, cwo_ref,
             dup_ref, dcw_ref, dcb_ref, b1, b2, bda, bdu):
        j, i = pl.program_id(0), pl.program_id(1)
        first, last = i == 0, i == ni - 1

        @pl.when(i == 0)
        def _():
            dcw_ref[...] = jnp.zeros_like(dcw_ref)
            dcb_ref[...] = jnp.zeros_like(dcb_ref)

        for x_ref, p_ref, n_ref, buf in ((x1_ref, p1_ref, n1_ref, b1), (x2_ref, p2_ref, n2_ref, b2)):
            buf[0:HALO, :] = jnp.where(first, 0.0, p_ref[...].astype(F32))
            buf[HALO:HALO + tr, :] = x_ref[...].astype(F32)
            buf[HALO + tr:, :] = n_ref[...].astype(F32)
        bda[0:tr, :] = da_ref[...].astype(F32)
        bda[tr:, :] = jnp.where(last, 0.0, dan_ref[...].astype(F32))
        dae = bda[pl.ds(0, ext), :]
        u1 = _conv_taps(b1, cw1_ref, cb1_ref, ext)
        sg = _sigmoid(u1)

        def finish(du, own):
            bdu[...] = du
            d0 = bdu[pl.ds(0, tr), :]
            dup = cwo_ref[2:3, :] * d0 + cwo_ref[1:2, :] * bdu[pl.ds(1, tr), :] + cwo_ref[0:1, :] * bdu[pl.ds(2, tr), :]
            dup_ref[...] = dup.astype(dup_ref.dtype)
            dcb_ref[...] += jnp.sum(d0, axis=0, keepdims=True)
            for tap in range(CONV_WIDTH):
                dcw_ref[tap:tap + 1, :] += jnp.sum(d0 * own[pl.ds(HALO - 2 + tap, tr), :], axis=0, keepdims=True)

        @pl.when(j < nc)
        def _():
            u2 = _conv_taps(b2, cw2_ref, cb2_ref, ext)
            finish(dae * u2 * (sg * (1.0 + u1 * (1.0 - sg))), b1)

        @pl.when(j >= nc)
        def _():
            finish(dae * (u1 * sg), b2)

    return _call(
        body,
        name=name,
        grid=(2 * nc, ni),
        in_specs=[pl.BlockSpec((tr, tc), cur(0)), pl.BlockSpec((HALO, tc), before(0)), pl.BlockSpec((HALO, tc), after(0)),
                  pl.BlockSpec((tr, tc), cur(nc)), pl.BlockSpec((HALO, tc), before(nc)), pl.BlockSpec((HALO, tc), after(nc)),
                  pl.BlockSpec((tr, tc), cur(0)), pl.BlockSpec((HALO, tc), after(0)),
                  vec(CONV_WIDTH, 0), vec(CONV_WIDTH, nc), vec(1, 0), vec(1, nc),
                  pl.BlockSpec((CONV_WIDTH, tc), lambda j, i: (0, j))],
        out_specs=[pl.BlockSpec((tr, tc), lambda j, i: (i, j)),
                   pl.BlockSpec((CONV_WIDTH, tc), lambda j, i: (0, j)),
                   pl.BlockSpec((1, tc), lambda j, i: (0, j))],
        out_shape=[jax.ShapeDtypeStruct((S, F2), BF16), jax.ShapeDtypeStruct((CONV_WIDTH, F2), F32),
                   jax.ShapeDtypeStruct((1, F2), F32)],
        scratch_shapes=[pltpu.VMEM((tr + 2 * HALO, tc), F32), pltpu.VMEM((tr + 2 * HALO, tc), F32),
                        pltpu.VMEM((tr + HALO, tc), F32), pltpu.VMEM((ext, tc), F32)],
        sem=("parallel", "arbitrary"),
        args=(up, up, up, up, up, up, da, da, cw, cw, cb, cb, cw),
        comm=comm,
    )


def _adam_math(w, g, m, v):
    m = ADAM_B1 * m + (1.0 - ADAM_B1) * g
    v = ADAM_B2 * v + (1.0 - ADAM_B2) * (g * g)
    m_hat = m / (1.0 - ADAM_B1 ** ADAM_STEP)
    v_hat = v / (1.0 - ADAM_B2 ** ADAM_STEP)
    delta = -ADAM_LR * (m_hat / (jnp.sqrt(v_hat) + ADAM_EPS) + ADAM_WD * w)
    return delta, m, v


def _adamw(w, m, v, parts, *, name):
    R, C = w.shape
    tr = _tile(R, 256, 16)
    row = pl.BlockSpec((tr, C), lambda i: (i, 0))

    def body(w_ref, m_ref, v_ref, p_ref, g_ref, d_ref, m2_ref, v2_ref):
        g = p_ref[0].astype(F32)
        for k in range(1, N_DEV):
            g = g + p_ref[k].astype(F32)
        g_ref[...] = g
        d_ref[...], m2_ref[...], v2_ref[...] = _adam_math(w_ref[...], g, m_ref[...], v_ref[...])

    return pl.pallas_call(
        body,
        name=name,
        grid=(R // tr,),
        in_specs=[row, row, row, pl.BlockSpec((N_DEV, tr, C), lambda i: (0, i, 0))],
        out_specs=[row] * 4,
        out_shape=[jax.ShapeDtypeStruct((R, C), F32)] * 4,
        compiler_params=_params("parallel"),
    )(w, m, v, parts)


def _adamw_ada(w, m, v, cact_t, dmod_cols, *, name):
    R, C = w.shape
    B = cact_t.shape[1]
    tr = _tile(R, 256, 8)
    row = pl.BlockSpec((tr, C), lambda i: (i, 0))

    def body(w_ref, m_ref, v_ref, c_ref, d_ref, g_ref, dl_ref, m2_ref, v2_ref):
        g = c_ref[:, 0:1] * d_ref[0:1, :]
        for b in range(1, B):
            g = g + c_ref[:, b:b + 1] * d_ref[b:b + 1, :]
        g_ref[...] = g
        dl_ref[...], m2_ref[...], v2_ref[...] = _adam_math(w_ref[...], g, m_ref[...], v_ref[...])

    return pl.pallas_call(
        body,
        name=name,
        grid=(R // tr,),
        in_specs=[row, row, row, pl.BlockSpec((tr, B), lambda i: (i, 0)), pl.BlockSpec((B, C), lambda i: (0, 0))],
        out_specs=[row] * 4,
        out_shape=[jax.ShapeDtypeStruct((R, C), F32)] * 4,
        compiler_params=_params("parallel"),
    )(w, m, v, cact_t, dmod_cols)


def _z_layout(D, q_rank, kv_rank):
    kv = SWA_KVH * SWA_HD
    orig = {}
    o = 0
    for nm, w in (("cq", q_rank), ("ckv", kv_rank), ("kr", MLA_ROPE), ("qs", D), ("ks", kv), ("vs", kv),
                  ("ga", D), ("gb", D)):
        orig[nm] = (o, w)
        o += w
    blockw = {"cq": q_rank, "ckv": kv_rank, "kr": LANES, "qs": D, "ks": kv, "vs": kv, "ga": D, "gb": D}
    best = None
    for perm in itertools.permutations(("cq", "ckv", "ks", "vs", "kr")):
        off, new = 0, {}
        for nm in ("ga", "gb", "qs") + perm:
            off = _round_up(off, blockw[nm])
            new[nm] = off
            off += blockw[nm]
        if best is None or off < best[0]:
            best = (off, new)
    total = _round_up(best[0], 1024 if best[0] > 4096 else 512)
    return orig, best[1], blockw, total, o


def _permute_w_in(w, lay):
    orig, new, blockw, total, _ = lay
    parts, at = [], 0
    for nm in sorted(new, key=new.get):
        if new[nm] > at:
            parts.append(jnp.zeros((w.shape[0], new[nm] - at), w.dtype))
        o, wd = orig[nm]
        parts.append(w[:, o:o + wd])
        if blockw[nm] > wd:
            parts.append(jnp.zeros((w.shape[0], blockw[nm] - wd), w.dtype))
        at = new[nm] + blockw[nm]
    if total > at:
        parts.append(jnp.zeros((w.shape[0], total - at), w.dtype))
    return jnp.concatenate(parts, axis=1)


def _unpermute_w_in(wp, lay):
    orig, new, _, _, _ = lay
    return jnp.concatenate([wp[:, new[nm]:new[nm] + orig[nm][1]] for nm in sorted(orig, key=lambda n: orig[n][0])],
                           axis=1)


def _assemble_dz(parts, lay, S):
    _, new, blockw, total, _ = lay
    cols, at = [], 0
    for nm in sorted(new, key=new.get):
        if new[nm] > at:
            cols.append(jnp.zeros((S, new[nm] - at), BF16))
        cols.append(parts[nm])
        at = new[nm] + blockw[nm]
    if total > at:
        cols.append(jnp.zeros((S, total - at), BF16))
    return jnp.concatenate(cols, axis=1)


def _unshard_cols(g):
    return jnp.transpose(g, (1, 0, 2)).reshape(g.shape[1], N_DEV * g.shape[2])


def _shard_cols(w):
    K, N = w.shape
    return jnp.transpose(w.reshape(K, N_DEV, N // N_DEV), (1, 0, 2))


def _pack(vecs, rows):
    flat = jnp.concatenate([v.reshape(-1) for v in vecs])
    return jnp.pad(flat, (0, rows * LANES - flat.shape[0])).reshape(rows, LANES)


def kernel(x, c, w_ada, b_ada, g_pre_mix, g_post_mix, w_in, g_q_lat, w_uq, g_kv_lat, w_ukv, rel_bias, sinks, w_o, g_pre_ffn, g_post_ffn, w_up, conv_w, conv_b, w_down, loss_target, m_w_ada, m_b_ada, m_g_pre_mix, m_g_post_mix, m_w_in, m_g_q_lat, m_w_uq, m_g_kv_lat, m_w_ukv, m_rel_bias, m_sinks, m_w_o, m_g_pre_ffn, m_g_post_ffn, m_w_up, m_conv_w, m_conv_b, m_w_down, v_w_ada, v_b_ada, v_g_pre_mix, v_g_post_mix, v_w_in, v_g_q_lat, v_w_uq, v_g_kv_lat, v_w_ukv, v_rel_bias, v_sinks, v_w_o, v_g_pre_ffn, v_g_post_ffn, v_w_up, v_conv_w, v_conv_b, v_w_down):
    S, D = x.shape[1], x.shape[2]
    Q_RANK, KV_RANK = g_q_lat.shape[1], g_kv_lat.shape[1]
    H_MLA = D // MLA_V
    H_SWA = D // SWA_HD
    G_SWA = H_SWA // SWA_KVH
    F2 = w_up.shape[2] * N_DEV
    F = F2 // 2
    ada_n = w_ada.shape[2]
    me = 4 * lax.axis_index("x") + 2 * lax.axis_index("y") + lax.axis_index("c")
    lay = _z_layout(D, Q_RANK, KV_RANK)
    _, zoff, _, NZ, in_cols = lay
    assert in_cols == w_in.shape[2] * N_DEV

    x2, tgt = x[0], loss_target[0]

    cw_n = conv_w.shape[2]
    small = jnp.concatenate([jnp.pad(c, ((0, 7), (0, 0))), jnp.pad(conv_w[0], ((0, 8 - CONV_WIDTH), (0, 0)))], axis=1)
    small_all = _all_gather(small, name="ag_cond", in_vmem=True)
    c_all = small_all[:, 0, :D]
    cw_full = _unshard_cols(small_all[:, :CONV_WIDTH, D:])
    b_cols = lax.dynamic_slice_in_dim(b_ada, me * ada_n, ada_n, axis=1)
    c_act, mod_cols = _ada_fwd(c_all, w_ada[0], b_cols, name="ada_fwd")
    mod_all = _all_gather(mod_cols, name="ag_mod", in_vmem=True)
    mod_me = lax.dynamic_index_in_dim(mod_all, me, axis=1, keepdims=False).reshape(1, N_DEV * ada_n)
    sh1, sc1, gt1, sh2, sc2, gt2 = [mod_me[:, k * D:(k + 1) * D] for k in range(6)]

    w_in_p = _permute_w_in(_unshard_cols(_all_gather(w_in[0].astype(BF16), name="ag_w_in", in_vmem=False)), lay)

    h1 = _prenorm(x2, g_pre_mix, sc1, sh1, name="prenorm_mix")
    z, (uq_g, ukv_g, o_g) = _matmul(h1, w_in_p, mode="nn", out_dtype=BF16, name="mm_in",
                                    comm=[("gather", w_uq[0].astype(BF16)), ("gather", w_ukv[0].astype(BF16)),
                                          ("gather", w_o[0].astype(BF16))])
    w_uq_p = jnp.pad(_unshard_cols(uq_g).reshape(Q_RANK, H_MLA, MLA_QK), ((0, 0), (0, 0), (0, MLA_QK_PAD - MLA_QK))
                     ).reshape(Q_RANK, H_MLA * MLA_QK_PAD)
    w_ukv_f = _unshard_cols(ukv_g)
    w_o_f = o_g.reshape(D, D)
    cqn = _prenorm(z, g_q_lat, None, None, name="norm_cq", off=zoff["cq"], width=Q_RANK)
    ckvn = _prenorm(z, g_kv_lat, None, None, name="norm_ckv", off=zoff["ckv"], width=KV_RANK)
    q_raw = _matmul(cqn, w_uq_p, mode="nn", out_dtype=BF16, name="mm_uq")
    kv = _matmul(ckvn, w_ukv_f, mode="nn", out_dtype=BF16, name="mm_ukv")
    tab_q = _rope_tables(S, MLA_QK_PAD, MLA_NOPE)
    tab_k = _rope_tables(S, LANES, 0)
    Qr = _rope(q_raw, tab_q, heads=H_MLA, width=MLA_QK_PAD, transpose=False, name="rope_q")
    krr = _rope(z, tab_k, heads=1, width=LANES, transpose=False, name="rope_k", off=zoff["kr"])
    Kc = _assemble_k(kv, krr, heads=H_MLA, name="assemble_k")
    (o_a, lse), (up_g,) = _flash_fwd(Qr, Kc, kv, heads=H_MLA, name="mla_fwd", comm=[("gather", w_up[0].astype(BF16))])
    w_up_f = _unshard_cols(up_g)

    bucket, valid = _t5_bucket_table()
    onehot = (jnp.asarray(bucket).reshape(-1, 1) == jnp.arange(LANES)[None, :]).astype(F32)
    rb_pad = jnp.pad(rel_bias, ((0, LANES - REL_BUCKETS), (0, LANES - H_SWA)))
    bias_t = _matmul(onehot, rb_pad, mode="nn", out_dtype=F32, name="bias_table", tm=2048, precision=HIGHEST)
    bias_full = jnp.transpose(bias_t[:, :H_SWA].reshape(BLOCK, 2 * BLOCK, H_SWA), (2, 0, 1))
    bias_full = jnp.where(jnp.asarray(valid)[None], bias_full, NEG)
    sink_rows = jnp.broadcast_to(sinks.reshape(SWA_KVH, G_SWA, 1, 1), (SWA_KVH, G_SWA, BLOCK, LANES)
                                 ).reshape(SWA_KVH, G_SWA * BLOCK, LANES)
    kvw = SWA_KVH * SWA_HD

    def heads_first(t, n):
        return jnp.transpose(t.reshape(S, n, SWA_HD), (1, 0, 2))

    def heads_last(t):
        return jnp.transpose(t, (1, 0, 2)).reshape(S, t.shape[0] * SWA_HD)

    qs_h = heads_first(z[:, zoff["qs"]:zoff["qs"] + D], H_SWA)
    ks_h = heads_first(z[:, zoff["ks"]:zoff["ks"] + kvw], SWA_KVH)
    vs_h = heads_first(z[:, zoff["vs"]:zoff["vs"] + kvw], SWA_KVH)
    o_b_h, (down_g,) = _swa_fwd(qs_h, ks_h, vs_h, bias_full, sink_rows, name="swa_fwd",
                                comm=[("gather", w_down[0].astype(BF16))])
    o_b = heads_last(o_b_h)
    w_down_f = down_g.reshape(F, D)

    mixin = _gate_mix(z, o_a, o_b, D=D, off_a=zoff["ga"], off_b=zoff["gb"], name="gate_mix")
    mix = _matmul(mixin, w_o_f, mode="nn", out_dtype=F32, name="mm_o")
    x1 = _postnorm_res(x2, mix, gt1, g_post_mix, name="postnorm_mix")

    h2 = _prenorm(x1, g_pre_ffn, sc2, sh2, name="prenorm_ffn")
    up = _matmul(h2, w_up_f, mode="nn", out_dtype=BF16, name="mm_up")
    act = _conv_gate(up, cw_full, conv_b, name="conv_gate")
    y = _matmul(act, w_down_f, mode="nn", out_dtype=F32, name="mm_down")
    loss_part, dout, dy, dgt2, dg_post_ffn = _final_loss(x1, y, tgt, gt2, g_post_ffn, name="final_loss")
    loss = lax.psum(loss_part[0, 0], ("x", "y", "c"))

    dw_down = _matmul(act, dy, mode="tn", out_dtype=BF16, name="mm_down_dw")
    dact = _matmul(dy, w_down_f, mode="nt", out_dtype=BF16, name="mm_down_dx")
    (dup, dcw, dcb), (got_down,) = _conv_gate_bwd(up, dact, cw_full, conv_b, name="conv_gate_bwd",
                                                  comm=[("scatter", dw_down.reshape(N_DEV, F // N_DEV, D))])
    dw_up = _matmul(h2, dup, mode="tn", out_dtype=BF16, name="mm_up_dw")
    dh2 = _matmul(dup, w_up_f, mode="nt", out_dtype=F32, name="mm_up_dx")
    dx1, dg_pre_ffn, dsc2, dsh2 = _prenorm_bwd(x1, dh2, dout, g_pre_ffn, sc2, name="prenorm_ffn_bwd", out_dtype=F32)

    dmix, dgt1, dg_post_mix = _postnorm_bwd(dx1, mix, gt1, g_post_mix, name="postnorm_mix_bwd")
    dw_o = _matmul(mixin, dmix, mode="tn", out_dtype=BF16, name="mm_o_dw")
    dmixin = _matmul(dmix, w_o_f, mode="nt", out_dtype=BF16, name="mm_o_dx")
    dga, dgb, do_a, do_b = _gate_mix_bwd(dmixin, z, o_a, o_b, D=D, off_a=zoff["ga"], off_b=zoff["gb"],
                                         name="gate_mix_bwd")
    dcw_parts = jnp.pad(_shard_cols(dcw), ((0, 0), (0, 16 - CONV_WIDTH), (0, 0)))
    (dqs_h, dks_h, dvs_h, dbias, dsink), (got_o, got_cw) = _swa_bwd(
        qs_h, ks_h, vs_h, bias_full, sink_rows, heads_first(do_b, H_SWA), name="swa_bwd",
        comm=[("scatter", dw_o.reshape(N_DEV, D // N_DEV, D)), ("scatter", dcw_parts)])
    drel_t = _matmul(dbias.reshape(H_SWA, BLOCK * 2 * BLOCK), onehot, mode="nn", out_dtype=F32, name="bias_grad",
                     tk=4096, precision=HIGHEST)
    d_rel_bias = jnp.transpose(drel_t[:, :REL_BUCKETS])
    d_sinks = dsink[:, :, 0].reshape(1, H_SWA)

    (dQ, delta), (got_up,) = _flash_bwd_dq(Qr, Kc, kv, do_a, o_a, lse, heads=H_MLA, name="mla_bwd_dq",
                                           comm=[("scatter", _shard_cols(dw_up))])
    (dK, dV), _ = _flash_bwd_dkv(Qr, Kc, kv, do_a, lse, delta, heads=H_MLA, name="mla_bwd_dkv")
    dq_raw = _rope(dQ, tab_q, heads=H_MLA, width=MLA_QK_PAD, transpose=True, name="rope_q_bwd")
    dkv, dkr = _assemble_k_bwd(dK, dV, tab_k, heads=H_MLA, name="assemble_k_bwd")
    dcqn = _matmul(dq_raw, w_uq_p, mode="nt", out_dtype=F32, name="mm_uq_dx")
    dw_uq_p = _matmul(cqn, dq_raw, mode="tn", out_dtype=BF16, name="mm_uq_dw")
    dckvn = _matmul(dkv, w_ukv_f, mode="nt", out_dtype=F32, name="mm_ukv_dx")
    dw_ukv = _matmul(ckvn, dkv, mode="tn", out_dtype=BF16, name="mm_ukv_dw")
    dw_uq = dw_uq_p.reshape(Q_RANK, H_MLA, MLA_QK_PAD)[:, :, :MLA_QK].reshape(Q_RANK, H_MLA * MLA_QK)
    dcq, dg_q = _prenorm_bwd(z, dcqn, None, g_q_lat, None, name="norm_cq_bwd", out_dtype=BF16,
                             off=zoff["cq"], width=Q_RANK)
    dckv, dg_kv = _prenorm_bwd(z, dckvn, None, g_kv_lat, None, name="norm_ckv_bwd", out_dtype=BF16,
                               off=zoff["ckv"], width=KV_RANK)
    dz = _assemble_dz({"ga": dga, "gb": dgb, "qs": heads_last(dqs_h), "cq": dcq, "ckv": dckv,
                       "ks": heads_last(dks_h), "vs": heads_last(dvs_h), "kr": dkr}, lay, S)
    dw_in_p, (got_uq, got_ukv) = _matmul(h1, dz, mode="tn", out_dtype=BF16, name="mm_in_dw",
                                         comm=[("scatter", _shard_cols(dw_uq)), ("scatter", _shard_cols(dw_ukv))])
    dh1, (got_in,) = _matmul(dz, w_in_p, mode="nt", out_dtype=F32, name="mm_in_dx",
                             comm=[("scatter", _shard_cols(_unpermute_w_in(dw_in_p, lay)))])
    grad_x, dg_pre_mix, dsc1, dsh1 = _prenorm_bwd(x2, dh1, dx1, g_pre_mix, sc1, name="prenorm_mix_bwd",
                                                  out_dtype=F32)
    dmod = jnp.concatenate([dsh1, dsc1, dgt1, dsh2, dsc2, dgt2], axis=1)

    small_names = ["b_ada", "g_pre_mix", "g_post_mix", "g_q_lat", "g_kv_lat", "rel_bias", "sinks", "g_pre_ffn",
                   "g_post_ffn", "conv_b"]
    small_w = [b_ada, g_pre_mix, g_post_mix, g_q_lat, g_kv_lat, rel_bias, sinks, g_pre_ffn, g_post_ffn, conv_b]
    small_m = [m_b_ada, m_g_pre_mix, m_g_post_mix, m_g_q_lat, m_g_kv_lat, m_rel_bias, m_sinks, m_g_pre_ffn,
               m_g_post_ffn, m_conv_b]
    small_v = [v_b_ada, v_g_pre_mix, v_g_post_mix, v_g_q_lat, v_g_kv_lat, v_rel_bias, v_sinks, v_g_pre_ffn,
               v_g_post_ffn, v_conv_b]
    small_g = [dmod, dg_pre_mix, dg_post_mix, dg_q, dg_kv, d_rel_bias, d_sinks, dg_pre_ffn, dg_post_ffn, dcb]
    n_small = sum(int(np.prod(w.shape)) for w in small_w)
    rows = _round_up(-(-n_small // LANES), 16)
    parts_small = _all_gather(_pack(small_g, rows), name="ag_small_grads", in_vmem=True)
    sg, sd, sm, sv = _adamw(_pack(small_w, rows), _pack(small_m, rows), _pack(small_v, rows), parts_small,
                            name="adamw_small")

    def unpack(packed):
        flat, out, at = packed.reshape(-1), {}, 0
        for nm, w in zip(small_names, small_w):
            n = int(np.prod(w.shape))
            out[nm] = flat[at:at + n].reshape(w.shape)
            at += n
        return out

    small_out = [unpack(t) for t in (sg, sd, sm, sv)]

    dmod_all = parts_small.reshape(N_DEV, rows * LANES)[:, :6 * D]
    dmod_cols = lax.dynamic_slice_in_dim(dmod_all, me * ada_n, ada_n, axis=1)
    ada_out = _adamw_ada(w_ada[0], m_w_ada[0], v_w_ada[0], jnp.transpose(c_act), dmod_cols, name="adamw_w_ada")

    def owner_update(got, w, m, v, name):
        shp = w.shape
        w2, m2, v2 = (t.reshape(shp[-2], shp[-1]) for t in (w, m, v))
        return [t.reshape(shp) for t in _adamw(w2, m2, v2, got, name="adamw_" + name)]

    def pad_rows(t):
        return jnp.pad(t[0], ((0, 16 - CONV_WIDTH), (0, 0)))

    big = {
        "w_in": owner_update(got_in, w_in, m_w_in, v_w_in, "w_in"),
        "w_uq": owner_update(got_uq, w_uq, m_w_uq, v_w_uq, "w_uq"),
        "w_ukv": owner_update(got_ukv, w_ukv, m_w_ukv, v_w_ukv, "w_ukv"),
        "w_o": owner_update(got_o, w_o, m_w_o, v_w_o, "w_o"),
        "w_up": owner_update(got_up, w_up, m_w_up, v_w_up, "w_up"),
        "w_down": owner_update(got_down, w_down, m_w_down, v_w_down, "w_down"),
    }
    cw_upd = _adamw(pad_rows(conv_w), pad_rows(m_conv_w), pad_rows(v_conv_w), got_cw, name="adamw_conv_w")
    big["conv_w"] = [t[:CONV_WIDTH].reshape(conv_w.shape) for t in cw_upd]
    big["w_ada"] = [t.reshape(w_ada.shape) for t in ada_out]

    order = ["w_ada", "b_ada", "g_pre_mix", "g_post_mix", "w_in", "g_q_lat", "w_uq", "g_kv_lat", "w_ukv", "rel_bias",
             "sinks", "w_o", "g_pre_ffn", "g_post_ffn", "w_up", "conv_w", "conv_b", "w_down"]
    outs = [loss, grad_x.reshape(x.shape)]
    for kind in range(4):
        for nm in order:
            outs.append(big[nm][kind] if nm in big else small_out[kind][nm])
    return tuple(outs)
```

```python
import functools
import itertools
import math

import numpy as np

import jax
import jax.numpy as jnp
from jax import lax
from jax.experimental import pallas as pl
from jax.experimental.pallas import tpu as pltpu

F32 = jnp.float32
BF16 = jnp.bfloat16

N_DEV = 8
MLA_NOPE = 128
MLA_ROPE = 64
MLA_V = 128
MLA_QK = MLA_NOPE + MLA_ROPE
MLA_QK_PAD = 256
ROPE_HALF = MLA_ROPE // 2
ROPE_THETA = 10000.0
SWA_HD = 64
SWA_KVH = 4
WINDOW = 128
BLOCK = 128
REL_BUCKETS = 32
REL_MAX_DIST = 128
CONV_WIDTH = 3
EPS = 1e-6
NEG = -1e30
ADAM_LR = 0.001
ADAM_B1 = 0.9
ADAM_B2 = 0.999
ADAM_EPS = 1e-08
ADAM_WD = 0.01
ADAM_STEP = 10
LANES = 128
HALO = 16
MESH = pl.DeviceIdType.MESH
HIGHEST = lax.Precision.HIGHEST

NN = (((1,), (0,)), ((), ()))
NT = (((1,), (1,)), ((), ()))
TN = (((0,), (0,)), ((), ()))


def _tile(n, pref, align=LANES):
    if n <= pref:
        return n
    t = (pref // align) * align
    while t >= align:
        if n % t == 0:
            return t
        t -= align
    return n


def _round_up(n, m):
    return (n + m - 1) // m * m


def _params(*sem):
    return pltpu.CompilerParams(dimension_semantics=sem)


def _sigmoid(x):
    return 1.0 / (1.0 + jnp.exp(-x))


def _my_place():
    return lax.axis_index("x"), lax.axis_index("y"), lax.axis_index("c")


def _all_gather(x, *, name, in_vmem):
    space = pltpu.VMEM if in_vmem else pl.ANY

    def body(x_ref, out_ref, send_sems, recv_sems, local_sem):
        x_, y_, c_ = _my_place()
        me, sibling = (x_, y_, c_), (x_, y_, 1 - c_)
        chips = [(1 - x_, y_), (x_, 1 - y_), (1 - x_, 1 - y_)]

        def slot(px, py, pc):
            return out_ref.at[4 * px + 2 * py + pc]

        def copy(k, block, to, src=None):
            return pltpu.make_async_remote_copy(
                src_ref=slot(*block) if src is None else src,
                dst_ref=slot(*block),
                send_sem=send_sems.at[k],
                recv_sem=recv_sems.at[k],
                device_id=to,
                device_id_type=MESH,
            )

        mine = pltpu.make_async_copy(x_ref, slot(*me), local_sem)
        mine.start()
        first = [copy(0, me, sibling, src=x_ref)]
        first += [copy(1 + j, me, (*chip, c_), src=x_ref) for j, chip in enumerate(chips)]
        for cp in first:
            cp.start()
        passed = [copy(4 + j, (*chip, c_), sibling) for j, chip in enumerate(chips)]
        for j, chip in enumerate(chips):
            copy(1 + j, (*chip, c_), me).wait_recv()
            passed[j].start()
        copy(0, sibling, me).wait_recv()
        for j, chip in enumerate(chips):
            copy(4 + j, (*chip, 1 - c_), me).wait_recv()
        for cp in first + passed:
            cp.wait_send()
        mine.wait()

    return pl.pallas_call(
        body,
        name=name,
        out_shape=jax.ShapeDtypeStruct((N_DEV,) + x.shape, x.dtype),
        in_specs=[pl.BlockSpec(memory_space=space)],
        out_specs=pl.BlockSpec(memory_space=space),
        scratch_shapes=[
            pltpu.SemaphoreType.DMA((7,)),
            pltpu.SemaphoreType.DMA((7,)),
            pltpu.SemaphoreType.DMA,
        ],
    )(x)


def _carried_copies(kind, x_ref, out_ref, send_sems, recv_sems, local_sems, t):
    x_, y_, c_ = _my_place()
    me = 4 * x_ + 2 * y_ + c_
    remote = []
    for r in range(1, N_DEV):
        px, py, pc = x_ ^ ((r >> 2) & 1), y_ ^ ((r >> 1) & 1), c_ ^ (r & 1)
        src = x_ref if kind == "gather" else x_ref.at[4 * px + 2 * py + pc]
        remote.append(pltpu.make_async_remote_copy(
            src_ref=src, dst_ref=out_ref.at[me],
            send_sem=send_sems.at[7 * t + r - 1], recv_sem=recv_sems.at[7 * t + r - 1],
            device_id=(px, py, pc), device_id_type=MESH))
    own = x_ref if kind == "gather" else x_ref.at[me]
    return remote, pltpu.make_async_copy(own, out_ref.at[me], local_sems.at[t])


def _call(body, *, name, grid, in_specs, out_specs, out_shape, args, scratch_shapes=(), sem=(), comm=(), prefetch=()):
    n_pf = len(prefetch)

    def launch(fn, ins, outs, shapes, scratch, semantics, operands):
        spec = pltpu.PrefetchScalarGridSpec(num_scalar_prefetch=n_pf, grid=grid, in_specs=ins, out_specs=outs,
                                            scratch_shapes=scratch)
        return pl.pallas_call(fn, name=name, grid_spec=spec, out_shape=shapes,
                              compiler_params=_params(*semantics))(*prefetch, *operands)

    if not comm:
        return list(launch(body, list(in_specs), list(out_specs), list(out_shape), list(scratch_shapes), sem, args)), []
    n_in, n_out, n_c, n_s = len(in_specs), len(out_specs), len(comm), len(scratch_shapes)
    kinds = [kind for kind, _ in comm]
    hbm = pl.BlockSpec(memory_space=pl.ANY)

    def wrapped(*refs):
        tables, refs = refs[:n_pf], refs[n_pf:]
        ins, cin = refs[:n_in], refs[n_in:n_in + n_c]
        at = n_in + n_c
        outs, cout = refs[at:at + n_out], refs[at + n_out:at + n_out + n_c]
        scr = refs[at + n_out + n_c:at + n_out + n_c + n_s]
        send, recv, local = refs[-3:]
        ids = [pl.program_id(a) for a in range(len(grid))]
        first = functools.reduce(jnp.logical_and, [i == 0 for i in ids])
        last = functools.reduce(jnp.logical_and, [i == g - 1 for i, g in zip(ids, grid)])

        def copies():
            return [_carried_copies(kinds[t], cin[t], cout[t], send, recv, local, t) for t in range(n_c)]

        @pl.when(first)
        def _():
            for remote, own in copies():
                own.start()
                for cp in remote:
                    cp.start()

        body(*tables, *ins, *outs, *scr)

        @pl.when(last)
        def _():
            for remote, own in copies():
                for cp in remote:
                    cp.wait_recv()
                for cp in remote:
                    cp.wait_send()
                own.wait()

    c_shapes = [jax.ShapeDtypeStruct(((N_DEV,) + a.shape) if kind == "gather" else a.shape, a.dtype)
                for kind, a in comm]
    sems = [pltpu.SemaphoreType.DMA((7 * n_c,)), pltpu.SemaphoreType.DMA((7 * n_c,)), pltpu.SemaphoreType.DMA((n_c,))]
    res = launch(wrapped, list(in_specs) + [hbm] * n_c, list(out_specs) + [hbm] * n_c, list(out_shape) + c_shapes,
                 list(scratch_shapes) + sems, ["arbitrary"] * len(grid), (*args, *[a for _, a in comm]))
    return list(res[:n_out]), list(res[n_out:])


def _matmul(a, b, *, mode, out_dtype, name, tm=1024, tn=1024, tk=2816, precision=None, comm=()):
    if mode == "nn":
        (M, K), (K2, N) = a.shape, b.shape
    elif mode == "nt":
        (M, K), (N, K2) = a.shape, b.shape
    else:
        (K, M), (K2, N) = a.shape, b.shape
    assert K == K2, (a.shape, b.shape, mode)
    tm, tn, tk = _tile(M, tm, LANES if mode == "tn" else 16), _tile(N, tn), _tile(K, tk)
    nk = K // tk
    if mode == "tn":
        a_spec = pl.BlockSpec((tk, tm), lambda i, j, k: (k, i))
    else:
        a_spec = pl.BlockSpec((tm, tk), lambda i, j, k: (i, k))
    if mode == "nt":
        b_spec = pl.BlockSpec((tn, tk), lambda i, j, k: (j, k))
    else:
        b_spec = pl.BlockSpec((tk, tn), lambda i, j, k: (k, j))
    dn = {"nn": NN, "nt": NT, "tn": TN}[mode]

    def product(a_ref, b_ref):
        return lax.dot_general(a_ref[...], b_ref[...], dn, preferred_element_type=F32, precision=precision)

    def body_one(a_ref, b_ref, o_ref):
        o_ref[...] = product(a_ref, b_ref).astype(o_ref.dtype)

    def body_acc(a_ref, b_ref, o_ref, acc_ref):
        k = pl.program_id(2)

        @pl.when(k == 0)
        def _():
            acc_ref[...] = product(a_ref, b_ref)

        @pl.when(k > 0)
        def _():
            acc_ref[...] += product(a_ref, b_ref)

        @pl.when(k == nk - 1)
        def _():
            o_ref[...] = acc_ref[...].astype(o_ref.dtype)

    outs, moved = _call(
        body_one if nk == 1 else body_acc,
        name=name,
        grid=(M // tm, N // tn, nk),
        in_specs=[a_spec, b_spec],
        out_specs=[pl.BlockSpec((tm, tn), lambda i, j, k: (i, j))],
        out_shape=[jax.ShapeDtypeStruct((M, N), out_dtype)],
        scratch_shapes=[] if nk == 1 else [pltpu.VMEM((tm, tn), F32)],
        sem=("parallel", "parallel", "arbitrary"),
        args=(a, b),
        comm=comm,
    )
    return (outs[0], moved) if comm else outs[0]


def _rstd(xf):
    return lax.rsqrt(jnp.mean(xf * xf, axis=-1, keepdims=True) + EPS)


def _col_view(width, off):
    assert off % width == 0
    return off // width


def _prenorm(x, g, sc, sh, *, name, off=0, width=None):
    S = x.shape[0]
    W = x.shape[1] if width is None else width
    cb = _col_view(W, off)
    tr = _tile(S, 512, 16)
    mod = sc is not None
    vec = pl.BlockSpec((1, W), lambda i: (0, 0))

    def body(*refs):
        if mod:
            x_ref, g_ref, sc_ref, sh_ref, o_ref = refs
        else:
            x_ref, g_ref, o_ref = refs
        xf = x_ref[...].astype(F32)
        y = xf * _rstd(xf) * g_ref[...]
        if mod:
            y = y * (1.0 + sc_ref[...]) + sh_ref[...]
        o_ref[...] = y.astype(o_ref.dtype)

    args = (x, g, sc, sh) if mod else (x, g)
    return pl.pallas_call(
        body,
        name=name,
        grid=(S // tr,),
        in_specs=[pl.BlockSpec((tr, W), lambda i: (i, cb))] + [vec] * (len(args) - 1),
        out_specs=pl.BlockSpec((tr, W), lambda i: (i, 0)),
        out_shape=jax.ShapeDtypeStruct((S, W), BF16),
        compiler_params=_params("parallel"),
    )(*args)


def _prenorm_bwd(x, dh, dres, g, sc, *, name, out_dtype, off=0, width=None):
    S = x.shape[0]
    W = x.shape[1] if width is None else width
    cb = _col_view(W, off)
    tr = _tile(S, 256, 16)
    mod = sc is not None
    res = dres is not None
    vec = pl.BlockSpec((1, W), lambda i: (0, 0))
    row = pl.BlockSpec((tr, W), lambda i: (i, 0))

    def body(*refs):
        it = iter(refs)
        x_ref, dh_ref = next(it), next(it)
        dres_ref = next(it) if res else None
        g_ref = next(it)
        sc_ref = next(it) if mod else None
        dx_ref, dg_ref = next(it), next(it)
        dsc_ref, dsh_ref = (next(it), next(it)) if mod else (None, None)
        i = pl.program_id(0)

        @pl.when(i == 0)
        def _():
            dg_ref[...] = jnp.zeros_like(dg_ref)
            if mod:
                dsc_ref[...] = jnp.zeros_like(dsc_ref)
                dsh_ref[...] = jnp.zeros_like(dsh_ref)

        xf = x_ref[...].astype(F32)
        r = _rstd(xf)
        xn = xf * r
        dhf = dh_ref[...].astype(F32)
        gv = g_ref[...]
        if mod:
            one_sc = 1.0 + sc_ref[...]
            dsh_ref[...] += jnp.sum(dhf, axis=0, keepdims=True)
            dsc_ref[...] += jnp.sum(dhf * (xn * gv), axis=0, keepdims=True)
            dg_ref[...] += jnp.sum(dhf * xn * one_sc, axis=0, keepdims=True)
            dxn = dhf * (gv * one_sc)
        else:
            dg_ref[...] += jnp.sum(dhf * xn, axis=0, keepdims=True)
            dxn = dhf * gv
        dx = r * (dxn - xn * jnp.mean(dxn * xn, axis=-1, keepdims=True))
        if res:
            dx = dx + dres_ref[...]
        dx_ref[...] = dx.astype(dx_ref.dtype)

    args = [x, dh] + ([dres] if res else []) + [g] + ([sc] if mod else [])
    in_specs = [pl.BlockSpec((tr, W), lambda i: (i, cb)), row] + ([row] if res else []) + [vec] + ([vec] if mod else [])
    n_vec = 3 if mod else 1
    outs = pl.pallas_call(
        body,
        name=name,
        grid=(S // tr,),
        in_specs=in_specs,
        out_specs=[row] + [vec] * n_vec,
        out_shape=[jax.ShapeDtypeStruct((S, W), out_dtype)] + [jax.ShapeDtypeStruct((1, W), F32)] * n_vec,
        compiler_params=_params("arbitrary"),
    )(*args)
    return outs


def _postnorm_res(x, y, gt, g, *, name):
    S, D = x.shape
    tr = _tile(S, 512, 8)
    row = pl.BlockSpec((tr, D), lambda i: (i, 0))
    vec = pl.BlockSpec((1, D), lambda i: (0, 0))

    def body(x_ref, y_ref, gt_ref, g_ref, o_ref):
        yf = y_ref[...]
        o_ref[...] = x_ref[...] + gt_ref[...] * (yf * _rstd(yf) * g_ref[...])

    return pl.pallas_call(
        body,
        name=name,
        grid=(S // tr,),
        in_specs=[row, row, vec, vec],
        out_specs=row,
        out_shape=jax.ShapeDtypeStruct((S, D), F32),
        compiler_params=_params("parallel"),
    )(x, y, gt, g)


def _postnorm_bwd(dx1, y, gt, g, *, name):
    S, D = y.shape
    tr = _tile(S, 256, 16)
    row = pl.BlockSpec((tr, D), lambda i: (i, 0))
    vec = pl.BlockSpec((1, D), lambda i: (0, 0))

    def body(dx_ref, y_ref, gt_ref, g_ref, dy_ref, dgt_ref, dg_ref):
        @pl.when(pl.program_id(0) == 0)
        def _():
            dgt_ref[...] = jnp.zeros_like(dgt_ref)
            dg_ref[...] = jnp.zeros_like(dg_ref)

        yf = y_ref[...]
        r = _rstd(yf)
        yn = yf * r
        d = dx_ref[...]
        gtv, gv = gt_ref[...], g_ref[...]
        dgt_ref[...] += jnp.sum(d * (yn * gv), axis=0, keepdims=True)
        dg_ref[...] += jnp.sum(d * gtv * yn, axis=0, keepdims=True)
        dyn = d * (gtv * gv)
        dy_ref[...] = (r * (dyn - yn * jnp.mean(dyn * yn, axis=-1, keepdims=True))).astype(dy_ref.dtype)

    return pl.pallas_call(
        body,
        name=name,
        grid=(S // tr,),
        in_specs=[row, row, vec, vec],
        out_specs=[row, vec, vec],
        out_shape=[jax.ShapeDtypeStruct((S, D), BF16), jax.ShapeDtypeStruct((1, D), F32),
                   jax.ShapeDtypeStruct((1, D), F32)],
        compiler_params=_params("arbitrary"),
    )(dx1, y, gt, g)


def _final_loss(x1, y, target, gt, g, *, name):
    S, D = y.shape
    tr = _tile(S, 256, 16)
    row = pl.BlockSpec((tr, D), lambda i: (i, 0))
    vec = pl.BlockSpec((1, D), lambda i: (0, 0))
    one = pl.BlockSpec((1, LANES), lambda i: (0, 0))

    def body(x_ref, y_ref, t_ref, gt_ref, g_ref, loss_ref, dout_ref, dy_ref, dgt_ref, dg_ref):
        @pl.when(pl.program_id(0) == 0)
        def _():
            loss_ref[...] = jnp.zeros_like(loss_ref)
            dgt_ref[...] = jnp.zeros_like(dgt_ref)
            dg_ref[...] = jnp.zeros_like(dg_ref)

        yf = y_ref[...]
        r = _rstd(yf)
        yn = yf * r
        gtv, gv = gt_ref[...], g_ref[...]
        out = x_ref[...] + gtv * (yn * gv)
        diff = out - t_ref[...]
        per_tok = jnp.mean(diff * diff, axis=-1, keepdims=True)
        loss_ref[...] += 0.5 * jnp.sum(per_tok, axis=0, keepdims=True)
        d = diff / D
        dout_ref[...] = d
        dgt_ref[...] += jnp.sum(d * (yn * gv), axis=0, keepdims=True)
        dg_ref[...] += jnp.sum(d * gtv * yn, axis=0, keepdims=True)
        dyn = d * (gtv * gv)
        dy_ref[...] = (r * (dyn - yn * jnp.mean(dyn * yn, axis=-1, keepdims=True))).astype(dy_ref.dtype)

    return pl.pallas_call(
        body,
        name=name,
        grid=(S // tr,),
        in_specs=[row, row, row, vec, vec],
        out_specs=[one, row, row, vec, vec],
        out_shape=[jax.ShapeDtypeStruct((1, LANES), F32), jax.ShapeDtypeStruct((S, D), F32),
                   jax.ShapeDtypeStruct((S, D), BF16), jax.ShapeDtypeStruct((1, D), F32),
                   jax.ShapeDtypeStruct((1, D), F32)],
        compiler_params=_params("arbitrary"),
    )(x1, y, target, gt, g)


def _ada_fwd(c_all, w_local, b_cols, *, name):
    B, D = c_all.shape
    N = w_local.shape[1]
    tn = _tile(N, 512)

    def body(c_ref, w_ref, b_ref, ca_ref, mod_ref):
        cv = c_ref[...]
        ca = cv * _sigmoid(cv)
        ca_ref[...] = ca
        mod_ref[...] = jnp.dot(ca, w_ref[...], preferred_element_type=F32, precision=HIGHEST) + b_ref[...]

    return pl.pallas_call(
        body,
        name=name,
        grid=(N // tn,),
        in_specs=[pl.BlockSpec((B, D), lambda j: (0, 0)), pl.BlockSpec((D, tn), lambda j: (0, j)),
                  pl.BlockSpec((1, tn), lambda j: (0, j))],
        out_specs=[pl.BlockSpec((B, D), lambda j: (0, 0)), pl.BlockSpec((B, tn), lambda j: (0, j))],
        out_shape=[jax.ShapeDtypeStruct((B, D), F32), jax.ShapeDtypeStruct((B, N), F32)],
        compiler_params=_params("arbitrary"),
    )(c_all, w_local, b_cols)


def _rope_tables(S, width, lane_off):
    pos = jnp.arange(S, dtype=F32)
    inv = ROPE_THETA ** (-jnp.arange(0, MLA_ROPE, 2, dtype=F32) / MLA_ROPE)
    ang = pos[:, None] * inv[None, :]
    ang = jnp.concatenate([ang, ang], axis=-1)
    cos, sin = jnp.cos(ang), jnp.sin(ang)
    first = (jnp.arange(MLA_ROPE) < ROPE_HALF)[None, :]
    sa = jnp.where(first, -sin, 0.0)
    sb = jnp.where(first, 0.0, sin)

    def place(t, fill):
        return jnp.pad(t, ((0, 0), (lane_off, width - lane_off - MLA_ROPE)), constant_values=fill)

    return place(cos, 1.0), place(sa, 0.0), place(sb, 0.0)


def _rope_apply(x, cos, sa, sb, width, transpose):
    if transpose:
        return x * cos + pltpu.roll(x * sa, ROPE_HALF, 1) + pltpu.roll(x * sb, width - ROPE_HALF, 1)
    return x * cos + pltpu.roll(x, width - ROPE_HALF, 1) * sa + pltpu.roll(x, ROPE_HALF, 1) * sb


def _rope(x, tables, *, heads, width, transpose, name, off=0, scale=1.0):
    S = x.shape[0]
    cb = _col_view(width, off)
    tr = _tile(S, 512, 16)
    tab = pl.BlockSpec((tr, width), lambda i, h: (i, 0))

    def body(x_ref, c_ref, sa_ref, sb_ref, o_ref):
        y = _rope_apply(x_ref[...].astype(F32), c_ref[...], sa_ref[...], sb_ref[...], width, transpose)
        o_ref[...] = (y if scale == 1.0 else y * scale).astype(o_ref.dtype)

    return pl.pallas_call(
        body,
        name=name,
        grid=(S // tr, heads),
        in_specs=[pl.BlockSpec((tr, width), lambda i, h: (i, cb + h)), tab, tab, tab],
        out_specs=pl.BlockSpec((tr, width), lambda i, h: (i, h)),
        out_shape=jax.ShapeDtypeStruct((S, heads * width), BF16),
        compiler_params=_params("parallel", "parallel"),
    )(x, *tables)


def _assemble_k(kv, krr, *, heads, name):
    S = kv.shape[0]
    tr = _tile(S, 512, 16)

    def body(kn_ref, kr_ref, o_ref):
        o_ref[:, :MLA_NOPE] = kn_ref[...]
        o_ref[:, MLA_NOPE:] = kr_ref[...]

    return pl.pallas_call(
        body,
        name=name,
        grid=(S // tr, heads),
        in_specs=[pl.BlockSpec((tr, MLA_NOPE), lambda i, h: (i, 2 * h)),
                  pl.BlockSpec((tr, LANES), lambda i, h: (i, 0))],
        out_specs=pl.BlockSpec((tr, MLA_QK_PAD), lambda i, h: (i, h)),
        out_shape=jax.ShapeDtypeStruct((S, heads * MLA_QK_PAD), BF16),
        compiler_params=_params("parallel", "parallel"),
    )(kv, krr)


def _assemble_k_bwd(dK, dV, tables, *, heads, name):
    S = dK.shape[0]
    tr = _tile(S, 512, 16)
    tab = pl.BlockSpec((tr, LANES), lambda i, h: (i, 0))

    def body(dk_ref, dv_ref, c_ref, sa_ref, sb_ref, dkv_ref, dkr_ref, acc_ref):
        h = pl.program_id(1)

        @pl.when(h == 0)
        def _():
            acc_ref[...] = jnp.zeros_like(acc_ref)

        dkv_ref[:, :MLA_NOPE] = dk_ref[:, :MLA_NOPE]
        dkv_ref[:, MLA_NOPE:] = dv_ref[...]
        acc_ref[...] += dk_ref[:, MLA_NOPE:].astype(F32)

        @pl.when(h == heads - 1)
        def _():
            dkr_ref[...] = _rope_apply(acc_ref[...], c_ref[...], sa_ref[...], sb_ref[...], LANES,
                                       True).astype(dkr_ref.dtype)

    return pl.pallas_call(
        body,
        name=name,
        grid=(S // tr, heads),
        in_specs=[pl.BlockSpec((tr, MLA_QK_PAD), lambda i, h: (i, h)),
                  pl.BlockSpec((tr, MLA_V), lambda i, h: (i, h)), tab, tab, tab],
        out_specs=[pl.BlockSpec((tr, MLA_QK_PAD), lambda i, h: (i, h)),
                   pl.BlockSpec((tr, LANES), lambda i, h: (i, 0))],
        out_shape=[jax.ShapeDtypeStruct((S, heads * MLA_QK_PAD), BF16), jax.ShapeDtypeStruct((S, LANES), BF16)],
        scratch_shapes=[pltpu.VMEM((tr, LANES), F32)],
        compiler_params=_params("parallel", "arbitrary"),
    )(dK, dV, *tables)


MLA_SCALE = MLA_QK ** -0.5
LOG2E = math.log2(math.e)
LN2 = math.log(2.0)
MLA_Q_PRESCALE = MLA_SCALE * LOG2E


def _lane_tile(v, n):
    return v if n == LANES else jnp.tile(v, (1, n // LANES))


def _causal_mask(s):
    rows = lax.broadcasted_iota(jnp.int32, s.shape, 0)
    cols = lax.broadcasted_iota(jnp.int32, s.shape, 1)
    return jnp.where(cols <= rows, s, NEG)


def _tri_blocks(nb, q_major):
    if q_major:
        pairs = [(q, k) for q in range(nb) for k in range(q + 1)]
    else:
        pairs = [(q, k) for k in range(nb) for q in range(k, nb)]
    return (jnp.asarray(np.array([p[0] for p in pairs], np.int32)),
            jnp.asarray(np.array([p[1] for p in pairs], np.int32)))


HEAD_PAIR = 2


def _flash_fwd(Q, K, KV, *, heads, name, comm=()):
    S = Q.shape[0]
    t = _tile(S, 512)
    nb = S // t
    qt, kt = _tri_blocks(nb, True)
    qw, vw = HEAD_PAIR * MLA_QK_PAD, HEAD_PAIR * MLA_V

    def body(qt_ref, kt_ref, q_ref, k_ref, v0_ref, v1_ref, o_ref, lse_ref, m_scr, l_scr, acc_scr):
        step_id = pl.program_id(1)
        qi, ki = qt_ref[step_id], kt_ref[step_id]

        @pl.when(ki == 0)
        def _():
            m_scr[...] = jnp.full_like(m_scr, NEG)
            l_scr[...] = jnp.zeros_like(l_scr)
            acc_scr[...] = jnp.zeros_like(acc_scr)

        def step(diagonal):
            for h, v_ref in enumerate((v0_ref, v1_ref)):
                cols = slice(h * MLA_QK_PAD, (h + 1) * MLA_QK_PAD)
                s = lax.dot_general(q_ref[:, cols], k_ref[:, cols], NT, preferred_element_type=F32)
                if diagonal:
                    s = _causal_mask(s)
                m_prev = m_scr[h]
                m_new = jnp.maximum(m_prev, jnp.max(s, axis=1, keepdims=True))
                alpha = jnp.exp2(m_prev - m_new)
                p = jnp.exp2(s - _lane_tile(m_new, t))
                l_new = alpha * l_scr[h] + jnp.sum(p, axis=1, keepdims=True)
                acc = alpha * acc_scr[h] + jnp.dot(p.astype(BF16), v_ref[...], preferred_element_type=F32)
                if diagonal:
                    o_ref[:, h * MLA_V:(h + 1) * MLA_V] = (acc / l_new).astype(o_ref.dtype)
                    lse_ref[h] = m_new + jnp.log(l_new) * LOG2E
                else:
                    l_scr[h], acc_scr[h], m_scr[h] = l_new, acc, m_new

        pl.when(ki < qi)(lambda: step(False))
        pl.when(ki == qi)(lambda: step(True))

    def vspec(h):
        return pl.BlockSpec((t, MLA_V), lambda hp, s, qt, kt: (kt[s], 2 * (HEAD_PAIR * hp + h) + 1))

    return _call(
        body,
        name=name,
        grid=(heads // HEAD_PAIR, int(qt.shape[0])),
        in_specs=[pl.BlockSpec((t, qw), lambda hp, s, qt, kt: (qt[s], hp)),
                  pl.BlockSpec((t, qw), lambda hp, s, qt, kt: (kt[s], hp)), vspec(0), vspec(1)],
        out_specs=[pl.BlockSpec((t, vw), lambda hp, s, qt, kt: (qt[s], hp)),
                   pl.BlockSpec((HEAD_PAIR, t, LANES), lambda hp, s, qt, kt: (hp, qt[s], 0))],
        out_shape=[jax.ShapeDtypeStruct((S, heads * MLA_V), BF16),
                   jax.ShapeDtypeStruct((heads, S, LANES), F32)],
        scratch_shapes=[pltpu.VMEM((HEAD_PAIR, t, LANES), F32), pltpu.VMEM((HEAD_PAIR, t, LANES), F32),
                        pltpu.VMEM((HEAD_PAIR, t, MLA_V), F32)],
        sem=("parallel", "arbitrary"),
        args=(Q, K, KV, KV),
        comm=comm,
        prefetch=(qt, kt),
    )


def _flash_bwd(Q, K, KV, dO, O, lse, *, heads, name, comm=()):
    S = Q.shape[0]
    t = _tile(S, 512)
    nb = S // t
    qt, kt = _tri_blocks(nb, False)
    n_steps = int(qt.shape[0])
    qw, vw = HEAD_PAIR * MLA_QK_PAD, HEAD_PAIR * MLA_V

    def body(qt_ref, kt_ref, q_ref, k_ref, v0_ref, v1_ref, do_ref, o_ref, lse_ref,
             dq_ref, dk_ref, dv_ref, dq_scr, dk_scr, dv_scr, delta_scr):
        step_id = pl.program_id(1)
        qi, ki = qt_ref[step_id], kt_ref[step_id]

        @pl.when(step_id == 0)
        def _():
            dq_scr[...] = jnp.zeros_like(dq_scr)

        @pl.when(ki == 0)
        def _():
            for h in range(HEAD_PAIR):
                vc = slice(h * MLA_V, (h + 1) * MLA_V)
                d = jnp.sum(do_ref[:, vc].astype(F32) * o_ref[:, vc].astype(F32), axis=1, keepdims=True)
                delta_scr[h, qi] = jnp.broadcast_to(d, (t, LANES))

        def step(diagonal):
            for h, v_ref in enumerate((v0_ref, v1_ref)):
                cols = slice(h * MLA_QK_PAD, (h + 1) * MLA_QK_PAD)
                vc = slice(h * MLA_V, (h + 1) * MLA_V)
                q, k, do = q_ref[:, cols], k_ref[:, cols], do_ref[:, vc]
                s = lax.dot_general(q, k, NT, preferred_element_type=F32)
                if diagonal:
                    s = _causal_mask(s)
                p = jnp.exp2(s - _lane_tile(lse_ref[h], t))
                dv = lax.dot_general(p.astype(BF16), do, TN, preferred_element_type=F32)
                dp = lax.dot_general(do, v_ref[...], NT, preferred_element_type=F32)
                ds = (p * (dp - _lane_tile(delta_scr[h, qi], t))).astype(BF16)
                dk = lax.dot_general(ds, q, TN, preferred_element_type=F32)
                dq_scr[qi, :, cols] += jnp.dot(ds, k, preferred_element_type=F32)
                if diagonal:
                    dk_scr[h], dv_scr[h] = dk, dv
                else:
                    dk_scr[h] += dk
                    dv_scr[h] += dv

        pl.when(qi > ki)(lambda: step(False))
        pl.when(qi == ki)(lambda: step(True))

        @pl.when(qi == nb - 1)
        def _():
            for h in range(HEAD_PAIR):
                dk_ref[:, h * MLA_QK_PAD:(h + 1) * MLA_QK_PAD] = (dk_scr[h] * LN2).astype(dk_ref.dtype)
                dv_ref[:, h * MLA_V:(h + 1) * MLA_V] = dv_scr[h].astype(dv_ref.dtype)

        @pl.when(step_id == n_steps - 1)
        def _():
            for b in range(nb):
                dq_ref[b * t:(b + 1) * t, :] = (dq_scr[b] * LN2).astype(dq_ref.dtype)

    def vspec(h):
        return pl.BlockSpec((t, MLA_V), lambda hp, s, qt, kt: (kt[s], 2 * (HEAD_PAIR * hp + h) + 1))

    qrow = lambda hp, s, qt, kt: (qt[s], hp)
    krow = lambda hp, s, qt, kt: (kt[s], hp)
    return _call(
        body,
        name=name,
        grid=(heads // HEAD_PAIR, n_steps),
        in_specs=[pl.BlockSpec((t, qw), qrow), pl.BlockSpec((t, qw), krow), vspec(0), vspec(1),
                  pl.BlockSpec((t, vw), qrow), pl.BlockSpec((t, vw), qrow),
                  pl.BlockSpec((HEAD_PAIR, t, LANES), lambda hp, s, qt, kt: (hp, qt[s], 0))],
        out_specs=[pl.BlockSpec((S, qw), lambda hp, s, qt, kt: (0, hp)),
                   pl.BlockSpec((t, qw), krow), pl.BlockSpec((t, vw), krow)],
        out_shape=[jax.ShapeDtypeStruct((S, heads * MLA_QK_PAD), BF16),
                   jax.ShapeDtypeStruct((S, heads * MLA_QK_PAD), BF16),
                   jax.ShapeDtypeStruct((S, heads * MLA_V), BF16)],
        scratch_shapes=[pltpu.VMEM((nb, t, qw), F32), pltpu.VMEM((HEAD_PAIR, t, MLA_QK_PAD), F32),
                        pltpu.VMEM((HEAD_PAIR, t, MLA_V), F32), pltpu.VMEM((HEAD_PAIR, nb, t, LANES), F32)],
        sem=("parallel", "arbitrary"),
        args=(Q, K, KV, KV, dO, O, lse),
        comm=comm,
        prefetch=(qt, kt),
    )


SWA_SCALE = SWA_HD ** -0.5


def _t5_bucket_table():
    a = np.arange(BLOCK)[:, None]
    j = np.arange(2 * BLOCK)[None, :]
    dist = BLOCK + a - j
    max_exact = REL_BUCKETS // 2
    n = np.maximum(dist, 0)
    large = max_exact + (np.log(np.maximum(n, 1).astype(np.float32) / np.float32(max_exact))
                         / np.float32(math.log(REL_MAX_DIST / max_exact))
                         * np.float32(REL_BUCKETS - max_exact)).astype(np.int32)
    large = np.minimum(large, REL_BUCKETS - 1)
    bucket = np.where(n < max_exact, n, large)
    valid = (dist >= 0) & (dist < WINDOW)
    return bucket.astype(np.int32), valid


def _swa_scores(q_ref, kp_ref, kc_ref, bias_ref, qb, G):
    q2 = q_ref[...].reshape(G * BLOCK, SWA_HD)
    kb = jnp.concatenate([kp_ref[0], kc_ref[0]], axis=0)
    s = lax.dot_general(q2, kb, NT, preferred_element_type=F32) * SWA_SCALE
    s = s + bias_ref[...].reshape(G * BLOCK, 2 * BLOCK)
    cols = lax.broadcasted_iota(jnp.int32, s.shape, 1)
    s = jnp.where((cols >= BLOCK) | (qb > 0), s, NEG)
    return q2, kb, s


def _swa_probs(s, sink):
    m = jnp.maximum(jnp.max(s, axis=1, keepdims=True), sink)
    e = jnp.exp(s - m)
    es = jnp.exp(sink - m)
    den = jnp.sum(e, axis=1, keepdims=True) + es
    return e / den, es / den


def _swa_fwd(q, k, v, bias, sink, *, name, comm=()):
    H, S, _ = q.shape
    G = H // SWA_KVH
    nb = S // BLOCK
    cur = lambda kh, qb: (kh, qb, 0)
    prev = lambda kh, qb: (kh, jnp.maximum(qb - 1, 0), 0)
    kvspec = lambda im: pl.BlockSpec((1, BLOCK, SWA_HD), im)

    def body(q_ref, kc_ref, kp_ref, vc_ref, vp_ref, bias_ref, sink_ref, o_ref):
        qb = pl.program_id(1)
        _, _, s = _swa_scores(q_ref, kp_ref, kc_ref, bias_ref, qb, G)
        p, _ = _swa_probs(s, sink_ref[0][:, :1])
        vb = jnp.concatenate([vp_ref[0], vc_ref[0]], axis=0)
        o = jnp.dot(p.astype(BF16), vb, preferred_element_type=F32)
        o_ref[...] = o.reshape(G, BLOCK, SWA_HD).astype(o_ref.dtype)

    outs, moved = _call(
        body,
        name=name,
        grid=(SWA_KVH, nb),
        in_specs=[pl.BlockSpec((G, BLOCK, SWA_HD), cur), kvspec(cur), kvspec(prev), kvspec(cur), kvspec(prev),
                  pl.BlockSpec((G, BLOCK, 2 * BLOCK), lambda kh, qb: (kh, 0, 0)),
                  pl.BlockSpec((1, G * BLOCK, LANES), lambda kh, qb: (kh, 0, 0))],
        out_specs=[pl.BlockSpec((G, BLOCK, SWA_HD), cur)],
        out_shape=[jax.ShapeDtypeStruct((H, S, SWA_HD), BF16)],
        sem=("parallel", "parallel"),
        args=(q, k, k, v, v, bias, sink),
        comm=comm,
    )
    return outs[0], moved


def _swa_bwd(q, k, v, bias, sink, do, *, name, comm=()):
    H, S, _ = q.shape
    G = H // SWA_KVH
    nb = S // BLOCK
    cur = lambda kh, qb: (kh, jnp.minimum(qb, nb - 1), 0)
    prev = lambda kh, qb: (kh, jnp.maximum(jnp.minimum(qb, nb - 1) - 1, 0), 0)
    lag = lambda kh, qb: (kh, jnp.maximum(qb - 1, 0), 0)
    kvspec = lambda im: pl.BlockSpec((1, BLOCK, SWA_HD), im)

    def body(q_ref, kc_ref, kp_ref, vc_ref, vp_ref, bias_ref, sink_ref, do_ref,
             dq_ref, dk_ref, dv_ref, dbias_ref, dsink_ref, ck_scr, cv_scr):
        qb = pl.program_id(1)

        @pl.when(qb == 0)
        def _():
            dbias_ref[...] = jnp.zeros_like(dbias_ref)
            dsink_ref[...] = jnp.zeros_like(dsink_ref)
            ck_scr[...] = jnp.zeros_like(ck_scr)
            cv_scr[...] = jnp.zeros_like(cv_scr)

        @pl.when(qb < nb)
        def _():
            q2, kb, s = _swa_scores(q_ref, kp_ref, kc_ref, bias_ref, qb, G)
            p, ps = _swa_probs(s, sink_ref[0][:, :1])
            vb = jnp.concatenate([vp_ref[0], vc_ref[0]], axis=0)
            do2 = do_ref[...].reshape(G * BLOCK, SWA_HD)
            dp = lax.dot_general(do2, vb, NT, preferred_element_type=F32)
            delta = jnp.sum(dp * p, axis=1, keepdims=True)
            ds = p * (dp - delta)
            dbias_ref[...] += ds.reshape(G, BLOCK, 2 * BLOCK)
            dsk = jnp.sum((-ps * delta).reshape(G, BLOCK, 1), axis=1)
            dsink_ref[0] += jnp.broadcast_to(dsk, (G, LANES))
            dsb = (ds * SWA_SCALE).astype(BF16)
            dq_ref[...] = jnp.dot(dsb, kb, preferred_element_type=F32).reshape(G, BLOCK, SWA_HD).astype(dq_ref.dtype)
            dkb = lax.dot_general(dsb, q2, TN, preferred_element_type=F32)
            dvb = lax.dot_general(p.astype(BF16), do2, TN, preferred_element_type=F32)
            dk_ref[0] = (ck_scr[...] + dkb[:BLOCK]).astype(dk_ref.dtype)
            dv_ref[0] = (cv_scr[...] + dvb[:BLOCK]).astype(dv_ref.dtype)
            ck_scr[...] = dkb[BLOCK:]
            cv_scr[...] = dvb[BLOCK:]

        @pl.when(qb == nb)
        def _():
            dk_ref[0] = ck_scr[...].astype(dk_ref.dtype)
            dv_ref[0] = cv_scr[...].astype(dv_ref.dtype)

    return _call(
        body,
        name=name,
        grid=(SWA_KVH, nb + 1),
        in_specs=[pl.BlockSpec((G, BLOCK, SWA_HD), cur), kvspec(cur), kvspec(prev), kvspec(cur), kvspec(prev),
                  pl.BlockSpec((G, BLOCK, 2 * BLOCK), lambda kh, qb: (kh, 0, 0)),
                  pl.BlockSpec((1, G * BLOCK, LANES), lambda kh, qb: (kh, 0, 0)),
                  pl.BlockSpec((G, BLOCK, SWA_HD), cur)],
        out_specs=[pl.BlockSpec((G, BLOCK, SWA_HD), cur), kvspec(lag), kvspec(lag),
                   pl.BlockSpec((G, BLOCK, 2 * BLOCK), lambda kh, qb: (kh, 0, 0)),
                   pl.BlockSpec((1, G, LANES), lambda kh, qb: (kh, 0, 0))],
        out_shape=[jax.ShapeDtypeStruct((H, S, SWA_HD), BF16),
                   jax.ShapeDtypeStruct((SWA_KVH, S, SWA_HD), BF16),
                   jax.ShapeDtypeStruct((SWA_KVH, S, SWA_HD), BF16),
                   jax.ShapeDtypeStruct((H, BLOCK, 2 * BLOCK), F32),
                   jax.ShapeDtypeStruct((SWA_KVH, G, LANES), F32)],
        scratch_shapes=[pltpu.VMEM((BLOCK, SWA_HD), F32), pltpu.VMEM((BLOCK, SWA_HD), F32)],
        sem=("parallel", "arbitrary"),
        args=(q, k, k, v, v, bias, sink, do),
        comm=comm,
    )


def _gate_mix(z, o_a, o_b, *, D, off_a, off_b, name):
    S = z.shape[0]
    tr = _tile(S, 256, 16)
    row = pl.BlockSpec((tr, D), lambda i: (i, 0))
    ca, cb = _col_view(D, off_a), _col_view(D, off_b)

    def body(ga_ref, gb_ref, oa_ref, ob_ref, m_ref):
        m = (_sigmoid(ga_ref[...].astype(F32)) * oa_ref[...].astype(F32)
             + _sigmoid(gb_ref[...].astype(F32)) * ob_ref[...].astype(F32))
        m_ref[...] = m.astype(m_ref.dtype)

    return pl.pallas_call(
        body,
        name=name,
        grid=(S // tr,),
        in_specs=[pl.BlockSpec((tr, D), lambda i: (i, ca)), pl.BlockSpec((tr, D), lambda i: (i, cb)), row, row],
        out_specs=row,
        out_shape=jax.ShapeDtypeStruct((S, D), BF16),
        compiler_params=_params("parallel"),
    )(z, z, o_a, o_b)


def _gate_mix_bwd(dm, z, o_a, o_b, *, D, off_a, off_b, name):
    S = z.shape[0]
    tr = _tile(S, 256, 16)
    row = pl.BlockSpec((tr, D), lambda i: (i, 0))
    ca, cb = _col_view(D, off_a), _col_view(D, off_b)

    def body(dm_ref, ga_ref, gb_ref, oa_ref, ob_ref, dga_ref, dgb_ref, doa_ref, dob_ref):
        d = dm_ref[...].astype(F32)
        for g_ref, o_ref, dg_ref, do_ref in ((ga_ref, oa_ref, dga_ref, doa_ref), (gb_ref, ob_ref, dgb_ref, dob_ref)):
            sg = _sigmoid(g_ref[...].astype(F32))
            dg_ref[...] = (d * o_ref[...].astype(F32) * (sg * (1.0 - sg))).astype(dg_ref.dtype)
            do_ref[...] = (d * sg).astype(do_ref.dtype)

    return pl.pallas_call(
        body,
        name=name,
        grid=(S // tr,),
        in_specs=[row, pl.BlockSpec((tr, D), lambda i: (i, ca)), pl.BlockSpec((tr, D), lambda i: (i, cb)), row, row],
        out_specs=[row] * 4,
        out_shape=[jax.ShapeDtypeStruct((S, D), BF16)] * 4,
        compiler_params=_params("parallel"),
    )(dm, z, z, o_a, o_b)


def _conv_taps(buf, cw_ref, cb_ref, rows):
    y = cb_ref[...] + cw_ref[0:1, :] * buf[pl.ds(HALO - 2, rows), :]
    y = y + cw_ref[1:2, :] * buf[pl.ds(HALO - 1, rows), :]
    return y + cw_ref[2:3, :] * buf[pl.ds(HALO, rows), :]


def _conv_gate(up, cw, cb, *, name):
    S, F2 = up.shape
    F = F2 // 2
    tr = _tile(S, 512, HALO)
    tc = _tile(F, 512)
    nc = F // tc
    hb = tr // HALO

    def halo_map(shift):
        return lambda i, j: (jnp.maximum(i * hb - 1, 0), j + shift)

    def body(x1_ref, h1_ref, x2_ref, h2_ref, cw1_ref, cw2_ref, cb1_ref, cb2_ref, a_ref, b1, b2):
        first = pl.program_id(0) == 0
        us = []
        for x_ref, h_ref, cw_ref, cb_ref, buf in ((x1_ref, h1_ref, cw1_ref, cb1_ref, b1),
                                                  (x2_ref, h2_ref, cw2_ref, cb2_ref, b2)):
            buf[0:HALO, :] = jnp.where(first, 0.0, h_ref[...].astype(F32))
            buf[HALO:, :] = x_ref[...].astype(F32)
            us.append(_conv_taps(buf, cw_ref, cb_ref, tr))
        u1, u2 = us
        a_ref[...] = (u1 * _sigmoid(u1) * u2).astype(a_ref.dtype)

    return pl.pallas_call(
        body,
        name=name,
        grid=(S // tr, nc),
        in_specs=[pl.BlockSpec((tr, tc), lambda i, j: (i, j)), pl.BlockSpec((HALO, tc), halo_map(0)),
                  pl.BlockSpec((tr, tc), lambda i, j: (i, j + nc)), pl.BlockSpec((HALO, tc), halo_map(nc)),
                  pl.BlockSpec((CONV_WIDTH, tc), lambda i, j: (0, j)),
                  pl.BlockSpec((CONV_WIDTH, tc), lambda i, j: (0, j + nc)),
                  pl.BlockSpec((1, tc), lambda i, j: (0, j)), pl.BlockSpec((1, tc), lambda i, j: (0, j + nc))],
        out_specs=pl.BlockSpec((tr, tc), lambda i, j: (i, j)),
        out_shape=jax.ShapeDtypeStruct((S, F), BF16),
        scratch_shapes=[pltpu.VMEM((tr + HALO, tc), F32), pltpu.VMEM((tr + HALO, tc), F32)],
        compiler_params=_params("parallel", "parallel"),
    )(up, up, up, up, cw, cw, cb, cb)


def _conv_gate_bwd(up, da, cw, cb, *, name, comm=()):
    S, F2 = up.shape
    F = F2 // 2
    tr = _tile(S, 256, HALO)
    tc = _tile(F, 512)
    nc = F // tc
    hb = tr // HALO
    n_halo = S // HALO
    ni = S // tr
    ext = tr + 8

    def cur(shift):
        return lambda j, i: (i, j % nc + shift)

    def before(shift):
        return lambda j, i: (jnp.maximum(i * hb - 1, 0), j % nc + shift)

    def after(shift):
        return lambda j, i: (jnp.minimum((i + 1) * hb, n_halo - 1), j % nc + shift)

    def vec(rows, shift):
        return pl.BlockSpec((rows, tc), lambda j, i: (0, j % nc + shift))

    def body(x1_ref, p1_ref, n1_ref, x2_ref, p2_ref, n2_ref, da_ref, dan_ref,
             cw1_ref, cw2_ref, cb1_ref, cb2_ref, cwo_ref,
             dup_ref, dcw_ref, dcb_ref, b1, b2, bda, bdu):
        j, i = pl.program_id(0), pl.program_id(1)
        first, last = i == 0, i == ni - 1

        @pl.when(i == 0)
        def _():
            dcw_ref[...] = jnp.zeros_like(dcw_ref)
            dcb_ref[...] = jnp.zeros_like(dcb_ref)

        for x_ref, p_ref, n_ref, buf in ((x1_ref, p1_ref, n1_ref, b1), (x2_ref, p2_ref, n2_ref, b2)):
            buf[0:HALO, :] = jnp.where(first, 0.0, p_ref[...].astype(F32))
            buf[HALO:HALO + tr, :] = x_ref[...].astype(F32)
            buf[HALO + tr:, :] = n_ref[...].astype(F32)
        bda[0:tr, :] = da_ref[...].astype(F32)
        bda[tr:, :] = jnp.where(last, 0.0, dan_ref[...].astype(F32))
        dae = bda[pl.ds(0, ext), :]
        u1 = _conv_taps(b1, cw1_ref, cb1_ref, ext)
        sg = _sigmoid(u1)

        def finish(du, own):
            bdu[...] = du
            d0 = bdu[pl.ds(0, tr), :]
            dup = cwo_ref[2:3, :] * d0 + cwo_ref[1:2, :] * bdu[pl.ds(1, tr), :] + cwo_ref[0:1, :] * bdu[pl.ds(2, tr), :]
            dup_ref[...] = dup.astype(dup_ref.dtype)
            dcb_ref[...] += jnp.sum(d0, axis=0, keepdims=True)
            for tap in range(CONV_WIDTH):
                dcw_ref[tap:tap + 1, :] += jnp.sum(d0 * own[pl.ds(HALO - 2 + tap, tr), :], axis=0, keepdims=True)

        @pl.when(j < nc)
        def _():
            u2 = _conv_taps(b2, cw2_ref, cb2_ref, ext)
            finish(dae * u2 * (sg * (1.0 + u1 * (1.0 - sg))), b1)

        @pl.when(j >= nc)
        def _():
            finish(dae * (u1 * sg), b2)

    return _call(
        body,
        name=name,
        grid=(2 * nc, ni),
        in_specs=[pl.BlockSpec((tr, tc), cur(0)), pl.BlockSpec((HALO, tc), before(0)), pl.BlockSpec((HALO, tc), after(0)),
                  pl.BlockSpec((tr, tc), cur(nc)), pl.BlockSpec((HALO, tc), before(nc)), pl.BlockSpec((HALO, tc), after(nc)),
                  pl.BlockSpec((tr, tc), cur(0)), pl.BlockSpec((HALO, tc), after(0)),
                  vec(CONV_WIDTH, 0), vec(CONV_WIDTH, nc), vec(1, 0), vec(1, nc),
                  pl.BlockSpec((CONV_WIDTH, tc), lambda j, i: (0, j))],
        out_specs=[pl.BlockSpec((tr, tc), lambda j, i: (i, j)),
                   pl.BlockSpec((CONV_WIDTH, tc), lambda j, i: (0, j)),
                   pl.BlockSpec((1, tc), lambda j, i: (0, j))],
        out_shape=[jax.ShapeDtypeStruct((S, F2), BF16), jax.ShapeDtypeStruct((CONV_WIDTH, F2), F32),
                   jax.ShapeDtypeStruct((1, F2), F32)],
        scratch_shapes=[pltpu.VMEM((tr + 2 * HALO, tc), F32), pltpu.VMEM((tr + 2 * HALO, tc), F32),
                        pltpu.VMEM((tr + HALO, tc), F32), pltpu.VMEM((ext, tc), F32)],
        sem=("parallel", "arbitrary"),
        args=(up, up, up, up, up, up, da, da, cw, cw, cb, cb, cw),
        comm=comm,
    )


def _adam_math(w, g, m, v):
    m = ADAM_B1 * m + (1.0 - ADAM_B1) * g
    v = ADAM_B2 * v + (1.0 - ADAM_B2) * (g * g)
    m_hat = m / (1.0 - ADAM_B1 ** ADAM_STEP)
    v_hat = v / (1.0 - ADAM_B2 ** ADAM_STEP)
    delta = -ADAM_LR * (m_hat / (jnp.sqrt(v_hat) + ADAM_EPS) + ADAM_WD * w)
    return delta, m, v


def _adamw(w, m, v, parts, *, name):
    R, C = w.shape
    tr = _tile(R, 256, 16)
    row = pl.BlockSpec((tr, C), lambda i: (i, 0))

    def body(w_ref, m_ref, v_ref, p_ref, g_ref, d_ref, m2_ref, v2_ref):
        g = p_ref[0].astype(F32)
        for k in range(1, N_DEV):
            g = g + p_ref[k].astype(F32)
        g_ref[...] = g
        d_ref[...], m2_ref[...], v2_ref[...] = _adam_math(w_ref[...], g, m_ref[...], v_ref[...])

    return pl.pallas_call(
        body,
        name=name,
        grid=(R // tr,),
        in_specs=[row, row, row, pl.BlockSpec((N_DEV, tr, C), lambda i: (0, i, 0))],
        out_specs=[row] * 4,
        out_shape=[jax.ShapeDtypeStruct((R, C), F32)] * 4,
        compiler_params=_params("parallel"),
    )(w, m, v, parts)


def _adamw_ada(w, m, v, cact_t, dmod_cols, *, name):
    R, C = w.shape
    B = cact_t.shape[1]
    tr = _tile(R, 256, 8)
    row = pl.BlockSpec((tr, C), lambda i: (i, 0))

    def body(w_ref, m_ref, v_ref, c_ref, d_ref, g_ref, dl_ref, m2_ref, v2_ref):
        g = c_ref[:, 0:1] * d_ref[0:1, :]
        for b in range(1, B):
            g = g + c_ref[:, b:b + 1] * d_ref[b:b + 1, :]
        g_ref[...] = g
        dl_ref[...], m2_ref[...], v2_ref[...] = _adam_math(w_ref[...], g, m_ref[...], v_ref[...])

    return pl.pallas_call(
        body,
        name=name,
        grid=(R // tr,),
        in_specs=[row, row, row, pl.BlockSpec((tr, B), lambda i: (i, 0)), pl.BlockSpec((B, C), lambda i: (0, 0))],
        out_specs=[row] * 4,
        out_shape=[jax.ShapeDtypeStruct((R, C), F32)] * 4,
        compiler_params=_params("parallel"),
    )(w, m, v, cact_t, dmod_cols)


def _z_layout(D, q_rank, kv_rank):
    kv = SWA_KVH * SWA_HD
    orig = {}
    o = 0
    for nm, w in (("cq", q_rank), ("ckv", kv_rank), ("kr", MLA_ROPE), ("qs", D), ("ks", kv), ("vs", kv),
                  ("ga", D), ("gb", D)):
        orig[nm] = (o, w)
        o += w
    blockw = {"cq": q_rank, "ckv": kv_rank, "kr": LANES, "qs": D, "ks": kv, "vs": kv, "ga": D, "gb": D}
    best = None
    for perm in itertools.permutations(("cq", "ckv", "ks", "vs", "kr")):
        off, new = 0, {}
        for nm in ("ga", "gb", "qs") + perm:
            off = _round_up(off, blockw[nm])
            new[nm] = off
            off += blockw[nm]
        if best is None or off < best[0]:
            best = (off, new)
    total = _round_up(best[0], 1024 if best[0] > 4096 else 512)
    return orig, best[1], blockw, total, o


def _permute_w_in(w, lay):
    orig, new, blockw, total, _ = lay
    parts, at = [], 0
    for nm in sorted(new, key=new.get):
        if new[nm] > at:
            parts.append(jnp.zeros((w.shape[0], new[nm] - at), w.dtype))
        o, wd = orig[nm]
        parts.append(w[:, o:o + wd])
        if blockw[nm] > wd:
            parts.append(jnp.zeros((w.shape[0], blockw[nm] - wd), w.dtype))
        at = new[nm] + blockw[nm]
    if total > at:
        parts.append(jnp.zeros((w.shape[0], total - at), w.dtype))
    return jnp.concatenate(parts, axis=1)


def _unpermute_w_in(wp, lay):
    orig, new, _, _, _ = lay
    return jnp.concatenate([wp[:, new[nm]:new[nm] + orig[nm][1]] for nm in sorted(orig, key=lambda n: orig[n][0])],
                           axis=1)


def _assemble_dz(parts, lay, S):
    _, new, blockw, total, _ = lay
    cols, at = [], 0
    for nm in sorted(new, key=new.get):
        if new[nm] > at:
            cols.append(jnp.zeros((S, new[nm] - at), BF16))
        cols.append(parts[nm])
        at = new[nm] + blockw[nm]
    if total > at:
        cols.append(jnp.zeros((S, total - at), BF16))
    return jnp.concatenate(cols, axis=1)


def _unshard_cols(g):
    return jnp.transpose(g, (1, 0, 2)).reshape(g.shape[1], N_DEV * g.shape[2])


def _shard_cols(w):
    K, N = w.shape
    return jnp.transpose(w.reshape(K, N_DEV, N // N_DEV), (1, 0, 2))


def _pack(vecs, rows):
    flat = jnp.concatenate([v.reshape(-1) for v in vecs])
    return jnp.pad(flat, (0, rows * LANES - flat.shape[0])).reshape(rows, LANES)


def kernel(x, c, w_ada, b_ada, g_pre_mix, g_post_mix, w_in, g_q_lat, w_uq, g_kv_lat, w_ukv, rel_bias, sinks, w_o, g_pre_ffn, g_post_ffn, w_up, conv_w, conv_b, w_down, loss_target, m_w_ada, m_b_ada, m_g_pre_mix, m_g_post_mix, m_w_in, m_g_q_lat, m_w_uq, m_g_kv_lat, m_w_ukv, m_rel_bias, m_sinks, m_w_o, m_g_pre_ffn, m_g_post_ffn, m_w_up, m_conv_w, m_conv_b, m_w_down, v_w_ada, v_b_ada, v_g_pre_mix, v_g_post_mix, v_w_in, v_g_q_lat, v_w_uq, v_g_kv_lat, v_w_ukv, v_rel_bias, v_sinks, v_w_o, v_g_pre_ffn, v_g_post_ffn, v_w_up, v_conv_w, v_conv_b, v_w_down):
    S, D = x.shape[1], x.shape[2]
    Q_RANK, KV_RANK = g_q_lat.shape[1], g_kv_lat.shape[1]
    H_MLA = D // MLA_V
    H_SWA = D // SWA_HD
    G_SWA = H_SWA // SWA_KVH
    F2 = w_up.shape[2] * N_DEV
    F = F2 // 2
    ada_n = w_ada.shape[2]
    me = 4 * lax.axis_index("x") + 2 * lax.axis_index("y") + lax.axis_index("c")
    lay = _z_layout(D, Q_RANK, KV_RANK)
    _, zoff, _, NZ, in_cols = lay
    assert in_cols == w_in.shape[2] * N_DEV

    x2, tgt = x[0], loss_target[0]

    cw_n = conv_w.shape[2]
    small = jnp.concatenate([jnp.pad(c, ((0, 7), (0, 0))), jnp.pad(conv_w[0], ((0, 8 - CONV_WIDTH), (0, 0)))], axis=1)
    small_all = _all_gather(small, name="ag_cond", in_vmem=True)
    c_all = small_all[:, 0, :D]
    cw_full = _unshard_cols(small_all[:, :CONV_WIDTH, D:])
    b_cols = lax.dynamic_slice_in_dim(b_ada, me * ada_n, ada_n, axis=1)
    c_act, mod_cols = _ada_fwd(c_all, w_ada[0], b_cols, name="ada_fwd")
    mod_all = _all_gather(mod_cols, name="ag_mod", in_vmem=True)
    mod_me = lax.dynamic_index_in_dim(mod_all, me, axis=1, keepdims=False).reshape(1, N_DEV * ada_n)
    sh1, sc1, gt1, sh2, sc2, gt2 = [mod_me[:, k * D:(k + 1) * D] for k in range(6)]

    w_in_p = _permute_w_in(_unshard_cols(_all_gather(w_in[0].astype(BF16), name="ag_w_in", in_vmem=False)), lay)

    h1 = _prenorm(x2, g_pre_mix, sc1, sh1, name="prenorm_mix")
    z, (uq_g, ukv_g, o_g) = _matmul(h1, w_in_p, mode="nn", out_dtype=BF16, name="mm_in",
                                    comm=[("gather", w_uq[0].astype(BF16)), ("gather", w_ukv[0].astype(BF16)),
                                          ("gather", w_o[0].astype(BF16))])
    w_uq_p = jnp.pad(_unshard_cols(uq_g).reshape(Q_RANK, H_MLA, MLA_QK), ((0, 0), (0, 0), (0, MLA_QK_PAD - MLA_QK))
                     ).reshape(Q_RANK, H_MLA * MLA_QK_PAD)
    w_ukv_f = _unshard_cols(ukv_g)
    w_o_f = o_g.reshape(D, D)
    cqn = _prenorm(z, g_q_lat, None, None, name="norm_cq", off=zoff["cq"], width=Q_RANK)
    ckvn = _prenorm(z, g_kv_lat, None, None, name="norm_ckv", off=zoff["ckv"], width=KV_RANK)
    q_raw = _matmul(cqn, w_uq_p, mode="nn", out_dtype=BF16, name="mm_uq")
    kv = _matmul(ckvn, w_ukv_f, mode="nn", out_dtype=BF16, name="mm_ukv")
    tab_q = _rope_tables(S, MLA_QK_PAD, MLA_NOPE)
    tab_k = _rope_tables(S, LANES, 0)
    Qr = _rope(q_raw, tab_q, heads=H_MLA, width=MLA_QK_PAD, transpose=False, name="rope_q", scale=MLA_Q_PRESCALE)
    krr = _rope(z, tab_k, heads=1, width=LANES, transpose=False, name="rope_k", off=zoff["kr"])
    Kc = _assemble_k(kv, krr, heads=H_MLA, name="assemble_k")
    (o_a, lse), (up_g,) = _flash_fwd(Qr, Kc, kv, heads=H_MLA, name="mla_fwd", comm=[("gather", w_up[0].astype(BF16))])
    w_up_f = _unshard_cols(up_g)

    bucket, valid = _t5_bucket_table()
    onehot = (jnp.asarray(bucket).reshape(-1, 1) == jnp.arange(LANES)[None, :]).astype(F32)
    rb_pad = jnp.pad(rel_bias, ((0, LANES - REL_BUCKETS), (0, LANES - H_SWA)))
    bias_t = _matmul(onehot, rb_pad, mode="nn", out_dtype=F32, name="bias_table", tm=2048, precision=HIGHEST)
    bias_full = jnp.transpose(bias_t[:, :H_SWA].reshape(BLOCK, 2 * BLOCK, H_SWA), (2, 0, 1))
    bias_full = jnp.where(jnp.asarray(valid)[None], bias_full, NEG)
    sink_rows = jnp.broadcast_to(sinks.reshape(SWA_KVH, G_SWA, 1, 1), (SWA_KVH, G_SWA, BLOCK, LANES)
                                 ).reshape(SWA_KVH, G_SWA * BLOCK, LANES)
    kvw = SWA_KVH * SWA_HD

    def heads_first(t, n):
        return jnp.transpose(t.reshape(S, n, SWA_HD), (1, 0, 2))

    def heads_last(t):
        return jnp.transpose(t, (1, 0, 2)).reshape(S, t.shape[0] * SWA_HD)

    qs_h = heads_first(z[:, zoff["qs"]:zoff["qs"] + D], H_SWA)
    ks_h = heads_first(z[:, zoff["ks"]:zoff["ks"] + kvw], SWA_KVH)
    vs_h = heads_first(z[:, zoff["vs"]:zoff["vs"] + kvw], SWA_KVH)
    o_b_h, (down_g,) = _swa_fwd(qs_h, ks_h, vs_h, bias_full, sink_rows, name="swa_fwd",
                                comm=[("gather", w_down[0].astype(BF16))])
    o_b = heads_last(o_b_h)
    w_down_f = down_g.reshape(F, D)

    mixin = _gate_mix(z, o_a, o_b, D=D, off_a=zoff["ga"], off_b=zoff["gb"], name="gate_mix")
    mix = _matmul(mixin, w_o_f, mode="nn", out_dtype=F32, name="mm_o")
    x1 = _postnorm_res(x2, mix, gt1, g_post_mix, name="postnorm_mix")

    h2 = _prenorm(x1, g_pre_ffn, sc2, sh2, name="prenorm_ffn")
    up = _matmul(h2, w_up_f, mode="nn", out_dtype=BF16, name="mm_up")
    act = _conv_gate(up, cw_full, conv_b, name="conv_gate")
    y = _matmul(act, w_down_f, mode="nn", out_dtype=F32, name="mm_down")
    loss_part, dout, dy, dgt2, dg_post_ffn = _final_loss(x1, y, tgt, gt2, g_post_ffn, name="final_loss")
    loss = lax.psum(loss_part[0, 0], ("x", "y", "c"))

    dw_down = _matmul(act, dy, mode="tn", out_dtype=BF16, name="mm_down_dw")
    dact = _matmul(dy, w_down_f, mode="nt", out_dtype=BF16, name="mm_down_dx")
    (dup, dcw, dcb), (got_down,) = _conv_gate_bwd(up, dact, cw_full, conv_b, name="conv_gate_bwd",
                                                  comm=[("scatter", dw_down.reshape(N_DEV, F // N_DEV, D))])
    dw_up = _matmul(h2, dup, mode="tn", out_dtype=BF16, name="mm_up_dw")
    dh2 = _matmul(dup, w_up_f, mode="nt", out_dtype=F32, name="mm_up_dx")
    dx1, dg_pre_ffn, dsc2, dsh2 = _prenorm_bwd(x1, dh2, dout, g_pre_ffn, sc2, name="prenorm_ffn_bwd", out_dtype=F32)

    dmix, dgt1, dg_post_mix = _postnorm_bwd(dx1, mix, gt1, g_post_mix, name="postnorm_mix_bwd")
    dw_o = _matmul(mixin, dmix, mode="tn", out_dtype=BF16, name="mm_o_dw")
    dmixin = _matmul(dmix, w_o_f, mode="nt", out_dtype=BF16, name="mm_o_dx")
    dga, dgb, do_a, do_b = _gate_mix_bwd(dmixin, z, o_a, o_b, D=D, off_a=zoff["ga"], off_b=zoff["gb"],
                                         name="gate_mix_bwd")
    dcw_parts = jnp.pad(_shard_cols(dcw), ((0, 0), (0, 16 - CONV_WIDTH), (0, 0)))
    (dqs_h, dks_h, dvs_h, dbias, dsink), (got_o, got_cw) = _swa_bwd(
        qs_h, ks_h, vs_h, bias_full, sink_rows, heads_first(do_b, H_SWA), name="swa_bwd",
        comm=[("scatter", dw_o.reshape(N_DEV, D // N_DEV, D)), ("scatter", dcw_parts)])
    drel_t = _matmul(dbias.reshape(H_SWA, BLOCK * 2 * BLOCK), onehot, mode="nn", out_dtype=F32, name="bias_grad",
                     tk=4096, precision=HIGHEST)
    d_rel_bias = jnp.transpose(drel_t[:, :REL_BUCKETS])
    d_sinks = dsink[:, :, 0].reshape(1, H_SWA)

    (dQ, dK, dV), (got_up,) = _flash_bwd(Qr, Kc, kv, do_a, o_a, lse, heads=H_MLA, name="mla_bwd",
                                         comm=[("scatter", _shard_cols(dw_up))])
    dq_raw = _rope(dQ, tab_q, heads=H_MLA, width=MLA_QK_PAD, transpose=True, name="rope_q_bwd",
                   scale=MLA_Q_PRESCALE)
    dkv, dkr = _assemble_k_bwd(dK, dV, tab_k, heads=H_MLA, name="assemble_k_bwd")
    dcqn = _matmul(dq_raw, w_uq_p, mode="nt", out_dtype=F32, name="mm_uq_dx")
    dw_uq_p = _matmul(cqn, dq_raw, mode="tn", out_dtype=BF16, name="mm_uq_dw")
    dckvn = _matmul(dkv, w_ukv_f, mode="nt", out_dtype=F32, name="mm_ukv_dx")
    dw_ukv = _matmul(ckvn, dkv, mode="tn", out_dtype=BF16, name="mm_ukv_dw")
    dw_uq = dw_uq_p.reshape(Q_RANK, H_MLA, MLA_QK_PAD)[:, :, :MLA_QK].reshape(Q_RANK, H_MLA * MLA_QK)
    dcq, dg_q = _prenorm_bwd(z, dcqn, None, g_q_lat, None, name="norm_cq_bwd", out_dtype=BF16,
                             off=zoff["cq"], width=Q_RANK)
    dckv, dg_kv = _prenorm_bwd(z, dckvn, None, g_kv_lat, None, name="norm_ckv_bwd", out_dtype=BF16,
                               off=zoff["ckv"], width=KV_RANK)
    dz = _assemble_dz({"ga": dga, "gb": dgb, "qs": heads_last(dqs_h), "cq": dcq, "ckv": dckv,
                       "ks": heads_last(dks_h), "vs": heads_last(dvs_h), "kr": dkr}, lay, S)
    dw_in_p, (got_uq, got_ukv) = _matmul(h1, dz, mode="tn", out_dtype=BF16, name="mm_in_dw",
                                         comm=[("scatter", _shard_cols(dw_uq)), ("scatter", _shard_cols(dw_ukv))])
    dh1, (got_in,) = _matmul(dz, w_in_p, mode="nt", out_dtype=F32, name="mm_in_dx",
                             comm=[("scatter", _shard_cols(_unpermute_w_in(dw_in_p, lay)))])
    grad_x, dg_pre_mix, dsc1, dsh1 = _prenorm_bwd(x2, dh1, dx1, g_pre_mix, sc1, name="prenorm_mix_bwd",
                                                  out_dtype=F32)
    dmod = jnp.concatenate([dsh1, dsc1, dgt1, dsh2, dsc2, dgt2], axis=1)

    small_names = ["b_ada", "g_pre_mix", "g_post_mix", "g_q_lat", "g_kv_lat", "rel_bias", "sinks", "g_pre_ffn",
                   "g_post_ffn", "conv_b"]
    small_w = [b_ada, g_pre_mix, g_post_mix, g_q_lat, g_kv_lat, rel_bias, sinks, g_pre_ffn, g_post_ffn, conv_b]
    small_m = [m_b_ada, m_g_pre_mix, m_g_post_mix, m_g_q_lat, m_g_kv_lat, m_rel_bias, m_sinks, m_g_pre_ffn,
               m_g_post_ffn, m_conv_b]
    small_v = [v_b_ada, v_g_pre_mix, v_g_post_mix, v_g_q_lat, v_g_kv_lat, v_rel_bias, v_sinks, v_g_pre_ffn,
               v_g_post_ffn, v_conv_b]
    small_g = [dmod, dg_pre_mix, dg_post_mix, dg_q, dg_kv, d_rel_bias, d_sinks, dg_pre_ffn, dg_post_ffn, dcb]
    n_small = sum(int(np.prod(w.shape)) for w in small_w)
    rows = _round_up(-(-n_small // LANES), 16)
    parts_small = _all_gather(_pack(small_g, rows), name="ag_small_grads", in_vmem=True)
    sg, sd, sm, sv = _adamw(_pack(small_w, rows), _pack(small_m, rows), _pack(small_v, rows), parts_small,
                            name="adamw_small")

    def unpack(packed):
        flat, out, at = packed.reshape(-1), {}, 0
        for nm, w in zip(small_names, small_w):
            n = int(np.prod(w.shape))
            out[nm] = flat[at:at + n].reshape(w.shape)
            at += n
        return out

    small_out = [unpack(t) for t in (sg, sd, sm, sv)]

    dmod_all = parts_small.reshape(N_DEV, rows * LANES)[:, :6 * D]
    dmod_cols = lax.dynamic_slice_in_dim(dmod_all, me * ada_n, ada_n, axis=1)
    ada_out = _adamw_ada(w_ada[0], m_w_ada[0], v_w_ada[0], jnp.transpose(c_act), dmod_cols, name="adamw_w_ada")

    def owner_update(got, w, m, v, name):
        shp = w.shape
        w2, m2, v2 = (t.reshape(shp[-2], shp[-1]) for t in (w, m, v))
        return [t.reshape(shp) for t in _adamw(w2, m2, v2, got, name="adamw_" + name)]

    def pad_rows(t):
        return jnp.pad(t[0], ((0, 16 - CONV_WIDTH), (0, 0)))

    big = {
        "w_in": owner_update(got_in, w_in, m_w_in, v_w_in, "w_in"),
        "w_uq": owner_update(got_uq, w_uq, m_w_uq, v_w_uq, "w_uq"),
        "w_ukv": owner_update(got_ukv, w_ukv, m_w_ukv, v_w_ukv, "w_ukv"),
        "w_o": owner_update(got_o, w_o, m_w_o, v_w_o, "w_o"),
        "w_up": owner_update(got_up, w_up, m_w_up, v_w_up, "w_up"),
        "w_down": owner_update(got_down, w_down, m_w_down, v_w_down, "w_down"),
    }
    cw_upd = _adamw(pad_rows(conv_w), pad_rows(m_conv_w), pad_rows(v_conv_w), got_cw, name="adamw_conv_w")
    big["conv_w"] = [t[:CONV_WIDTH].reshape(conv_w.shape) for t in cw_upd]
    big["w_ada"] = [t.reshape(w_ada.shape) for t in ada_out]

    order = ["w_ada", "b_ada", "g_pre_mix", "g_post_mix", "w_in", "g_q_lat", "w_uq", "g_kv_lat", "w_ukv", "rel_bias",
             "sinks", "w_o", "g_pre_ffn", "g_post_ffn", "w_up", "conv_w", "conv_b", "w_down"]
    outs = [loss, grad_x.reshape(x.shape)]
    for kind in range(4):
        for nm in order:
            outs.append(big[nm][kind] if nm in big else small_out[kind][nm])
    return tuple(outs)
```

```python
import functools
import itertools
import math

import numpy as np

import jax
import jax.numpy as jnp
from jax import lax
from jax.experimental import pallas as pl
from jax.experimental.pallas import tpu as pltpu

F32 = jnp.float32
BF16 = jnp.bfloat16

N_DEV = 8
MLA_NOPE = 128
MLA_ROPE = 64
MLA_V = 128
MLA_QK = MLA_NOPE + MLA_ROPE
MLA_QK_PAD = 256
ROPE_HALF = MLA_ROPE // 2
ROPE_THETA = 10000.0
SWA_HD = 64
SWA_KVH = 4
WINDOW = 128
BLOCK = 128
REL_BUCKETS = 32
REL_MAX_DIST = 128
CONV_WIDTH = 3
EPS = 1e-6
NEG = -1e30
ADAM_LR = 0.001
ADAM_B1 = 0.9
ADAM_B2 = 0.999
ADAM_EPS = 1e-08
ADAM_WD = 0.01
ADAM_STEP = 10
LANES = 128
HALO = 16
MESH = pl.DeviceIdType.MESH
HIGHEST = lax.Precision.HIGHEST

NN = (((1,), (0,)), ((), ()))
NT = (((1,), (1,)), ((), ()))
TN = (((0,), (0,)), ((), ()))


def _tile(n, pref, align=LANES):
    if n <= pref:
        return n
    t = (pref // align) * align
    while t >= align:
        if n % t == 0:
            return t
        t -= align
    return n


def _round_up(n, m):
    return (n + m - 1) // m * m


def _params(*sem):
    return pltpu.CompilerParams(dimension_semantics=sem)


def _sigmoid(x):
    return 1.0 / (1.0 + jnp.exp(-x))


def _my_place():
    return lax.axis_index("x"), lax.axis_index("y"), lax.axis_index("c")


def _all_gather(x, *, name, in_vmem):
    space = pltpu.VMEM if in_vmem else pl.ANY

    def body(x_ref, out_ref, send_sems, recv_sems, local_sem):
        x_, y_, c_ = _my_place()
        me, sibling = (x_, y_, c_), (x_, y_, 1 - c_)
        chips = [(1 - x_, y_), (x_, 1 - y_), (1 - x_, 1 - y_)]

        def slot(px, py, pc):
            return out_ref.at[4 * px + 2 * py + pc]

        def copy(k, block, to, src=None):
            return pltpu.make_async_remote_copy(
                src_ref=slot(*block) if src is None else src,
                dst_ref=slot(*block),
                send_sem=send_sems.at[k],
                recv_sem=recv_sems.at[k],
                device_id=to,
                device_id_type=MESH,
            )

        mine = pltpu.make_async_copy(x_ref, slot(*me), local_sem)
        mine.start()
        first = [copy(0, me, sibling, src=x_ref)]
        first += [copy(1 + j, me, (*chip, c_), src=x_ref) for j, chip in enumerate(chips)]
        for cp in first:
            cp.start()
        passed = [copy(4 + j, (*chip, c_), sibling) for j, chip in enumerate(chips)]
        for j, chip in enumerate(chips):
            copy(1 + j, (*chip, c_), me).wait_recv()
            passed[j].start()
        copy(0, sibling, me).wait_recv()
        for j, chip in enumerate(chips):
            copy(4 + j, (*chip, 1 - c_), me).wait_recv()
        for cp in first + passed:
            cp.wait_send()
        mine.wait()

    return pl.pallas_call(
        body,
        name=name,
        out_shape=jax.ShapeDtypeStruct((N_DEV,) + x.shape, x.dtype),
        in_specs=[pl.BlockSpec(memory_space=space)],
        out_specs=pl.BlockSpec(memory_space=space),
        scratch_shapes=[
            pltpu.SemaphoreType.DMA((7,)),
            pltpu.SemaphoreType.DMA((7,)),
            pltpu.SemaphoreType.DMA,
        ],
    )(x)


def _carried_copies(kind, x_ref, out_ref, send_sems, recv_sems, local_sems, t):
    x_, y_, c_ = _my_place()
    me = 4 * x_ + 2 * y_ + c_
    remote = []
    for r in range(1, N_DEV):
        px, py, pc = x_ ^ ((r >> 2) & 1), y_ ^ ((r >> 1) & 1), c_ ^ (r & 1)
        src = x_ref if kind == "gather" else x_ref.at[4 * px + 2 * py + pc]
        remote.append(pltpu.make_async_remote_copy(
            src_ref=src, dst_ref=out_ref.at[me],
            send_sem=send_sems.at[7 * t + r - 1], recv_sem=recv_sems.at[7 * t + r - 1],
            device_id=(px, py, pc), device_id_type=MESH))
    own = x_ref if kind == "gather" else x_ref.at[me]
    return remote, pltpu.make_async_copy(own, out_ref.at[me], local_sems.at[t])


def _call(body, *, name, grid, in_specs, out_specs, out_shape, args, scratch_shapes=(), sem=(), comm=(), prefetch=()):
    n_pf = len(prefetch)

    def launch(fn, ins, outs, shapes, scratch, semantics, operands):
        spec = pltpu.PrefetchScalarGridSpec(num_scalar_prefetch=n_pf, grid=grid, in_specs=ins, out_specs=outs,
                                            scratch_shapes=scratch)
        return pl.pallas_call(fn, name=name, grid_spec=spec, out_shape=shapes,
                              compiler_params=_params(*semantics))(*prefetch, *operands)

    if not comm:
        return list(launch(body, list(in_specs), list(out_specs), list(out_shape), list(scratch_shapes), sem, args)), []
    n_in, n_out, n_c, n_s = len(in_specs), len(out_specs), len(comm), len(scratch_shapes)
    kinds = [kind for kind, _ in comm]
    hbm = pl.BlockSpec(memory_space=pl.ANY)

    def wrapped(*refs):
        tables, refs = refs[:n_pf], refs[n_pf:]
        ins, cin = refs[:n_in], refs[n_in:n_in + n_c]
        at = n_in + n_c
        outs, cout = refs[at:at + n_out], refs[at + n_out:at + n_out + n_c]
        scr = refs[at + n_out + n_c:at + n_out + n_c + n_s]
        send, recv, local = refs[-3:]
        ids = [pl.program_id(a) for a in range(len(grid))]
        first = functools.reduce(jnp.logical_and, [i == 0 for i in ids])
        last = functools.reduce(jnp.logical_and, [i == g - 1 for i, g in zip(ids, grid)])

        def copies():
            return [_carried_copies(kinds[t], cin[t], cout[t], send, recv, local, t) for t in range(n_c)]

        @pl.when(first)
        def _():
            for remote, own in copies():
                own.start()
                for cp in remote:
                    cp.start()

        body(*tables, *ins, *outs, *scr)

        @pl.when(last)
        def _():
            for remote, own in copies():
                for cp in remote:
                    cp.wait_recv()
                for cp in remote:
                    cp.wait_send()
                own.wait()

    c_shapes = [jax.ShapeDtypeStruct(((N_DEV,) + a.shape) if kind == "gather" else a.shape, a.dtype)
                for kind, a in comm]
    sems = [pltpu.SemaphoreType.DMA((7 * n_c,)), pltpu.SemaphoreType.DMA((7 * n_c,)), pltpu.SemaphoreType.DMA((n_c,))]
    res = launch(wrapped, list(in_specs) + [hbm] * n_c, list(out_specs) + [hbm] * n_c, list(out_shape) + c_shapes,
                 list(scratch_shapes) + sems, ["arbitrary"] * len(grid), (*args, *[a for _, a in comm]))
    return list(res[:n_out]), list(res[n_out:])


def _matmul(a, b, *, mode, out_dtype, name, tm=1024, tn=1024, tk=2816, precision=None, comm=()):
    if mode == "nn":
        (M, K), (K2, N) = a.shape, b.shape
    elif mode == "nt":
        (M, K), (N, K2) = a.shape, b.shape
    else:
        (K, M), (K2, N) = a.shape, b.shape
    assert K == K2, (a.shape, b.shape, mode)
    tm, tn, tk = _tile(M, tm, LANES if mode == "tn" else 16), _tile(N, tn), _tile(K, tk)
    nk = K // tk
    if mode == "tn":
        a_spec = pl.BlockSpec((tk, tm), lambda i, j, k: (k, i))
    else:
        a_spec = pl.BlockSpec((tm, tk), lambda i, j, k: (i, k))
    if mode == "nt":
        b_spec = pl.BlockSpec((tn, tk), lambda i, j, k: (j, k))
    else:
        b_spec = pl.BlockSpec((tk, tn), lambda i, j, k: (k, j))
    dn = {"nn": NN, "nt": NT, "tn": TN}[mode]

    def product(a_ref, b_ref):
        return lax.dot_general(a_ref[...], b_ref[...], dn, preferred_element_type=F32, precision=precision)

    def body_one(a_ref, b_ref, o_ref):
        o_ref[...] = product(a_ref, b_ref).astype(o_ref.dtype)

    def body_acc(a_ref, b_ref, o_ref, acc_ref):
        k = pl.program_id(2)

        @pl.when(k == 0)
        def _():
            acc_ref[...] = product(a_ref, b_ref)

        @pl.when(k > 0)
        def _():
            acc_ref[...] += product(a_ref, b_ref)

        @pl.when(k == nk - 1)
        def _():
            o_ref[...] = acc_ref[...].astype(o_ref.dtype)

    outs, moved = _call(
        body_one if nk == 1 else body_acc,
        name=name,
        grid=(M // tm, N // tn, nk),
        in_specs=[a_spec, b_spec],
        out_specs=[pl.BlockSpec((tm, tn), lambda i, j, k: (i, j))],
        out_shape=[jax.ShapeDtypeStruct((M, N), out_dtype)],
        scratch_shapes=[] if nk == 1 else [pltpu.VMEM((tm, tn), F32)],
        sem=("parallel", "parallel", "arbitrary"),
        args=(a, b),
        comm=comm,
    )
    return (outs[0], moved) if comm else outs[0]


def _rstd(xf):
    return lax.rsqrt(jnp.mean(xf * xf, axis=-1, keepdims=True) + EPS)


def _col_view(width, off):
    assert off % width == 0
    return off // width


def _prenorm(x, g, sc, sh, *, name, off=0, width=None):
    S = x.shape[0]
    W = x.shape[1] if width is None else width
    cb = _col_view(W, off)
    tr = _tile(S, 512, 16)
    mod = sc is not None
    vec = pl.BlockSpec((1, W), lambda i: (0, 0))

    def body(*refs):
        if mod:
            x_ref, g_ref, sc_ref, sh_ref, o_ref = refs
        else:
            x_ref, g_ref, o_ref = refs
        xf = x_ref[...].astype(F32)
        y = xf * _rstd(xf) * g_ref[...]
        if mod:
            y = y * (1.0 + sc_ref[...]) + sh_ref[...]
        o_ref[...] = y.astype(o_ref.dtype)

    args = (x, g, sc, sh) if mod else (x, g)
    return pl.pallas_call(
        body,
        name=name,
        grid=(S // tr,),
        in_specs=[pl.BlockSpec((tr, W), lambda i: (i, cb))] + [vec] * (len(args) - 1),
        out_specs=pl.BlockSpec((tr, W), lambda i: (i, 0)),
        out_shape=jax.ShapeDtypeStruct((S, W), BF16),
        compiler_params=_params("parallel"),
    )(*args)


def _prenorm_bwd(x, dh, dres, g, sc, *, name, out_dtype, off=0, width=None):
    S = x.shape[0]
    W = x.shape[1] if width is None else width
    cb = _col_view(W, off)
    tr = _tile(S, 256, 16)
    mod = sc is not None
    res = dres is not None
    vec = pl.BlockSpec((1, W), lambda i: (0, 0))
    row = pl.BlockSpec((tr, W), lambda i: (i, 0))

    def body(*refs):
        it = iter(refs)
        x_ref, dh_ref = next(it), next(it)
        dres_ref = next(it) if res else None
        g_ref = next(it)
        sc_ref = next(it) if mod else None
        dx_ref, dg_ref = next(it), next(it)
        dsc_ref, dsh_ref = (next(it), next(it)) if mod else (None, None)
        i = pl.program_id(0)

        @pl.when(i == 0)
        def _():
            dg_ref[...] = jnp.zeros_like(dg_ref)
            if mod:
                dsc_ref[...] = jnp.zeros_like(dsc_ref)
                dsh_ref[...] = jnp.zeros_like(dsh_ref)

        xf = x_ref[...].astype(F32)
        r = _rstd(xf)
        xn = xf * r
        dhf = dh_ref[...].astype(F32)
        gv = g_ref[...]
        if mod:
            one_sc = 1.0 + sc_ref[...]
            dsh_ref[...] += jnp.sum(dhf, axis=0, keepdims=True)
            dsc_ref[...] += jnp.sum(dhf * (xn * gv), axis=0, keepdims=True)
            dg_ref[...] += jnp.sum(dhf * xn * one_sc, axis=0, keepdims=True)
            dxn = dhf * (gv * one_sc)
        else:
            dg_ref[...] += jnp.sum(dhf * xn, axis=0, keepdims=True)
            dxn = dhf * gv
        dx = r * (dxn - xn * jnp.mean(dxn * xn, axis=-1, keepdims=True))
        if res:
            dx = dx + dres_ref[...]
        dx_ref[...] = dx.astype(dx_ref.dtype)

    args = [x, dh] + ([dres] if res else []) + [g] + ([sc] if mod else [])
    in_specs = [pl.BlockSpec((tr, W), lambda i: (i, cb)), row] + ([row] if res else []) + [vec] + ([vec] if mod else [])
    n_vec = 3 if mod else 1
    outs = pl.pallas_call(
        body,
        name=name,
        grid=(S // tr,),
        in_specs=in_specs,
        out_specs=[row] + [vec] * n_vec,
        out_shape=[jax.ShapeDtypeStruct((S, W), out_dtype)] + [jax.ShapeDtypeStruct((1, W), F32)] * n_vec,
        compiler_params=_params("arbitrary"),
    )(*args)
    return outs


def _postnorm_res(x, y, gt, g, *, name):
    S, D = x.shape
    tr = _tile(S, 512, 8)
    row = pl.BlockSpec((tr, D), lambda i: (i, 0))
    vec = pl.BlockSpec((1, D), lambda i: (0, 0))

    def body(x_ref, y_ref, gt_ref, g_ref, o_ref):
        yf = y_ref[...]
        o_ref[...] = x_ref[...] + gt_ref[...] * (yf * _rstd(yf) * g_ref[...])

    return pl.pallas_call(
        body,
        name=name,
        grid=(S // tr,),
        in_specs=[row, row, vec, vec],
        out_specs=row,
        out_shape=jax.ShapeDtypeStruct((S, D), F32),
        compiler_params=_params("parallel"),
    )(x, y, gt, g)


def _postnorm_bwd(dx1, y, gt, g, *, name):
    S, D = y.shape
    tr = _tile(S, 256, 16)
    row = pl.BlockSpec((tr, D), lambda i: (i, 0))
    vec = pl.BlockSpec((1, D), lambda i: (0, 0))

    def body(dx_ref, y_ref, gt_ref, g_ref, dy_ref, dgt_ref, dg_ref):
        @pl.when(pl.program_id(0) == 0)
        def _():
            dgt_ref[...] = jnp.zeros_like(dgt_ref)
            dg_ref[...] = jnp.zeros_like(dg_ref)

        yf = y_ref[...]
        r = _rstd(yf)
        yn = yf * r
        d = dx_ref[...]
        gtv, gv = gt_ref[...], g_ref[...]
        dgt_ref[...] += jnp.sum(d * (yn * gv), axis=0, keepdims=True)
        dg_ref[...] += jnp.sum(d * gtv * yn, axis=0, keepdims=True)
        dyn = d * (gtv * gv)
        dy_ref[...] = (r * (dyn - yn * jnp.mean(dyn * yn, axis=-1, keepdims=True))).astype(dy_ref.dtype)

    return pl.pallas_call(
        body,
        name=name,
        grid=(S // tr,),
        in_specs=[row, row, vec, vec],
        out_specs=[row, vec, vec],
        out_shape=[jax.ShapeDtypeStruct((S, D), BF16), jax.ShapeDtypeStruct((1, D), F32),
                   jax.ShapeDtypeStruct((1, D), F32)],
        compiler_params=_params("arbitrary"),
    )(dx1, y, gt, g)


def _final_loss(x1, y, target, gt, g, *, name):
    S, D = y.shape
    tr = _tile(S, 256, 16)
    row = pl.BlockSpec((tr, D), lambda i: (i, 0))
    vec = pl.BlockSpec((1, D), lambda i: (0, 0))
    one = pl.BlockSpec((1, LANES), lambda i: (0, 0))

    def body(x_ref, y_ref, t_ref, gt_ref, g_ref, loss_ref, dout_ref, dy_ref, dgt_ref, dg_ref):
        @pl.when(pl.program_id(0) == 0)
        def _():
            loss_ref[...] = jnp.zeros_like(loss_ref)
            dgt_ref[...] = jnp.zeros_like(dgt_ref)
            dg_ref[...] = jnp.zeros_like(dg_ref)

        yf = y_ref[...]
        r = _rstd(yf)
        yn = yf * r
        gtv, gv = gt_ref[...], g_ref[...]
        out = x_ref[...] + gtv * (yn * gv)
        diff = out - t_ref[...]
        per_tok = jnp.mean(diff * diff, axis=-1, keepdims=True)
        loss_ref[...] += 0.5 * jnp.sum(per_tok, axis=0, keepdims=True)
        d = diff / D
        dout_ref[...] = d
        dgt_ref[...] += jnp.sum(d * (yn * gv), axis=0, keepdims=True)
        dg_ref[...] += jnp.sum(d * gtv * yn, axis=0, keepdims=True)
        dyn = d * (gtv * gv)
        dy_ref[...] = (r * (dyn - yn * jnp.mean(dyn * yn, axis=-1, keepdims=True))).astype(dy_ref.dtype)

    return pl.pallas_call(
        body,
        name=name,
        grid=(S // tr,),
        in_specs=[row, row, row, vec, vec],
        out_specs=[one, row, row, vec, vec],
        out_shape=[jax.ShapeDtypeStruct((1, LANES), F32), jax.ShapeDtypeStruct((S, D), F32),
                   jax.ShapeDtypeStruct((S, D), BF16), jax.ShapeDtypeStruct((1, D), F32),
                   jax.ShapeDtypeStruct((1, D), F32)],
        compiler_params=_params("arbitrary"),
    )(x1, y, target, gt, g)


def _ada_fwd(c_all, w_local, b_cols, *, name):
    B, D = c_all.shape
    N = w_local.shape[1]
    tn = _tile(N, 512)

    def body(c_ref, w_ref, b_ref, ca_ref, mod_ref):
        cv = c_ref[...]
        ca = cv * _sigmoid(cv)
        ca_ref[...] = ca
        mod_ref[...] = jnp.dot(ca, w_ref[...], preferred_element_type=F32, precision=HIGHEST) + b_ref[...]

    return pl.pallas_call(
        body,
        name=name,
        grid=(N // tn,),
        in_specs=[pl.BlockSpec((B, D), lambda j: (0, 0)), pl.BlockSpec((D, tn), lambda j: (0, j)),
                  pl.BlockSpec((1, tn), lambda j: (0, j))],
        out_specs=[pl.BlockSpec((B, D), lambda j: (0, 0)), pl.BlockSpec((B, tn), lambda j: (0, j))],
        out_shape=[jax.ShapeDtypeStruct((B, D), F32), jax.ShapeDtypeStruct((B, N), F32)],
        compiler_params=_params("arbitrary"),
    )(c_all, w_local, b_cols)


def _rope_tables(S, width, lane_off):
    pos = jnp.arange(S, dtype=F32)
    inv = ROPE_THETA ** (-jnp.arange(0, MLA_ROPE, 2, dtype=F32) / MLA_ROPE)
    ang = pos[:, None] * inv[None, :]
    ang = jnp.concatenate([ang, ang], axis=-1)
    cos, sin = jnp.cos(ang), jnp.sin(ang)
    first = (jnp.arange(MLA_ROPE) < ROPE_HALF)[None, :]
    sa = jnp.where(first, -sin, 0.0)
    sb = jnp.where(first, 0.0, sin)

    def place(t, fill):
        return jnp.pad(t, ((0, 0), (lane_off, width - lane_off - MLA_ROPE)), constant_values=fill)

    return place(cos, 1.0), place(sa, 0.0), place(sb, 0.0)


def _rope_apply(x, cos, sa, sb, width, transpose):
    if transpose:
        return x * cos + pltpu.roll(x * sa, ROPE_HALF, 1) + pltpu.roll(x * sb, width - ROPE_HALF, 1)
    return x * cos + pltpu.roll(x, width - ROPE_HALF, 1) * sa + pltpu.roll(x, ROPE_HALF, 1) * sb


def _rope(x, tables, *, heads, width, transpose, name, off=0, scale=1.0):
    S = x.shape[0]
    cb = _col_view(width, off)
    tr = _tile(S, 512, 16)
    tab = pl.BlockSpec((tr, width), lambda i, h: (i, 0))

    def body(x_ref, c_ref, sa_ref, sb_ref, o_ref):
        y = _rope_apply(x_ref[...].astype(F32), c_ref[...], sa_ref[...], sb_ref[...], width, transpose)
        o_ref[...] = (y if scale == 1.0 else y * scale).astype(o_ref.dtype)

    return pl.pallas_call(
        body,
        name=name,
        grid=(S // tr, heads),
        in_specs=[pl.BlockSpec((tr, width), lambda i, h: (i, cb + h)), tab, tab, tab],
        out_specs=pl.BlockSpec((tr, width), lambda i, h: (i, h)),
        out_shape=jax.ShapeDtypeStruct((S, heads * width), BF16),
        compiler_params=_params("parallel", "parallel"),
    )(x, *tables)


def _assemble_k(kv, krr, *, heads, name):
    S = kv.shape[0]
    tr = _tile(S, 512, 16)

    def body(kn_ref, kr_ref, o_ref):
        o_ref[:, :MLA_NOPE] = kn_ref[...]
        o_ref[:, MLA_NOPE:] = kr_ref[...]

    return pl.pallas_call(
        body,
        name=name,
        grid=(S // tr, heads),
        in_specs=[pl.BlockSpec((tr, MLA_NOPE), lambda i, h: (i, 2 * h)),
                  pl.BlockSpec((tr, LANES), lambda i, h: (i, 0))],
        out_specs=pl.BlockSpec((tr, MLA_QK_PAD), lambda i, h: (i, h)),
        out_shape=jax.ShapeDtypeStruct((S, heads * MLA_QK_PAD), BF16),
        compiler_params=_params("parallel", "parallel"),
    )(kv, krr)


def _assemble_k_bwd(dK, dV, tables, *, heads, name):
    S = dK.shape[0]
    tr = _tile(S, 512, 16)
    tab = pl.BlockSpec((tr, LANES), lambda i, h: (i, 0))

    def body(dk_ref, dv_ref, c_ref, sa_ref, sb_ref, dkv_ref, dkr_ref, acc_ref):
        h = pl.program_id(1)

        @pl.when(h == 0)
        def _():
            acc_ref[...] = jnp.zeros_like(acc_ref)

        dkv_ref[:, :MLA_NOPE] = dk_ref[:, :MLA_NOPE]
        dkv_ref[:, MLA_NOPE:] = dv_ref[...]
        acc_ref[...] += dk_ref[:, MLA_NOPE:].astype(F32)

        @pl.when(h == heads - 1)
        def _():
            dkr_ref[...] = _rope_apply(acc_ref[...], c_ref[...], sa_ref[...], sb_ref[...], LANES,
                                       True).astype(dkr_ref.dtype)

    return pl.pallas_call(
        body,
        name=name,
        grid=(S // tr, heads),
        in_specs=[pl.BlockSpec((tr, MLA_QK_PAD), lambda i, h: (i, h)),
                  pl.BlockSpec((tr, MLA_V), lambda i, h: (i, h)), tab, tab, tab],
        out_specs=[pl.BlockSpec((tr, MLA_QK_PAD), lambda i, h: (i, h)),
                   pl.BlockSpec((tr, LANES), lambda i, h: (i, 0))],
        out_shape=[jax.ShapeDtypeStruct((S, heads * MLA_QK_PAD), BF16), jax.ShapeDtypeStruct((S, LANES), BF16)],
        scratch_shapes=[pltpu.VMEM((tr, LANES), F32)],
        compiler_params=_params("parallel", "arbitrary"),
    )(dK, dV, *tables)


MLA_SCALE = MLA_QK ** -0.5
LOG2E = math.log2(math.e)
LN2 = math.log(2.0)
MLA_Q_PRESCALE = MLA_SCALE * LOG2E


def _lane_tile(v, n):
    return v if n == LANES else jnp.tile(v, (1, n // LANES))


def _causal_mask(s):
    rows = lax.broadcasted_iota(jnp.int32, s.shape, 0)
    cols = lax.broadcasted_iota(jnp.int32, s.shape, 1)
    return jnp.where(cols <= rows, s, NEG)


def _tri_blocks(nb, q_major):
    if q_major:
        pairs = [(q, k) for q in range(nb) for k in range(q + 1)]
    else:
        pairs = [(q, k) for k in range(nb) for q in range(k, nb)]
    return (jnp.asarray(np.array([p[0] for p in pairs], np.int32)),
            jnp.asarray(np.array([p[1] for p in pairs], np.int32)))


HEAD_PAIR = 2


def _flash_fwd(Q, K, KV, *, heads, name, comm=()):
    S = Q.shape[0]
    t = _tile(S, 512)
    nb = S // t
    qt, kt = _tri_blocks(nb, True)
    qw, vw = HEAD_PAIR * MLA_QK_PAD, HEAD_PAIR * MLA_V

    def body(qt_ref, kt_ref, q_ref, k_ref, v0_ref, v1_ref, o_ref, lse_ref, m_scr, l_scr, acc_scr):
        step_id = pl.program_id(1)
        qi, ki = qt_ref[step_id], kt_ref[step_id]

        @pl.when(ki == 0)
        def _():
            m_scr[...] = jnp.full_like(m_scr, NEG)
            l_scr[...] = jnp.zeros_like(l_scr)
            acc_scr[...] = jnp.zeros_like(acc_scr)

        def step(diagonal):
            for h, v_ref in enumerate((v0_ref, v1_ref)):
                cols = slice(h * MLA_QK_PAD, (h + 1) * MLA_QK_PAD)
                s = lax.dot_general(q_ref[:, cols], k_ref[:, cols], NT, preferred_element_type=F32)
                if diagonal:
                    s = _causal_mask(s)
                m_prev = m_scr[h]
                m_new = jnp.maximum(m_prev, jnp.max(s, axis=1, keepdims=True))
                alpha = jnp.exp2(m_prev - m_new)
                p = jnp.exp2(s - _lane_tile(m_new, t))
                l_new = alpha * l_scr[h] + jnp.sum(p, axis=1, keepdims=True)
                acc = alpha * acc_scr[h] + jnp.dot(p.astype(BF16), v_ref[...], preferred_element_type=F32)
                if diagonal:
                    o_ref[:, h * MLA_V:(h + 1) * MLA_V] = (acc / l_new).astype(o_ref.dtype)
                    lse_ref[h] = m_new + jnp.log(l_new) * LOG2E
                else:
                    l_scr[h], acc_scr[h], m_scr[h] = l_new, acc, m_new

        pl.when(ki < qi)(lambda: step(False))
        pl.when(ki == qi)(lambda: step(True))

    def vspec(h):
        return pl.BlockSpec((t, MLA_V), lambda hp, s, qt, kt: (kt[s], 2 * (HEAD_PAIR * hp + h) + 1))

    return _call(
        body,
        name=name,
        grid=(heads // HEAD_PAIR, int(qt.shape[0])),
        in_specs=[pl.BlockSpec((t, qw), lambda hp, s, qt, kt: (qt[s], hp)),
                  pl.BlockSpec((t, qw), lambda hp, s, qt, kt: (kt[s], hp)), vspec(0), vspec(1)],
        out_specs=[pl.BlockSpec((t, vw), lambda hp, s, qt, kt: (qt[s], hp)),
                   pl.BlockSpec((HEAD_PAIR, t, LANES), lambda hp, s, qt, kt: (hp, qt[s], 0))],
        out_shape=[jax.ShapeDtypeStruct((S, heads * MLA_V), BF16),
                   jax.ShapeDtypeStruct((heads, S, LANES), F32)],
        scratch_shapes=[pltpu.VMEM((HEAD_PAIR, t, LANES), F32), pltpu.VMEM((HEAD_PAIR, t, LANES), F32),
                        pltpu.VMEM((HEAD_PAIR, t, MLA_V), F32)],
        sem=("parallel", "arbitrary"),
        args=(Q, K, KV, KV),
        comm=comm,
        prefetch=(qt, kt),
    )


def _flash_bwd(Q, K, KV, dO, O, lse, *, heads, name, comm=()):
    S = Q.shape[0]
    t = _tile(S, 512)
    nb = S // t
    qt, kt = _tri_blocks(nb, False)
    n_steps = int(qt.shape[0])
    qw, vw = HEAD_PAIR * MLA_QK_PAD, HEAD_PAIR * MLA_V

    def body(qt_ref, kt_ref, q_ref, k_ref, v0_ref, v1_ref, do_ref, o_ref, lse_ref,
             dq_ref, dk_ref, dv_ref, dq_scr, dk_scr, dv_scr, delta_scr):
        step_id = pl.program_id(1)
        qi, ki = qt_ref[step_id], kt_ref[step_id]

        @pl.when(step_id == 0)
        def _():
            dq_scr[...] = jnp.zeros_like(dq_scr)

        @pl.when(ki == 0)
        def _():
            for h in range(HEAD_PAIR):
                vc = slice(h * MLA_V, (h + 1) * MLA_V)
                d = jnp.sum(do_ref[:, vc].astype(F32) * o_ref[:, vc].astype(F32), axis=1, keepdims=True)
                delta_scr[h, qi] = jnp.broadcast_to(d, (t, LANES))

        def step(diagonal):
            for h, v_ref in enumerate((v0_ref, v1_ref)):
                cols = slice(h * MLA_QK_PAD, (h + 1) * MLA_QK_PAD)
                vc = slice(h * MLA_V, (h + 1) * MLA_V)
                q, k, do = q_ref[:, cols], k_ref[:, cols], do_ref[:, vc]
                s = lax.dot_general(q, k, NT, preferred_element_type=F32)
                if diagonal:
                    s = _causal_mask(s)
                p = jnp.exp2(s - _lane_tile(lse_ref[h], t))
                dv = lax.dot_general(p.astype(BF16), do, TN, preferred_element_type=F32)
                dp = lax.dot_general(do, v_ref[...], NT, preferred_element_type=F32)
                ds = (p * (dp - _lane_tile(delta_scr[h, qi], t))).astype(BF16)
                dk = lax.dot_general(ds, q, TN, preferred_element_type=F32)
                dq_scr[qi, :, cols] += jnp.dot(ds, k, preferred_element_type=F32)
                if diagonal:
                    dk_scr[h], dv_scr[h] = dk, dv
                else:
                    dk_scr[h] += dk
                    dv_scr[h] += dv

        pl.when(qi > ki)(lambda: step(False))
        pl.when(qi == ki)(lambda: step(True))

        @pl.when(qi == nb - 1)
        def _():
            for h in range(HEAD_PAIR):
                dk_ref[:, h * MLA_QK_PAD:(h + 1) * MLA_QK_PAD] = (dk_scr[h] * LN2).astype(dk_ref.dtype)
                dv_ref[:, h * MLA_V:(h + 1) * MLA_V] = dv_scr[h].astype(dv_ref.dtype)

        @pl.when(step_id == n_steps - 1)
        def _():
            for b in range(nb):
                dq_ref[b * t:(b + 1) * t, :] = (dq_scr[b] * LN2).astype(dq_ref.dtype)

    def vspec(h):
        return pl.BlockSpec((t, MLA_V), lambda hp, s, qt, kt: (kt[s], 2 * (HEAD_PAIR * hp + h) + 1))

    qrow = lambda hp, s, qt, kt: (qt[s], hp)
    krow = lambda hp, s, qt, kt: (kt[s], hp)
    return _call(
        body,
        name=name,
        grid=(heads // HEAD_PAIR, n_steps),
        in_specs=[pl.BlockSpec((t, qw), qrow), pl.BlockSpec((t, qw), krow), vspec(0), vspec(1),
                  pl.BlockSpec((t, vw), qrow), pl.BlockSpec((t, vw), qrow),
                  pl.BlockSpec((HEAD_PAIR, t, LANES), lambda hp, s, qt, kt: (hp, qt[s], 0))],
        out_specs=[pl.BlockSpec((S, qw), lambda hp, s, qt, kt: (0, hp)),
                   pl.BlockSpec((t, qw), krow), pl.BlockSpec((t, vw), krow)],
        out_shape=[jax.ShapeDtypeStruct((S, heads * MLA_QK_PAD), BF16),
                   jax.ShapeDtypeStruct((S, heads * MLA_QK_PAD), BF16),
                   jax.ShapeDtypeStruct((S, heads * MLA_V), BF16)],
        scratch_shapes=[pltpu.VMEM((nb, t, qw), F32), pltpu.VMEM((HEAD_PAIR, t, MLA_QK_PAD), F32),
                        pltpu.VMEM((HEAD_PAIR, t, MLA_V), F32), pltpu.VMEM((HEAD_PAIR, nb, t, LANES), F32)],
        sem=("parallel", "arbitrary"),
        args=(Q, K, KV, KV, dO, O, lse),
        comm=comm,
        prefetch=(qt, kt),
    )


SWA_SCALE = SWA_HD ** -0.5


def _t5_bucket_table():
    a = np.arange(BLOCK)[:, None]
    j = np.arange(2 * BLOCK)[None, :]
    dist = BLOCK + a - j
    max_exact = REL_BUCKETS // 2
    n = np.maximum(dist, 0)
    large = max_exact + (np.log(np.maximum(n, 1).astype(np.float32) / np.float32(max_exact))
                         / np.float32(math.log(REL_MAX_DIST / max_exact))
                         * np.float32(REL_BUCKETS - max_exact)).astype(np.int32)
    large = np.minimum(large, REL_BUCKETS - 1)
    bucket = np.where(n < max_exact, n, large)
    valid = (dist >= 0) & (dist < WINDOW)
    return bucket.astype(np.int32), valid


def _swa_probs(q_ref, kp_ref, kc_ref, bias_ref, sink_ref, qb, G):
    q2 = q_ref[...].reshape(G * BLOCK, SWA_HD)
    kb = jnp.concatenate([kp_ref[0], kc_ref[0]], axis=0)
    s = lax.dot_general(kb, q2, NT, preferred_element_type=F32) * SWA_SCALE + bias_ref[0]
    keys = lax.broadcasted_iota(jnp.int32, s.shape, 0)
    s = jnp.where((keys >= BLOCK) | (qb > 0), s, NEG)
    sink = sink_ref[0]
    m = jnp.maximum(jnp.max(s, axis=0, keepdims=True), sink)
    e = jnp.exp(s - m)
    es = jnp.exp(sink - m)
    inv = 1.0 / (jnp.sum(e, axis=0, keepdims=True) + es)
    return q2, kb, e * inv, es * inv


def _swa_fwd(q, k, v, bias_t, sink, *, name, comm=()):
    H, S, _ = q.shape
    G = H // SWA_KVH
    nb = S // BLOCK
    cur = lambda kh, qb: (kh, qb, 0)
    prev = lambda kh, qb: (kh, jnp.maximum(qb - 1, 0), 0)
    kvspec = lambda im: pl.BlockSpec((1, BLOCK, SWA_HD), im)

    def body(q_ref, kc_ref, kp_ref, vc_ref, vp_ref, bias_ref, sink_ref, o_ref):
        qb = pl.program_id(1)
        _, _, pt, _ = _swa_probs(q_ref, kp_ref, kc_ref, bias_ref, sink_ref, qb, G)
        vb = jnp.concatenate([vp_ref[0], vc_ref[0]], axis=0)
        o_ref[0, 0] = lax.dot_general(vb, pt.astype(BF16), TN, preferred_element_type=F32).astype(o_ref.dtype)

    outs, moved = _call(
        body,
        name=name,
        grid=(SWA_KVH, nb),
        in_specs=[pl.BlockSpec((G, BLOCK, SWA_HD), cur), kvspec(cur), kvspec(prev), kvspec(cur), kvspec(prev),
                  pl.BlockSpec((1, 2 * BLOCK, G * BLOCK), lambda kh, qb: (kh, 0, 0)),
                  pl.BlockSpec((1, 1, G * BLOCK), lambda kh, qb: (kh, 0, 0))],
        out_specs=[pl.BlockSpec((1, 1, SWA_HD, G * BLOCK), lambda kh, qb: (kh, qb, 0, 0))],
        out_shape=[jax.ShapeDtypeStruct((SWA_KVH, nb, SWA_HD, G * BLOCK), BF16)],
        sem=("parallel", "parallel"),
        args=(q, k, k, v, v, bias_t, sink),
        comm=comm,
    )
    return outs[0], moved


def _swa_bwd(q, k, v, bias_t, sink, do, *, name, comm=()):
    H, S, _ = q.shape
    G = H // SWA_KVH
    nb = S // BLOCK
    cur = lambda kh, qb: (kh, jnp.minimum(qb, nb - 1), 0)
    prev = lambda kh, qb: (kh, jnp.maximum(jnp.minimum(qb, nb - 1) - 1, 0), 0)
    lag = lambda kh, qb: (kh, jnp.maximum(qb - 1, 0), 0)
    kvspec = lambda im: pl.BlockSpec((1, BLOCK, SWA_HD), im)

    def body(q_ref, kc_ref, kp_ref, vc_ref, vp_ref, bias_ref, sink_ref, do_ref,
             dq_ref, dk_ref, dv_ref, dbias_ref, dsink_ref, ck_scr, cv_scr):
        qb = pl.program_id(1)

        @pl.when(qb == 0)
        def _():
            dbias_ref[...] = jnp.zeros_like(dbias_ref)
            dsink_ref[...] = jnp.zeros_like(dsink_ref)
            ck_scr[...] = jnp.zeros_like(ck_scr)
            cv_scr[...] = jnp.zeros_like(cv_scr)

        @pl.when(qb < nb)
        def _():
            q2, kb, pt, ps = _swa_probs(q_ref, kp_ref, kc_ref, bias_ref, sink_ref, qb, G)
            vb = jnp.concatenate([vp_ref[0], vc_ref[0]], axis=0)
            do2 = do_ref[...].reshape(G * BLOCK, SWA_HD)
            dpt = lax.dot_general(vb, do2, NT, preferred_element_type=F32)
            delta = jnp.sum(dpt * pt, axis=0, keepdims=True)
            dst = pt * (dpt - delta)
            dbias_ref[0] += dst
            dsink_ref[0] += -ps * delta
            dsb = (dst * SWA_SCALE).astype(BF16)
            dq_ref[0, 0] = lax.dot_general(kb, dsb, TN, preferred_element_type=F32).astype(dq_ref.dtype)
            dkb = jnp.dot(dsb, q2, preferred_element_type=F32)
            dvb = jnp.dot(pt.astype(BF16), do2, preferred_element_type=F32)
            dk_ref[0] = (ck_scr[...] + dkb[:BLOCK]).astype(dk_ref.dtype)
            dv_ref[0] = (cv_scr[...] + dvb[:BLOCK]).astype(dv_ref.dtype)
            ck_scr[...] = dkb[BLOCK:]
            cv_scr[...] = dvb[BLOCK:]

        @pl.when(qb == nb)
        def _():
            dk_ref[0] = ck_scr[...].astype(dk_ref.dtype)
            dv_ref[0] = cv_scr[...].astype(dv_ref.dtype)

    tspec = pl.BlockSpec((1, 1, SWA_HD, G * BLOCK), lambda kh, qb: (kh, jnp.minimum(qb, nb - 1), 0, 0))
    return _call(
        body,
        name=name,
        grid=(SWA_KVH, nb + 1),
        in_specs=[pl.BlockSpec((G, BLOCK, SWA_HD), cur), kvspec(cur), kvspec(prev), kvspec(cur), kvspec(prev),
                  pl.BlockSpec((1, 2 * BLOCK, G * BLOCK), lambda kh, qb: (kh, 0, 0)),
                  pl.BlockSpec((1, 1, G * BLOCK), lambda kh, qb: (kh, 0, 0)),
                  pl.BlockSpec((G, BLOCK, SWA_HD), cur)],
        out_specs=[tspec, kvspec(lag), kvspec(lag),
                   pl.BlockSpec((1, 2 * BLOCK, G * BLOCK), lambda kh, qb: (kh, 0, 0)),
                   pl.BlockSpec((1, 1, G * BLOCK), lambda kh, qb: (kh, 0, 0))],
        out_shape=[jax.ShapeDtypeStruct((SWA_KVH, nb, SWA_HD, G * BLOCK), BF16),
                   jax.ShapeDtypeStruct((SWA_KVH, S, SWA_HD), BF16),
                   jax.ShapeDtypeStruct((SWA_KVH, S, SWA_HD), BF16),
                   jax.ShapeDtypeStruct((SWA_KVH, 2 * BLOCK, G * BLOCK), F32),
                   jax.ShapeDtypeStruct((SWA_KVH, 1, G * BLOCK), F32)],
        scratch_shapes=[pltpu.VMEM((BLOCK, SWA_HD), F32), pltpu.VMEM((BLOCK, SWA_HD), F32)],
        sem=("parallel", "arbitrary"),
        args=(q, k, k, v, v, bias_t, sink, do),
        comm=comm,
    )


def _gate_mix(z, o_a, o_b, *, D, off_a, off_b, name):
    S = z.shape[0]
    tr = _tile(S, 256, 16)
    row = pl.BlockSpec((tr, D), lambda i: (i, 0))
    ca, cb = _col_view(D, off_a), _col_view(D, off_b)

    def body(ga_ref, gb_ref, oa_ref, ob_ref, m_ref):
        m = (_sigmoid(ga_ref[...].astype(F32)) * oa_ref[...].astype(F32)
             + _sigmoid(gb_ref[...].astype(F32)) * ob_ref[...].astype(F32))
        m_ref[...] = m.astype(m_ref.dtype)

    return pl.pallas_call(
        body,
        name=name,
        grid=(S // tr,),
        in_specs=[pl.BlockSpec((tr, D), lambda i: (i, ca)), pl.BlockSpec((tr, D), lambda i: (i, cb)), row, row],
        out_specs=row,
        out_shape=jax.ShapeDtypeStruct((S, D), BF16),
        compiler_params=_params("parallel"),
    )(z, z, o_a, o_b)


def _gate_mix_bwd(dm, z, o_a, o_b, *, D, off_a, off_b, name):
    S = z.shape[0]
    tr = _tile(S, 256, 16)
    row = pl.BlockSpec((tr, D), lambda i: (i, 0))
    ca, cb = _col_view(D, off_a), _col_view(D, off_b)

    def body(dm_ref, ga_ref, gb_ref, oa_ref, ob_ref, dga_ref, dgb_ref, doa_ref, dob_ref):
        d = dm_ref[...].astype(F32)
        for g_ref, o_ref, dg_ref, do_ref in ((ga_ref, oa_ref, dga_ref, doa_ref), (gb_ref, ob_ref, dgb_ref, dob_ref)):
            sg = _sigmoid(g_ref[...].astype(F32))
            dg_ref[...] = (d * o_ref[...].astype(F32) * (sg * (1.0 - sg))).astype(dg_ref.dtype)
            do_ref[...] = (d * sg).astype(do_ref.dtype)

    return pl.pallas_call(
        body,
        name=name,
        grid=(S // tr,),
        in_specs=[row, pl.BlockSpec((tr, D), lambda i: (i, ca)), pl.BlockSpec((tr, D), lambda i: (i, cb)), row, row],
        out_specs=[row] * 4,
        out_shape=[jax.ShapeDtypeStruct((S, D), BF16)] * 4,
        compiler_params=_params("parallel"),
    )(dm, z, z, o_a, o_b)


CONV_ROWS = 256
SUBLANES = 8


def _shift_matrices(tr):
    r = np.arange(tr)[:, None]
    c = np.arange(tr)[None, :]
    back = [jnp.asarray(r == c + d, dtype=BF16) for d in (1, 2)]
    ahead = [jnp.asarray(r + d == c, dtype=BF16) for d in (1, 2)]
    return back, ahead


def _rows_before(x, halo_ref, first, b1_ref, b2_ref):
    s1 = jnp.dot(b1_ref[...], x, preferred_element_type=F32)
    s2 = jnp.dot(b2_ref[...], x, preferred_element_type=F32)
    h8 = jnp.where(first, 0.0, halo_ref[...].astype(F32)[HALO - SUBLANES:])
    rows = lax.broadcasted_iota(jnp.int32, h8.shape, 0)
    fix1 = jnp.where(rows < 1, pltpu.roll(h8, 1, 0), 0.0)
    fix2 = jnp.where(rows < 2, pltpu.roll(h8, 2, 0), 0.0)
    s1 = jnp.concatenate([s1[:SUBLANES] + fix1, s1[SUBLANES:]], axis=0)
    s2 = jnp.concatenate([s2[:SUBLANES] + fix2, s2[SUBLANES:]], axis=0)
    return s1, s2


def _conv_taps(x, s1, s2, cw_ref, cb_ref):
    return cb_ref[...] + cw_ref[0:1, :] * s2 + cw_ref[1:2, :] * s1 + cw_ref[2:3, :] * x


def _conv_gate(up, cw, cb, *, name):
    S, F2 = up.shape
    F = F2 // 2
    tr = _tile(S, CONV_ROWS, HALO)
    tc = _tile(F, 512)
    nc = F // tc
    hb = tr // HALO
    back, _ = _shift_matrices(tr)
    mat = pl.BlockSpec((tr, tr), lambda i, j: (0, 0))

    def halo_map(shift):
        return lambda i, j: (jnp.maximum(i * hb - 1, 0), j + shift)

    def body(x1_ref, h1_ref, x2_ref, h2_ref, cw1_ref, cw2_ref, cb1_ref, cb2_ref, b1_ref, b2_ref, a_ref):
        first = pl.program_id(0) == 0
        us = []
        for x_ref, h_ref, cw_ref, cb_ref in ((x1_ref, h1_ref, cw1_ref, cb1_ref), (x2_ref, h2_ref, cw2_ref, cb2_ref)):
            x = x_ref[...]
            s1, s2 = _rows_before(x, h_ref, first, b1_ref, b2_ref)
            us.append(_conv_taps(x.astype(F32), s1, s2, cw_ref, cb_ref))
        u1, u2 = us
        a_ref[...] = (u1 * _sigmoid(u1) * u2).astype(a_ref.dtype)

    return pl.pallas_call(
        body,
        name=name,
        grid=(S // tr, nc),
        in_specs=[pl.BlockSpec((tr, tc), lambda i, j: (i, j)), pl.BlockSpec((HALO, tc), halo_map(0)),
                  pl.BlockSpec((tr, tc), lambda i, j: (i, j + nc)), pl.BlockSpec((HALO, tc), halo_map(nc)),
                  pl.BlockSpec((CONV_WIDTH, tc), lambda i, j: (0, j)),
                  pl.BlockSpec((CONV_WIDTH, tc), lambda i, j: (0, j + nc)),
                  pl.BlockSpec((1, tc), lambda i, j: (0, j)), pl.BlockSpec((1, tc), lambda i, j: (0, j + nc)),
                  mat, mat],
        out_specs=pl.BlockSpec((tr, tc), lambda i, j: (i, j)),
        out_shape=jax.ShapeDtypeStruct((S, F), BF16),
        compiler_params=_params("parallel", "parallel"),
    )(up, up, up, up, cw, cw, cb, cb, *back)


def _conv_gate_bwd(up, da, cw, cb, *, name, comm=()):
    S, F2 = up.shape
    F = F2 // 2
    tr = _tile(S, CONV_ROWS, HALO)
    tc = _tile(F, 512)
    nc = F // tc
    hb = tr // HALO
    ni = S // tr
    back, ahead = _shift_matrices(tr)
    mat = pl.BlockSpec((tr, tr), lambda j, r: (0, 0))

    def cur(shift):
        return lambda j, r: (ni - 1 - r, j % nc + shift)

    def before(shift):
        return lambda j, r: (jnp.maximum((ni - 1 - r) * hb - 1, 0), j % nc + shift)

    def vec(rows, shift):
        return pl.BlockSpec((rows, tc), lambda j, r: (0, j % nc + shift))

    def body(x1_ref, h1_ref, x2_ref, h2_ref, da_ref, cw1_ref, cw2_ref, cb1_ref, cb2_ref, cwo_ref,
             b1_ref, b2_ref, a1_ref, a2_ref, dup_ref, dcw_ref, dcb_ref, next_du):
        j, r = pl.program_id(0), pl.program_id(1)
        first = r == ni - 1

        @pl.when(r == 0)
        def _():
            dcw_ref[...] = jnp.zeros_like(dcw_ref)
            dcb_ref[...] = jnp.zeros_like(dcb_ref)
            next_du[...] = jnp.zeros_like(next_du)

        x1 = x1_ref[...]
        x1f = x1.astype(F32)
        s11, s12 = _rows_before(x1, h1_ref, first, b1_ref, b2_ref)
        u1 = _conv_taps(x1f, s11, s12, cw1_ref, cb1_ref)
        sg = _sigmoid(u1)
        daf = da_ref[...].astype(F32)

        def finish(du, own, own1, own2):
            du_b = du.astype(BF16)
            n1 = jnp.dot(a1_ref[...], du_b, preferred_element_type=F32)
            n2 = jnp.dot(a2_ref[...], du_b, preferred_element_type=F32)
            c8 = next_du[...]
            rows = lax.broadcasted_iota(jnp.int32, c8.shape, 0)
            fix1 = jnp.where(rows >= SUBLANES - 1, pltpu.roll(c8, SUBLANES - 1, 0), 0.0)
            fix2 = jnp.where(rows >= SUBLANES - 2, pltpu.roll(c8, SUBLANES - 2, 0), 0.0)
            n1 = jnp.concatenate([n1[:tr - SUBLANES], n1[tr - SUBLANES:] + fix1], axis=0)
            n2 = jnp.concatenate([n2[:tr - SUBLANES], n2[tr - SUBLANES:] + fix2], axis=0)
            dup = cwo_ref[2:3, :] * du + cwo_ref[1:2, :] * n1 + cwo_ref[0:1, :] * n2
            dup_ref[...] = dup.astype(dup_ref.dtype)
            dcb_ref[...] += jnp.sum(du, axis=0, keepdims=True)
            for tap, shifted in enumerate((own2, own1, own)):
                dcw_ref[tap:tap + 1, :] += jnp.sum(du * shifted, axis=0, keepdims=True)
            next_du[...] = du[:SUBLANES].astype(BF16).astype(F32)

        @pl.when(j < nc)
        def _():
            x2 = x2_ref[...]
            s21, s22 = _rows_before(x2, h2_ref, first, b1_ref, b2_ref)
            u2 = _conv_taps(x2.astype(F32), s21, s22, cw2_ref, cb2_ref)
            finish(daf * u2 * (sg * (1.0 + u1 * (1.0 - sg))), x1f, s11, s12)

        @pl.when(j >= nc)
        def _():
            x2 = x2_ref[...]
            s21, s22 = _rows_before(x2, h2_ref, first, b1_ref, b2_ref)
            finish(daf * (u1 * sg), x2.astype(F32), s21, s22)

    return _call(
        body,
        name=name,
        grid=(2 * nc, ni),
        in_specs=[pl.BlockSpec((tr, tc), cur(0)), pl.BlockSpec((HALO, tc), before(0)),
                  pl.BlockSpec((tr, tc), cur(nc)), pl.BlockSpec((HALO, tc), before(nc)),
                  pl.BlockSpec((tr, tc), cur(0)),
                  vec(CONV_WIDTH, 0), vec(CONV_WIDTH, nc), vec(1, 0), vec(1, nc),
                  pl.BlockSpec((CONV_WIDTH, tc), lambda j, r: (0, j)), mat, mat, mat, mat],
        out_specs=[pl.BlockSpec((tr, tc), lambda j, r: (ni - 1 - r, j)),
                   pl.BlockSpec((CONV_WIDTH, tc), lambda j, r: (0, j)),
                   pl.BlockSpec((1, tc), lambda j, r: (0, j))],
        out_shape=[jax.ShapeDtypeStruct((S, F2), BF16), jax.ShapeDtypeStruct((CONV_WIDTH, F2), F32),
                   jax.ShapeDtypeStruct((1, F2), F32)],
        scratch_shapes=[pltpu.VMEM((SUBLANES, tc), F32)],
        sem=("parallel", "arbitrary"),
        args=(up, up, up, up, da, cw, cw, cb, cb, cw, *back, *ahead),
        comm=comm,
    )


def _adam_math(w, g, m, v):
    m = ADAM_B1 * m + (1.0 - ADAM_B1) * g
    v = ADAM_B2 * v + (1.0 - ADAM_B2) * (g * g)
    m_hat = m / (1.0 - ADAM_B1 ** ADAM_STEP)
    v_hat = v / (1.0 - ADAM_B2 ** ADAM_STEP)
    delta = -ADAM_LR * (m_hat / (jnp.sqrt(v_hat) + ADAM_EPS) + ADAM_WD * w)
    return delta, m, v


def _adamw(w, m, v, parts, *, name):
    R, C = w.shape
    plist = list(parts) if isinstance(parts, (list, tuple)) else [parts]
    tr = _tile(min(p.shape[1] for p in plist), 256, 16)
    assert sum(p.shape[1] for p in plist) == R and all(p.shape[1] % tr == 0 for p in plist)
    row = pl.BlockSpec((tr, C), lambda i: (i, 0))
    first, spans = 0, []
    for p in plist:
        spans.append((first, first + p.shape[1] // tr))
        first = spans[-1][1]

    def body(w_ref, m_ref, v_ref, *rest):
        p_refs, (g_ref, d_ref, m2_ref, v2_ref) = rest[:len(plist)], rest[len(plist):]
        i = pl.program_id(0)

        def update(p_ref):
            g = p_ref[0].astype(F32)
            for k in range(1, N_DEV):
                g = g + p_ref[k].astype(F32)
            g_ref[...] = g
            d_ref[...], m2_ref[...], v2_ref[...] = _adam_math(w_ref[...], g, m_ref[...], v_ref[...])

        if len(plist) == 1:
            update(p_refs[0])
        else:
            for p_ref, (lo, hi) in zip(p_refs, spans):
                pl.when((i >= lo) & (i < hi))(functools.partial(update, p_ref))

    def part_spec(lo, hi):
        return pl.BlockSpec((N_DEV, tr, C), lambda i: (0, jnp.clip(i - lo, 0, hi - lo - 1), 0))

    return pl.pallas_call(
        body,
        name=name,
        grid=(R // tr,),
        in_specs=[row, row, row] + [part_spec(lo, hi) for lo, hi in spans],
        out_specs=[row] * 4,
        out_shape=[jax.ShapeDtypeStruct((R, C), F32)] * 4,
        compiler_params=_params("parallel"),
    )(w, m, v, *plist)


def _adamw_ada(w, m, v, cact_t, dmod_cols, *, name):
    R, C = w.shape
    B = cact_t.shape[1]
    tr = _tile(R, 256, 8)
    row = pl.BlockSpec((tr, C), lambda i: (i, 0))

    def body(w_ref, m_ref, v_ref, c_ref, d_ref, g_ref, dl_ref, m2_ref, v2_ref):
        g = c_ref[:, 0:1] * d_ref[0:1, :]
        for b in range(1, B):
            g = g + c_ref[:, b:b + 1] * d_ref[b:b + 1, :]
        g_ref[...] = g
        dl_ref[...], m2_ref[...], v2_ref[...] = _adam_math(w_ref[...], g, m_ref[...], v_ref[...])

    return pl.pallas_call(
        body,
        name=name,
        grid=(R // tr,),
        in_specs=[row, row, row, pl.BlockSpec((tr, B), lambda i: (i, 0)), pl.BlockSpec((B, C), lambda i: (0, 0))],
        out_specs=[row] * 4,
        out_shape=[jax.ShapeDtypeStruct((R, C), F32)] * 4,
        compiler_params=_params("parallel"),
    )(w, m, v, cact_t, dmod_cols)


def _z_layout(D, q_rank, kv_rank):
    kv = SWA_KVH * SWA_HD
    orig = {}
    o = 0
    for nm, w in (("cq", q_rank), ("ckv", kv_rank), ("kr", MLA_ROPE), ("qs", D), ("ks", kv), ("vs", kv),
                  ("ga", D), ("gb", D)):
        orig[nm] = (o, w)
        o += w
    blockw = {"cq": q_rank, "ckv": kv_rank, "kr": LANES, "qs": D, "ks": kv, "vs": kv, "ga": D, "gb": D}
    best = None
    for perm in itertools.permutations(("cq", "ckv", "ks", "vs", "kr")):
        off, new = 0, {}
        for nm in ("ga", "gb", "qs") + perm:
            off = _round_up(off, blockw[nm])
            new[nm] = off
            off += blockw[nm]
        if best is None or off < best[0]:
            best = (off, new)
    total = _round_up(best[0], 1024 if best[0] > 4096 else 512)
    return orig, best[1], blockw, total, o


def _permute_w_in(w, lay):
    orig, new, blockw, total, _ = lay
    parts, at = [], 0
    for nm in sorted(new, key=new.get):
        if new[nm] > at:
            parts.append(jnp.zeros((w.shape[0], new[nm] - at), w.dtype))
        o, wd = orig[nm]
        parts.append(w[:, o:o + wd])
        if blockw[nm] > wd:
            parts.append(jnp.zeros((w.shape[0], blockw[nm] - wd), w.dtype))
        at = new[nm] + blockw[nm]
    if total > at:
        parts.append(jnp.zeros((w.shape[0], total - at), w.dtype))
    return jnp.concatenate(parts, axis=1)


def _unpermute_w_in(wp, lay):
    orig, new, _, _, _ = lay
    return jnp.concatenate([wp[:, new[nm]:new[nm] + orig[nm][1]] for nm in sorted(orig, key=lambda n: orig[n][0])],
                           axis=1)


def _assemble_dz(parts, lay, S):
    _, new, blockw, total, _ = lay
    cols, at = [], 0
    for nm in sorted(new, key=new.get):
        if new[nm] > at:
            cols.append(jnp.zeros((S, new[nm] - at), BF16))
        cols.append(parts[nm])
        at = new[nm] + blockw[nm]
    if total > at:
        cols.append(jnp.zeros((S, total - at), BF16))
    return jnp.concatenate(cols, axis=1)


def _unshard_cols(g):
    return jnp.transpose(g, (1, 0, 2)).reshape(g.shape[1], N_DEV * g.shape[2])


def _shard_cols(w):
    K, N = w.shape
    return jnp.transpose(w.reshape(K, N_DEV, N // N_DEV), (1, 0, 2))


def _pack(vecs, rows):
    flat = jnp.concatenate([v.reshape(-1) for v in vecs])
    return jnp.pad(flat, (0, rows * LANES - flat.shape[0])).reshape(rows, LANES)


def kernel(x, c, w_ada, b_ada, g_pre_mix, g_post_mix, w_in, g_q_lat, w_uq, g_kv_lat, w_ukv, rel_bias, sinks, w_o, g_pre_ffn, g_post_ffn, w_up, conv_w, conv_b, w_down, loss_target, m_w_ada, m_b_ada, m_g_pre_mix, m_g_post_mix, m_w_in, m_g_q_lat, m_w_uq, m_g_kv_lat, m_w_ukv, m_rel_bias, m_sinks, m_w_o, m_g_pre_ffn, m_g_post_ffn, m_w_up, m_conv_w, m_conv_b, m_w_down, v_w_ada, v_b_ada, v_g_pre_mix, v_g_post_mix, v_w_in, v_g_q_lat, v_w_uq, v_g_kv_lat, v_w_ukv, v_rel_bias, v_sinks, v_w_o, v_g_pre_ffn, v_g_post_ffn, v_w_up, v_conv_w, v_conv_b, v_w_down):
    S, D = x.shape[1], x.shape[2]
    Q_RANK, KV_RANK = g_q_lat.shape[1], g_kv_lat.shape[1]
    H_MLA = D // MLA_V
    H_SWA = D // SWA_HD
    G_SWA = H_SWA // SWA_KVH
    F2 = w_up.shape[2] * N_DEV
    F = F2 // 2
    ada_n = w_ada.shape[2]
    me = 4 * lax.axis_index("x") + 2 * lax.axis_index("y") + lax.axis_index("c")
    lay = _z_layout(D, Q_RANK, KV_RANK)
    _, zoff, _, NZ, in_cols = lay
    assert in_cols == w_in.shape[2] * N_DEV

    x2, tgt = x[0], loss_target[0]

    cw_n = conv_w.shape[2]
    small = jnp.concatenate([jnp.pad(c, ((0, 7), (0, 0))), jnp.pad(conv_w[0], ((0, 8 - CONV_WIDTH), (0, 0)))], axis=1)
    small_all = _all_gather(small, name="ag_cond", in_vmem=True)
    c_all = small_all[:, 0, :D]
    cw_full = _unshard_cols(small_all[:, :CONV_WIDTH, D:])
    b_cols = lax.dynamic_slice_in_dim(b_ada, me * ada_n, ada_n, axis=1)
    c_act, mod_cols = _ada_fwd(c_all, w_ada[0], b_cols, name="ada_fwd")
    mod_all = _all_gather(mod_cols, name="ag_mod", in_vmem=True)
    mod_me = lax.dynamic_index_in_dim(mod_all, me, axis=1, keepdims=False).reshape(1, N_DEV * ada_n)
    sh1, sc1, gt1, sh2, sc2, gt2 = [mod_me[:, k * D:(k + 1) * D] for k in range(6)]

    w_in_p = _permute_w_in(_unshard_cols(_all_gather(w_in[0].astype(BF16), name="ag_w_in", in_vmem=False)), lay)

    h1 = _prenorm(x2, g_pre_mix, sc1, sh1, name="prenorm_mix")
    z, (uq_g, ukv_g, o_g) = _matmul(h1, w_in_p, mode="nn", out_dtype=BF16, name="mm_in",
                                    comm=[("gather", w_uq[0].astype(BF16)), ("gather", w_ukv[0].astype(BF16)),
                                          ("gather", w_o[0].astype(BF16))])
    w_uq_p = jnp.pad(_unshard_cols(uq_g).reshape(Q_RANK, H_MLA, MLA_QK), ((0, 0), (0, 0), (0, MLA_QK_PAD - MLA_QK))
                     ).reshape(Q_RANK, H_MLA * MLA_QK_PAD)
    w_ukv_f = _unshard_cols(ukv_g)
    w_o_f = o_g.reshape(D, D)
    cqn = _prenorm(z, g_q_lat, None, None, name="norm_cq", off=zoff["cq"], width=Q_RANK)
    ckvn = _prenorm(z, g_kv_lat, None, None, name="norm_ckv", off=zoff["ckv"], width=KV_RANK)
    q_raw = _matmul(cqn, w_uq_p, mode="nn", out_dtype=BF16, name="mm_uq")
    kv = _matmul(ckvn, w_ukv_f, mode="nn", out_dtype=BF16, name="mm_ukv")
    tab_q = _rope_tables(S, MLA_QK_PAD, MLA_NOPE)
    tab_k = _rope_tables(S, LANES, 0)
    Qr = _rope(q_raw, tab_q, heads=H_MLA, width=MLA_QK_PAD, transpose=False, name="rope_q", scale=MLA_Q_PRESCALE)
    krr = _rope(z, tab_k, heads=1, width=LANES, transpose=False, name="rope_k", off=zoff["kr"])
    Kc = _assemble_k(kv, krr, heads=H_MLA, name="assemble_k")
    (o_a, lse), (up_g,) = _flash_fwd(Qr, Kc, kv, heads=H_MLA, name="mla_fwd", comm=[("gather", w_up[0].astype(BF16))])
    w_up_f = _unshard_cols(up_g)

    bucket, valid = _t5_bucket_table()
    onehot = (jnp.asarray(bucket).reshape(-1, 1) == jnp.arange(LANES)[None, :]).astype(F32)
    rb_pad = jnp.pad(rel_bias, ((0, LANES - REL_BUCKETS), (0, LANES - H_SWA)))
    bias_t = _matmul(onehot, rb_pad, mode="nn", out_dtype=F32, name="bias_table", tm=2048, precision=HIGHEST)
    bias_full = jnp.transpose(bias_t[:, :H_SWA].reshape(BLOCK, 2 * BLOCK, H_SWA), (2, 0, 1))
    bias_full = jnp.where(jnp.asarray(valid)[None], bias_full, NEG)
    bias_full = jnp.transpose(bias_full.reshape(SWA_KVH, G_SWA, BLOCK, 2 * BLOCK), (0, 3, 1, 2)
                              ).reshape(SWA_KVH, 2 * BLOCK, G_SWA * BLOCK)
    sink_rows = jnp.broadcast_to(sinks.reshape(SWA_KVH, G_SWA, 1), (SWA_KVH, G_SWA, BLOCK)
                                 ).reshape(SWA_KVH, 1, G_SWA * BLOCK)
    kvw = SWA_KVH * SWA_HD

    def heads_first(t, n):
        return jnp.transpose(t.reshape(S, n, SWA_HD), (1, 0, 2))

    def heads_last(t):
        return jnp.transpose(t, (1, 0, 2)).reshape(S, t.shape[0] * SWA_HD)

    def queries_first(t):
        t = t.reshape(SWA_KVH, S // BLOCK, SWA_HD, G_SWA, BLOCK)
        return jnp.transpose(t, (1, 4, 0, 3, 2)).reshape(S, H_SWA * SWA_HD)

    qs_h = heads_first(z[:, zoff["qs"]:zoff["qs"] + D], H_SWA)
    ks_h = heads_first(z[:, zoff["ks"]:zoff["ks"] + kvw], SWA_KVH)
    vs_h = heads_first(z[:, zoff["vs"]:zoff["vs"] + kvw], SWA_KVH)
    o_b_h, (down_g,) = _swa_fwd(qs_h, ks_h, vs_h, bias_full, sink_rows, name="swa_fwd",
                                comm=[("gather", w_down[0].astype(BF16))])
    o_b = queries_first(o_b_h)
    w_down_f = down_g.reshape(F, D)

    mixin = _gate_mix(z, o_a, o_b, D=D, off_a=zoff["ga"], off_b=zoff["gb"], name="gate_mix")
    mix = _matmul(mixin, w_o_f, mode="nn", out_dtype=F32, name="mm_o")
    x1 = _postnorm_res(x2, mix, gt1, g_post_mix, name="postnorm_mix")

    h2 = _prenorm(x1, g_pre_ffn, sc2, sh2, name="prenorm_ffn")
    up = _matmul(h2, w_up_f, mode="nn", out_dtype=BF16, name="mm_up")
    act = _conv_gate(up, cw_full, conv_b, name="conv_gate")
    y = _matmul(act, w_down_f, mode="nn", out_dtype=F32, name="mm_down")
    loss_part, dout, dy, dgt2, dg_post_ffn = _final_loss(x1, y, tgt, gt2, g_post_ffn, name="final_loss")
    loss = lax.psum(loss_part[0, 0], ("x", "y", "c"))

    dw_down = _matmul(act, dy, mode="tn", out_dtype=BF16, name="mm_down_dw")
    dact = _matmul(dy, w_down_f, mode="nt", out_dtype=BF16, name="mm_down_dx")
    (dup, dcw, dcb), (got_down,) = _conv_gate_bwd(up, dact, cw_full, conv_b, name="conv_gate_bwd",
                                                  comm=[("scatter", dw_down.reshape(N_DEV, F // N_DEV, D))])
    dw_up = _matmul(h2, dup, mode="tn", out_dtype=BF16, name="mm_up_dw")
    dh2 = _matmul(dup, w_up_f, mode="nt", out_dtype=F32, name="mm_up_dx")
    dx1, dg_pre_ffn, dsc2, dsh2 = _prenorm_bwd(x1, dh2, dout, g_pre_ffn, sc2, name="prenorm_ffn_bwd", out_dtype=F32)

    dmix, dgt1, dg_post_mix = _postnorm_bwd(dx1, mix, gt1, g_post_mix, name="postnorm_mix_bwd")
    dw_o = _matmul(mixin, dmix, mode="tn", out_dtype=BF16, name="mm_o_dw")
    dmixin = _matmul(dmix, w_o_f, mode="nt", out_dtype=BF16, name="mm_o_dx")
    dga, dgb, do_a, do_b = _gate_mix_bwd(dmixin, z, o_a, o_b, D=D, off_a=zoff["ga"], off_b=zoff["gb"],
                                         name="gate_mix_bwd")
    dcw_parts = jnp.pad(_shard_cols(dcw), ((0, 0), (0, 16 - CONV_WIDTH), (0, 0)))
    (dqs_h, dks_h, dvs_h, dbias, dsink), (got_o, got_cw) = _swa_bwd(
        qs_h, ks_h, vs_h, bias_full, sink_rows, heads_first(do_b, H_SWA), name="swa_bwd",
        comm=[("scatter", dw_o.reshape(N_DEV, D // N_DEV, D)), ("scatter", dcw_parts)])
    dbias = jnp.transpose(dbias.reshape(SWA_KVH, 2 * BLOCK, G_SWA, BLOCK), (0, 2, 3, 1))
    drel_t = _matmul(dbias.reshape(H_SWA, BLOCK * 2 * BLOCK), onehot, mode="nn", out_dtype=F32, name="bias_grad",
                     tk=4096, precision=HIGHEST)
    d_rel_bias = jnp.transpose(drel_t[:, :REL_BUCKETS])
    d_sinks = jnp.sum(dsink.reshape(SWA_KVH, G_SWA, BLOCK), axis=-1).reshape(1, H_SWA)

    (dQ, dK, dV), (got_up,) = _flash_bwd(Qr, Kc, kv, do_a, o_a, lse, heads=H_MLA, name="mla_bwd",
                                         comm=[("scatter", _shard_cols(dw_up))])
    dq_raw = _rope(dQ, tab_q, heads=H_MLA, width=MLA_QK_PAD, transpose=True, name="rope_q_bwd",
                   scale=MLA_Q_PRESCALE)
    dkv, dkr = _assemble_k_bwd(dK, dV, tab_k, heads=H_MLA, name="assemble_k_bwd")
    dcqn = _matmul(dq_raw, w_uq_p, mode="nt", out_dtype=F32, name="mm_uq_dx")
    dw_uq_p = _matmul(cqn, dq_raw, mode="tn", out_dtype=BF16, name="mm_uq_dw")
    dckvn = _matmul(dkv, w_ukv_f, mode="nt", out_dtype=F32, name="mm_ukv_dx")
    dw_ukv = _matmul(ckvn, dkv, mode="tn", out_dtype=BF16, name="mm_ukv_dw")
    dw_uq = dw_uq_p.reshape(Q_RANK, H_MLA, MLA_QK_PAD)[:, :, :MLA_QK].reshape(Q_RANK, H_MLA * MLA_QK)
    dcq, dg_q = _prenorm_bwd(z, dcqn, None, g_q_lat, None, name="norm_cq_bwd", out_dtype=BF16,
                             off=zoff["cq"], width=Q_RANK)
    dckv, dg_kv = _prenorm_bwd(z, dckvn, None, g_kv_lat, None, name="norm_ckv_bwd", out_dtype=BF16,
                               off=zoff["ckv"], width=KV_RANK)
    dz = _assemble_dz({"ga": dga, "gb": dgb, "qs": queries_first(dqs_h), "cq": dcq, "ckv": dckv,
                       "ks": heads_last(dks_h), "vs": heads_last(dvs_h), "kr": dkr}, lay, S)
    dw_in_a, (got_uq, got_ukv) = _matmul(h1[:, :D // 2], dz, mode="tn", out_dtype=BF16, name="mm_in_dw_a",
                                         comm=[("scatter", _shard_cols(dw_uq)), ("scatter", _shard_cols(dw_ukv))])
    dw_in_b, (got_in_a,) = _matmul(h1[:, D // 2:], dz, mode="tn", out_dtype=BF16, name="mm_in_dw_b",
                                   comm=[("scatter", _shard_cols(_unpermute_w_in(dw_in_a, lay)))])
    dh1, (got_in_b,) = _matmul(dz, w_in_p, mode="nt", out_dtype=F32, name="mm_in_dx",
                               comm=[("scatter", _shard_cols(_unpermute_w_in(dw_in_b, lay)))])
    grad_x, dg_pre_mix, dsc1, dsh1 = _prenorm_bwd(x2, dh1, dx1, g_pre_mix, sc1, name="prenorm_mix_bwd",
                                                  out_dtype=F32)
    dmod = jnp.concatenate([dsh1, dsc1, dgt1, dsh2, dsc2, dgt2], axis=1)

    small_names = ["b_ada", "g_pre_mix", "g_post_mix", "g_q_lat", "g_kv_lat", "rel_bias", "sinks", "g_pre_ffn",
                   "g_post_ffn", "conv_b"]
    small_w = [b_ada, g_pre_mix, g_post_mix, g_q_lat, g_kv_lat, rel_bias, sinks, g_pre_ffn, g_post_ffn, conv_b]
    small_m = [m_b_ada, m_g_pre_mix, m_g_post_mix, m_g_q_lat, m_g_kv_lat, m_rel_bias, m_sinks, m_g_pre_ffn,
               m_g_post_ffn, m_conv_b]
    small_v = [v_b_ada, v_g_pre_mix, v_g_post_mix, v_g_q_lat, v_g_kv_lat, v_rel_bias, v_sinks, v_g_pre_ffn,
               v_g_post_ffn, v_conv_b]
    small_g = [dmod, dg_pre_mix, dg_post_mix, dg_q, dg_kv, d_rel_bias, d_sinks, dg_pre_ffn, dg_post_ffn, dcb]
    n_small = sum(int(np.prod(w.shape)) for w in small_w)
    rows = _round_up(-(-n_small // LANES), 16)
    parts_small = _all_gather(_pack(small_g, rows), name="ag_small_grads", in_vmem=True)
    sg, sd, sm, sv = _adamw(_pack(small_w, rows), _pack(small_m, rows), _pack(small_v, rows), parts_small,
                            name="adamw_small")

    def unpack(packed):
        flat, out, at = packed.reshape(-1), {}, 0
        for nm, w in zip(small_names, small_w):
            n = int(np.prod(w.shape))
            out[nm] = flat[at:at + n].reshape(w.shape)
            at += n
        return out

    small_out = [unpack(t) for t in (sg, sd, sm, sv)]

    dmod_all = parts_small.reshape(N_DEV, rows * LANES)[:, :6 * D]
    dmod_cols = lax.dynamic_slice_in_dim(dmod_all, me * ada_n, ada_n, axis=1)
    ada_out = _adamw_ada(w_ada[0], m_w_ada[0], v_w_ada[0], jnp.transpose(c_act), dmod_cols, name="adamw_w_ada")

    def owner_update(got, w, m, v, name):
        shp = w.shape
        w2, m2, v2 = (t.reshape(shp[-2], shp[-1]) for t in (w, m, v))
        return [t.reshape(shp) for t in _adamw(w2, m2, v2, got, name="adamw_" + name)]

    def pad_rows(t):
        return jnp.pad(t[0], ((0, 16 - CONV_WIDTH), (0, 0)))

    big = {
        "w_in": owner_update([got_in_a, got_in_b], w_in, m_w_in, v_w_in, "w_in"),
        "w_uq": owner_update(got_uq, w_uq, m_w_uq, v_w_uq, "w_uq"),
        "w_ukv": owner_update(got_ukv, w_ukv, m_w_ukv, v_w_ukv, "w_ukv"),
        "w_o": owner_update(got_o, w_o, m_w_o, v_w_o, "w_o"),
        "w_up": owner_update(got_up, w_up, m_w_up, v_w_up, "w_up"),
        "w_down": owner_update(got_down, w_down, m_w_down, v_w_down, "w_down"),
    }
    cw_upd = _adamw(pad_rows(conv_w), pad_rows(m_conv_w), pad_rows(v_conv_w), got_cw, name="adamw_conv_w")
    big["conv_w"] = [t[:CONV_WIDTH].reshape(conv_w.shape) for t in cw_upd]
    big["w_ada"] = [t.reshape(w_ada.shape) for t in ada_out]

    order = ["w_ada", "b_ada", "g_pre_mix", "g_post_mix", "w_in", "g_q_lat", "w_uq", "g_kv_lat", "w_ukv", "rel_bias",
             "sinks", "w_o", "g_pre_ffn", "g_post_ffn", "w_up", "conv_w", "conv_b", "w_down"]
    outs = [loss, grad_x.reshape(x.shape)]
    for kind in range(4):
        for nm in order:
            outs.append(big[nm][kind] if nm in big else small_out[kind][nm])
    return tuple(outs)
```

```python
import functools
import itertools
import math

import numpy as np

import jax
import jax.numpy as jnp
from jax import lax
from jax.experimental import pallas as pl
from jax.experimental.pallas import tpu as pltpu

F32 = jnp.float32
BF16 = jnp.bfloat16

N_DEV = 8
MLA_NOPE = 128
MLA_ROPE = 64
MLA_V = 128
MLA_QK = MLA_NOPE + MLA_ROPE
MLA_QK_PAD = 256
ROPE_HALF = MLA_ROPE // 2
ROPE_THETA = 10000.0
SWA_HD = 64
SWA_KVH = 4
WINDOW = 128
BLOCK = 128
REL_BUCKETS = 32
REL_MAX_DIST = 128
CONV_WIDTH = 3
EPS = 1e-6
NEG = -1e30
ADAM_LR = 0.001
ADAM_B1 = 0.9
ADAM_B2 = 0.999
ADAM_EPS = 1e-08
ADAM_WD = 0.01
ADAM_STEP = 10
LANES = 128
HALO = 16
MESH = pl.DeviceIdType.MESH
HIGHEST = lax.Precision.HIGHEST

NN = (((1,), (0,)), ((), ()))
NT = (((1,), (1,)), ((), ()))
TN = (((0,), (0,)), ((), ()))


def _tile(n, pref, align=LANES):
    if n <= pref:
        return n
    t = (pref // align) * align
    while t >= align:
        if n % t == 0:
            return t
        t -= align
    return n


def _round_up(n, m):
    return (n + m - 1) // m * m


def _params(*sem):
    return pltpu.CompilerParams(dimension_semantics=sem)


def _sigmoid(x):
    return 1.0 / (1.0 + jnp.exp(-x))


def _my_place():
    return lax.axis_index("x"), lax.axis_index("y"), lax.axis_index("c")


def _all_gather(x, *, name, in_vmem):
    space = pltpu.VMEM if in_vmem else pl.ANY

    def body(x_ref, out_ref, send_sems, recv_sems, local_sem):
        x_, y_, c_ = _my_place()
        me, sibling = (x_, y_, c_), (x_, y_, 1 - c_)
        chips = [(1 - x_, y_), (x_, 1 - y_), (1 - x_, 1 - y_)]

        def slot(px, py, pc):
            return out_ref.at[4 * px + 2 * py + pc]

        def copy(k, block, to, src=None):
            return pltpu.make_async_remote_copy(
                src_ref=slot(*block) if src is None else src,
                dst_ref=slot(*block),
                send_sem=send_sems.at[k],
                recv_sem=recv_sems.at[k],
                device_id=to,
                device_id_type=MESH,
            )

        mine = pltpu.make_async_copy(x_ref, slot(*me), local_sem)
        mine.start()
        first = [copy(0, me, sibling, src=x_ref)]
        first += [copy(1 + j, me, (*chip, c_), src=x_ref) for j, chip in enumerate(chips)]
        for cp in first:
            cp.start()
        passed = [copy(4 + j, (*chip, c_), sibling) for j, chip in enumerate(chips)]
        for j, chip in enumerate(chips):
            copy(1 + j, (*chip, c_), me).wait_recv()
            passed[j].start()
        copy(0, sibling, me).wait_recv()
        for j, chip in enumerate(chips):
            copy(4 + j, (*chip, 1 - c_), me).wait_recv()
        for cp in first + passed:
            cp.wait_send()
        mine.wait()

    return pl.pallas_call(
        body,
        name=name,
        out_shape=jax.ShapeDtypeStruct((N_DEV,) + x.shape, x.dtype),
        in_specs=[pl.BlockSpec(memory_space=space)],
        out_specs=pl.BlockSpec(memory_space=space),
        scratch_shapes=[
            pltpu.SemaphoreType.DMA((7,)),
            pltpu.SemaphoreType.DMA((7,)),
            pltpu.SemaphoreType.DMA,
        ],
    )(x)


class _Exchange:
    def __init__(self, kind, x_ref, out_ref, send_sems, recv_sems, local_sems, t):
        x_, y_, c_ = _my_place()
        me = 4 * x_ + 2 * y_ + c_

        def pair(k, src, dst, to):
            return pltpu.make_async_remote_copy(src_ref=src, dst_ref=dst, send_sem=send_sems.at[7 * t + k],
                                                recv_sem=recv_sems.at[7 * t + k], device_id=to, device_id_type=MESH)

        none = lambda: []
        if kind == "scatter":
            peers = [(x_ ^ ((r >> 2) & 1), y_ ^ ((r >> 1) & 1), c_ ^ (r & 1)) for r in range(1, N_DEV)]
            self.at_start = lambda: [pair(k, x_ref.at[4 * px + 2 * py + pc], out_ref.at[me], (px, py, pc))
                                     for k, (px, py, pc) in enumerate(peers)]
            self.relay_after, self.at_relay = none, none
            self.arrivals = self.at_start
            self.own = lambda: pltpu.make_async_copy(x_ref.at[me], out_ref.at[me], local_sems.at[t])
        else:
            sibling = (x_, y_, 1 - c_)
            chips = list(enumerate([(1 - x_, y_), (x_, 1 - y_), (1 - x_, 1 - y_)]))

            def slot(px, py, pc):
                return out_ref.at[4 * px + 2 * py + pc]

            mine = slot(x_, y_, c_)
            self.at_start = lambda: ([pair(0, x_ref, mine, sibling)]
                                     + [pair(1 + j, x_ref, mine, (*chip, c_)) for j, chip in chips])
            self.relay_after = lambda: [pair(1 + j, slot(*chip, c_), slot(*chip, c_), (*chip, c_)) for j, chip in chips]
            self.at_relay = lambda: [pair(4 + j, slot(*chip, c_), slot(*chip, c_), sibling) for j, chip in chips]
            self.arrivals = lambda: ([pair(0, slot(*sibling), slot(*sibling), sibling)]
                                     + [pair(4 + j, slot(*chip, 1 - c_), slot(*chip, 1 - c_), sibling)
                                        for j, chip in chips])
            self.own = lambda: pltpu.make_async_copy(x_ref, mine, local_sems.at[t])

    def start(self):
        self.own().start()
        for cp in self.at_start():
            cp.start()

    def relay(self):
        for landed, onward in zip(self.relay_after(), self.at_relay()):
            landed.wait_recv()
            onward.start()

    def finish(self):
        for cp in self.arrivals():
            cp.wait_recv()
        for cp in self.at_start() + self.at_relay():
            cp.wait_send()
        self.own().wait()


RELAY_AT = 0.7


def _call(body, *, name, grid, in_specs, out_specs, out_shape, args, scratch_shapes=(), sem=(), comm=(), prefetch=()):
    n_pf = len(prefetch)

    def launch(fn, ins, outs, shapes, scratch, semantics, operands):
        spec = pltpu.PrefetchScalarGridSpec(num_scalar_prefetch=n_pf, grid=grid, in_specs=ins, out_specs=outs,
                                            scratch_shapes=scratch)
        return pl.pallas_call(fn, name=name, grid_spec=spec, out_shape=shapes,
                              compiler_params=_params(*semantics))(*prefetch, *operands)

    if not comm:
        return list(launch(body, list(in_specs), list(out_specs), list(out_shape), list(scratch_shapes), sem, args)), []
    n_in, n_out, n_c, n_s = len(in_specs), len(out_specs), len(comm), len(scratch_shapes)
    kinds = [kind for kind, _ in comm]
    hbm = pl.BlockSpec(memory_space=pl.ANY)

    def wrapped(*refs):
        tables, refs = refs[:n_pf], refs[n_pf:]
        ins, cin = refs[:n_in], refs[n_in:n_in + n_c]
        at = n_in + n_c
        outs, cout = refs[at:at + n_out], refs[at + n_out:at + n_out + n_c]
        scr = refs[at + n_out + n_c:at + n_out + n_c + n_s]
        send, recv, local = refs[-3:]
        step = 0
        for a, g in enumerate(grid):
            step = step * g + pl.program_id(a)
        n_steps = int(np.prod(grid))

        def exchanges():
            return [_Exchange(kinds[t], cin[t], cout[t], send, recv, local, t) for t in range(n_c)]

        @pl.when(step == 0)
        def _():
            for ex in exchanges():
                ex.start()

        body(*tables, *ins, *outs, *scr)

        @pl.when(step == min(int(RELAY_AT * n_steps), n_steps - 1))
        def _():
            for ex in exchanges():
                ex.relay()

        @pl.when(step == n_steps - 1)
        def _():
            for ex in exchanges():
                ex.finish()

    c_shapes = [jax.ShapeDtypeStruct(((N_DEV,) + a.shape) if kind == "gather" else a.shape, a.dtype)
                for kind, a in comm]
    sems = [pltpu.SemaphoreType.DMA((7 * n_c,)), pltpu.SemaphoreType.DMA((7 * n_c,)), pltpu.SemaphoreType.DMA((n_c,))]
    res = launch(wrapped, list(in_specs) + [hbm] * n_c, list(out_specs) + [hbm] * n_c, list(out_shape) + c_shapes,
                 list(scratch_shapes) + sems, ["arbitrary"] * len(grid), (*args, *[a for _, a in comm]))
    return list(res[:n_out]), list(res[n_out:])


def _matmul(a, b, *, mode, out_dtype, name, tm=1024, tn=1024, tk=2816, precision=None, comm=(), shard_out=False):
    if mode == "nn":
        (M, K), (K2, N) = a.shape, b.shape
    elif mode == "nt":
        (M, K), (N, K2) = a.shape, b.shape
    else:
        (K, M), (K2, N) = a.shape, b.shape
    assert K == K2, (a.shape, b.shape, mode)
    tm, tk = _tile(M, tm, LANES if mode == "tn" else 16), _tile(K, tk)
    tn = _tile(N // N_DEV, max(tn, 1408)) if shard_out else _tile(N, tn)
    nk = K // tk
    if mode == "tn":
        a_spec = pl.BlockSpec((tk, tm), lambda i, j, k: (k, i))
    else:
        a_spec = pl.BlockSpec((tm, tk), lambda i, j, k: (i, k))
    if mode == "nt":
        b_spec = pl.BlockSpec((tn, tk), lambda i, j, k: (j, k))
    else:
        b_spec = pl.BlockSpec((tk, tn), lambda i, j, k: (k, j))
    dn = {"nn": NN, "nt": NT, "tn": TN}[mode]
    if shard_out:
        per = N // N_DEV // tn
        o_spec = pl.BlockSpec((None, tm, tn), lambda i, j, k: (j // per, i, j % per))
        o_shape = jax.ShapeDtypeStruct((N_DEV, M, N // N_DEV), out_dtype)
    else:
        o_spec = pl.BlockSpec((tm, tn), lambda i, j, k: (i, j))
        o_shape = jax.ShapeDtypeStruct((M, N), out_dtype)

    def product(a_ref, b_ref):
        return lax.dot_general(a_ref[...], b_ref[...], dn, preferred_element_type=F32, precision=precision)

    def body_one(a_ref, b_ref, o_ref):
        o_ref[...] = product(a_ref, b_ref).astype(o_ref.dtype)

    def body_acc(a_ref, b_ref, o_ref, acc_ref):
        k = pl.program_id(2)

        @pl.when(k == 0)
        def _():
            acc_ref[...] = product(a_ref, b_ref)

        @pl.when(k > 0)
        def _():
            acc_ref[...] += product(a_ref, b_ref)

        @pl.when(k == nk - 1)
        def _():
            o_ref[...] = acc_ref[...].astype(o_ref.dtype)

    outs, moved = _call(
        body_one if nk == 1 else body_acc,
        name=name,
        grid=(M // tm, N // tn, nk),
        in_specs=[a_spec, b_spec],
        out_specs=[o_spec],
        out_shape=[o_shape],
        scratch_shapes=[] if nk == 1 else [pltpu.VMEM((tm, tn), F32)],
        sem=("parallel", "parallel", "arbitrary"),
        args=(a, b),
        comm=comm,
    )
    return (outs[0], moved) if comm else outs[0]


def _rstd(xf):
    return lax.rsqrt(jnp.mean(xf * xf, axis=-1, keepdims=True) + EPS)


def _col_view(width, off):
    assert off % width == 0
    return off // width


def _prenorm(x, g, sc, sh, *, name, off=0, width=None):
    S = x.shape[0]
    W = x.shape[1] if width is None else width
    cb = _col_view(W, off)
    tr = _tile(S, 512, 16)
    mod = sc is not None
    vec = pl.BlockSpec((1, W), lambda i: (0, 0))

    def body(*refs):
        if mod:
            x_ref, g_ref, sc_ref, sh_ref, o_ref = refs
        else:
            x_ref, g_ref, o_ref = refs
        xf = x_ref[...].astype(F32)
        y = xf * _rstd(xf) * g_ref[...]
        if mod:
            y = y * (1.0 + sc_ref[...]) + sh_ref[...]
        o_ref[...] = y.astype(o_ref.dtype)

    args = (x, g, sc, sh) if mod else (x, g)
    return pl.pallas_call(
        body,
        name=name,
        grid=(S // tr,),
        in_specs=[pl.BlockSpec((tr, W), lambda i: (i, cb))] + [vec] * (len(args) - 1),
        out_specs=pl.BlockSpec((tr, W), lambda i: (i, 0)),
        out_shape=jax.ShapeDtypeStruct((S, W), BF16),
        compiler_params=_params("parallel"),
    )(*args)


def _prenorm_bwd(x, dh, dres, g, sc, *, name, out_dtype, off=0, width=None):
    S = x.shape[0]
    W = x.shape[1] if width is None else width
    cb = _col_view(W, off)
    tr = _tile(S, 256, 16)
    mod = sc is not None
    res = dres is not None
    vec = pl.BlockSpec((1, W), lambda i: (0, 0))
    row = pl.BlockSpec((tr, W), lambda i: (i, 0))

    def body(*refs):
        it = iter(refs)
        x_ref, dh_ref = next(it), next(it)
        dres_ref = next(it) if res else None
        g_ref = next(it)
        sc_ref = next(it) if mod else None
        dx_ref, dg_ref = next(it), next(it)
        dsc_ref, dsh_ref = (next(it), next(it)) if mod else (None, None)
        i = pl.program_id(0)

        @pl.when(i == 0)
        def _():
            dg_ref[...] = jnp.zeros_like(dg_ref)
            if mod:
                dsc_ref[...] = jnp.zeros_like(dsc_ref)
                dsh_ref[...] = jnp.zeros_like(dsh_ref)

        xf = x_ref[...].astype(F32)
        r = _rstd(xf)
        xn = xf * r
        dhf = dh_ref[...].astype(F32)
        gv = g_ref[...]
        if mod:
            one_sc = 1.0 + sc_ref[...]
            dsh_ref[...] += jnp.sum(dhf, axis=0, keepdims=True)
            dsc_ref[...] += jnp.sum(dhf * (xn * gv), axis=0, keepdims=True)
            dg_ref[...] += jnp.sum(dhf * xn * one_sc, axis=0, keepdims=True)
            dxn = dhf * (gv * one_sc)
        else:
            dg_ref[...] += jnp.sum(dhf * xn, axis=0, keepdims=True)
            dxn = dhf * gv
        dx = r * (dxn - xn * jnp.mean(dxn * xn, axis=-1, keepdims=True))
        if res:
            dx = dx + dres_ref[...]
        dx_ref[...] = dx.astype(dx_ref.dtype)

    args = [x, dh] + ([dres] if res else []) + [g] + ([sc] if mod else [])
    in_specs = [pl.BlockSpec((tr, W), lambda i: (i, cb)), row] + ([row] if res else []) + [vec] + ([vec] if mod else [])
    n_vec = 3 if mod else 1
    outs = pl.pallas_call(
        body,
        name=name,
        grid=(S // tr,),
        in_specs=in_specs,
        out_specs=[row] + [vec] * n_vec,
        out_shape=[jax.ShapeDtypeStruct((S, W), out_dtype)] + [jax.ShapeDtypeStruct((1, W), F32)] * n_vec,
        compiler_params=_params("arbitrary"),
    )(*args)
    return outs


def _postnorm_res(x, y, gt, g, *, name):
    S, D = x.shape
    tr = _tile(S, 512, 8)
    row = pl.BlockSpec((tr, D), lambda i: (i, 0))
    vec = pl.BlockSpec((1, D), lambda i: (0, 0))

    def body(x_ref, y_ref, gt_ref, g_ref, o_ref):
        yf = y_ref[...]
        o_ref[...] = x_ref[...] + gt_ref[...] * (yf * _rstd(yf) * g_ref[...])

    return pl.pallas_call(
        body,
        name=name,
        grid=(S // tr,),
        in_specs=[row, row, vec, vec],
        out_specs=row,
        out_shape=jax.ShapeDtypeStruct((S, D), F32),
        compiler_params=_params("parallel"),
    )(x, y, gt, g)


def _postnorm_bwd(dx1, y, gt, g, *, name):
    S, D = y.shape
    tr = _tile(S, 256, 16)
    row = pl.BlockSpec((tr, D), lambda i: (i, 0))
    vec = pl.BlockSpec((1, D), lambda i: (0, 0))

    def body(dx_ref, y_ref, gt_ref, g_ref, dy_ref, dgt_ref, dg_ref):
        @pl.when(pl.program_id(0) == 0)
        def _():
            dgt_ref[...] = jnp.zeros_like(dgt_ref)
            dg_ref[...] = jnp.zeros_like(dg_ref)

        yf = y_ref[...]
        r = _rstd(yf)
        yn = yf * r
        d = dx_ref[...]
        gtv, gv = gt_ref[...], g_ref[...]
        dgt_ref[...] += jnp.sum(d * (yn * gv), axis=0, keepdims=True)
        dg_ref[...] += jnp.sum(d * gtv * yn, axis=0, keepdims=True)
        dyn = d * (gtv * gv)
        dy_ref[...] = (r * (dyn - yn * jnp.mean(dyn * yn, axis=-1, keepdims=True))).astype(dy_ref.dtype)

    return pl.pallas_call(
        body,
        name=name,
        grid=(S // tr,),
        in_specs=[row, row, vec, vec],
        out_specs=[row, vec, vec],
        out_shape=[jax.ShapeDtypeStruct((S, D), BF16), jax.ShapeDtypeStruct((1, D), F32),
                   jax.ShapeDtypeStruct((1, D), F32)],
        compiler_params=_params("arbitrary"),
    )(dx1, y, gt, g)


def _final_loss(x1, y, target, gt, g, *, name):
    S, D = y.shape
    tr = _tile(S, 256, 16)
    row = pl.BlockSpec((tr, D), lambda i: (i, 0))
    vec = pl.BlockSpec((1, D), lambda i: (0, 0))
    one = pl.BlockSpec((1, LANES), lambda i: (0, 0))

    def body(x_ref, y_ref, t_ref, gt_ref, g_ref, loss_ref, dout_ref, dy_ref, dgt_ref, dg_ref):
        @pl.when(pl.program_id(0) == 0)
        def _():
            loss_ref[...] = jnp.zeros_like(loss_ref)
            dgt_ref[...] = jnp.zeros_like(dgt_ref)
            dg_ref[...] = jnp.zeros_like(dg_ref)

        yf = y_ref[...]
        r = _rstd(yf)
        yn = yf * r
        gtv, gv = gt_ref[...], g_ref[...]
        out = x_ref[...] + gtv * (yn * gv)
        diff = out - t_ref[...]
        per_tok = jnp.mean(diff * diff, axis=-1, keepdims=True)
        loss_ref[...] += 0.5 * jnp.sum(per_tok, axis=0, keepdims=True)
        d = diff / D
        dout_ref[...] = d
        dgt_ref[...] += jnp.sum(d * (yn * gv), axis=0, keepdims=True)
        dg_ref[...] += jnp.sum(d * gtv * yn, axis=0, keepdims=True)
        dyn = d * (gtv * gv)
        dy_ref[...] = (r * (dyn - yn * jnp.mean(dyn * yn, axis=-1, keepdims=True))).astype(dy_ref.dtype)

    return pl.pallas_call(
        body,
        name=name,
        grid=(S // tr,),
        in_specs=[row, row, row, vec, vec],
        out_specs=[one, row, row, vec, vec],
        out_shape=[jax.ShapeDtypeStruct((1, LANES), F32), jax.ShapeDtypeStruct((S, D), F32),
                   jax.ShapeDtypeStruct((S, D), BF16), jax.ShapeDtypeStruct((1, D), F32),
                   jax.ShapeDtypeStruct((1, D), F32)],
        compiler_params=_params("arbitrary"),
    )(x1, y, target, gt, g)


def _ada_fwd(c_all, w_local, b_cols, *, name):
    B, D = c_all.shape
    N = w_local.shape[1]
    tn = _tile(N, 512)

    def body(c_ref, w_ref, b_ref, ca_ref, mod_ref):
        cv = c_ref[...]
        ca = cv * _sigmoid(cv)
        ca_ref[...] = ca
        mod_ref[...] = jnp.dot(ca, w_ref[...], preferred_element_type=F32, precision=HIGHEST) + b_ref[...]

    return pl.pallas_call(
        body,
        name=name,
        grid=(N // tn,),
        in_specs=[pl.BlockSpec((B, D), lambda j: (0, 0)), pl.BlockSpec((D, tn), lambda j: (0, j)),
                  pl.BlockSpec((1, tn), lambda j: (0, j))],
        out_specs=[pl.BlockSpec((B, D), lambda j: (0, 0)), pl.BlockSpec((B, tn), lambda j: (0, j))],
        out_shape=[jax.ShapeDtypeStruct((B, D), F32), jax.ShapeDtypeStruct((B, N), F32)],
        compiler_params=_params("arbitrary"),
    )(c_all, w_local, b_cols)


def _rope_tables(S, width, lane_off):
    pos = jnp.arange(S, dtype=F32)
    inv = ROPE_THETA ** (-jnp.arange(0, MLA_ROPE, 2, dtype=F32) / MLA_ROPE)
    ang = pos[:, None] * inv[None, :]
    ang = jnp.concatenate([ang, ang], axis=-1)
    cos, sin = jnp.cos(ang), jnp.sin(ang)
    first = (jnp.arange(MLA_ROPE) < ROPE_HALF)[None, :]
    sa = jnp.where(first, -sin, 0.0)
    sb = jnp.where(first, 0.0, sin)

    def place(t, fill):
        return jnp.pad(t, ((0, 0), (lane_off, width - lane_off - MLA_ROPE)), constant_values=fill)

    return place(cos, 1.0), place(sa, 0.0), place(sb, 0.0)


def _rope_apply(x, cos, sa, sb, width, transpose):
    if transpose:
        return x * cos + pltpu.roll(x * sa, ROPE_HALF, 1) + pltpu.roll(x * sb, width - ROPE_HALF, 1)
    return x * cos + pltpu.roll(x, width - ROPE_HALF, 1) * sa + pltpu.roll(x, ROPE_HALF, 1) * sb


def _rope(x, tables, *, heads, width, transpose, name, off=0, scale=1.0):
    S = x.shape[0]
    cb = _col_view(width, off)
    tr = _tile(S, 512, 16)
    tab = pl.BlockSpec((tr, width), lambda i, h: (i, 0))

    def body(x_ref, c_ref, sa_ref, sb_ref, o_ref):
        y = _rope_apply(x_ref[...].astype(F32), c_ref[...], sa_ref[...], sb_ref[...], width, transpose)
        o_ref[...] = (y if scale == 1.0 else y * scale).astype(o_ref.dtype)

    return pl.pallas_call(
        body,
        name=name,
        grid=(S // tr, heads),
        in_specs=[pl.BlockSpec((tr, width), lambda i, h: (i, cb + h)), tab, tab, tab],
        out_specs=pl.BlockSpec((tr, width), lambda i, h: (i, h)),
        out_shape=jax.ShapeDtypeStruct((S, heads * width), BF16),
        compiler_params=_params("parallel", "parallel"),
    )(x, *tables)


def _assemble_k(kv, krr, *, heads, name):
    S = kv.shape[0]
    tr = _tile(S, 512, 16)

    def body(kn_ref, kr_ref, o_ref):
        o_ref[:, :MLA_NOPE] = kn_ref[...]
        o_ref[:, MLA_NOPE:] = kr_ref[...]

    return pl.pallas_call(
        body,
        name=name,
        grid=(S // tr, heads),
        in_specs=[pl.BlockSpec((tr, MLA_NOPE), lambda i, h: (i, 2 * h)),
                  pl.BlockSpec((tr, LANES), lambda i, h: (i, 0))],
        out_specs=pl.BlockSpec((tr, MLA_QK_PAD), lambda i, h: (i, h)),
        out_shape=jax.ShapeDtypeStruct((S, heads * MLA_QK_PAD), BF16),
        compiler_params=_params("parallel", "parallel"),
    )(kv, krr)


def _assemble_k_bwd(dK, dV, tables, *, heads, name):
    S = dK.shape[0]
    tr = _tile(S, 512, 16)
    tab = pl.BlockSpec((tr, LANES), lambda i, h: (i, 0))

    def body(dk_ref, dv_ref, c_ref, sa_ref, sb_ref, dkv_ref, dkr_ref, acc_ref):
        h = pl.program_id(1)

        @pl.when(h == 0)
        def _():
            acc_ref[...] = jnp.zeros_like(acc_ref)

        dkv_ref[:, :MLA_NOPE] = dk_ref[:, :MLA_NOPE]
        dkv_ref[:, MLA_NOPE:] = dv_ref[...]
        acc_ref[...] += dk_ref[:, MLA_NOPE:].astype(F32)

        @pl.when(h == heads - 1)
        def _():
            dkr_ref[...] = _rope_apply(acc_ref[...], c_ref[...], sa_ref[...], sb_ref[...], LANES,
                                       True).astype(dkr_ref.dtype)

    return pl.pallas_call(
        body,
        name=name,
        grid=(S // tr, heads),
        in_specs=[pl.BlockSpec((tr, MLA_QK_PAD), lambda i, h: (i, h)),
                  pl.BlockSpec((tr, MLA_V), lambda i, h: (i, h)), tab, tab, tab],
        out_specs=[pl.BlockSpec((tr, MLA_QK_PAD), lambda i, h: (i, h)),
                   pl.BlockSpec((tr, LANES), lambda i, h: (i, 0))],
        out_shape=[jax.ShapeDtypeStruct((S, heads * MLA_QK_PAD), BF16), jax.ShapeDtypeStruct((S, LANES), BF16)],
        scratch_shapes=[pltpu.VMEM((tr, LANES), F32)],
        compiler_params=_params("parallel", "arbitrary"),
    )(dK, dV, *tables)


MLA_SCALE = MLA_QK ** -0.5
LOG2E = math.log2(math.e)
LN2 = math.log(2.0)
MLA_Q_PRESCALE = MLA_SCALE * LOG2E


def _lane_tile(v, n):
    return v if n == LANES else jnp.tile(v, (1, n // LANES))


def _causal_mask(s):
    rows = lax.broadcasted_iota(jnp.int32, s.shape, 0)
    cols = lax.broadcasted_iota(jnp.int32, s.shape, 1)
    return jnp.where(cols <= rows, s, NEG)


def _tri_blocks(nb, q_major):
    if q_major:
        pairs = [(q, k) for q in range(nb) for k in range(q + 1)]
    else:
        pairs = [(q, k) for k in range(nb) for q in range(k, nb)]
    return (jnp.asarray(np.array([p[0] for p in pairs], np.int32)),
            jnp.asarray(np.array([p[1] for p in pairs], np.int32)))


HEAD_PAIR = 2


def _flash_fwd(Q, K, KV, *, heads, name, comm=()):
    S = Q.shape[0]
    t = _tile(S, 512)
    nb = S // t
    qt, kt = _tri_blocks(nb, True)
    qw, vw = HEAD_PAIR * MLA_QK_PAD, HEAD_PAIR * MLA_V

    def body(qt_ref, kt_ref, q_ref, k_ref, v0_ref, v1_ref, o_ref, lse_ref, m_scr, l_scr, acc_scr):
        step_id = pl.program_id(1)
        qi, ki = qt_ref[step_id], kt_ref[step_id]

        @pl.when(ki == 0)
        def _():
            m_scr[...] = jnp.full_like(m_scr, NEG)
            l_scr[...] = jnp.zeros_like(l_scr)
            acc_scr[...] = jnp.zeros_like(acc_scr)

        def step(diagonal):
            for h, v_ref in enumerate((v0_ref, v1_ref)):
                cols = slice(h * MLA_QK_PAD, (h + 1) * MLA_QK_PAD)
                s = lax.dot_general(q_ref[:, cols], k_ref[:, cols], NT, preferred_element_type=F32)
                if diagonal:
                    s = _causal_mask(s)
                m_prev = m_scr[h]
                m_new = jnp.maximum(m_prev, jnp.max(s, axis=1, keepdims=True))
                alpha = jnp.exp2(m_prev - m_new)
                p = jnp.exp2(s - _lane_tile(m_new, t))
                l_new = alpha * l_scr[h] + jnp.sum(p, axis=1, keepdims=True)
                acc = alpha * acc_scr[h] + jnp.dot(p.astype(BF16), v_ref[...], preferred_element_type=F32)
                if diagonal:
                    o_ref[:, h * MLA_V:(h + 1) * MLA_V] = (acc / l_new).astype(o_ref.dtype)
                    lse_ref[h] = m_new + jnp.log(l_new) * LOG2E
                else:
                    l_scr[h], acc_scr[h], m_scr[h] = l_new, acc, m_new

        pl.when(ki < qi)(lambda: step(False))
        pl.when(ki == qi)(lambda: step(True))

    def vspec(h):
        return pl.BlockSpec((t, MLA_V), lambda hp, s, qt, kt: (kt[s], 2 * (HEAD_PAIR * hp + h) + 1))

    return _call(
        body,
        name=name,
        grid=(heads // HEAD_PAIR, int(qt.shape[0])),
        in_specs=[pl.BlockSpec((t, qw), lambda hp, s, qt, kt: (qt[s], hp)),
                  pl.BlockSpec((t, qw), lambda hp, s, qt, kt: (kt[s], hp)), vspec(0), vspec(1)],
        out_specs=[pl.BlockSpec((t, vw), lambda hp, s, qt, kt: (qt[s], hp)),
                   pl.BlockSpec((HEAD_PAIR, t, LANES), lambda hp, s, qt, kt: (hp, qt[s], 0))],
        out_shape=[jax.ShapeDtypeStruct((S, heads * MLA_V), BF16),
                   jax.ShapeDtypeStruct((heads, S, LANES), F32)],
        scratch_shapes=[pltpu.VMEM((HEAD_PAIR, t, LANES), F32), pltpu.VMEM((HEAD_PAIR, t, LANES), F32),
                        pltpu.VMEM((HEAD_PAIR, t, MLA_V), F32)],
        sem=("parallel", "arbitrary"),
        args=(Q, K, KV, KV),
        comm=comm,
        prefetch=(qt, kt),
    )


def _flash_bwd(Q, K, KV, dO, O, lse, *, heads, name, comm=()):
    S = Q.shape[0]
    t = _tile(S, 512)
    nb = S // t
    qt, kt = _tri_blocks(nb, False)
    n_steps = int(qt.shape[0])
    qw, vw = HEAD_PAIR * MLA_QK_PAD, HEAD_PAIR * MLA_V

    def body(qt_ref, kt_ref, q_ref, k_ref, v0_ref, v1_ref, do_ref, o_ref, lse_ref,
             dq_ref, dk_ref, dv_ref, dq_scr, dk_scr, dv_scr, delta_scr):
        step_id = pl.program_id(1)
        qi, ki = qt_ref[step_id], kt_ref[step_id]

        @pl.when(step_id == 0)
        def _():
            dq_scr[...] = jnp.zeros_like(dq_scr)

        @pl.when(ki == 0)
        def _():
            for h in range(HEAD_PAIR):
                vc = slice(h * MLA_V, (h + 1) * MLA_V)
                d = jnp.sum(do_ref[:, vc].astype(F32) * o_ref[:, vc].astype(F32), axis=1, keepdims=True)
                delta_scr[h, qi] = jnp.broadcast_to(d, (t, LANES))

        def step(diagonal):
            for h, v_ref in enumerate((v0_ref, v1_ref)):
                cols = slice(h * MLA_QK_PAD, (h + 1) * MLA_QK_PAD)
                vc = slice(h * MLA_V, (h + 1) * MLA_V)
                q, k, do = q_ref[:, cols], k_ref[:, cols], do_ref[:, vc]
                s = lax.dot_general(q, k, NT, preferred_element_type=F32)
                if diagonal:
                    s = _causal_mask(s)
                p = jnp.exp2(s - _lane_tile(lse_ref[h], t))
                dv = lax.dot_general(p.astype(BF16), do, TN, preferred_element_type=F32)
                dp = lax.dot_general(do, v_ref[...], NT, preferred_element_type=F32)
                ds = (p * (dp - _lane_tile(delta_scr[h, qi], t))).astype(BF16)
                dk = lax.dot_general(ds, q, TN, preferred_element_type=F32)
                dq_scr[qi, :, cols] += jnp.dot(ds, k, preferred_element_type=F32)
                if diagonal:
                    dk_scr[h], dv_scr[h] = dk, dv
                else:
                    dk_scr[h] += dk
                    dv_scr[h] += dv

        pl.when(qi > ki)(lambda: step(False))
        pl.when(qi == ki)(lambda: step(True))

        @pl.when(qi == nb - 1)
        def _():
            for h in range(HEAD_PAIR):
                dk_ref[:, h * MLA_QK_PAD:(h + 1) * MLA_QK_PAD] = (dk_scr[h] * LN2).astype(dk_ref.dtype)
                dv_ref[:, h * MLA_V:(h + 1) * MLA_V] = dv_scr[h].astype(dv_ref.dtype)

        @pl.when(step_id == n_steps - 1)
        def _():
            for b in range(nb):
                dq_ref[b * t:(b + 1) * t, :] = (dq_scr[b] * LN2).astype(dq_ref.dtype)

    def vspec(h):
        return pl.BlockSpec((t, MLA_V), lambda hp, s, qt, kt: (kt[s], 2 * (HEAD_PAIR * hp + h) + 1))

    qrow = lambda hp, s, qt, kt: (qt[s], hp)
    krow = lambda hp, s, qt, kt: (kt[s], hp)
    return _call(
        body,
        name=name,
        grid=(heads // HEAD_PAIR, n_steps),
        in_specs=[pl.BlockSpec((t, qw), qrow), pl.BlockSpec((t, qw), krow), vspec(0), vspec(1),
                  pl.BlockSpec((t, vw), qrow), pl.BlockSpec((t, vw), qrow),
                  pl.BlockSpec((HEAD_PAIR, t, LANES), lambda hp, s, qt, kt: (hp, qt[s], 0))],
        out_specs=[pl.BlockSpec((S, qw), lambda hp, s, qt, kt: (0, hp)),
                   pl.BlockSpec((t, qw), krow), pl.BlockSpec((t, vw), krow)],
        out_shape=[jax.ShapeDtypeStruct((S, heads * MLA_QK_PAD), BF16),
                   jax.ShapeDtypeStruct((S, heads * MLA_QK_PAD), BF16),
                   jax.ShapeDtypeStruct((S, heads * MLA_V), BF16)],
        scratch_shapes=[pltpu.VMEM((nb, t, qw), F32), pltpu.VMEM((HEAD_PAIR, t, MLA_QK_PAD), F32),
                        pltpu.VMEM((HEAD_PAIR, t, MLA_V), F32), pltpu.VMEM((HEAD_PAIR, nb, t, LANES), F32)],
        sem=("parallel", "arbitrary"),
        args=(Q, K, KV, KV, dO, O, lse),
        comm=comm,
        prefetch=(qt, kt),
    )


SWA_SCALE = SWA_HD ** -0.5


def _t5_bucket_table():
    a = np.arange(BLOCK)[:, None]
    j = np.arange(2 * BLOCK)[None, :]
    dist = BLOCK + a - j
    max_exact = REL_BUCKETS // 2
    n = np.maximum(dist, 0)
    large = max_exact + (np.log(np.maximum(n, 1).astype(np.float32) / np.float32(max_exact))
                         / np.float32(math.log(REL_MAX_DIST / max_exact))
                         * np.float32(REL_BUCKETS - max_exact)).astype(np.int32)
    large = np.minimum(large, REL_BUCKETS - 1)
    bucket = np.where(n < max_exact, n, large)
    valid = (dist >= 0) & (dist < WINDOW)
    return bucket.astype(np.int32), valid


def _swa_probs(q_ref, kp_ref, kc_ref, bias_ref, sink_ref, qb, G):
    q2 = q_ref[...].reshape(G * BLOCK, SWA_HD)
    kb = jnp.concatenate([kp_ref[0], kc_ref[0]], axis=0)
    s = lax.dot_general(kb, q2, NT, preferred_element_type=F32) * SWA_SCALE + bias_ref[0]
    keys = lax.broadcasted_iota(jnp.int32, s.shape, 0)
    s = jnp.where((keys >= BLOCK) | (qb > 0), s, NEG)
    sink = sink_ref[0]
    m = jnp.maximum(jnp.max(s, axis=0, keepdims=True), sink)
    e = jnp.exp(s - m)
    es = jnp.exp(sink - m)
    inv = 1.0 / (jnp.sum(e, axis=0, keepdims=True) + es)
    return q2, kb, e * inv, es * inv


def _swa_fwd(q, k, v, bias_t, sink, *, name, comm=()):
    H, S, _ = q.shape
    G = H // SWA_KVH
    nb = S // BLOCK
    cur = lambda kh, qb: (kh, qb, 0)
    prev = lambda kh, qb: (kh, jnp.maximum(qb - 1, 0), 0)
    kvspec = lambda im: pl.BlockSpec((1, BLOCK, SWA_HD), im)

    def body(q_ref, kc_ref, kp_ref, vc_ref, vp_ref, bias_ref, sink_ref, o_ref):
        qb = pl.program_id(1)
        _, _, pt, _ = _swa_probs(q_ref, kp_ref, kc_ref, bias_ref, sink_ref, qb, G)
        vb = jnp.concatenate([vp_ref[0], vc_ref[0]], axis=0)
        o_ref[0, 0] = lax.dot_general(vb, pt.astype(BF16), TN, preferred_element_type=F32).astype(o_ref.dtype)

    outs, moved = _call(
        body,
        name=name,
        grid=(SWA_KVH, nb),
        in_specs=[pl.BlockSpec((G, BLOCK, SWA_HD), cur), kvspec(cur), kvspec(prev), kvspec(cur), kvspec(prev),
                  pl.BlockSpec((1, 2 * BLOCK, G * BLOCK), lambda kh, qb: (kh, 0, 0)),
                  pl.BlockSpec((1, 1, G * BLOCK), lambda kh, qb: (kh, 0, 0))],
        out_specs=[pl.BlockSpec((1, 1, SWA_HD, G * BLOCK), lambda kh, qb: (kh, qb, 0, 0))],
        out_shape=[jax.ShapeDtypeStruct((SWA_KVH, nb, SWA_HD, G * BLOCK), BF16)],
        sem=("parallel", "parallel"),
        args=(q, k, k, v, v, bias_t, sink),
        comm=comm,
    )
    return outs[0], moved


def _swa_bwd(q, k, v, bias_t, sink, do, *, name, comm=()):
    H, S, _ = q.shape
    G = H // SWA_KVH
    nb = S // BLOCK
    cur = lambda kh, qb: (kh, jnp.minimum(qb, nb - 1), 0)
    prev = lambda kh, qb: (kh, jnp.maximum(jnp.minimum(qb, nb - 1) - 1, 0), 0)
    lag = lambda kh, qb: (kh, jnp.maximum(qb - 1, 0), 0)
    kvspec = lambda im: pl.BlockSpec((1, BLOCK, SWA_HD), im)

    def body(q_ref, kc_ref, kp_ref, vc_ref, vp_ref, bias_ref, sink_ref, do_ref,
             dq_ref, dk_ref, dv_ref, dbias_ref, dsink_ref, ck_scr, cv_scr):
        qb = pl.program_id(1)

        @pl.when(qb == 0)
        def _():
            dbias_ref[...] = jnp.zeros_like(dbias_ref)
            dsink_ref[...] = jnp.zeros_like(dsink_ref)
            ck_scr[...] = jnp.zeros_like(ck_scr)
            cv_scr[...] = jnp.zeros_like(cv_scr)

        @pl.when(qb < nb)
        def _():
            q2, kb, pt, ps = _swa_probs(q_ref, kp_ref, kc_ref, bias_ref, sink_ref, qb, G)
            vb = jnp.concatenate([vp_ref[0], vc_ref[0]], axis=0)
            do2 = do_ref[...].reshape(G * BLOCK, SWA_HD)
            dpt = lax.dot_general(vb, do2, NT, preferred_element_type=F32)
            delta = jnp.sum(dpt * pt, axis=0, keepdims=True)
            dst = pt * (dpt - delta)
            dbias_ref[0] += dst
            dsink_ref[0] += -ps * delta
            dsb = (dst * SWA_SCALE).astype(BF16)
            dq_ref[0, 0] = lax.dot_general(kb, dsb, TN, preferred_element_type=F32).astype(dq_ref.dtype)
            dkb = jnp.dot(dsb, q2, preferred_element_type=F32)
            dvb = jnp.dot(pt.astype(BF16), do2, preferred_element_type=F32)
            dk_ref[0] = (ck_scr[...] + dkb[:BLOCK]).astype(dk_ref.dtype)
            dv_ref[0] = (cv_scr[...] + dvb[:BLOCK]).astype(dv_ref.dtype)
            ck_scr[...] = dkb[BLOCK:]
            cv_scr[...] = dvb[BLOCK:]

        @pl.when(qb == nb)
        def _():
            dk_ref[0] = ck_scr[...].astype(dk_ref.dtype)
            dv_ref[0] = cv_scr[...].astype(dv_ref.dtype)

    tspec = pl.BlockSpec((1, 1, SWA_HD, G * BLOCK), lambda kh, qb: (kh, jnp.minimum(qb, nb - 1), 0, 0))
    return _call(
        body,
        name=name,
        grid=(SWA_KVH, nb + 1),
        in_specs=[pl.BlockSpec((G, BLOCK, SWA_HD), cur), kvspec(cur), kvspec(prev), kvspec(cur), kvspec(prev),
                  pl.BlockSpec((1, 2 * BLOCK, G * BLOCK), lambda kh, qb: (kh, 0, 0)),
                  pl.BlockSpec((1, 1, G * BLOCK), lambda kh, qb: (kh, 0, 0)),
                  pl.BlockSpec((G, BLOCK, SWA_HD), cur)],
        out_specs=[tspec, kvspec(lag), kvspec(lag),
                   pl.BlockSpec((1, 2 * BLOCK, G * BLOCK), lambda kh, qb: (kh, 0, 0)),
                   pl.BlockSpec((1, 1, G * BLOCK), lambda kh, qb: (kh, 0, 0))],
        out_shape=[jax.ShapeDtypeStruct((SWA_KVH, nb, SWA_HD, G * BLOCK), BF16),
                   jax.ShapeDtypeStruct((SWA_KVH, S, SWA_HD), BF16),
                   jax.ShapeDtypeStruct((SWA_KVH, S, SWA_HD), BF16),
                   jax.ShapeDtypeStruct((SWA_KVH, 2 * BLOCK, G * BLOCK), F32),
                   jax.ShapeDtypeStruct((SWA_KVH, 1, G * BLOCK), F32)],
        scratch_shapes=[pltpu.VMEM((BLOCK, SWA_HD), F32), pltpu.VMEM((BLOCK, SWA_HD), F32)],
        sem=("parallel", "arbitrary"),
        args=(q, k, k, v, v, bias_t, sink, do),
        comm=comm,
    )


def _gate_mix(z, o_a, o_b, *, D, off_a, off_b, name):
    S = z.shape[0]
    tr = _tile(S, 256, 16)
    row = pl.BlockSpec((tr, D), lambda i: (i, 0))
    ca, cb = _col_view(D, off_a), _col_view(D, off_b)

    def body(ga_ref, gb_ref, oa_ref, ob_ref, m_ref):
        m = (_sigmoid(ga_ref[...].astype(F32)) * oa_ref[...].astype(F32)
             + _sigmoid(gb_ref[...].astype(F32)) * ob_ref[...].astype(F32))
        m_ref[...] = m.astype(m_ref.dtype)

    return pl.pallas_call(
        body,
        name=name,
        grid=(S // tr,),
        in_specs=[pl.BlockSpec((tr, D), lambda i: (i, ca)), pl.BlockSpec((tr, D), lambda i: (i, cb)), row, row],
        out_specs=row,
        out_shape=jax.ShapeDtypeStruct((S, D), BF16),
        compiler_params=_params("parallel"),
    )(z, z, o_a, o_b)


def _gate_mix_bwd(dm, z, o_a, o_b, *, D, off_a, off_b, name):
    S = z.shape[0]
    tr = _tile(S, 256, 16)
    row = pl.BlockSpec((tr, D), lambda i: (i, 0))
    ca, cb = _col_view(D, off_a), _col_view(D, off_b)

    def body(dm_ref, ga_ref, gb_ref, oa_ref, ob_ref, dga_ref, dgb_ref, doa_ref, dob_ref):
        d = dm_ref[...].astype(F32)
        for g_ref, o_ref, dg_ref, do_ref in ((ga_ref, oa_ref, dga_ref, doa_ref), (gb_ref, ob_ref, dgb_ref, dob_ref)):
            sg = _sigmoid(g_ref[...].astype(F32))
            dg_ref[...] = (d * o_ref[...].astype(F32) * (sg * (1.0 - sg))).astype(dg_ref.dtype)
            do_ref[...] = (d * sg).astype(do_ref.dtype)

    return pl.pallas_call(
        body,
        name=name,
        grid=(S // tr,),
        in_specs=[row, pl.BlockSpec((tr, D), lambda i: (i, ca)), pl.BlockSpec((tr, D), lambda i: (i, cb)), row, row],
        out_specs=[row] * 4,
        out_shape=[jax.ShapeDtypeStruct((S, D), BF16)] * 4,
        compiler_params=_params("parallel"),
    )(dm, z, z, o_a, o_b)


CONV_ROWS = 256
CONV_COLS = 1408
SUBLANES = 8


def _shift_matrices(tr):
    r = np.arange(tr)[:, None]
    c = np.arange(tr)[None, :]
    back = [jnp.asarray(r == c + d, dtype=BF16) for d in (1, 2)]
    ahead = [jnp.asarray(r + d == c, dtype=BF16) for d in (1, 2)]
    return back, ahead


def _rows_before(x, halo_ref, first, b1_ref, b2_ref):
    s1 = jnp.dot(b1_ref[...], x, preferred_element_type=F32)
    s2 = jnp.dot(b2_ref[...], x, preferred_element_type=F32)
    h8 = jnp.where(first, 0.0, halo_ref[...].astype(F32)[HALO - SUBLANES:])
    rows = lax.broadcasted_iota(jnp.int32, h8.shape, 0)
    fix1 = jnp.where(rows < 1, pltpu.roll(h8, 1, 0), 0.0)
    fix2 = jnp.where(rows < 2, pltpu.roll(h8, 2, 0), 0.0)
    s1 = jnp.concatenate([s1[:SUBLANES] + fix1, s1[SUBLANES:]], axis=0)
    s2 = jnp.concatenate([s2[:SUBLANES] + fix2, s2[SUBLANES:]], axis=0)
    return s1, s2


def _conv_taps(x, s1, s2, cw_ref, cb_ref):
    return cb_ref[...] + cw_ref[0:1, :] * s2 + cw_ref[1:2, :] * s1 + cw_ref[2:3, :] * x


def _conv_gate(up, cw, cb, *, name):
    S, F2 = up.shape
    F = F2 // 2
    tr = _tile(S, CONV_ROWS, HALO)
    tc = _tile(F, CONV_COLS)
    nc = F // tc
    hb = tr // HALO
    back, _ = _shift_matrices(tr)
    mat = pl.BlockSpec((tr, tr), lambda i, j: (0, 0))

    def halo_map(shift):
        return lambda i, j: (jnp.maximum(i * hb - 1, 0), j + shift)

    def body(x1_ref, h1_ref, x2_ref, h2_ref, cw1_ref, cw2_ref, cb1_ref, cb2_ref, b1_ref, b2_ref, a_ref):
        first = pl.program_id(0) == 0
        us = []
        for x_ref, h_ref, cw_ref, cb_ref in ((x1_ref, h1_ref, cw1_ref, cb1_ref), (x2_ref, h2_ref, cw2_ref, cb2_ref)):
            x = x_ref[...]
            s1, s2 = _rows_before(x, h_ref, first, b1_ref, b2_ref)
            us.append(_conv_taps(x.astype(F32), s1, s2, cw_ref, cb_ref))
        u1, u2 = us
        a_ref[...] = (u1 * _sigmoid(u1) * u2).astype(a_ref.dtype)

    return pl.pallas_call(
        body,
        name=name,
        grid=(S // tr, nc),
        in_specs=[pl.BlockSpec((tr, tc), lambda i, j: (i, j)), pl.BlockSpec((HALO, tc), halo_map(0)),
                  pl.BlockSpec((tr, tc), lambda i, j: (i, j + nc)), pl.BlockSpec((HALO, tc), halo_map(nc)),
                  pl.BlockSpec((CONV_WIDTH, tc), lambda i, j: (0, j)),
                  pl.BlockSpec((CONV_WIDTH, tc), lambda i, j: (0, j + nc)),
                  pl.BlockSpec((1, tc), lambda i, j: (0, j)), pl.BlockSpec((1, tc), lambda i, j: (0, j + nc)),
                  mat, mat],
        out_specs=pl.BlockSpec((tr, tc), lambda i, j: (i, j)),
        out_shape=jax.ShapeDtypeStruct((S, F), BF16),
        compiler_params=_params("parallel", "parallel"),
    )(up, up, up, up, cw, cw, cb, cb, *back)


def _conv_gate_bwd(up, da, cw, cb, *, name, comm=()):
    S, F2 = up.shape
    F = F2 // 2
    tr = _tile(S, CONV_ROWS, HALO)
    tc = _tile(F, CONV_COLS)
    nc = F // tc
    hb = tr // HALO
    ni = S // tr
    back, ahead = _shift_matrices(tr)
    mat = pl.BlockSpec((tr, tr), lambda j, r: (0, 0))

    def cur(shift):
        return lambda j, r: (ni - 1 - r, j % nc + shift)

    def before(shift):
        return lambda j, r: (jnp.maximum((ni - 1 - r) * hb - 1, 0), j % nc + shift)

    def vec(rows, shift):
        return pl.BlockSpec((rows, tc), lambda j, r: (0, j % nc + shift))

    def body(x1_ref, h1_ref, x2_ref, h2_ref, da_ref, cw1_ref, cw2_ref, cb1_ref, cb2_ref, cwo_ref,
             b1_ref, b2_ref, a1_ref, a2_ref, dup_ref, dcw_ref, dcb_ref, next_du):
        j, r = pl.program_id(0), pl.program_id(1)
        first = r == ni - 1

        @pl.when(r == 0)
        def _():
            dcw_ref[...] = jnp.zeros_like(dcw_ref)
            dcb_ref[...] = jnp.zeros_like(dcb_ref)
            next_du[...] = jnp.zeros_like(next_du)

        x1 = x1_ref[...]
        x1f = x1.astype(F32)
        s11, s12 = _rows_before(x1, h1_ref, first, b1_ref, b2_ref)
        u1 = _conv_taps(x1f, s11, s12, cw1_ref, cb1_ref)
        sg = _sigmoid(u1)
        daf = da_ref[...].astype(F32)

        def finish(du, own, own1, own2):
            du_b = du.astype(BF16)
            n1 = jnp.dot(a1_ref[...], du_b, preferred_element_type=F32)
            n2 = jnp.dot(a2_ref[...], du_b, preferred_element_type=F32)
            c8 = next_du[...]
            rows = lax.broadcasted_iota(jnp.int32, c8.shape, 0)
            fix1 = jnp.where(rows >= SUBLANES - 1, pltpu.roll(c8, SUBLANES - 1, 0), 0.0)
            fix2 = jnp.where(rows >= SUBLANES - 2, pltpu.roll(c8, SUBLANES - 2, 0), 0.0)
            n1 = jnp.concatenate([n1[:tr - SUBLANES], n1[tr - SUBLANES:] + fix1], axis=0)
            n2 = jnp.concatenate([n2[:tr - SUBLANES], n2[tr - SUBLANES:] + fix2], axis=0)
            dup = cwo_ref[2:3, :] * du + cwo_ref[1:2, :] * n1 + cwo_ref[0:1, :] * n2
            dup_ref[...] = dup.astype(dup_ref.dtype)
            dcb_ref[...] += jnp.sum(du, axis=0, keepdims=True)
            for tap, shifted in enumerate((own2, own1, own)):
                dcw_ref[tap:tap + 1, :] += jnp.sum(du * shifted, axis=0, keepdims=True)
            next_du[...] = du[:SUBLANES].astype(BF16).astype(F32)

        @pl.when(j < nc)
        def _():
            x2 = x2_ref[...]
            s21, s22 = _rows_before(x2, h2_ref, first, b1_ref, b2_ref)
            u2 = _conv_taps(x2.astype(F32), s21, s22, cw2_ref, cb2_ref)
            finish(daf * u2 * (sg * (1.0 + u1 * (1.0 - sg))), x1f, s11, s12)

        @pl.when(j >= nc)
        def _():
            x2 = x2_ref[...]
            s21, s22 = _rows_before(x2, h2_ref, first, b1_ref, b2_ref)
            finish(daf * (u1 * sg), x2.astype(F32), s21, s22)

    return _call(
        body,
        name=name,
        grid=(2 * nc, ni),
        in_specs=[pl.BlockSpec((tr, tc), cur(0)), pl.BlockSpec((HALO, tc), before(0)),
                  pl.BlockSpec((tr, tc), cur(nc)), pl.BlockSpec((HALO, tc), before(nc)),
                  pl.BlockSpec((tr, tc), cur(0)),
                  vec(CONV_WIDTH, 0), vec(CONV_WIDTH, nc), vec(1, 0), vec(1, nc),
                  pl.BlockSpec((CONV_WIDTH, tc), lambda j, r: (0, j)), mat, mat, mat, mat],
        out_specs=[pl.BlockSpec((tr, tc), lambda j, r: (ni - 1 - r, j)),
                   pl.BlockSpec((CONV_WIDTH, tc), lambda j, r: (0, j)),
                   pl.BlockSpec((1, tc), lambda j, r: (0, j))],
        out_shape=[jax.ShapeDtypeStruct((S, F2), BF16), jax.ShapeDtypeStruct((CONV_WIDTH, F2), F32),
                   jax.ShapeDtypeStruct((1, F2), F32)],
        scratch_shapes=[pltpu.VMEM((SUBLANES, tc), F32)],
        sem=("parallel", "arbitrary"),
        args=(up, up, up, up, da, cw, cw, cb, cb, cw, *back, *ahead),
        comm=comm,
    )


def _adam_math(w, g, m, v):
    m = ADAM_B1 * m + (1.0 - ADAM_B1) * g
    v = ADAM_B2 * v + (1.0 - ADAM_B2) * (g * g)
    m_hat = m / (1.0 - ADAM_B1 ** ADAM_STEP)
    v_hat = v / (1.0 - ADAM_B2 ** ADAM_STEP)
    delta = -ADAM_LR * (m_hat / (jnp.sqrt(v_hat) + ADAM_EPS) + ADAM_WD * w)
    return delta, m, v


def _adamw(w, m, v, parts, *, name):
    R, C = w.shape
    plist = list(parts) if isinstance(parts, (list, tuple)) else [parts]
    tr = _tile(min(p.shape[1] for p in plist), 256, 16)
    assert sum(p.shape[1] for p in plist) == R and all(p.shape[1] % tr == 0 for p in plist)
    row = pl.BlockSpec((tr, C), lambda i: (i, 0))
    first, spans = 0, []
    for p in plist:
        spans.append((first, first + p.shape[1] // tr))
        first = spans[-1][1]

    def body(w_ref, m_ref, v_ref, *rest):
        p_refs, (g_ref, d_ref, m2_ref, v2_ref) = rest[:len(plist)], rest[len(plist):]
        i = pl.program_id(0)

        def update(p_ref):
            g = p_ref[0].astype(F32)
            for k in range(1, N_DEV):
                g = g + p_ref[k].astype(F32)
            g_ref[...] = g
            d_ref[...], m2_ref[...], v2_ref[...] = _adam_math(w_ref[...], g, m_ref[...], v_ref[...])

        if len(plist) == 1:
            update(p_refs[0])
        else:
            for p_ref, (lo, hi) in zip(p_refs, spans):
                pl.when((i >= lo) & (i < hi))(functools.partial(update, p_ref))

    def part_spec(lo, hi):
        return pl.BlockSpec((N_DEV, tr, C), lambda i: (0, jnp.clip(i - lo, 0, hi - lo - 1), 0))

    return pl.pallas_call(
        body,
        name=name,
        grid=(R // tr,),
        in_specs=[row, row, row] + [part_spec(lo, hi) for lo, hi in spans],
        out_specs=[row] * 4,
        out_shape=[jax.ShapeDtypeStruct((R, C), F32)] * 4,
        compiler_params=_params("parallel"),
    )(w, m, v, *plist)


def _adamw_ada(w, m, v, cact_t, dmod_cols, *, name):
    R, C = w.shape
    B = cact_t.shape[1]
    tr = _tile(R, 256, 8)
    row = pl.BlockSpec((tr, C), lambda i: (i, 0))

    def body(w_ref, m_ref, v_ref, c_ref, d_ref, g_ref, dl_ref, m2_ref, v2_ref):
        g = c_ref[:, 0:1] * d_ref[0:1, :]
        for b in range(1, B):
            g = g + c_ref[:, b:b + 1] * d_ref[b:b + 1, :]
        g_ref[...] = g
        dl_ref[...], m2_ref[...], v2_ref[...] = _adam_math(w_ref[...], g, m_ref[...], v_ref[...])

    return pl.pallas_call(
        body,
        name=name,
        grid=(R // tr,),
        in_specs=[row, row, row, pl.BlockSpec((tr, B), lambda i: (i, 0)), pl.BlockSpec((B, C), lambda i: (0, 0))],
        out_specs=[row] * 4,
        out_shape=[jax.ShapeDtypeStruct((R, C), F32)] * 4,
        compiler_params=_params("parallel"),
    )(w, m, v, cact_t, dmod_cols)


def _z_layout(D, q_rank, kv_rank):
    kv = SWA_KVH * SWA_HD
    orig = {}
    o = 0
    for nm, w in (("cq", q_rank), ("ckv", kv_rank), ("kr", MLA_ROPE), ("qs", D), ("ks", kv), ("vs", kv),
                  ("ga", D), ("gb", D)):
        orig[nm] = (o, w)
        o += w
    blockw = {"cq": q_rank, "ckv": kv_rank, "kr": LANES, "qs": D, "ks": kv, "vs": kv, "ga": D, "gb": D}
    best = None
    for perm in itertools.permutations(("cq", "ckv", "ks", "vs", "kr")):
        off, new = 0, {}
        for nm in ("ga", "gb", "qs") + perm:
            off = _round_up(off, blockw[nm])
            new[nm] = off
            off += blockw[nm]
        if best is None or off < best[0]:
            best = (off, new)
    total = _round_up(best[0], 1024 if best[0] > 4096 else 512)
    return orig, best[1], blockw, total, o


def _permute_w_in(w, lay):
    orig, new, blockw, total, _ = lay
    parts, at = [], 0
    for nm in sorted(new, key=new.get):
        if new[nm] > at:
            parts.append(jnp.zeros((w.shape[0], new[nm] - at), w.dtype))
        o, wd = orig[nm]
        parts.append(w[:, o:o + wd])
        if blockw[nm] > wd:
            parts.append(jnp.zeros((w.shape[0], blockw[nm] - wd), w.dtype))
        at = new[nm] + blockw[nm]
    if total > at:
        parts.append(jnp.zeros((w.shape[0], total - at), w.dtype))
    return jnp.concatenate(parts, axis=1)


def _unpermute_w_in(wp, lay):
    orig, new, _, _, _ = lay
    return jnp.concatenate([wp[:, new[nm]:new[nm] + orig[nm][1]] for nm in sorted(orig, key=lambda n: orig[n][0])],
                           axis=1)


def _assemble_dz(parts, lay, S):
    _, new, blockw, total, _ = lay
    cols, at = [], 0
    for nm in sorted(new, key=new.get):
        if new[nm] > at:
            cols.append(jnp.zeros((S, new[nm] - at), BF16))
        cols.append(parts[nm])
        at = new[nm] + blockw[nm]
    if total > at:
        cols.append(jnp.zeros((S, total - at), BF16))
    return jnp.concatenate(cols, axis=1)


def _unshard_cols(g):
    return jnp.transpose(g, (1, 0, 2)).reshape(g.shape[1], N_DEV * g.shape[2])


def _shard_cols(w):
    K, N = w.shape
    return jnp.transpose(w.reshape(K, N_DEV, N // N_DEV), (1, 0, 2))


def _pack(vecs, rows):
    flat = jnp.concatenate([v.reshape(-1) for v in vecs])
    return jnp.pad(flat, (0, rows * LANES - flat.shape[0])).reshape(rows, LANES)


def kernel(x, c, w_ada, b_ada, g_pre_mix, g_post_mix, w_in, g_q_lat, w_uq, g_kv_lat, w_ukv, rel_bias, sinks, w_o, g_pre_ffn, g_post_ffn, w_up, conv_w, conv_b, w_down, loss_target, m_w_ada, m_b_ada, m_g_pre_mix, m_g_post_mix, m_w_in, m_g_q_lat, m_w_uq, m_g_kv_lat, m_w_ukv, m_rel_bias, m_sinks, m_w_o, m_g_pre_ffn, m_g_post_ffn, m_w_up, m_conv_w, m_conv_b, m_w_down, v_w_ada, v_b_ada, v_g_pre_mix, v_g_post_mix, v_w_in, v_g_q_lat, v_w_uq, v_g_kv_lat, v_w_ukv, v_rel_bias, v_sinks, v_w_o, v_g_pre_ffn, v_g_post_ffn, v_w_up, v_conv_w, v_conv_b, v_w_down):
    S, D = x.shape[1], x.shape[2]
    Q_RANK, KV_RANK = g_q_lat.shape[1], g_kv_lat.shape[1]
    H_MLA = D // MLA_V
    H_SWA = D // SWA_HD
    G_SWA = H_SWA // SWA_KVH
    F2 = w_up.shape[2] * N_DEV
    F = F2 // 2
    ada_n = w_ada.shape[2]
    me = 4 * lax.axis_index("x") + 2 * lax.axis_index("y") + lax.axis_index("c")
    lay = _z_layout(D, Q_RANK, KV_RANK)
    _, zoff, _, NZ, in_cols = lay
    assert in_cols == w_in.shape[2] * N_DEV

    x2, tgt = x[0], loss_target[0]

    cw_n = conv_w.shape[2]
    small = jnp.concatenate([jnp.pad(c, ((0, 7), (0, 0))), jnp.pad(conv_w[0], ((0, 8 - CONV_WIDTH), (0, 0)))], axis=1)
    small_all = _all_gather(small, name="ag_cond", in_vmem=True)
    c_all = small_all[:, 0, :D]
    cw_full = _unshard_cols(small_all[:, :CONV_WIDTH, D:])
    b_cols = lax.dynamic_slice_in_dim(b_ada, me * ada_n, ada_n, axis=1)
    c_act, mod_cols = _ada_fwd(c_all, w_ada[0], b_cols, name="ada_fwd")
    mod_all = _all_gather(mod_cols, name="ag_mod", in_vmem=True)
    mod_me = lax.dynamic_index_in_dim(mod_all, me, axis=1, keepdims=False).reshape(1, N_DEV * ada_n)
    sh1, sc1, gt1, sh2, sc2, gt2 = [mod_me[:, k * D:(k + 1) * D] for k in range(6)]

    w_in_p = _permute_w_in(_unshard_cols(_all_gather(w_in[0].astype(BF16), name="ag_w_in", in_vmem=False)), lay)

    h1 = _prenorm(x2, g_pre_mix, sc1, sh1, name="prenorm_mix")
    z, (uq_g, ukv_g, o_g) = _matmul(h1, w_in_p, mode="nn", out_dtype=BF16, name="mm_in",
                                    comm=[("gather", w_uq[0].astype(BF16)), ("gather", w_ukv[0].astype(BF16)),
                                          ("gather", w_o[0].astype(BF16))])
    w_uq_p = jnp.pad(_unshard_cols(uq_g).reshape(Q_RANK, H_MLA, MLA_QK), ((0, 0), (0, 0), (0, MLA_QK_PAD - MLA_QK))
                     ).reshape(Q_RANK, H_MLA * MLA_QK_PAD)
    w_ukv_f = _unshard_cols(ukv_g)
    w_o_f = o_g.reshape(D, D)
    cqn = _prenorm(z, g_q_lat, None, None, name="norm_cq", off=zoff["cq"], width=Q_RANK)
    ckvn = _prenorm(z, g_kv_lat, None, None, name="norm_ckv", off=zoff["ckv"], width=KV_RANK)
    q_raw = _matmul(cqn, w_uq_p, mode="nn", out_dtype=BF16, name="mm_uq")
    kv = _matmul(ckvn, w_ukv_f, mode="nn", out_dtype=BF16, name="mm_ukv")
    tab_q = _rope_tables(S, MLA_QK_PAD, MLA_NOPE)
    tab_k = _rope_tables(S, LANES, 0)
    Qr = _rope(q_raw, tab_q, heads=H_MLA, width=MLA_QK_PAD, transpose=False, name="rope_q", scale=MLA_Q_PRESCALE)
    krr = _rope(z, tab_k, heads=1, width=LANES, transpose=False, name="rope_k", off=zoff["kr"])
    Kc = _assemble_k(kv, krr, heads=H_MLA, name="assemble_k")
    (o_a, lse), (up_g,) = _flash_fwd(Qr, Kc, kv, heads=H_MLA, name="mla_fwd", comm=[("gather", w_up[0].astype(BF16))])
    w_up_f = _unshard_cols(up_g)

    bucket, valid = _t5_bucket_table()
    onehot = (jnp.asarray(bucket).reshape(-1, 1) == jnp.arange(LANES)[None, :]).astype(F32)
    rb_pad = jnp.pad(rel_bias, ((0, LANES - REL_BUCKETS), (0, LANES - H_SWA)))
    bias_t = _matmul(onehot, rb_pad, mode="nn", out_dtype=F32, name="bias_table", tm=2048, precision=HIGHEST)
    bias_full = jnp.transpose(bias_t[:, :H_SWA].reshape(BLOCK, 2 * BLOCK, H_SWA), (2, 0, 1))
    bias_full = jnp.where(jnp.asarray(valid)[None], bias_full, NEG)
    bias_full = jnp.transpose(bias_full.reshape(SWA_KVH, G_SWA, BLOCK, 2 * BLOCK), (0, 3, 1, 2)
                              ).reshape(SWA_KVH, 2 * BLOCK, G_SWA * BLOCK)
    sink_rows = jnp.broadcast_to(sinks.reshape(SWA_KVH, G_SWA, 1), (SWA_KVH, G_SWA, BLOCK)
                                 ).reshape(SWA_KVH, 1, G_SWA * BLOCK)
    kvw = SWA_KVH * SWA_HD

    def heads_first(t, n):
        return jnp.transpose(t.reshape(S, n, SWA_HD), (1, 0, 2))

    def heads_last(t):
        return jnp.transpose(t, (1, 0, 2)).reshape(S, t.shape[0] * SWA_HD)

    def queries_first(t):
        t = t.reshape(SWA_KVH, S // BLOCK, SWA_HD, G_SWA, BLOCK)
        return jnp.transpose(t, (1, 4, 0, 3, 2)).reshape(S, H_SWA * SWA_HD)

    qs_h = heads_first(z[:, zoff["qs"]:zoff["qs"] + D], H_SWA)
    ks_h = heads_first(z[:, zoff["ks"]:zoff["ks"] + kvw], SWA_KVH)
    vs_h = heads_first(z[:, zoff["vs"]:zoff["vs"] + kvw], SWA_KVH)
    o_b_h, _ = _swa_fwd(qs_h, ks_h, vs_h, bias_full, sink_rows, name="swa_fwd")
    o_b = queries_first(o_b_h)

    mixin = _gate_mix(z, o_a, o_b, D=D, off_a=zoff["ga"], off_b=zoff["gb"], name="gate_mix")
    mix = _matmul(mixin, w_o_f, mode="nn", out_dtype=F32, name="mm_o")
    x1 = _postnorm_res(x2, mix, gt1, g_post_mix, name="postnorm_mix")

    h2 = _prenorm(x1, g_pre_ffn, sc2, sh2, name="prenorm_ffn")
    up, (down_g,) = _matmul(h2, w_up_f, mode="nn", out_dtype=BF16, name="mm_up",
                            comm=[("gather", w_down[0].astype(BF16))])
    w_down_f = down_g.reshape(F, D)
    act = _conv_gate(up, cw_full, conv_b, name="conv_gate")
    y = _matmul(act, w_down_f, mode="nn", out_dtype=F32, name="mm_down")
    loss_part, dout, dy, dgt2, dg_post_ffn = _final_loss(x1, y, tgt, gt2, g_post_ffn, name="final_loss")
    loss = lax.psum(loss_part[0, 0], ("x", "y", "c"))

    dw_down = _matmul(act, dy, mode="tn", out_dtype=BF16, name="mm_down_dw")
    dact = _matmul(dy, w_down_f, mode="nt", out_dtype=BF16, name="mm_down_dx")
    (dup, dcw, dcb), (got_down,) = _conv_gate_bwd(up, dact, cw_full, conv_b, name="conv_gate_bwd",
                                                  comm=[("scatter", dw_down.reshape(N_DEV, F // N_DEV, D))])
    dw_up = _matmul(h2, dup, mode="tn", out_dtype=BF16, name="mm_up_dw", shard_out=True)
    dh2 = _matmul(dup, w_up_f, mode="nt", out_dtype=F32, name="mm_up_dx")
    dx1, dg_pre_ffn, dsc2, dsh2 = _prenorm_bwd(x1, dh2, dout, g_pre_ffn, sc2, name="prenorm_ffn_bwd", out_dtype=F32)

    dmix, dgt1, dg_post_mix = _postnorm_bwd(dx1, mix, gt1, g_post_mix, name="postnorm_mix_bwd")
    dw_o = _matmul(mixin, dmix, mode="tn", out_dtype=BF16, name="mm_o_dw")
    dmixin = _matmul(dmix, w_o_f, mode="nt", out_dtype=BF16, name="mm_o_dx")
    dga, dgb, do_a, do_b = _gate_mix_bwd(dmixin, z, o_a, o_b, D=D, off_a=zoff["ga"], off_b=zoff["gb"],
                                         name="gate_mix_bwd")
    dcw_parts = jnp.pad(_shard_cols(dcw), ((0, 0), (0, 16 - CONV_WIDTH), (0, 0)))
    (dqs_h, dks_h, dvs_h, dbias, dsink), (got_o, got_cw) = _swa_bwd(
        qs_h, ks_h, vs_h, bias_full, sink_rows, heads_first(do_b, H_SWA), name="swa_bwd",
        comm=[("scatter", dw_o.reshape(N_DEV, D // N_DEV, D)), ("scatter", dcw_parts)])
    dbias = jnp.transpose(dbias.reshape(SWA_KVH, 2 * BLOCK, G_SWA, BLOCK), (0, 2, 3, 1))
    drel_t = _matmul(dbias.reshape(H_SWA, BLOCK * 2 * BLOCK), onehot, mode="nn", out_dtype=F32, name="bias_grad",
                     tk=4096, precision=HIGHEST)
    d_rel_bias = jnp.transpose(drel_t[:, :REL_BUCKETS])
    d_sinks = jnp.sum(dsink.reshape(SWA_KVH, G_SWA, BLOCK), axis=-1).reshape(1, H_SWA)

    (dQ, dK, dV), (got_up,) = _flash_bwd(Qr, Kc, kv, do_a, o_a, lse, heads=H_MLA, name="mla_bwd",
                                         comm=[("scatter", dw_up)])
    dq_raw = _rope(dQ, tab_q, heads=H_MLA, width=MLA_QK_PAD, transpose=True, name="rope_q_bwd",
                   scale=MLA_Q_PRESCALE)
    dkv, dkr = _assemble_k_bwd(dK, dV, tab_k, heads=H_MLA, name="assemble_k_bwd")
    dcqn = _matmul(dq_raw, w_uq_p, mode="nt", out_dtype=F32, name="mm_uq_dx")
    dw_uq_p = _matmul(cqn, dq_raw, mode="tn", out_dtype=BF16, name="mm_uq_dw")
    dckvn = _matmul(dkv, w_ukv_f, mode="nt", out_dtype=F32, name="mm_ukv_dx")
    dw_ukv = _matmul(ckvn, dkv, mode="tn", out_dtype=BF16, name="mm_ukv_dw", shard_out=True)
    dw_uq = dw_uq_p.reshape(Q_RANK, H_MLA, MLA_QK_PAD)[:, :, :MLA_QK].reshape(Q_RANK, H_MLA * MLA_QK)
    dcq, dg_q = _prenorm_bwd(z, dcqn, None, g_q_lat, None, name="norm_cq_bwd", out_dtype=BF16,
                             off=zoff["cq"], width=Q_RANK)
    dckv, dg_kv = _prenorm_bwd(z, dckvn, None, g_kv_lat, None, name="norm_ckv_bwd", out_dtype=BF16,
                               off=zoff["ckv"], width=KV_RANK)
    dz = _assemble_dz({"ga": dga, "gb": dgb, "qs": queries_first(dqs_h), "cq": dcq, "ckv": dckv,
                       "ks": heads_last(dks_h), "vs": heads_last(dvs_h), "kr": dkr}, lay, S)
    dw_in_a, (got_uq, got_ukv) = _matmul(h1[:, :D // 2], dz, mode="tn", out_dtype=BF16, name="mm_in_dw_a",
                                         comm=[("scatter", _shard_cols(dw_uq)), ("scatter", dw_ukv)])
    dw_in_b, (got_in_a,) = _matmul(h1[:, D // 2:], dz, mode="tn", out_dtype=BF16, name="mm_in_dw_b",
                                   comm=[("scatter", _shard_cols(_unpermute_w_in(dw_in_a, lay)))])
    dh1, (got_in_b,) = _matmul(dz, w_in_p, mode="nt", out_dtype=F32, name="mm_in_dx",
                               comm=[("scatter", _shard_cols(_unpermute_w_in(dw_in_b, lay)))])
    grad_x, dg_pre_mix, dsc1, dsh1 = _prenorm_bwd(x2, dh1, dx1, g_pre_mix, sc1, name="prenorm_mix_bwd",
                                                  out_dtype=F32)
    dmod = jnp.concatenate([dsh1, dsc1, dgt1, dsh2, dsc2, dgt2], axis=1)

    small_names = ["b_ada", "g_pre_mix", "g_post_mix", "g_q_lat", "g_kv_lat", "rel_bias", "sinks", "g_pre_ffn",
                   "g_post_ffn", "conv_b"]
    small_w = [b_ada, g_pre_mix, g_post_mix, g_q_lat, g_kv_lat, rel_bias, sinks, g_pre_ffn, g_post_ffn, conv_b]
    small_m = [m_b_ada, m_g_pre_mix, m_g_post_mix, m_g_q_lat, m_g_kv_lat, m_rel_bias, m_sinks, m_g_pre_ffn,
               m_g_post_ffn, m_conv_b]
    small_v = [v_b_ada, v_g_pre_mix, v_g_post_mix, v_g_q_lat, v_g_kv_lat, v_rel_bias, v_sinks, v_g_pre_ffn,
               v_g_post_ffn, v_conv_b]
    small_g = [dmod, dg_pre_mix, dg_post_mix, dg_q, dg_kv, d_rel_bias, d_sinks, dg_pre_ffn, dg_post_ffn, dcb]
    n_small = sum(int(np.prod(w.shape)) for w in small_w)
    rows = _round_up(-(-n_small // LANES), 16)
    parts_small = _all_gather(_pack(small_g, rows), name="ag_small_grads", in_vmem=True)
    sg, sd, sm, sv = _adamw(_pack(small_w, rows), _pack(small_m, rows), _pack(small_v, rows), parts_small,
                            name="adamw_small")

    def unpack(packed):
        flat, out, at = packed.reshape(-1), {}, 0
        for nm, w in zip(small_names, small_w):
            n = int(np.prod(w.shape))
            out[nm] = flat[at:at + n].reshape(w.shape)
            at += n
        return out

    small_out = [unpack(t) for t in (sg, sd, sm, sv)]

    dmod_all = parts_small.reshape(N_DEV, rows * LANES)[:, :6 * D]
    dmod_cols = lax.dynamic_slice_in_dim(dmod_all, me * ada_n, ada_n, axis=1)
    ada_out = _adamw_ada(w_ada[0], m_w_ada[0], v_w_ada[0], jnp.transpose(c_act), dmod_cols, name="adamw_w_ada")

    def owner_update(got, w, m, v, name):
        shp = w.shape
        w2, m2, v2 = (t.reshape(shp[-2], shp[-1]) for t in (w, m, v))
        return [t.reshape(shp) for t in _adamw(w2, m2, v2, got, name="adamw_" + name)]

    def pad_rows(t):
        return jnp.pad(t[0], ((0, 16 - CONV_WIDTH), (0, 0)))

    big = {
        "w_in": owner_update([got_in_a, got_in_b], w_in, m_w_in, v_w_in, "w_in"),
        "w_uq": owner_update(got_uq, w_uq, m_w_uq, v_w_uq, "w_uq"),
        "w_ukv": owner_update(got_ukv, w_ukv, m_w_ukv, v_w_ukv, "w_ukv"),
        "w_o": owner_update(got_o, w_o, m_w_o, v_w_o, "w_o"),
        "w_up": owner_update(got_up, w_up, m_w_up, v_w_up, "w_up"),
        "w_down": owner_update(got_down, w_down, m_w_down, v_w_down, "w_down"),
    }
    cw_upd = _adamw(pad_rows(conv_w), pad_rows(m_conv_w), pad_rows(v_conv_w), got_cw, name="adamw_conv_w")
    big["conv_w"] = [t[:CONV_WIDTH].reshape(conv_w.shape) for t in cw_upd]
    big["w_ada"] = [t.reshape(w_ada.shape) for t in ada_out]

    order = ["w_ada", "b_ada", "g_pre_mix", "g_post_mix", "w_in", "g_q_lat", "w_uq", "g_kv_lat", "w_ukv", "rel_bias",
             "sinks", "w_o", "g_pre_ffn", "g_post_ffn", "w_up", "conv_w", "conv_b", "w_down"]
    outs = [loss, grad_x.reshape(x.shape)]
    for kind in range(4):
        for nm in order:
            outs.append(big[nm][kind] if nm in big else small_out[kind][nm])
    return tuple(outs)
```

```python
import functools
import itertools
import math

import numpy as np

import jax
import jax.numpy as jnp
from jax import lax
from jax.experimental import pallas as pl
from jax.experimental.pallas import tpu as pltpu

F32 = jnp.float32
BF16 = jnp.bfloat16

N_DEV = 8
MLA_NOPE = 128
MLA_ROPE = 64
MLA_V = 128
MLA_QK = MLA_NOPE + MLA_ROPE
MLA_QK_PAD = 256
ROPE_HALF = MLA_ROPE // 2
ROPE_THETA = 10000.0
SWA_HD = 64
SWA_KVH = 4
WINDOW = 128
BLOCK = 128
REL_BUCKETS = 32
REL_MAX_DIST = 128
CONV_WIDTH = 3
EPS = 1e-6
NEG = -1e30
ADAM_LR = 0.001
ADAM_B1 = 0.9
ADAM_B2 = 0.999
ADAM_EPS = 1e-08
ADAM_WD = 0.01
ADAM_STEP = 10
LANES = 128
HALO = 16
MESH = pl.DeviceIdType.MESH
HIGHEST = lax.Precision.HIGHEST

NN = (((1,), (0,)), ((), ()))
NT = (((1,), (1,)), ((), ()))
TN = (((0,), (0,)), ((), ()))


def _tile(n, pref, align=LANES):
    if n <= pref:
        return n
    t = (pref // align) * align
    while t >= align:
        if n % t == 0:
            return t
        t -= align
    return n


def _round_up(n, m):
    return (n + m - 1) // m * m


def _params(*sem):
    return pltpu.CompilerParams(dimension_semantics=sem)


def _sigmoid(x):
    return 1.0 / (1.0 + jnp.exp(-x))


def _my_place():
    return lax.axis_index("x"), lax.axis_index("y"), lax.axis_index("c")


def _all_gather(x, *, name, in_vmem):
    space = pltpu.VMEM if in_vmem else pl.ANY

    def body(x_ref, out_ref, send_sems, recv_sems, local_sem):
        x_, y_, c_ = _my_place()
        me, sibling = (x_, y_, c_), (x_, y_, 1 - c_)
        chips = [(1 - x_, y_), (x_, 1 - y_), (1 - x_, 1 - y_)]

        def slot(px, py, pc):
            return out_ref.at[4 * px + 2 * py + pc]

        def copy(k, block, to, src=None):
            return pltpu.make_async_remote_copy(
                src_ref=slot(*block) if src is None else src,
                dst_ref=slot(*block),
                send_sem=send_sems.at[k],
                recv_sem=recv_sems.at[k],
                device_id=to,
                device_id_type=MESH,
            )

        mine = pltpu.make_async_copy(x_ref, slot(*me), local_sem)
        mine.start()
        first = [copy(0, me, sibling, src=x_ref)]
        first += [copy(1 + j, me, (*chip, c_), src=x_ref) for j, chip in enumerate(chips)]
        for cp in first:
            cp.start()
        passed = [copy(4 + j, (*chip, c_), sibling) for j, chip in enumerate(chips)]
        for j, chip in enumerate(chips):
            copy(1 + j, (*chip, c_), me).wait_recv()
            passed[j].start()
        copy(0, sibling, me).wait_recv()
        for j, chip in enumerate(chips):
            copy(4 + j, (*chip, 1 - c_), me).wait_recv()
        for cp in first + passed:
            cp.wait_send()
        mine.wait()

    return pl.pallas_call(
        body,
        name=name,
        out_shape=jax.ShapeDtypeStruct((N_DEV,) + x.shape, x.dtype),
        in_specs=[pl.BlockSpec(memory_space=space)],
        out_specs=pl.BlockSpec(memory_space=space),
        scratch_shapes=[
            pltpu.SemaphoreType.DMA((7,)),
            pltpu.SemaphoreType.DMA((7,)),
            pltpu.SemaphoreType.DMA,
        ],
    )(x)


class _Exchange:
    def __init__(self, kind, x_ref, out_ref, send_sems, recv_sems, local_sems, t):
        x_, y_, c_ = _my_place()
        me = 4 * x_ + 2 * y_ + c_

        def pair(k, src, dst, to):
            return pltpu.make_async_remote_copy(src_ref=src, dst_ref=dst, send_sem=send_sems.at[7 * t + k],
                                                recv_sem=recv_sems.at[7 * t + k], device_id=to, device_id_type=MESH)

        none = lambda: []
        if kind == "scatter":
            peers = [(x_ ^ ((r >> 2) & 1), y_ ^ ((r >> 1) & 1), c_ ^ (r & 1)) for r in range(1, N_DEV)]
            self.at_start = lambda: [pair(k, x_ref.at[4 * px + 2 * py + pc], out_ref.at[me], (px, py, pc))
                                     for k, (px, py, pc) in enumerate(peers)]
            self.relay_after, self.at_relay = none, none
            self.arrivals = self.at_start
            self.own = lambda: pltpu.make_async_copy(x_ref.at[me], out_ref.at[me], local_sems.at[t])
        else:
            sibling = (x_, y_, 1 - c_)
            chips = list(enumerate([(1 - x_, y_), (x_, 1 - y_), (1 - x_, 1 - y_)]))

            def slot(px, py, pc):
                return out_ref.at[4 * px + 2 * py + pc]

            mine = slot(x_, y_, c_)
            self.at_start = lambda: ([pair(0, x_ref, mine, sibling)]
                                     + [pair(1 + j, x_ref, mine, (*chip, c_)) for j, chip in chips])
            self.relay_after = lambda: [pair(1 + j, slot(*chip, c_), slot(*chip, c_), (*chip, c_)) for j, chip in chips]
            self.at_relay = lambda: [pair(4 + j, slot(*chip, c_), slot(*chip, c_), sibling) for j, chip in chips]
            self.arrivals = lambda: ([pair(0, slot(*sibling), slot(*sibling), sibling)]
                                     + [pair(4 + j, slot(*chip, 1 - c_), slot(*chip, 1 - c_), sibling)
                                        for j, chip in chips])
            self.own = lambda: pltpu.make_async_copy(x_ref, mine, local_sems.at[t])

    def start(self):
        self.own().start()
        for cp in self.at_start():
            cp.start()

    def relay(self):
        for landed, onward in zip(self.relay_after(), self.at_relay()):
            landed.wait_recv()
            onward.start()

    def finish(self):
        for cp in self.arrivals():
            cp.wait_recv()
        for cp in self.at_start() + self.at_relay():
            cp.wait_send()
        self.own().wait()


RELAY_AT = 0.7


def _call(body, *, name, grid, in_specs, out_specs, out_shape, args, scratch_shapes=(), sem=(), comm=(), prefetch=()):
    n_pf = len(prefetch)

    def launch(fn, ins, outs, shapes, scratch, semantics, operands):
        spec = pltpu.PrefetchScalarGridSpec(num_scalar_prefetch=n_pf, grid=grid, in_specs=ins, out_specs=outs,
                                            scratch_shapes=scratch)
        return pl.pallas_call(fn, name=name, grid_spec=spec, out_shape=shapes,
                              compiler_params=_params(*semantics))(*prefetch, *operands)

    if not comm:
        return list(launch(body, list(in_specs), list(out_specs), list(out_shape), list(scratch_shapes), sem, args)), []
    n_in, n_out, n_c, n_s = len(in_specs), len(out_specs), len(comm), len(scratch_shapes)
    kinds = [kind for kind, _ in comm]
    hbm = pl.BlockSpec(memory_space=pl.ANY)

    def wrapped(*refs):
        tables, refs = refs[:n_pf], refs[n_pf:]
        ins, cin = refs[:n_in], refs[n_in:n_in + n_c]
        at = n_in + n_c
        outs, cout = refs[at:at + n_out], refs[at + n_out:at + n_out + n_c]
        scr = refs[at + n_out + n_c:at + n_out + n_c + n_s]
        send, recv, local = refs[-3:]
        step = 0
        for a, g in enumerate(grid):
            step = step * g + pl.program_id(a)
        n_steps = int(np.prod(grid))

        def exchanges():
            return [_Exchange(kinds[t], cin[t], cout[t], send, recv, local, t) for t in range(n_c)]

        @pl.when(step == 0)
        def _():
            for ex in exchanges():
                ex.start()

        body(*tables, *ins, *outs, *scr)

        @pl.when(step == min(int(RELAY_AT * n_steps), n_steps - 1))
        def _():
            for ex in exchanges():
                ex.relay()

        @pl.when(step == n_steps - 1)
        def _():
            for ex in exchanges():
                ex.finish()

    c_shapes = [jax.ShapeDtypeStruct(((N_DEV,) + a.shape) if kind == "gather" else a.shape, a.dtype)
                for kind, a in comm]
    sems = [pltpu.SemaphoreType.DMA((7 * n_c,)), pltpu.SemaphoreType.DMA((7 * n_c,)), pltpu.SemaphoreType.DMA((n_c,))]
    res = launch(wrapped, list(in_specs) + [hbm] * n_c, list(out_specs) + [hbm] * n_c, list(out_shape) + c_shapes,
                 list(scratch_shapes) + sems, ["arbitrary"] * len(grid), (*args, *[a for _, a in comm]))
    return list(res[:n_out]), list(res[n_out:])


def _matmul(a, b, *, mode, out_dtype, name, tm=1024, tn=1024, tk=2816, precision=None, comm=(), shard_out=False):
    if mode == "nn":
        (M, K), (K2, N) = a.shape, b.shape
    elif mode == "nt":
        (M, K), (N, K2) = a.shape, b.shape
    else:
        (K, M), (K2, N) = a.shape, b.shape
    assert K == K2, (a.shape, b.shape, mode)
    tm, tk = _tile(M, tm, LANES if mode == "tn" else 16), _tile(K, tk)
    tn = _tile(N // N_DEV, max(tn, 1408)) if shard_out else _tile(N, tn)
    nk = K // tk
    if mode == "tn":
        a_spec = pl.BlockSpec((tk, tm), lambda i, j, k: (k, i))
    else:
        a_spec = pl.BlockSpec((tm, tk), lambda i, j, k: (i, k))
    if mode == "nt":
        b_spec = pl.BlockSpec((tn, tk), lambda i, j, k: (j, k))
    else:
        b_spec = pl.BlockSpec((tk, tn), lambda i, j, k: (k, j))
    dn = {"nn": NN, "nt": NT, "tn": TN}[mode]
    if shard_out:
        per = N // N_DEV // tn
        o_spec = pl.BlockSpec((None, tm, tn), lambda i, j, k: (j // per, i, j % per))
        o_shape = jax.ShapeDtypeStruct((N_DEV, M, N // N_DEV), out_dtype)
    else:
        o_spec = pl.BlockSpec((tm, tn), lambda i, j, k: (i, j))
        o_shape = jax.ShapeDtypeStruct((M, N), out_dtype)

    def product(a_ref, b_ref):
        return lax.dot_general(a_ref[...], b_ref[...], dn, preferred_element_type=F32, precision=precision)

    def body_one(a_ref, b_ref, o_ref):
        o_ref[...] = product(a_ref, b_ref).astype(o_ref.dtype)

    def body_acc(a_ref, b_ref, o_ref, acc_ref):
        k = pl.program_id(2)

        @pl.when(k == 0)
        def _():
            acc_ref[...] = product(a_ref, b_ref)

        @pl.when(k > 0)
        def _():
            acc_ref[...] += product(a_ref, b_ref)

        @pl.when(k == nk - 1)
        def _():
            o_ref[...] = acc_ref[...].astype(o_ref.dtype)

    outs, moved = _call(
        body_one if nk == 1 else body_acc,
        name=name,
        grid=(M // tm, N // tn, nk),
        in_specs=[a_spec, b_spec],
        out_specs=[o_spec],
        out_shape=[o_shape],
        scratch_shapes=[] if nk == 1 else [pltpu.VMEM((tm, tn), F32)],
        sem=("parallel", "parallel", "arbitrary"),
        args=(a, b),
        comm=comm,
    )
    return (outs[0], moved) if comm else outs[0]


def _rstd(xf):
    return lax.rsqrt(jnp.mean(xf * xf, axis=-1, keepdims=True) + EPS)


def _col_view(width, off):
    assert off % width == 0
    return off // width


def _prenorm(x, g, sc, sh, *, name, off=0, width=None):
    S = x.shape[0]
    W = x.shape[1] if width is None else width
    cb = _col_view(W, off)
    tr = _tile(S, 512, 16)
    mod = sc is not None
    vec = pl.BlockSpec((1, W), lambda i: (0, 0))

    def body(*refs):
        if mod:
            x_ref, g_ref, sc_ref, sh_ref, o_ref = refs
        else:
            x_ref, g_ref, o_ref = refs
        xf = x_ref[...].astype(F32)
        y = xf * _rstd(xf) * g_ref[...]
        if mod:
            y = y * (1.0 + sc_ref[...]) + sh_ref[...]
        o_ref[...] = y.astype(o_ref.dtype)

    args = (x, g, sc, sh) if mod else (x, g)
    return pl.pallas_call(
        body,
        name=name,
        grid=(S // tr,),
        in_specs=[pl.BlockSpec((tr, W), lambda i: (i, cb))] + [vec] * (len(args) - 1),
        out_specs=pl.BlockSpec((tr, W), lambda i: (i, 0)),
        out_shape=jax.ShapeDtypeStruct((S, W), BF16),
        compiler_params=_params("parallel"),
    )(*args)


def _prenorm_bwd(x, dh, dres, g, sc, *, name, out_dtype, off=0, width=None):
    S = x.shape[0]
    W = x.shape[1] if width is None else width
    cb = _col_view(W, off)
    tr = _tile(S, 256, 16)
    mod = sc is not None
    res = dres is not None
    vec = pl.BlockSpec((1, W), lambda i: (0, 0))
    row = pl.BlockSpec((tr, W), lambda i: (i, 0))

    def body(*refs):
        it = iter(refs)
        x_ref, dh_ref = next(it), next(it)
        dres_ref = next(it) if res else None
        g_ref = next(it)
        sc_ref = next(it) if mod else None
        dx_ref, dg_ref = next(it), next(it)
        dsc_ref, dsh_ref = (next(it), next(it)) if mod else (None, None)
        i = pl.program_id(0)

        @pl.when(i == 0)
        def _():
            dg_ref[...] = jnp.zeros_like(dg_ref)
            if mod:
                dsc_ref[...] = jnp.zeros_like(dsc_ref)
                dsh_ref[...] = jnp.zeros_like(dsh_ref)

        xf = x_ref[...].astype(F32)
        r = _rstd(xf)
        xn = xf * r
        dhf = dh_ref[...].astype(F32)
        gv = g_ref[...]
        if mod:
            one_sc = 1.0 + sc_ref[...]
            dsh_ref[...] += jnp.sum(dhf, axis=0, keepdims=True)
            dsc_ref[...] += jnp.sum(dhf * (xn * gv), axis=0, keepdims=True)
            dg_ref[...] += jnp.sum(dhf * xn * one_sc, axis=0, keepdims=True)
            dxn = dhf * (gv * one_sc)
        else:
            dg_ref[...] += jnp.sum(dhf * xn, axis=0, keepdims=True)
            dxn = dhf * gv
        dx = r * (dxn - xn * jnp.mean(dxn * xn, axis=-1, keepdims=True))
        if res:
            dx = dx + dres_ref[...]
        dx_ref[...] = dx.astype(dx_ref.dtype)

    args = [x, dh] + ([dres] if res else []) + [g] + ([sc] if mod else [])
    in_specs = [pl.BlockSpec((tr, W), lambda i: (i, cb)), row] + ([row] if res else []) + [vec] + ([vec] if mod else [])
    n_vec = 3 if mod else 1
    outs = pl.pallas_call(
        body,
        name=name,
        grid=(S // tr,),
        in_specs=in_specs,
        out_specs=[row] + [vec] * n_vec,
        out_shape=[jax.ShapeDtypeStruct((S, W), out_dtype)] + [jax.ShapeDtypeStruct((1, W), F32)] * n_vec,
        compiler_params=_params("arbitrary"),
    )(*args)
    return outs


def _postnorm_res(x, y, gt, g, *, name):
    S, D = x.shape
    tr = _tile(S, 512, 8)
    row = pl.BlockSpec((tr, D), lambda i: (i, 0))
    vec = pl.BlockSpec((1, D), lambda i: (0, 0))

    def body(x_ref, y_ref, gt_ref, g_ref, o_ref):
        yf = y_ref[...]
        o_ref[...] = x_ref[...] + gt_ref[...] * (yf * _rstd(yf) * g_ref[...])

    return pl.pallas_call(
        body,
        name=name,
        grid=(S // tr,),
        in_specs=[row, row, vec, vec],
        out_specs=row,
        out_shape=jax.ShapeDtypeStruct((S, D), F32),
        compiler_params=_params("parallel"),
    )(x, y, gt, g)


def _postnorm_bwd(dx1, y, gt, g, *, name):
    S, D = y.shape
    tr = _tile(S, 256, 16)
    row = pl.BlockSpec((tr, D), lambda i: (i, 0))
    vec = pl.BlockSpec((1, D), lambda i: (0, 0))

    def body(dx_ref, y_ref, gt_ref, g_ref, dy_ref, dgt_ref, dg_ref):
        @pl.when(pl.program_id(0) == 0)
        def _():
            dgt_ref[...] = jnp.zeros_like(dgt_ref)
            dg_ref[...] = jnp.zeros_like(dg_ref)

        yf = y_ref[...]
        r = _rstd(yf)
        yn = yf * r
        d = dx_ref[...]
        gtv, gv = gt_ref[...], g_ref[...]
        dgt_ref[...] += jnp.sum(d * (yn * gv), axis=0, keepdims=True)
        dg_ref[...] += jnp.sum(d * gtv * yn, axis=0, keepdims=True)
        dyn = d * (gtv * gv)
        dy_ref[...] = (r * (dyn - yn * jnp.mean(dyn * yn, axis=-1, keepdims=True))).astype(dy_ref.dtype)

    return pl.pallas_call(
        body,
        name=name,
        grid=(S // tr,),
        in_specs=[row, row, vec, vec],
        out_specs=[row, vec, vec],
        out_shape=[jax.ShapeDtypeStruct((S, D), BF16), jax.ShapeDtypeStruct((1, D), F32),
                   jax.ShapeDtypeStruct((1, D), F32)],
        compiler_params=_params("arbitrary"),
    )(dx1, y, gt, g)


def _final_loss(x1, y, target, gt, g, *, name):
    S, D = y.shape
    tr = _tile(S, 256, 16)
    row = pl.BlockSpec((tr, D), lambda i: (i, 0))
    vec = pl.BlockSpec((1, D), lambda i: (0, 0))
    one = pl.BlockSpec((1, LANES), lambda i: (0, 0))

    def body(x_ref, y_ref, t_ref, gt_ref, g_ref, loss_ref, dout_ref, dy_ref, dgt_ref, dg_ref):
        @pl.when(pl.program_id(0) == 0)
        def _():
            loss_ref[...] = jnp.zeros_like(loss_ref)
            dgt_ref[...] = jnp.zeros_like(dgt_ref)
            dg_ref[...] = jnp.zeros_like(dg_ref)

        yf = y_ref[...]
        r = _rstd(yf)
        yn = yf * r
        gtv, gv = gt_ref[...], g_ref[...]
        out = x_ref[...] + gtv * (yn * gv)
        diff = out - t_ref[...]
        per_tok = jnp.mean(diff * diff, axis=-1, keepdims=True)
        loss_ref[...] += 0.5 * jnp.sum(per_tok, axis=0, keepdims=True)
        d = diff / D
        dout_ref[...] = d
        dgt_ref[...] += jnp.sum(d * (yn * gv), axis=0, keepdims=True)
        dg_ref[...] += jnp.sum(d * gtv * yn, axis=0, keepdims=True)
        dyn = d * (gtv * gv)
        dy_ref[...] = (r * (dyn - yn * jnp.mean(dyn * yn, axis=-1, keepdims=True))).astype(dy_ref.dtype)

    return pl.pallas_call(
        body,
        name=name,
        grid=(S // tr,),
        in_specs=[row, row, row, vec, vec],
        out_specs=[one, row, row, vec, vec],
        out_shape=[jax.ShapeDtypeStruct((1, LANES), F32), jax.ShapeDtypeStruct((S, D), F32),
                   jax.ShapeDtypeStruct((S, D), BF16), jax.ShapeDtypeStruct((1, D), F32),
                   jax.ShapeDtypeStruct((1, D), F32)],
        compiler_params=_params("arbitrary"),
    )(x1, y, target, gt, g)


def _ada_fwd(c_all, w_local, b_cols, *, name):
    B, D = c_all.shape
    N = w_local.shape[1]
    tn = _tile(N, 512)

    def body(c_ref, w_ref, b_ref, ca_ref, mod_ref):
        cv = c_ref[...]
        ca = cv * _sigmoid(cv)
        ca_ref[...] = ca
        mod_ref[...] = jnp.dot(ca, w_ref[...], preferred_element_type=F32, precision=HIGHEST) + b_ref[...]

    return pl.pallas_call(
        body,
        name=name,
        grid=(N // tn,),
        in_specs=[pl.BlockSpec((B, D), lambda j: (0, 0)), pl.BlockSpec((D, tn), lambda j: (0, j)),
                  pl.BlockSpec((1, tn), lambda j: (0, j))],
        out_specs=[pl.BlockSpec((B, D), lambda j: (0, 0)), pl.BlockSpec((B, tn), lambda j: (0, j))],
        out_shape=[jax.ShapeDtypeStruct((B, D), F32), jax.ShapeDtypeStruct((B, N), F32)],
        compiler_params=_params("arbitrary"),
    )(c_all, w_local, b_cols)


def _rope_tables(S, width, lane_off):
    pos = jnp.arange(S, dtype=F32)
    inv = ROPE_THETA ** (-jnp.arange(0, MLA_ROPE, 2, dtype=F32) / MLA_ROPE)
    ang = pos[:, None] * inv[None, :]
    ang = jnp.concatenate([ang, ang], axis=-1)
    cos, sin = jnp.cos(ang), jnp.sin(ang)
    first = (jnp.arange(MLA_ROPE) < ROPE_HALF)[None, :]
    sa = jnp.where(first, -sin, 0.0)
    sb = jnp.where(first, 0.0, sin)

    def place(t, fill):
        return jnp.pad(t, ((0, 0), (lane_off, width - lane_off - MLA_ROPE)), constant_values=fill)

    return place(cos, 1.0), place(sa, 0.0), place(sb, 0.0)


def _rope_apply(x, cos, sa, sb, width, transpose):
    if transpose:
        return x * cos + pltpu.roll(x * sa, ROPE_HALF, 1) + pltpu.roll(x * sb, width - ROPE_HALF, 1)
    return x * cos + pltpu.roll(x, width - ROPE_HALF, 1) * sa + pltpu.roll(x, ROPE_HALF, 1) * sb


def _rope(x, tables, *, heads, width, transpose, name, off=0, scale=1.0):
    S = x.shape[0]
    cb = _col_view(width, off)
    tr = _tile(S, 512, 16)
    tab = pl.BlockSpec((tr, width), lambda i, h: (i, 0))

    def body(x_ref, c_ref, sa_ref, sb_ref, o_ref):
        y = _rope_apply(x_ref[...].astype(F32), c_ref[...], sa_ref[...], sb_ref[...], width, transpose)
        o_ref[...] = (y if scale == 1.0 else y * scale).astype(o_ref.dtype)

    return pl.pallas_call(
        body,
        name=name,
        grid=(S // tr, heads),
        in_specs=[pl.BlockSpec((tr, width), lambda i, h: (i, cb + h)), tab, tab, tab],
        out_specs=pl.BlockSpec((tr, width), lambda i, h: (i, h)),
        out_shape=jax.ShapeDtypeStruct((S, heads * width), BF16),
        compiler_params=_params("parallel", "parallel"),
    )(x, *tables)


def _shared_rope_grad(parts, tables, *, name):
    P, S, _ = parts.shape
    tr = _tile(S, 512, 16)
    tab = pl.BlockSpec((tr, LANES), lambda i: (i, 0))

    def body(p_ref, c_ref, sa_ref, sb_ref, o_ref):
        acc = p_ref[0]
        for k in range(1, P):
            acc = acc + p_ref[k]
        o_ref[...] = _rope_apply(acc, c_ref[...], sa_ref[...], sb_ref[...], LANES, True).astype(o_ref.dtype)

    return pl.pallas_call(
        body,
        name=name,
        grid=(S // tr,),
        in_specs=[pl.BlockSpec((P, tr, LANES), lambda i: (0, i, 0)), tab, tab, tab],
        out_specs=tab,
        out_shape=jax.ShapeDtypeStruct((S, LANES), BF16),
        compiler_params=_params("parallel"),
    )(parts, *tables)


MLA_SCALE = MLA_QK ** -0.5
LOG2E = math.log2(math.e)
LN2 = math.log(2.0)
MLA_Q_PRESCALE = MLA_SCALE * LOG2E


def _lane_tile(v, n):
    return v if n == LANES else jnp.tile(v, (1, n // LANES))


def _causal_mask(s):
    rows = lax.broadcasted_iota(jnp.int32, s.shape, 0)
    cols = lax.broadcasted_iota(jnp.int32, s.shape, 1)
    return jnp.where(cols <= rows, s, NEG)


def _tri_blocks(nb, q_major):
    if q_major:
        pairs = [(q, k) for q in range(nb) for k in range(q + 1)]
    else:
        pairs = [(q, k) for k in range(nb) for q in range(k, nb)]
    return (jnp.asarray(np.array([p[0] for p in pairs], np.int32)),
            jnp.asarray(np.array([p[1] for p in pairs], np.int32)))


HEAD_PAIR = 2


def _flash_fwd(q_raw, KV, krr, tables, *, heads, name, comm=()):
    S = q_raw.shape[0]
    t = _tile(S, 512)
    nb = S // t
    qt, kt = _tri_blocks(nb, True)
    qw, vw = HEAD_PAIR * MLA_QK_PAD, HEAD_PAIR * MLA_V

    def body(qt_ref, kt_ref, q_ref, kn0_ref, kn1_ref, kr_ref, v0_ref, v1_ref, c_ref, sa_ref, sb_ref,
             o_ref, lse_ref, qr_ref, m_scr, l_scr, acc_scr):
        step_id = pl.program_id(1)
        qi, ki = qt_ref[step_id], kt_ref[step_id]

        @pl.when(ki == 0)
        def _():
            m_scr[...] = jnp.full_like(m_scr, NEG)
            l_scr[...] = jnp.zeros_like(l_scr)
            acc_scr[...] = jnp.zeros_like(acc_scr)
            for h in range(HEAD_PAIR):
                base = h * MLA_QK_PAD
                nope = q_ref[:, base:base + MLA_NOPE].astype(F32) * MLA_Q_PRESCALE
                rot = _rope_apply(q_ref[:, base + MLA_NOPE:base + MLA_QK_PAD].astype(F32), c_ref[...], sa_ref[...],
                                  sb_ref[...], LANES, False) * MLA_Q_PRESCALE
                qr_ref[:, base:base + MLA_NOPE] = nope.astype(qr_ref.dtype)
                qr_ref[:, base + MLA_NOPE:base + MLA_QK_PAD] = rot.astype(qr_ref.dtype)

        def step(diagonal):
            for h, (kn_ref, v_ref) in enumerate(((kn0_ref, v0_ref), (kn1_ref, v1_ref))):
                cols = slice(h * MLA_QK_PAD, (h + 1) * MLA_QK_PAD)
                k = jnp.concatenate([kn_ref[...], kr_ref[...]], axis=1)
                s = lax.dot_general(qr_ref[:, cols], k, NT, preferred_element_type=F32)
                if diagonal:
                    s = _causal_mask(s)
                m_prev = m_scr[h]
                m_new = jnp.maximum(m_prev, jnp.max(s, axis=1, keepdims=True))
                alpha = jnp.exp2(m_prev - m_new)
                p = jnp.exp2(s - _lane_tile(m_new, t))
                l_new = alpha * l_scr[h] + jnp.sum(p, axis=1, keepdims=True)
                acc = alpha * acc_scr[h] + jnp.dot(p.astype(BF16), v_ref[...], preferred_element_type=F32)
                if diagonal:
                    o_ref[:, h * MLA_V:(h + 1) * MLA_V] = (acc / l_new).astype(o_ref.dtype)
                    lse_ref[h] = m_new + jnp.log(l_new) * LOG2E
                else:
                    l_scr[h], acc_scr[h], m_scr[h] = l_new, acc, m_new

        pl.when(ki < qi)(lambda: step(False))
        pl.when(ki == qi)(lambda: step(True))

    def kvspec(h, half):
        return pl.BlockSpec((t, LANES), lambda hp, s, qt, kt: (kt[s], 2 * (HEAD_PAIR * hp + h) + half))

    qtab = pl.BlockSpec((t, LANES), lambda hp, s, qt, kt: (qt[s], 0))
    qrow = lambda hp, s, qt, kt: (qt[s], hp)
    return _call(
        body,
        name=name,
        grid=(heads // HEAD_PAIR, int(qt.shape[0])),
        in_specs=[pl.BlockSpec((t, qw), qrow), kvspec(0, 0), kvspec(1, 0),
                  pl.BlockSpec((t, LANES), lambda hp, s, qt, kt: (kt[s], 0)), kvspec(0, 1), kvspec(1, 1),
                  qtab, qtab, qtab],
        out_specs=[pl.BlockSpec((t, vw), qrow),
                   pl.BlockSpec((HEAD_PAIR, t, LANES), lambda hp, s, qt, kt: (hp, qt[s], 0)),
                   pl.BlockSpec((t, qw), qrow)],
        out_shape=[jax.ShapeDtypeStruct((S, heads * MLA_V), BF16),
                   jax.ShapeDtypeStruct((heads, S, LANES), F32),
                   jax.ShapeDtypeStruct((S, heads * MLA_QK_PAD), BF16)],
        scratch_shapes=[pltpu.VMEM((HEAD_PAIR, t, LANES), F32), pltpu.VMEM((HEAD_PAIR, t, LANES), F32),
                        pltpu.VMEM((HEAD_PAIR, t, MLA_V), F32)],
        sem=("parallel", "arbitrary"),
        args=(q_raw, KV, KV, krr, KV, KV, *tables),
        comm=comm,
        prefetch=(qt, kt),
    )


def _flash_bwd(Q, KV, krr, dO, O, lse, tables, *, heads, name, comm=()):
    S = Q.shape[0]
    t = _tile(S, 512)
    nb = S // t
    qt, kt = _tri_blocks(nb, False)
    n_steps = int(qt.shape[0])
    qw, vw = HEAD_PAIR * MLA_QK_PAD, HEAD_PAIR * MLA_V

    def body(qt_ref, kt_ref, q_ref, kn0_ref, kn1_ref, kr_ref, v0_ref, v1_ref, do_ref, o_ref, lse_ref,
             c_ref, sa_ref, sb_ref, dq_ref, dkv_ref, dkr_ref, dq_scr, dk_scr, dv_scr, delta_scr):
        step_id = pl.program_id(1)
        qi, ki = qt_ref[step_id], kt_ref[step_id]

        @pl.when(ki == 0)
        def _():
            for h in range(HEAD_PAIR):
                vc = slice(h * MLA_V, (h + 1) * MLA_V)
                d = jnp.sum(do_ref[:, vc].astype(F32) * o_ref[:, vc].astype(F32), axis=1, keepdims=True)
                delta_scr[h, qi] = jnp.broadcast_to(d, (t, LANES))

        def step(diagonal):
            for h, (kn_ref, v_ref) in enumerate(((kn0_ref, v0_ref), (kn1_ref, v1_ref))):
                base = h * MLA_QK_PAD
                cols = slice(base, base + MLA_QK_PAD)
                vc = slice(h * MLA_V, (h + 1) * MLA_V)
                q, do = q_ref[:, cols], do_ref[:, vc]
                k = jnp.concatenate([kn_ref[...], kr_ref[...]], axis=1)
                s = lax.dot_general(q, k, NT, preferred_element_type=F32)
                if diagonal:
                    s = _causal_mask(s)
                p = jnp.exp2(s - _lane_tile(lse_ref[h], t))
                dv = lax.dot_general(p.astype(BF16), do, TN, preferred_element_type=F32)
                dp = lax.dot_general(do, v_ref[...], NT, preferred_element_type=F32)
                ds = (p * (dp - _lane_tile(delta_scr[h, qi], t))).astype(BF16)
                dk = lax.dot_general(ds, q, TN, preferred_element_type=F32)
                dq = jnp.dot(ds, k, preferred_element_type=F32)
                if diagonal:
                    dk_scr[h], dv_scr[h] = dk, dv
                    dq = (dq_scr[qi, :, cols] + dq) * (LN2 * MLA_Q_PRESCALE)
                    rot = _rope_apply(dq[:, MLA_NOPE:], c_ref[...], sa_ref[...], sb_ref[...], LANES, True)
                    dq_ref[:, base:base + MLA_NOPE] = dq[:, :MLA_NOPE].astype(dq_ref.dtype)
                    dq_ref[:, base + MLA_NOPE:base + MLA_QK_PAD] = rot.astype(dq_ref.dtype)
                else:
                    dk_scr[h] += dk
                    dv_scr[h] += dv
                    dq_scr[qi, :, cols] += dq

        @pl.when(ki == 0)
        def _():
            dq_scr[qi] = jnp.zeros((t, qw), F32)

        pl.when(qi > ki)(lambda: step(False))
        pl.when(qi == ki)(lambda: step(True))

        @pl.when(qi == nb - 1)
        def _():
            shared = jnp.zeros((t, LANES), F32)
            for h in range(HEAD_PAIR):
                base = h * MLA_QK_PAD
                dk = dk_scr[h] * LN2
                dkv_ref[:, base:base + MLA_NOPE] = dk[:, :MLA_NOPE].astype(dkv_ref.dtype)
                dkv_ref[:, base + MLA_NOPE:base + MLA_QK_PAD] = dv_scr[h].astype(dkv_ref.dtype)
                shared = shared + dk[:, MLA_NOPE:]
            dkr_ref[0] = shared

    def kvspec(h, half):
        return pl.BlockSpec((t, LANES), lambda hp, s, qt, kt: (kt[s], 2 * (HEAD_PAIR * hp + h) + half))

    qrow = lambda hp, s, qt, kt: (qt[s], hp)
    krow = lambda hp, s, qt, kt: (kt[s], hp)
    ktab = pl.BlockSpec((t, LANES), lambda hp, s, qt, kt: (kt[s], 0))
    return _call(
        body,
        name=name,
        grid=(heads // HEAD_PAIR, n_steps),
        in_specs=[pl.BlockSpec((t, qw), qrow), kvspec(0, 0), kvspec(1, 0), ktab, kvspec(0, 1), kvspec(1, 1),
                  pl.BlockSpec((t, vw), qrow), pl.BlockSpec((t, vw), qrow),
                  pl.BlockSpec((HEAD_PAIR, t, LANES), lambda hp, s, qt, kt: (hp, qt[s], 0)),
                  ktab, ktab, ktab],
        out_specs=[pl.BlockSpec((t, qw), krow), pl.BlockSpec((t, qw), krow),
                   pl.BlockSpec((1, t, LANES), lambda hp, s, qt, kt: (hp, kt[s], 0))],
        out_shape=[jax.ShapeDtypeStruct((S, heads * MLA_QK_PAD), BF16),
                   jax.ShapeDtypeStruct((S, heads * MLA_QK_PAD), BF16),
                   jax.ShapeDtypeStruct((heads // HEAD_PAIR, S, LANES), F32)],
        scratch_shapes=[pltpu.VMEM((nb, t, qw), F32), pltpu.VMEM((HEAD_PAIR, t, MLA_QK_PAD), F32),
                        pltpu.VMEM((HEAD_PAIR, t, MLA_V), F32), pltpu.VMEM((HEAD_PAIR, nb, t, LANES), F32)],
        sem=("parallel", "arbitrary"),
        args=(Q, KV, KV, krr, KV, KV, dO, O, lse, *tables),
        comm=comm,
        prefetch=(qt, kt),
    )


SWA_SCALE = SWA_HD ** -0.5


def _t5_bucket_table():
    a = np.arange(BLOCK)[:, None]
    j = np.arange(2 * BLOCK)[None, :]
    dist = BLOCK + a - j
    max_exact = REL_BUCKETS // 2
    n = np.maximum(dist, 0)
    large = max_exact + (np.log(np.maximum(n, 1).astype(np.float32) / np.float32(max_exact))
                         / np.float32(math.log(REL_MAX_DIST / max_exact))
                         * np.float32(REL_BUCKETS - max_exact)).astype(np.int32)
    large = np.minimum(large, REL_BUCKETS - 1)
    bucket = np.where(n < max_exact, n, large)
    valid = (dist >= 0) & (dist < WINDOW)
    return bucket.astype(np.int32), valid


def _swa_probs(q_ref, kp_ref, kc_ref, bias_ref, sink_ref, qb, G):
    q2 = q_ref[...].reshape(G * BLOCK, SWA_HD)
    kb = jnp.concatenate([kp_ref[0], kc_ref[0]], axis=0)
    s = lax.dot_general(kb, q2, NT, preferred_element_type=F32) * SWA_SCALE + bias_ref[0]
    keys = lax.broadcasted_iota(jnp.int32, s.shape, 0)
    s = jnp.where((keys >= BLOCK) | (qb > 0), s, NEG)
    sink = sink_ref[0]
    m = jnp.maximum(jnp.max(s, axis=0, keepdims=True), sink)
    e = jnp.exp(s - m)
    es = jnp.exp(sink - m)
    inv = 1.0 / (jnp.sum(e, axis=0, keepdims=True) + es)
    return q2, kb, e * inv, es * inv


def _swa_fwd(q, k, v, bias_t, sink, *, name, comm=()):
    H, S, _ = q.shape
    G = H // SWA_KVH
    nb = S // BLOCK
    cur = lambda kh, qb: (kh, qb, 0)
    prev = lambda kh, qb: (kh, jnp.maximum(qb - 1, 0), 0)
    kvspec = lambda im: pl.BlockSpec((1, BLOCK, SWA_HD), im)

    def body(q_ref, kc_ref, kp_ref, vc_ref, vp_ref, bias_ref, sink_ref, o_ref):
        qb = pl.program_id(1)
        _, _, pt, _ = _swa_probs(q_ref, kp_ref, kc_ref, bias_ref, sink_ref, qb, G)
        vb = jnp.concatenate([vp_ref[0], vc_ref[0]], axis=0)
        o_ref[0, 0] = lax.dot_general(vb, pt.astype(BF16), TN, preferred_element_type=F32).astype(o_ref.dtype)

    outs, moved = _call(
        body,
        name=name,
        grid=(SWA_KVH, nb),
        in_specs=[pl.BlockSpec((G, BLOCK, SWA_HD), cur), kvspec(cur), kvspec(prev), kvspec(cur), kvspec(prev),
                  pl.BlockSpec((1, 2 * BLOCK, G * BLOCK), lambda kh, qb: (kh, 0, 0)),
                  pl.BlockSpec((1, 1, G * BLOCK), lambda kh, qb: (kh, 0, 0))],
        out_specs=[pl.BlockSpec((1, 1, SWA_HD, G * BLOCK), lambda kh, qb: (kh, qb, 0, 0))],
        out_shape=[jax.ShapeDtypeStruct((SWA_KVH, nb, SWA_HD, G * BLOCK), BF16)],
        sem=("parallel", "parallel"),
        args=(q, k, k, v, v, bias_t, sink),
        comm=comm,
    )
    return outs[0], moved


def _swa_bwd(q, k, v, bias_t, sink, do, *, name, comm=()):
    H, S, _ = q.shape
    G = H // SWA_KVH
    nb = S // BLOCK
    cur = lambda kh, qb: (kh, jnp.minimum(qb, nb - 1), 0)
    prev = lambda kh, qb: (kh, jnp.maximum(jnp.minimum(qb, nb - 1) - 1, 0), 0)
    lag = lambda kh, qb: (kh, jnp.maximum(qb - 1, 0), 0)
    kvspec = lambda im: pl.BlockSpec((1, BLOCK, SWA_HD), im)

    def body(q_ref, kc_ref, kp_ref, vc_ref, vp_ref, bias_ref, sink_ref, do_ref,
             dq_ref, dk_ref, dv_ref, dbias_ref, dsink_ref, ck_scr, cv_scr):
        qb = pl.program_id(1)

        @pl.when(qb == 0)
        def _():
            dbias_ref[...] = jnp.zeros_like(dbias_ref)
            dsink_ref[...] = jnp.zeros_like(dsink_ref)
            ck_scr[...] = jnp.zeros_like(ck_scr)
            cv_scr[...] = jnp.zeros_like(cv_scr)

        @pl.when(qb < nb)
        def _():
            q2, kb, pt, ps = _swa_probs(q_ref, kp_ref, kc_ref, bias_ref, sink_ref, qb, G)
            vb = jnp.concatenate([vp_ref[0], vc_ref[0]], axis=0)
            do2 = do_ref[...].reshape(G * BLOCK, SWA_HD)
            dpt = lax.dot_general(vb, do2, NT, preferred_element_type=F32)
            delta = jnp.sum(dpt * pt, axis=0, keepdims=True)
            dst = pt * (dpt - delta)
            dbias_ref[0] += dst
            dsink_ref[0] += -ps * delta
            dsb = (dst * SWA_SCALE).astype(BF16)
            dq_ref[0, 0] = lax.dot_general(kb, dsb, TN, preferred_element_type=F32).astype(dq_ref.dtype)
            dkb = jnp.dot(dsb, q2, preferred_element_type=F32)
            dvb = jnp.dot(pt.astype(BF16), do2, preferred_element_type=F32)
            dk_ref[0] = (ck_scr[...] + dkb[:BLOCK]).astype(dk_ref.dtype)
            dv_ref[0] = (cv_scr[...] + dvb[:BLOCK]).astype(dv_ref.dtype)
            ck_scr[...] = dkb[BLOCK:]
            cv_scr[...] = dvb[BLOCK:]

        @pl.when(qb == nb)
        def _():
            dk_ref[0] = ck_scr[...].astype(dk_ref.dtype)
            dv_ref[0] = cv_scr[...].astype(dv_ref.dtype)

    tspec = pl.BlockSpec((1, 1, SWA_HD, G * BLOCK), lambda kh, qb: (kh, jnp.minimum(qb, nb - 1), 0, 0))
    return _call(
        body,
        name=name,
        grid=(SWA_KVH, nb + 1),
        in_specs=[pl.BlockSpec((G, BLOCK, SWA_HD), cur), kvspec(cur), kvspec(prev), kvspec(cur), kvspec(prev),
                  pl.BlockSpec((1, 2 * BLOCK, G * BLOCK), lambda kh, qb: (kh, 0, 0)),
                  pl.BlockSpec((1, 1, G * BLOCK), lambda kh, qb: (kh, 0, 0)),
                  pl.BlockSpec((G, BLOCK, SWA_HD), cur)],
        out_specs=[tspec, kvspec(lag), kvspec(lag),
                   pl.BlockSpec((1, 2 * BLOCK, G * BLOCK), lambda kh, qb: (kh, 0, 0)),
                   pl.BlockSpec((1, 1, G * BLOCK), lambda kh, qb: (kh, 0, 0))],
        out_shape=[jax.ShapeDtypeStruct((SWA_KVH, nb, SWA_HD, G * BLOCK), BF16),
                   jax.ShapeDtypeStruct((SWA_KVH, S, SWA_HD), BF16),
                   jax.ShapeDtypeStruct((SWA_KVH, S, SWA_HD), BF16),
                   jax.ShapeDtypeStruct((SWA_KVH, 2 * BLOCK, G * BLOCK), F32),
                   jax.ShapeDtypeStruct((SWA_KVH, 1, G * BLOCK), F32)],
        scratch_shapes=[pltpu.VMEM((BLOCK, SWA_HD), F32), pltpu.VMEM((BLOCK, SWA_HD), F32)],
        sem=("parallel", "arbitrary"),
        args=(q, k, k, v, v, bias_t, sink, do),
        comm=comm,
    )


def _gate_mix(z, o_a, o_b, *, D, off_a, off_b, name):
    S = z.shape[0]
    tr = _tile(S, 256, 16)
    row = pl.BlockSpec((tr, D), lambda i: (i, 0))
    ca, cb = _col_view(D, off_a), _col_view(D, off_b)

    def body(ga_ref, gb_ref, oa_ref, ob_ref, m_ref):
        m = (_sigmoid(ga_ref[...].astype(F32)) * oa_ref[...].astype(F32)
             + _sigmoid(gb_ref[...].astype(F32)) * ob_ref[...].astype(F32))
        m_ref[...] = m.astype(m_ref.dtype)

    return pl.pallas_call(
        body,
        name=name,
        grid=(S // tr,),
        in_specs=[pl.BlockSpec((tr, D), lambda i: (i, ca)), pl.BlockSpec((tr, D), lambda i: (i, cb)), row, row],
        out_specs=row,
        out_shape=jax.ShapeDtypeStruct((S, D), BF16),
        compiler_params=_params("parallel"),
    )(z, z, o_a, o_b)


def _gate_mix_bwd(dm, z, o_a, o_b, *, D, off_a, off_b, name):
    S = z.shape[0]
    tr = _tile(S, 256, 16)
    row = pl.BlockSpec((tr, D), lambda i: (i, 0))
    ca, cb = _col_view(D, off_a), _col_view(D, off_b)

    def body(dm_ref, ga_ref, gb_ref, oa_ref, ob_ref, dga_ref, dgb_ref, doa_ref, dob_ref):
        d = dm_ref[...].astype(F32)
        for g_ref, o_ref, dg_ref, do_ref in ((ga_ref, oa_ref, dga_ref, doa_ref), (gb_ref, ob_ref, dgb_ref, dob_ref)):
            sg = _sigmoid(g_ref[...].astype(F32))
            dg_ref[...] = (d * o_ref[...].astype(F32) * (sg * (1.0 - sg))).astype(dg_ref.dtype)
            do_ref[...] = (d * sg).astype(do_ref.dtype)

    return pl.pallas_call(
        body,
        name=name,
        grid=(S // tr,),
        in_specs=[row, pl.BlockSpec((tr, D), lambda i: (i, ca)), pl.BlockSpec((tr, D), lambda i: (i, cb)), row, row],
        out_specs=[row] * 4,
        out_shape=[jax.ShapeDtypeStruct((S, D), BF16)] * 4,
        compiler_params=_params("parallel"),
    )(dm, z, z, o_a, o_b)


CONV_ROWS = 256
CONV_COLS = 1408
SUBLANES = 8


def _shift_matrices(tr):
    r = np.arange(tr)[:, None]
    c = np.arange(tr)[None, :]
    back = [jnp.asarray(r == c + d, dtype=BF16) for d in (1, 2)]
    ahead = [jnp.asarray(r + d == c, dtype=BF16) for d in (1, 2)]
    return back, ahead


def _rows_before(x, halo_ref, first, b1_ref, b2_ref):
    s1 = jnp.dot(b1_ref[...], x, preferred_element_type=F32)
    s2 = jnp.dot(b2_ref[...], x, preferred_element_type=F32)
    h8 = jnp.where(first, 0.0, halo_ref[...].astype(F32)[HALO - SUBLANES:])
    rows = lax.broadcasted_iota(jnp.int32, h8.shape, 0)
    fix1 = jnp.where(rows < 1, pltpu.roll(h8, 1, 0), 0.0)
    fix2 = jnp.where(rows < 2, pltpu.roll(h8, 2, 0), 0.0)
    s1 = jnp.concatenate([s1[:SUBLANES] + fix1, s1[SUBLANES:]], axis=0)
    s2 = jnp.concatenate([s2[:SUBLANES] + fix2, s2[SUBLANES:]], axis=0)
    return s1, s2


def _conv_taps(x, s1, s2, cw_ref, cb_ref):
    return cb_ref[...] + cw_ref[0:1, :] * s2 + cw_ref[1:2, :] * s1 + cw_ref[2:3, :] * x


def _conv_gate(up, cw, cb, *, name):
    S, F2 = up.shape
    F = F2 // 2
    tr = _tile(S, CONV_ROWS, HALO)
    tc = _tile(F, CONV_COLS)
    nc = F // tc
    hb = tr // HALO
    back, _ = _shift_matrices(tr)
    mat = pl.BlockSpec((tr, tr), lambda i, j: (0, 0))

    def halo_map(shift):
        return lambda i, j: (jnp.maximum(i * hb - 1, 0), j + shift)

    def body(x1_ref, h1_ref, x2_ref, h2_ref, cw1_ref, cw2_ref, cb1_ref, cb2_ref, b1_ref, b2_ref, a_ref):
        first = pl.program_id(0) == 0
        us = []
        for x_ref, h_ref, cw_ref, cb_ref in ((x1_ref, h1_ref, cw1_ref, cb1_ref), (x2_ref, h2_ref, cw2_ref, cb2_ref)):
            x = x_ref[...]
            s1, s2 = _rows_before(x, h_ref, first, b1_ref, b2_ref)
            us.append(_conv_taps(x.astype(F32), s1, s2, cw_ref, cb_ref))
        u1, u2 = us
        a_ref[...] = (u1 * _sigmoid(u1) * u2).astype(a_ref.dtype)

    return pl.pallas_call(
        body,
        name=name,
        grid=(S // tr, nc),
        in_specs=[pl.BlockSpec((tr, tc), lambda i, j: (i, j)), pl.BlockSpec((HALO, tc), halo_map(0)),
                  pl.BlockSpec((tr, tc), lambda i, j: (i, j + nc)), pl.BlockSpec((HALO, tc), halo_map(nc)),
                  pl.BlockSpec((CONV_WIDTH, tc), lambda i, j: (0, j)),
                  pl.BlockSpec((CONV_WIDTH, tc), lambda i, j: (0, j + nc)),
                  pl.BlockSpec((1, tc), lambda i, j: (0, j)), pl.BlockSpec((1, tc), lambda i, j: (0, j + nc)),
                  mat, mat],
        out_specs=pl.BlockSpec((tr, tc), lambda i, j: (i, j)),
        out_shape=jax.ShapeDtypeStruct((S, F), BF16),
        compiler_params=_params("parallel", "parallel"),
    )(up, up, up, up, cw, cw, cb, cb, *back)


def _conv_gate_bwd(up, da, cw, cb, *, name, comm=()):
    S, F2 = up.shape
    F = F2 // 2
    tr = _tile(S, CONV_ROWS, HALO)
    tc = _tile(F, CONV_COLS)
    nc = F // tc
    hb = tr // HALO
    ni = S // tr
    back, ahead = _shift_matrices(tr)
    mat = pl.BlockSpec((tr, tr), lambda j, r: (0, 0))

    def cur(shift):
        return lambda j, r: (ni - 1 - r, j % nc + shift)

    def before(shift):
        return lambda j, r: (jnp.maximum((ni - 1 - r) * hb - 1, 0), j % nc + shift)

    def vec(rows, shift):
        return pl.BlockSpec((rows, tc), lambda j, r: (0, j % nc + shift))

    def body(x1_ref, h1_ref, x2_ref, h2_ref, da_ref, cw1_ref, cw2_ref, cb1_ref, cb2_ref, cwo_ref,
             b1_ref, b2_ref, a1_ref, a2_ref, dup_ref, dcw_ref, dcb_ref, next_du):
        j, r = pl.program_id(0), pl.program_id(1)
        first = r == ni - 1

        @pl.when(r == 0)
        def _():
            dcw_ref[...] = jnp.zeros_like(dcw_ref)
            dcb_ref[...] = jnp.zeros_like(dcb_ref)
            next_du[...] = jnp.zeros_like(next_du)

        x1 = x1_ref[...]
        x1f = x1.astype(F32)
        s11, s12 = _rows_before(x1, h1_ref, first, b1_ref, b2_ref)
        u1 = _conv_taps(x1f, s11, s12, cw1_ref, cb1_ref)
        sg = _sigmoid(u1)
        daf = da_ref[...].astype(F32)

        def finish(du, own, own1, own2):
            du_b = du.astype(BF16)
            n1 = jnp.dot(a1_ref[...], du_b, preferred_element_type=F32)
            n2 = jnp.dot(a2_ref[...], du_b, preferred_element_type=F32)
            c8 = next_du[...]
            rows = lax.broadcasted_iota(jnp.int32, c8.shape, 0)
            fix1 = jnp.where(rows >= SUBLANES - 1, pltpu.roll(c8, SUBLANES - 1, 0), 0.0)
            fix2 = jnp.where(rows >= SUBLANES - 2, pltpu.roll(c8, SUBLANES - 2, 0), 0.0)
            n1 = jnp.concatenate([n1[:tr - SUBLANES], n1[tr - SUBLANES:] + fix1], axis=0)
            n2 = jnp.concatenate([n2[:tr - SUBLANES], n2[tr - SUBLANES:] + fix2], axis=0)
            dup = cwo_ref[2:3, :] * du + cwo_ref[1:2, :] * n1 + cwo_ref[0:1, :] * n2
            dup_ref[...] = dup.astype(dup_ref.dtype)
            dcb_ref[...] += jnp.sum(du, axis=0, keepdims=True)
            for tap, shifted in enumerate((own2, own1, own)):
                dcw_ref[tap:tap + 1, :] += jnp.sum(du * shifted, axis=0, keepdims=True)
            next_du[...] = du[:SUBLANES].astype(BF16).astype(F32)

        @pl.when(j < nc)
        def _():
            x2 = x2_ref[...]
            s21, s22 = _rows_before(x2, h2_ref, first, b1_ref, b2_ref)
            u2 = _conv_taps(x2.astype(F32), s21, s22, cw2_ref, cb2_ref)
            finish(daf * u2 * (sg * (1.0 + u1 * (1.0 - sg))), x1f, s11, s12)

        @pl.when(j >= nc)
        def _():
            x2 = x2_ref[...]
            s21, s22 = _rows_before(x2, h2_ref, first, b1_ref, b2_ref)
            finish(daf * (u1 * sg), x2.astype(F32), s21, s22)

    return _call(
        body,
        name=name,
        grid=(2 * nc, ni),
        in_specs=[pl.BlockSpec((tr, tc), cur(0)), pl.BlockSpec((HALO, tc), before(0)),
                  pl.BlockSpec((tr, tc), cur(nc)), pl.BlockSpec((HALO, tc), before(nc)),
                  pl.BlockSpec((tr, tc), cur(0)),
                  vec(CONV_WIDTH, 0), vec(CONV_WIDTH, nc), vec(1, 0), vec(1, nc),
                  pl.BlockSpec((CONV_WIDTH, tc), lambda j, r: (0, j)), mat, mat, mat, mat],
        out_specs=[pl.BlockSpec((tr, tc), lambda j, r: (ni - 1 - r, j)),
                   pl.BlockSpec((CONV_WIDTH, tc), lambda j, r: (0, j)),
                   pl.BlockSpec((1, tc), lambda j, r: (0, j))],
        out_shape=[jax.ShapeDtypeStruct((S, F2), BF16), jax.ShapeDtypeStruct((CONV_WIDTH, F2), F32),
                   jax.ShapeDtypeStruct((1, F2), F32)],
        scratch_shapes=[pltpu.VMEM((SUBLANES, tc), F32)],
        sem=("parallel", "arbitrary"),
        args=(up, up, up, up, da, cw, cw, cb, cb, cw, *back, *ahead),
        comm=comm,
    )


def _adam_math(w, g, m, v):
    m = ADAM_B1 * m + (1.0 - ADAM_B1) * g
    v = ADAM_B2 * v + (1.0 - ADAM_B2) * (g * g)
    m_hat = m / (1.0 - ADAM_B1 ** ADAM_STEP)
    v_hat = v / (1.0 - ADAM_B2 ** ADAM_STEP)
    delta = -ADAM_LR * (m_hat / (jnp.sqrt(v_hat) + ADAM_EPS) + ADAM_WD * w)
    return delta, m, v


def _adamw(w, m, v, parts, *, name):
    R, C = w.shape
    plist = list(parts) if isinstance(parts, (list, tuple)) else [parts]
    tr = _tile(min(p.shape[1] for p in plist), 256, 16)
    assert sum(p.shape[1] for p in plist) == R and all(p.shape[1] % tr == 0 for p in plist)
    row = pl.BlockSpec((tr, C), lambda i: (i, 0))
    first, spans = 0, []
    for p in plist:
        spans.append((first, first + p.shape[1] // tr))
        first = spans[-1][1]

    def body(w_ref, m_ref, v_ref, *rest):
        p_refs, (g_ref, d_ref, m2_ref, v2_ref) = rest[:len(plist)], rest[len(plist):]
        i = pl.program_id(0)

        def update(p_ref):
            g = p_ref[0].astype(F32)
            for k in range(1, N_DEV):
                g = g + p_ref[k].astype(F32)
            g_ref[...] = g
            d_ref[...], m2_ref[...], v2_ref[...] = _adam_math(w_ref[...], g, m_ref[...], v_ref[...])

        if len(plist) == 1:
            update(p_refs[0])
        else:
            for p_ref, (lo, hi) in zip(p_refs, spans):
                pl.when((i >= lo) & (i < hi))(functools.partial(update, p_ref))

    def part_spec(lo, hi):
        return pl.BlockSpec((N_DEV, tr, C), lambda i: (0, jnp.clip(i - lo, 0, hi - lo - 1), 0))

    return pl.pallas_call(
        body,
        name=name,
        grid=(R // tr,),
        in_specs=[row, row, row] + [part_spec(lo, hi) for lo, hi in spans],
        out_specs=[row] * 4,
        out_shape=[jax.ShapeDtypeStruct((R, C), F32)] * 4,
        compiler_params=_params("parallel"),
    )(w, m, v, *plist)


def _adamw_ada(w, m, v, cact_t, dmod_cols, *, name):
    R, C = w.shape
    B = cact_t.shape[1]
    tr = _tile(R, 256, 8)
    row = pl.BlockSpec((tr, C), lambda i: (i, 0))

    def body(w_ref, m_ref, v_ref, c_ref, d_ref, g_ref, dl_ref, m2_ref, v2_ref):
        g = c_ref[:, 0:1] * d_ref[0:1, :]
        for b in range(1, B):
            g = g + c_ref[:, b:b + 1] * d_ref[b:b + 1, :]
        g_ref[...] = g
        dl_ref[...], m2_ref[...], v2_ref[...] = _adam_math(w_ref[...], g, m_ref[...], v_ref[...])

    return pl.pallas_call(
        body,
        name=name,
        grid=(R // tr,),
        in_specs=[row, row, row, pl.BlockSpec((tr, B), lambda i: (i, 0)), pl.BlockSpec((B, C), lambda i: (0, 0))],
        out_specs=[row] * 4,
        out_shape=[jax.ShapeDtypeStruct((R, C), F32)] * 4,
        compiler_params=_params("parallel"),
    )(w, m, v, cact_t, dmod_cols)


def _z_layout(D, q_rank, kv_rank):
    kv = SWA_KVH * SWA_HD
    orig = {}
    o = 0
    for nm, w in (("cq", q_rank), ("ckv", kv_rank), ("kr", MLA_ROPE), ("qs", D), ("ks", kv), ("vs", kv),
                  ("ga", D), ("gb", D)):
        orig[nm] = (o, w)
        o += w
    blockw = {"cq": q_rank, "ckv": kv_rank, "kr": LANES, "qs": D, "ks": kv, "vs": kv, "ga": D, "gb": D}
    best = None
    for perm in itertools.permutations(("cq", "ckv", "ks", "vs", "kr")):
        off, new = 0, {}
        for nm in ("ga", "gb", "qs") + perm:
            off = _round_up(off, blockw[nm])
            new[nm] = off
            off += blockw[nm]
        if best is None or off < best[0]:
            best = (off, new)
    total = _round_up(best[0], 1024 if best[0] > 4096 else 512)
    return orig, best[1], blockw, total, o


def _permute_w_in(w, lay):
    orig, new, blockw, total, _ = lay
    parts, at = [], 0
    for nm in sorted(new, key=new.get):
        if new[nm] > at:
            parts.append(jnp.zeros((w.shape[0], new[nm] - at), w.dtype))
        o, wd = orig[nm]
        parts.append(w[:, o:o + wd])
        if blockw[nm] > wd:
            parts.append(jnp.zeros((w.shape[0], blockw[nm] - wd), w.dtype))
        at = new[nm] + blockw[nm]
    if total > at:
        parts.append(jnp.zeros((w.shape[0], total - at), w.dtype))
    return jnp.concatenate(parts, axis=1)


def _unpermute_w_in(wp, lay):
    orig, new, _, _, _ = lay
    return jnp.concatenate([wp[:, new[nm]:new[nm] + orig[nm][1]] for nm in sorted(orig, key=lambda n: orig[n][0])],
                           axis=1)


def _assemble_dz(parts, lay, S):
    _, new, blockw, total, _ = lay
    cols, at = [], 0
    for nm in sorted(new, key=new.get):
        if new[nm] > at:
            cols.append(jnp.zeros((S, new[nm] - at), BF16))
        cols.append(parts[nm])
        at = new[nm] + blockw[nm]
    if total > at:
        cols.append(jnp.zeros((S, total - at), BF16))
    return jnp.concatenate(cols, axis=1)


def _unshard_cols(g):
    return jnp.transpose(g, (1, 0, 2)).reshape(g.shape[1], N_DEV * g.shape[2])


def _shard_cols(w):
    K, N = w.shape
    return jnp.transpose(w.reshape(K, N_DEV, N // N_DEV), (1, 0, 2))


def _pack(vecs, rows):
    flat = jnp.concatenate([v.reshape(-1) for v in vecs])
    return jnp.pad(flat, (0, rows * LANES - flat.shape[0])).reshape(rows, LANES)


def kernel(x, c, w_ada, b_ada, g_pre_mix, g_post_mix, w_in, g_q_lat, w_uq, g_kv_lat, w_ukv, rel_bias, sinks, w_o, g_pre_ffn, g_post_ffn, w_up, conv_w, conv_b, w_down, loss_target, m_w_ada, m_b_ada, m_g_pre_mix, m_g_post_mix, m_w_in, m_g_q_lat, m_w_uq, m_g_kv_lat, m_w_ukv, m_rel_bias, m_sinks, m_w_o, m_g_pre_ffn, m_g_post_ffn, m_w_up, m_conv_w, m_conv_b, m_w_down, v_w_ada, v_b_ada, v_g_pre_mix, v_g_post_mix, v_w_in, v_g_q_lat, v_w_uq, v_g_kv_lat, v_w_ukv, v_rel_bias, v_sinks, v_w_o, v_g_pre_ffn, v_g_post_ffn, v_w_up, v_conv_w, v_conv_b, v_w_down):
    S, D = x.shape[1], x.shape[2]
    Q_RANK, KV_RANK = g_q_lat.shape[1], g_kv_lat.shape[1]
    H_MLA = D // MLA_V
    H_SWA = D // SWA_HD
    G_SWA = H_SWA // SWA_KVH
    F2 = w_up.shape[2] * N_DEV
    F = F2 // 2
    ada_n = w_ada.shape[2]
    me = 4 * lax.axis_index("x") + 2 * lax.axis_index("y") + lax.axis_index("c")
    lay = _z_layout(D, Q_RANK, KV_RANK)
    _, zoff, _, NZ, in_cols = lay
    assert in_cols == w_in.shape[2] * N_DEV

    x2, tgt = x[0], loss_target[0]

    cw_n = conv_w.shape[2]
    small = jnp.concatenate([jnp.pad(c, ((0, 7), (0, 0))), jnp.pad(conv_w[0], ((0, 8 - CONV_WIDTH), (0, 0)))], axis=1)
    small_all = _all_gather(small, name="ag_cond", in_vmem=True)
    c_all = small_all[:, 0, :D]
    cw_full = _unshard_cols(small_all[:, :CONV_WIDTH, D:])
    b_cols = lax.dynamic_slice_in_dim(b_ada, me * ada_n, ada_n, axis=1)
    c_act, mod_cols = _ada_fwd(c_all, w_ada[0], b_cols, name="ada_fwd")
    mod_all = _all_gather(mod_cols, name="ag_mod", in_vmem=True)
    mod_me = lax.dynamic_index_in_dim(mod_all, me, axis=1, keepdims=False).reshape(1, N_DEV * ada_n)
    sh1, sc1, gt1, sh2, sc2, gt2 = [mod_me[:, k * D:(k + 1) * D] for k in range(6)]

    w_in_p = _permute_w_in(_unshard_cols(_all_gather(w_in[0].astype(BF16), name="ag_w_in", in_vmem=False)), lay)

    h1 = _prenorm(x2, g_pre_mix, sc1, sh1, name="prenorm_mix")
    z, (uq_g, ukv_g, o_g) = _matmul(h1, w_in_p, mode="nn", out_dtype=BF16, name="mm_in",
                                    comm=[("gather", w_uq[0].astype(BF16)), ("gather", w_ukv[0].astype(BF16)),
                                          ("gather", w_o[0].astype(BF16))])
    w_uq_p = jnp.pad(_unshard_cols(uq_g).reshape(Q_RANK, H_MLA, MLA_QK), ((0, 0), (0, 0), (0, MLA_QK_PAD - MLA_QK))
                     ).reshape(Q_RANK, H_MLA * MLA_QK_PAD)
    w_ukv_f = _unshard_cols(ukv_g)
    w_o_f = o_g.reshape(D, D)
    cqn = _prenorm(z, g_q_lat, None, None, name="norm_cq", off=zoff["cq"], width=Q_RANK)
    ckvn = _prenorm(z, g_kv_lat, None, None, name="norm_ckv", off=zoff["ckv"], width=KV_RANK)
    q_raw = _matmul(cqn, w_uq_p, mode="nn", out_dtype=BF16, name="mm_uq")
    kv = _matmul(ckvn, w_ukv_f, mode="nn", out_dtype=BF16, name="mm_ukv")
    tab_k = _rope_tables(S, LANES, 0)
    krr = _rope(z, tab_k, heads=1, width=LANES, transpose=False, name="rope_k", off=zoff["kr"])
    (o_a, lse, Qr), (up_g,) = _flash_fwd(q_raw, kv, krr, tab_k, heads=H_MLA, name="mla_fwd",
                                        comm=[("gather", w_up[0].astype(BF16))])
    w_up_f = _unshard_cols(up_g)

    bucket, valid = _t5_bucket_table()
    onehot = (jnp.asarray(bucket).reshape(-1, 1) == jnp.arange(LANES)[None, :]).astype(F32)
    rb_pad = jnp.pad(rel_bias, ((0, LANES - REL_BUCKETS), (0, LANES - H_SWA)))
    bias_t = _matmul(onehot, rb_pad, mode="nn", out_dtype=F32, name="bias_table", tm=2048, precision=HIGHEST)
    bias_full = jnp.transpose(bias_t[:, :H_SWA].reshape(BLOCK, 2 * BLOCK, H_SWA), (2, 0, 1))
    bias_full = jnp.where(jnp.asarray(valid)[None], bias_full, NEG)
    bias_full = jnp.transpose(bias_full.reshape(SWA_KVH, G_SWA, BLOCK, 2 * BLOCK), (0, 3, 1, 2)
                              ).reshape(SWA_KVH, 2 * BLOCK, G_SWA * BLOCK)
    sink_rows = jnp.broadcast_to(sinks.reshape(SWA_KVH, G_SWA, 1), (SWA_KVH, G_SWA, BLOCK)
                                 ).reshape(SWA_KVH, 1, G_SWA * BLOCK)
    kvw = SWA_KVH * SWA_HD

    def heads_first(t, n):
        return jnp.transpose(t.reshape(S, n, SWA_HD), (1, 0, 2))

    def heads_last(t):
        return jnp.transpose(t, (1, 0, 2)).reshape(S, t.shape[0] * SWA_HD)

    def queries_first(t):
        t = t.reshape(SWA_KVH, S // BLOCK, SWA_HD, G_SWA, BLOCK)
        return jnp.transpose(t, (1, 4, 0, 3, 2)).reshape(S, H_SWA * SWA_HD)

    qs_h = heads_first(z[:, zoff["qs"]:zoff["qs"] + D], H_SWA)
    ks_h = heads_first(z[:, zoff["ks"]:zoff["ks"] + kvw], SWA_KVH)
    vs_h = heads_first(z[:, zoff["vs"]:zoff["vs"] + kvw], SWA_KVH)
    o_b_h, _ = _swa_fwd(qs_h, ks_h, vs_h, bias_full, sink_rows, name="swa_fwd")
    o_b = queries_first(o_b_h)

    mixin = _gate_mix(z, o_a, o_b, D=D, off_a=zoff["ga"], off_b=zoff["gb"], name="gate_mix")
    mix = _matmul(mixin, w_o_f, mode="nn", out_dtype=F32, name="mm_o")
    x1 = _postnorm_res(x2, mix, gt1, g_post_mix, name="postnorm_mix")

    h2 = _prenorm(x1, g_pre_ffn, sc2, sh2, name="prenorm_ffn")
    up, (down_g,) = _matmul(h2, w_up_f, mode="nn", out_dtype=BF16, name="mm_up",
                            comm=[("gather", w_down[0].astype(BF16))])
    w_down_f = down_g.reshape(F, D)
    act = _conv_gate(up, cw_full, conv_b, name="conv_gate")
    y = _matmul(act, w_down_f, mode="nn", out_dtype=F32, name="mm_down")
    loss_part, dout, dy, dgt2, dg_post_ffn = _final_loss(x1, y, tgt, gt2, g_post_ffn, name="final_loss")
    loss = lax.psum(loss_part[0, 0], ("x", "y", "c"))

    dw_down = _matmul(act, dy, mode="tn", out_dtype=BF16, name="mm_down_dw")
    dact = _matmul(dy, w_down_f, mode="nt", out_dtype=BF16, name="mm_down_dx")
    (dup, dcw, dcb), (got_down,) = _conv_gate_bwd(up, dact, cw_full, conv_b, name="conv_gate_bwd",
                                                  comm=[("scatter", dw_down.reshape(N_DEV, F // N_DEV, D))])
    dw_up = _matmul(h2, dup, mode="tn", out_dtype=BF16, name="mm_up_dw", shard_out=True)
    dh2 = _matmul(dup, w_up_f, mode="nt", out_dtype=F32, name="mm_up_dx")
    dx1, dg_pre_ffn, dsc2, dsh2 = _prenorm_bwd(x1, dh2, dout, g_pre_ffn, sc2, name="prenorm_ffn_bwd", out_dtype=F32)

    dmix, dgt1, dg_post_mix = _postnorm_bwd(dx1, mix, gt1, g_post_mix, name="postnorm_mix_bwd")
    dw_o = _matmul(mixin, dmix, mode="tn", out_dtype=BF16, name="mm_o_dw")
    dmixin = _matmul(dmix, w_o_f, mode="nt", out_dtype=BF16, name="mm_o_dx")
    dga, dgb, do_a, do_b = _gate_mix_bwd(dmixin, z, o_a, o_b, D=D, off_a=zoff["ga"], off_b=zoff["gb"],
                                         name="gate_mix_bwd")
    dcw_parts = jnp.pad(_shard_cols(dcw), ((0, 0), (0, 16 - CONV_WIDTH), (0, 0)))
    (dqs_h, dks_h, dvs_h, dbias, dsink), (got_o, got_cw) = _swa_bwd(
        qs_h, ks_h, vs_h, bias_full, sink_rows, heads_first(do_b, H_SWA), name="swa_bwd",
        comm=[("scatter", dw_o.reshape(N_DEV, D // N_DEV, D)), ("scatter", dcw_parts)])
    dbias = jnp.transpose(dbias.reshape(SWA_KVH, 2 * BLOCK, G_SWA, BLOCK), (0, 2, 3, 1))
    drel_t = _matmul(dbias.reshape(H_SWA, BLOCK * 2 * BLOCK), onehot, mode="nn", out_dtype=F32, name="bias_grad",
                     tk=4096, precision=HIGHEST)
    d_rel_bias = jnp.transpose(drel_t[:, :REL_BUCKETS])
    d_sinks = jnp.sum(dsink.reshape(SWA_KVH, G_SWA, BLOCK), axis=-1).reshape(1, H_SWA)

    (dq_raw, dkv, dkr_parts), (got_up,) = _flash_bwd(Qr, kv, krr, do_a, o_a, lse, tab_k, heads=H_MLA, name="mla_bwd",
                                                     comm=[("scatter", dw_up)])
    dkr = _shared_rope_grad(dkr_parts, tab_k, name="rope_k_bwd")
    dcqn = _matmul(dq_raw, w_uq_p, mode="nt", out_dtype=F32, name="mm_uq_dx")
    dw_uq_p = _matmul(cqn, dq_raw, mode="tn", out_dtype=BF16, name="mm_uq_dw")
    dckvn = _matmul(dkv, w_ukv_f, mode="nt", out_dtype=F32, name="mm_ukv_dx")
    dw_ukv = _matmul(ckvn, dkv, mode="tn", out_dtype=BF16, name="mm_ukv_dw", shard_out=True)
    dw_uq = dw_uq_p.reshape(Q_RANK, H_MLA, MLA_QK_PAD)[:, :, :MLA_QK].reshape(Q_RANK, H_MLA * MLA_QK)
    dcq, dg_q = _prenorm_bwd(z, dcqn, None, g_q_lat, None, name="norm_cq_bwd", out_dtype=BF16,
                             off=zoff["cq"], width=Q_RANK)
    dckv, dg_kv = _prenorm_bwd(z, dckvn, None, g_kv_lat, None, name="norm_ckv_bwd", out_dtype=BF16,
                               off=zoff["ckv"], width=KV_RANK)
    dz = _assemble_dz({"ga": dga, "gb": dgb, "qs": queries_first(dqs_h), "cq": dcq, "ckv": dckv,
                       "ks": heads_last(dks_h), "vs": heads_last(dvs_h), "kr": dkr}, lay, S)
    dw_in_a, (got_uq, got_ukv) = _matmul(h1[:, :D // 2], dz, mode="tn", out_dtype=BF16, name="mm_in_dw_a",
                                         comm=[("scatter", _shard_cols(dw_uq)), ("scatter", dw_ukv)])
    dw_in_b, (got_in_a,) = _matmul(h1[:, D // 2:], dz, mode="tn", out_dtype=BF16, name="mm_in_dw_b",
                                   comm=[("scatter", _shard_cols(_unpermute_w_in(dw_in_a, lay)))])
    dh1, (got_in_b,) = _matmul(dz, w_in_p, mode="nt", out_dtype=F32, name="mm_in_dx",
                               comm=[("scatter", _shard_cols(_unpermute_w_in(dw_in_b, lay)))])
    grad_x, dg_pre_mix, dsc1, dsh1 = _prenorm_bwd(x2, dh1, dx1, g_pre_mix, sc1, name="prenorm_mix_bwd",
                                                  out_dtype=F32)
    dmod = jnp.concatenate([dsh1, dsc1, dgt1, dsh2, dsc2, dgt2], axis=1)

    small_names = ["b_ada", "g_pre_mix", "g_post_mix", "g_q_lat", "g_kv_lat", "rel_bias", "sinks", "g_pre_ffn",
                   "g_post_ffn", "conv_b"]
    small_w = [b_ada, g_pre_mix, g_post_mix, g_q_lat, g_kv_lat, rel_bias, sinks, g_pre_ffn, g_post_ffn, conv_b]
    small_m = [m_b_ada, m_g_pre_mix, m_g_post_mix, m_g_q_lat, m_g_kv_lat, m_rel_bias, m_sinks, m_g_pre_ffn,
               m_g_post_ffn, m_conv_b]
    small_v = [v_b_ada, v_g_pre_mix, v_g_post_mix, v_g_q_lat, v_g_kv_lat, v_rel_bias, v_sinks, v_g_pre_ffn,
               v_g_post_ffn, v_conv_b]
    small_g = [dmod, dg_pre_mix, dg_post_mix, dg_q, dg_kv, d_rel_bias, d_sinks, dg_pre_ffn, dg_post_ffn, dcb]
    n_small = sum(int(np.prod(w.shape)) for w in small_w)
    rows = _round_up(-(-n_small // LANES), 16)
    parts_small = _all_gather(_pack(small_g, rows), name="ag_small_grads", in_vmem=True)
    sg, sd, sm, sv = _adamw(_pack(small_w, rows), _pack(small_m, rows), _pack(small_v, rows), parts_small,
                            name="adamw_small")

    def unpack(packed):
        flat, out, at = packed.reshape(-1), {}, 0
        for nm, w in zip(small_names, small_w):
            n = int(np.prod(w.shape))
            out[nm] = flat[at:at + n].reshape(w.shape)
            at += n
        return out

    small_out = [unpack(t) for t in (sg, sd, sm, sv)]

    dmod_all = parts_small.reshape(N_DEV, rows * LANES)[:, :6 * D]
    dmod_cols = lax.dynamic_slice_in_dim(dmod_all, me * ada_n, ada_n, axis=1)
    ada_out = _adamw_ada(w_ada[0], m_w_ada[0], v_w_ada[0], jnp.transpose(c_act), dmod_cols, name="adamw_w_ada")

    def owner_update(got, w, m, v, name):
        shp = w.shape
        w2, m2, v2 = (t.reshape(shp[-2], shp[-1]) for t in (w, m, v))
        return [t.reshape(shp) for t in _adamw(w2, m2, v2, got, name="adamw_" + name)]

    def pad_rows(t):
        return jnp.pad(t[0], ((0, 16 - CONV_WIDTH), (0, 0)))

    big = {
        "w_in": owner_update([got_in_a, got_in_b], w_in, m_w_in, v_w_in, "w_in"),
        "w_uq": owner_update(got_uq, w_uq, m_w_uq, v_w_uq, "w_uq"),
        "w_ukv": owner_update(got_ukv, w_ukv, m_w_ukv, v_w_ukv, "w_ukv"),
        "w_o": owner_update(got_o, w_o, m_w_o, v_w_o, "w_o"),
        "w_up": owner_update(got_up, w_up, m_w_up, v_w_up, "w_up"),
        "w_down": owner_update(got_down, w_down, m_w_down, v_w_down, "w_down"),
    }
    cw_upd = _adamw(pad_rows(conv_w), pad_rows(m_conv_w), pad_rows(v_conv_w), got_cw, name="adamw_conv_w")
    big["conv_w"] = [t[:CONV_WIDTH].reshape(conv_w.shape) for t in cw_upd]
    big["w_ada"] = [t.reshape(w_ada.shape) for t in ada_out]

    order = ["w_ada", "b_ada", "g_pre_mix", "g_post_mix", "w_in", "g_q_lat", "w_uq", "g_kv_lat", "w_ukv", "rel_bias",
             "sinks", "w_o", "g_pre_ffn", "g_post_ffn", "w_up", "conv_w", "conv_b", "w_down"]
    outs = [loss, grad_x.reshape(x.shape)]
    for kind in range(4):
        for nm in order:
            outs.append(big[nm][kind] if nm in big else small_out[kind][nm])
    return tuple(outs)
```

```python
import functools
import itertools
import math

import numpy as np

import jax
import jax.numpy as jnp
from jax import lax
from jax.experimental import pallas as pl
from jax.experimental.pallas import tpu as pltpu

F32 = jnp.float32
BF16 = jnp.bfloat16

N_DEV = 8
MLA_NOPE = 128
MLA_ROPE = 64
MLA_V = 128
MLA_QK = MLA_NOPE + MLA_ROPE
MLA_QK_PAD = 256
ROPE_HALF = MLA_ROPE // 2
ROPE_THETA = 10000.0
SWA_HD = 64
SWA_KVH = 4
WINDOW = 128
BLOCK = 128
REL_BUCKETS = 32
REL_MAX_DIST = 128
CONV_WIDTH = 3
EPS = 1e-6
NEG = -1e30
ADAM_LR = 0.001
ADAM_B1 = 0.9
ADAM_B2 = 0.999
ADAM_EPS = 1e-08
ADAM_WD = 0.01
ADAM_STEP = 10
LANES = 128
HALO = 16
MESH = pl.DeviceIdType.MESH
HIGHEST = lax.Precision.HIGHEST

NN = (((1,), (0,)), ((), ()))
NT = (((1,), (1,)), ((), ()))
TN = (((0,), (0,)), ((), ()))


def _tile(n, pref, align=LANES):
    if n <= pref:
        return n
    t = (pref // align) * align
    while t >= align:
        if n % t == 0:
            return t
        t -= align
    return n


def _round_up(n, m):
    return (n + m - 1) // m * m


def _params(*sem):
    return pltpu.CompilerParams(dimension_semantics=sem)


def _sigmoid(x):
    return 1.0 / (1.0 + jnp.exp(-x))


def _my_place():
    return lax.axis_index("x"), lax.axis_index("y"), lax.axis_index("c")


def _all_gather(x, *, name, in_vmem):
    space = pltpu.VMEM if in_vmem else pl.ANY

    def body(x_ref, out_ref, send_sems, recv_sems, local_sem):
        x_, y_, c_ = _my_place()
        me, sibling = (x_, y_, c_), (x_, y_, 1 - c_)
        chips = [(1 - x_, y_), (x_, 1 - y_), (1 - x_, 1 - y_)]

        def slot(px, py, pc):
            return out_ref.at[4 * px + 2 * py + pc]

        def copy(k, block, to, src=None):
            return pltpu.make_async_remote_copy(
                src_ref=slot(*block) if src is None else src,
                dst_ref=slot(*block),
                send_sem=send_sems.at[k],
                recv_sem=recv_sems.at[k],
                device_id=to,
                device_id_type=MESH,
            )

        mine = pltpu.make_async_copy(x_ref, slot(*me), local_sem)
        mine.start()
        first = [copy(0, me, sibling, src=x_ref)]
        first += [copy(1 + j, me, (*chip, c_), src=x_ref) for j, chip in enumerate(chips)]
        for cp in first:
            cp.start()
        passed = [copy(4 + j, (*chip, c_), sibling) for j, chip in enumerate(chips)]
        for j, chip in enumerate(chips):
            copy(1 + j, (*chip, c_), me).wait_recv()
            passed[j].start()
        copy(0, sibling, me).wait_recv()
        for j, chip in enumerate(chips):
            copy(4 + j, (*chip, 1 - c_), me).wait_recv()
        for cp in first + passed:
            cp.wait_send()
        mine.wait()

    return pl.pallas_call(
        body,
        name=name,
        out_shape=jax.ShapeDtypeStruct((N_DEV,) + x.shape, x.dtype),
        in_specs=[pl.BlockSpec(memory_space=space)],
        out_specs=pl.BlockSpec(memory_space=space),
        scratch_shapes=[
            pltpu.SemaphoreType.DMA((7,)),
            pltpu.SemaphoreType.DMA((7,)),
            pltpu.SemaphoreType.DMA,
        ],
    )(x)


class _Exchange:
    def __init__(self, kind, x_ref, out_ref, send_sems, recv_sems, local_sems, t):
        x_, y_, c_ = _my_place()
        me = 4 * x_ + 2 * y_ + c_

        def pair(k, src, dst, to):
            return pltpu.make_async_remote_copy(src_ref=src, dst_ref=dst, send_sem=send_sems.at[7 * t + k],
                                                recv_sem=recv_sems.at[7 * t + k], device_id=to, device_id_type=MESH)

        none = lambda: []
        if kind == "scatter":
            peers = [(x_ ^ ((r >> 2) & 1), y_ ^ ((r >> 1) & 1), c_ ^ (r & 1)) for r in range(1, N_DEV)]
            self.at_start = lambda: [pair(k, x_ref.at[4 * px + 2 * py + pc], out_ref.at[me], (px, py, pc))
                                     for k, (px, py, pc) in enumerate(peers)]
            self.relay_after, self.at_relay = none, none
            self.arrivals = self.at_start
            self.own = lambda: pltpu.make_async_copy(x_ref.at[me], out_ref.at[me], local_sems.at[t])
        else:
            sibling = (x_, y_, 1 - c_)
            chips = list(enumerate([(1 - x_, y_), (x_, 1 - y_), (1 - x_, 1 - y_)]))

            def slot(px, py, pc):
                return out_ref.at[4 * px + 2 * py + pc]

            mine = slot(x_, y_, c_)
            self.at_start = lambda: ([pair(0, x_ref, mine, sibling)]
                                     + [pair(1 + j, x_ref, mine, (*chip, c_)) for j, chip in chips])
            self.relay_after = lambda: [pair(1 + j, slot(*chip, c_), slot(*chip, c_), (*chip, c_)) for j, chip in chips]
            self.at_relay = lambda: [pair(4 + j, slot(*chip, c_), slot(*chip, c_), sibling) for j, chip in chips]
            self.arrivals = lambda: ([pair(0, slot(*sibling), slot(*sibling), sibling)]
                                     + [pair(4 + j, slot(*chip, 1 - c_), slot(*chip, 1 - c_), sibling)
                                        for j, chip in chips])
            self.own = lambda: pltpu.make_async_copy(x_ref, mine, local_sems.at[t])

    def start(self):
        self.own().start()
        for cp in self.at_start():
            cp.start()

    def relay(self):
        for landed, onward in zip(self.relay_after(), self.at_relay()):
            landed.wait_recv()
            onward.start()

    def finish(self):
        for cp in self.arrivals():
            cp.wait_recv()
        for cp in self.at_start() + self.at_relay():
            cp.wait_send()
        self.own().wait()


RELAY_AT = 0.7


def _call(body, *, name, grid, in_specs, out_specs, out_shape, args, scratch_shapes=(), sem=(), comm=(), prefetch=()):
    n_pf = len(prefetch)

    def launch(fn, ins, outs, shapes, scratch, semantics, operands):
        spec = pltpu.PrefetchScalarGridSpec(num_scalar_prefetch=n_pf, grid=grid, in_specs=ins, out_specs=outs,
                                            scratch_shapes=scratch)
        return pl.pallas_call(fn, name=name, grid_spec=spec, out_shape=shapes,
                              compiler_params=_params(*semantics))(*prefetch, *operands)

    if not comm:
        return list(launch(body, list(in_specs), list(out_specs), list(out_shape), list(scratch_shapes), sem, args)), []
    n_in, n_out, n_c, n_s = len(in_specs), len(out_specs), len(comm), len(scratch_shapes)
    kinds = [kind for kind, _ in comm]
    hbm = pl.BlockSpec(memory_space=pl.ANY)

    def wrapped(*refs):
        tables, refs = refs[:n_pf], refs[n_pf:]
        ins, cin = refs[:n_in], refs[n_in:n_in + n_c]
        at = n_in + n_c
        outs, cout = refs[at:at + n_out], refs[at + n_out:at + n_out + n_c]
        scr = refs[at + n_out + n_c:at + n_out + n_c + n_s]
        send, recv, local = refs[-3:]
        step = 0
        for a, g in enumerate(grid):
            step = step * g + pl.program_id(a)
        n_steps = int(np.prod(grid))

        def exchanges():
            return [_Exchange(kinds[t], cin[t], cout[t], send, recv, local, t) for t in range(n_c)]

        @pl.when(step == 0)
        def _():
            for ex in exchanges():
                ex.start()

        body(*tables, *ins, *outs, *scr)

        @pl.when(step == min(int(RELAY_AT * n_steps), n_steps - 1))
        def _():
            for ex in exchanges():
                ex.relay()

        @pl.when(step == n_steps - 1)
        def _():
            for ex in exchanges():
                ex.finish()

    c_shapes = [jax.ShapeDtypeStruct(((N_DEV,) + a.shape) if kind == "gather" else a.shape, a.dtype)
                for kind, a in comm]
    sems = [pltpu.SemaphoreType.DMA((7 * n_c,)), pltpu.SemaphoreType.DMA((7 * n_c,)), pltpu.SemaphoreType.DMA((n_c,))]
    res = launch(wrapped, list(in_specs) + [hbm] * n_c, list(out_specs) + [hbm] * n_c, list(out_shape) + c_shapes,
                 list(scratch_shapes) + sems, ["arbitrary"] * len(grid), (*args, *[a for _, a in comm]))
    return list(res[:n_out]), list(res[n_out:])


def _matmul(a, b, *, mode, out_dtype, name, tm=1024, tn=1024, tk=2816, precision=None, comm=(), shard_out=False,
            halves=False, m_range=None):
    if mode == "nn":
        (M, K), (K2, N) = a.shape, b.shape
    elif mode == "nt":
        (M, K), (N, K2) = (a.shape[1], 2 * a.shape[2]) if halves else a.shape, b.shape
    else:
        (K, M), (K2, N) = a.shape, (b.shape[1], 2 * b.shape[2]) if halves else b.shape
    assert K == K2, (a.shape, b.shape, mode)
    m_off = 0
    if m_range is not None:
        m_off, M = m_range
    tm = _tile(M, tm, LANES if mode == "tn" else 16)
    tk = _tile(K // 2 if halves and mode == "nt" else K, tk)
    tn = _tile(N // N_DEV, max(tn, 1408)) if shard_out else _tile(N // 2 if halves and mode == "tn" else N, tn)
    nk = K // tk
    m_off //= tm
    if mode == "tn":
        a_spec = pl.BlockSpec((tk, tm), lambda i, j, k: (k, i + m_off))
    elif halves:
        a_spec = pl.BlockSpec((None, tm, tk), lambda i, j, k: (k // (nk // 2), i, k % (nk // 2)))
    else:
        a_spec = pl.BlockSpec((tm, tk), lambda i, j, k: (i, k))
    if mode == "nt":
        b_spec = pl.BlockSpec((tn, tk), lambda i, j, k: (j, k))
    elif halves:
        nj = N // tn
        b_spec = pl.BlockSpec((None, tk, tn), lambda i, j, k: (j // (nj // 2), k, j % (nj // 2)))
    else:
        b_spec = pl.BlockSpec((tk, tn), lambda i, j, k: (k, j))
    dn = {"nn": NN, "nt": NT, "tn": TN}[mode]
    if shard_out:
        per = N // N_DEV // tn
        o_spec = pl.BlockSpec((None, tm, tn), lambda i, j, k: (j // per, i, j % per))
        o_shape = jax.ShapeDtypeStruct((N_DEV, M, N // N_DEV), out_dtype)
    else:
        o_spec = pl.BlockSpec((tm, tn), lambda i, j, k: (i, j))
        o_shape = jax.ShapeDtypeStruct((M, N), out_dtype)

    def product(a_ref, b_ref):
        return lax.dot_general(a_ref[...], b_ref[...], dn, preferred_element_type=F32, precision=precision)

    def body_one(a_ref, b_ref, o_ref):
        o_ref[...] = product(a_ref, b_ref).astype(o_ref.dtype)

    def body_acc(a_ref, b_ref, o_ref, acc_ref):
        k = pl.program_id(2)

        @pl.when(k == 0)
        def _():
            acc_ref[...] = product(a_ref, b_ref)

        @pl.when(k > 0)
        def _():
            acc_ref[...] += product(a_ref, b_ref)

        @pl.when(k == nk - 1)
        def _():
            o_ref[...] = acc_ref[...].astype(o_ref.dtype)

    outs, moved = _call(
        body_one if nk == 1 else body_acc,
        name=name,
        grid=(M // tm, N // tn, nk),
        in_specs=[a_spec, b_spec],
        out_specs=[o_spec],
        out_shape=[o_shape],
        scratch_shapes=[] if nk == 1 else [pltpu.VMEM((tm, tn), F32)],
        sem=("parallel", "parallel", "arbitrary"),
        args=(a, b),
        comm=comm,
    )
    return (outs[0], moved) if comm else outs[0]


def _rstd(xf):
    return lax.rsqrt(jnp.mean(xf * xf, axis=-1, keepdims=True) + EPS)


def _col_view(width, off):
    assert off % width == 0
    return off // width


def _prenorm(x, g, sc, sh, *, name, off=0, width=None):
    S = x.shape[0]
    W = x.shape[1] if width is None else width
    cb = _col_view(W, off)
    tr = _tile(S, 512, 16)
    mod = sc is not None
    vec = pl.BlockSpec((1, W), lambda i: (0, 0))

    def body(*refs):
        if mod:
            x_ref, g_ref, sc_ref, sh_ref, o_ref = refs
        else:
            x_ref, g_ref, o_ref = refs
        xf = x_ref[...].astype(F32)
        y = xf * _rstd(xf) * g_ref[...]
        if mod:
            y = y * (1.0 + sc_ref[...]) + sh_ref[...]
        o_ref[...] = y.astype(o_ref.dtype)

    args = (x, g, sc, sh) if mod else (x, g)
    return pl.pallas_call(
        body,
        name=name,
        grid=(S // tr,),
        in_specs=[pl.BlockSpec((tr, W), lambda i: (i, cb))] + [vec] * (len(args) - 1),
        out_specs=pl.BlockSpec((tr, W), lambda i: (i, 0)),
        out_shape=jax.ShapeDtypeStruct((S, W), BF16),
        compiler_params=_params("parallel"),
    )(*args)


def _prenorm_bwd(x, dh, dres, g, sc, *, name, out_dtype, off=0, width=None):
    S = x.shape[0]
    W = x.shape[1] if width is None else width
    cb = _col_view(W, off)
    tr = _tile(S, 256, 16)
    mod = sc is not None
    res = dres is not None
    vec = pl.BlockSpec((1, W), lambda i: (0, 0))
    row = pl.BlockSpec((tr, W), lambda i: (i, 0))

    def body(*refs):
        it = iter(refs)
        x_ref, dh_ref = next(it), next(it)
        dres_ref = next(it) if res else None
        g_ref = next(it)
        sc_ref = next(it) if mod else None
        dx_ref, dg_ref = next(it), next(it)
        dsc_ref, dsh_ref = (next(it), next(it)) if mod else (None, None)
        i = pl.program_id(0)

        @pl.when(i == 0)
        def _():
            dg_ref[...] = jnp.zeros_like(dg_ref)
            if mod:
                dsc_ref[...] = jnp.zeros_like(dsc_ref)
                dsh_ref[...] = jnp.zeros_like(dsh_ref)

        xf = x_ref[...].astype(F32)
        r = _rstd(xf)
        xn = xf * r
        dhf = dh_ref[...].astype(F32)
        gv = g_ref[...]
        if mod:
            one_sc = 1.0 + sc_ref[...]
            dsh_ref[...] += jnp.sum(dhf, axis=0, keepdims=True)
            dsc_ref[...] += jnp.sum(dhf * (xn * gv), axis=0, keepdims=True)
            dg_ref[...] += jnp.sum(dhf * xn * one_sc, axis=0, keepdims=True)
            dxn = dhf * (gv * one_sc)
        else:
            dg_ref[...] += jnp.sum(dhf * xn, axis=0, keepdims=True)
            dxn = dhf * gv
        dx = r * (dxn - xn * jnp.mean(dxn * xn, axis=-1, keepdims=True))
        if res:
            dx = dx + dres_ref[...]
        dx_ref[...] = dx.astype(dx_ref.dtype)

    args = [x, dh] + ([dres] if res else []) + [g] + ([sc] if mod else [])
    in_specs = [pl.BlockSpec((tr, W), lambda i: (i, cb)), row] + ([row] if res else []) + [vec] + ([vec] if mod else [])
    n_vec = 3 if mod else 1
    outs = pl.pallas_call(
        body,
        name=name,
        grid=(S // tr,),
        in_specs=in_specs,
        out_specs=[row] + [vec] * n_vec,
        out_shape=[jax.ShapeDtypeStruct((S, W), out_dtype)] + [jax.ShapeDtypeStruct((1, W), F32)] * n_vec,
        compiler_params=_params("arbitrary"),
    )(*args)
    return outs


def _postnorm_res(x, y, gt, g, *, name):
    S, D = x.shape
    tr = _tile(S, 512, 8)
    row = pl.BlockSpec((tr, D), lambda i: (i, 0))
    vec = pl.BlockSpec((1, D), lambda i: (0, 0))

    def body(x_ref, y_ref, gt_ref, g_ref, o_ref):
        yf = y_ref[...]
        o_ref[...] = x_ref[...] + gt_ref[...] * (yf * _rstd(yf) * g_ref[...])

    return pl.pallas_call(
        body,
        name=name,
        grid=(S // tr,),
        in_specs=[row, row, vec, vec],
        out_specs=row,
        out_shape=jax.ShapeDtypeStruct((S, D), F32),
        compiler_params=_params("parallel"),
    )(x, y, gt, g)


def _postnorm_bwd(dx1, y, gt, g, *, name):
    S, D = y.shape
    tr = _tile(S, 256, 16)
    row = pl.BlockSpec((tr, D), lambda i: (i, 0))
    vec = pl.BlockSpec((1, D), lambda i: (0, 0))

    def body(dx_ref, y_ref, gt_ref, g_ref, dy_ref, dgt_ref, dg_ref):
        @pl.when(pl.program_id(0) == 0)
        def _():
            dgt_ref[...] = jnp.zeros_like(dgt_ref)
            dg_ref[...] = jnp.zeros_like(dg_ref)

        yf = y_ref[...]
        r = _rstd(yf)
        yn = yf * r
        d = dx_ref[...]
        gtv, gv = gt_ref[...], g_ref[...]
        dgt_ref[...] += jnp.sum(d * (yn * gv), axis=0, keepdims=True)
        dg_ref[...] += jnp.sum(d * gtv * yn, axis=0, keepdims=True)
        dyn = d * (gtv * gv)
        dy_ref[...] = (r * (dyn - yn * jnp.mean(dyn * yn, axis=-1, keepdims=True))).astype(dy_ref.dtype)

    return pl.pallas_call(
        body,
        name=name,
        grid=(S // tr,),
        in_specs=[row, row, vec, vec],
        out_specs=[row, vec, vec],
        out_shape=[jax.ShapeDtypeStruct((S, D), BF16), jax.ShapeDtypeStruct((1, D), F32),
                   jax.ShapeDtypeStruct((1, D), F32)],
        compiler_params=_params("arbitrary"),
    )(dx1, y, gt, g)


def _final_loss(x1, y, target, gt, g, *, name):
    S, D = y.shape
    tr = _tile(S, 256, 16)
    row = pl.BlockSpec((tr, D), lambda i: (i, 0))
    vec = pl.BlockSpec((1, D), lambda i: (0, 0))
    one = pl.BlockSpec((1, LANES), lambda i: (0, 0))

    def body(x_ref, y_ref, t_ref, gt_ref, g_ref, loss_ref, dout_ref, dy_ref, dgt_ref, dg_ref):
        @pl.when(pl.program_id(0) == 0)
        def _():
            loss_ref[...] = jnp.zeros_like(loss_ref)
            dgt_ref[...] = jnp.zeros_like(dgt_ref)
            dg_ref[...] = jnp.zeros_like(dg_ref)

        yf = y_ref[...]
        r = _rstd(yf)
        yn = yf * r
        gtv, gv = gt_ref[...], g_ref[...]
        out = x_ref[...] + gtv * (yn * gv)
        diff = out - t_ref[...]
        per_tok = jnp.mean(diff * diff, axis=-1, keepdims=True)
        loss_ref[...] += 0.5 * jnp.sum(per_tok, axis=0, keepdims=True)
        d = diff / D
        dout_ref[...] = d
        dgt_ref[...] += jnp.sum(d * (yn * gv), axis=0, keepdims=True)
        dg_ref[...] += jnp.sum(d * gtv * yn, axis=0, keepdims=True)
        dyn = d * (gtv * gv)
        dy_ref[...] = (r * (dyn - yn * jnp.mean(dyn * yn, axis=-1, keepdims=True))).astype(dy_ref.dtype)

    return pl.pallas_call(
        body,
        name=name,
        grid=(S // tr,),
        in_specs=[row, row, row, vec, vec],
        out_specs=[one, row, row, vec, vec],
        out_shape=[jax.ShapeDtypeStruct((1, LANES), F32), jax.ShapeDtypeStruct((S, D), F32),
                   jax.ShapeDtypeStruct((S, D), BF16), jax.ShapeDtypeStruct((1, D), F32),
                   jax.ShapeDtypeStruct((1, D), F32)],
        compiler_params=_params("arbitrary"),
    )(x1, y, target, gt, g)


def _ada_fwd(c_all, w_local, b_cols, *, name):
    B, D = c_all.shape
    N = w_local.shape[1]
    tn = _tile(N, 512)

    def body(c_ref, w_ref, b_ref, ca_ref, mod_ref):
        cv = c_ref[...]
        ca = cv * _sigmoid(cv)
        ca_ref[...] = ca
        mod_ref[...] = jnp.dot(ca, w_ref[...], preferred_element_type=F32, precision=HIGHEST) + b_ref[...]

    return pl.pallas_call(
        body,
        name=name,
        grid=(N // tn,),
        in_specs=[pl.BlockSpec((B, D), lambda j: (0, 0)), pl.BlockSpec((D, tn), lambda j: (0, j)),
                  pl.BlockSpec((1, tn), lambda j: (0, j))],
        out_specs=[pl.BlockSpec((B, D), lambda j: (0, 0)), pl.BlockSpec((B, tn), lambda j: (0, j))],
        out_shape=[jax.ShapeDtypeStruct((B, D), F32), jax.ShapeDtypeStruct((B, N), F32)],
        compiler_params=_params("arbitrary"),
    )(c_all, w_local, b_cols)


def _rope_tables(S, width, lane_off):
    pos = jnp.arange(S, dtype=F32)
    inv = ROPE_THETA ** (-jnp.arange(0, MLA_ROPE, 2, dtype=F32) / MLA_ROPE)
    ang = pos[:, None] * inv[None, :]
    ang = jnp.concatenate([ang, ang], axis=-1)
    cos, sin = jnp.cos(ang), jnp.sin(ang)
    first = (jnp.arange(MLA_ROPE) < ROPE_HALF)[None, :]
    sa = jnp.where(first, -sin, 0.0)
    sb = jnp.where(first, 0.0, sin)

    def place(t, fill):
        return jnp.pad(t, ((0, 0), (lane_off, width - lane_off - MLA_ROPE)), constant_values=fill)

    return place(cos, 1.0), place(sa, 0.0), place(sb, 0.0)


def _rope_apply(x, cos, sa, sb, width, transpose):
    if transpose:
        return x * cos + pltpu.roll(x * sa, ROPE_HALF, 1) + pltpu.roll(x * sb, width - ROPE_HALF, 1)
    return x * cos + pltpu.roll(x, width - ROPE_HALF, 1) * sa + pltpu.roll(x, ROPE_HALF, 1) * sb


def _rope(x, tables, *, heads, width, transpose, name, off=0, scale=1.0):
    S = x.shape[0]
    cb = _col_view(width, off)
    tr = _tile(S, 512, 16)
    tab = pl.BlockSpec((tr, width), lambda i, h: (i, 0))

    def body(x_ref, c_ref, sa_ref, sb_ref, o_ref):
        y = _rope_apply(x_ref[...].astype(F32), c_ref[...], sa_ref[...], sb_ref[...], width, transpose)
        o_ref[...] = (y if scale == 1.0 else y * scale).astype(o_ref.dtype)

    return pl.pallas_call(
        body,
        name=name,
        grid=(S // tr, heads),
        in_specs=[pl.BlockSpec((tr, width), lambda i, h: (i, cb + h)), tab, tab, tab],
        out_specs=pl.BlockSpec((tr, width), lambda i, h: (i, h)),
        out_shape=jax.ShapeDtypeStruct((S, heads * width), BF16),
        compiler_params=_params("parallel", "parallel"),
    )(x, *tables)


def _shared_rope_grad(parts, tables, *, name):
    P, S, _ = parts.shape
    tr = _tile(S, 512, 16)
    tab = pl.BlockSpec((tr, LANES), lambda i: (i, 0))

    def body(p_ref, c_ref, sa_ref, sb_ref, o_ref):
        acc = p_ref[0]
        for k in range(1, P):
            acc = acc + p_ref[k]
        o_ref[...] = _rope_apply(acc, c_ref[...], sa_ref[...], sb_ref[...], LANES, True).astype(o_ref.dtype)

    return pl.pallas_call(
        body,
        name=name,
        grid=(S // tr,),
        in_specs=[pl.BlockSpec((P, tr, LANES), lambda i: (0, i, 0)), tab, tab, tab],
        out_specs=tab,
        out_shape=jax.ShapeDtypeStruct((S, LANES), BF16),
        compiler_params=_params("parallel"),
    )(parts, *tables)


MLA_SCALE = MLA_QK ** -0.5
LOG2E = math.log2(math.e)
LN2 = math.log(2.0)
MLA_Q_PRESCALE = MLA_SCALE * LOG2E


def _lane_tile(v, n):
    return v if n == LANES else jnp.tile(v, (1, n // LANES))


def _causal_mask(s):
    rows = lax.broadcasted_iota(jnp.int32, s.shape, 0)
    cols = lax.broadcasted_iota(jnp.int32, s.shape, 1)
    return jnp.where(cols <= rows, s, NEG)


def _tri_blocks(nb, q_major):
    if q_major:
        pairs = [(q, k) for q in range(nb) for k in range(q + 1)]
    else:
        pairs = [(q, k) for k in range(nb) for q in range(k, nb)]
    return (jnp.asarray(np.array([p[0] for p in pairs], np.int32)),
            jnp.asarray(np.array([p[1] for p in pairs], np.int32)))


HEAD_PAIR = 2


def _flash_fwd(q_raw, KV, krr, tables, *, heads, name, comm=()):
    S = q_raw.shape[0]
    t = _tile(S, 512)
    nb = S // t
    qt, kt = _tri_blocks(nb, True)
    qw, vw = HEAD_PAIR * MLA_QK_PAD, HEAD_PAIR * MLA_V

    def body(qt_ref, kt_ref, q_ref, kn0_ref, kn1_ref, kr_ref, v0_ref, v1_ref, c_ref, sa_ref, sb_ref,
             o_ref, lse_ref, qr_ref, m_scr, l_scr, acc_scr):
        step_id = pl.program_id(1)
        qi, ki = qt_ref[step_id], kt_ref[step_id]

        @pl.when(ki == 0)
        def _():
            m_scr[...] = jnp.full_like(m_scr, NEG)
            l_scr[...] = jnp.zeros_like(l_scr)
            acc_scr[...] = jnp.zeros_like(acc_scr)
            for h in range(HEAD_PAIR):
                base = h * MLA_QK_PAD
                nope = q_ref[:, base:base + MLA_NOPE].astype(F32) * MLA_Q_PRESCALE
                rot = _rope_apply(q_ref[:, base + MLA_NOPE:base + MLA_QK_PAD].astype(F32), c_ref[...], sa_ref[...],
                                  sb_ref[...], LANES, False) * MLA_Q_PRESCALE
                qr_ref[:, base:base + MLA_NOPE] = nope.astype(qr_ref.dtype)
                qr_ref[:, base + MLA_NOPE:base + MLA_QK_PAD] = rot.astype(qr_ref.dtype)

        def step(diagonal):
            for h, (kn_ref, v_ref) in enumerate(((kn0_ref, v0_ref), (kn1_ref, v1_ref))):
                cols = slice(h * MLA_QK_PAD, (h + 1) * MLA_QK_PAD)
                k = jnp.concatenate([kn_ref[...], kr_ref[...]], axis=1)
                s = lax.dot_general(qr_ref[:, cols], k, NT, preferred_element_type=F32)
                if diagonal:
                    s = _causal_mask(s)
                m_prev = m_scr[h]
                m_new = jnp.maximum(m_prev, jnp.max(s, axis=1, keepdims=True))
                alpha = jnp.exp2(m_prev - m_new)
                p = jnp.exp2(s - _lane_tile(m_new, t))
                l_new = alpha * l_scr[h] + jnp.sum(p, axis=1, keepdims=True)
                acc = alpha * acc_scr[h] + jnp.dot(p.astype(BF16), v_ref[...], preferred_element_type=F32)
                if diagonal:
                    o_ref[:, h * MLA_V:(h + 1) * MLA_V] = (acc / l_new).astype(o_ref.dtype)
                    lse_ref[h] = m_new + jnp.log(l_new) * LOG2E
                else:
                    l_scr[h], acc_scr[h], m_scr[h] = l_new, acc, m_new

        pl.when(ki < qi)(lambda: step(False))
        pl.when(ki == qi)(lambda: step(True))

    def kvspec(h, half):
        return pl.BlockSpec((t, LANES), lambda hp, s, qt, kt: (kt[s], 2 * (HEAD_PAIR * hp + h) + half))

    qtab = pl.BlockSpec((t, LANES), lambda hp, s, qt, kt: (qt[s], 0))
    qrow = lambda hp, s, qt, kt: (qt[s], hp)
    return _call(
        body,
        name=name,
        grid=(heads // HEAD_PAIR, int(qt.shape[0])),
        in_specs=[pl.BlockSpec((t, qw), qrow), kvspec(0, 0), kvspec(1, 0),
                  pl.BlockSpec((t, LANES), lambda hp, s, qt, kt: (kt[s], 0)), kvspec(0, 1), kvspec(1, 1),
                  qtab, qtab, qtab],
        out_specs=[pl.BlockSpec((t, vw), qrow),
                   pl.BlockSpec((HEAD_PAIR, t, LANES), lambda hp, s, qt, kt: (hp, qt[s], 0)),
                   pl.BlockSpec((t, qw), qrow)],
        out_shape=[jax.ShapeDtypeStruct((S, heads * MLA_V), BF16),
                   jax.ShapeDtypeStruct((heads, S, LANES), F32),
                   jax.ShapeDtypeStruct((S, heads * MLA_QK_PAD), BF16)],
        scratch_shapes=[pltpu.VMEM((HEAD_PAIR, t, LANES), F32), pltpu.VMEM((HEAD_PAIR, t, LANES), F32),
                        pltpu.VMEM((HEAD_PAIR, t, MLA_V), F32)],
        sem=("parallel", "arbitrary"),
        args=(q_raw, KV, KV, krr, KV, KV, *tables),
        comm=comm,
        prefetch=(qt, kt),
    )


def _flash_bwd(Q, KV, krr, dO, O, lse, tables, *, heads, name, comm=()):
    S = Q.shape[0]
    t = _tile(S, 512)
    nb = S // t
    qt, kt = _tri_blocks(nb, False)
    n_steps = int(qt.shape[0])
    qw, vw = HEAD_PAIR * MLA_QK_PAD, HEAD_PAIR * MLA_V

    def body(qt_ref, kt_ref, q_ref, kn0_ref, kn1_ref, kr_ref, v0_ref, v1_ref, do_ref, o_ref, lse_ref,
             c_ref, sa_ref, sb_ref, dq_ref, dkv_ref, dkr_ref, dq_scr, dk_scr, dv_scr, delta_scr):
        step_id = pl.program_id(1)
        qi, ki = qt_ref[step_id], kt_ref[step_id]

        @pl.when(ki == 0)
        def _():
            for h in range(HEAD_PAIR):
                vc = slice(h * MLA_V, (h + 1) * MLA_V)
                d = jnp.sum(do_ref[:, vc].astype(F32) * o_ref[:, vc].astype(F32), axis=1, keepdims=True)
                delta_scr[h, qi] = jnp.broadcast_to(d, (t, LANES))

        def step(diagonal):
            for h, (kn_ref, v_ref) in enumerate(((kn0_ref, v0_ref), (kn1_ref, v1_ref))):
                base = h * MLA_QK_PAD
                cols = slice(base, base + MLA_QK_PAD)
                vc = slice(h * MLA_V, (h + 1) * MLA_V)
                q, do = q_ref[:, cols], do_ref[:, vc]
                k = jnp.concatenate([kn_ref[...], kr_ref[...]], axis=1)
                s = lax.dot_general(q, k, NT, preferred_element_type=F32)
                if diagonal:
                    s = _causal_mask(s)
                p = jnp.exp2(s - _lane_tile(lse_ref[h], t))
                dv = lax.dot_general(p.astype(BF16), do, TN, preferred_element_type=F32)
                dp = lax.dot_general(do, v_ref[...], NT, preferred_element_type=F32)
                ds = (p * (dp - _lane_tile(delta_scr[h, qi], t))).astype(BF16)
                dk = lax.dot_general(ds, q, TN, preferred_element_type=F32)
                dq = jnp.dot(ds, k, preferred_element_type=F32)
                if diagonal:
                    dk_scr[h], dv_scr[h] = dk, dv
                    dq = (dq_scr[qi, :, cols] + dq) * (LN2 * MLA_Q_PRESCALE)
                    rot = _rope_apply(dq[:, MLA_NOPE:], c_ref[...], sa_ref[...], sb_ref[...], LANES, True)
                    dq_ref[:, base:base + MLA_NOPE] = dq[:, :MLA_NOPE].astype(dq_ref.dtype)
                    dq_ref[:, base + MLA_NOPE:base + MLA_QK_PAD] = rot.astype(dq_ref.dtype)
                else:
                    dk_scr[h] += dk
                    dv_scr[h] += dv
                    dq_scr[qi, :, cols] += dq

        @pl.when(ki == 0)
        def _():
            dq_scr[qi] = jnp.zeros((t, qw), F32)

        pl.when(qi > ki)(lambda: step(False))
        pl.when(qi == ki)(lambda: step(True))

        @pl.when(qi == nb - 1)
        def _():
            shared = jnp.zeros((t, LANES), F32)
            for h in range(HEAD_PAIR):
                base = h * MLA_QK_PAD
                dk = dk_scr[h] * LN2
                dkv_ref[:, base:base + MLA_NOPE] = dk[:, :MLA_NOPE].astype(dkv_ref.dtype)
                dkv_ref[:, base + MLA_NOPE:base + MLA_QK_PAD] = dv_scr[h].astype(dkv_ref.dtype)
                shared = shared + dk[:, MLA_NOPE:]
            dkr_ref[0] = shared

    def kvspec(h, half):
        return pl.BlockSpec((t, LANES), lambda hp, s, qt, kt: (kt[s], 2 * (HEAD_PAIR * hp + h) + half))

    qrow = lambda hp, s, qt, kt: (qt[s], hp)
    krow = lambda hp, s, qt, kt: (kt[s], hp)
    ktab = pl.BlockSpec((t, LANES), lambda hp, s, qt, kt: (kt[s], 0))
    return _call(
        body,
        name=name,
        grid=(heads // HEAD_PAIR, n_steps),
        in_specs=[pl.BlockSpec((t, qw), qrow), kvspec(0, 0), kvspec(1, 0), ktab, kvspec(0, 1), kvspec(1, 1),
                  pl.BlockSpec((t, vw), qrow), pl.BlockSpec((t, vw), qrow),
                  pl.BlockSpec((HEAD_PAIR, t, LANES), lambda hp, s, qt, kt: (hp, qt[s], 0)),
                  ktab, ktab, ktab],
        out_specs=[pl.BlockSpec((t, qw), krow), pl.BlockSpec((t, qw), krow),
                   pl.BlockSpec((1, t, LANES), lambda hp, s, qt, kt: (hp, kt[s], 0))],
        out_shape=[jax.ShapeDtypeStruct((S, heads * MLA_QK_PAD), BF16),
                   jax.ShapeDtypeStruct((S, heads * MLA_QK_PAD), BF16),
                   jax.ShapeDtypeStruct((heads // HEAD_PAIR, S, LANES), F32)],
        scratch_shapes=[pltpu.VMEM((nb, t, qw), F32), pltpu.VMEM((HEAD_PAIR, t, MLA_QK_PAD), F32),
                        pltpu.VMEM((HEAD_PAIR, t, MLA_V), F32), pltpu.VMEM((HEAD_PAIR, nb, t, LANES), F32)],
        sem=("parallel", "arbitrary"),
        args=(Q, KV, KV, krr, KV, KV, dO, O, lse, *tables),
        comm=comm,
        prefetch=(qt, kt),
    )


SWA_SCALE = SWA_HD ** -0.5


def _t5_bucket_table():
    a = np.arange(BLOCK)[:, None]
    j = np.arange(2 * BLOCK)[None, :]
    dist = BLOCK + a - j
    max_exact = REL_BUCKETS // 2
    n = np.maximum(dist, 0)
    large = max_exact + (np.log(np.maximum(n, 1).astype(np.float32) / np.float32(max_exact))
                         / np.float32(math.log(REL_MAX_DIST / max_exact))
                         * np.float32(REL_BUCKETS - max_exact)).astype(np.int32)
    large = np.minimum(large, REL_BUCKETS - 1)
    bucket = np.where(n < max_exact, n, large)
    valid = (dist >= 0) & (dist < WINDOW)
    return bucket.astype(np.int32), valid


def _swa_probs(q_ref, kp_ref, kc_ref, bias_ref, sink_ref, qb, G):
    q2 = q_ref[...].reshape(G * BLOCK, SWA_HD)
    kb = jnp.concatenate([kp_ref[0], kc_ref[0]], axis=0)
    s = lax.dot_general(kb, q2, NT, preferred_element_type=F32) * SWA_SCALE + bias_ref[0]
    keys = lax.broadcasted_iota(jnp.int32, s.shape, 0)
    s = jnp.where((keys >= BLOCK) | (qb > 0), s, NEG)
    sink = sink_ref[0]
    m = jnp.maximum(jnp.max(s, axis=0, keepdims=True), sink)
    e = jnp.exp(s - m)
    es = jnp.exp(sink - m)
    inv = 1.0 / (jnp.sum(e, axis=0, keepdims=True) + es)
    return q2, kb, e * inv, es * inv


def _swa_fwd(q, k, v, bias_t, sink, *, name, comm=()):
    H, S, _ = q.shape
    G = H // SWA_KVH
    nb = S // BLOCK
    cur = lambda kh, qb: (kh, qb, 0)
    prev = lambda kh, qb: (kh, jnp.maximum(qb - 1, 0), 0)
    kvspec = lambda im: pl.BlockSpec((1, BLOCK, SWA_HD), im)

    def body(q_ref, kc_ref, kp_ref, vc_ref, vp_ref, bias_ref, sink_ref, o_ref):
        qb = pl.program_id(1)
        _, _, pt, _ = _swa_probs(q_ref, kp_ref, kc_ref, bias_ref, sink_ref, qb, G)
        vb = jnp.concatenate([vp_ref[0], vc_ref[0]], axis=0)
        o_ref[0, 0] = lax.dot_general(vb, pt.astype(BF16), TN, preferred_element_type=F32).astype(o_ref.dtype)

    outs, moved = _call(
        body,
        name=name,
        grid=(SWA_KVH, nb),
        in_specs=[pl.BlockSpec((G, BLOCK, SWA_HD), cur), kvspec(cur), kvspec(prev), kvspec(cur), kvspec(prev),
                  pl.BlockSpec((1, 2 * BLOCK, G * BLOCK), lambda kh, qb: (kh, 0, 0)),
                  pl.BlockSpec((1, 1, G * BLOCK), lambda kh, qb: (kh, 0, 0))],
        out_specs=[pl.BlockSpec((1, 1, SWA_HD, G * BLOCK), lambda kh, qb: (kh, qb, 0, 0))],
        out_shape=[jax.ShapeDtypeStruct((SWA_KVH, nb, SWA_HD, G * BLOCK), BF16)],
        sem=("parallel", "parallel"),
        args=(q, k, k, v, v, bias_t, sink),
        comm=comm,
    )
    return outs[0], moved


def _swa_bwd(q, k, v, bias_t, sink, do, *, name, comm=()):
    H, S, _ = q.shape
    G = H // SWA_KVH
    nb = S // BLOCK
    cur = lambda kh, qb: (kh, jnp.minimum(qb, nb - 1), 0)
    prev = lambda kh, qb: (kh, jnp.maximum(jnp.minimum(qb, nb - 1) - 1, 0), 0)
    lag = lambda kh, qb: (kh, jnp.maximum(qb - 1, 0), 0)
    kvspec = lambda im: pl.BlockSpec((1, BLOCK, SWA_HD), im)

    def body(q_ref, kc_ref, kp_ref, vc_ref, vp_ref, bias_ref, sink_ref, do_ref,
             dq_ref, dk_ref, dv_ref, dbias_ref, dsink_ref, ck_scr, cv_scr):
        qb = pl.program_id(1)

        @pl.when(qb == 0)
        def _():
            dbias_ref[...] = jnp.zeros_like(dbias_ref)
            dsink_ref[...] = jnp.zeros_like(dsink_ref)
            ck_scr[...] = jnp.zeros_like(ck_scr)
            cv_scr[...] = jnp.zeros_like(cv_scr)

        @pl.when(qb < nb)
        def _():
            q2, kb, pt, ps = _swa_probs(q_ref, kp_ref, kc_ref, bias_ref, sink_ref, qb, G)
            vb = jnp.concatenate([vp_ref[0], vc_ref[0]], axis=0)
            do2 = do_ref[...].reshape(G * BLOCK, SWA_HD)
            dpt = lax.dot_general(vb, do2, NT, preferred_element_type=F32)
            delta = jnp.sum(dpt * pt, axis=0, keepdims=True)
            dst = pt * (dpt - delta)
            dbias_ref[0] += dst
            dsink_ref[0] += -ps * delta
            dsb = (dst * SWA_SCALE).astype(BF16)
            dq_ref[0, 0] = lax.dot_general(kb, dsb, TN, preferred_element_type=F32).astype(dq_ref.dtype)
            dkb = jnp.dot(dsb, q2, preferred_element_type=F32)
            dvb = jnp.dot(pt.astype(BF16), do2, preferred_element_type=F32)
            dk_ref[0] = (ck_scr[...] + dkb[:BLOCK]).astype(dk_ref.dtype)
            dv_ref[0] = (cv_scr[...] + dvb[:BLOCK]).astype(dv_ref.dtype)
            ck_scr[...] = dkb[BLOCK:]
            cv_scr[...] = dvb[BLOCK:]

        @pl.when(qb == nb)
        def _():
            dk_ref[0] = ck_scr[...].astype(dk_ref.dtype)
            dv_ref[0] = cv_scr[...].astype(dv_ref.dtype)

    tspec = pl.BlockSpec((1, 1, SWA_HD, G * BLOCK), lambda kh, qb: (kh, jnp.minimum(qb, nb - 1), 0, 0))
    return _call(
        body,
        name=name,
        grid=(SWA_KVH, nb + 1),
        in_specs=[pl.BlockSpec((G, BLOCK, SWA_HD), cur), kvspec(cur), kvspec(prev), kvspec(cur), kvspec(prev),
                  pl.BlockSpec((1, 2 * BLOCK, G * BLOCK), lambda kh, qb: (kh, 0, 0)),
                  pl.BlockSpec((1, 1, G * BLOCK), lambda kh, qb: (kh, 0, 0)),
                  pl.BlockSpec((G, BLOCK, SWA_HD), cur)],
        out_specs=[tspec, kvspec(lag), kvspec(lag),
                   pl.BlockSpec((1, 2 * BLOCK, G * BLOCK), lambda kh, qb: (kh, 0, 0)),
                   pl.BlockSpec((1, 1, G * BLOCK), lambda kh, qb: (kh, 0, 0))],
        out_shape=[jax.ShapeDtypeStruct((SWA_KVH, nb, SWA_HD, G * BLOCK), BF16),
                   jax.ShapeDtypeStruct((SWA_KVH, S, SWA_HD), BF16),
                   jax.ShapeDtypeStruct((SWA_KVH, S, SWA_HD), BF16),
                   jax.ShapeDtypeStruct((SWA_KVH, 2 * BLOCK, G * BLOCK), F32),
                   jax.ShapeDtypeStruct((SWA_KVH, 1, G * BLOCK), F32)],
        scratch_shapes=[pltpu.VMEM((BLOCK, SWA_HD), F32), pltpu.VMEM((BLOCK, SWA_HD), F32)],
        sem=("parallel", "arbitrary"),
        args=(q, k, k, v, v, bias_t, sink, do),
        comm=comm,
    )


def _gate_mix(z, o_a, o_b, *, D, off_a, off_b, name):
    S = z.shape[0]
    tr = _tile(S, 256, 16)
    row = pl.BlockSpec((tr, D), lambda i: (i, 0))
    ca, cb = _col_view(D, off_a), _col_view(D, off_b)

    def body(ga_ref, gb_ref, oa_ref, ob_ref, m_ref):
        m = (_sigmoid(ga_ref[...].astype(F32)) * oa_ref[...].astype(F32)
             + _sigmoid(gb_ref[...].astype(F32)) * ob_ref[...].astype(F32))
        m_ref[...] = m.astype(m_ref.dtype)

    return pl.pallas_call(
        body,
        name=name,
        grid=(S // tr,),
        in_specs=[pl.BlockSpec((tr, D), lambda i: (i, ca)), pl.BlockSpec((tr, D), lambda i: (i, cb)), row, row],
        out_specs=row,
        out_shape=jax.ShapeDtypeStruct((S, D), BF16),
        compiler_params=_params("parallel"),
    )(z, z, o_a, o_b)


def _gate_mix_bwd(dm, z, o_a, o_b, *, D, off_a, off_b, name):
    S = z.shape[0]
    tr = _tile(S, 256, 16)
    row = pl.BlockSpec((tr, D), lambda i: (i, 0))
    ca, cb = _col_view(D, off_a), _col_view(D, off_b)

    def body(dm_ref, ga_ref, gb_ref, oa_ref, ob_ref, dga_ref, dgb_ref, doa_ref, dob_ref):
        d = dm_ref[...].astype(F32)
        for g_ref, o_ref, dg_ref, do_ref in ((ga_ref, oa_ref, dga_ref, doa_ref), (gb_ref, ob_ref, dgb_ref, dob_ref)):
            sg = _sigmoid(g_ref[...].astype(F32))
            dg_ref[...] = (d * o_ref[...].astype(F32) * (sg * (1.0 - sg))).astype(dg_ref.dtype)
            do_ref[...] = (d * sg).astype(do_ref.dtype)

    return pl.pallas_call(
        body,
        name=name,
        grid=(S // tr,),
        in_specs=[row, pl.BlockSpec((tr, D), lambda i: (i, ca)), pl.BlockSpec((tr, D), lambda i: (i, cb)), row, row],
        out_specs=[row] * 4,
        out_shape=[jax.ShapeDtypeStruct((S, D), BF16)] * 4,
        compiler_params=_params("parallel"),
    )(dm, z, z, o_a, o_b)


CONV_ROWS = 256
CONV_COLS = 1408
SUBLANES = 8


def _shift_matrices(tr):
    r = np.arange(tr)[:, None]
    c = np.arange(tr)[None, :]
    back = [jnp.asarray(r == c + d, dtype=BF16) for d in (1, 2)]
    ahead = [jnp.asarray(r + d == c, dtype=BF16) for d in (1, 2)]
    return back, ahead


def _rows_before(x, halo_ref, first, b1_ref, b2_ref):
    s1 = jnp.dot(b1_ref[...], x, preferred_element_type=F32)
    s2 = jnp.dot(b2_ref[...], x, preferred_element_type=F32)
    h8 = jnp.where(first, 0.0, halo_ref[...].astype(F32)[HALO - SUBLANES:])
    rows = lax.broadcasted_iota(jnp.int32, h8.shape, 0)
    fix1 = jnp.where(rows < 1, pltpu.roll(h8, 1, 0), 0.0)
    fix2 = jnp.where(rows < 2, pltpu.roll(h8, 2, 0), 0.0)
    s1 = jnp.concatenate([s1[:SUBLANES] + fix1, s1[SUBLANES:]], axis=0)
    s2 = jnp.concatenate([s2[:SUBLANES] + fix2, s2[SUBLANES:]], axis=0)
    return s1, s2


def _conv_taps(x, s1, s2, cw_ref, cb_ref):
    return cb_ref[...] + cw_ref[0:1, :] * s2 + cw_ref[1:2, :] * s1 + cw_ref[2:3, :] * x


def _conv_gate(up, cw, cb, *, name):
    S, F2 = up.shape
    F = F2 // 2
    tr = _tile(S, CONV_ROWS, HALO)
    tc = _tile(F, CONV_COLS)
    nc = F // tc
    hb = tr // HALO
    back, _ = _shift_matrices(tr)
    mat = pl.BlockSpec((tr, tr), lambda i, j: (0, 0))

    def halo_map(shift):
        return lambda i, j: (jnp.maximum(i * hb - 1, 0), j + shift)

    def body(x1_ref, h1_ref, x2_ref, h2_ref, cw1_ref, cw2_ref, cb1_ref, cb2_ref, b1_ref, b2_ref, a_ref):
        first = pl.program_id(0) == 0
        us = []
        for x_ref, h_ref, cw_ref, cb_ref in ((x1_ref, h1_ref, cw1_ref, cb1_ref), (x2_ref, h2_ref, cw2_ref, cb2_ref)):
            x = x_ref[...]
            s1, s2 = _rows_before(x, h_ref, first, b1_ref, b2_ref)
            us.append(_conv_taps(x.astype(F32), s1, s2, cw_ref, cb_ref))
        u1, u2 = us
        a_ref[...] = (u1 * _sigmoid(u1) * u2).astype(a_ref.dtype)

    return pl.pallas_call(
        body,
        name=name,
        grid=(S // tr, nc),
        in_specs=[pl.BlockSpec((tr, tc), lambda i, j: (i, j)), pl.BlockSpec((HALO, tc), halo_map(0)),
                  pl.BlockSpec((tr, tc), lambda i, j: (i, j + nc)), pl.BlockSpec((HALO, tc), halo_map(nc)),
                  pl.BlockSpec((CONV_WIDTH, tc), lambda i, j: (0, j)),
                  pl.BlockSpec((CONV_WIDTH, tc), lambda i, j: (0, j + nc)),
                  pl.BlockSpec((1, tc), lambda i, j: (0, j)), pl.BlockSpec((1, tc), lambda i, j: (0, j + nc)),
                  mat, mat],
        out_specs=pl.BlockSpec((tr, tc), lambda i, j: (i, j)),
        out_shape=jax.ShapeDtypeStruct((S, F), BF16),
        compiler_params=_params("parallel", "parallel"),
    )(up, up, up, up, cw, cw, cb, cb, *back)


def _conv_gate_bwd(up, da, cw, cb, *, name, comm=()):
    S, F2 = up.shape
    F = F2 // 2
    tr = _tile(S, CONV_ROWS, HALO)
    tc = _tile(F, CONV_COLS)
    nc = F // tc
    hb = tr // HALO
    ni = S // tr
    back, ahead = _shift_matrices(tr)
    mat = pl.BlockSpec((tr, tr), lambda j, r: (0, 0))

    def cur(shift):
        return lambda j, r: (ni - 1 - r, j + shift)

    def before(shift):
        return lambda j, r: (jnp.maximum((ni - 1 - r) * hb - 1, 0), j + shift)

    def vec(rows, shift):
        return pl.BlockSpec((rows, tc), lambda j, r: (0, j + shift))

    def body(x1_ref, h1_ref, x2_ref, h2_ref, da_ref, cw1_ref, cw2_ref, cb1_ref, cb2_ref,
             b1_ref, b2_ref, a1_ref, a2_ref, dup_ref, dcw_ref, dcb_ref, next_du):
        r = pl.program_id(1)
        first = r == ni - 1

        @pl.when(r == 0)
        def _():
            dcw_ref[...] = jnp.zeros_like(dcw_ref)
            dcb_ref[...] = jnp.zeros_like(dcb_ref)
            next_du[...] = jnp.zeros_like(next_du)

        x1, x2 = x1_ref[...], x2_ref[...]
        x1f, x2f = x1.astype(F32), x2.astype(F32)
        s11, s12 = _rows_before(x1, h1_ref, first, b1_ref, b2_ref)
        s21, s22 = _rows_before(x2, h2_ref, first, b1_ref, b2_ref)
        u1 = _conv_taps(x1f, s11, s12, cw1_ref, cb1_ref)
        u2 = _conv_taps(x2f, s21, s22, cw2_ref, cb2_ref)
        sg = _sigmoid(u1)
        daf = da_ref[...].astype(F32)
        du1 = daf * u2 * (sg * (1.0 + u1 * (1.0 - sg)))
        du2 = daf * (u1 * sg)
        rows = lax.broadcasted_iota(jnp.int32, (SUBLANES, tc), 0)

        for half, (du, own, own1, own2, cw_ref) in enumerate(((du1, x1f, s11, s12, cw1_ref),
                                                             (du2, x2f, s21, s22, cw2_ref))):
            du_b = du.astype(BF16)
            n1 = jnp.dot(a1_ref[...], du_b, preferred_element_type=F32)
            n2 = jnp.dot(a2_ref[...], du_b, preferred_element_type=F32)
            c8 = next_du[half]
            fix1 = jnp.where(rows >= SUBLANES - 1, pltpu.roll(c8, SUBLANES - 1, 0), 0.0)
            fix2 = jnp.where(rows >= SUBLANES - 2, pltpu.roll(c8, SUBLANES - 2, 0), 0.0)
            n1 = jnp.concatenate([n1[:tr - SUBLANES], n1[tr - SUBLANES:] + fix1], axis=0)
            n2 = jnp.concatenate([n2[:tr - SUBLANES], n2[tr - SUBLANES:] + fix2], axis=0)
            dup = cw_ref[2:3, :] * du + cw_ref[1:2, :] * n1 + cw_ref[0:1, :] * n2
            dup_ref[half] = dup.astype(dup_ref.dtype)
            dcb_ref[half] += jnp.sum(du, axis=0, keepdims=True)
            for tap, shifted in enumerate((own2, own1, own)):
                dcw_ref[half, tap:tap + 1, :] += jnp.sum(du * shifted, axis=0, keepdims=True)
            next_du[half] = du[:SUBLANES].astype(BF16).astype(F32)

    return _call(
        body,
        name=name,
        grid=(nc, ni),
        in_specs=[pl.BlockSpec((tr, tc), cur(0)), pl.BlockSpec((HALO, tc), before(0)),
                  pl.BlockSpec((tr, tc), cur(nc)), pl.BlockSpec((HALO, tc), before(nc)),
                  pl.BlockSpec((tr, tc), cur(0)),
                  vec(CONV_WIDTH, 0), vec(CONV_WIDTH, nc), vec(1, 0), vec(1, nc), mat, mat, mat, mat],
        out_specs=[pl.BlockSpec((2, tr, tc), lambda j, r: (0, ni - 1 - r, j)),
                   pl.BlockSpec((2, CONV_WIDTH, tc), lambda j, r: (0, 0, j)),
                   pl.BlockSpec((2, 1, tc), lambda j, r: (0, 0, j))],
        out_shape=[jax.ShapeDtypeStruct((2, S, F), BF16), jax.ShapeDtypeStruct((2, CONV_WIDTH, F), F32),
                   jax.ShapeDtypeStruct((2, 1, F), F32)],
        scratch_shapes=[pltpu.VMEM((2, SUBLANES, tc), F32)],
        sem=("parallel", "arbitrary"),
        args=(up, up, up, up, da, cw, cw, cb, cb, *back, *ahead),
        comm=comm,
    )


def _adam_math(w, g, m, v):
    m = ADAM_B1 * m + (1.0 - ADAM_B1) * g
    v = ADAM_B2 * v + (1.0 - ADAM_B2) * (g * g)
    m_hat = m / (1.0 - ADAM_B1 ** ADAM_STEP)
    v_hat = v / (1.0 - ADAM_B2 ** ADAM_STEP)
    delta = -ADAM_LR * (m_hat / (jnp.sqrt(v_hat) + ADAM_EPS) + ADAM_WD * w)
    return delta, m, v


def _adamw(w, m, v, parts, *, name):
    R, C = w.shape
    plist = list(parts) if isinstance(parts, (list, tuple)) else [parts]
    tr = _tile(min(p.shape[1] for p in plist), 256, 16)
    assert sum(p.shape[1] for p in plist) == R and all(p.shape[1] % tr == 0 for p in plist)
    row = pl.BlockSpec((tr, C), lambda i: (i, 0))
    first, spans = 0, []
    for p in plist:
        spans.append((first, first + p.shape[1] // tr))
        first = spans[-1][1]

    def body(w_ref, m_ref, v_ref, *rest):
        p_refs, (g_ref, d_ref, m2_ref, v2_ref) = rest[:len(plist)], rest[len(plist):]
        i = pl.program_id(0)

        def update(p_ref):
            g = p_ref[0].astype(F32)
            for k in range(1, N_DEV):
                g = g + p_ref[k].astype(F32)
            g_ref[...] = g
            d_ref[...], m2_ref[...], v2_ref[...] = _adam_math(w_ref[...], g, m_ref[...], v_ref[...])

        if len(plist) == 1:
            update(p_refs[0])
        else:
            for p_ref, (lo, hi) in zip(p_refs, spans):
                pl.when((i >= lo) & (i < hi))(functools.partial(update, p_ref))

    def part_spec(lo, hi):
        return pl.BlockSpec((N_DEV, tr, C), lambda i: (0, jnp.clip(i - lo, 0, hi - lo - 1), 0))

    return pl.pallas_call(
        body,
        name=name,
        grid=(R // tr,),
        in_specs=[row, row, row] + [part_spec(lo, hi) for lo, hi in spans],
        out_specs=[row] * 4,
        out_shape=[jax.ShapeDtypeStruct((R, C), F32)] * 4,
        compiler_params=_params("parallel"),
    )(w, m, v, *plist)


def _adamw_ada(w, m, v, cact_t, dmod_cols, *, name):
    R, C = w.shape
    B = cact_t.shape[1]
    tr = _tile(R, 256, 8)
    row = pl.BlockSpec((tr, C), lambda i: (i, 0))

    def body(w_ref, m_ref, v_ref, c_ref, d_ref, g_ref, dl_ref, m2_ref, v2_ref):
        g = c_ref[:, 0:1] * d_ref[0:1, :]
        for b in range(1, B):
            g = g + c_ref[:, b:b + 1] * d_ref[b:b + 1, :]
        g_ref[...] = g
        dl_ref[...], m2_ref[...], v2_ref[...] = _adam_math(w_ref[...], g, m_ref[...], v_ref[...])

    return pl.pallas_call(
        body,
        name=name,
        grid=(R // tr,),
        in_specs=[row, row, row, pl.BlockSpec((tr, B), lambda i: (i, 0)), pl.BlockSpec((B, C), lambda i: (0, 0))],
        out_specs=[row] * 4,
        out_shape=[jax.ShapeDtypeStruct((R, C), F32)] * 4,
        compiler_params=_params("parallel"),
    )(w, m, v, cact_t, dmod_cols)


def _z_layout(D, q_rank, kv_rank):
    kv = SWA_KVH * SWA_HD
    orig = {}
    o = 0
    for nm, w in (("cq", q_rank), ("ckv", kv_rank), ("kr", MLA_ROPE), ("qs", D), ("ks", kv), ("vs", kv),
                  ("ga", D), ("gb", D)):
        orig[nm] = (o, w)
        o += w
    blockw = {"cq": q_rank, "ckv": kv_rank, "kr": LANES, "qs": D, "ks": kv, "vs": kv, "ga": D, "gb": D}
    best = None
    for perm in itertools.permutations(("cq", "ckv", "ks", "vs", "kr")):
        off, new = 0, {}
        for nm in ("ga", "gb", "qs") + perm:
            off = _round_up(off, blockw[nm])
            new[nm] = off
            off += blockw[nm]
        if best is None or off < best[0]:
            best = (off, new)
    total = _round_up(best[0], 1024 if best[0] > 4096 else 512)
    return orig, best[1], blockw, total, o


def _permute_w_in(w, lay):
    orig, new, blockw, total, _ = lay
    parts, at = [], 0
    for nm in sorted(new, key=new.get):
        if new[nm] > at:
            parts.append(jnp.zeros((w.shape[0], new[nm] - at), w.dtype))
        o, wd = orig[nm]
        parts.append(w[:, o:o + wd])
        if blockw[nm] > wd:
            parts.append(jnp.zeros((w.shape[0], blockw[nm] - wd), w.dtype))
        at = new[nm] + blockw[nm]
    if total > at:
        parts.append(jnp.zeros((w.shape[0], total - at), w.dtype))
    return jnp.concatenate(parts, axis=1)


def _unpermute_w_in(wp, lay):
    orig, new, _, _, _ = lay
    return jnp.concatenate([wp[:, new[nm]:new[nm] + orig[nm][1]] for nm in sorted(orig, key=lambda n: orig[n][0])],
                           axis=1)


def _assemble_dz(parts, lay, S):
    _, new, blockw, total, _ = lay
    cols, at = [], 0
    for nm in sorted(new, key=new.get):
        if new[nm] > at:
            cols.append(jnp.zeros((S, new[nm] - at), BF16))
        cols.append(parts[nm])
        at = new[nm] + blockw[nm]
    if total > at:
        cols.append(jnp.zeros((S, total - at), BF16))
    return jnp.concatenate(cols, axis=1)


def _unshard_cols(g):
    return jnp.transpose(g, (1, 0, 2)).reshape(g.shape[1], N_DEV * g.shape[2])


def _shard_cols(w):
    K, N = w.shape
    return jnp.transpose(w.reshape(K, N_DEV, N // N_DEV), (1, 0, 2))


def _pack(vecs, rows):
    flat = jnp.concatenate([v.reshape(-1) for v in vecs])
    return jnp.pad(flat, (0, rows * LANES - flat.shape[0])).reshape(rows, LANES)


def kernel(x, c, w_ada, b_ada, g_pre_mix, g_post_mix, w_in, g_q_lat, w_uq, g_kv_lat, w_ukv, rel_bias, sinks, w_o, g_pre_ffn, g_post_ffn, w_up, conv_w, conv_b, w_down, loss_target, m_w_ada, m_b_ada, m_g_pre_mix, m_g_post_mix, m_w_in, m_g_q_lat, m_w_uq, m_g_kv_lat, m_w_ukv, m_rel_bias, m_sinks, m_w_o, m_g_pre_ffn, m_g_post_ffn, m_w_up, m_conv_w, m_conv_b, m_w_down, v_w_ada, v_b_ada, v_g_pre_mix, v_g_post_mix, v_w_in, v_g_q_lat, v_w_uq, v_g_kv_lat, v_w_ukv, v_rel_bias, v_sinks, v_w_o, v_g_pre_ffn, v_g_post_ffn, v_w_up, v_conv_w, v_conv_b, v_w_down):
    S, D = x.shape[1], x.shape[2]
    Q_RANK, KV_RANK = g_q_lat.shape[1], g_kv_lat.shape[1]
    H_MLA = D // MLA_V
    H_SWA = D // SWA_HD
    G_SWA = H_SWA // SWA_KVH
    F2 = w_up.shape[2] * N_DEV
    F = F2 // 2
    ada_n = w_ada.shape[2]
    me = 4 * lax.axis_index("x") + 2 * lax.axis_index("y") + lax.axis_index("c")
    lay = _z_layout(D, Q_RANK, KV_RANK)
    _, zoff, _, NZ, in_cols = lay
    assert in_cols == w_in.shape[2] * N_DEV

    x2, tgt = x[0], loss_target[0]

    cw_n = conv_w.shape[2]
    small = jnp.concatenate([jnp.pad(c, ((0, 7), (0, 0))), jnp.pad(conv_w[0], ((0, 8 - CONV_WIDTH), (0, 0)))], axis=1)
    small_all = _all_gather(small, name="ag_cond", in_vmem=True)
    c_all = small_all[:, 0, :D]
    cw_full = _unshard_cols(small_all[:, :CONV_WIDTH, D:])
    b_cols = lax.dynamic_slice_in_dim(b_ada, me * ada_n, ada_n, axis=1)
    c_act, mod_cols = _ada_fwd(c_all, w_ada[0], b_cols, name="ada_fwd")
    mod_all = _all_gather(mod_cols, name="ag_mod", in_vmem=True)
    mod_me = lax.dynamic_index_in_dim(mod_all, me, axis=1, keepdims=False).reshape(1, N_DEV * ada_n)
    sh1, sc1, gt1, sh2, sc2, gt2 = [mod_me[:, k * D:(k + 1) * D] for k in range(6)]

    w_in_p = _permute_w_in(_unshard_cols(_all_gather(w_in[0].astype(BF16), name="ag_w_in", in_vmem=False)), lay)

    h1 = _prenorm(x2, g_pre_mix, sc1, sh1, name="prenorm_mix")
    z, (uq_g, ukv_g, o_g) = _matmul(h1, w_in_p, mode="nn", out_dtype=BF16, name="mm_in",
                                    comm=[("gather", w_uq[0].astype(BF16)), ("gather", w_ukv[0].astype(BF16)),
                                          ("gather", w_o[0].astype(BF16))])
    w_uq_p = jnp.pad(_unshard_cols(uq_g).reshape(Q_RANK, H_MLA, MLA_QK), ((0, 0), (0, 0), (0, MLA_QK_PAD - MLA_QK))
                     ).reshape(Q_RANK, H_MLA * MLA_QK_PAD)
    w_ukv_f = _unshard_cols(ukv_g)
    w_o_f = o_g.reshape(D, D)
    cqn = _prenorm(z, g_q_lat, None, None, name="norm_cq", off=zoff["cq"], width=Q_RANK)
    ckvn = _prenorm(z, g_kv_lat, None, None, name="norm_ckv", off=zoff["ckv"], width=KV_RANK)
    q_raw = _matmul(cqn, w_uq_p, mode="nn", out_dtype=BF16, name="mm_uq")
    kv = _matmul(ckvn, w_ukv_f, mode="nn", out_dtype=BF16, name="mm_ukv")
    tab_k = _rope_tables(S, LANES, 0)
    krr = _rope(z, tab_k, heads=1, width=LANES, transpose=False, name="rope_k", off=zoff["kr"])
    (o_a, lse, Qr), (up_g,) = _flash_fwd(q_raw, kv, krr, tab_k, heads=H_MLA, name="mla_fwd",
                                        comm=[("gather", w_up[0].astype(BF16))])
    w_up_f = _unshard_cols(up_g)

    bucket, valid = _t5_bucket_table()
    onehot = (jnp.asarray(bucket).reshape(-1, 1) == jnp.arange(LANES)[None, :]).astype(F32)
    rb_pad = jnp.pad(rel_bias, ((0, LANES - REL_BUCKETS), (0, LANES - H_SWA)))
    bias_t = _matmul(onehot, rb_pad, mode="nn", out_dtype=F32, name="bias_table", tm=2048, precision=HIGHEST)
    bias_full = jnp.transpose(bias_t[:, :H_SWA].reshape(BLOCK, 2 * BLOCK, H_SWA), (2, 0, 1))
    bias_full = jnp.where(jnp.asarray(valid)[None], bias_full, NEG)
    bias_full = jnp.transpose(bias_full.reshape(SWA_KVH, G_SWA, BLOCK, 2 * BLOCK), (0, 3, 1, 2)
                              ).reshape(SWA_KVH, 2 * BLOCK, G_SWA * BLOCK)
    sink_rows = jnp.broadcast_to(sinks.reshape(SWA_KVH, G_SWA, 1), (SWA_KVH, G_SWA, BLOCK)
                                 ).reshape(SWA_KVH, 1, G_SWA * BLOCK)
    kvw = SWA_KVH * SWA_HD

    def heads_first(t, n):
        return jnp.transpose(t.reshape(S, n, SWA_HD), (1, 0, 2))

    def heads_last(t):
        return jnp.transpose(t, (1, 0, 2)).reshape(S, t.shape[0] * SWA_HD)

    def queries_first(t):
        t = t.reshape(SWA_KVH, S // BLOCK, SWA_HD, G_SWA, BLOCK)
        return jnp.transpose(t, (1, 4, 0, 3, 2)).reshape(S, H_SWA * SWA_HD)

    qs_h = heads_first(z[:, zoff["qs"]:zoff["qs"] + D], H_SWA)
    ks_h = heads_first(z[:, zoff["ks"]:zoff["ks"] + kvw], SWA_KVH)
    vs_h = heads_first(z[:, zoff["vs"]:zoff["vs"] + kvw], SWA_KVH)
    o_b_h, _ = _swa_fwd(qs_h, ks_h, vs_h, bias_full, sink_rows, name="swa_fwd")
    o_b = queries_first(o_b_h)

    mixin = _gate_mix(z, o_a, o_b, D=D, off_a=zoff["ga"], off_b=zoff["gb"], name="gate_mix")
    mix = _matmul(mixin, w_o_f, mode="nn", out_dtype=F32, name="mm_o")
    x1 = _postnorm_res(x2, mix, gt1, g_post_mix, name="postnorm_mix")

    h2 = _prenorm(x1, g_pre_ffn, sc2, sh2, name="prenorm_ffn")
    up, (down_g,) = _matmul(h2, w_up_f, mode="nn", out_dtype=BF16, name="mm_up",
                            comm=[("gather", w_down[0].astype(BF16))])
    w_down_f = down_g.reshape(F, D)
    act = _conv_gate(up, cw_full, conv_b, name="conv_gate")
    y = _matmul(act, w_down_f, mode="nn", out_dtype=F32, name="mm_down")
    loss_part, dout, dy, dgt2, dg_post_ffn = _final_loss(x1, y, tgt, gt2, g_post_ffn, name="final_loss")
    loss = lax.psum(loss_part[0, 0], ("x", "y", "c"))

    dw_down = _matmul(act, dy, mode="tn", out_dtype=BF16, name="mm_down_dw")
    dact = _matmul(dy, w_down_f, mode="nt", out_dtype=BF16, name="mm_down_dx")
    (dup, dcw, dcb), (got_down,) = _conv_gate_bwd(up, dact, cw_full, conv_b, name="conv_gate_bwd",
                                                  comm=[("scatter", dw_down.reshape(N_DEV, F // N_DEV, D))])
    dcw = jnp.transpose(dcw, (1, 0, 2)).reshape(CONV_WIDTH, F2)
    dcb = dcb.reshape(1, F2)
    dw_up = _matmul(h2, dup, mode="tn", out_dtype=BF16, name="mm_up_dw", shard_out=True, halves=True)
    dh2 = _matmul(dup, w_up_f, mode="nt", out_dtype=F32, name="mm_up_dx", halves=True)
    dx1, dg_pre_ffn, dsc2, dsh2 = _prenorm_bwd(x1, dh2, dout, g_pre_ffn, sc2, name="prenorm_ffn_bwd", out_dtype=F32)

    dmix, dgt1, dg_post_mix = _postnorm_bwd(dx1, mix, gt1, g_post_mix, name="postnorm_mix_bwd")
    dw_o = _matmul(mixin, dmix, mode="tn", out_dtype=BF16, name="mm_o_dw")
    dmixin = _matmul(dmix, w_o_f, mode="nt", out_dtype=BF16, name="mm_o_dx")
    dga, dgb, do_a, do_b = _gate_mix_bwd(dmixin, z, o_a, o_b, D=D, off_a=zoff["ga"], off_b=zoff["gb"],
                                         name="gate_mix_bwd")
    (dq_raw, dkv, dkr_parts), (got_up,) = _flash_bwd(Qr, kv, krr, do_a, o_a, lse, tab_k, heads=H_MLA, name="mla_bwd",
                                                     comm=[("scatter", dw_up)])
    dkr = _shared_rope_grad(dkr_parts, tab_k, name="rope_k_bwd")
    dw_uq_p = _matmul(cqn, dq_raw, mode="tn", out_dtype=BF16, name="mm_uq_dw")
    dcqn = _matmul(dq_raw, w_uq_p, mode="nt", out_dtype=F32, name="mm_uq_dx")
    dw_ukv = _matmul(ckvn, dkv, mode="tn", out_dtype=BF16, name="mm_ukv_dw", shard_out=True)
    dckvn = _matmul(dkv, w_ukv_f, mode="nt", out_dtype=F32, name="mm_ukv_dx")
    dw_uq = dw_uq_p.reshape(Q_RANK, H_MLA, MLA_QK_PAD)[:, :, :MLA_QK].reshape(Q_RANK, H_MLA * MLA_QK)

    dcw_parts = jnp.pad(_shard_cols(dcw), ((0, 0), (0, 16 - CONV_WIDTH), (0, 0)))
    (dqs_h, dks_h, dvs_h, dbias, dsink), (got_o, got_cw, got_uq, got_ukv) = _swa_bwd(
        qs_h, ks_h, vs_h, bias_full, sink_rows, heads_first(do_b, H_SWA), name="swa_bwd",
        comm=[("scatter", dw_o.reshape(N_DEV, D // N_DEV, D)), ("scatter", dcw_parts),
              ("scatter", _shard_cols(dw_uq)), ("scatter", dw_ukv)])
    dbias = jnp.transpose(dbias.reshape(SWA_KVH, 2 * BLOCK, G_SWA, BLOCK), (0, 2, 3, 1))
    drel_t = _matmul(dbias.reshape(H_SWA, BLOCK * 2 * BLOCK), onehot, mode="nn", out_dtype=F32, name="bias_grad",
                     tk=4096, precision=HIGHEST)
    d_rel_bias = jnp.transpose(drel_t[:, :REL_BUCKETS])
    d_sinks = jnp.sum(dsink.reshape(SWA_KVH, G_SWA, BLOCK), axis=-1).reshape(1, H_SWA)

    dcq, dg_q = _prenorm_bwd(z, dcqn, None, g_q_lat, None, name="norm_cq_bwd", out_dtype=BF16,
                             off=zoff["cq"], width=Q_RANK)
    dckv, dg_kv = _prenorm_bwd(z, dckvn, None, g_kv_lat, None, name="norm_ckv_bwd", out_dtype=BF16,
                               off=zoff["ckv"], width=KV_RANK)
    dz = _assemble_dz({"ga": dga, "gb": dgb, "qs": queries_first(dqs_h), "cq": dcq, "ckv": dckv,
                       "ks": heads_last(dks_h), "vs": heads_last(dvs_h), "kr": dkr}, lay, S)
    dw_in_a = _matmul(h1, dz, mode="tn", out_dtype=BF16, name="mm_in_dw_a", m_range=(0, D // 2))
    dw_in_b, (got_in_a,) = _matmul(h1, dz, mode="tn", out_dtype=BF16, name="mm_in_dw_b", m_range=(D // 2, D // 2),
                                   comm=[("scatter", _shard_cols(_unpermute_w_in(dw_in_a, lay)))])
    dh1, (got_in_b,) = _matmul(dz, w_in_p, mode="nt", out_dtype=F32, name="mm_in_dx",
                               comm=[("scatter", _shard_cols(_unpermute_w_in(dw_in_b, lay)))])
    grad_x, dg_pre_mix, dsc1, dsh1 = _prenorm_bwd(x2, dh1, dx1, g_pre_mix, sc1, name="prenorm_mix_bwd",
                                                  out_dtype=F32)
    dmod = jnp.concatenate([dsh1, dsc1, dgt1, dsh2, dsc2, dgt2], axis=1)

    small_names = ["b_ada", "g_pre_mix", "g_post_mix", "g_q_lat", "g_kv_lat", "rel_bias", "sinks", "g_pre_ffn",
                   "g_post_ffn", "conv_b"]
    small_w = [b_ada, g_pre_mix, g_post_mix, g_q_lat, g_kv_lat, rel_bias, sinks, g_pre_ffn, g_post_ffn, conv_b]
    small_m = [m_b_ada, m_g_pre_mix, m_g_post_mix, m_g_q_lat, m_g_kv_lat, m_rel_bias, m_sinks, m_g_pre_ffn,
               m_g_post_ffn, m_conv_b]
    small_v = [v_b_ada, v_g_pre_mix, v_g_post_mix, v_g_q_lat, v_g_kv_lat, v_rel_bias, v_sinks, v_g_pre_ffn,
               v_g_post_ffn, v_conv_b]
    small_g = [dmod, dg_pre_mix, dg_post_mix, dg_q, dg_kv, d_rel_bias, d_sinks, dg_pre_ffn, dg_post_ffn, dcb]
    n_small = sum(int(np.prod(w.shape)) for w in small_w)
    rows = _round_up(-(-n_small // LANES), 16)
    parts_small = _all_gather(_pack(small_g, rows), name="ag_small_grads", in_vmem=True)
    sg, sd, sm, sv = _adamw(_pack(small_w, rows), _pack(small_m, rows), _pack(small_v, rows), parts_small,
                            name="adamw_small")

    def unpack(packed):
        flat, out, at = packed.reshape(-1), {}, 0
        for nm, w in zip(small_names, small_w):
            n = int(np.prod(w.shape))
            out[nm] = flat[at:at + n].reshape(w.shape)
            at += n
        return out

    small_out = [unpack(t) for t in (sg, sd, sm, sv)]

    dmod_all = parts_small.reshape(N_DEV, rows * LANES)[:, :6 * D]
    dmod_cols = lax.dynamic_slice_in_dim(dmod_all, me * ada_n, ada_n, axis=1)
    ada_out = _adamw_ada(w_ada[0], m_w_ada[0], v_w_ada[0], jnp.transpose(c_act), dmod_cols, name="adamw_w_ada")

    def owner_update(got, w, m, v, name):
        shp = w.shape
        w2, m2, v2 = (t.reshape(shp[-2], shp[-1]) for t in (w, m, v))
        return [t.reshape(shp) for t in _adamw(w2, m2, v2, got, name="adamw_" + name)]

    def pad_rows(t):
        return jnp.pad(t[0], ((0, 16 - CONV_WIDTH), (0, 0)))

    big = {
        "w_in": owner_update([got_in_a, got_in_b], w_in, m_w_in, v_w_in, "w_in"),
        "w_uq": owner_update(got_uq, w_uq, m_w_uq, v_w_uq, "w_uq"),
        "w_ukv": owner_update(got_ukv, w_ukv, m_w_ukv, v_w_ukv, "w_ukv"),
        "w_o": owner_update(got_o, w_o, m_w_o, v_w_o, "w_o"),
        "w_up": owner_update(got_up, w_up, m_w_up, v_w_up, "w_up"),
        "w_down": owner_update(got_down, w_down, m_w_down, v_w_down, "w_down"),
    }
    cw_upd = _adamw(pad_rows(conv_w), pad_rows(m_conv_w), pad_rows(v_conv_w), got_cw, name="adamw_conv_w")
    big["conv_w"] = [t[:CONV_WIDTH].reshape(conv_w.shape) for t in cw_upd]
    big["w_ada"] = [t.reshape(w_ada.shape) for t in ada_out]

    order = ["w_ada", "b_ada", "g_pre_mix", "g_post_mix", "w_in", "g_q_lat", "w_uq", "g_kv_lat", "w_ukv", "rel_bias",
             "sinks", "w_o", "g_pre_ffn", "g_post_ffn", "w_up", "conv_w", "conv_b", "w_down"]
    outs = [loss, grad_x.reshape(x.shape)]
    for kind in range(4):
        for nm in order:
            outs.append(big[nm][kind] if nm in big else small_out[kind][nm])
    return tuple(outs)
```

```python
import functools
import itertools
import math

import numpy as np

import jax
import jax.numpy as jnp
from jax import lax
from jax.experimental import pallas as pl
from jax.experimental.pallas import tpu as pltpu

F32 = jnp.float32
BF16 = jnp.bfloat16

N_DEV = 8
MLA_NOPE = 128
MLA_ROPE = 64
MLA_V = 128
MLA_QK = MLA_NOPE + MLA_ROPE
MLA_QK_PAD = 256
ROPE_HALF = MLA_ROPE // 2
ROPE_THETA = 10000.0
SWA_HD = 64
SWA_KVH = 4
WINDOW = 128
BLOCK = 128
REL_BUCKETS = 32
REL_MAX_DIST = 128
CONV_WIDTH = 3
EPS = 1e-6
NEG = -1e30
ADAM_LR = 0.001
ADAM_B1 = 0.9
ADAM_B2 = 0.999
ADAM_EPS = 1e-08
ADAM_WD = 0.01
ADAM_STEP = 10
LANES = 128
HALO = 16
MESH = pl.DeviceIdType.MESH
HIGHEST = lax.Precision.HIGHEST

NN = (((1,), (0,)), ((), ()))
NT = (((1,), (1,)), ((), ()))
TN = (((0,), (0,)), ((), ()))


def _tile(n, pref, align=LANES):
    if n <= pref:
        return n
    t = (pref // align) * align
    while t >= align:
        if n % t == 0:
            return t
        t -= align
    return n


def _round_up(n, m):
    return (n + m - 1) // m * m


def _params(*sem):
    return pltpu.CompilerParams(dimension_semantics=sem)


def _sigmoid(x):
    return 1.0 / (1.0 + jnp.exp(-x))


def _my_place():
    return lax.axis_index("x"), lax.axis_index("y"), lax.axis_index("c")


def _all_gather(x, *, name, in_vmem):
    space = pltpu.VMEM if in_vmem else pl.ANY

    def body(x_ref, out_ref, send_sems, recv_sems, local_sem):
        x_, y_, c_ = _my_place()
        me, sibling = (x_, y_, c_), (x_, y_, 1 - c_)
        chips = [(1 - x_, y_), (x_, 1 - y_), (1 - x_, 1 - y_)]

        def slot(px, py, pc):
            return out_ref.at[4 * px + 2 * py + pc]

        def copy(k, block, to, src=None):
            return pltpu.make_async_remote_copy(
                src_ref=slot(*block) if src is None else src,
                dst_ref=slot(*block),
                send_sem=send_sems.at[k],
                recv_sem=recv_sems.at[k],
                device_id=to,
                device_id_type=MESH,
            )

        mine = pltpu.make_async_copy(x_ref, slot(*me), local_sem)
        mine.start()
        first = [copy(0, me, sibling, src=x_ref)]
        first += [copy(1 + j, me, (*chip, c_), src=x_ref) for j, chip in enumerate(chips)]
        for cp in first:
            cp.start()
        passed = [copy(4 + j, (*chip, c_), sibling) for j, chip in enumerate(chips)]
        for j, chip in enumerate(chips):
            copy(1 + j, (*chip, c_), me).wait_recv()
            passed[j].start()
        copy(0, sibling, me).wait_recv()
        for j, chip in enumerate(chips):
            copy(4 + j, (*chip, 1 - c_), me).wait_recv()
        for cp in first + passed:
            cp.wait_send()
        mine.wait()

    return pl.pallas_call(
        body,
        name=name,
        out_shape=jax.ShapeDtypeStruct((N_DEV,) + x.shape, x.dtype),
        in_specs=[pl.BlockSpec(memory_space=space)],
        out_specs=pl.BlockSpec(memory_space=space),
        scratch_shapes=[
            pltpu.SemaphoreType.DMA((7,)),
            pltpu.SemaphoreType.DMA((7,)),
            pltpu.SemaphoreType.DMA,
        ],
    )(x)


class _Exchange:
    def __init__(self, kind, x_ref, out_ref, send_sems, recv_sems, local_sems, t):
        x_, y_, c_ = _my_place()
        me = 4 * x_ + 2 * y_ + c_

        def pair(k, src, dst, to):
            return pltpu.make_async_remote_copy(src_ref=src, dst_ref=dst, send_sem=send_sems.at[7 * t + k],
                                                recv_sem=recv_sems.at[7 * t + k], device_id=to, device_id_type=MESH)

        none = lambda: []
        if kind == "scatter":
            peers = [(x_ ^ ((r >> 2) & 1), y_ ^ ((r >> 1) & 1), c_ ^ (r & 1)) for r in range(1, N_DEV)]
            self.at_start = lambda: [pair(k, x_ref.at[4 * px + 2 * py + pc], out_ref.at[me], (px, py, pc))
                                     for k, (px, py, pc) in enumerate(peers)]
            self.relay_after, self.at_relay = none, none
            self.arrivals = self.at_start
            self.own = lambda: pltpu.make_async_copy(x_ref.at[me], out_ref.at[me], local_sems.at[t])
        else:
            sibling = (x_, y_, 1 - c_)
            chips = list(enumerate([(1 - x_, y_), (x_, 1 - y_), (1 - x_, 1 - y_)]))

            def slot(px, py, pc):
                return out_ref.at[4 * px + 2 * py + pc]

            mine = slot(x_, y_, c_)
            self.at_start = lambda: ([pair(0, x_ref, mine, sibling)]
                                     + [pair(1 + j, x_ref, mine, (*chip, c_)) for j, chip in chips])
            self.relay_after = lambda: [pair(1 + j, slot(*chip, c_), slot(*chip, c_), (*chip, c_)) for j, chip in chips]
            self.at_relay = lambda: [pair(4 + j, slot(*chip, c_), slot(*chip, c_), sibling) for j, chip in chips]
            self.arrivals = lambda: ([pair(0, slot(*sibling), slot(*sibling), sibling)]
                                     + [pair(4 + j, slot(*chip, 1 - c_), slot(*chip, 1 - c_), sibling)
                                        for j, chip in chips])
            self.own = lambda: pltpu.make_async_copy(x_ref, mine, local_sems.at[t])

    def start(self):
        self.own().start()
        for cp in self.at_start():
            cp.start()

    def relay(self):
        for landed, onward in zip(self.relay_after(), self.at_relay()):
            landed.wait_recv()
            onward.start()

    def finish(self):
        for cp in self.arrivals():
            cp.wait_recv()
        for cp in self.at_start() + self.at_relay():
            cp.wait_send()
        self.own().wait()


RELAY_AT = 0.7


def _call(body, *, name, grid, in_specs, out_specs, out_shape, args, scratch_shapes=(), sem=(), comm=(), prefetch=()):
    n_pf = len(prefetch)

    def launch(fn, ins, outs, shapes, scratch, semantics, operands):
        spec = pltpu.PrefetchScalarGridSpec(num_scalar_prefetch=n_pf, grid=grid, in_specs=ins, out_specs=outs,
                                            scratch_shapes=scratch)
        return pl.pallas_call(fn, name=name, grid_spec=spec, out_shape=shapes,
                              compiler_params=_params(*semantics))(*prefetch, *operands)

    if not comm:
        return list(launch(body, list(in_specs), list(out_specs), list(out_shape), list(scratch_shapes), sem, args)), []
    n_in, n_out, n_c, n_s = len(in_specs), len(out_specs), len(comm), len(scratch_shapes)
    kinds = [kind for kind, _ in comm]
    hbm = pl.BlockSpec(memory_space=pl.ANY)

    def wrapped(*refs):
        tables, refs = refs[:n_pf], refs[n_pf:]
        ins, cin = refs[:n_in], refs[n_in:n_in + n_c]
        at = n_in + n_c
        outs, cout = refs[at:at + n_out], refs[at + n_out:at + n_out + n_c]
        scr = refs[at + n_out + n_c:at + n_out + n_c + n_s]
        send, recv, local = refs[-3:]
        step = 0
        for a, g in enumerate(grid):
            step = step * g + pl.program_id(a)
        n_steps = int(np.prod(grid))

        def exchanges():
            return [_Exchange(kinds[t], cin[t], cout[t], send, recv, local, t) for t in range(n_c)]

        @pl.when(step == 0)
        def _():
            for ex in exchanges():
                ex.start()

        body(*tables, *ins, *outs, *scr)

        @pl.when(step == min(int(RELAY_AT * n_steps), n_steps - 1))
        def _():
            for ex in exchanges():
                ex.relay()

        @pl.when(step == n_steps - 1)
        def _():
            for ex in exchanges():
                ex.finish()

    c_shapes = [jax.ShapeDtypeStruct(((N_DEV,) + a.shape) if kind == "gather" else a.shape, a.dtype)
                for kind, a in comm]
    sems = [pltpu.SemaphoreType.DMA((7 * n_c,)), pltpu.SemaphoreType.DMA((7 * n_c,)), pltpu.SemaphoreType.DMA((n_c,))]
    res = launch(wrapped, list(in_specs) + [hbm] * n_c, list(out_specs) + [hbm] * n_c, list(out_shape) + c_shapes,
                 list(scratch_shapes) + sems, ["arbitrary"] * len(grid), (*args, *[a for _, a in comm]))
    return list(res[:n_out]), list(res[n_out:])


def _matmul(a, b, *, mode, out_dtype, name, tm=1024, tn=1024, tk=2816, precision=None, comm=(), shard_out=False,
            halves=False, m_range=None):
    if mode == "nn":
        (M, K), (K2, N) = a.shape, b.shape
    elif mode == "nt":
        (M, K), (N, K2) = (a.shape[1], 2 * a.shape[2]) if halves else a.shape, b.shape
    else:
        (K, M), (K2, N) = a.shape, (b.shape[1], 2 * b.shape[2]) if halves else b.shape
    assert K == K2, (a.shape, b.shape, mode)
    m_off = 0
    if m_range is not None:
        m_off, M = m_range
    tm = _tile(M, tm, LANES if mode == "tn" else 16)
    tk = _tile(K // 2 if halves and mode == "nt" else K, tk)
    tn = _tile(N // N_DEV, max(tn, 1408)) if shard_out else _tile(N // 2 if halves and mode == "tn" else N, tn)
    nk = K // tk
    m_off //= tm
    if mode == "tn":
        a_spec = pl.BlockSpec((tk, tm), lambda i, j, k: (k, i + m_off))
    elif halves:
        a_spec = pl.BlockSpec((None, tm, tk), lambda i, j, k: (k // (nk // 2), i, k % (nk // 2)))
    else:
        a_spec = pl.BlockSpec((tm, tk), lambda i, j, k: (i, k))
    if mode == "nt":
        b_spec = pl.BlockSpec((tn, tk), lambda i, j, k: (j, k))
    elif halves:
        nj = N // tn
        b_spec = pl.BlockSpec((None, tk, tn), lambda i, j, k: (j // (nj // 2), k, j % (nj // 2)))
    else:
        b_spec = pl.BlockSpec((tk, tn), lambda i, j, k: (k, j))
    dn = {"nn": NN, "nt": NT, "tn": TN}[mode]
    if shard_out:
        per = N // N_DEV // tn
        o_spec = pl.BlockSpec((None, tm, tn), lambda i, j, k: (j // per, i, j % per))
        o_shape = jax.ShapeDtypeStruct((N_DEV, M, N // N_DEV), out_dtype)
    else:
        o_spec = pl.BlockSpec((tm, tn), lambda i, j, k: (i, j))
        o_shape = jax.ShapeDtypeStruct((M, N), out_dtype)

    def product(a_ref, b_ref):
        return lax.dot_general(a_ref[...], b_ref[...], dn, preferred_element_type=F32, precision=precision)

    def body_one(a_ref, b_ref, o_ref):
        o_ref[...] = product(a_ref, b_ref).astype(o_ref.dtype)

    def body_acc(a_ref, b_ref, o_ref, acc_ref):
        k = pl.program_id(2)

        @pl.when(k == 0)
        def _():
            acc_ref[...] = product(a_ref, b_ref)

        @pl.when(k > 0)
        def _():
            acc_ref[...] += product(a_ref, b_ref)

        @pl.when(k == nk - 1)
        def _():
            o_ref[...] = acc_ref[...].astype(o_ref.dtype)

    outs, moved = _call(
        body_one if nk == 1 else body_acc,
        name=name,
        grid=(M // tm, N // tn, nk),
        in_specs=[a_spec, b_spec],
        out_specs=[o_spec],
        out_shape=[o_shape],
        scratch_shapes=[] if nk == 1 else [pltpu.VMEM((tm, tn), F32)],
        sem=("parallel", "parallel", "arbitrary"),
        args=(a, b),
        comm=comm,
    )
    return (outs[0], moved) if comm else outs[0]


def _rstd(xf):
    return lax.rsqrt(jnp.mean(xf * xf, axis=-1, keepdims=True) + EPS)


def _col_view(width, off):
    assert off % width == 0
    return off // width


def _prenorm(x, g, sc, sh, *, name, off=0, width=None):
    S = x.shape[0]
    W = x.shape[1] if width is None else width
    cb = _col_view(W, off)
    tr = _tile(S, 512, 16)
    mod = sc is not None
    vec = pl.BlockSpec((1, W), lambda i: (0, 0))

    def body(*refs):
        if mod:
            x_ref, g_ref, sc_ref, sh_ref, o_ref = refs
        else:
            x_ref, g_ref, o_ref = refs
        xf = x_ref[...].astype(F32)
        y = xf * _rstd(xf) * g_ref[...]
        if mod:
            y = y * (1.0 + sc_ref[...]) + sh_ref[...]
        o_ref[...] = y.astype(o_ref.dtype)

    args = (x, g, sc, sh) if mod else (x, g)
    return pl.pallas_call(
        body,
        name=name,
        grid=(S // tr,),
        in_specs=[pl.BlockSpec((tr, W), lambda i: (i, cb))] + [vec] * (len(args) - 1),
        out_specs=pl.BlockSpec((tr, W), lambda i: (i, 0)),
        out_shape=jax.ShapeDtypeStruct((S, W), BF16),
        compiler_params=_params("parallel"),
    )(*args)


def _prenorm_bwd(x, dh, dres, g, sc, *, name, out_dtype, off=0, width=None):
    S = x.shape[0]
    W = x.shape[1] if width is None else width
    cb = _col_view(W, off)
    tr = _tile(S, 256, 16)
    mod = sc is not None
    res = dres is not None
    vec = pl.BlockSpec((1, W), lambda i: (0, 0))
    row = pl.BlockSpec((tr, W), lambda i: (i, 0))

    def body(*refs):
        it = iter(refs)
        x_ref, dh_ref = next(it), next(it)
        dres_ref = next(it) if res else None
        g_ref = next(it)
        sc_ref = next(it) if mod else None
        dx_ref, dg_ref = next(it), next(it)
        dsc_ref, dsh_ref = (next(it), next(it)) if mod else (None, None)
        i = pl.program_id(0)

        @pl.when(i == 0)
        def _():
            dg_ref[...] = jnp.zeros_like(dg_ref)
            if mod:
                dsc_ref[...] = jnp.zeros_like(dsc_ref)
                dsh_ref[...] = jnp.zeros_like(dsh_ref)

        xf = x_ref[...].astype(F32)
        r = _rstd(xf)
        xn = xf * r
        dhf = dh_ref[...].astype(F32)
        gv = g_ref[...]
        if mod:
            one_sc = 1.0 + sc_ref[...]
            dsh_ref[...] += jnp.sum(dhf, axis=0, keepdims=True)
            dsc_ref[...] += jnp.sum(dhf * (xn * gv), axis=0, keepdims=True)
            dg_ref[...] += jnp.sum(dhf * xn * one_sc, axis=0, keepdims=True)
            dxn = dhf * (gv * one_sc)
        else:
            dg_ref[...] += jnp.sum(dhf * xn, axis=0, keepdims=True)
            dxn = dhf * gv
        dx = r * (dxn - xn * jnp.mean(dxn * xn, axis=-1, keepdims=True))
        if res:
            dx = dx + dres_ref[...]
        dx_ref[...] = dx.astype(dx_ref.dtype)

    args = [x, dh] + ([dres] if res else []) + [g] + ([sc] if mod else [])
    in_specs = [pl.BlockSpec((tr, W), lambda i: (i, cb)), row] + ([row] if res else []) + [vec] + ([vec] if mod else [])
    n_vec = 3 if mod else 1
    outs = pl.pallas_call(
        body,
        name=name,
        grid=(S // tr,),
        in_specs=in_specs,
        out_specs=[row] + [vec] * n_vec,
        out_shape=[jax.ShapeDtypeStruct((S, W), out_dtype)] + [jax.ShapeDtypeStruct((1, W), F32)] * n_vec,
        compiler_params=_params("arbitrary"),
    )(*args)
    return outs


def _postnorm_res(x, y, gt, g, *, name):
    S, D = x.shape
    tr = _tile(S, 512, 8)
    row = pl.BlockSpec((tr, D), lambda i: (i, 0))
    vec = pl.BlockSpec((1, D), lambda i: (0, 0))

    def body(x_ref, y_ref, gt_ref, g_ref, o_ref):
        yf = y_ref[...]
        o_ref[...] = x_ref[...] + gt_ref[...] * (yf * _rstd(yf) * g_ref[...])

    return pl.pallas_call(
        body,
        name=name,
        grid=(S // tr,),
        in_specs=[row, row, vec, vec],
        out_specs=row,
        out_shape=jax.ShapeDtypeStruct((S, D), F32),
        compiler_params=_params("parallel"),
    )(x, y, gt, g)


def _postnorm_bwd(dx1, y, gt, g, *, name):
    S, D = y.shape
    tr = _tile(S, 256, 16)
    row = pl.BlockSpec((tr, D), lambda i: (i, 0))
    vec = pl.BlockSpec((1, D), lambda i: (0, 0))

    def body(dx_ref, y_ref, gt_ref, g_ref, dy_ref, dgt_ref, dg_ref):
        @pl.when(pl.program_id(0) == 0)
        def _():
            dgt_ref[...] = jnp.zeros_like(dgt_ref)
            dg_ref[...] = jnp.zeros_like(dg_ref)

        yf = y_ref[...]
        r = _rstd(yf)
        yn = yf * r
        d = dx_ref[...]
        gtv, gv = gt_ref[...], g_ref[...]
        dgt_ref[...] += jnp.sum(d * (yn * gv), axis=0, keepdims=True)
        dg_ref[...] += jnp.sum(d * gtv * yn, axis=0, keepdims=True)
        dyn = d * (gtv * gv)
        dy_ref[...] = (r * (dyn - yn * jnp.mean(dyn * yn, axis=-1, keepdims=True))).astype(dy_ref.dtype)

    return pl.pallas_call(
        body,
        name=name,
        grid=(S // tr,),
        in_specs=[row, row, vec, vec],
        out_specs=[row, vec, vec],
        out_shape=[jax.ShapeDtypeStruct((S, D), BF16), jax.ShapeDtypeStruct((1, D), F32),
                   jax.ShapeDtypeStruct((1, D), F32)],
        compiler_params=_params("arbitrary"),
    )(dx1, y, gt, g)


def _final_loss(x1, y, target, gt, g, *, name):
    S, D = y.shape
    tr = _tile(S, 256, 16)
    row = pl.BlockSpec((tr, D), lambda i: (i, 0))
    vec = pl.BlockSpec((1, D), lambda i: (0, 0))
    one = pl.BlockSpec((1, LANES), lambda i: (0, 0))

    def body(x_ref, y_ref, t_ref, gt_ref, g_ref, loss_ref, dout_ref, dy_ref, dgt_ref, dg_ref):
        @pl.when(pl.program_id(0) == 0)
        def _():
            loss_ref[...] = jnp.zeros_like(loss_ref)
            dgt_ref[...] = jnp.zeros_like(dgt_ref)
            dg_ref[...] = jnp.zeros_like(dg_ref)

        yf = y_ref[...]
        r = _rstd(yf)
        yn = yf * r
        gtv, gv = gt_ref[...], g_ref[...]
        out = x_ref[...] + gtv * (yn * gv)
        diff = out - t_ref[...]
        per_tok = jnp.mean(diff * diff, axis=-1, keepdims=True)
        loss_ref[...] += 0.5 * jnp.sum(per_tok, axis=0, keepdims=True)
        d = diff / D
        dout_ref[...] = d
        dgt_ref[...] += jnp.sum(d * (yn * gv), axis=0, keepdims=True)
        dg_ref[...] += jnp.sum(d * gtv * yn, axis=0, keepdims=True)
        dyn = d * (gtv * gv)
        dy_ref[...] = (r * (dyn - yn * jnp.mean(dyn * yn, axis=-1, keepdims=True))).astype(dy_ref.dtype)

    return pl.pallas_call(
        body,
        name=name,
        grid=(S // tr,),
        in_specs=[row, row, row, vec, vec],
        out_specs=[one, row, row, vec, vec],
        out_shape=[jax.ShapeDtypeStruct((1, LANES), F32), jax.ShapeDtypeStruct((S, D), F32),
                   jax.ShapeDtypeStruct((S, D), BF16), jax.ShapeDtypeStruct((1, D), F32),
                   jax.ShapeDtypeStruct((1, D), F32)],
        compiler_params=_params("arbitrary"),
    )(x1, y, target, gt, g)


def _ada_fwd(c_all, w_local, b_cols, *, name):
    B, D = c_all.shape
    N = w_local.shape[1]
    tn = _tile(N, 512)

    def body(c_ref, w_ref, b_ref, ca_ref, mod_ref):
        cv = c_ref[...]
        ca = cv * _sigmoid(cv)
        ca_ref[...] = ca
        mod_ref[...] = jnp.dot(ca, w_ref[...], preferred_element_type=F32, precision=HIGHEST) + b_ref[...]

    return pl.pallas_call(
        body,
        name=name,
        grid=(N // tn,),
        in_specs=[pl.BlockSpec((B, D), lambda j: (0, 0)), pl.BlockSpec((D, tn), lambda j: (0, j)),
                  pl.BlockSpec((1, tn), lambda j: (0, j))],
        out_specs=[pl.BlockSpec((B, D), lambda j: (0, 0)), pl.BlockSpec((B, tn), lambda j: (0, j))],
        out_shape=[jax.ShapeDtypeStruct((B, D), F32), jax.ShapeDtypeStruct((B, N), F32)],
        compiler_params=_params("arbitrary"),
    )(c_all, w_local, b_cols)


def _rope_tables(S, width, lane_off):
    pos = jnp.arange(S, dtype=F32)
    inv = ROPE_THETA ** (-jnp.arange(0, MLA_ROPE, 2, dtype=F32) / MLA_ROPE)
    ang = pos[:, None] * inv[None, :]
    ang = jnp.concatenate([ang, ang], axis=-1)
    cos, sin = jnp.cos(ang), jnp.sin(ang)
    first = (jnp.arange(MLA_ROPE) < ROPE_HALF)[None, :]
    sa = jnp.where(first, -sin, 0.0)
    sb = jnp.where(first, 0.0, sin)

    def place(t, fill):
        return jnp.pad(t, ((0, 0), (lane_off, width - lane_off - MLA_ROPE)), constant_values=fill)

    return place(cos, 1.0), place(sa, 0.0), place(sb, 0.0)


def _rope_apply(x, cos, sa, sb, width, transpose):
    if transpose:
        return x * cos + pltpu.roll(x * sa, ROPE_HALF, 1) + pltpu.roll(x * sb, width - ROPE_HALF, 1)
    return x * cos + pltpu.roll(x, width - ROPE_HALF, 1) * sa + pltpu.roll(x, ROPE_HALF, 1) * sb


def _rope(x, tables, *, heads, width, transpose, name, off=0, scale=1.0):
    S = x.shape[0]
    cb = _col_view(width, off)
    tr = _tile(S, 512, 16)
    tab = pl.BlockSpec((tr, width), lambda i, h: (i, 0))

    def body(x_ref, c_ref, sa_ref, sb_ref, o_ref):
        y = _rope_apply(x_ref[...].astype(F32), c_ref[...], sa_ref[...], sb_ref[...], width, transpose)
        o_ref[...] = (y if scale == 1.0 else y * scale).astype(o_ref.dtype)

    return pl.pallas_call(
        body,
        name=name,
        grid=(S // tr, heads),
        in_specs=[pl.BlockSpec((tr, width), lambda i, h: (i, cb + h)), tab, tab, tab],
        out_specs=pl.BlockSpec((tr, width), lambda i, h: (i, h)),
        out_shape=jax.ShapeDtypeStruct((S, heads * width), BF16),
        compiler_params=_params("parallel", "parallel"),
    )(x, *tables)


def _shared_rope_grad(parts, tables, *, name):
    P, S, _ = parts.shape
    tr = _tile(S, 512, 16)
    tab = pl.BlockSpec((tr, LANES), lambda i: (i, 0))

    def body(p_ref, c_ref, sa_ref, sb_ref, o_ref):
        acc = p_ref[0]
        for k in range(1, P):
            acc = acc + p_ref[k]
        o_ref[...] = _rope_apply(acc, c_ref[...], sa_ref[...], sb_ref[...], LANES, True).astype(o_ref.dtype)

    return pl.pallas_call(
        body,
        name=name,
        grid=(S // tr,),
        in_specs=[pl.BlockSpec((P, tr, LANES), lambda i: (0, i, 0)), tab, tab, tab],
        out_specs=tab,
        out_shape=jax.ShapeDtypeStruct((S, LANES), BF16),
        compiler_params=_params("parallel"),
    )(parts, *tables)


MLA_SCALE = MLA_QK ** -0.5
LOG2E = math.log2(math.e)
LN2 = math.log(2.0)
MLA_Q_PRESCALE = MLA_SCALE * LOG2E


def _lane_tile(v, n):
    return v if n == LANES else jnp.tile(v, (1, n // LANES))


def _causal_mask(s):
    rows = lax.broadcasted_iota(jnp.int32, s.shape, 0)
    cols = lax.broadcasted_iota(jnp.int32, s.shape, 1)
    return jnp.where(cols <= rows, s, NEG)


def _tri_blocks(nb, q_major):
    if q_major:
        pairs = [(q, k) for q in range(nb) for k in range(q + 1)]
    else:
        pairs = [(q, k) for k in range(nb) for q in range(k, nb)]
    return (jnp.asarray(np.array([p[0] for p in pairs], np.int32)),
            jnp.asarray(np.array([p[1] for p in pairs], np.int32)))


HEAD_PAIR = 4


def _flash_fwd(q_raw, KV, krr, tables, *, heads, name, comm=()):
    S = q_raw.shape[0]
    t = _tile(S, 512)
    nb = S // t
    qt, kt = _tri_blocks(nb, True)
    qw, vw = HEAD_PAIR * MLA_QK_PAD, HEAD_PAIR * MLA_V

    def body(qt_ref, kt_ref, q_ref, *rest):
        kn_refs, kr_ref, v_refs = rest[:HEAD_PAIR], rest[HEAD_PAIR], rest[HEAD_PAIR + 1:2 * HEAD_PAIR + 1]
        c_ref, sa_ref, sb_ref, o_ref, lse_ref, qr_ref, m_scr, l_scr, acc_scr = rest[2 * HEAD_PAIR + 1:]
        step_id = pl.program_id(1)
        qi, ki = qt_ref[step_id], kt_ref[step_id]

        @pl.when(ki == 0)
        def _():
            m_scr[...] = jnp.full_like(m_scr, NEG)
            l_scr[...] = jnp.zeros_like(l_scr)
            acc_scr[...] = jnp.zeros_like(acc_scr)
            for h in range(HEAD_PAIR):
                base = h * MLA_QK_PAD
                nope = q_ref[:, base:base + MLA_NOPE].astype(F32) * MLA_Q_PRESCALE
                rot = _rope_apply(q_ref[:, base + MLA_NOPE:base + MLA_QK_PAD].astype(F32), c_ref[...], sa_ref[...],
                                  sb_ref[...], LANES, False) * MLA_Q_PRESCALE
                qr_ref[:, base:base + MLA_NOPE] = nope.astype(qr_ref.dtype)
                qr_ref[:, base + MLA_NOPE:base + MLA_QK_PAD] = rot.astype(qr_ref.dtype)

        def step(diagonal):
            for h, (kn_ref, v_ref) in enumerate(zip(kn_refs, v_refs)):
                cols = slice(h * MLA_QK_PAD, (h + 1) * MLA_QK_PAD)
                k = jnp.concatenate([kn_ref[...], kr_ref[...]], axis=1)
                s = lax.dot_general(qr_ref[:, cols], k, NT, preferred_element_type=F32)
                if diagonal:
                    s = _causal_mask(s)
                m_prev = m_scr[h]
                m_new = jnp.maximum(m_prev, jnp.max(s, axis=1, keepdims=True))
                alpha = jnp.exp2(m_prev - m_new)
                p = jnp.exp2(s - _lane_tile(m_new, t))
                l_new = alpha * l_scr[h] + jnp.sum(p, axis=1, keepdims=True)
                acc = alpha * acc_scr[h] + jnp.dot(p.astype(BF16), v_ref[...], preferred_element_type=F32)
                if diagonal:
                    o_ref[:, h * MLA_V:(h + 1) * MLA_V] = (acc / l_new).astype(o_ref.dtype)
                    lse_ref[h] = m_new + jnp.log(l_new) * LOG2E
                else:
                    l_scr[h], acc_scr[h], m_scr[h] = l_new, acc, m_new

        pl.when(ki < qi)(lambda: step(False))
        pl.when(ki == qi)(lambda: step(True))

    def kvspec(h, half):
        return pl.BlockSpec((t, LANES), lambda hp, s, qt, kt: (kt[s], 2 * (HEAD_PAIR * hp + h) + half))

    qtab = pl.BlockSpec((t, LANES), lambda hp, s, qt, kt: (qt[s], 0))
    qrow = lambda hp, s, qt, kt: (qt[s], hp)
    return _call(
        body,
        name=name,
        grid=(heads // HEAD_PAIR, int(qt.shape[0])),
        in_specs=[pl.BlockSpec((t, qw), qrow), *[kvspec(h, 0) for h in range(HEAD_PAIR)],
                  pl.BlockSpec((t, LANES), lambda hp, s, qt, kt: (kt[s], 0)),
                  *[kvspec(h, 1) for h in range(HEAD_PAIR)], qtab, qtab, qtab],
        out_specs=[pl.BlockSpec((t, vw), qrow),
                   pl.BlockSpec((HEAD_PAIR, t, LANES), lambda hp, s, qt, kt: (hp, qt[s], 0)),
                   pl.BlockSpec((t, qw), qrow)],
        out_shape=[jax.ShapeDtypeStruct((S, heads * MLA_V), BF16),
                   jax.ShapeDtypeStruct((heads, S, LANES), F32),
                   jax.ShapeDtypeStruct((S, heads * MLA_QK_PAD), BF16)],
        scratch_shapes=[pltpu.VMEM((HEAD_PAIR, t, LANES), F32), pltpu.VMEM((HEAD_PAIR, t, LANES), F32),
                        pltpu.VMEM((HEAD_PAIR, t, MLA_V), F32)],
        sem=("parallel", "arbitrary"),
        args=(q_raw, *[KV] * HEAD_PAIR, krr, *[KV] * HEAD_PAIR, *tables),
        comm=comm,
        prefetch=(qt, kt),
    )


def _flash_bwd(Q, KV, krr, dO, O, lse, tables, *, heads, name, comm=()):
    S = Q.shape[0]
    t = _tile(S, 512)
    nb = S // t
    qt, kt = _tri_blocks(nb, False)
    n_steps = int(qt.shape[0])
    qw, vw = HEAD_PAIR * MLA_QK_PAD, HEAD_PAIR * MLA_V

    def body(qt_ref, kt_ref, q_ref, *rest):
        kn_refs, kr_ref, v_refs = rest[:HEAD_PAIR], rest[HEAD_PAIR], rest[HEAD_PAIR + 1:2 * HEAD_PAIR + 1]
        (do_ref, o_ref, lse_ref, c_ref, sa_ref, sb_ref, dq_ref, dkv_ref, dkr_ref,
         dq_scr, dk_scr, dv_scr, delta_scr) = rest[2 * HEAD_PAIR + 1:]
        step_id = pl.program_id(1)
        qi, ki = qt_ref[step_id], kt_ref[step_id]

        @pl.when(ki == 0)
        def _():
            for h in range(HEAD_PAIR):
                vc = slice(h * MLA_V, (h + 1) * MLA_V)
                d = jnp.sum(do_ref[:, vc].astype(F32) * o_ref[:, vc].astype(F32), axis=1, keepdims=True)
                delta_scr[h, qi] = jnp.broadcast_to(d, (t, LANES))

        def step(diagonal):
            for h, (kn_ref, v_ref) in enumerate(zip(kn_refs, v_refs)):
                base = h * MLA_QK_PAD
                cols = slice(base, base + MLA_QK_PAD)
                vc = slice(h * MLA_V, (h + 1) * MLA_V)
                q, do = q_ref[:, cols], do_ref[:, vc]
                k = jnp.concatenate([kn_ref[...], kr_ref[...]], axis=1)
                s = lax.dot_general(q, k, NT, preferred_element_type=F32)
                if diagonal:
                    s = _causal_mask(s)
                p = jnp.exp2(s - _lane_tile(lse_ref[h], t))
                dv = lax.dot_general(p.astype(BF16), do, TN, preferred_element_type=F32)
                dp = lax.dot_general(do, v_ref[...], NT, preferred_element_type=F32)
                ds = (p * (dp - _lane_tile(delta_scr[h, qi], t))).astype(BF16)
                dk = lax.dot_general(ds, q, TN, preferred_element_type=F32)
                dq = jnp.dot(ds, k, preferred_element_type=F32)
                if diagonal:
                    dk_scr[h], dv_scr[h] = dk, dv
                    dq = (dq_scr[qi, :, cols] + dq) * (LN2 * MLA_Q_PRESCALE)
                    rot = _rope_apply(dq[:, MLA_NOPE:], c_ref[...], sa_ref[...], sb_ref[...], LANES, True)
                    dq_ref[:, base:base + MLA_NOPE] = dq[:, :MLA_NOPE].astype(dq_ref.dtype)
                    dq_ref[:, base + MLA_NOPE:base + MLA_QK_PAD] = rot.astype(dq_ref.dtype)
                else:
                    dk_scr[h] += dk
                    dv_scr[h] += dv
                    dq_scr[qi, :, cols] += dq

        @pl.when(ki == 0)
        def _():
            dq_scr[qi] = jnp.zeros((t, qw), F32)

        pl.when(qi > ki)(lambda: step(False))
        pl.when(qi == ki)(lambda: step(True))

        @pl.when(qi == nb - 1)
        def _():
            shared = jnp.zeros((t, LANES), F32)
            for h in range(HEAD_PAIR):
                base = h * MLA_QK_PAD
                dk = dk_scr[h] * LN2
                dkv_ref[:, base:base + MLA_NOPE] = dk[:, :MLA_NOPE].astype(dkv_ref.dtype)
                dkv_ref[:, base + MLA_NOPE:base + MLA_QK_PAD] = dv_scr[h].astype(dkv_ref.dtype)
                shared = shared + dk[:, MLA_NOPE:]
            dkr_ref[0] = shared

    def kvspec(h, half):
        return pl.BlockSpec((t, LANES), lambda hp, s, qt, kt: (kt[s], 2 * (HEAD_PAIR * hp + h) + half))

    qrow = lambda hp, s, qt, kt: (qt[s], hp)
    krow = lambda hp, s, qt, kt: (kt[s], hp)
    ktab = pl.BlockSpec((t, LANES), lambda hp, s, qt, kt: (kt[s], 0))
    return _call(
        body,
        name=name,
        grid=(heads // HEAD_PAIR, n_steps),
        in_specs=[pl.BlockSpec((t, qw), qrow), *[kvspec(h, 0) for h in range(HEAD_PAIR)], ktab,
                  *[kvspec(h, 1) for h in range(HEAD_PAIR)],
                  pl.BlockSpec((t, vw), qrow), pl.BlockSpec((t, vw), qrow),
                  pl.BlockSpec((HEAD_PAIR, t, LANES), lambda hp, s, qt, kt: (hp, qt[s], 0)),
                  ktab, ktab, ktab],
        out_specs=[pl.BlockSpec((t, qw), krow), pl.BlockSpec((t, qw), krow),
                   pl.BlockSpec((1, t, LANES), lambda hp, s, qt, kt: (hp, kt[s], 0))],
        out_shape=[jax.ShapeDtypeStruct((S, heads * MLA_QK_PAD), BF16),
                   jax.ShapeDtypeStruct((S, heads * MLA_QK_PAD), BF16),
                   jax.ShapeDtypeStruct((heads // HEAD_PAIR, S, LANES), F32)],
        scratch_shapes=[pltpu.VMEM((nb, t, qw), F32), pltpu.VMEM((HEAD_PAIR, t, MLA_QK_PAD), F32),
                        pltpu.VMEM((HEAD_PAIR, t, MLA_V), F32), pltpu.VMEM((HEAD_PAIR, nb, t, LANES), F32)],
        sem=("parallel", "arbitrary"),
        args=(Q, *[KV] * HEAD_PAIR, krr, *[KV] * HEAD_PAIR, dO, O, lse, *tables),
        comm=comm,
        prefetch=(qt, kt),
    )


SWA_SCALE = SWA_HD ** -0.5


def _t5_bucket_table():
    a = np.arange(BLOCK)[:, None]
    j = np.arange(2 * BLOCK)[None, :]
    dist = BLOCK + a - j
    max_exact = REL_BUCKETS // 2
    n = np.maximum(dist, 0)
    large = max_exact + (np.log(np.maximum(n, 1).astype(np.float32) / np.float32(max_exact))
                         / np.float32(math.log(REL_MAX_DIST / max_exact))
                         * np.float32(REL_BUCKETS - max_exact)).astype(np.int32)
    large = np.minimum(large, REL_BUCKETS - 1)
    bucket = np.where(n < max_exact, n, large)
    valid = (dist >= 0) & (dist < WINDOW)
    return bucket.astype(np.int32), valid


def _swa_probs(q_ref, kp_ref, kc_ref, bias_ref, sink_ref, qb, G):
    q2 = q_ref[...].reshape(G * BLOCK, SWA_HD)
    kb = jnp.concatenate([kp_ref[0], kc_ref[0]], axis=0)
    s = lax.dot_general(kb, q2, NT, preferred_element_type=F32) * SWA_SCALE + bias_ref[0]
    keys = lax.broadcasted_iota(jnp.int32, s.shape, 0)
    s = jnp.where((keys >= BLOCK) | (qb > 0), s, NEG)
    sink = sink_ref[0]
    m = jnp.maximum(jnp.max(s, axis=0, keepdims=True), sink)
    e = jnp.exp(s - m)
    es = jnp.exp(sink - m)
    inv = 1.0 / (jnp.sum(e, axis=0, keepdims=True) + es)
    return q2, kb, e * inv, es * inv


def _swa_fwd(q, k, v, bias_t, sink, *, name, comm=()):
    H, S, _ = q.shape
    G = H // SWA_KVH
    nb = S // BLOCK
    cur = lambda kh, qb: (kh, qb, 0)
    prev = lambda kh, qb: (kh, jnp.maximum(qb - 1, 0), 0)
    kvspec = lambda im: pl.BlockSpec((1, BLOCK, SWA_HD), im)

    def body(q_ref, kc_ref, kp_ref, vc_ref, vp_ref, bias_ref, sink_ref, o_ref):
        qb = pl.program_id(1)
        _, _, pt, _ = _swa_probs(q_ref, kp_ref, kc_ref, bias_ref, sink_ref, qb, G)
        vb = jnp.concatenate([vp_ref[0], vc_ref[0]], axis=0)
        o_ref[0, 0] = lax.dot_general(vb, pt.astype(BF16), TN, preferred_element_type=F32).astype(o_ref.dtype)

    outs, moved = _call(
        body,
        name=name,
        grid=(SWA_KVH, nb),
        in_specs=[pl.BlockSpec((G, BLOCK, SWA_HD), cur), kvspec(cur), kvspec(prev), kvspec(cur), kvspec(prev),
                  pl.BlockSpec((1, 2 * BLOCK, G * BLOCK), lambda kh, qb: (kh, 0, 0)),
                  pl.BlockSpec((1, 1, G * BLOCK), lambda kh, qb: (kh, 0, 0))],
        out_specs=[pl.BlockSpec((1, 1, SWA_HD, G * BLOCK), lambda kh, qb: (kh, qb, 0, 0))],
        out_shape=[jax.ShapeDtypeStruct((SWA_KVH, nb, SWA_HD, G * BLOCK), BF16)],
        sem=("parallel", "parallel"),
        args=(q, k, k, v, v, bias_t, sink),
        comm=comm,
    )
    return outs[0], moved


def _swa_bwd(q, k, v, bias_t, sink, do, *, name, comm=()):
    H, S, _ = q.shape
    G = H // SWA_KVH
    nb = S // BLOCK
    cur = lambda kh, qb: (kh, jnp.minimum(qb, nb - 1), 0)
    prev = lambda kh, qb: (kh, jnp.maximum(jnp.minimum(qb, nb - 1) - 1, 0), 0)
    lag = lambda kh, qb: (kh, jnp.maximum(qb - 1, 0), 0)
    kvspec = lambda im: pl.BlockSpec((1, BLOCK, SWA_HD), im)

    def body(q_ref, kc_ref, kp_ref, vc_ref, vp_ref, bias_ref, sink_ref, do_ref,
             dq_ref, dk_ref, dv_ref, dbias_ref, dsink_ref, ck_scr, cv_scr):
        qb = pl.program_id(1)

        @pl.when(qb == 0)
        def _():
            dbias_ref[...] = jnp.zeros_like(dbias_ref)
            dsink_ref[...] = jnp.zeros_like(dsink_ref)
            ck_scr[...] = jnp.zeros_like(ck_scr)
            cv_scr[...] = jnp.zeros_like(cv_scr)

        @pl.when(qb < nb)
        def _():
            q2, kb, pt, ps = _swa_probs(q_ref, kp_ref, kc_ref, bias_ref, sink_ref, qb, G)
            vb = jnp.concatenate([vp_ref[0], vc_ref[0]], axis=0)
            do2 = do_ref[...].reshape(G * BLOCK, SWA_HD)
            dpt = lax.dot_general(vb, do2, NT, preferred_element_type=F32)
            delta = jnp.sum(dpt * pt, axis=0, keepdims=True)
            dst = pt * (dpt - delta)
            dbias_ref[0] += dst
            dsink_ref[0] += -ps * delta
            dsb = (dst * SWA_SCALE).astype(BF16)
            dq_ref[0, 0] = lax.dot_general(kb, dsb, TN, preferred_element_type=F32).astype(dq_ref.dtype)
            dkb = jnp.dot(dsb, q2, preferred_element_type=F32)
            dvb = jnp.dot(pt.astype(BF16), do2, preferred_element_type=F32)
            dk_ref[0] = (ck_scr[...] + dkb[:BLOCK]).astype(dk_ref.dtype)
            dv_ref[0] = (cv_scr[...] + dvb[:BLOCK]).astype(dv_ref.dtype)
            ck_scr[...] = dkb[BLOCK:]
            cv_scr[...] = dvb[BLOCK:]

        @pl.when(qb == nb)
        def _():
            dk_ref[0] = ck_scr[...].astype(dk_ref.dtype)
            dv_ref[0] = cv_scr[...].astype(dv_ref.dtype)

    tspec = pl.BlockSpec((1, 1, SWA_HD, G * BLOCK), lambda kh, qb: (kh, jnp.minimum(qb, nb - 1), 0, 0))
    return _call(
        body,
        name=name,
        grid=(SWA_KVH, nb + 1),
        in_specs=[pl.BlockSpec((G, BLOCK, SWA_HD), cur), kvspec(cur), kvspec(prev), kvspec(cur), kvspec(prev),
                  pl.BlockSpec((1, 2 * BLOCK, G * BLOCK), lambda kh, qb: (kh, 0, 0)),
                  pl.BlockSpec((1, 1, G * BLOCK), lambda kh, qb: (kh, 0, 0)),
                  pl.BlockSpec((G, BLOCK, SWA_HD), cur)],
        out_specs=[tspec, kvspec(lag), kvspec(lag),
                   pl.BlockSpec((1, 2 * BLOCK, G * BLOCK), lambda kh, qb: (kh, 0, 0)),
                   pl.BlockSpec((1, 1, G * BLOCK), lambda kh, qb: (kh, 0, 0))],
        out_shape=[jax.ShapeDtypeStruct((SWA_KVH, nb, SWA_HD, G * BLOCK), BF16),
                   jax.ShapeDtypeStruct((SWA_KVH, S, SWA_HD), BF16),
                   jax.ShapeDtypeStruct((SWA_KVH, S, SWA_HD), BF16),
                   jax.ShapeDtypeStruct((SWA_KVH, 2 * BLOCK, G * BLOCK), F32),
                   jax.ShapeDtypeStruct((SWA_KVH, 1, G * BLOCK), F32)],
        scratch_shapes=[pltpu.VMEM((BLOCK, SWA_HD), F32), pltpu.VMEM((BLOCK, SWA_HD), F32)],
        sem=("parallel", "arbitrary"),
        args=(q, k, k, v, v, bias_t, sink, do),
        comm=comm,
    )


def _gate_mix(z, o_a, o_b, *, D, off_a, off_b, name):
    S = z.shape[0]
    tr = _tile(S, 256, 16)
    row = pl.BlockSpec((tr, D), lambda i: (i, 0))
    ca, cb = _col_view(D, off_a), _col_view(D, off_b)

    def body(ga_ref, gb_ref, oa_ref, ob_ref, m_ref):
        m = (_sigmoid(ga_ref[...].astype(F32)) * oa_ref[...].astype(F32)
             + _sigmoid(gb_ref[...].astype(F32)) * ob_ref[...].astype(F32))
        m_ref[...] = m.astype(m_ref.dtype)

    return pl.pallas_call(
        body,
        name=name,
        grid=(S // tr,),
        in_specs=[pl.BlockSpec((tr, D), lambda i: (i, ca)), pl.BlockSpec((tr, D), lambda i: (i, cb)), row, row],
        out_specs=row,
        out_shape=jax.ShapeDtypeStruct((S, D), BF16),
        compiler_params=_params("parallel"),
    )(z, z, o_a, o_b)


def _gate_mix_bwd(dm, z, o_a, o_b, *, D, off_a, off_b, name):
    S = z.shape[0]
    tr = _tile(S, 256, 16)
    row = pl.BlockSpec((tr, D), lambda i: (i, 0))
    ca, cb = _col_view(D, off_a), _col_view(D, off_b)

    def body(dm_ref, ga_ref, gb_ref, oa_ref, ob_ref, dga_ref, dgb_ref, doa_ref, dob_ref):
        d = dm_ref[...].astype(F32)
        for g_ref, o_ref, dg_ref, do_ref in ((ga_ref, oa_ref, dga_ref, doa_ref), (gb_ref, ob_ref, dgb_ref, dob_ref)):
            sg = _sigmoid(g_ref[...].astype(F32))
            dg_ref[...] = (d * o_ref[...].astype(F32) * (sg * (1.0 - sg))).astype(dg_ref.dtype)
            do_ref[...] = (d * sg).astype(do_ref.dtype)

    return pl.pallas_call(
        body,
        name=name,
        grid=(S // tr,),
        in_specs=[row, pl.BlockSpec((tr, D), lambda i: (i, ca)), pl.BlockSpec((tr, D), lambda i: (i, cb)), row, row],
        out_specs=[row] * 4,
        out_shape=[jax.ShapeDtypeStruct((S, D), BF16)] * 4,
        compiler_params=_params("parallel"),
    )(dm, z, z, o_a, o_b)


CONV_ROWS = 256
CONV_COLS = 1408
SUBLANES = 8


def _shift_matrices(tr):
    r = np.arange(tr)[:, None]
    c = np.arange(tr)[None, :]
    back = [jnp.asarray(r == c + d, dtype=BF16) for d in (1, 2)]
    ahead = [jnp.asarray(r + d == c, dtype=BF16) for d in (1, 2)]
    return back, ahead


def _rows_before(x, halo_ref, first, b1_ref, b2_ref):
    s1 = jnp.dot(b1_ref[...], x, preferred_element_type=F32)
    s2 = jnp.dot(b2_ref[...], x, preferred_element_type=F32)
    h8 = jnp.where(first, 0.0, halo_ref[...].astype(F32)[HALO - SUBLANES:])
    rows = lax.broadcasted_iota(jnp.int32, h8.shape, 0)
    fix1 = jnp.where(rows < 1, pltpu.roll(h8, 1, 0), 0.0)
    fix2 = jnp.where(rows < 2, pltpu.roll(h8, 2, 0), 0.0)
    s1 = jnp.concatenate([s1[:SUBLANES] + fix1, s1[SUBLANES:]], axis=0)
    s2 = jnp.concatenate([s2[:SUBLANES] + fix2, s2[SUBLANES:]], axis=0)
    return s1, s2


def _conv_taps(x, s1, s2, cw_ref, cb_ref):
    return cb_ref[...] + cw_ref[0:1, :] * s2 + cw_ref[1:2, :] * s1 + cw_ref[2:3, :] * x


def _conv_gate(up, cw, cb, *, name):
    S, F2 = up.shape
    F = F2 // 2
    tr = _tile(S, CONV_ROWS, HALO)
    tc = _tile(F, CONV_COLS)
    nc = F // tc
    hb = tr // HALO
    back, _ = _shift_matrices(tr)
    mat = pl.BlockSpec((tr, tr), lambda i, j: (0, 0))

    def halo_map(shift):
        return lambda i, j: (jnp.maximum(i * hb - 1, 0), j + shift)

    def body(x1_ref, h1_ref, x2_ref, h2_ref, cw1_ref, cw2_ref, cb1_ref, cb2_ref, b1_ref, b2_ref, a_ref):
        first = pl.program_id(0) == 0
        us = []
        for x_ref, h_ref, cw_ref, cb_ref in ((x1_ref, h1_ref, cw1_ref, cb1_ref), (x2_ref, h2_ref, cw2_ref, cb2_ref)):
            x = x_ref[...]
            s1, s2 = _rows_before(x, h_ref, first, b1_ref, b2_ref)
            us.append(_conv_taps(x.astype(F32), s1, s2, cw_ref, cb_ref))
        u1, u2 = us
        a_ref[...] = (u1 * _sigmoid(u1) * u2).astype(a_ref.dtype)

    return pl.pallas_call(
        body,
        name=name,
        grid=(S // tr, nc),
        in_specs=[pl.BlockSpec((tr, tc), lambda i, j: (i, j)), pl.BlockSpec((HALO, tc), halo_map(0)),
                  pl.BlockSpec((tr, tc), lambda i, j: (i, j + nc)), pl.BlockSpec((HALO, tc), halo_map(nc)),
                  pl.BlockSpec((CONV_WIDTH, tc), lambda i, j: (0, j)),
                  pl.BlockSpec((CONV_WIDTH, tc), lambda i, j: (0, j + nc)),
                  pl.BlockSpec((1, tc), lambda i, j: (0, j)), pl.BlockSpec((1, tc), lambda i, j: (0, j + nc)),
                  mat, mat],
        out_specs=pl.BlockSpec((tr, tc), lambda i, j: (i, j)),
        out_shape=jax.ShapeDtypeStruct((S, F), BF16),
        compiler_params=_params("parallel", "parallel"),
    )(up, up, up, up, cw, cw, cb, cb, *back)


def _conv_gate_bwd(up, da, cw, cb, *, name, comm=()):
    S, F2 = up.shape
    F = F2 // 2
    tr = _tile(S, CONV_ROWS, HALO)
    tc = _tile(F, CONV_COLS)
    nc = F // tc
    hb = tr // HALO
    ni = S // tr
    back, ahead = _shift_matrices(tr)
    mat = pl.BlockSpec((tr, tr), lambda j, r: (0, 0))

    def cur(shift):
        return lambda j, r: (ni - 1 - r, j + shift)

    def before(shift):
        return lambda j, r: (jnp.maximum((ni - 1 - r) * hb - 1, 0), j + shift)

    def vec(rows, shift):
        return pl.BlockSpec((rows, tc), lambda j, r: (0, j + shift))

    def body(x1_ref, h1_ref, x2_ref, h2_ref, da_ref, cw1_ref, cw2_ref, cb1_ref, cb2_ref,
             b1_ref, b2_ref, a1_ref, a2_ref, dup_ref, dcw_ref, dcb_ref, next_du):
        r = pl.program_id(1)
        first = r == ni - 1

        @pl.when(r == 0)
        def _():
            dcw_ref[...] = jnp.zeros_like(dcw_ref)
            dcb_ref[...] = jnp.zeros_like(dcb_ref)
            next_du[...] = jnp.zeros_like(next_du)

        x1, x2 = x1_ref[...], x2_ref[...]
        x1f, x2f = x1.astype(F32), x2.astype(F32)
        s11, s12 = _rows_before(x1, h1_ref, first, b1_ref, b2_ref)
        s21, s22 = _rows_before(x2, h2_ref, first, b1_ref, b2_ref)
        u1 = _conv_taps(x1f, s11, s12, cw1_ref, cb1_ref)
        u2 = _conv_taps(x2f, s21, s22, cw2_ref, cb2_ref)
        sg = _sigmoid(u1)
        daf = da_ref[...].astype(F32)
        du1 = daf * u2 * (sg * (1.0 + u1 * (1.0 - sg)))
        du2 = daf * (u1 * sg)
        rows = lax.broadcasted_iota(jnp.int32, (SUBLANES, tc), 0)

        for half, (du, own, own1, own2, cw_ref) in enumerate(((du1, x1f, s11, s12, cw1_ref),
                                                             (du2, x2f, s21, s22, cw2_ref))):
            du_b = du.astype(BF16)
            n1 = jnp.dot(a1_ref[...], du_b, preferred_element_type=F32)
            n2 = jnp.dot(a2_ref[...], du_b, preferred_element_type=F32)
            c8 = next_du[half]
            fix1 = jnp.where(rows >= SUBLANES - 1, pltpu.roll(c8, SUBLANES - 1, 0), 0.0)
            fix2 = jnp.where(rows >= SUBLANES - 2, pltpu.roll(c8, SUBLANES - 2, 0), 0.0)
            n1 = jnp.concatenate([n1[:tr - SUBLANES], n1[tr - SUBLANES:] + fix1], axis=0)
            n2 = jnp.concatenate([n2[:tr - SUBLANES], n2[tr - SUBLANES:] + fix2], axis=0)
            dup = cw_ref[2:3, :] * du + cw_ref[1:2, :] * n1 + cw_ref[0:1, :] * n2
            dup_ref[half] = dup.astype(dup_ref.dtype)
            dcb_ref[half] += jnp.sum(du, axis=0, keepdims=True)
            for tap, shifted in enumerate((own2, own1, own)):
                dcw_ref[half, tap:tap + 1, :] += jnp.sum(du * shifted, axis=0, keepdims=True)
            next_du[half] = du[:SUBLANES].astype(BF16).astype(F32)

    return _call(
        body,
        name=name,
        grid=(nc, ni),
        in_specs=[pl.BlockSpec((tr, tc), cur(0)), pl.BlockSpec((HALO, tc), before(0)),
                  pl.BlockSpec((tr, tc), cur(nc)), pl.BlockSpec((HALO, tc), before(nc)),
                  pl.BlockSpec((tr, tc), cur(0)),
                  vec(CONV_WIDTH, 0), vec(CONV_WIDTH, nc), vec(1, 0), vec(1, nc), mat, mat, mat, mat],
        out_specs=[pl.BlockSpec((2, tr, tc), lambda j, r: (0, ni - 1 - r, j)),
                   pl.BlockSpec((2, CONV_WIDTH, tc), lambda j, r: (0, 0, j)),
                   pl.BlockSpec((2, 1, tc), lambda j, r: (0, 0, j))],
        out_shape=[jax.ShapeDtypeStruct((2, S, F), BF16), jax.ShapeDtypeStruct((2, CONV_WIDTH, F), F32),
                   jax.ShapeDtypeStruct((2, 1, F), F32)],
        scratch_shapes=[pltpu.VMEM((2, SUBLANES, tc), F32)],
        sem=("parallel", "arbitrary"),
        args=(up, up, up, up, da, cw, cw, cb, cb, *back, *ahead),
        comm=comm,
    )


def _adam_math(w, g, m, v):
    m = ADAM_B1 * m + (1.0 - ADAM_B1) * g
    v = ADAM_B2 * v + (1.0 - ADAM_B2) * (g * g)
    m_hat = m / (1.0 - ADAM_B1 ** ADAM_STEP)
    v_hat = v / (1.0 - ADAM_B2 ** ADAM_STEP)
    delta = -ADAM_LR * (m_hat / (jnp.sqrt(v_hat) + ADAM_EPS) + ADAM_WD * w)
    return delta, m, v


def _adamw(w, m, v, parts, *, name):
    R, C = w.shape
    plist = list(parts) if isinstance(parts, (list, tuple)) else [parts]
    tr = _tile(min(p.shape[1] for p in plist), 256, 16)
    assert sum(p.shape[1] for p in plist) == R and all(p.shape[1] % tr == 0 for p in plist)
    row = pl.BlockSpec((tr, C), lambda i: (i, 0))
    first, spans = 0, []
    for p in plist:
        spans.append((first, first + p.shape[1] // tr))
        first = spans[-1][1]

    def body(w_ref, m_ref, v_ref, *rest):
        p_refs, (g_ref, d_ref, m2_ref, v2_ref) = rest[:len(plist)], rest[len(plist):]
        i = pl.program_id(0)

        def update(p_ref):
            g = p_ref[0].astype(F32)
            for k in range(1, N_DEV):
                g = g + p_ref[k].astype(F32)
            g_ref[...] = g
            d_ref[...], m2_ref[...], v2_ref[...] = _adam_math(w_ref[...], g, m_ref[...], v_ref[...])

        if len(plist) == 1:
            update(p_refs[0])
        else:
            for p_ref, (lo, hi) in zip(p_refs, spans):
                pl.when((i >= lo) & (i < hi))(functools.partial(update, p_ref))

    def part_spec(lo, hi):
        return pl.BlockSpec((N_DEV, tr, C), lambda i: (0, jnp.clip(i - lo, 0, hi - lo - 1), 0))

    return pl.pallas_call(
        body,
        name=name,
        grid=(R // tr,),
        in_specs=[row, row, row] + [part_spec(lo, hi) for lo, hi in spans],
        out_specs=[row] * 4,
        out_shape=[jax.ShapeDtypeStruct((R, C), F32)] * 4,
        compiler_params=_params("parallel"),
    )(w, m, v, *plist)


def _adamw_ada(w, m, v, cact_t, dmod_cols, *, name):
    R, C = w.shape
    B = cact_t.shape[1]
    tr = _tile(R, 256, 8)
    row = pl.BlockSpec((tr, C), lambda i: (i, 0))

    def body(w_ref, m_ref, v_ref, c_ref, d_ref, g_ref, dl_ref, m2_ref, v2_ref):
        g = c_ref[:, 0:1] * d_ref[0:1, :]
        for b in range(1, B):
            g = g + c_ref[:, b:b + 1] * d_ref[b:b + 1, :]
        g_ref[...] = g
        dl_ref[...], m2_ref[...], v2_ref[...] = _adam_math(w_ref[...], g, m_ref[...], v_ref[...])

    return pl.pallas_call(
        body,
        name=name,
        grid=(R // tr,),
        in_specs=[row, row, row, pl.BlockSpec((tr, B), lambda i: (i, 0)), pl.BlockSpec((B, C), lambda i: (0, 0))],
        out_specs=[row] * 4,
        out_shape=[jax.ShapeDtypeStruct((R, C), F32)] * 4,
        compiler_params=_params("parallel"),
    )(w, m, v, cact_t, dmod_cols)


def _z_layout(D, q_rank, kv_rank):
    kv = SWA_KVH * SWA_HD
    orig = {}
    o = 0
    for nm, w in (("cq", q_rank), ("ckv", kv_rank), ("kr", MLA_ROPE), ("qs", D), ("ks", kv), ("vs", kv),
                  ("ga", D), ("gb", D)):
        orig[nm] = (o, w)
        o += w
    blockw = {"cq": q_rank, "ckv": kv_rank, "kr": LANES, "qs": D, "ks": kv, "vs": kv, "ga": D, "gb": D}
    best = None
    for perm in itertools.permutations(("cq", "ckv", "ks", "vs", "kr")):
        off, new = 0, {}
        for nm in ("ga", "gb", "qs") + perm:
            off = _round_up(off, blockw[nm])
            new[nm] = off
            off += blockw[nm]
        if best is None or off < best[0]:
            best = (off, new)
    total = _round_up(best[0], 1024 if best[0] > 4096 else 512)
    return orig, best[1], blockw, total, o


def _permute_w_in(w, lay):
    orig, new, blockw, total, _ = lay
    parts, at = [], 0
    for nm in sorted(new, key=new.get):
        if new[nm] > at:
            parts.append(jnp.zeros((w.shape[0], new[nm] - at), w.dtype))
        o, wd = orig[nm]
        parts.append(w[:, o:o + wd])
        if blockw[nm] > wd:
            parts.append(jnp.zeros((w.shape[0], blockw[nm] - wd), w.dtype))
        at = new[nm] + blockw[nm]
    if total > at:
        parts.append(jnp.zeros((w.shape[0], total - at), w.dtype))
    return jnp.concatenate(parts, axis=1)


def _unpermute_w_in(wp, lay):
    orig, new, _, _, _ = lay
    return jnp.concatenate([wp[:, new[nm]:new[nm] + orig[nm][1]] for nm in sorted(orig, key=lambda n: orig[n][0])],
                           axis=1)


def _assemble_dz(parts, lay, S):
    _, new, blockw, total, _ = lay
    names = sorted(new, key=new.get)
    tr = _tile(S, 256, 16)

    def body(*refs):
        o_ref = refs[-1]
        cols, at = [], 0
        for nm, ref in zip(names, refs):
            if new[nm] > at:
                cols.append(jnp.zeros((tr, new[nm] - at), BF16))
            cols.append(ref[...])
            at = new[nm] + blockw[nm]
        if total > at:
            cols.append(jnp.zeros((tr, total - at), BF16))
        o_ref[...] = jnp.concatenate(cols, axis=1)

    return pl.pallas_call(
        body,
        name="assemble_dz",
        grid=(S // tr,),
        in_specs=[pl.BlockSpec((tr, blockw[nm]), lambda i: (i, 0)) for nm in names],
        out_specs=pl.BlockSpec((tr, total), lambda i: (i, 0)),
        out_shape=jax.ShapeDtypeStruct((S, total), BF16),
        compiler_params=_params("parallel"),
    )(*[parts[nm] for nm in names])


def _unshard_cols(g):
    return jnp.transpose(g, (1, 0, 2)).reshape(g.shape[1], N_DEV * g.shape[2])


def _shard_cols(w):
    K, N = w.shape
    return jnp.transpose(w.reshape(K, N_DEV, N // N_DEV), (1, 0, 2))


def _pack(vecs, rows):
    flat = jnp.concatenate([v.reshape(-1) for v in vecs])
    return jnp.pad(flat, (0, rows * LANES - flat.shape[0])).reshape(rows, LANES)


def kernel(x, c, w_ada, b_ada, g_pre_mix, g_post_mix, w_in, g_q_lat, w_uq, g_kv_lat, w_ukv, rel_bias, sinks, w_o, g_pre_ffn, g_post_ffn, w_up, conv_w, conv_b, w_down, loss_target, m_w_ada, m_b_ada, m_g_pre_mix, m_g_post_mix, m_w_in, m_g_q_lat, m_w_uq, m_g_kv_lat, m_w_ukv, m_rel_bias, m_sinks, m_w_o, m_g_pre_ffn, m_g_post_ffn, m_w_up, m_conv_w, m_conv_b, m_w_down, v_w_ada, v_b_ada, v_g_pre_mix, v_g_post_mix, v_w_in, v_g_q_lat, v_w_uq, v_g_kv_lat, v_w_ukv, v_rel_bias, v_sinks, v_w_o, v_g_pre_ffn, v_g_post_ffn, v_w_up, v_conv_w, v_conv_b, v_w_down):
    S, D = x.shape[1], x.shape[2]
    Q_RANK, KV_RANK = g_q_lat.shape[1], g_kv_lat.shape[1]
    H_MLA = D // MLA_V
    H_SWA = D // SWA_HD
    G_SWA = H_SWA // SWA_KVH
    F2 = w_up.shape[2] * N_DEV
    F = F2 // 2
    ada_n = w_ada.shape[2]
    me = 4 * lax.axis_index("x") + 2 * lax.axis_index("y") + lax.axis_index("c")
    lay = _z_layout(D, Q_RANK, KV_RANK)
    _, zoff, _, NZ, in_cols = lay
    assert in_cols == w_in.shape[2] * N_DEV

    x2, tgt = x[0], loss_target[0]

    cw_n = conv_w.shape[2]
    small = jnp.concatenate([jnp.pad(c, ((0, 7), (0, 0))), jnp.pad(conv_w[0], ((0, 8 - CONV_WIDTH), (0, 0)))], axis=1)
    small_all = _all_gather(small, name="ag_cond", in_vmem=True)
    c_all = small_all[:, 0, :D]
    cw_full = _unshard_cols(small_all[:, :CONV_WIDTH, D:])
    b_cols = lax.dynamic_slice_in_dim(b_ada, me * ada_n, ada_n, axis=1)
    c_act, mod_cols = _ada_fwd(c_all, w_ada[0], b_cols, name="ada_fwd")
    mod_all = _all_gather(mod_cols, name="ag_mod", in_vmem=True)
    mod_me = lax.dynamic_index_in_dim(mod_all, me, axis=1, keepdims=False).reshape(1, N_DEV * ada_n)
    sh1, sc1, gt1, sh2, sc2, gt2 = [mod_me[:, k * D:(k + 1) * D] for k in range(6)]

    w_in_p = _permute_w_in(_unshard_cols(_all_gather(w_in[0].astype(BF16), name="ag_w_in", in_vmem=False)), lay)

    h1 = _prenorm(x2, g_pre_mix, sc1, sh1, name="prenorm_mix")
    z, (uq_g, ukv_g, o_g) = _matmul(h1, w_in_p, mode="nn", out_dtype=BF16, name="mm_in",
                                    comm=[("gather", w_uq[0].astype(BF16)), ("gather", w_ukv[0].astype(BF16)),
                                          ("gather", w_o[0].astype(BF16))])
    w_uq_p = jnp.pad(_unshard_cols(uq_g).reshape(Q_RANK, H_MLA, MLA_QK), ((0, 0), (0, 0), (0, MLA_QK_PAD - MLA_QK))
                     ).reshape(Q_RANK, H_MLA * MLA_QK_PAD)
    w_ukv_f = _unshard_cols(ukv_g)
    w_o_f = o_g.reshape(D, D)
    cqn = _prenorm(z, g_q_lat, None, None, name="norm_cq", off=zoff["cq"], width=Q_RANK)
    ckvn = _prenorm(z, g_kv_lat, None, None, name="norm_ckv", off=zoff["ckv"], width=KV_RANK)
    q_raw = _matmul(cqn, w_uq_p, mode="nn", out_dtype=BF16, name="mm_uq")
    kv = _matmul(ckvn, w_ukv_f, mode="nn", out_dtype=BF16, name="mm_ukv")
    tab_k = _rope_tables(S, LANES, 0)
    krr = _rope(z, tab_k, heads=1, width=LANES, transpose=False, name="rope_k", off=zoff["kr"])
    (o_a, lse, Qr), (up_g,) = _flash_fwd(q_raw, kv, krr, tab_k, heads=H_MLA, name="mla_fwd",
                                        comm=[("gather", w_up[0].astype(BF16))])
    w_up_f = _unshard_cols(up_g)

    bucket, valid = _t5_bucket_table()
    onehot = (jnp.asarray(bucket).reshape(-1, 1) == jnp.arange(LANES)[None, :]).astype(F32)
    rb_pad = jnp.pad(rel_bias, ((0, LANES - REL_BUCKETS), (0, LANES - H_SWA)))
    bias_t = _matmul(onehot, rb_pad, mode="nn", out_dtype=F32, name="bias_table", tm=2048, precision=HIGHEST)
    bias_full = jnp.transpose(bias_t[:, :H_SWA].reshape(BLOCK, 2 * BLOCK, H_SWA), (2, 0, 1))
    bias_full = jnp.where(jnp.asarray(valid)[None], bias_full, NEG)
    bias_full = jnp.transpose(bias_full.reshape(SWA_KVH, G_SWA, BLOCK, 2 * BLOCK), (0, 3, 1, 2)
                              ).reshape(SWA_KVH, 2 * BLOCK, G_SWA * BLOCK)
    sink_rows = jnp.broadcast_to(sinks.reshape(SWA_KVH, G_SWA, 1), (SWA_KVH, G_SWA, BLOCK)
                                 ).reshape(SWA_KVH, 1, G_SWA * BLOCK)
    kvw = SWA_KVH * SWA_HD

    def heads_first(t, n):
        return jnp.transpose(t.reshape(S, n, SWA_HD), (1, 0, 2))

    def heads_last(t):
        return jnp.transpose(t, (1, 0, 2)).reshape(S, t.shape[0] * SWA_HD)

    def queries_first(t):
        t = t.reshape(SWA_KVH, S // BLOCK, SWA_HD, G_SWA, BLOCK)
        return jnp.transpose(t, (1, 4, 0, 3, 2)).reshape(S, H_SWA * SWA_HD)

    qs_h = heads_first(z[:, zoff["qs"]:zoff["qs"] + D], H_SWA)
    ks_h = heads_first(z[:, zoff["ks"]:zoff["ks"] + kvw], SWA_KVH)
    vs_h = heads_first(z[:, zoff["vs"]:zoff["vs"] + kvw], SWA_KVH)
    o_b_h, _ = _swa_fwd(qs_h, ks_h, vs_h, bias_full, sink_rows, name="swa_fwd")
    o_b = queries_first(o_b_h)

    mixin = _gate_mix(z, o_a, o_b, D=D, off_a=zoff["ga"], off_b=zoff["gb"], name="gate_mix")
    mix = _matmul(mixin, w_o_f, mode="nn", out_dtype=F32, name="mm_o")
    x1 = _postnorm_res(x2, mix, gt1, g_post_mix, name="postnorm_mix")

    h2 = _prenorm(x1, g_pre_ffn, sc2, sh2, name="prenorm_ffn")
    up, (down_g,) = _matmul(h2, w_up_f, mode="nn", out_dtype=BF16, name="mm_up",
                            comm=[("gather", w_down[0].astype(BF16))])
    w_down_f = down_g.reshape(F, D)
    act = _conv_gate(up, cw_full, conv_b, name="conv_gate")
    y = _matmul(act, w_down_f, mode="nn", out_dtype=F32, name="mm_down")
    loss_part, dout, dy, dgt2, dg_post_ffn = _final_loss(x1, y, tgt, gt2, g_post_ffn, name="final_loss")
    loss = lax.psum(loss_part[0, 0], ("x", "y", "c"))

    dw_down = _matmul(act, dy, mode="tn", out_dtype=BF16, name="mm_down_dw")
    dact = _matmul(dy, w_down_f, mode="nt", out_dtype=BF16, name="mm_down_dx")
    (dup, dcw, dcb), (got_down,) = _conv_gate_bwd(up, dact, cw_full, conv_b, name="conv_gate_bwd",
                                                  comm=[("scatter", dw_down.reshape(N_DEV, F // N_DEV, D))])
    dcw = jnp.transpose(dcw, (1, 0, 2)).reshape(CONV_WIDTH, F2)
    dcb = dcb.reshape(1, F2)
    dw_up = _matmul(h2, dup, mode="tn", out_dtype=BF16, name="mm_up_dw", shard_out=True, halves=True)
    dh2 = _matmul(dup, w_up_f, mode="nt", out_dtype=F32, name="mm_up_dx", halves=True)
    dx1, dg_pre_ffn, dsc2, dsh2 = _prenorm_bwd(x1, dh2, dout, g_pre_ffn, sc2, name="prenorm_ffn_bwd", out_dtype=F32)

    dmix, dgt1, dg_post_mix = _postnorm_bwd(dx1, mix, gt1, g_post_mix, name="postnorm_mix_bwd")
    dw_o = _matmul(mixin, dmix, mode="tn", out_dtype=BF16, name="mm_o_dw")
    dmixin = _matmul(dmix, w_o_f, mode="nt", out_dtype=BF16, name="mm_o_dx")
    dga, dgb, do_a, do_b = _gate_mix_bwd(dmixin, z, o_a, o_b, D=D, off_a=zoff["ga"], off_b=zoff["gb"],
                                         name="gate_mix_bwd")
    (dq_raw, dkv, dkr_parts), (got_up,) = _flash_bwd(Qr, kv, krr, do_a, o_a, lse, tab_k, heads=H_MLA, name="mla_bwd",
                                                     comm=[("scatter", dw_up)])
    dkr = _shared_rope_grad(dkr_parts, tab_k, name="rope_k_bwd")
    dw_uq_p = _matmul(cqn, dq_raw, mode="tn", out_dtype=BF16, name="mm_uq_dw")
    dcqn = _matmul(dq_raw, w_uq_p, mode="nt", out_dtype=F32, name="mm_uq_dx")
    dw_ukv = _matmul(ckvn, dkv, mode="tn", out_dtype=BF16, name="mm_ukv_dw", shard_out=True)
    dckvn = _matmul(dkv, w_ukv_f, mode="nt", out_dtype=F32, name="mm_ukv_dx")
    dw_uq = dw_uq_p.reshape(Q_RANK, H_MLA, MLA_QK_PAD)[:, :, :MLA_QK].reshape(Q_RANK, H_MLA * MLA_QK)

    dcw_parts = jnp.pad(_shard_cols(dcw), ((0, 0), (0, 16 - CONV_WIDTH), (0, 0)))
    (dqs_h, dks_h, dvs_h, dbias, dsink), (got_o, got_cw, got_uq, got_ukv) = _swa_bwd(
        qs_h, ks_h, vs_h, bias_full, sink_rows, heads_first(do_b, H_SWA), name="swa_bwd",
        comm=[("scatter", dw_o.reshape(N_DEV, D // N_DEV, D)), ("scatter", dcw_parts),
              ("scatter", _shard_cols(dw_uq)), ("scatter", dw_ukv)])
    dbias = jnp.transpose(dbias.reshape(SWA_KVH, 2 * BLOCK, G_SWA, BLOCK), (0, 2, 3, 1))
    drel_t = _matmul(dbias.reshape(H_SWA, BLOCK * 2 * BLOCK), onehot, mode="nn", out_dtype=F32, name="bias_grad",
                     tk=4096, precision=HIGHEST)
    d_rel_bias = jnp.transpose(drel_t[:, :REL_BUCKETS])
    d_sinks = jnp.sum(dsink.reshape(SWA_KVH, G_SWA, BLOCK), axis=-1).reshape(1, H_SWA)

    dcq, dg_q = _prenorm_bwd(z, dcqn, None, g_q_lat, None, name="norm_cq_bwd", out_dtype=BF16,
                             off=zoff["cq"], width=Q_RANK)
    dckv, dg_kv = _prenorm_bwd(z, dckvn, None, g_kv_lat, None, name="norm_ckv_bwd", out_dtype=BF16,
                               off=zoff["ckv"], width=KV_RANK)
    dz = _assemble_dz({"ga": dga, "gb": dgb, "qs": queries_first(dqs_h), "cq": dcq, "ckv": dckv,
                       "ks": heads_last(dks_h), "vs": heads_last(dvs_h), "kr": dkr}, lay, S)
    dw_in_a = _matmul(h1, dz, mode="tn", out_dtype=BF16, name="mm_in_dw_a", m_range=(0, D // 2))
    dw_in_b, (got_in_a,) = _matmul(h1, dz, mode="tn", out_dtype=BF16, name="mm_in_dw_b", m_range=(D // 2, D // 2),
                                   comm=[("scatter", _shard_cols(_unpermute_w_in(dw_in_a, lay)))])
    dh1, (got_in_b,) = _matmul(dz, w_in_p, mode="nt", out_dtype=F32, name="mm_in_dx",
                               comm=[("scatter", _shard_cols(_unpermute_w_in(dw_in_b, lay)))])
    grad_x, dg_pre_mix, dsc1, dsh1 = _prenorm_bwd(x2, dh1, dx1, g_pre_mix, sc1, name="prenorm_mix_bwd",
                                                  out_dtype=F32)
    dmod = jnp.concatenate([dsh1, dsc1, dgt1, dsh2, dsc2, dgt2], axis=1)

    small_names = ["b_ada", "g_pre_mix", "g_post_mix", "g_q_lat", "g_kv_lat", "rel_bias", "sinks", "g_pre_ffn",
                   "g_post_ffn", "conv_b"]
    small_w = [b_ada, g_pre_mix, g_post_mix, g_q_lat, g_kv_lat, rel_bias, sinks, g_pre_ffn, g_post_ffn, conv_b]
    small_m = [m_b_ada, m_g_pre_mix, m_g_post_mix, m_g_q_lat, m_g_kv_lat, m_rel_bias, m_sinks, m_g_pre_ffn,
               m_g_post_ffn, m_conv_b]
    small_v = [v_b_ada, v_g_pre_mix, v_g_post_mix, v_g_q_lat, v_g_kv_lat, v_rel_bias, v_sinks, v_g_pre_ffn,
               v_g_post_ffn, v_conv_b]
    small_g = [dmod, dg_pre_mix, dg_post_mix, dg_q, dg_kv, d_rel_bias, d_sinks, dg_pre_ffn, dg_post_ffn, dcb]
    n_small = sum(int(np.prod(w.shape)) for w in small_w)
    rows = _round_up(-(-n_small // LANES), 16)
    parts_small = _all_gather(_pack(small_g, rows), name="ag_small_grads", in_vmem=True)
    sg, sd, sm, sv = _adamw(_pack(small_w, rows), _pack(small_m, rows), _pack(small_v, rows), parts_small,
                            name="adamw_small")

    def unpack(packed):
        flat, out, at = packed.reshape(-1), {}, 0
        for nm, w in zip(small_names, small_w):
            n = int(np.prod(w.shape))
            out[nm] = flat[at:at + n].reshape(w.shape)
            at += n
        return out

    small_out = [unpack(t) for t in (sg, sd, sm, sv)]

    dmod_all = parts_small.reshape(N_DEV, rows * LANES)[:, :6 * D]
    dmod_cols = lax.dynamic_slice_in_dim(dmod_all, me * ada_n, ada_n, axis=1)
    ada_out = _adamw_ada(w_ada[0], m_w_ada[0], v_w_ada[0], jnp.transpose(c_act), dmod_cols, name="adamw_w_ada")

    def owner_update(got, w, m, v, name):
        shp = w.shape
        w2, m2, v2 = (t.reshape(shp[-2], shp[-1]) for t in (w, m, v))
        return [t.reshape(shp) for t in _adamw(w2, m2, v2, got, name="adamw_" + name)]

    def pad_rows(t):
        return jnp.pad(t[0], ((0, 16 - CONV_WIDTH), (0, 0)))

    big = {
        "w_in": owner_update([got_in_a, got_in_b], w_in, m_w_in, v_w_in, "w_in"),
        "w_uq": owner_update(got_uq, w_uq, m_w_uq, v_w_uq, "w_uq"),
        "w_ukv": owner_update(got_ukv, w_ukv, m_w_ukv, v_w_ukv, "w_ukv"),
        "w_o": owner_update(got_o, w_o, m_w_o, v_w_o, "w_o"),
        "w_up": owner_update(got_up, w_up, m_w_up, v_w_up, "w_up"),
        "w_down": owner_update(got_down, w_down, m_w_down, v_w_down, "w_down"),
    }
    cw_upd = _adamw(pad_rows(conv_w), pad_rows(m_conv_w), pad_rows(v_conv_w), got_cw, name="adamw_conv_w")
    big["conv_w"] = [t[:CONV_WIDTH].reshape(conv_w.shape) for t in cw_upd]
    big["w_ada"] = [t.reshape(w_ada.shape) for t in ada_out]

    order = ["w_ada", "b_ada", "g_pre_mix", "g_post_mix", "w_in", "g_q_lat", "w_uq", "g_kv_lat", "w_ukv", "rel_bias",
             "sinks", "w_o", "g_pre_ffn", "g_post_ffn", "w_up", "conv_w", "conv_b", "w_down"]
    outs = [loss, grad_x.reshape(x.shape)]
    for kind in range(4):
        for nm in order:
            outs.append(big[nm][kind] if nm in big else small_out[kind][nm])
    return tuple(outs)
```

```python
import functools
import itertools
import math

import numpy as np

import jax
import jax.numpy as jnp
from jax import lax
from jax.experimental import pallas as pl
from jax.experimental.pallas import tpu as pltpu

F32 = jnp.float32
BF16 = jnp.bfloat16

N_DEV = 8
MLA_NOPE = 128
MLA_ROPE = 64
MLA_V = 128
MLA_QK = MLA_NOPE + MLA_ROPE
MLA_QK_PAD = 256
ROPE_HALF = MLA_ROPE // 2
ROPE_THETA = 10000.0
SWA_HD = 64
SWA_KVH = 4
WINDOW = 128
BLOCK = 128
REL_BUCKETS = 32
REL_MAX_DIST = 128
CONV_WIDTH = 3
EPS = 1e-6
NEG = -1e30
ADAM_LR = 0.001
ADAM_B1 = 0.9
ADAM_B2 = 0.999
ADAM_EPS = 1e-08
ADAM_WD = 0.01
ADAM_STEP = 10
LANES = 128
HALO = 16
MESH = pl.DeviceIdType.MESH
HIGHEST = lax.Precision.HIGHEST

NN = (((1,), (0,)), ((), ()))
NT = (((1,), (1,)), ((), ()))
TN = (((0,), (0,)), ((), ()))


def _tile(n, pref, align=LANES):
    if n <= pref:
        return n
    t = (pref // align) * align
    while t >= align:
        if n % t == 0:
            return t
        t -= align
    return n


def _round_up(n, m):
    return (n + m - 1) // m * m


def _params(*sem):
    return pltpu.CompilerParams(dimension_semantics=sem)


def _sigmoid(x):
    return 1.0 / (1.0 + jnp.exp(-x))


def _my_place():
    return lax.axis_index("x"), lax.axis_index("y"), lax.axis_index("c")


def _all_gather(x, *, name, in_vmem):
    space = pltpu.VMEM if in_vmem else pl.ANY

    def body(x_ref, out_ref, send_sems, recv_sems, local_sem):
        x_, y_, c_ = _my_place()
        me, sibling = (x_, y_, c_), (x_, y_, 1 - c_)
        chips = [(1 - x_, y_), (x_, 1 - y_), (1 - x_, 1 - y_)]

        def slot(px, py, pc):
            return out_ref.at[4 * px + 2 * py + pc]

        def copy(k, block, to, src=None):
            return pltpu.make_async_remote_copy(
                src_ref=slot(*block) if src is None else src,
                dst_ref=slot(*block),
                send_sem=send_sems.at[k],
                recv_sem=recv_sems.at[k],
                device_id=to,
                device_id_type=MESH,
            )

        mine = pltpu.make_async_copy(x_ref, slot(*me), local_sem)
        mine.start()
        first = [copy(0, me, sibling, src=x_ref)]
        first += [copy(1 + j, me, (*chip, c_), src=x_ref) for j, chip in enumerate(chips)]
        for cp in first:
            cp.start()
        passed = [copy(4 + j, (*chip, c_), sibling) for j, chip in enumerate(chips)]
        for j, chip in enumerate(chips):
            copy(1 + j, (*chip, c_), me).wait_recv()
            passed[j].start()
        copy(0, sibling, me).wait_recv()
        for j, chip in enumerate(chips):
            copy(4 + j, (*chip, 1 - c_), me).wait_recv()
        for cp in first + passed:
            cp.wait_send()
        mine.wait()

    return pl.pallas_call(
        body,
        name=name,
        out_shape=jax.ShapeDtypeStruct((N_DEV,) + x.shape, x.dtype),
        in_specs=[pl.BlockSpec(memory_space=space)],
        out_specs=pl.BlockSpec(memory_space=space),
        scratch_shapes=[
            pltpu.SemaphoreType.DMA((7,)),
            pltpu.SemaphoreType.DMA((7,)),
            pltpu.SemaphoreType.DMA,
        ],
    )(x)


class _Exchange:
    def __init__(self, kind, x_ref, out_ref, send_sems, recv_sems, local_sems, t):
        x_, y_, c_ = _my_place()
        me = 4 * x_ + 2 * y_ + c_

        def pair(k, src, dst, to):
            return pltpu.make_async_remote_copy(src_ref=src, dst_ref=dst, send_sem=send_sems.at[7 * t + k],
                                                recv_sem=recv_sems.at[7 * t + k], device_id=to, device_id_type=MESH)

        none = lambda: []
        if kind == "scatter":
            peers = [(x_ ^ ((r >> 2) & 1), y_ ^ ((r >> 1) & 1), c_ ^ (r & 1)) for r in range(1, N_DEV)]
            self.at_start = lambda: [pair(k, x_ref.at[4 * px + 2 * py + pc], out_ref.at[me], (px, py, pc))
                                     for k, (px, py, pc) in enumerate(peers)]
            self.relay_after, self.at_relay = none, none
            self.arrivals = self.at_start
            self.own = lambda: pltpu.make_async_copy(x_ref.at[me], out_ref.at[me], local_sems.at[t])
        else:
            sibling = (x_, y_, 1 - c_)
            chips = list(enumerate([(1 - x_, y_), (x_, 1 - y_), (1 - x_, 1 - y_)]))

            def slot(px, py, pc):
                return out_ref.at[4 * px + 2 * py + pc]

            mine = slot(x_, y_, c_)
            self.at_start = lambda: ([pair(0, x_ref, mine, sibling)]
                                     + [pair(1 + j, x_ref, mine, (*chip, c_)) for j, chip in chips])
            self.relay_after = lambda: [pair(1 + j, slot(*chip, c_), slot(*chip, c_), (*chip, c_)) for j, chip in chips]
            self.at_relay = lambda: [pair(4 + j, slot(*chip, c_), slot(*chip, c_), sibling) for j, chip in chips]
            self.arrivals = lambda: ([pair(0, slot(*sibling), slot(*sibling), sibling)]
                                     + [pair(4 + j, slot(*chip, 1 - c_), slot(*chip, 1 - c_), sibling)
                                        for j, chip in chips])
            self.own = lambda: pltpu.make_async_copy(x_ref, mine, local_sems.at[t])

    def start(self):
        self.own().start()
        for cp in self.at_start():
            cp.start()

    def relay(self):
        for landed, onward in zip(self.relay_after(), self.at_relay()):
            landed.wait_recv()
            onward.start()

    def finish(self):
        for cp in self.arrivals():
            cp.wait_recv()
        for cp in self.at_start() + self.at_relay():
            cp.wait_send()
        self.own().wait()


RELAY_AT = 0.7


def _call(body, *, name, grid, in_specs, out_specs, out_shape, args, scratch_shapes=(), sem=(), comm=(), prefetch=()):
    n_pf = len(prefetch)

    def launch(fn, ins, outs, shapes, scratch, semantics, operands):
        spec = pltpu.PrefetchScalarGridSpec(num_scalar_prefetch=n_pf, grid=grid, in_specs=ins, out_specs=outs,
                                            scratch_shapes=scratch)
        return pl.pallas_call(fn, name=name, grid_spec=spec, out_shape=shapes,
                              compiler_params=_params(*semantics))(*prefetch, *operands)

    if not comm:
        return list(launch(body, list(in_specs), list(out_specs), list(out_shape), list(scratch_shapes), sem, args)), []
    n_in, n_out, n_c, n_s = len(in_specs), len(out_specs), len(comm), len(scratch_shapes)
    kinds = [kind for kind, _ in comm]
    hbm = pl.BlockSpec(memory_space=pl.ANY)

    def wrapped(*refs):
        tables, refs = refs[:n_pf], refs[n_pf:]
        ins, cin = refs[:n_in], refs[n_in:n_in + n_c]
        at = n_in + n_c
        outs, cout = refs[at:at + n_out], refs[at + n_out:at + n_out + n_c]
        scr = refs[at + n_out + n_c:at + n_out + n_c + n_s]
        send, recv, local = refs[-3:]
        step = 0
        for a, g in enumerate(grid):
            step = step * g + pl.program_id(a)
        n_steps = int(np.prod(grid))

        def exchanges():
            return [_Exchange(kinds[t], cin[t], cout[t], send, recv, local, t) for t in range(n_c)]

        @pl.when(step == 0)
        def _():
            for ex in exchanges():
                ex.start()

        body(*tables, *ins, *outs, *scr)

        @pl.when(step == min(int(RELAY_AT * n_steps), n_steps - 1))
        def _():
            for ex in exchanges():
                ex.relay()

        @pl.when(step == n_steps - 1)
        def _():
            for ex in exchanges():
                ex.finish()

    c_shapes = [jax.ShapeDtypeStruct(((N_DEV,) + a.shape) if kind == "gather" else a.shape, a.dtype)
                for kind, a in comm]
    sems = [pltpu.SemaphoreType.DMA((7 * n_c,)), pltpu.SemaphoreType.DMA((7 * n_c,)), pltpu.SemaphoreType.DMA((n_c,))]
    res = launch(wrapped, list(in_specs) + [hbm] * n_c, list(out_specs) + [hbm] * n_c, list(out_shape) + c_shapes,
                 list(scratch_shapes) + sems, ["arbitrary"] * len(grid), (*args, *[a for _, a in comm]))
    return list(res[:n_out]), list(res[n_out:])


def _matmul(a, b, *, mode, out_dtype, name, tm=1024, tn=1024, tk=2816, precision=None, comm=(), shard_out=False,
            halves=False, m_range=None):
    if mode == "nn":
        (M, K), (K2, N) = a.shape, b.shape
    elif mode == "nt":
        (M, K), (N, K2) = (a.shape[1], 2 * a.shape[2]) if halves else a.shape, b.shape
    else:
        (K, M), (K2, N) = a.shape, (b.shape[1], 2 * b.shape[2]) if halves else b.shape
    assert K == K2, (a.shape, b.shape, mode)
    m_off = 0
    if m_range is not None:
        m_off, M = m_range
    tm = _tile(M, tm, LANES if mode == "tn" else 16)
    tk = _tile(K // 2 if halves and mode == "nt" else K, tk)
    tn = _tile(N // N_DEV, max(tn, 1408)) if shard_out else _tile(N // 2 if halves and mode == "tn" else N, tn)
    nk = K // tk
    m_off //= tm
    if mode == "tn":
        a_spec = pl.BlockSpec((tk, tm), lambda i, j, k: (k, i + m_off))
    elif halves:
        a_spec = pl.BlockSpec((None, tm, tk), lambda i, j, k: (k // (nk // 2), i, k % (nk // 2)))
    else:
        a_spec = pl.BlockSpec((tm, tk), lambda i, j, k: (i, k))
    if mode == "nt":
        b_spec = pl.BlockSpec((tn, tk), lambda i, j, k: (j, k))
    elif halves:
        nj = N // tn
        b_spec = pl.BlockSpec((None, tk, tn), lambda i, j, k: (j // (nj // 2), k, j % (nj // 2)))
    else:
        b_spec = pl.BlockSpec((tk, tn), lambda i, j, k: (k, j))
    dn = {"nn": NN, "nt": NT, "tn": TN}[mode]
    if shard_out:
        per = N // N_DEV // tn
        o_spec = pl.BlockSpec((None, tm, tn), lambda i, j, k: (j // per, i, j % per))
        o_shape = jax.ShapeDtypeStruct((N_DEV, M, N // N_DEV), out_dtype)
    else:
        o_spec = pl.BlockSpec((tm, tn), lambda i, j, k: (i, j))
        o_shape = jax.ShapeDtypeStruct((M, N), out_dtype)

    def product(a_ref, b_ref):
        return lax.dot_general(a_ref[...], b_ref[...], dn, preferred_element_type=F32, precision=precision)

    def body_one(a_ref, b_ref, o_ref):
        o_ref[...] = product(a_ref, b_ref).astype(o_ref.dtype)

    def body_acc(a_ref, b_ref, o_ref, acc_ref):
        k = pl.program_id(2)

        @pl.when(k == 0)
        def _():
            acc_ref[...] = product(a_ref, b_ref)

        @pl.when(k > 0)
        def _():
            acc_ref[...] += product(a_ref, b_ref)

        @pl.when(k == nk - 1)
        def _():
            o_ref[...] = acc_ref[...].astype(o_ref.dtype)

    outs, moved = _call(
        body_one if nk == 1 else body_acc,
        name=name,
        grid=(M // tm, N // tn, nk),
        in_specs=[a_spec, b_spec],
        out_specs=[o_spec],
        out_shape=[o_shape],
        scratch_shapes=[] if nk == 1 else [pltpu.VMEM((tm, tn), F32)],
        sem=("parallel", "parallel", "arbitrary"),
        args=(a, b),
        comm=comm,
    )
    return (outs[0], moved) if comm else outs[0]


def _rstd(xf):
    return lax.rsqrt(jnp.mean(xf * xf, axis=-1, keepdims=True) + EPS)


def _col_view(width, off):
    assert off % width == 0
    return off // width


def _prenorm(x, g, sc, sh, *, name, off=0, width=None):
    S = x.shape[0]
    W = x.shape[1] if width is None else width
    cb = _col_view(W, off)
    tr = _tile(S, 512, 16)
    mod = sc is not None
    vec = pl.BlockSpec((1, W), lambda i: (0, 0))

    def body(*refs):
        if mod:
            x_ref, g_ref, sc_ref, sh_ref, o_ref = refs
        else:
            x_ref, g_ref, o_ref = refs
        xf = x_ref[...].astype(F32)
        y = xf * _rstd(xf) * g_ref[...]
        if mod:
            y = y * (1.0 + sc_ref[...]) + sh_ref[...]
        o_ref[...] = y.astype(o_ref.dtype)

    args = (x, g, sc, sh) if mod else (x, g)
    return pl.pallas_call(
        body,
        name=name,
        grid=(S // tr,),
        in_specs=[pl.BlockSpec((tr, W), lambda i: (i, cb))] + [vec] * (len(args) - 1),
        out_specs=pl.BlockSpec((tr, W), lambda i: (i, 0)),
        out_shape=jax.ShapeDtypeStruct((S, W), BF16),
        compiler_params=_params("parallel"),
    )(*args)


def _prenorm_bwd(x, dh, dres, g, sc, *, name, out_dtype, off=0, width=None):
    S = x.shape[0]
    W = x.shape[1] if width is None else width
    cb = _col_view(W, off)
    tr = _tile(S, 256, 16)
    mod = sc is not None
    res = dres is not None
    vec = pl.BlockSpec((1, W), lambda i: (0, 0))
    row = pl.BlockSpec((tr, W), lambda i: (i, 0))

    def body(*refs):
        it = iter(refs)
        x_ref, dh_ref = next(it), next(it)
        dres_ref = next(it) if res else None
        g_ref = next(it)
        sc_ref = next(it) if mod else None
        dx_ref, dg_ref = next(it), next(it)
        dsc_ref, dsh_ref = (next(it), next(it)) if mod else (None, None)
        i = pl.program_id(0)

        @pl.when(i == 0)
        def _():
            dg_ref[...] = jnp.zeros_like(dg_ref)
            if mod:
                dsc_ref[...] = jnp.zeros_like(dsc_ref)
                dsh_ref[...] = jnp.zeros_like(dsh_ref)

        xf = x_ref[...].astype(F32)
        r = _rstd(xf)
        xn = xf * r
        dhf = dh_ref[...].astype(F32)
        gv = g_ref[...]
        if mod:
            one_sc = 1.0 + sc_ref[...]
            dsh_ref[...] += jnp.sum(dhf, axis=0, keepdims=True)
            dsc_ref[...] += jnp.sum(dhf * (xn * gv), axis=0, keepdims=True)
            dg_ref[...] += jnp.sum(dhf * xn * one_sc, axis=0, keepdims=True)
            dxn = dhf * (gv * one_sc)
        else:
            dg_ref[...] += jnp.sum(dhf * xn, axis=0, keepdims=True)
            dxn = dhf * gv
        dx = r * (dxn - xn * jnp.mean(dxn * xn, axis=-1, keepdims=True))
        if res:
            dx = dx + dres_ref[...]
        dx_ref[...] = dx.astype(dx_ref.dtype)

    args = [x, dh] + ([dres] if res else []) + [g] + ([sc] if mod else [])
    in_specs = [pl.BlockSpec((tr, W), lambda i: (i, cb)), row] + ([row] if res else []) + [vec] + ([vec] if mod else [])
    n_vec = 3 if mod else 1
    outs = pl.pallas_call(
        body,
        name=name,
        grid=(S // tr,),
        in_specs=in_specs,
        out_specs=[row] + [vec] * n_vec,
        out_shape=[jax.ShapeDtypeStruct((S, W), out_dtype)] + [jax.ShapeDtypeStruct((1, W), F32)] * n_vec,
        compiler_params=_params("arbitrary"),
    )(*args)
    return outs


def _postnorm_res(x, y, gt, g, *, name):
    S, D = x.shape
    tr = _tile(S, 512, 8)
    row = pl.BlockSpec((tr, D), lambda i: (i, 0))
    vec = pl.BlockSpec((1, D), lambda i: (0, 0))

    def body(x_ref, y_ref, gt_ref, g_ref, o_ref):
        yf = y_ref[...]
        o_ref[...] = x_ref[...] + gt_ref[...] * (yf * _rstd(yf) * g_ref[...])

    return pl.pallas_call(
        body,
        name=name,
        grid=(S // tr,),
        in_specs=[row, row, vec, vec],
        out_specs=row,
        out_shape=jax.ShapeDtypeStruct((S, D), F32),
        compiler_params=_params("parallel"),
    )(x, y, gt, g)


def _postnorm_bwd(dx1, y, gt, g, *, name):
    S, D = y.shape
    tr = _tile(S, 256, 16)
    row = pl.BlockSpec((tr, D), lambda i: (i, 0))
    vec = pl.BlockSpec((1, D), lambda i: (0, 0))

    def body(dx_ref, y_ref, gt_ref, g_ref, dy_ref, dgt_ref, dg_ref):
        @pl.when(pl.program_id(0) == 0)
        def _():
            dgt_ref[...] = jnp.zeros_like(dgt_ref)
            dg_ref[...] = jnp.zeros_like(dg_ref)

        yf = y_ref[...]
        r = _rstd(yf)
        yn = yf * r
        d = dx_ref[...]
        gtv, gv = gt_ref[...], g_ref[...]
        dgt_ref[...] += jnp.sum(d * (yn * gv), axis=0, keepdims=True)
        dg_ref[...] += jnp.sum(d * gtv * yn, axis=0, keepdims=True)
        dyn = d * (gtv * gv)
        dy_ref[...] = (r * (dyn - yn * jnp.mean(dyn * yn, axis=-1, keepdims=True))).astype(dy_ref.dtype)

    return pl.pallas_call(
        body,
        name=name,
        grid=(S // tr,),
        in_specs=[row, row, vec, vec],
        out_specs=[row, vec, vec],
        out_shape=[jax.ShapeDtypeStruct((S, D), BF16), jax.ShapeDtypeStruct((1, D), F32),
                   jax.ShapeDtypeStruct((1, D), F32)],
        compiler_params=_params("arbitrary"),
    )(dx1, y, gt, g)


def _final_loss(x1, y, target, gt, g, *, name):
    S, D = y.shape
    tr = _tile(S, 256, 16)
    row = pl.BlockSpec((tr, D), lambda i: (i, 0))
    vec = pl.BlockSpec((1, D), lambda i: (0, 0))
    one = pl.BlockSpec((1, LANES), lambda i: (0, 0))

    def body(x_ref, y_ref, t_ref, gt_ref, g_ref, loss_ref, dout_ref, dy_ref, dgt_ref, dg_ref):
        @pl.when(pl.program_id(0) == 0)
        def _():
            loss_ref[...] = jnp.zeros_like(loss_ref)
            dgt_ref[...] = jnp.zeros_like(dgt_ref)
            dg_ref[...] = jnp.zeros_like(dg_ref)

        yf = y_ref[...]
        r = _rstd(yf)
        yn = yf * r
        gtv, gv = gt_ref[...], g_ref[...]
        out = x_ref[...] + gtv * (yn * gv)
        diff = out - t_ref[...]
        per_tok = jnp.mean(diff * diff, axis=-1, keepdims=True)
        loss_ref[...] += 0.5 * jnp.sum(per_tok, axis=0, keepdims=True)
        d = diff / D
        dout_ref[...] = d
        dgt_ref[...] += jnp.sum(d * (yn * gv), axis=0, keepdims=True)
        dg_ref[...] += jnp.sum(d * gtv * yn, axis=0, keepdims=True)
        dyn = d * (gtv * gv)
        dy_ref[...] = (r * (dyn - yn * jnp.mean(dyn * yn, axis=-1, keepdims=True))).astype(dy_ref.dtype)

    return pl.pallas_call(
        body,
        name=name,
        grid=(S // tr,),
        in_specs=[row, row, row, vec, vec],
        out_specs=[one, row, row, vec, vec],
        out_shape=[jax.ShapeDtypeStruct((1, LANES), F32), jax.ShapeDtypeStruct((S, D), F32),
                   jax.ShapeDtypeStruct((S, D), BF16), jax.ShapeDtypeStruct((1, D), F32),
                   jax.ShapeDtypeStruct((1, D), F32)],
        compiler_params=_params("arbitrary"),
    )(x1, y, target, gt, g)


def _ada_fwd(c_all, w_local, b_cols, *, name):
    B, D = c_all.shape
    N = w_local.shape[1]
    tn = _tile(N, 512)

    def body(c_ref, w_ref, b_ref, ca_ref, mod_ref):
        cv = c_ref[...]
        ca = cv * _sigmoid(cv)
        ca_ref[...] = ca
        mod_ref[...] = jnp.dot(ca, w_ref[...], preferred_element_type=F32, precision=HIGHEST) + b_ref[...]

    return pl.pallas_call(
        body,
        name=name,
        grid=(N // tn,),
        in_specs=[pl.BlockSpec((B, D), lambda j: (0, 0)), pl.BlockSpec((D, tn), lambda j: (0, j)),
                  pl.BlockSpec((1, tn), lambda j: (0, j))],
        out_specs=[pl.BlockSpec((B, D), lambda j: (0, 0)), pl.BlockSpec((B, tn), lambda j: (0, j))],
        out_shape=[jax.ShapeDtypeStruct((B, D), F32), jax.ShapeDtypeStruct((B, N), F32)],
        compiler_params=_params("arbitrary"),
    )(c_all, w_local, b_cols)


def _rope_tables(S, width, lane_off):
    pos = jnp.arange(S, dtype=F32)
    inv = ROPE_THETA ** (-jnp.arange(0, MLA_ROPE, 2, dtype=F32) / MLA_ROPE)
    ang = pos[:, None] * inv[None, :]
    ang = jnp.concatenate([ang, ang], axis=-1)
    cos, sin = jnp.cos(ang), jnp.sin(ang)
    first = (jnp.arange(MLA_ROPE) < ROPE_HALF)[None, :]
    sa = jnp.where(first, -sin, 0.0)
    sb = jnp.where(first, 0.0, sin)

    def place(t, fill):
        return jnp.pad(t, ((0, 0), (lane_off, width - lane_off - MLA_ROPE)), constant_values=fill)

    return place(cos, 1.0), place(sa, 0.0), place(sb, 0.0)


def _rope_apply(x, cos, sa, sb, width, transpose):
    if transpose:
        return x * cos + pltpu.roll(x * sa, ROPE_HALF, 1) + pltpu.roll(x * sb, width - ROPE_HALF, 1)
    return x * cos + pltpu.roll(x, width - ROPE_HALF, 1) * sa + pltpu.roll(x, ROPE_HALF, 1) * sb


def _rope(x, tables, *, heads, width, transpose, name, off=0, scale=1.0):
    S = x.shape[0]
    cb = _col_view(width, off)
    tr = _tile(S, 512, 16)
    tab = pl.BlockSpec((tr, width), lambda i, h: (i, 0))

    def body(x_ref, c_ref, sa_ref, sb_ref, o_ref):
        y = _rope_apply(x_ref[...].astype(F32), c_ref[...], sa_ref[...], sb_ref[...], width, transpose)
        o_ref[...] = (y if scale == 1.0 else y * scale).astype(o_ref.dtype)

    return pl.pallas_call(
        body,
        name=name,
        grid=(S // tr, heads),
        in_specs=[pl.BlockSpec((tr, width), lambda i, h: (i, cb + h)), tab, tab, tab],
        out_specs=pl.BlockSpec((tr, width), lambda i, h: (i, h)),
        out_shape=jax.ShapeDtypeStruct((S, heads * width), BF16),
        compiler_params=_params("parallel", "parallel"),
    )(x, *tables)


def _shared_rope_grad(parts, tables, *, name):
    P, S, _ = parts.shape
    tr = _tile(S, 512, 16)
    tab = pl.BlockSpec((tr, LANES), lambda i: (i, 0))

    def body(p_ref, c_ref, sa_ref, sb_ref, o_ref):
        acc = p_ref[0]
        for k in range(1, P):
            acc = acc + p_ref[k]
        o_ref[...] = _rope_apply(acc, c_ref[...], sa_ref[...], sb_ref[...], LANES, True).astype(o_ref.dtype)

    return pl.pallas_call(
        body,
        name=name,
        grid=(S // tr,),
        in_specs=[pl.BlockSpec((P, tr, LANES), lambda i: (0, i, 0)), tab, tab, tab],
        out_specs=tab,
        out_shape=jax.ShapeDtypeStruct((S, LANES), BF16),
        compiler_params=_params("parallel"),
    )(parts, *tables)


MLA_SCALE = MLA_QK ** -0.5
LOG2E = math.log2(math.e)
LN2 = math.log(2.0)
MLA_Q_PRESCALE = MLA_SCALE * LOG2E


def _lane_tile(v, n):
    return v if n == LANES else jnp.tile(v, (1, n // LANES))


def _causal_mask(s):
    rows = lax.broadcasted_iota(jnp.int32, s.shape, 0)
    cols = lax.broadcasted_iota(jnp.int32, s.shape, 1)
    return jnp.where(cols <= rows, s, NEG)


def _tri_blocks(nb, q_major):
    if q_major:
        pairs = [(q, k) for q in range(nb) for k in range(q + 1)]
    else:
        pairs = [(q, k) for k in range(nb) for q in range(k, nb)]
    return (jnp.asarray(np.array([p[0] for p in pairs], np.int32)),
            jnp.asarray(np.array([p[1] for p in pairs], np.int32)))


HEAD_PAIR = 4


def _flash_fwd(q_raw, KV, krr, tables, *, heads, name, comm=()):
    S = q_raw.shape[0]
    t = _tile(S, 512)
    nb = S // t
    qt, kt = _tri_blocks(nb, True)
    qw, vw = HEAD_PAIR * MLA_QK_PAD, HEAD_PAIR * MLA_V

    def body(qt_ref, kt_ref, q_ref, *rest):
        kn_refs, kr_ref, v_refs = rest[:HEAD_PAIR], rest[HEAD_PAIR], rest[HEAD_PAIR + 1:2 * HEAD_PAIR + 1]
        c_ref, sa_ref, sb_ref, o_ref, lse_ref, qr_ref, m_scr, l_scr, acc_scr = rest[2 * HEAD_PAIR + 1:]
        step_id = pl.program_id(1)
        qi, ki = qt_ref[step_id], kt_ref[step_id]

        @pl.when(ki == 0)
        def _():
            m_scr[...] = jnp.full_like(m_scr, NEG)
            l_scr[...] = jnp.zeros_like(l_scr)
            acc_scr[...] = jnp.zeros_like(acc_scr)
            for h in range(HEAD_PAIR):
                base = h * MLA_QK_PAD
                nope = q_ref[:, base:base + MLA_NOPE].astype(F32) * MLA_Q_PRESCALE
                rot = _rope_apply(q_ref[:, base + MLA_NOPE:base + MLA_QK_PAD].astype(F32), c_ref[...], sa_ref[...],
                                  sb_ref[...], LANES, False) * MLA_Q_PRESCALE
                qr_ref[:, base:base + MLA_NOPE] = nope.astype(qr_ref.dtype)
                qr_ref[:, base + MLA_NOPE:base + MLA_QK_PAD] = rot.astype(qr_ref.dtype)

        def step(diagonal):
            for h, (kn_ref, v_ref) in enumerate(zip(kn_refs, v_refs)):
                cols = slice(h * MLA_QK_PAD, (h + 1) * MLA_QK_PAD)
                k = jnp.concatenate([kn_ref[...], kr_ref[...]], axis=1)
                s = lax.dot_general(qr_ref[:, cols], k, NT, preferred_element_type=F32)
                if diagonal:
                    s = _causal_mask(s)
                m_prev = m_scr[h]
                m_new = jnp.maximum(m_prev, jnp.max(s, axis=1, keepdims=True))
                alpha = jnp.exp2(m_prev - m_new)
                p = jnp.exp2(s - _lane_tile(m_new, t))
                l_new = alpha * l_scr[h] + jnp.sum(p, axis=1, keepdims=True)
                acc = alpha * acc_scr[h] + jnp.dot(p.astype(BF16), v_ref[...], preferred_element_type=F32)
                if diagonal:
                    o_ref[:, h * MLA_V:(h + 1) * MLA_V] = (acc / l_new).astype(o_ref.dtype)
                    lse_ref[h] = m_new + jnp.log(l_new) * LOG2E
                else:
                    l_scr[h], acc_scr[h], m_scr[h] = l_new, acc, m_new

        pl.when(ki < qi)(lambda: step(False))
        pl.when(ki == qi)(lambda: step(True))

    def kvspec(h, half):
        return pl.BlockSpec((t, LANES), lambda hp, s, qt, kt: (kt[s], 2 * (HEAD_PAIR * hp + h) + half))

    qtab = pl.BlockSpec((t, LANES), lambda hp, s, qt, kt: (qt[s], 0))
    qrow = lambda hp, s, qt, kt: (qt[s], hp)
    return _call(
        body,
        name=name,
        grid=(heads // HEAD_PAIR, int(qt.shape[0])),
        in_specs=[pl.BlockSpec((t, qw), qrow), *[kvspec(h, 0) for h in range(HEAD_PAIR)],
                  pl.BlockSpec((t, LANES), lambda hp, s, qt, kt: (kt[s], 0)),
                  *[kvspec(h, 1) for h in range(HEAD_PAIR)], qtab, qtab, qtab],
        out_specs=[pl.BlockSpec((t, vw), qrow),
                   pl.BlockSpec((HEAD_PAIR, t, LANES), lambda hp, s, qt, kt: (hp, qt[s], 0)),
                   pl.BlockSpec((t, qw), qrow)],
        out_shape=[jax.ShapeDtypeStruct((S, heads * MLA_V), BF16),
                   jax.ShapeDtypeStruct((heads, S, LANES), F32),
                   jax.ShapeDtypeStruct((S, heads * MLA_QK_PAD), BF16)],
        scratch_shapes=[pltpu.VMEM((HEAD_PAIR, t, LANES), F32), pltpu.VMEM((HEAD_PAIR, t, LANES), F32),
                        pltpu.VMEM((HEAD_PAIR, t, MLA_V), F32)],
        sem=("parallel", "arbitrary"),
        args=(q_raw, *[KV] * HEAD_PAIR, krr, *[KV] * HEAD_PAIR, *tables),
        comm=comm,
        prefetch=(qt, kt),
    )


def _flash_bwd(Q, KV, krr, dO, O, lse, tables, *, heads, name, comm=()):
    S = Q.shape[0]
    t = _tile(S, 512)
    nb = S // t
    qt, kt = _tri_blocks(nb, False)
    n_steps = int(qt.shape[0])
    qw, vw = HEAD_PAIR * MLA_QK_PAD, HEAD_PAIR * MLA_V

    def body(qt_ref, kt_ref, q_ref, *rest):
        kn_refs, kr_ref, v_refs = rest[:HEAD_PAIR], rest[HEAD_PAIR], rest[HEAD_PAIR + 1:2 * HEAD_PAIR + 1]
        (do_ref, o_ref, lse_ref, c_ref, sa_ref, sb_ref, dq_ref, dkv_ref, dkr_ref,
         dq_scr, dk_scr, dv_scr, delta_scr) = rest[2 * HEAD_PAIR + 1:]
        step_id = pl.program_id(1)
        qi, ki = qt_ref[step_id], kt_ref[step_id]

        @pl.when(ki == 0)
        def _():
            for h in range(HEAD_PAIR):
                vc = slice(h * MLA_V, (h + 1) * MLA_V)
                d = jnp.sum(do_ref[:, vc].astype(F32) * o_ref[:, vc].astype(F32), axis=1, keepdims=True)
                delta_scr[h, qi] = jnp.broadcast_to(d, (t, LANES))

        def step(diagonal):
            for h, (kn_ref, v_ref) in enumerate(zip(kn_refs, v_refs)):
                base = h * MLA_QK_PAD
                cols = slice(base, base + MLA_QK_PAD)
                vc = slice(h * MLA_V, (h + 1) * MLA_V)
                q, do = q_ref[:, cols], do_ref[:, vc]
                k = jnp.concatenate([kn_ref[...], kr_ref[...]], axis=1)
                s = lax.dot_general(q, k, NT, preferred_element_type=F32)
                if diagonal:
                    s = _causal_mask(s)
                p = jnp.exp2(s - _lane_tile(lse_ref[h], t))
                dv = lax.dot_general(p.astype(BF16), do, TN, preferred_element_type=F32)
                dp = lax.dot_general(do, v_ref[...], NT, preferred_element_type=F32)
                ds = (p * (dp - _lane_tile(delta_scr[h, qi], t))).astype(BF16)
                dk = lax.dot_general(ds, q, TN, preferred_element_type=F32)
                dq = jnp.dot(ds, k, preferred_element_type=F32)
                if diagonal:
                    dk_scr[h], dv_scr[h] = dk, dv
                    dq = (dq_scr[qi, :, cols] + dq) * (LN2 * MLA_Q_PRESCALE)
                    rot = _rope_apply(dq[:, MLA_NOPE:], c_ref[...], sa_ref[...], sb_ref[...], LANES, True)
                    dq_ref[:, base:base + MLA_NOPE] = dq[:, :MLA_NOPE].astype(dq_ref.dtype)
                    dq_ref[:, base + MLA_NOPE:base + MLA_QK_PAD] = rot.astype(dq_ref.dtype)
                else:
                    dk_scr[h] += dk
                    dv_scr[h] += dv
                    dq_scr[qi, :, cols] += dq

        @pl.when(ki == 0)
        def _():
            dq_scr[qi] = jnp.zeros((t, qw), F32)

        pl.when(qi > ki)(lambda: step(False))
        pl.when(qi == ki)(lambda: step(True))

        @pl.when(qi == nb - 1)
        def _():
            shared = jnp.zeros((t, LANES), F32)
            for h in range(HEAD_PAIR):
                base = h * MLA_QK_PAD
                dk = dk_scr[h] * LN2
                dkv_ref[:, base:base + MLA_NOPE] = dk[:, :MLA_NOPE].astype(dkv_ref.dtype)
                dkv_ref[:, base + MLA_NOPE:base + MLA_QK_PAD] = dv_scr[h].astype(dkv_ref.dtype)
                shared = shared + dk[:, MLA_NOPE:]
            dkr_ref[0] = shared

    def kvspec(h, half):
        return pl.BlockSpec((t, LANES), lambda hp, s, qt, kt: (kt[s], 2 * (HEAD_PAIR * hp + h) + half))

    qrow = lambda hp, s, qt, kt: (qt[s], hp)
    krow = lambda hp, s, qt, kt: (kt[s], hp)
    ktab = pl.BlockSpec((t, LANES), lambda hp, s, qt, kt: (kt[s], 0))
    return _call(
        body,
        name=name,
        grid=(heads // HEAD_PAIR, n_steps),
        in_specs=[pl.BlockSpec((t, qw), qrow), *[kvspec(h, 0) for h in range(HEAD_PAIR)], ktab,
                  *[kvspec(h, 1) for h in range(HEAD_PAIR)],
                  pl.BlockSpec((t, vw), qrow), pl.BlockSpec((t, vw), qrow),
                  pl.BlockSpec((HEAD_PAIR, t, LANES), lambda hp, s, qt, kt: (hp, qt[s], 0)),
                  ktab, ktab, ktab],
        out_specs=[pl.BlockSpec((t, qw), krow), pl.BlockSpec((t, qw), krow),
                   pl.BlockSpec((1, t, LANES), lambda hp, s, qt, kt: (hp, kt[s], 0))],
        out_shape=[jax.ShapeDtypeStruct((S, heads * MLA_QK_PAD), BF16),
                   jax.ShapeDtypeStruct((S, heads * MLA_QK_PAD), BF16),
                   jax.ShapeDtypeStruct((heads // HEAD_PAIR, S, LANES), F32)],
        scratch_shapes=[pltpu.VMEM((nb, t, qw), F32), pltpu.VMEM((HEAD_PAIR, t, MLA_QK_PAD), F32),
                        pltpu.VMEM((HEAD_PAIR, t, MLA_V), F32), pltpu.VMEM((HEAD_PAIR, nb, t, LANES), F32)],
        sem=("parallel", "arbitrary"),
        args=(Q, *[KV] * HEAD_PAIR, krr, *[KV] * HEAD_PAIR, dO, O, lse, *tables),
        comm=comm,
        prefetch=(qt, kt),
    )


SWA_SCALE = SWA_HD ** -0.5


def _t5_bucket_table():
    a = np.arange(BLOCK)[:, None]
    j = np.arange(2 * BLOCK)[None, :]
    dist = BLOCK + a - j
    max_exact = REL_BUCKETS // 2
    n = np.maximum(dist, 0)
    large = max_exact + (np.log(np.maximum(n, 1).astype(np.float32) / np.float32(max_exact))
                         / np.float32(math.log(REL_MAX_DIST / max_exact))
                         * np.float32(REL_BUCKETS - max_exact)).astype(np.int32)
    large = np.minimum(large, REL_BUCKETS - 1)
    bucket = np.where(n < max_exact, n, large)
    valid = (dist >= 0) & (dist < WINDOW)
    return bucket.astype(np.int32), valid


def _swa_probs(q_ref, kp_ref, kc_ref, bias_ref, sink_ref, qb, G):
    q2 = q_ref[...].reshape(G * BLOCK, SWA_HD)
    kb = jnp.concatenate([kp_ref[0], kc_ref[0]], axis=0)
    s = lax.dot_general(kb, q2, NT, preferred_element_type=F32) * SWA_SCALE + bias_ref[0]
    keys = lax.broadcasted_iota(jnp.int32, s.shape, 0)
    s = jnp.where((keys >= BLOCK) | (qb > 0), s, NEG)
    sink = sink_ref[0]
    m = jnp.maximum(jnp.max(s, axis=0, keepdims=True), sink)
    e = jnp.exp(s - m)
    es = jnp.exp(sink - m)
    inv = 1.0 / (jnp.sum(e, axis=0, keepdims=True) + es)
    return q2, kb, e * inv, es * inv


def _swa_fwd(q, k, v, bias_t, sink, *, name, comm=()):
    H, S, _ = q.shape
    G = H // SWA_KVH
    nb = S // BLOCK
    cur = lambda kh, qb: (kh, qb, 0)
    prev = lambda kh, qb: (kh, jnp.maximum(qb - 1, 0), 0)
    kvspec = lambda im: pl.BlockSpec((1, BLOCK, SWA_HD), im)

    def body(q_ref, kc_ref, kp_ref, vc_ref, vp_ref, bias_ref, sink_ref, o_ref):
        qb = pl.program_id(1)
        _, _, pt, _ = _swa_probs(q_ref, kp_ref, kc_ref, bias_ref, sink_ref, qb, G)
        vb = jnp.concatenate([vp_ref[0], vc_ref[0]], axis=0)
        o_ref[0, 0] = lax.dot_general(vb, pt.astype(BF16), TN, preferred_element_type=F32).astype(o_ref.dtype)

    outs, moved = _call(
        body,
        name=name,
        grid=(SWA_KVH, nb),
        in_specs=[pl.BlockSpec((G, BLOCK, SWA_HD), cur), kvspec(cur), kvspec(prev), kvspec(cur), kvspec(prev),
                  pl.BlockSpec((1, 2 * BLOCK, G * BLOCK), lambda kh, qb: (kh, 0, 0)),
                  pl.BlockSpec((1, 1, G * BLOCK), lambda kh, qb: (kh, 0, 0))],
        out_specs=[pl.BlockSpec((1, 1, SWA_HD, G * BLOCK), lambda kh, qb: (kh, qb, 0, 0))],
        out_shape=[jax.ShapeDtypeStruct((SWA_KVH, nb, SWA_HD, G * BLOCK), BF16)],
        sem=("parallel", "parallel"),
        args=(q, k, k, v, v, bias_t, sink),
        comm=comm,
    )
    return outs[0], moved


def _swa_bwd(q, k, v, bias_t, sink, do, *, name, comm=()):
    H, S, _ = q.shape
    G = H // SWA_KVH
    nb = S // BLOCK
    cur = lambda kh, qb: (kh, jnp.minimum(qb, nb - 1), 0)
    prev = lambda kh, qb: (kh, jnp.maximum(jnp.minimum(qb, nb - 1) - 1, 0), 0)
    lag = lambda kh, qb: (kh, jnp.maximum(qb - 1, 0), 0)
    kvspec = lambda im: pl.BlockSpec((1, BLOCK, SWA_HD), im)

    def body(q_ref, kc_ref, kp_ref, vc_ref, vp_ref, bias_ref, sink_ref, do_ref,
             dq_ref, dk_ref, dv_ref, dbias_ref, dsink_ref, ck_scr, cv_scr):
        qb = pl.program_id(1)

        @pl.when(qb == 0)
        def _():
            dbias_ref[...] = jnp.zeros_like(dbias_ref)
            dsink_ref[...] = jnp.zeros_like(dsink_ref)
            ck_scr[...] = jnp.zeros_like(ck_scr)
            cv_scr[...] = jnp.zeros_like(cv_scr)

        @pl.when(qb < nb)
        def _():
            q2, kb, pt, ps = _swa_probs(q_ref, kp_ref, kc_ref, bias_ref, sink_ref, qb, G)
            vb = jnp.concatenate([vp_ref[0], vc_ref[0]], axis=0)
            do2 = do_ref[...].reshape(G * BLOCK, SWA_HD)
            dpt = lax.dot_general(vb, do2, NT, preferred_element_type=F32)
            delta = jnp.sum(dpt * pt, axis=0, keepdims=True)
            dst = pt * (dpt - delta)
            dbias_ref[0] += dst
            dsink_ref[0] += -ps * delta
            dsb = (dst * SWA_SCALE).astype(BF16)
            dq_ref[0, 0] = lax.dot_general(kb, dsb, TN, preferred_element_type=F32).astype(dq_ref.dtype)
            dkb = jnp.dot(dsb, q2, preferred_element_type=F32)
            dvb = jnp.dot(pt.astype(BF16), do2, preferred_element_type=F32)
            dk_ref[0] = (ck_scr[...] + dkb[:BLOCK]).astype(dk_ref.dtype)
            dv_ref[0] = (cv_scr[...] + dvb[:BLOCK]).astype(dv_ref.dtype)
            ck_scr[...] = dkb[BLOCK:]
            cv_scr[...] = dvb[BLOCK:]

        @pl.when(qb == nb)
        def _():
            dk_ref[0] = ck_scr[...].astype(dk_ref.dtype)
            dv_ref[0] = cv_scr[...].astype(dv_ref.dtype)

    tspec = pl.BlockSpec((1, 1, SWA_HD, G * BLOCK), lambda kh, qb: (kh, jnp.minimum(qb, nb - 1), 0, 0))
    return _call(
        body,
        name=name,
        grid=(SWA_KVH, nb + 1),
        in_specs=[pl.BlockSpec((G, BLOCK, SWA_HD), cur), kvspec(cur), kvspec(prev), kvspec(cur), kvspec(prev),
                  pl.BlockSpec((1, 2 * BLOCK, G * BLOCK), lambda kh, qb: (kh, 0, 0)),
                  pl.BlockSpec((1, 1, G * BLOCK), lambda kh, qb: (kh, 0, 0)),
                  pl.BlockSpec((G, BLOCK, SWA_HD), cur)],
        out_specs=[tspec, kvspec(lag), kvspec(lag),
                   pl.BlockSpec((1, 2 * BLOCK, G * BLOCK), lambda kh, qb: (kh, 0, 0)),
                   pl.BlockSpec((1, 1, G * BLOCK), lambda kh, qb: (kh, 0, 0))],
        out_shape=[jax.ShapeDtypeStruct((SWA_KVH, nb, SWA_HD, G * BLOCK), BF16),
                   jax.ShapeDtypeStruct((SWA_KVH, S, SWA_HD), BF16),
                   jax.ShapeDtypeStruct((SWA_KVH, S, SWA_HD), BF16),
                   jax.ShapeDtypeStruct((SWA_KVH, 2 * BLOCK, G * BLOCK), F32),
                   jax.ShapeDtypeStruct((SWA_KVH, 1, G * BLOCK), F32)],
        scratch_shapes=[pltpu.VMEM((BLOCK, SWA_HD), F32), pltpu.VMEM((BLOCK, SWA_HD), F32)],
        sem=("parallel", "arbitrary"),
        args=(q, k, k, v, v, bias_t, sink, do),
        comm=comm,
    )


def _gate_mix(z, o_a, o_b, *, D, off_a, off_b, name):
    S = z.shape[0]
    tr = _tile(S, 256, 16)
    row = pl.BlockSpec((tr, D), lambda i: (i, 0))
    ca, cb = _col_view(D, off_a), _col_view(D, off_b)

    def body(ga_ref, gb_ref, oa_ref, ob_ref, m_ref):
        m = (_sigmoid(ga_ref[...].astype(F32)) * oa_ref[...].astype(F32)
             + _sigmoid(gb_ref[...].astype(F32)) * ob_ref[...].astype(F32))
        m_ref[...] = m.astype(m_ref.dtype)

    return pl.pallas_call(
        body,
        name=name,
        grid=(S // tr,),
        in_specs=[pl.BlockSpec((tr, D), lambda i: (i, ca)), pl.BlockSpec((tr, D), lambda i: (i, cb)), row, row],
        out_specs=row,
        out_shape=jax.ShapeDtypeStruct((S, D), BF16),
        compiler_params=_params("parallel"),
    )(z, z, o_a, o_b)


def _gate_mix_bwd(dm, z, o_a, o_b, *, D, off_a, off_b, name):
    S = z.shape[0]
    tr = _tile(S, 256, 16)
    row = pl.BlockSpec((tr, D), lambda i: (i, 0))
    ca, cb = _col_view(D, off_a), _col_view(D, off_b)

    def body(dm_ref, ga_ref, gb_ref, oa_ref, ob_ref, dga_ref, dgb_ref, doa_ref, dob_ref):
        d = dm_ref[...].astype(F32)
        for g_ref, o_ref, dg_ref, do_ref in ((ga_ref, oa_ref, dga_ref, doa_ref), (gb_ref, ob_ref, dgb_ref, dob_ref)):
            sg = _sigmoid(g_ref[...].astype(F32))
            dg_ref[...] = (d * o_ref[...].astype(F32) * (sg * (1.0 - sg))).astype(dg_ref.dtype)
            do_ref[...] = (d * sg).astype(do_ref.dtype)

    return pl.pallas_call(
        body,
        name=name,
        grid=(S // tr,),
        in_specs=[row, pl.BlockSpec((tr, D), lambda i: (i, ca)), pl.BlockSpec((tr, D), lambda i: (i, cb)), row, row],
        out_specs=[row] * 4,
        out_shape=[jax.ShapeDtypeStruct((S, D), BF16)] * 4,
        compiler_params=_params("parallel"),
    )(dm, z, z, o_a, o_b)


CONV_ROWS = 256
CONV_COLS = 1408
SUBLANES = 8


def _shift_matrices(tr):
    r = np.arange(tr)[:, None]
    c = np.arange(tr)[None, :]
    back = [jnp.asarray(r == c + d, dtype=BF16) for d in (1, 2)]
    ahead = [jnp.asarray(r + d == c, dtype=BF16) for d in (1, 2)]
    return back, ahead


def _rows_before(x, halo_ref, first, b1_ref, b2_ref):
    s1 = jnp.dot(b1_ref[...], x, preferred_element_type=F32)
    s2 = jnp.dot(b2_ref[...], x, preferred_element_type=F32)
    h8 = jnp.where(first, 0.0, halo_ref[...].astype(F32)[HALO - SUBLANES:])
    rows = lax.broadcasted_iota(jnp.int32, h8.shape, 0)
    fix1 = jnp.where(rows < 1, pltpu.roll(h8, 1, 0), 0.0)
    fix2 = jnp.where(rows < 2, pltpu.roll(h8, 2, 0), 0.0)
    s1 = jnp.concatenate([s1[:SUBLANES] + fix1, s1[SUBLANES:]], axis=0)
    s2 = jnp.concatenate([s2[:SUBLANES] + fix2, s2[SUBLANES:]], axis=0)
    return s1, s2


def _conv_taps(x, s1, s2, cw_ref, cb_ref):
    return cb_ref[...] + cw_ref[0:1, :] * s2 + cw_ref[1:2, :] * s1 + cw_ref[2:3, :] * x


def _conv_gate(up, cw, cb, *, name):
    S, F2 = up.shape
    F = F2 // 2
    tr = _tile(S, CONV_ROWS, HALO)
    tc = _tile(F, CONV_COLS)
    nc = F // tc
    hb = tr // HALO
    back, _ = _shift_matrices(tr)
    mat = pl.BlockSpec((tr, tr), lambda i, j: (0, 0))

    def halo_map(shift):
        return lambda i, j: (jnp.maximum(i * hb - 1, 0), j + shift)

    def body(x1_ref, h1_ref, x2_ref, h2_ref, cw1_ref, cw2_ref, cb1_ref, cb2_ref, b1_ref, b2_ref, a_ref):
        first = pl.program_id(0) == 0
        us = []
        for x_ref, h_ref, cw_ref, cb_ref in ((x1_ref, h1_ref, cw1_ref, cb1_ref), (x2_ref, h2_ref, cw2_ref, cb2_ref)):
            x = x_ref[...]
            s1, s2 = _rows_before(x, h_ref, first, b1_ref, b2_ref)
            us.append(_conv_taps(x.astype(F32), s1, s2, cw_ref, cb_ref))
        u1, u2 = us
        a_ref[...] = (u1 * _sigmoid(u1) * u2).astype(a_ref.dtype)

    return pl.pallas_call(
        body,
        name=name,
        grid=(S // tr, nc),
        in_specs=[pl.BlockSpec((tr, tc), lambda i, j: (i, j)), pl.BlockSpec((HALO, tc), halo_map(0)),
                  pl.BlockSpec((tr, tc), lambda i, j: (i, j + nc)), pl.BlockSpec((HALO, tc), halo_map(nc)),
                  pl.BlockSpec((CONV_WIDTH, tc), lambda i, j: (0, j)),
                  pl.BlockSpec((CONV_WIDTH, tc), lambda i, j: (0, j + nc)),
                  pl.BlockSpec((1, tc), lambda i, j: (0, j)), pl.BlockSpec((1, tc), lambda i, j: (0, j + nc)),
                  mat, mat],
        out_specs=pl.BlockSpec((tr, tc), lambda i, j: (i, j)),
        out_shape=jax.ShapeDtypeStruct((S, F), BF16),
        compiler_params=_params("parallel", "parallel"),
    )(up, up, up, up, cw, cw, cb, cb, *back)


def _conv_gate_bwd(up, da, cw, cb, *, name, comm=()):
    S, F2 = up.shape
    F = F2 // 2
    tr = _tile(S, CONV_ROWS, HALO)
    tc = _tile(F, CONV_COLS)
    nc = F // tc
    hb = tr // HALO
    ni = S // tr
    back, ahead = _shift_matrices(tr)
    mat = pl.BlockSpec((tr, tr), lambda j, r: (0, 0))

    def cur(shift):
        return lambda j, r: (ni - 1 - r, j + shift)

    def before(shift):
        return lambda j, r: (jnp.maximum((ni - 1 - r) * hb - 1, 0), j + shift)

    def vec(rows, shift):
        return pl.BlockSpec((rows, tc), lambda j, r: (0, j + shift))

    def body(x1_ref, h1_ref, x2_ref, h2_ref, da_ref, cw1_ref, cw2_ref, cb1_ref, cb2_ref,
             b1_ref, b2_ref, a1_ref, a2_ref, dup_ref, dcw_ref, dcb_ref, next_du):
        r = pl.program_id(1)
        first = r == ni - 1

        @pl.when(r == 0)
        def _():
            dcw_ref[...] = jnp.zeros_like(dcw_ref)
            dcb_ref[...] = jnp.zeros_like(dcb_ref)
            next_du[...] = jnp.zeros_like(next_du)

        x1, x2 = x1_ref[...], x2_ref[...]
        x1f, x2f = x1.astype(F32), x2.astype(F32)
        s11, s12 = _rows_before(x1, h1_ref, first, b1_ref, b2_ref)
        s21, s22 = _rows_before(x2, h2_ref, first, b1_ref, b2_ref)
        u1 = _conv_taps(x1f, s11, s12, cw1_ref, cb1_ref)
        u2 = _conv_taps(x2f, s21, s22, cw2_ref, cb2_ref)
        sg = _sigmoid(u1)
        daf = da_ref[...].astype(F32)
        du1 = daf * u2 * (sg * (1.0 + u1 * (1.0 - sg)))
        du2 = daf * (u1 * sg)
        rows = lax.broadcasted_iota(jnp.int32, (SUBLANES, tc), 0)

        for half, (du, own, own1, own2, cw_ref) in enumerate(((du1, x1f, s11, s12, cw1_ref),
                                                             (du2, x2f, s21, s22, cw2_ref))):
            du_b = du.astype(BF16)
            n1 = jnp.dot(a1_ref[...], du_b, preferred_element_type=F32)
            n2 = jnp.dot(a2_ref[...], du_b, preferred_element_type=F32)
            c8 = next_du[half]
            fix1 = jnp.where(rows >= SUBLANES - 1, pltpu.roll(c8, SUBLANES - 1, 0), 0.0)
            fix2 = jnp.where(rows >= SUBLANES - 2, pltpu.roll(c8, SUBLANES - 2, 0), 0.0)
            n1 = jnp.concatenate([n1[:tr - SUBLANES], n1[tr - SUBLANES:] + fix1], axis=0)
            n2 = jnp.concatenate([n2[:tr - SUBLANES], n2[tr - SUBLANES:] + fix2], axis=0)
            dup = cw_ref[2:3, :] * du + cw_ref[1:2, :] * n1 + cw_ref[0:1, :] * n2
            dup_ref[half] = dup.astype(dup_ref.dtype)
            dcb_ref[half] += jnp.sum(du, axis=0, keepdims=True)
            for tap, shifted in enumerate((own2, own1, own)):
                dcw_ref[half, tap:tap + 1, :] += jnp.sum(du * shifted, axis=0, keepdims=True)
            next_du[half] = du[:SUBLANES].astype(BF16).astype(F32)

    return _call(
        body,
        name=name,
        grid=(nc, ni),
        in_specs=[pl.BlockSpec((tr, tc), cur(0)), pl.BlockSpec((HALO, tc), before(0)),
                  pl.BlockSpec((tr, tc), cur(nc)), pl.BlockSpec((HALO, tc), before(nc)),
                  pl.BlockSpec((tr, tc), cur(0)),
                  vec(CONV_WIDTH, 0), vec(CONV_WIDTH, nc), vec(1, 0), vec(1, nc), mat, mat, mat, mat],
        out_specs=[pl.BlockSpec((2, tr, tc), lambda j, r: (0, ni - 1 - r, j)),
                   pl.BlockSpec((2, CONV_WIDTH, tc), lambda j, r: (0, 0, j)),
                   pl.BlockSpec((2, 1, tc), lambda j, r: (0, 0, j))],
        out_shape=[jax.ShapeDtypeStruct((2, S, F), BF16), jax.ShapeDtypeStruct((2, CONV_WIDTH, F), F32),
                   jax.ShapeDtypeStruct((2, 1, F), F32)],
        scratch_shapes=[pltpu.VMEM((2, SUBLANES, tc), F32)],
        sem=("parallel", "arbitrary"),
        args=(up, up, up, up, da, cw, cw, cb, cb, *back, *ahead),
        comm=comm,
    )


def _adam_math(w, g, m, v):
    m = ADAM_B1 * m + (1.0 - ADAM_B1) * g
    v = ADAM_B2 * v + (1.0 - ADAM_B2) * (g * g)
    m_hat = m / (1.0 - ADAM_B1 ** ADAM_STEP)
    v_hat = v / (1.0 - ADAM_B2 ** ADAM_STEP)
    delta = -ADAM_LR * (m_hat / (jnp.sqrt(v_hat) + ADAM_EPS) + ADAM_WD * w)
    return delta, m, v


def _adamw(w, m, v, parts, *, name):
    R, C = w.shape
    plist = list(parts) if isinstance(parts, (list, tuple)) else [parts]
    tr = _tile(min(p.shape[1] for p in plist), 256, 16)
    assert sum(p.shape[1] for p in plist) == R and all(p.shape[1] % tr == 0 for p in plist)
    row = pl.BlockSpec((tr, C), lambda i: (i, 0))
    first, spans = 0, []
    for p in plist:
        spans.append((first, first + p.shape[1] // tr))
        first = spans[-1][1]

    def body(w_ref, m_ref, v_ref, *rest):
        p_refs, (g_ref, d_ref, m2_ref, v2_ref) = rest[:len(plist)], rest[len(plist):]
        i = pl.program_id(0)

        def update(p_ref):
            g = p_ref[0].astype(F32)
            for k in range(1, N_DEV):
                g = g + p_ref[k].astype(F32)
            g_ref[...] = g
            d_ref[...], m2_ref[...], v2_ref[...] = _adam_math(w_ref[...], g, m_ref[...], v_ref[...])

        if len(plist) == 1:
            update(p_refs[0])
        else:
            for p_ref, (lo, hi) in zip(p_refs, spans):
                pl.when((i >= lo) & (i < hi))(functools.partial(update, p_ref))

    def part_spec(lo, hi):
        return pl.BlockSpec((N_DEV, tr, C), lambda i: (0, jnp.clip(i - lo, 0, hi - lo - 1), 0))

    return pl.pallas_call(
        body,
        name=name,
        grid=(R // tr,),
        in_specs=[row, row, row] + [part_spec(lo, hi) for lo, hi in spans],
        out_specs=[row] * 4,
        out_shape=[jax.ShapeDtypeStruct((R, C), F32)] * 4,
        compiler_params=_params("parallel"),
    )(w, m, v, *plist)


def _adamw_ada(w, m, v, cact_t, dmod_cols, *, name):
    R, C = w.shape
    B = cact_t.shape[1]
    tr = _tile(R, 256, 8)
    row = pl.BlockSpec((tr, C), lambda i: (i, 0))

    def body(w_ref, m_ref, v_ref, c_ref, d_ref, g_ref, dl_ref, m2_ref, v2_ref):
        g = c_ref[:, 0:1] * d_ref[0:1, :]
        for b in range(1, B):
            g = g + c_ref[:, b:b + 1] * d_ref[b:b + 1, :]
        g_ref[...] = g
        dl_ref[...], m2_ref[...], v2_ref[...] = _adam_math(w_ref[...], g, m_ref[...], v_ref[...])

    return pl.pallas_call(
        body,
        name=name,
        grid=(R // tr,),
        in_specs=[row, row, row, pl.BlockSpec((tr, B), lambda i: (i, 0)), pl.BlockSpec((B, C), lambda i: (0, 0))],
        out_specs=[row] * 4,
        out_shape=[jax.ShapeDtypeStruct((R, C), F32)] * 4,
        compiler_params=_params("parallel"),
    )(w, m, v, cact_t, dmod_cols)


def _z_layout(D, q_rank, kv_rank):
    kv = SWA_KVH * SWA_HD
    orig = {}
    o = 0
    for nm, w in (("cq", q_rank), ("ckv", kv_rank), ("kr", MLA_ROPE), ("qs", D), ("ks", kv), ("vs", kv),
                  ("ga", D), ("gb", D)):
        orig[nm] = (o, w)
        o += w
    blockw = {"cq": q_rank, "ckv": kv_rank, "kr": LANES, "qs": D, "ks": kv, "vs": kv, "ga": D, "gb": D}
    best = None
    for perm in itertools.permutations(("cq", "ckv", "ks", "vs", "kr")):
        off, new = 0, {}
        for nm in ("ga", "gb", "qs") + perm:
            off = _round_up(off, blockw[nm])
            new[nm] = off
            off += blockw[nm]
        if best is None or off < best[0]:
            best = (off, new)
    total = _round_up(best[0], 1024 if best[0] > 4096 else 512)
    return orig, best[1], blockw, total, o


def _permute_w_in(w, lay):
    orig, new, blockw, total, _ = lay
    parts, at = [], 0
    for nm in sorted(new, key=new.get):
        if new[nm] > at:
            parts.append(jnp.zeros((w.shape[0], new[nm] - at), w.dtype))
        o, wd = orig[nm]
        parts.append(w[:, o:o + wd])
        if blockw[nm] > wd:
            parts.append(jnp.zeros((w.shape[0], blockw[nm] - wd), w.dtype))
        at = new[nm] + blockw[nm]
    if total > at:
        parts.append(jnp.zeros((w.shape[0], total - at), w.dtype))
    return jnp.concatenate(parts, axis=1)


def _unpermute_w_in(wp, lay):
    orig, new, _, _, _ = lay
    return jnp.concatenate([wp[:, new[nm]:new[nm] + orig[nm][1]] for nm in sorted(orig, key=lambda n: orig[n][0])],
                           axis=1)


def _assemble_dz(parts, lay, S):
    _, new, blockw, total, _ = lay
    names = sorted(new, key=new.get)
    tr = _tile(S, 256, 16)

    def body(*refs):
        o_ref = refs[-1]
        cols, at = [], 0
        for nm, ref in zip(names, refs):
            if new[nm] > at:
                cols.append(jnp.zeros((tr, new[nm] - at), BF16))
            cols.append(ref[...])
            at = new[nm] + blockw[nm]
        if total > at:
            cols.append(jnp.zeros((tr, total - at), BF16))
        o_ref[...] = jnp.concatenate(cols, axis=1)

    return pl.pallas_call(
        body,
        name="assemble_dz",
        grid=(S // tr,),
        in_specs=[pl.BlockSpec((tr, blockw[nm]), lambda i: (i, 0)) for nm in names],
        out_specs=pl.BlockSpec((tr, total), lambda i: (i, 0)),
        out_shape=jax.ShapeDtypeStruct((S, total), BF16),
        compiler_params=_params("parallel"),
    )(*[parts[nm] for nm in names])


def _unshard_cols(g):
    return jnp.transpose(g, (1, 0, 2)).reshape(g.shape[1], N_DEV * g.shape[2])


def _shard_cols(w):
    K, N = w.shape
    return jnp.transpose(w.reshape(K, N_DEV, N // N_DEV), (1, 0, 2))


def _pack(vecs, rows):
    flat = jnp.concatenate([v.reshape(-1) for v in vecs])
    return jnp.pad(flat, (0, rows * LANES - flat.shape[0])).reshape(rows, LANES)


def kernel(x, c, w_ada, b_ada, g_pre_mix, g_post_mix, w_in, g_q_lat, w_uq, g_kv_lat, w_ukv, rel_bias, sinks, w_o, g_pre_ffn, g_post_ffn, w_up, conv_w, conv_b, w_down, loss_target, m_w_ada, m_b_ada, m_g_pre_mix, m_g_post_mix, m_w_in, m_g_q_lat, m_w_uq, m_g_kv_lat, m_w_ukv, m_rel_bias, m_sinks, m_w_o, m_g_pre_ffn, m_g_post_ffn, m_w_up, m_conv_w, m_conv_b, m_w_down, v_w_ada, v_b_ada, v_g_pre_mix, v_g_post_mix, v_w_in, v_g_q_lat, v_w_uq, v_g_kv_lat, v_w_ukv, v_rel_bias, v_sinks, v_w_o, v_g_pre_ffn, v_g_post_ffn, v_w_up, v_conv_w, v_conv_b, v_w_down):
    S, D = x.shape[1], x.shape[2]
    Q_RANK, KV_RANK = g_q_lat.shape[1], g_kv_lat.shape[1]
    H_MLA = D // MLA_V
    H_SWA = D // SWA_HD
    G_SWA = H_SWA // SWA_KVH
    F2 = w_up.shape[2] * N_DEV
    F = F2 // 2
    ada_n = w_ada.shape[2]
    me = 4 * lax.axis_index("x") + 2 * lax.axis_index("y") + lax.axis_index("c")
    lay = _z_layout(D, Q_RANK, KV_RANK)
    _, zoff, _, NZ, in_cols = lay
    assert in_cols == w_in.shape[2] * N_DEV

    x2, tgt = x[0], loss_target[0]

    cw_n = conv_w.shape[2]
    small = jnp.concatenate([jnp.pad(c, ((0, 7), (0, 0))), jnp.pad(conv_w[0], ((0, 8 - CONV_WIDTH), (0, 0)))], axis=1)
    small_all = _all_gather(small, name="ag_cond", in_vmem=True)
    c_all = small_all[:, 0, :D]
    cw_full = _unshard_cols(small_all[:, :CONV_WIDTH, D:])
    b_cols = lax.dynamic_slice_in_dim(b_ada, me * ada_n, ada_n, axis=1)
    c_act, mod_cols = _ada_fwd(c_all, w_ada[0], b_cols, name="ada_fwd")
    mod_all = _all_gather(mod_cols, name="ag_mod", in_vmem=True)
    mod_me = lax.dynamic_index_in_dim(mod_all, me, axis=1, keepdims=False).reshape(1, N_DEV * ada_n)
    sh1, sc1, gt1, sh2, sc2, gt2 = [mod_me[:, k * D:(k + 1) * D] for k in range(6)]

    w_in_p = _permute_w_in(_unshard_cols(_all_gather(w_in[0].astype(BF16), name="ag_w_in", in_vmem=False)), lay)

    h1 = _prenorm(x2, g_pre_mix, sc1, sh1, name="prenorm_mix")
    z, (uq_g, ukv_g, o_g) = _matmul(h1, w_in_p, mode="nn", out_dtype=BF16, name="mm_in",
                                    comm=[("gather", w_uq[0].astype(BF16)), ("gather", w_ukv[0].astype(BF16)),
                                          ("gather", w_o[0].astype(BF16))])
    w_uq_p = jnp.pad(_unshard_cols(uq_g).reshape(Q_RANK, H_MLA, MLA_QK), ((0, 0), (0, 0), (0, MLA_QK_PAD - MLA_QK))
                     ).reshape(Q_RANK, H_MLA * MLA_QK_PAD)
    w_ukv_f = _unshard_cols(ukv_g)
    w_o_f = o_g.reshape(D, D)
    cqn = _prenorm(z, g_q_lat, None, None, name="norm_cq", off=zoff["cq"], width=Q_RANK)
    ckvn = _prenorm(z, g_kv_lat, None, None, name="norm_ckv", off=zoff["ckv"], width=KV_RANK)
    q_raw = _matmul(cqn, w_uq_p, mode="nn", out_dtype=BF16, name="mm_uq")
    kv = _matmul(ckvn, w_ukv_f, mode="nn", out_dtype=BF16, name="mm_ukv")
    tab_k = _rope_tables(S, LANES, 0)
    krr = _rope(z, tab_k, heads=1, width=LANES, transpose=False, name="rope_k", off=zoff["kr"])
    (o_a, lse, Qr), (up_g,) = _flash_fwd(q_raw, kv, krr, tab_k, heads=H_MLA, name="mla_fwd",
                                        comm=[("gather", w_up[0].astype(BF16))])
    w_up_f = _unshard_cols(up_g)

    bucket, valid = _t5_bucket_table()
    onehot = (jnp.asarray(bucket).reshape(-1, 1) == jnp.arange(LANES)[None, :]).astype(F32)
    rb_pad = jnp.pad(rel_bias, ((0, LANES - REL_BUCKETS), (0, LANES - H_SWA)))
    bias_t = _matmul(onehot, rb_pad, mode="nn", out_dtype=F32, name="bias_table", tm=2048, precision=HIGHEST)
    bias_full = jnp.transpose(bias_t[:, :H_SWA].reshape(BLOCK, 2 * BLOCK, H_SWA), (2, 0, 1))
    bias_full = jnp.where(jnp.asarray(valid)[None], bias_full, NEG)
    bias_full = jnp.transpose(bias_full.reshape(SWA_KVH, G_SWA, BLOCK, 2 * BLOCK), (0, 3, 1, 2)
                              ).reshape(SWA_KVH, 2 * BLOCK, G_SWA * BLOCK)
    sink_rows = jnp.broadcast_to(sinks.reshape(SWA_KVH, G_SWA, 1), (SWA_KVH, G_SWA, BLOCK)
                                 ).reshape(SWA_KVH, 1, G_SWA * BLOCK)
    kvw = SWA_KVH * SWA_HD

    def heads_first(t, n):
        return jnp.transpose(t.reshape(S, n, SWA_HD), (1, 0, 2))

    def heads_last(t):
        return jnp.transpose(t, (1, 0, 2)).reshape(S, t.shape[0] * SWA_HD)

    def queries_first(t):
        t = t.reshape(SWA_KVH, S // BLOCK, SWA_HD, G_SWA, BLOCK)
        return jnp.transpose(t, (1, 4, 0, 3, 2)).reshape(S, H_SWA * SWA_HD)

    qs_h = heads_first(z[:, zoff["qs"]:zoff["qs"] + D], H_SWA)
    ks_h = heads_first(z[:, zoff["ks"]:zoff["ks"] + kvw], SWA_KVH)
    vs_h = heads_first(z[:, zoff["vs"]:zoff["vs"] + kvw], SWA_KVH)
    o_b_h, _ = _swa_fwd(qs_h, ks_h, vs_h, bias_full, sink_rows, name="swa_fwd")
    o_b = queries_first(o_b_h)

    mixin = _gate_mix(z, o_a, o_b, D=D, off_a=zoff["ga"], off_b=zoff["gb"], name="gate_mix")
    mix = _matmul(mixin, w_o_f, mode="nn", out_dtype=F32, name="mm_o")
    x1 = _postnorm_res(x2, mix, gt1, g_post_mix, name="postnorm_mix")

    h2 = _prenorm(x1, g_pre_ffn, sc2, sh2, name="prenorm_ffn")
    up, (down_g,) = _matmul(h2, w_up_f, mode="nn", out_dtype=BF16, name="mm_up",
                            comm=[("gather", w_down[0].astype(BF16))])
    w_down_f = down_g.reshape(F, D)
    act = _conv_gate(up, cw_full, conv_b, name="conv_gate")
    y = _matmul(act, w_down_f, mode="nn", out_dtype=F32, name="mm_down")
    loss_part, dout, dy, dgt2, dg_post_ffn = _final_loss(x1, y, tgt, gt2, g_post_ffn, name="final_loss")
    loss = lax.psum(loss_part[0, 0], ("x", "y", "c"))

    dw_down = _matmul(act, dy, mode="tn", out_dtype=BF16, name="mm_down_dw")
    dact = _matmul(dy, w_down_f, mode="nt", out_dtype=BF16, name="mm_down_dx")
    (dup, dcw, dcb), (got_down,) = _conv_gate_bwd(up, dact, cw_full, conv_b, name="conv_gate_bwd",
                                                  comm=[("scatter", dw_down.reshape(N_DEV, F // N_DEV, D))])
    dcw = jnp.transpose(dcw, (1, 0, 2)).reshape(CONV_WIDTH, F2)
    dcb = dcb.reshape(1, F2)
    dw_up = _matmul(h2, dup, mode="tn", out_dtype=BF16, name="mm_up_dw", shard_out=True, halves=True)
    dh2 = _matmul(dup, w_up_f, mode="nt", out_dtype=F32, name="mm_up_dx", halves=True)
    dx1, dg_pre_ffn, dsc2, dsh2 = _prenorm_bwd(x1, dh2, dout, g_pre_ffn, sc2, name="prenorm_ffn_bwd", out_dtype=F32)

    dmix, dgt1, dg_post_mix = _postnorm_bwd(dx1, mix, gt1, g_post_mix, name="postnorm_mix_bwd")
    dw_o = _matmul(mixin, dmix, mode="tn", out_dtype=BF16, name="mm_o_dw")
    dmixin = _matmul(dmix, w_o_f, mode="nt", out_dtype=BF16, name="mm_o_dx")
    dga, dgb, do_a, do_b = _gate_mix_bwd(dmixin, z, o_a, o_b, D=D, off_a=zoff["ga"], off_b=zoff["gb"],
                                         name="gate_mix_bwd")
    (dq_raw, dkv, dkr_parts), (got_up,) = _flash_bwd(Qr, kv, krr, do_a, o_a, lse, tab_k, heads=H_MLA, name="mla_bwd",
                                                     comm=[("scatter", dw_up)])
    dkr = _shared_rope_grad(dkr_parts, tab_k, name="rope_k_bwd")
    dw_uq_p = _matmul(cqn, dq_raw, mode="tn", out_dtype=BF16, name="mm_uq_dw")
    dcqn = _matmul(dq_raw, w_uq_p, mode="nt", out_dtype=F32, name="mm_uq_dx")
    dw_ukv = _matmul(ckvn, dkv, mode="tn", out_dtype=BF16, name="mm_ukv_dw", shard_out=True)
    dckvn = _matmul(dkv, w_ukv_f, mode="nt", out_dtype=F32, name="mm_ukv_dx")
    dw_uq = dw_uq_p.reshape(Q_RANK, H_MLA, MLA_QK_PAD)[:, :, :MLA_QK].reshape(Q_RANK, H_MLA * MLA_QK)

    dcw_parts = jnp.pad(_shard_cols(dcw), ((0, 0), (0, 16 - CONV_WIDTH), (0, 0)))
    (dqs_h, dks_h, dvs_h, dbias, dsink), (got_o, got_cw, got_uq, got_ukv) = _swa_bwd(
        qs_h, ks_h, vs_h, bias_full, sink_rows, heads_first(do_b, H_SWA), name="swa_bwd",
        comm=[("scatter", dw_o.reshape(N_DEV, D // N_DEV, D)), ("scatter", dcw_parts),
              ("scatter", _shard_cols(dw_uq)), ("scatter", dw_ukv)])
    dbias = jnp.transpose(dbias.reshape(SWA_KVH, 2 * BLOCK, G_SWA, BLOCK), (0, 2, 3, 1))
    drel_t = _matmul(dbias.reshape(H_SWA, BLOCK * 2 * BLOCK), onehot, mode="nn", out_dtype=F32, name="bias_grad",
                     tk=4096, precision=HIGHEST)
    d_rel_bias = jnp.transpose(drel_t[:, :REL_BUCKETS])
    d_sinks = jnp.sum(dsink.reshape(SWA_KVH, G_SWA, BLOCK), axis=-1).reshape(1, H_SWA)

    dcq, dg_q = _prenorm_bwd(z, dcqn, None, g_q_lat, None, name="norm_cq_bwd", out_dtype=BF16,
                             off=zoff["cq"], width=Q_RANK)
    dckv, dg_kv = _prenorm_bwd(z, dckvn, None, g_kv_lat, None, name="norm_ckv_bwd", out_dtype=BF16,
                               off=zoff["ckv"], width=KV_RANK)
    dz = _assemble_dz({"ga": dga, "gb": dgb, "qs": queries_first(dqs_h), "cq": dcq, "ckv": dckv,
                       "ks": heads_last(dks_h), "vs": heads_last(dvs_h), "kr": dkr}, lay, S)
    def to_owners(part):
        return [("scatter", _shard_cols(_unpermute_w_in(part, lay)))]

    dw_in_a = _matmul(h1, dz, mode="tn", out_dtype=BF16, name="mm_in_dw_a", m_range=(0, D // 4))
    dw_in_b, (got_in_a,) = _matmul(h1, dz, mode="tn", out_dtype=BF16, name="mm_in_dw_b", m_range=(D // 4, D // 4),
                                   comm=to_owners(dw_in_a))
    dw_in_c, (got_in_b,) = _matmul(h1, dz, mode="tn", out_dtype=BF16, name="mm_in_dw_c", m_range=(D // 2, D // 2),
                                   comm=to_owners(dw_in_b))
    dh1, (got_in_c,) = _matmul(dz, w_in_p, mode="nt", out_dtype=F32, name="mm_in_dx", comm=to_owners(dw_in_c))
    grad_x, dg_pre_mix, dsc1, dsh1 = _prenorm_bwd(x2, dh1, dx1, g_pre_mix, sc1, name="prenorm_mix_bwd",
                                                  out_dtype=F32)
    dmod = jnp.concatenate([dsh1, dsc1, dgt1, dsh2, dsc2, dgt2], axis=1)

    small_names = ["b_ada", "g_pre_mix", "g_post_mix", "g_q_lat", "g_kv_lat", "rel_bias", "sinks", "g_pre_ffn",
                   "g_post_ffn", "conv_b"]
    small_w = [b_ada, g_pre_mix, g_post_mix, g_q_lat, g_kv_lat, rel_bias, sinks, g_pre_ffn, g_post_ffn, conv_b]
    small_m = [m_b_ada, m_g_pre_mix, m_g_post_mix, m_g_q_lat, m_g_kv_lat, m_rel_bias, m_sinks, m_g_pre_ffn,
               m_g_post_ffn, m_conv_b]
    small_v = [v_b_ada, v_g_pre_mix, v_g_post_mix, v_g_q_lat, v_g_kv_lat, v_rel_bias, v_sinks, v_g_pre_ffn,
               v_g_post_ffn, v_conv_b]
    small_g = [dmod, dg_pre_mix, dg_post_mix, dg_q, dg_kv, d_rel_bias, d_sinks, dg_pre_ffn, dg_post_ffn, dcb]
    n_small = sum(int(np.prod(w.shape)) for w in small_w)
    rows = _round_up(-(-n_small // LANES), 16)
    parts_small = _all_gather(_pack(small_g, rows), name="ag_small_grads", in_vmem=True)
    sg, sd, sm, sv = _adamw(_pack(small_w, rows), _pack(small_m, rows), _pack(small_v, rows), parts_small,
                            name="adamw_small")

    def unpack(packed):
        flat, out, at = packed.reshape(-1), {}, 0
        for nm, w in zip(small_names, small_w):
            n = int(np.prod(w.shape))
            out[nm] = flat[at:at + n].reshape(w.shape)
            at += n
        return out

    small_out = [unpack(t) for t in (sg, sd, sm, sv)]

    dmod_all = parts_small.reshape(N_DEV, rows * LANES)[:, :6 * D]
    dmod_cols = lax.dynamic_slice_in_dim(dmod_all, me * ada_n, ada_n, axis=1)
    ada_out = _adamw_ada(w_ada[0], m_w_ada[0], v_w_ada[0], jnp.transpose(c_act), dmod_cols, name="adamw_w_ada")

    def owner_update(got, w, m, v, name):
        shp = w.shape
        w2, m2, v2 = (t.reshape(shp[-2], shp[-1]) for t in (w, m, v))
        return [t.reshape(shp) for t in _adamw(w2, m2, v2, got, name="adamw_" + name)]

    def pad_rows(t):
        return jnp.pad(t[0], ((0, 16 - CONV_WIDTH), (0, 0)))

    big = {
        "w_in": owner_update([got_in_a, got_in_b, got_in_c], w_in, m_w_in, v_w_in, "w_in"),
        "w_uq": owner_update(got_uq, w_uq, m_w_uq, v_w_uq, "w_uq"),
        "w_ukv": owner_update(got_ukv, w_ukv, m_w_ukv, v_w_ukv, "w_ukv"),
        "w_o": owner_update(got_o, w_o, m_w_o, v_w_o, "w_o"),
        "w_up": owner_update(got_up, w_up, m_w_up, v_w_up, "w_up"),
        "w_down": owner_update(got_down, w_down, m_w_down, v_w_down, "w_down"),
    }
    cw_upd = _adamw(pad_rows(conv_w), pad_rows(m_conv_w), pad_rows(v_conv_w), got_cw, name="adamw_conv_w")
    big["conv_w"] = [t[:CONV_WIDTH].reshape(conv_w.shape) for t in cw_upd]
    big["w_ada"] = [t.reshape(w_ada.shape) for t in ada_out]

    order = ["w_ada", "b_ada", "g_pre_mix", "g_post_mix", "w_in", "g_q_lat", "w_uq", "g_kv_lat", "w_ukv", "rel_bias",
             "sinks", "w_o", "g_pre_ffn", "g_post_ffn", "w_up", "conv_w", "conv_b", "w_down"]
    outs = [loss, grad_x.reshape(x.shape)]
    for kind in range(4):
        for nm in order:
            outs.append(big[nm][kind] if nm in big else small_out[kind][nm])
    return tuple(outs)
```

```python
import functools
import itertools
import math

import numpy as np

import jax
import jax.numpy as jnp
from jax import lax
from jax.experimental import pallas as pl
from jax.experimental.pallas import tpu as pltpu

F32 = jnp.float32
BF16 = jnp.bfloat16

N_DEV = 8
MLA_NOPE = 128
MLA_ROPE = 64
MLA_V = 128
MLA_QK = MLA_NOPE + MLA_ROPE
MLA_QK_PAD = 256
ROPE_HALF = MLA_ROPE // 2
ROPE_THETA = 10000.0
SWA_HD = 64
SWA_KVH = 4
WINDOW = 128
BLOCK = 128
REL_BUCKETS = 32
REL_MAX_DIST = 128
CONV_WIDTH = 3
EPS = 1e-6
NEG = -1e30
ADAM_LR = 0.001
ADAM_B1 = 0.9
ADAM_B2 = 0.999
ADAM_EPS = 1e-08
ADAM_WD = 0.01
ADAM_STEP = 10
LANES = 128
HALO = 16
MESH = pl.DeviceIdType.MESH
HIGHEST = lax.Precision.HIGHEST

NN = (((1,), (0,)), ((), ()))
NT = (((1,), (1,)), ((), ()))
TN = (((0,), (0,)), ((), ()))


def _tile(n, pref, align=LANES):
    if n <= pref:
        return n
    t = (pref // align) * align
    while t >= align:
        if n % t == 0:
            return t
        t -= align
    return n


def _round_up(n, m):
    return (n + m - 1) // m * m


def _params(*sem):
    return pltpu.CompilerParams(dimension_semantics=sem)


def _sigmoid(x):
    return 1.0 / (1.0 + jnp.exp(-x))


def _my_place():
    return lax.axis_index("x"), lax.axis_index("y"), lax.axis_index("c")


def _all_gather(x, *, name, in_vmem):
    space = pltpu.VMEM if in_vmem else pl.ANY

    def body(x_ref, out_ref, send_sems, recv_sems, local_sem):
        x_, y_, c_ = _my_place()
        me, sibling = (x_, y_, c_), (x_, y_, 1 - c_)
        chips = [(1 - x_, y_), (x_, 1 - y_), (1 - x_, 1 - y_)]

        def slot(px, py, pc):
            return out_ref.at[4 * px + 2 * py + pc]

        def copy(k, block, to, src=None):
            return pltpu.make_async_remote_copy(
                src_ref=slot(*block) if src is None else src,
                dst_ref=slot(*block),
                send_sem=send_sems.at[k],
                recv_sem=recv_sems.at[k],
                device_id=to,
                device_id_type=MESH,
            )

        mine = pltpu.make_async_copy(x_ref, slot(*me), local_sem)
        mine.start()
        first = [copy(0, me, sibling, src=x_ref)]
        first += [copy(1 + j, me, (*chip, c_), src=x_ref) for j, chip in enumerate(chips)]
        for cp in first:
            cp.start()
        passed = [copy(4 + j, (*chip, c_), sibling) for j, chip in enumerate(chips)]
        for j, chip in enumerate(chips):
            copy(1 + j, (*chip, c_), me).wait_recv()
            passed[j].start()
        copy(0, sibling, me).wait_recv()
        for j, chip in enumerate(chips):
            copy(4 + j, (*chip, 1 - c_), me).wait_recv()
        for cp in first + passed:
            cp.wait_send()
        mine.wait()

    return pl.pallas_call(
        body,
        name=name,
        out_shape=jax.ShapeDtypeStruct((N_DEV,) + x.shape, x.dtype),
        in_specs=[pl.BlockSpec(memory_space=space)],
        out_specs=pl.BlockSpec(memory_space=space),
        scratch_shapes=[
            pltpu.SemaphoreType.DMA((7,)),
            pltpu.SemaphoreType.DMA((7,)),
            pltpu.SemaphoreType.DMA,
        ],
    )(x)


class _Exchange:
    def __init__(self, kind, x_ref, out_ref, send_sems, recv_sems, local_sems, t):
        x_, y_, c_ = _my_place()
        me = 4 * x_ + 2 * y_ + c_

        def pair(k, src, dst, to):
            return pltpu.make_async_remote_copy(src_ref=src, dst_ref=dst, send_sem=send_sems.at[7 * t + k],
                                                recv_sem=recv_sems.at[7 * t + k], device_id=to, device_id_type=MESH)

        none = lambda: []
        if kind == "scatter":
            peers = [(x_ ^ ((r >> 2) & 1), y_ ^ ((r >> 1) & 1), c_ ^ (r & 1)) for r in range(1, N_DEV)]
            self.at_start = lambda: [pair(k, x_ref.at[4 * px + 2 * py + pc], out_ref.at[me], (px, py, pc))
                                     for k, (px, py, pc) in enumerate(peers)]
            self.relay_after, self.at_relay = none, none
            self.arrivals = self.at_start
            self.own = lambda: pltpu.make_async_copy(x_ref.at[me], out_ref.at[me], local_sems.at[t])
        else:
            sibling = (x_, y_, 1 - c_)
            chips = list(enumerate([(1 - x_, y_), (x_, 1 - y_), (1 - x_, 1 - y_)]))

            def slot(px, py, pc):
                return out_ref.at[4 * px + 2 * py + pc]

            mine = slot(x_, y_, c_)
            self.at_start = lambda: ([pair(0, x_ref, mine, sibling)]
                                     + [pair(1 + j, x_ref, mine, (*chip, c_)) for j, chip in chips])
            self.relay_after = lambda: [pair(1 + j, slot(*chip, c_), slot(*chip, c_), (*chip, c_)) for j, chip in chips]
            self.at_relay = lambda: [pair(4 + j, slot(*chip, c_), slot(*chip, c_), sibling) for j, chip in chips]
            self.arrivals = lambda: ([pair(0, slot(*sibling), slot(*sibling), sibling)]
                                     + [pair(4 + j, slot(*chip, 1 - c_), slot(*chip, 1 - c_), sibling)
                                        for j, chip in chips])
            self.own = lambda: pltpu.make_async_copy(x_ref, mine, local_sems.at[t])

    def start(self):
        self.own().start()
        for cp in self.at_start():
            cp.start()

    def relay(self):
        for landed, onward in zip(self.relay_after(), self.at_relay()):
            landed.wait_recv()
            onward.start()

    def finish(self):
        for cp in self.arrivals():
            cp.wait_recv()
        for cp in self.at_start() + self.at_relay():
            cp.wait_send()
        self.own().wait()


RELAY_AT = 0.85


def _call(body, *, name, grid, in_specs, out_specs, out_shape, args, scratch_shapes=(), sem=(), comm=(), prefetch=()):
    n_pf = len(prefetch)

    def launch(fn, ins, outs, shapes, scratch, semantics, operands):
        spec = pltpu.PrefetchScalarGridSpec(num_scalar_prefetch=n_pf, grid=grid, in_specs=ins, out_specs=outs,
                                            scratch_shapes=scratch)
        return pl.pallas_call(fn, name=name, grid_spec=spec, out_shape=shapes,
                              compiler_params=_params(*semantics))(*prefetch, *operands)

    if not comm:
        return list(launch(body, list(in_specs), list(out_specs), list(out_shape), list(scratch_shapes), sem, args)), []
    n_in, n_out, n_c, n_s = len(in_specs), len(out_specs), len(comm), len(scratch_shapes)
    kinds = [kind for kind, _ in comm]
    hbm = pl.BlockSpec(memory_space=pl.ANY)

    def wrapped(*refs):
        tables, refs = refs[:n_pf], refs[n_pf:]
        ins, cin = refs[:n_in], refs[n_in:n_in + n_c]
        at = n_in + n_c
        outs, cout = refs[at:at + n_out], refs[at + n_out:at + n_out + n_c]
        scr = refs[at + n_out + n_c:at + n_out + n_c + n_s]
        send, recv, local = refs[-3:]
        step = 0
        for a, g in enumerate(grid):
            step = step * g + pl.program_id(a)
        n_steps = int(np.prod(grid))

        def exchanges():
            return [_Exchange(kinds[t], cin[t], cout[t], send, recv, local, t) for t in range(n_c)]

        @pl.when(step == 0)
        def _():
            for ex in exchanges():
                ex.start()

        body(*tables, *ins, *outs, *scr)

        @pl.when(step == min(int(RELAY_AT * n_steps), n_steps - 1))
        def _():
            for ex in exchanges():
                ex.relay()

        @pl.when(step == n_steps - 1)
        def _():
            for ex in exchanges():
                ex.finish()

    c_shapes = [jax.ShapeDtypeStruct(((N_DEV,) + a.shape) if kind == "gather" else a.shape, a.dtype)
                for kind, a in comm]
    sems = [pltpu.SemaphoreType.DMA((7 * n_c,)), pltpu.SemaphoreType.DMA((7 * n_c,)), pltpu.SemaphoreType.DMA((n_c,))]
    res = launch(wrapped, list(in_specs) + [hbm] * n_c, list(out_specs) + [hbm] * n_c, list(out_shape) + c_shapes,
                 list(scratch_shapes) + sems, ["arbitrary"] * len(grid), (*args, *[a for _, a in comm]))
    return list(res[:n_out]), list(res[n_out:])


def _matmul(a, b, *, mode, out_dtype, name, tm=1024, tn=1024, tk=2816, precision=None, comm=(), shard_out=False,
            halves=False, m_range=None):
    if mode == "nn":
        (M, K), (K2, N) = a.shape, b.shape
    elif mode == "nt":
        (M, K), (N, K2) = (a.shape[1], 2 * a.shape[2]) if halves else a.shape, b.shape
    else:
        (K, M), (K2, N) = a.shape, (b.shape[1], 2 * b.shape[2]) if halves else b.shape
    assert K == K2, (a.shape, b.shape, mode)
    m_off = 0
    if m_range is not None:
        m_off, M = m_range
    tm = _tile(M, tm, LANES if mode == "tn" else 16)
    tk = _tile(K // 2 if halves and mode == "nt" else K, tk)
    tn = _tile(N // N_DEV, max(tn, 1408)) if shard_out else _tile(N // 2 if halves and mode == "tn" else N, tn)
    nk = K // tk
    m_off //= tm
    if mode == "tn":
        a_spec = pl.BlockSpec((tk, tm), lambda i, j, k: (k, i + m_off))
    elif halves:
        a_spec = pl.BlockSpec((None, tm, tk), lambda i, j, k: (k // (nk // 2), i, k % (nk // 2)))
    else:
        a_spec = pl.BlockSpec((tm, tk), lambda i, j, k: (i, k))
    if mode == "nt":
        b_spec = pl.BlockSpec((tn, tk), lambda i, j, k: (j, k))
    elif halves:
        nj = N // tn
        b_spec = pl.BlockSpec((None, tk, tn), lambda i, j, k: (j // (nj // 2), k, j % (nj // 2)))
    else:
        b_spec = pl.BlockSpec((tk, tn), lambda i, j, k: (k, j))
    dn = {"nn": NN, "nt": NT, "tn": TN}[mode]
    if shard_out:
        per = N // N_DEV // tn
        o_spec = pl.BlockSpec((None, tm, tn), lambda i, j, k: (j // per, i, j % per))
        o_shape = jax.ShapeDtypeStruct((N_DEV, M, N // N_DEV), out_dtype)
    else:
        o_spec = pl.BlockSpec((tm, tn), lambda i, j, k: (i, j))
        o_shape = jax.ShapeDtypeStruct((M, N), out_dtype)

    def product(a_ref, b_ref):
        return lax.dot_general(a_ref[...], b_ref[...], dn, preferred_element_type=F32, precision=precision)

    def body_one(a_ref, b_ref, o_ref):
        o_ref[...] = product(a_ref, b_ref).astype(o_ref.dtype)

    def body_acc(a_ref, b_ref, o_ref, acc_ref):
        k = pl.program_id(2)

        @pl.when(k == 0)
        def _():
            acc_ref[...] = product(a_ref, b_ref)

        @pl.when(k > 0)
        def _():
            acc_ref[...] += product(a_ref, b_ref)

        @pl.when(k == nk - 1)
        def _():
            o_ref[...] = acc_ref[...].astype(o_ref.dtype)

    outs, moved = _call(
        body_one if nk == 1 else body_acc,
        name=name,
        grid=(M // tm, N // tn, nk),
        in_specs=[a_spec, b_spec],
        out_specs=[o_spec],
        out_shape=[o_shape],
        scratch_shapes=[] if nk == 1 else [pltpu.VMEM((tm, tn), F32)],
        sem=("parallel", "parallel", "arbitrary"),
        args=(a, b),
        comm=comm,
    )
    return (outs[0], moved) if comm else outs[0]


def _rstd(xf):
    return lax.rsqrt(jnp.mean(xf * xf, axis=-1, keepdims=True) + EPS)


def _col_view(width, off):
    assert off % width == 0
    return off // width


def _prenorm(x, g, sc, sh, *, name, off=0, width=None):
    S = x.shape[0]
    W = x.shape[1] if width is None else width
    cb = _col_view(W, off)
    tr = _tile(S, 512, 16)
    mod = sc is not None
    vec = pl.BlockSpec((1, W), lambda i: (0, 0))

    def body(*refs):
        if mod:
            x_ref, g_ref, sc_ref, sh_ref, o_ref = refs
        else:
            x_ref, g_ref, o_ref = refs
        xf = x_ref[...].astype(F32)
        y = xf * _rstd(xf) * g_ref[...]
        if mod:
            y = y * (1.0 + sc_ref[...]) + sh_ref[...]
        o_ref[...] = y.astype(o_ref.dtype)

    args = (x, g, sc, sh) if mod else (x, g)
    return pl.pallas_call(
        body,
        name=name,
        grid=(S // tr,),
        in_specs=[pl.BlockSpec((tr, W), lambda i: (i, cb))] + [vec] * (len(args) - 1),
        out_specs=pl.BlockSpec((tr, W), lambda i: (i, 0)),
        out_shape=jax.ShapeDtypeStruct((S, W), BF16),
        compiler_params=_params("parallel"),
    )(*args)


def _prenorm_bwd(x, dh, dres, g, sc, *, name, out_dtype, off=0, width=None):
    S = x.shape[0]
    W = x.shape[1] if width is None else width
    cb = _col_view(W, off)
    tr = _tile(S, 256, 16)
    mod = sc is not None
    res = dres is not None
    vec = pl.BlockSpec((1, W), lambda i: (0, 0))
    row = pl.BlockSpec((tr, W), lambda i: (i, 0))

    def body(*refs):
        it = iter(refs)
        x_ref, dh_ref = next(it), next(it)
        dres_ref = next(it) if res else None
        g_ref = next(it)
        sc_ref = next(it) if mod else None
        dx_ref, dg_ref = next(it), next(it)
        dsc_ref, dsh_ref = (next(it), next(it)) if mod else (None, None)
        i = pl.program_id(0)

        @pl.when(i == 0)
        def _():
            dg_ref[...] = jnp.zeros_like(dg_ref)
            if mod:
                dsc_ref[...] = jnp.zeros_like(dsc_ref)
                dsh_ref[...] = jnp.zeros_like(dsh_ref)

        xf = x_ref[...].astype(F32)
        r = _rstd(xf)
        xn = xf * r
        dhf = dh_ref[...].astype(F32)
        gv = g_ref[...]
        if mod:
            one_sc = 1.0 + sc_ref[...]
            dsh_ref[...] += jnp.sum(dhf, axis=0, keepdims=True)
            dsc_ref[...] += jnp.sum(dhf * (xn * gv), axis=0, keepdims=True)
            dg_ref[...] += jnp.sum(dhf * xn * one_sc, axis=0, keepdims=True)
            dxn = dhf * (gv * one_sc)
        else:
            dg_ref[...] += jnp.sum(dhf * xn, axis=0, keepdims=True)
            dxn = dhf * gv
        dx = r * (dxn - xn * jnp.mean(dxn * xn, axis=-1, keepdims=True))
        if res:
            dx = dx + dres_ref[...]
        dx_ref[...] = dx.astype(dx_ref.dtype)

    args = [x, dh] + ([dres] if res else []) + [g] + ([sc] if mod else [])
    in_specs = [pl.BlockSpec((tr, W), lambda i: (i, cb)), row] + ([row] if res else []) + [vec] + ([vec] if mod else [])
    n_vec = 3 if mod else 1
    outs = pl.pallas_call(
        body,
        name=name,
        grid=(S // tr,),
        in_specs=in_specs,
        out_specs=[row] + [vec] * n_vec,
        out_shape=[jax.ShapeDtypeStruct((S, W), out_dtype)] + [jax.ShapeDtypeStruct((1, W), F32)] * n_vec,
        compiler_params=_params("arbitrary"),
    )(*args)
    return outs


def _postnorm_res(x, y, gt, g, *, name):
    S, D = x.shape
    tr = _tile(S, 512, 8)
    row = pl.BlockSpec((tr, D), lambda i: (i, 0))
    vec = pl.BlockSpec((1, D), lambda i: (0, 0))

    def body(x_ref, y_ref, gt_ref, g_ref, o_ref):
        yf = y_ref[...]
        o_ref[...] = x_ref[...] + gt_ref[...] * (yf * _rstd(yf) * g_ref[...])

    return pl.pallas_call(
        body,
        name=name,
        grid=(S // tr,),
        in_specs=[row, row, vec, vec],
        out_specs=row,
        out_shape=jax.ShapeDtypeStruct((S, D), F32),
        compiler_params=_params("parallel"),
    )(x, y, gt, g)


def _postnorm_bwd(dx1, y, gt, g, *, name):
    S, D = y.shape
    tr = _tile(S, 256, 16)
    row = pl.BlockSpec((tr, D), lambda i: (i, 0))
    vec = pl.BlockSpec((1, D), lambda i: (0, 0))

    def body(dx_ref, y_ref, gt_ref, g_ref, dy_ref, dgt_ref, dg_ref):
        @pl.when(pl.program_id(0) == 0)
        def _():
            dgt_ref[...] = jnp.zeros_like(dgt_ref)
            dg_ref[...] = jnp.zeros_like(dg_ref)

        yf = y_ref[...]
        r = _rstd(yf)
        yn = yf * r
        d = dx_ref[...]
        gtv, gv = gt_ref[...], g_ref[...]
        dgt_ref[...] += jnp.sum(d * (yn * gv), axis=0, keepdims=True)
        dg_ref[...] += jnp.sum(d * gtv * yn, axis=0, keepdims=True)
        dyn = d * (gtv * gv)
        dy_ref[...] = (r * (dyn - yn * jnp.mean(dyn * yn, axis=-1, keepdims=True))).astype(dy_ref.dtype)

    return pl.pallas_call(
        body,
        name=name,
        grid=(S // tr,),
        in_specs=[row, row, vec, vec],
        out_specs=[row, vec, vec],
        out_shape=[jax.ShapeDtypeStruct((S, D), BF16), jax.ShapeDtypeStruct((1, D), F32),
                   jax.ShapeDtypeStruct((1, D), F32)],
        compiler_params=_params("arbitrary"),
    )(dx1, y, gt, g)


def _final_loss(x1, y, target, gt, g, *, name):
    S, D = y.shape
    tr = _tile(S, 256, 16)
    row = pl.BlockSpec((tr, D), lambda i: (i, 0))
    vec = pl.BlockSpec((1, D), lambda i: (0, 0))
    one = pl.BlockSpec((1, LANES), lambda i: (0, 0))

    def body(x_ref, y_ref, t_ref, gt_ref, g_ref, loss_ref, dout_ref, dy_ref, dgt_ref, dg_ref):
        @pl.when(pl.program_id(0) == 0)
        def _():
            loss_ref[...] = jnp.zeros_like(loss_ref)
            dgt_ref[...] = jnp.zeros_like(dgt_ref)
            dg_ref[...] = jnp.zeros_like(dg_ref)

        yf = y_ref[...]
        r = _rstd(yf)
        yn = yf * r
        gtv, gv = gt_ref[...], g_ref[...]
        out = x_ref[...] + gtv * (yn * gv)
        diff = out - t_ref[...]
        per_tok = jnp.mean(diff * diff, axis=-1, keepdims=True)
        loss_ref[...] += 0.5 * jnp.sum(per_tok, axis=0, keepdims=True)
        d = diff / D
        dout_ref[...] = d
        dgt_ref[...] += jnp.sum(d * (yn * gv), axis=0, keepdims=True)
        dg_ref[...] += jnp.sum(d * gtv * yn, axis=0, keepdims=True)
        dyn = d * (gtv * gv)
        dy_ref[...] = (r * (dyn - yn * jnp.mean(dyn * yn, axis=-1, keepdims=True))).astype(dy_ref.dtype)

    return pl.pallas_call(
        body,
        name=name,
        grid=(S // tr,),
        in_specs=[row, row, row, vec, vec],
        out_specs=[one, row, row, vec, vec],
        out_shape=[jax.ShapeDtypeStruct((1, LANES), F32), jax.ShapeDtypeStruct((S, D), F32),
                   jax.ShapeDtypeStruct((S, D), BF16), jax.ShapeDtypeStruct((1, D), F32),
                   jax.ShapeDtypeStruct((1, D), F32)],
        compiler_params=_params("arbitrary"),
    )(x1, y, target, gt, g)


def _ada_fwd(c_all, w_local, b_cols, *, name):
    B, D = c_all.shape
    N = w_local.shape[1]
    tn = _tile(N, 512)

    def body(c_ref, w_ref, b_ref, ca_ref, mod_ref):
        cv = c_ref[...]
        ca = cv * _sigmoid(cv)
        ca_ref[...] = ca
        mod_ref[...] = jnp.dot(ca, w_ref[...], preferred_element_type=F32, precision=HIGHEST) + b_ref[...]

    return pl.pallas_call(
        body,
        name=name,
        grid=(N // tn,),
        in_specs=[pl.BlockSpec((B, D), lambda j: (0, 0)), pl.BlockSpec((D, tn), lambda j: (0, j)),
                  pl.BlockSpec((1, tn), lambda j: (0, j))],
        out_specs=[pl.BlockSpec((B, D), lambda j: (0, 0)), pl.BlockSpec((B, tn), lambda j: (0, j))],
        out_shape=[jax.ShapeDtypeStruct((B, D), F32), jax.ShapeDtypeStruct((B, N), F32)],
        compiler_params=_params("arbitrary"),
    )(c_all, w_local, b_cols)


def _rope_tables(S, width, lane_off):
    pos = jnp.arange(S, dtype=F32)
    inv = ROPE_THETA ** (-jnp.arange(0, MLA_ROPE, 2, dtype=F32) / MLA_ROPE)
    ang = pos[:, None] * inv[None, :]
    ang = jnp.concatenate([ang, ang], axis=-1)
    cos, sin = jnp.cos(ang), jnp.sin(ang)
    first = (jnp.arange(MLA_ROPE) < ROPE_HALF)[None, :]
    sa = jnp.where(first, -sin, 0.0)
    sb = jnp.where(first, 0.0, sin)

    def place(t, fill):
        return jnp.pad(t, ((0, 0), (lane_off, width - lane_off - MLA_ROPE)), constant_values=fill)

    return place(cos, 1.0), place(sa, 0.0), place(sb, 0.0)


def _rope_apply(x, cos, sa, sb, width, transpose):
    if transpose:
        return x * cos + pltpu.roll(x * sa, ROPE_HALF, 1) + pltpu.roll(x * sb, width - ROPE_HALF, 1)
    return x * cos + pltpu.roll(x, width - ROPE_HALF, 1) * sa + pltpu.roll(x, ROPE_HALF, 1) * sb


def _rope(x, tables, *, heads, width, transpose, name, off=0, scale=1.0):
    S = x.shape[0]
    cb = _col_view(width, off)
    tr = _tile(S, 512, 16)
    tab = pl.BlockSpec((tr, width), lambda i, h: (i, 0))

    def body(x_ref, c_ref, sa_ref, sb_ref, o_ref):
        y = _rope_apply(x_ref[...].astype(F32), c_ref[...], sa_ref[...], sb_ref[...], width, transpose)
        o_ref[...] = (y if scale == 1.0 else y * scale).astype(o_ref.dtype)

    return pl.pallas_call(
        body,
        name=name,
        grid=(S // tr, heads),
        in_specs=[pl.BlockSpec((tr, width), lambda i, h: (i, cb + h)), tab, tab, tab],
        out_specs=pl.BlockSpec((tr, width), lambda i, h: (i, h)),
        out_shape=jax.ShapeDtypeStruct((S, heads * width), BF16),
        compiler_params=_params("parallel", "parallel"),
    )(x, *tables)


def _shared_rope_grad(parts, tables, *, name):
    P, S, _ = parts.shape
    tr = _tile(S, 512, 16)
    tab = pl.BlockSpec((tr, LANES), lambda i: (i, 0))

    def body(p_ref, c_ref, sa_ref, sb_ref, o_ref):
        acc = p_ref[0]
        for k in range(1, P):
            acc = acc + p_ref[k]
        o_ref[...] = _rope_apply(acc, c_ref[...], sa_ref[...], sb_ref[...], LANES, True).astype(o_ref.dtype)

    return pl.pallas_call(
        body,
        name=name,
        grid=(S // tr,),
        in_specs=[pl.BlockSpec((P, tr, LANES), lambda i: (0, i, 0)), tab, tab, tab],
        out_specs=tab,
        out_shape=jax.ShapeDtypeStruct((S, LANES), BF16),
        compiler_params=_params("parallel"),
    )(parts, *tables)


MLA_SCALE = MLA_QK ** -0.5
LOG2E = math.log2(math.e)
LN2 = math.log(2.0)
MLA_Q_PRESCALE = MLA_SCALE * LOG2E


def _lane_tile(v, n):
    return v if n == LANES else jnp.tile(v, (1, n // LANES))


def _causal_mask(s):
    rows = lax.broadcasted_iota(jnp.int32, s.shape, 0)
    cols = lax.broadcasted_iota(jnp.int32, s.shape, 1)
    return jnp.where(cols <= rows, s, NEG)


def _tri_blocks(nb, q_major):
    if q_major:
        pairs = [(q, k) for q in range(nb) for k in range(q + 1)]
    else:
        pairs = [(q, k) for k in range(nb) for q in range(k, nb)]
    return (jnp.asarray(np.array([p[0] for p in pairs], np.int32)),
            jnp.asarray(np.array([p[1] for p in pairs], np.int32)))


HEAD_PAIR = 4


def _flash_fwd(q_raw, KV, krr, tables, *, heads, name, comm=()):
    S = q_raw.shape[0]
    t = _tile(S, 512)
    nb = S // t
    qt, kt = _tri_blocks(nb, True)
    qw, vw = HEAD_PAIR * MLA_QK_PAD, HEAD_PAIR * MLA_V

    def body(qt_ref, kt_ref, q_ref, *rest):
        kn_refs, kr_ref, v_refs = rest[:HEAD_PAIR], rest[HEAD_PAIR], rest[HEAD_PAIR + 1:2 * HEAD_PAIR + 1]
        c_ref, sa_ref, sb_ref, o_ref, lse_ref, qr_ref, m_scr, l_scr, acc_scr = rest[2 * HEAD_PAIR + 1:]
        step_id = pl.program_id(1)
        qi, ki = qt_ref[step_id], kt_ref[step_id]

        @pl.when(ki == 0)
        def _():
            m_scr[...] = jnp.full_like(m_scr, NEG)
            l_scr[...] = jnp.zeros_like(l_scr)
            acc_scr[...] = jnp.zeros_like(acc_scr)
            for h in range(HEAD_PAIR):
                base = h * MLA_QK_PAD
                nope = q_ref[:, base:base + MLA_NOPE].astype(F32) * MLA_Q_PRESCALE
                rot = _rope_apply(q_ref[:, base + MLA_NOPE:base + MLA_QK_PAD].astype(F32), c_ref[...], sa_ref[...],
                                  sb_ref[...], LANES, False) * MLA_Q_PRESCALE
                qr_ref[:, base:base + MLA_NOPE] = nope.astype(qr_ref.dtype)
                qr_ref[:, base + MLA_NOPE:base + MLA_QK_PAD] = rot.astype(qr_ref.dtype)

        def step(diagonal):
            for h, (kn_ref, v_ref) in enumerate(zip(kn_refs, v_refs)):
                cols = slice(h * MLA_QK_PAD, (h + 1) * MLA_QK_PAD)
                k = jnp.concatenate([kn_ref[...], kr_ref[...]], axis=1)
                s = lax.dot_general(qr_ref[:, cols], k, NT, preferred_element_type=F32)
                if diagonal:
                    s = _causal_mask(s)
                m_prev = m_scr[h]
                m_new = jnp.maximum(m_prev, jnp.max(s, axis=1, keepdims=True))
                alpha = jnp.exp2(m_prev - m_new)
                p = jnp.exp2(s - _lane_tile(m_new, t))
                l_new = alpha * l_scr[h] + jnp.sum(p, axis=1, keepdims=True)
                acc = alpha * acc_scr[h] + jnp.dot(p.astype(BF16), v_ref[...], preferred_element_type=F32)
                if diagonal:
                    o_ref[:, h * MLA_V:(h + 1) * MLA_V] = (acc / l_new).astype(o_ref.dtype)
                    lse_ref[h] = m_new + jnp.log(l_new) * LOG2E
                else:
                    l_scr[h], acc_scr[h], m_scr[h] = l_new, acc, m_new

        pl.when(ki < qi)(lambda: step(False))
        pl.when(ki == qi)(lambda: step(True))

    def kvspec(h, half):
        return pl.BlockSpec((t, LANES), lambda hp, s, qt, kt: (kt[s], 2 * (HEAD_PAIR * hp + h) + half))

    qtab = pl.BlockSpec((t, LANES), lambda hp, s, qt, kt: (qt[s], 0))
    qrow = lambda hp, s, qt, kt: (qt[s], hp)
    return _call(
        body,
        name=name,
        grid=(heads // HEAD_PAIR, int(qt.shape[0])),
        in_specs=[pl.BlockSpec((t, qw), qrow), *[kvspec(h, 0) for h in range(HEAD_PAIR)],
                  pl.BlockSpec((t, LANES), lambda hp, s, qt, kt: (kt[s], 0)),
                  *[kvspec(h, 1) for h in range(HEAD_PAIR)], qtab, qtab, qtab],
        out_specs=[pl.BlockSpec((t, vw), qrow),
                   pl.BlockSpec((HEAD_PAIR, t, LANES), lambda hp, s, qt, kt: (hp, qt[s], 0)),
                   pl.BlockSpec((t, qw), qrow)],
        out_shape=[jax.ShapeDtypeStruct((S, heads * MLA_V), BF16),
                   jax.ShapeDtypeStruct((heads, S, LANES), F32),
                   jax.ShapeDtypeStruct((S, heads * MLA_QK_PAD), BF16)],
        scratch_shapes=[pltpu.VMEM((HEAD_PAIR, t, LANES), F32), pltpu.VMEM((HEAD_PAIR, t, LANES), F32),
                        pltpu.VMEM((HEAD_PAIR, t, MLA_V), F32)],
        sem=("parallel", "arbitrary"),
        args=(q_raw, *[KV] * HEAD_PAIR, krr, *[KV] * HEAD_PAIR, *tables),
        comm=comm,
        prefetch=(qt, kt),
    )


def _flash_bwd(Q, KV, krr, dO, O, lse, tables, *, heads, name, comm=()):
    S = Q.shape[0]
    t = _tile(S, 512)
    nb = S // t
    qt, kt = _tri_blocks(nb, False)
    n_steps = int(qt.shape[0])
    qw, vw = HEAD_PAIR * MLA_QK_PAD, HEAD_PAIR * MLA_V

    def body(qt_ref, kt_ref, q_ref, *rest):
        kn_refs, kr_ref, v_refs = rest[:HEAD_PAIR], rest[HEAD_PAIR], rest[HEAD_PAIR + 1:2 * HEAD_PAIR + 1]
        (do_ref, o_ref, lse_ref, c_ref, sa_ref, sb_ref, dq_ref, dkv_ref, dkr_ref,
         dq_scr, dk_scr, dv_scr, delta_scr) = rest[2 * HEAD_PAIR + 1:]
        step_id = pl.program_id(1)
        qi, ki = qt_ref[step_id], kt_ref[step_id]

        @pl.when(ki == 0)
        def _():
            for h in range(HEAD_PAIR):
                vc = slice(h * MLA_V, (h + 1) * MLA_V)
                d = jnp.sum(do_ref[:, vc].astype(F32) * o_ref[:, vc].astype(F32), axis=1, keepdims=True)
                delta_scr[h, qi] = jnp.broadcast_to(d, (t, LANES))

        def step(diagonal):
            for h, (kn_ref, v_ref) in enumerate(zip(kn_refs, v_refs)):
                base = h * MLA_QK_PAD
                cols = slice(base, base + MLA_QK_PAD)
                vc = slice(h * MLA_V, (h + 1) * MLA_V)
                q, do = q_ref[:, cols], do_ref[:, vc]
                k = jnp.concatenate([kn_ref[...], kr_ref[...]], axis=1)
                s = lax.dot_general(q, k, NT, preferred_element_type=F32)
                if diagonal:
                    s = _causal_mask(s)
                p = jnp.exp2(s - _lane_tile(lse_ref[h], t))
                dv = lax.dot_general(p.astype(BF16), do, TN, preferred_element_type=F32)
                dp = lax.dot_general(do, v_ref[...], NT, preferred_element_type=F32)
                ds = (p * (dp - _lane_tile(delta_scr[h, qi], t))).astype(BF16)
                dk = lax.dot_general(ds, q, TN, preferred_element_type=F32)
                dq = jnp.dot(ds, k, preferred_element_type=F32)
                if diagonal:
                    dk_scr[h], dv_scr[h] = dk, dv
                    dq = (dq_scr[qi, :, cols] + dq) * (LN2 * MLA_Q_PRESCALE)
                    rot = _rope_apply(dq[:, MLA_NOPE:], c_ref[...], sa_ref[...], sb_ref[...], LANES, True)
                    dq_ref[:, base:base + MLA_NOPE] = dq[:, :MLA_NOPE].astype(dq_ref.dtype)
                    dq_ref[:, base + MLA_NOPE:base + MLA_QK_PAD] = rot.astype(dq_ref.dtype)
                else:
                    dk_scr[h] += dk
                    dv_scr[h] += dv
                    dq_scr[qi, :, cols] += dq

        @pl.when(ki == 0)
        def _():
            dq_scr[qi] = jnp.zeros((t, qw), F32)

        pl.when(qi > ki)(lambda: step(False))
        pl.when(qi == ki)(lambda: step(True))

        @pl.when(qi == nb - 1)
        def _():
            shared = jnp.zeros((t, LANES), F32)
            for h in range(HEAD_PAIR):
                base = h * MLA_QK_PAD
                dk = dk_scr[h] * LN2
                dkv_ref[:, base:base + MLA_NOPE] = dk[:, :MLA_NOPE].astype(dkv_ref.dtype)
                dkv_ref[:, base + MLA_NOPE:base + MLA_QK_PAD] = dv_scr[h].astype(dkv_ref.dtype)
                shared = shared + dk[:, MLA_NOPE:]
            dkr_ref[0] = shared

    def kvspec(h, half):
        return pl.BlockSpec((t, LANES), lambda hp, s, qt, kt: (kt[s], 2 * (HEAD_PAIR * hp + h) + half))

    qrow = lambda hp, s, qt, kt: (qt[s], hp)
    krow = lambda hp, s, qt, kt: (kt[s], hp)
    ktab = pl.BlockSpec((t, LANES), lambda hp, s, qt, kt: (kt[s], 0))
    return _call(
        body,
        name=name,
        grid=(heads // HEAD_PAIR, n_steps),
        in_specs=[pl.BlockSpec((t, qw), qrow), *[kvspec(h, 0) for h in range(HEAD_PAIR)], ktab,
                  *[kvspec(h, 1) for h in range(HEAD_PAIR)],
                  pl.BlockSpec((t, vw), qrow), pl.BlockSpec((t, vw), qrow),
                  pl.BlockSpec((HEAD_PAIR, t, LANES), lambda hp, s, qt, kt: (hp, qt[s], 0)),
                  ktab, ktab, ktab],
        out_specs=[pl.BlockSpec((t, qw), krow), pl.BlockSpec((t, qw), krow),
                   pl.BlockSpec((1, t, LANES), lambda hp, s, qt, kt: (hp, kt[s], 0))],
        out_shape=[jax.ShapeDtypeStruct((S, heads * MLA_QK_PAD), BF16),
                   jax.ShapeDtypeStruct((S, heads * MLA_QK_PAD), BF16),
                   jax.ShapeDtypeStruct((heads // HEAD_PAIR, S, LANES), F32)],
        scratch_shapes=[pltpu.VMEM((nb, t, qw), F32), pltpu.VMEM((HEAD_PAIR, t, MLA_QK_PAD), F32),
                        pltpu.VMEM((HEAD_PAIR, t, MLA_V), F32), pltpu.VMEM((HEAD_PAIR, nb, t, LANES), F32)],
        sem=("parallel", "arbitrary"),
        args=(Q, *[KV] * HEAD_PAIR, krr, *[KV] * HEAD_PAIR, dO, O, lse, *tables),
        comm=comm,
        prefetch=(qt, kt),
    )


SWA_SCALE = SWA_HD ** -0.5


def _t5_bucket_table():
    a = np.arange(BLOCK)[:, None]
    j = np.arange(2 * BLOCK)[None, :]
    dist = BLOCK + a - j
    max_exact = REL_BUCKETS // 2
    n = np.maximum(dist, 0)
    large = max_exact + (np.log(np.maximum(n, 1).astype(np.float32) / np.float32(max_exact))
                         / np.float32(math.log(REL_MAX_DIST / max_exact))
                         * np.float32(REL_BUCKETS - max_exact)).astype(np.int32)
    large = np.minimum(large, REL_BUCKETS - 1)
    bucket = np.where(n < max_exact, n, large)
    valid = (dist >= 0) & (dist < WINDOW)
    return bucket.astype(np.int32), valid


def _heads_to_rows(x, G):
    return jnp.concatenate([x[:, g * SWA_HD:(g + 1) * SWA_HD] for g in range(G)], axis=0)


def _swa_probs(q_ref, kp_ref, kc_ref, bias_ref, sink_ref, qb, G):
    q2 = _heads_to_rows(q_ref[...], G)
    kb = jnp.concatenate([kp_ref[0], kc_ref[0]], axis=0)
    s = lax.dot_general(kb, q2, NT, preferred_element_type=F32) * SWA_SCALE + bias_ref[0]
    keys = lax.broadcasted_iota(jnp.int32, s.shape, 0)
    s = jnp.where((keys >= BLOCK) | (qb > 0), s, NEG)
    sink = sink_ref[0]
    m = jnp.maximum(jnp.max(s, axis=0, keepdims=True), sink)
    e = jnp.exp(s - m)
    es = jnp.exp(sink - m)
    inv = 1.0 / (jnp.sum(e, axis=0, keepdims=True) + es)
    return q2, kb, e * inv, es * inv


def _swa_fwd(q, q_off, k, v, bias_t, sink, *, name, comm=()):
    S = k.shape[1]
    G = bias_t.shape[2] // BLOCK
    nb = S // BLOCK
    qcol = _col_view(G * SWA_HD, q_off)
    cur = lambda kh, qb: (kh, qb, 0)
    prev = lambda kh, qb: (kh, jnp.maximum(qb - 1, 0), 0)
    kvspec = lambda im: pl.BlockSpec((1, BLOCK, SWA_HD), im)

    def body(q_ref, kc_ref, kp_ref, vc_ref, vp_ref, bias_ref, sink_ref, o_ref):
        qb = pl.program_id(1)
        _, _, pt, _ = _swa_probs(q_ref, kp_ref, kc_ref, bias_ref, sink_ref, qb, G)
        vb = jnp.concatenate([vp_ref[0], vc_ref[0]], axis=0)
        o_ref[0, 0] = lax.dot_general(vb, pt.astype(BF16), TN, preferred_element_type=F32).astype(o_ref.dtype)

    outs, moved = _call(
        body,
        name=name,
        grid=(SWA_KVH, nb),
        in_specs=[pl.BlockSpec((BLOCK, G * SWA_HD), lambda kh, qb: (qb, qcol + kh)),
                  kvspec(cur), kvspec(prev), kvspec(cur), kvspec(prev),
                  pl.BlockSpec((1, 2 * BLOCK, G * BLOCK), lambda kh, qb: (kh, 0, 0)),
                  pl.BlockSpec((1, 1, G * BLOCK), lambda kh, qb: (kh, 0, 0))],
        out_specs=[pl.BlockSpec((1, 1, SWA_HD, G * BLOCK), lambda kh, qb: (kh, qb, 0, 0))],
        out_shape=[jax.ShapeDtypeStruct((SWA_KVH, nb, SWA_HD, G * BLOCK), BF16)],
        sem=("parallel", "parallel"),
        args=(q, k, k, v, v, bias_t, sink),
        comm=comm,
    )
    return outs[0], moved


def _swa_bwd(q, q_off, k, v, bias_t, sink, do, *, name, comm=()):
    S = k.shape[1]
    G = bias_t.shape[2] // BLOCK
    nb = S // BLOCK
    qcol = _col_view(G * SWA_HD, q_off)
    cur = lambda kh, qb: (kh, jnp.minimum(qb, nb - 1), 0)
    prev = lambda kh, qb: (kh, jnp.maximum(jnp.minimum(qb, nb - 1) - 1, 0), 0)
    lag = lambda kh, qb: (kh, jnp.maximum(qb - 1, 0), 0)
    kvspec = lambda im: pl.BlockSpec((1, BLOCK, SWA_HD), im)

    def body(q_ref, kc_ref, kp_ref, vc_ref, vp_ref, bias_ref, sink_ref, do_ref,
             dq_ref, dk_ref, dv_ref, dbias_ref, dsink_ref, ck_scr, cv_scr):
        qb = pl.program_id(1)

        @pl.when(qb == 0)
        def _():
            dbias_ref[...] = jnp.zeros_like(dbias_ref)
            dsink_ref[...] = jnp.zeros_like(dsink_ref)
            ck_scr[...] = jnp.zeros_like(ck_scr)
            cv_scr[...] = jnp.zeros_like(cv_scr)

        @pl.when(qb < nb)
        def _():
            q2, kb, pt, ps = _swa_probs(q_ref, kp_ref, kc_ref, bias_ref, sink_ref, qb, G)
            vb = jnp.concatenate([vp_ref[0], vc_ref[0]], axis=0)
            do2 = _heads_to_rows(do_ref[...], G)
            dpt = lax.dot_general(vb, do2, NT, preferred_element_type=F32)
            delta = jnp.sum(dpt * pt, axis=0, keepdims=True)
            dst = pt * (dpt - delta)
            dbias_ref[0] += dst
            dsink_ref[0] += -ps * delta
            dsb = (dst * SWA_SCALE).astype(BF16)
            dq_ref[0, 0] = lax.dot_general(kb, dsb, TN, preferred_element_type=F32).astype(dq_ref.dtype)
            dkb = jnp.dot(dsb, q2, preferred_element_type=F32)
            dvb = jnp.dot(pt.astype(BF16), do2, preferred_element_type=F32)
            dk_ref[0] = (ck_scr[...] + dkb[:BLOCK]).astype(dk_ref.dtype)
            dv_ref[0] = (cv_scr[...] + dvb[:BLOCK]).astype(dv_ref.dtype)
            ck_scr[...] = dkb[BLOCK:]
            cv_scr[...] = dvb[BLOCK:]

        @pl.when(qb == nb)
        def _():
            dk_ref[0] = ck_scr[...].astype(dk_ref.dtype)
            dv_ref[0] = cv_scr[...].astype(dv_ref.dtype)

    tspec = pl.BlockSpec((1, 1, SWA_HD, G * BLOCK), lambda kh, qb: (kh, jnp.minimum(qb, nb - 1), 0, 0))
    return _call(
        body,
        name=name,
        grid=(SWA_KVH, nb + 1),
        in_specs=[pl.BlockSpec((BLOCK, G * SWA_HD), lambda kh, qb: (jnp.minimum(qb, nb - 1), qcol + kh)),
                  kvspec(cur), kvspec(prev), kvspec(cur), kvspec(prev),
                  pl.BlockSpec((1, 2 * BLOCK, G * BLOCK), lambda kh, qb: (kh, 0, 0)),
                  pl.BlockSpec((1, 1, G * BLOCK), lambda kh, qb: (kh, 0, 0)),
                  pl.BlockSpec((BLOCK, G * SWA_HD), lambda kh, qb: (jnp.minimum(qb, nb - 1), kh))],
        out_specs=[tspec, kvspec(lag), kvspec(lag),
                   pl.BlockSpec((1, 2 * BLOCK, G * BLOCK), lambda kh, qb: (kh, 0, 0)),
                   pl.BlockSpec((1, 1, G * BLOCK), lambda kh, qb: (kh, 0, 0))],
        out_shape=[jax.ShapeDtypeStruct((SWA_KVH, nb, SWA_HD, G * BLOCK), BF16),
                   jax.ShapeDtypeStruct((SWA_KVH, S, SWA_HD), BF16),
                   jax.ShapeDtypeStruct((SWA_KVH, S, SWA_HD), BF16),
                   jax.ShapeDtypeStruct((SWA_KVH, 2 * BLOCK, G * BLOCK), F32),
                   jax.ShapeDtypeStruct((SWA_KVH, 1, G * BLOCK), F32)],
        scratch_shapes=[pltpu.VMEM((BLOCK, SWA_HD), F32), pltpu.VMEM((BLOCK, SWA_HD), F32)],
        sem=("parallel", "arbitrary"),
        args=(q, k, k, v, v, bias_t, sink, do),
        comm=comm,
    )


def _gate_mix(z, o_a, o_b, *, D, off_a, off_b, name):
    S = z.shape[0]
    tr = _tile(S, 256, 16)
    row = pl.BlockSpec((tr, D), lambda i: (i, 0))
    ca, cb = _col_view(D, off_a), _col_view(D, off_b)

    def body(ga_ref, gb_ref, oa_ref, ob_ref, m_ref):
        m = (_sigmoid(ga_ref[...].astype(F32)) * oa_ref[...].astype(F32)
             + _sigmoid(gb_ref[...].astype(F32)) * ob_ref[...].astype(F32))
        m_ref[...] = m.astype(m_ref.dtype)

    return pl.pallas_call(
        body,
        name=name,
        grid=(S // tr,),
        in_specs=[pl.BlockSpec((tr, D), lambda i: (i, ca)), pl.BlockSpec((tr, D), lambda i: (i, cb)), row, row],
        out_specs=row,
        out_shape=jax.ShapeDtypeStruct((S, D), BF16),
        compiler_params=_params("parallel"),
    )(z, z, o_a, o_b)


def _gate_mix_bwd(dm, z, o_a, o_b, *, D, off_a, off_b, name):
    S = z.shape[0]
    tr = _tile(S, 256, 16)
    row = pl.BlockSpec((tr, D), lambda i: (i, 0))
    ca, cb = _col_view(D, off_a), _col_view(D, off_b)

    def body(dm_ref, ga_ref, gb_ref, oa_ref, ob_ref, dga_ref, dgb_ref, doa_ref, dob_ref):
        d = dm_ref[...].astype(F32)
        for g_ref, o_ref, dg_ref, do_ref in ((ga_ref, oa_ref, dga_ref, doa_ref), (gb_ref, ob_ref, dgb_ref, dob_ref)):
            sg = _sigmoid(g_ref[...].astype(F32))
            dg_ref[...] = (d * o_ref[...].astype(F32) * (sg * (1.0 - sg))).astype(dg_ref.dtype)
            do_ref[...] = (d * sg).astype(do_ref.dtype)

    return pl.pallas_call(
        body,
        name=name,
        grid=(S // tr,),
        in_specs=[row, pl.BlockSpec((tr, D), lambda i: (i, ca)), pl.BlockSpec((tr, D), lambda i: (i, cb)), row, row],
        out_specs=[row] * 4,
        out_shape=[jax.ShapeDtypeStruct((S, D), BF16)] * 4,
        compiler_params=_params("parallel"),
    )(dm, z, z, o_a, o_b)


CONV_ROWS = 256
CONV_COLS = 1408
SUBLANES = 8


def _shift_matrices(tr):
    r = np.arange(tr)[:, None]
    c = np.arange(tr)[None, :]
    back = [jnp.asarray(r == c + d, dtype=BF16) for d in (1, 2)]
    ahead = [jnp.asarray(r + d == c, dtype=BF16) for d in (1, 2)]
    return back, ahead


def _rows_before(x, halo_ref, first, b1_ref, b2_ref):
    s1 = jnp.dot(b1_ref[...], x, preferred_element_type=F32)
    s2 = jnp.dot(b2_ref[...], x, preferred_element_type=F32)
    h8 = jnp.where(first, 0.0, halo_ref[...].astype(F32)[HALO - SUBLANES:])
    rows = lax.broadcasted_iota(jnp.int32, h8.shape, 0)
    fix1 = jnp.where(rows < 1, pltpu.roll(h8, 1, 0), 0.0)
    fix2 = jnp.where(rows < 2, pltpu.roll(h8, 2, 0), 0.0)
    s1 = jnp.concatenate([s1[:SUBLANES] + fix1, s1[SUBLANES:]], axis=0)
    s2 = jnp.concatenate([s2[:SUBLANES] + fix2, s2[SUBLANES:]], axis=0)
    return s1, s2


def _conv_taps(x, s1, s2, cw_ref, cb_ref):
    return cb_ref[...] + cw_ref[0:1, :] * s2 + cw_ref[1:2, :] * s1 + cw_ref[2:3, :] * x


def _conv_gate(up, cw, cb, *, name):
    S, F2 = up.shape
    F = F2 // 2
    tr = _tile(S, CONV_ROWS, HALO)
    tc = _tile(F, CONV_COLS)
    nc = F // tc
    hb = tr // HALO
    back, _ = _shift_matrices(tr)
    mat = pl.BlockSpec((tr, tr), lambda i, j: (0, 0))

    def halo_map(shift):
        return lambda i, j: (jnp.maximum(i * hb - 1, 0), j + shift)

    def body(x1_ref, h1_ref, x2_ref, h2_ref, cw1_ref, cw2_ref, cb1_ref, cb2_ref, b1_ref, b2_ref, a_ref):
        first = pl.program_id(0) == 0
        us = []
        for x_ref, h_ref, cw_ref, cb_ref in ((x1_ref, h1_ref, cw1_ref, cb1_ref), (x2_ref, h2_ref, cw2_ref, cb2_ref)):
            x = x_ref[...]
            s1, s2 = _rows_before(x, h_ref, first, b1_ref, b2_ref)
            us.append(_conv_taps(x.astype(F32), s1, s2, cw_ref, cb_ref))
        u1, u2 = us
        a_ref[...] = (u1 * _sigmoid(u1) * u2).astype(a_ref.dtype)

    return pl.pallas_call(
        body,
        name=name,
        grid=(S // tr, nc),
        in_specs=[pl.BlockSpec((tr, tc), lambda i, j: (i, j)), pl.BlockSpec((HALO, tc), halo_map(0)),
                  pl.BlockSpec((tr, tc), lambda i, j: (i, j + nc)), pl.BlockSpec((HALO, tc), halo_map(nc)),
                  pl.BlockSpec((CONV_WIDTH, tc), lambda i, j: (0, j)),
                  pl.BlockSpec((CONV_WIDTH, tc), lambda i, j: (0, j + nc)),
                  pl.BlockSpec((1, tc), lambda i, j: (0, j)), pl.BlockSpec((1, tc), lambda i, j: (0, j + nc)),
                  mat, mat],
        out_specs=pl.BlockSpec((tr, tc), lambda i, j: (i, j)),
        out_shape=jax.ShapeDtypeStruct((S, F), BF16),
        compiler_params=_params("parallel", "parallel"),
    )(up, up, up, up, cw, cw, cb, cb, *back)


def _conv_gate_bwd(up, da, cw, cb, *, name, comm=()):
    S, F2 = up.shape
    F = F2 // 2
    tr = _tile(S, CONV_ROWS, HALO)
    tc = _tile(F, CONV_COLS)
    nc = F // tc
    hb = tr // HALO
    ni = S // tr
    back, ahead = _shift_matrices(tr)
    mat = pl.BlockSpec((tr, tr), lambda j, r: (0, 0))

    def cur(shift):
        return lambda j, r: (ni - 1 - r, j + shift)

    def before(shift):
        return lambda j, r: (jnp.maximum((ni - 1 - r) * hb - 1, 0), j + shift)

    def vec(rows, shift):
        return pl.BlockSpec((rows, tc), lambda j, r: (0, j + shift))

    def body(x1_ref, h1_ref, x2_ref, h2_ref, da_ref, cw1_ref, cw2_ref, cb1_ref, cb2_ref,
             b1_ref, b2_ref, a1_ref, a2_ref, dup_ref, dcw_ref, dcb_ref, next_du):
        r = pl.program_id(1)
        first = r == ni - 1

        @pl.when(r == 0)
        def _():
            dcw_ref[...] = jnp.zeros_like(dcw_ref)
            dcb_ref[...] = jnp.zeros_like(dcb_ref)
            next_du[...] = jnp.zeros_like(next_du)

        x1, x2 = x1_ref[...], x2_ref[...]
        x1f, x2f = x1.astype(F32), x2.astype(F32)
        s11, s12 = _rows_before(x1, h1_ref, first, b1_ref, b2_ref)
        s21, s22 = _rows_before(x2, h2_ref, first, b1_ref, b2_ref)
        u1 = _conv_taps(x1f, s11, s12, cw1_ref, cb1_ref)
        u2 = _conv_taps(x2f, s21, s22, cw2_ref, cb2_ref)
        sg = _sigmoid(u1)
        daf = da_ref[...].astype(F32)
        du1 = daf * u2 * (sg * (1.0 + u1 * (1.0 - sg)))
        du2 = daf * (u1 * sg)
        rows = lax.broadcasted_iota(jnp.int32, (SUBLANES, tc), 0)

        for half, (du, own, own1, own2, cw_ref) in enumerate(((du1, x1f, s11, s12, cw1_ref),
                                                             (du2, x2f, s21, s22, cw2_ref))):
            du_b = du.astype(BF16)
            n1 = jnp.dot(a1_ref[...], du_b, preferred_element_type=F32)
            n2 = jnp.dot(a2_ref[...], du_b, preferred_element_type=F32)
            c8 = next_du[half]
            fix1 = jnp.where(rows >= SUBLANES - 1, pltpu.roll(c8, SUBLANES - 1, 0), 0.0)
            fix2 = jnp.where(rows >= SUBLANES - 2, pltpu.roll(c8, SUBLANES - 2, 0), 0.0)
            n1 = jnp.concatenate([n1[:tr - SUBLANES], n1[tr - SUBLANES:] + fix1], axis=0)
            n2 = jnp.concatenate([n2[:tr - SUBLANES], n2[tr - SUBLANES:] + fix2], axis=0)
            dup = cw_ref[2:3, :] * du + cw_ref[1:2, :] * n1 + cw_ref[0:1, :] * n2
            dup_ref[half] = dup.astype(dup_ref.dtype)
            dcb_ref[half] += jnp.sum(du, axis=0, keepdims=True)
            for tap, shifted in enumerate((own2, own1, own)):
                dcw_ref[half, tap:tap + 1, :] += jnp.sum(du * shifted, axis=0, keepdims=True)
            next_du[half] = du[:SUBLANES].astype(BF16).astype(F32)

    return _call(
        body,
        name=name,
        grid=(nc, ni),
        in_specs=[pl.BlockSpec((tr, tc), cur(0)), pl.BlockSpec((HALO, tc), before(0)),
                  pl.BlockSpec((tr, tc), cur(nc)), pl.BlockSpec((HALO, tc), before(nc)),
                  pl.BlockSpec((tr, tc), cur(0)),
                  vec(CONV_WIDTH, 0), vec(CONV_WIDTH, nc), vec(1, 0), vec(1, nc), mat, mat, mat, mat],
        out_specs=[pl.BlockSpec((2, tr, tc), lambda j, r: (0, ni - 1 - r, j)),
                   pl.BlockSpec((2, CONV_WIDTH, tc), lambda j, r: (0, 0, j)),
                   pl.BlockSpec((2, 1, tc), lambda j, r: (0, 0, j))],
        out_shape=[jax.ShapeDtypeStruct((2, S, F), BF16), jax.ShapeDtypeStruct((2, CONV_WIDTH, F), F32),
                   jax.ShapeDtypeStruct((2, 1, F), F32)],
        scratch_shapes=[pltpu.VMEM((2, SUBLANES, tc), F32)],
        sem=("parallel", "arbitrary"),
        args=(up, up, up, up, da, cw, cw, cb, cb, *back, *ahead),
        comm=comm,
    )


def _adam_math(w, g, m, v):
    m = ADAM_B1 * m + (1.0 - ADAM_B1) * g
    v = ADAM_B2 * v + (1.0 - ADAM_B2) * (g * g)
    m_hat = m / (1.0 - ADAM_B1 ** ADAM_STEP)
    v_hat = v / (1.0 - ADAM_B2 ** ADAM_STEP)
    delta = -ADAM_LR * (m_hat / (jnp.sqrt(v_hat) + ADAM_EPS) + ADAM_WD * w)
    return delta, m, v


def _adamw(w, m, v, parts, *, name):
    R, C = w.shape
    plist = list(parts) if isinstance(parts, (list, tuple)) else [parts]
    tr = _tile(min(p.shape[1] for p in plist), 256, 16)
    assert sum(p.shape[1] for p in plist) == R and all(p.shape[1] % tr == 0 for p in plist)
    row = pl.BlockSpec((tr, C), lambda i: (i, 0))
    first, spans = 0, []
    for p in plist:
        spans.append((first, first + p.shape[1] // tr))
        first = spans[-1][1]

    def body(w_ref, m_ref, v_ref, *rest):
        p_refs, (g_ref, d_ref, m2_ref, v2_ref) = rest[:len(plist)], rest[len(plist):]
        i = pl.program_id(0)

        def update(p_ref):
            g = p_ref[0].astype(F32)
            for k in range(1, N_DEV):
                g = g + p_ref[k].astype(F32)
            g_ref[...] = g
            d_ref[...], m2_ref[...], v2_ref[...] = _adam_math(w_ref[...], g, m_ref[...], v_ref[...])

        if len(plist) == 1:
            update(p_refs[0])
        else:
            for p_ref, (lo, hi) in zip(p_refs, spans):
                pl.when((i >= lo) & (i < hi))(functools.partial(update, p_ref))

    def part_spec(lo, hi):
        return pl.BlockSpec((N_DEV, tr, C), lambda i: (0, jnp.clip(i - lo, 0, hi - lo - 1), 0))

    return pl.pallas_call(
        body,
        name=name,
        grid=(R // tr,),
        in_specs=[row, row, row] + [part_spec(lo, hi) for lo, hi in spans],
        out_specs=[row] * 4,
        out_shape=[jax.ShapeDtypeStruct((R, C), F32)] * 4,
        compiler_params=_params("parallel"),
    )(w, m, v, *plist)


def _adamw_ada(w, m, v, cact_t, dmod_cols, *, name):
    R, C = w.shape
    B = cact_t.shape[1]
    tr = _tile(R, 256, 8)
    row = pl.BlockSpec((tr, C), lambda i: (i, 0))

    def body(w_ref, m_ref, v_ref, c_ref, d_ref, g_ref, dl_ref, m2_ref, v2_ref):
        g = c_ref[:, 0:1] * d_ref[0:1, :]
        for b in range(1, B):
            g = g + c_ref[:, b:b + 1] * d_ref[b:b + 1, :]
        g_ref[...] = g
        dl_ref[...], m2_ref[...], v2_ref[...] = _adam_math(w_ref[...], g, m_ref[...], v_ref[...])

    return pl.pallas_call(
        body,
        name=name,
        grid=(R // tr,),
        in_specs=[row, row, row, pl.BlockSpec((tr, B), lambda i: (i, 0)), pl.BlockSpec((B, C), lambda i: (0, 0))],
        out_specs=[row] * 4,
        out_shape=[jax.ShapeDtypeStruct((R, C), F32)] * 4,
        compiler_params=_params("parallel"),
    )(w, m, v, cact_t, dmod_cols)


def _z_layout(D, q_rank, kv_rank):
    kv = SWA_KVH * SWA_HD
    orig = {}
    o = 0
    for nm, w in (("cq", q_rank), ("ckv", kv_rank), ("kr", MLA_ROPE), ("qs", D), ("ks", kv), ("vs", kv),
                  ("ga", D), ("gb", D)):
        orig[nm] = (o, w)
        o += w
    blockw = {"cq": q_rank, "ckv": kv_rank, "kr": LANES, "qs": D, "ks": kv, "vs": kv, "ga": D, "gb": D}
    best = None
    for perm in itertools.permutations(("cq", "ckv", "ks", "vs", "kr")):
        off, new = 0, {}
        for nm in ("ga", "gb", "qs") + perm:
            off = _round_up(off, blockw[nm])
            new[nm] = off
            off += blockw[nm]
        if best is None or off < best[0]:
            best = (off, new)
    total = _round_up(best[0], 1024 if best[0] > 4096 else 512)
    return orig, best[1], blockw, total, o


def _permute_w_in(w, lay):
    orig, new, blockw, total, _ = lay
    parts, at = [], 0
    for nm in sorted(new, key=new.get):
        if new[nm] > at:
            parts.append(jnp.zeros((w.shape[0], new[nm] - at), w.dtype))
        o, wd = orig[nm]
        parts.append(w[:, o:o + wd])
        if blockw[nm] > wd:
            parts.append(jnp.zeros((w.shape[0], blockw[nm] - wd), w.dtype))
        at = new[nm] + blockw[nm]
    if total > at:
        parts.append(jnp.zeros((w.shape[0], total - at), w.dtype))
    return jnp.concatenate(parts, axis=1)


def _unpermute_w_in(wp, lay):
    orig, new, _, _, _ = lay
    return jnp.concatenate([wp[:, new[nm]:new[nm] + orig[nm][1]] for nm in sorted(orig, key=lambda n: orig[n][0])],
                           axis=1)


def _assemble_dz(parts, lay, S):
    _, new, blockw, total, _ = lay
    names = sorted(new, key=new.get)
    tr = _tile(S, 256, 16)

    def body(*refs):
        o_ref = refs[-1]
        cols, at = [], 0
        for nm, ref in zip(names, refs):
            if new[nm] > at:
                cols.append(jnp.zeros((tr, new[nm] - at), BF16))
            cols.append(ref[...])
            at = new[nm] + blockw[nm]
        if total > at:
            cols.append(jnp.zeros((tr, total - at), BF16))
        o_ref[...] = jnp.concatenate(cols, axis=1)

    return pl.pallas_call(
        body,
        name="assemble_dz",
        grid=(S // tr,),
        in_specs=[pl.BlockSpec((tr, blockw[nm]), lambda i: (i, 0)) for nm in names],
        out_specs=pl.BlockSpec((tr, total), lambda i: (i, 0)),
        out_shape=jax.ShapeDtypeStruct((S, total), BF16),
        compiler_params=_params("parallel"),
    )(*[parts[nm] for nm in names])


def _unshard_cols(g):
    return jnp.transpose(g, (1, 0, 2)).reshape(g.shape[1], N_DEV * g.shape[2])


def _shard_cols(w):
    K, N = w.shape
    return jnp.transpose(w.reshape(K, N_DEV, N // N_DEV), (1, 0, 2))


def _pack(vecs, rows):
    flat = jnp.concatenate([v.reshape(-1) for v in vecs])
    return jnp.pad(flat, (0, rows * LANES - flat.shape[0])).reshape(rows, LANES)


def kernel(x, c, w_ada, b_ada, g_pre_mix, g_post_mix, w_in, g_q_lat, w_uq, g_kv_lat, w_ukv, rel_bias, sinks, w_o, g_pre_ffn, g_post_ffn, w_up, conv_w, conv_b, w_down, loss_target, m_w_ada, m_b_ada, m_g_pre_mix, m_g_post_mix, m_w_in, m_g_q_lat, m_w_uq, m_g_kv_lat, m_w_ukv, m_rel_bias, m_sinks, m_w_o, m_g_pre_ffn, m_g_post_ffn, m_w_up, m_conv_w, m_conv_b, m_w_down, v_w_ada, v_b_ada, v_g_pre_mix, v_g_post_mix, v_w_in, v_g_q_lat, v_w_uq, v_g_kv_lat, v_w_ukv, v_rel_bias, v_sinks, v_w_o, v_g_pre_ffn, v_g_post_ffn, v_w_up, v_conv_w, v_conv_b, v_w_down):
    S, D = x.shape[1], x.shape[2]
    Q_RANK, KV_RANK = g_q_lat.shape[1], g_kv_lat.shape[1]
    H_MLA = D // MLA_V
    H_SWA = D // SWA_HD
    G_SWA = H_SWA // SWA_KVH
    F2 = w_up.shape[2] * N_DEV
    F = F2 // 2
    ada_n = w_ada.shape[2]
    me = 4 * lax.axis_index("x") + 2 * lax.axis_index("y") + lax.axis_index("c")
    lay = _z_layout(D, Q_RANK, KV_RANK)
    _, zoff, _, NZ, in_cols = lay
    assert in_cols == w_in.shape[2] * N_DEV

    x2, tgt = x[0], loss_target[0]

    cw_n = conv_w.shape[2]
    small = jnp.concatenate([jnp.pad(c, ((0, 7), (0, 0))), jnp.pad(conv_w[0], ((0, 8 - CONV_WIDTH), (0, 0)))], axis=1)
    small_all = _all_gather(small, name="ag_cond", in_vmem=True)
    c_all = small_all[:, 0, :D]
    cw_full = _unshard_cols(small_all[:, :CONV_WIDTH, D:])
    b_cols = lax.dynamic_slice_in_dim(b_ada, me * ada_n, ada_n, axis=1)
    c_act, mod_cols = _ada_fwd(c_all, w_ada[0], b_cols, name="ada_fwd")
    mod_all = _all_gather(mod_cols, name="ag_mod", in_vmem=True)
    mod_me = lax.dynamic_index_in_dim(mod_all, me, axis=1, keepdims=False).reshape(1, N_DEV * ada_n)
    sh1, sc1, gt1, sh2, sc2, gt2 = [mod_me[:, k * D:(k + 1) * D] for k in range(6)]

    w_in_p = _permute_w_in(_unshard_cols(_all_gather(w_in[0].astype(BF16), name="ag_w_in", in_vmem=False)), lay)

    h1 = _prenorm(x2, g_pre_mix, sc1, sh1, name="prenorm_mix")
    z, (uq_g, ukv_g, o_g) = _matmul(h1, w_in_p, mode="nn", out_dtype=BF16, name="mm_in",
                                    comm=[("gather", w_uq[0].astype(BF16)), ("gather", w_ukv[0].astype(BF16)),
                                          ("gather", w_o[0].astype(BF16))])
    w_uq_p = jnp.pad(_unshard_cols(uq_g).reshape(Q_RANK, H_MLA, MLA_QK), ((0, 0), (0, 0), (0, MLA_QK_PAD - MLA_QK))
                     ).reshape(Q_RANK, H_MLA * MLA_QK_PAD)
    w_ukv_f = _unshard_cols(ukv_g)
    w_o_f = o_g.reshape(D, D)
    cqn = _prenorm(z, g_q_lat, None, None, name="norm_cq", off=zoff["cq"], width=Q_RANK)
    ckvn = _prenorm(z, g_kv_lat, None, None, name="norm_ckv", off=zoff["ckv"], width=KV_RANK)
    q_raw = _matmul(cqn, w_uq_p, mode="nn", out_dtype=BF16, name="mm_uq")
    kv = _matmul(ckvn, w_ukv_f, mode="nn", out_dtype=BF16, name="mm_ukv")
    tab_k = _rope_tables(S, LANES, 0)
    krr = _rope(z, tab_k, heads=1, width=LANES, transpose=False, name="rope_k", off=zoff["kr"])
    (o_a, lse, Qr), (up_g,) = _flash_fwd(q_raw, kv, krr, tab_k, heads=H_MLA, name="mla_fwd",
                                        comm=[("gather", w_up[0].astype(BF16))])
    w_up_f = _unshard_cols(up_g)

    bucket, valid = _t5_bucket_table()
    onehot = (jnp.asarray(bucket).reshape(-1, 1) == jnp.arange(LANES)[None, :]).astype(F32)
    rb_pad = jnp.pad(rel_bias, ((0, LANES - REL_BUCKETS), (0, LANES - H_SWA)))
    bias_t = _matmul(onehot, rb_pad, mode="nn", out_dtype=F32, name="bias_table", tm=2048, precision=HIGHEST)
    bias_full = jnp.transpose(bias_t[:, :H_SWA].reshape(BLOCK, 2 * BLOCK, H_SWA), (2, 0, 1))
    bias_full = jnp.where(jnp.asarray(valid)[None], bias_full, NEG)
    bias_full = jnp.transpose(bias_full.reshape(SWA_KVH, G_SWA, BLOCK, 2 * BLOCK), (0, 3, 1, 2)
                              ).reshape(SWA_KVH, 2 * BLOCK, G_SWA * BLOCK)
    sink_rows = jnp.broadcast_to(sinks.reshape(SWA_KVH, G_SWA, 1), (SWA_KVH, G_SWA, BLOCK)
                                 ).reshape(SWA_KVH, 1, G_SWA * BLOCK)
    kvw = SWA_KVH * SWA_HD

    def heads_first(t, n):
        return jnp.transpose(t.reshape(S, n, SWA_HD), (1, 0, 2))

    def heads_last(t):
        return jnp.transpose(t, (1, 0, 2)).reshape(S, t.shape[0] * SWA_HD)

    def queries_first(t):
        t = t.reshape(SWA_KVH, S // BLOCK, SWA_HD, G_SWA, BLOCK)
        return jnp.transpose(t, (1, 4, 0, 3, 2)).reshape(S, H_SWA * SWA_HD)

    ks_h = heads_first(z[:, zoff["ks"]:zoff["ks"] + kvw], SWA_KVH)
    vs_h = heads_first(z[:, zoff["vs"]:zoff["vs"] + kvw], SWA_KVH)
    o_b_h, _ = _swa_fwd(z, zoff["qs"], ks_h, vs_h, bias_full, sink_rows, name="swa_fwd")
    o_b = queries_first(o_b_h)

    mixin = _gate_mix(z, o_a, o_b, D=D, off_a=zoff["ga"], off_b=zoff["gb"], name="gate_mix")
    mix = _matmul(mixin, w_o_f, mode="nn", out_dtype=F32, name="mm_o")
    x1 = _postnorm_res(x2, mix, gt1, g_post_mix, name="postnorm_mix")

    h2 = _prenorm(x1, g_pre_ffn, sc2, sh2, name="prenorm_ffn")
    up, (down_g,) = _matmul(h2, w_up_f, mode="nn", out_dtype=BF16, name="mm_up",
                            comm=[("gather", w_down[0].astype(BF16))])
    w_down_f = down_g.reshape(F, D)
    act = _conv_gate(up, cw_full, conv_b, name="conv_gate")
    y = _matmul(act, w_down_f, mode="nn", out_dtype=F32, name="mm_down")
    loss_part, dout, dy, dgt2, dg_post_ffn = _final_loss(x1, y, tgt, gt2, g_post_ffn, name="final_loss")
    loss = lax.psum(loss_part[0, 0], ("x", "y", "c"))

    dw_down = _matmul(act, dy, mode="tn", out_dtype=BF16, name="mm_down_dw")
    dact = _matmul(dy, w_down_f, mode="nt", out_dtype=BF16, name="mm_down_dx")
    (dup, dcw, dcb), (got_down,) = _conv_gate_bwd(up, dact, cw_full, conv_b, name="conv_gate_bwd",
                                                  comm=[("scatter", dw_down.reshape(N_DEV, F // N_DEV, D))])
    dcw = jnp.transpose(dcw, (1, 0, 2)).reshape(CONV_WIDTH, F2)
    dcb = dcb.reshape(1, F2)
    dw_up = _matmul(h2, dup, mode="tn", out_dtype=BF16, name="mm_up_dw", shard_out=True, halves=True)
    dh2 = _matmul(dup, w_up_f, mode="nt", out_dtype=F32, name="mm_up_dx", halves=True)
    dx1, dg_pre_ffn, dsc2, dsh2 = _prenorm_bwd(x1, dh2, dout, g_pre_ffn, sc2, name="prenorm_ffn_bwd", out_dtype=F32)

    dmix, dgt1, dg_post_mix = _postnorm_bwd(dx1, mix, gt1, g_post_mix, name="postnorm_mix_bwd")
    dw_o = _matmul(mixin, dmix, mode="tn", out_dtype=BF16, name="mm_o_dw")
    dmixin = _matmul(dmix, w_o_f, mode="nt", out_dtype=BF16, name="mm_o_dx")
    dga, dgb, do_a, do_b = _gate_mix_bwd(dmixin, z, o_a, o_b, D=D, off_a=zoff["ga"], off_b=zoff["gb"],
                                         name="gate_mix_bwd")
    (dq_raw, dkv, dkr_parts), (got_up,) = _flash_bwd(Qr, kv, krr, do_a, o_a, lse, tab_k, heads=H_MLA, name="mla_bwd",
                                                     comm=[("scatter", dw_up)])
    dkr = _shared_rope_grad(dkr_parts, tab_k, name="rope_k_bwd")
    dw_uq_p = _matmul(cqn, dq_raw, mode="tn", out_dtype=BF16, name="mm_uq_dw")
    dcqn = _matmul(dq_raw, w_uq_p, mode="nt", out_dtype=F32, name="mm_uq_dx")
    dw_ukv = _matmul(ckvn, dkv, mode="tn", out_dtype=BF16, name="mm_ukv_dw", shard_out=True)
    dckvn = _matmul(dkv, w_ukv_f, mode="nt", out_dtype=F32, name="mm_ukv_dx")
    dw_uq = dw_uq_p.reshape(Q_RANK, H_MLA, MLA_QK_PAD)[:, :, :MLA_QK].reshape(Q_RANK, H_MLA * MLA_QK)

    dcw_parts = jnp.pad(_shard_cols(dcw), ((0, 0), (0, 16 - CONV_WIDTH), (0, 0)))
    (dqs_h, dks_h, dvs_h, dbias, dsink), (got_o, got_cw, got_uq, got_ukv) = _swa_bwd(
        z, zoff["qs"], ks_h, vs_h, bias_full, sink_rows, do_b, name="swa_bwd",
        comm=[("scatter", dw_o.reshape(N_DEV, D // N_DEV, D)), ("scatter", dcw_parts),
              ("scatter", _shard_cols(dw_uq)), ("scatter", dw_ukv)])
    dbias = jnp.transpose(dbias.reshape(SWA_KVH, 2 * BLOCK, G_SWA, BLOCK), (0, 2, 3, 1))
    drel_t = _matmul(dbias.reshape(H_SWA, BLOCK * 2 * BLOCK), onehot, mode="nn", out_dtype=F32, name="bias_grad",
                     tk=4096, precision=HIGHEST)
    d_rel_bias = jnp.transpose(drel_t[:, :REL_BUCKETS])
    d_sinks = jnp.sum(dsink.reshape(SWA_KVH, G_SWA, BLOCK), axis=-1).reshape(1, H_SWA)

    dcq, dg_q = _prenorm_bwd(z, dcqn, None, g_q_lat, None, name="norm_cq_bwd", out_dtype=BF16,
                             off=zoff["cq"], width=Q_RANK)
    dckv, dg_kv = _prenorm_bwd(z, dckvn, None, g_kv_lat, None, name="norm_ckv_bwd", out_dtype=BF16,
                               off=zoff["ckv"], width=KV_RANK)
    dz = _assemble_dz({"ga": dga, "gb": dgb, "qs": queries_first(dqs_h), "cq": dcq, "ckv": dckv,
                       "ks": heads_last(dks_h), "vs": heads_last(dvs_h), "kr": dkr}, lay, S)
    dw_in_a = _matmul(h1, dz, mode="tn", out_dtype=BF16, name="mm_in_dw_a", m_range=(0, D // 2))
    dw_in_b, (got_in_a,) = _matmul(h1, dz, mode="tn", out_dtype=BF16, name="mm_in_dw_b", m_range=(D // 2, D // 2),
                                   comm=[("scatter", _shard_cols(_unpermute_w_in(dw_in_a, lay)))])
    dh1, (got_in_b,) = _matmul(dz, w_in_p, mode="nt", out_dtype=F32, name="mm_in_dx",
                               comm=[("scatter", _shard_cols(_unpermute_w_in(dw_in_b, lay)))])
    grad_x, dg_pre_mix, dsc1, dsh1 = _prenorm_bwd(x2, dh1, dx1, g_pre_mix, sc1, name="prenorm_mix_bwd",
                                                  out_dtype=F32)
    dmod = jnp.concatenate([dsh1, dsc1, dgt1, dsh2, dsc2, dgt2], axis=1)

    small_names = ["b_ada", "g_pre_mix", "g_post_mix", "g_q_lat", "g_kv_lat", "rel_bias", "sinks", "g_pre_ffn",
                   "g_post_ffn", "conv_b"]
    small_w = [b_ada, g_pre_mix, g_post_mix, g_q_lat, g_kv_lat, rel_bias, sinks, g_pre_ffn, g_post_ffn, conv_b]
    small_m = [m_b_ada, m_g_pre_mix, m_g_post_mix, m_g_q_lat, m_g_kv_lat, m_rel_bias, m_sinks, m_g_pre_ffn,
               m_g_post_ffn, m_conv_b]
    small_v = [v_b_ada, v_g_pre_mix, v_g_post_mix, v_g_q_lat, v_g_kv_lat, v_rel_bias, v_sinks, v_g_pre_ffn,
               v_g_post_ffn, v_conv_b]
    small_g = [dmod, dg_pre_mix, dg_post_mix, dg_q, dg_kv, d_rel_bias, d_sinks, dg_pre_ffn, dg_post_ffn, dcb]
    n_small = sum(int(np.prod(w.shape)) for w in small_w)
    rows = _round_up(-(-n_small // LANES), 16)
    parts_small = _all_gather(_pack(small_g, rows), name="ag_small_grads", in_vmem=True)
    sg, sd, sm, sv = _adamw(_pack(small_w, rows), _pack(small_m, rows), _pack(small_v, rows), parts_small,
                            name="adamw_small")

    def unpack(packed):
        flat, out, at = packed.reshape(-1), {}, 0
        for nm, w in zip(small_names, small_w):
            n = int(np.prod(w.shape))
            out[nm] = flat[at:at + n].reshape(w.shape)
            at += n
        return out

    small_out = [unpack(t) for t in (sg, sd, sm, sv)]

    dmod_all = parts_small.reshape(N_DEV, rows * LANES)[:, :6 * D]
    dmod_cols = lax.dynamic_slice_in_dim(dmod_all, me * ada_n, ada_n, axis=1)
    ada_out = _adamw_ada(w_ada[0], m_w_ada[0], v_w_ada[0], jnp.transpose(c_act), dmod_cols, name="adamw_w_ada")

    def owner_update(got, w, m, v, name):
        shp = w.shape
        w2, m2, v2 = (t.reshape(shp[-2], shp[-1]) for t in (w, m, v))
        return [t.reshape(shp) for t in _adamw(w2, m2, v2, got, name="adamw_" + name)]

    def pad_rows(t):
        return jnp.pad(t[0], ((0, 16 - CONV_WIDTH), (0, 0)))

    big = {
        "w_in": owner_update([got_in_a, got_in_b], w_in, m_w_in, v_w_in, "w_in"),
        "w_uq": owner_update(got_uq, w_uq, m_w_uq, v_w_uq, "w_uq"),
        "w_ukv": owner_update(got_ukv, w_ukv, m_w_ukv, v_w_ukv, "w_ukv"),
        "w_o": owner_update(got_o, w_o, m_w_o, v_w_o, "w_o"),
        "w_up": owner_update(got_up, w_up, m_w_up, v_w_up, "w_up"),
        "w_down": owner_update(got_down, w_down, m_w_down, v_w_down, "w_down"),
    }
    cw_upd = _adamw(pad_rows(conv_w), pad_rows(m_conv_w), pad_rows(v_conv_w), got_cw, name="adamw_conv_w")
    big["conv_w"] = [t[:CONV_WIDTH].reshape(conv_w.shape) for t in cw_upd]
    big["w_ada"] = [t.reshape(w_ada.shape) for t in ada_out]

    order = ["w_ada", "b_ada", "g_pre_mix", "g_post_mix", "w_in", "g_q_lat", "w_uq", "g_kv_lat", "w_ukv", "rel_bias",
             "sinks", "w_o", "g_pre_ffn", "g_post_ffn", "w_up", "conv_w", "conv_b", "w_down"]
    outs = [loss, grad_x.reshape(x.shape)]
    for kind in range(4):
        for nm in order:
            outs.append(big[nm][kind] if nm in big else small_out[kind][nm])
    return tuple(outs)
```

```python
import functools
import itertools
import math

import numpy as np

import jax
import jax.numpy as jnp
from jax import lax
from jax.experimental import pallas as pl
from jax.experimental.pallas import tpu as pltpu

F32 = jnp.float32
BF16 = jnp.bfloat16

N_DEV = 8
MLA_NOPE = 128
MLA_ROPE = 64
MLA_V = 128
MLA_QK = MLA_NOPE + MLA_ROPE
MLA_QK_PAD = 256
ROPE_HALF = MLA_ROPE // 2
ROPE_THETA = 10000.0
SWA_HD = 64
SWA_KVH = 4
WINDOW = 128
BLOCK = 128
REL_BUCKETS = 32
REL_MAX_DIST = 128
CONV_WIDTH = 3
EPS = 1e-6
NEG = -1e30
ADAM_LR = 0.001
ADAM_B1 = 0.9
ADAM_B2 = 0.999
ADAM_EPS = 1e-08
ADAM_WD = 0.01
ADAM_STEP = 10
LANES = 128
HALO = 16
MESH = pl.DeviceIdType.MESH
HIGHEST = lax.Precision.HIGHEST

NN = (((1,), (0,)), ((), ()))
NT = (((1,), (1,)), ((), ()))
TN = (((0,), (0,)), ((), ()))


def _tile(n, pref, align=LANES):
    if n <= pref:
        return n
    t = (pref // align) * align
    while t >= align:
        if n % t == 0:
            return t
        t -= align
    return n


def _round_up(n, m):
    return (n + m - 1) // m * m


def _params(*sem):
    return pltpu.CompilerParams(dimension_semantics=sem)


def _sigmoid(x):
    return 1.0 / (1.0 + jnp.exp(-x))


def _my_place():
    return lax.axis_index("x"), lax.axis_index("y"), lax.axis_index("c")


def _all_gather(x, *, name, in_vmem):
    space = pltpu.VMEM if in_vmem else pl.ANY

    def body(x_ref, out_ref, send_sems, recv_sems, local_sem):
        x_, y_, c_ = _my_place()
        me, sibling = (x_, y_, c_), (x_, y_, 1 - c_)
        chips = [(1 - x_, y_), (x_, 1 - y_), (1 - x_, 1 - y_)]

        def slot(px, py, pc):
            return out_ref.at[4 * px + 2 * py + pc]

        def copy(k, block, to, src=None):
            return pltpu.make_async_remote_copy(
                src_ref=slot(*block) if src is None else src,
                dst_ref=slot(*block),
                send_sem=send_sems.at[k],
                recv_sem=recv_sems.at[k],
                device_id=to,
                device_id_type=MESH,
            )

        mine = pltpu.make_async_copy(x_ref, slot(*me), local_sem)
        mine.start()
        first = [copy(0, me, sibling, src=x_ref)]
        first += [copy(1 + j, me, (*chip, c_), src=x_ref) for j, chip in enumerate(chips)]
        for cp in first:
            cp.start()
        passed = [copy(4 + j, (*chip, c_), sibling) for j, chip in enumerate(chips)]
        for j, chip in enumerate(chips):
            copy(1 + j, (*chip, c_), me).wait_recv()
            passed[j].start()
        copy(0, sibling, me).wait_recv()
        for j, chip in enumerate(chips):
            copy(4 + j, (*chip, 1 - c_), me).wait_recv()
        for cp in first + passed:
            cp.wait_send()
        mine.wait()

    return pl.pallas_call(
        body,
        name=name,
        out_shape=jax.ShapeDtypeStruct((N_DEV,) + x.shape, x.dtype),
        in_specs=[pl.BlockSpec(memory_space=space)],
        out_specs=pl.BlockSpec(memory_space=space),
        scratch_shapes=[
            pltpu.SemaphoreType.DMA((7,)),
            pltpu.SemaphoreType.DMA((7,)),
            pltpu.SemaphoreType.DMA,
        ],
    )(x)


class _Exchange:
    def __init__(self, kind, x_ref, out_ref, send_sems, recv_sems, local_sems, t):
        x_, y_, c_ = _my_place()
        me = 4 * x_ + 2 * y_ + c_

        def pair(k, src, dst, to):
            return pltpu.make_async_remote_copy(src_ref=src, dst_ref=dst, send_sem=send_sems.at[7 * t + k],
                                                recv_sem=recv_sems.at[7 * t + k], device_id=to, device_id_type=MESH)

        none = lambda: []
        if kind == "scatter":
            peers = [(x_ ^ ((r >> 2) & 1), y_ ^ ((r >> 1) & 1), c_ ^ (r & 1)) for r in range(1, N_DEV)]
            self.at_start = lambda: [pair(k, x_ref.at[4 * px + 2 * py + pc], out_ref.at[me], (px, py, pc))
                                     for k, (px, py, pc) in enumerate(peers)]
            self.relay_after, self.at_relay = none, none
            self.arrivals = self.at_start
            self.own = lambda: pltpu.make_async_copy(x_ref.at[me], out_ref.at[me], local_sems.at[t])
        else:
            sibling = (x_, y_, 1 - c_)
            chips = list(enumerate([(1 - x_, y_), (x_, 1 - y_), (1 - x_, 1 - y_)]))

            def slot(px, py, pc):
                return out_ref.at[4 * px + 2 * py + pc]

            mine = slot(x_, y_, c_)
            self.at_start = lambda: ([pair(0, x_ref, mine, sibling)]
                                     + [pair(1 + j, x_ref, mine, (*chip, c_)) for j, chip in chips])
            self.relay_after = lambda: [pair(1 + j, slot(*chip, c_), slot(*chip, c_), (*chip, c_)) for j, chip in chips]
            self.at_relay = lambda: [pair(4 + j, slot(*chip, c_), slot(*chip, c_), sibling) for j, chip in chips]
            self.arrivals = lambda: ([pair(0, slot(*sibling), slot(*sibling), sibling)]
                                     + [pair(4 + j, slot(*chip, 1 - c_), slot(*chip, 1 - c_), sibling)
                                        for j, chip in chips])
            self.own = lambda: pltpu.make_async_copy(x_ref, mine, local_sems.at[t])

    def start(self):
        self.own().start()
        for cp in self.at_start():
            cp.start()

    def relay(self):
        for landed, onward in zip(self.relay_after(), self.at_relay()):
            landed.wait_recv()
            onward.start()

    def finish(self):
        for cp in self.arrivals():
            cp.wait_recv()
        for cp in self.at_start() + self.at_relay():
            cp.wait_send()
        self.own().wait()


RELAY_AT = 0.85


def _call(body, *, name, grid, in_specs, out_specs, out_shape, args, scratch_shapes=(), sem=(), comm=(), prefetch=()):
    n_pf = len(prefetch)

    def launch(fn, ins, outs, shapes, scratch, semantics, operands):
        spec = pltpu.PrefetchScalarGridSpec(num_scalar_prefetch=n_pf, grid=grid, in_specs=ins, out_specs=outs,
                                            scratch_shapes=scratch)
        return pl.pallas_call(fn, name=name, grid_spec=spec, out_shape=shapes,
                              compiler_params=_params(*semantics))(*prefetch, *operands)

    if not comm:
        return list(launch(body, list(in_specs), list(out_specs), list(out_shape), list(scratch_shapes), sem, args)), []
    n_in, n_out, n_c, n_s = len(in_specs), len(out_specs), len(comm), len(scratch_shapes)
    kinds = [kind for kind, _ in comm]
    hbm = pl.BlockSpec(memory_space=pl.ANY)

    def wrapped(*refs):
        tables, refs = refs[:n_pf], refs[n_pf:]
        ins, cin = refs[:n_in], refs[n_in:n_in + n_c]
        at = n_in + n_c
        outs, cout = refs[at:at + n_out], refs[at + n_out:at + n_out + n_c]
        scr = refs[at + n_out + n_c:at + n_out + n_c + n_s]
        send, recv, local = refs[-3:]
        step = 0
        for a, g in enumerate(grid):
            step = step * g + pl.program_id(a)
        n_steps = int(np.prod(grid))

        def exchanges():
            return [_Exchange(kinds[t], cin[t], cout[t], send, recv, local, t) for t in range(n_c)]

        @pl.when(step == 0)
        def _():
            for ex in exchanges():
                ex.start()

        body(*tables, *ins, *outs, *scr)

        @pl.when(step == min(int(RELAY_AT * n_steps), n_steps - 1))
        def _():
            for ex in exchanges():
                ex.relay()

        @pl.when(step == n_steps - 1)
        def _():
            for ex in exchanges():
                ex.finish()

    c_shapes = [jax.ShapeDtypeStruct(((N_DEV,) + a.shape) if kind == "gather" else a.shape, a.dtype)
                for kind, a in comm]
    sems = [pltpu.SemaphoreType.DMA((7 * n_c,)), pltpu.SemaphoreType.DMA((7 * n_c,)), pltpu.SemaphoreType.DMA((n_c,))]
    res = launch(wrapped, list(in_specs) + [hbm] * n_c, list(out_specs) + [hbm] * n_c, list(out_shape) + c_shapes,
                 list(scratch_shapes) + sems, ["arbitrary"] * len(grid), (*args, *[a for _, a in comm]))
    return list(res[:n_out]), list(res[n_out:])


def _matmul(a, b, *, mode, out_dtype, name, tm=1024, tn=1024, tk=2816, precision=None, comm=(), shard_out=False,
            halves=False, m_range=None):
    if mode == "nn":
        (M, K), (K2, N) = a.shape, b.shape
    elif mode == "nt":
        (M, K), (N, K2) = (a.shape[1], 2 * a.shape[2]) if halves else a.shape, b.shape
    else:
        (K, M), (K2, N) = a.shape, (b.shape[1], 2 * b.shape[2]) if halves else b.shape
    assert K == K2, (a.shape, b.shape, mode)
    m_off = 0
    if m_range is not None:
        m_off, M = m_range
    tm = _tile(M, tm, LANES if mode == "tn" else 16)
    tk = _tile(K // 2 if halves and mode == "nt" else K, tk)
    tn = _tile(N // N_DEV, max(tn, 1408)) if shard_out else _tile(N // 2 if halves and mode == "tn" else N, tn)
    nk = K // tk
    m_off //= tm
    if mode == "tn":
        a_spec = pl.BlockSpec((tk, tm), lambda i, j, k: (k, i + m_off))
    elif halves:
        a_spec = pl.BlockSpec((None, tm, tk), lambda i, j, k: (k // (nk // 2), i, k % (nk // 2)))
    else:
        a_spec = pl.BlockSpec((tm, tk), lambda i, j, k: (i, k))
    if mode == "nt":
        b_spec = pl.BlockSpec((tn, tk), lambda i, j, k: (j, k))
    elif halves:
        nj = N // tn
        b_spec = pl.BlockSpec((None, tk, tn), lambda i, j, k: (j // (nj // 2), k, j % (nj // 2)))
    else:
        b_spec = pl.BlockSpec((tk, tn), lambda i, j, k: (k, j))
    dn = {"nn": NN, "nt": NT, "tn": TN}[mode]
    if shard_out:
        per = N // N_DEV // tn
        o_spec = pl.BlockSpec((None, tm, tn), lambda i, j, k: (j // per, i, j % per))
        o_shape = jax.ShapeDtypeStruct((N_DEV, M, N // N_DEV), out_dtype)
    else:
        o_spec = pl.BlockSpec((tm, tn), lambda i, j, k: (i, j))
        o_shape = jax.ShapeDtypeStruct((M, N), out_dtype)

    def product(a_ref, b_ref):
        return lax.dot_general(a_ref[...], b_ref[...], dn, preferred_element_type=F32, precision=precision)

    def body_one(a_ref, b_ref, o_ref):
        o_ref[...] = product(a_ref, b_ref).astype(o_ref.dtype)

    def body_acc(a_ref, b_ref, o_ref, acc_ref):
        k = pl.program_id(2)

        @pl.when(k == 0)
        def _():
            acc_ref[...] = product(a_ref, b_ref)

        @pl.when(k > 0)
        def _():
            acc_ref[...] += product(a_ref, b_ref)

        @pl.when(k == nk - 1)
        def _():
            o_ref[...] = acc_ref[...].astype(o_ref.dtype)

    outs, moved = _call(
        body_one if nk == 1 else body_acc,
        name=name,
        grid=(M // tm, N // tn, nk),
        in_specs=[a_spec, b_spec],
        out_specs=[o_spec],
        out_shape=[o_shape],
        scratch_shapes=[] if nk == 1 else [pltpu.VMEM((tm, tn), F32)],
        sem=("parallel", "parallel", "arbitrary"),
        args=(a, b),
        comm=comm,
    )
    return (outs[0], moved) if comm else outs[0]


def _rstd(xf):
    return lax.rsqrt(jnp.mean(xf * xf, axis=-1, keepdims=True) + EPS)


def _col_view(width, off):
    assert off % width == 0
    return off // width


def _prenorm(x, g, sc, sh, *, name, off=0, width=None):
    S = x.shape[0]
    W = x.shape[1] if width is None else width
    cb = _col_view(W, off)
    tr = _tile(S, 512, 16)
    mod = sc is not None
    vec = pl.BlockSpec((1, W), lambda i: (0, 0))

    def body(*refs):
        if mod:
            x_ref, g_ref, sc_ref, sh_ref, o_ref = refs
        else:
            x_ref, g_ref, o_ref = refs
        xf = x_ref[...].astype(F32)
        y = xf * _rstd(xf) * g_ref[...]
        if mod:
            y = y * (1.0 + sc_ref[...]) + sh_ref[...]
        o_ref[...] = y.astype(o_ref.dtype)

    args = (x, g, sc, sh) if mod else (x, g)
    return pl.pallas_call(
        body,
        name=name,
        grid=(S // tr,),
        in_specs=[pl.BlockSpec((tr, W), lambda i: (i, cb))] + [vec] * (len(args) - 1),
        out_specs=pl.BlockSpec((tr, W), lambda i: (i, 0)),
        out_shape=jax.ShapeDtypeStruct((S, W), BF16),
        compiler_params=_params("parallel"),
    )(*args)


def _prenorm_bwd(x, dh, dres, g, sc, *, name, out_dtype, off=0, width=None):
    S = x.shape[0]
    W = x.shape[1] if width is None else width
    cb = _col_view(W, off)
    tr = _tile(S, 256, 16)
    mod = sc is not None
    res = dres is not None
    vec = pl.BlockSpec((1, W), lambda i: (0, 0))
    row = pl.BlockSpec((tr, W), lambda i: (i, 0))

    def body(*refs):
        it = iter(refs)
        x_ref, dh_ref = next(it), next(it)
        dres_ref = next(it) if res else None
        g_ref = next(it)
        sc_ref = next(it) if mod else None
        dx_ref, dg_ref = next(it), next(it)
        dsc_ref, dsh_ref = (next(it), next(it)) if mod else (None, None)
        i = pl.program_id(0)

        @pl.when(i == 0)
        def _():
            dg_ref[...] = jnp.zeros_like(dg_ref)
            if mod:
                dsc_ref[...] = jnp.zeros_like(dsc_ref)
                dsh_ref[...] = jnp.zeros_like(dsh_ref)

        xf = x_ref[...].astype(F32)
        r = _rstd(xf)
        xn = xf * r
        dhf = dh_ref[...].astype(F32)
        gv = g_ref[...]
        if mod:
            one_sc = 1.0 + sc_ref[...]
            dsh_ref[...] += jnp.sum(dhf, axis=0, keepdims=True)
            dsc_ref[...] += jnp.sum(dhf * (xn * gv), axis=0, keepdims=True)
            dg_ref[...] += jnp.sum(dhf * xn * one_sc, axis=0, keepdims=True)
            dxn = dhf * (gv * one_sc)
        else:
            dg_ref[...] += jnp.sum(dhf * xn, axis=0, keepdims=True)
            dxn = dhf * gv
        dx = r * (dxn - xn * jnp.mean(dxn * xn, axis=-1, keepdims=True))
        if res:
            dx = dx + dres_ref[...]
        dx_ref[...] = dx.astype(dx_ref.dtype)

    args = [x, dh] + ([dres] if res else []) + [g] + ([sc] if mod else [])
    in_specs = [pl.BlockSpec((tr, W), lambda i: (i, cb)), row] + ([row] if res else []) + [vec] + ([vec] if mod else [])
    n_vec = 3 if mod else 1
    outs = pl.pallas_call(
        body,
        name=name,
        grid=(S // tr,),
        in_specs=in_specs,
        out_specs=[row] + [vec] * n_vec,
        out_shape=[jax.ShapeDtypeStruct((S, W), out_dtype)] + [jax.ShapeDtypeStruct((1, W), F32)] * n_vec,
        compiler_params=_params("arbitrary"),
    )(*args)
    return outs


def _postnorm_res(x, y, gt, g, *, name):
    S, D = x.shape
    tr = _tile(S, 512, 8)
    row = pl.BlockSpec((tr, D), lambda i: (i, 0))
    vec = pl.BlockSpec((1, D), lambda i: (0, 0))

    def body(x_ref, y_ref, gt_ref, g_ref, o_ref):
        yf = y_ref[...]
        o_ref[...] = x_ref[...] + gt_ref[...] * (yf * _rstd(yf) * g_ref[...])

    return pl.pallas_call(
        body,
        name=name,
        grid=(S // tr,),
        in_specs=[row, row, vec, vec],
        out_specs=row,
        out_shape=jax.ShapeDtypeStruct((S, D), F32),
        compiler_params=_params("parallel"),
    )(x, y, gt, g)


def _postnorm_bwd(dx1, y, gt, g, *, name):
    S, D = y.shape
    tr = _tile(S, 256, 16)
    row = pl.BlockSpec((tr, D), lambda i: (i, 0))
    vec = pl.BlockSpec((1, D), lambda i: (0, 0))

    def body(dx_ref, y_ref, gt_ref, g_ref, dy_ref, dgt_ref, dg_ref):
        @pl.when(pl.program_id(0) == 0)
        def _():
            dgt_ref[...] = jnp.zeros_like(dgt_ref)
            dg_ref[...] = jnp.zeros_like(dg_ref)

        yf = y_ref[...]
        r = _rstd(yf)
        yn = yf * r
        d = dx_ref[...]
        gtv, gv = gt_ref[...], g_ref[...]
        dgt_ref[...] += jnp.sum(d * (yn * gv), axis=0, keepdims=True)
        dg_ref[...] += jnp.sum(d * gtv * yn, axis=0, keepdims=True)
        dyn = d * (gtv * gv)
        dy_ref[...] = (r * (dyn - yn * jnp.mean(dyn * yn, axis=-1, keepdims=True))).astype(dy_ref.dtype)

    return pl.pallas_call(
        body,
        name=name,
        grid=(S // tr,),
        in_specs=[row, row, vec, vec],
        out_specs=[row, vec, vec],
        out_shape=[jax.ShapeDtypeStruct((S, D), BF16), jax.ShapeDtypeStruct((1, D), F32),
                   jax.ShapeDtypeStruct((1, D), F32)],
        compiler_params=_params("arbitrary"),
    )(dx1, y, gt, g)


def _final_loss(x1, y, target, gt, g, *, name):
    S, D = y.shape
    tr = _tile(S, 256, 16)
    row = pl.BlockSpec((tr, D), lambda i: (i, 0))
    vec = pl.BlockSpec((1, D), lambda i: (0, 0))
    one = pl.BlockSpec((1, LANES), lambda i: (0, 0))

    def body(x_ref, y_ref, t_ref, gt_ref, g_ref, loss_ref, dout_ref, dy_ref, dgt_ref, dg_ref):
        @pl.when(pl.program_id(0) == 0)
        def _():
            loss_ref[...] = jnp.zeros_like(loss_ref)
            dgt_ref[...] = jnp.zeros_like(dgt_ref)
            dg_ref[...] = jnp.zeros_like(dg_ref)

        yf = y_ref[...]
        r = _rstd(yf)
        yn = yf * r
        gtv, gv = gt_ref[...], g_ref[...]
        out = x_ref[...] + gtv * (yn * gv)
        diff = out - t_ref[...]
        per_tok = jnp.mean(diff * diff, axis=-1, keepdims=True)
        loss_ref[...] += 0.5 * jnp.sum(per_tok, axis=0, keepdims=True)
        d = diff / D
        dout_ref[...] = d
        dgt_ref[...] += jnp.sum(d * (yn * gv), axis=0, keepdims=True)
        dg_ref[...] += jnp.sum(d * gtv * yn, axis=0, keepdims=True)
        dyn = d * (gtv * gv)
        dy_ref[...] = (r * (dyn - yn * jnp.mean(dyn * yn, axis=-1, keepdims=True))).astype(dy_ref.dtype)

    return pl.pallas_call(
        body,
        name=name,
        grid=(S // tr,),
        in_specs=[row, row, row, vec, vec],
        out_specs=[one, row, row, vec, vec],
        out_shape=[jax.ShapeDtypeStruct((1, LANES), F32), jax.ShapeDtypeStruct((S, D), F32),
                   jax.ShapeDtypeStruct((S, D), BF16), jax.ShapeDtypeStruct((1, D), F32),
                   jax.ShapeDtypeStruct((1, D), F32)],
        compiler_params=_params("arbitrary"),
    )(x1, y, target, gt, g)


def _ada_fwd(c_all, w_local, b_cols, *, name):
    B, D = c_all.shape
    N = w_local.shape[1]
    tn = _tile(N, 512)

    def body(c_ref, w_ref, b_ref, ca_ref, mod_ref):
        cv = c_ref[...]
        ca = cv * _sigmoid(cv)
        ca_ref[...] = ca
        mod_ref[...] = jnp.dot(ca, w_ref[...], preferred_element_type=F32, precision=HIGHEST) + b_ref[...]

    return pl.pallas_call(
        body,
        name=name,
        grid=(N // tn,),
        in_specs=[pl.BlockSpec((B, D), lambda j: (0, 0)), pl.BlockSpec((D, tn), lambda j: (0, j)),
                  pl.BlockSpec((1, tn), lambda j: (0, j))],
        out_specs=[pl.BlockSpec((B, D), lambda j: (0, 0)), pl.BlockSpec((B, tn), lambda j: (0, j))],
        out_shape=[jax.ShapeDtypeStruct((B, D), F32), jax.ShapeDtypeStruct((B, N), F32)],
        compiler_params=_params("arbitrary"),
    )(c_all, w_local, b_cols)


def _rope_tables(S, width, lane_off):
    pos = jnp.arange(S, dtype=F32)
    inv = ROPE_THETA ** (-jnp.arange(0, MLA_ROPE, 2, dtype=F32) / MLA_ROPE)
    ang = pos[:, None] * inv[None, :]
    ang = jnp.concatenate([ang, ang], axis=-1)
    cos, sin = jnp.cos(ang), jnp.sin(ang)
    first = (jnp.arange(MLA_ROPE) < ROPE_HALF)[None, :]
    sa = jnp.where(first, -sin, 0.0)
    sb = jnp.where(first, 0.0, sin)

    def place(t, fill):
        return jnp.pad(t, ((0, 0), (lane_off, width - lane_off - MLA_ROPE)), constant_values=fill)

    return place(cos, 1.0), place(sa, 0.0), place(sb, 0.0)


def _rope_apply(x, cos, sa, sb, width, transpose):
    if transpose:
        return x * cos + pltpu.roll(x * sa, ROPE_HALF, 1) + pltpu.roll(x * sb, width - ROPE_HALF, 1)
    return x * cos + pltpu.roll(x, width - ROPE_HALF, 1) * sa + pltpu.roll(x, ROPE_HALF, 1) * sb


def _rope(x, tables, *, heads, width, transpose, name, off=0, scale=1.0):
    S = x.shape[0]
    cb = _col_view(width, off)
    tr = _tile(S, 512, 16)
    tab = pl.BlockSpec((tr, width), lambda i, h: (i, 0))

    def body(x_ref, c_ref, sa_ref, sb_ref, o_ref):
        y = _rope_apply(x_ref[...].astype(F32), c_ref[...], sa_ref[...], sb_ref[...], width, transpose)
        o_ref[...] = (y if scale == 1.0 else y * scale).astype(o_ref.dtype)

    return pl.pallas_call(
        body,
        name=name,
        grid=(S // tr, heads),
        in_specs=[pl.BlockSpec((tr, width), lambda i, h: (i, cb + h)), tab, tab, tab],
        out_specs=pl.BlockSpec((tr, width), lambda i, h: (i, h)),
        out_shape=jax.ShapeDtypeStruct((S, heads * width), BF16),
        compiler_params=_params("parallel", "parallel"),
    )(x, *tables)


def _shared_rope_grad(parts, tables, *, name):
    P, S, _ = parts.shape
    tr = _tile(S, 512, 16)
    tab = pl.BlockSpec((tr, LANES), lambda i: (i, 0))

    def body(p_ref, c_ref, sa_ref, sb_ref, o_ref):
        acc = p_ref[0]
        for k in range(1, P):
            acc = acc + p_ref[k]
        o_ref[...] = _rope_apply(acc, c_ref[...], sa_ref[...], sb_ref[...], LANES, True).astype(o_ref.dtype)

    return pl.pallas_call(
        body,
        name=name,
        grid=(S // tr,),
        in_specs=[pl.BlockSpec((P, tr, LANES), lambda i: (0, i, 0)), tab, tab, tab],
        out_specs=tab,
        out_shape=jax.ShapeDtypeStruct((S, LANES), BF16),
        compiler_params=_params("parallel"),
    )(parts, *tables)


MLA_SCALE = MLA_QK ** -0.5
LOG2E = math.log2(math.e)
LN2 = math.log(2.0)
MLA_Q_PRESCALE = MLA_SCALE * LOG2E


def _lane_tile(v, n):
    return v if n == LANES else jnp.tile(v, (1, n // LANES))


def _causal_mask(s):
    rows = lax.broadcasted_iota(jnp.int32, s.shape, 0)
    cols = lax.broadcasted_iota(jnp.int32, s.shape, 1)
    return jnp.where(cols <= rows, s, NEG)


def _tri_blocks(nb, q_major):
    if q_major:
        pairs = [(q, k) for q in range(nb) for k in range(q + 1)]
    else:
        pairs = [(q, k) for k in range(nb) for q in range(k, nb)]
    return (jnp.asarray(np.array([p[0] for p in pairs], np.int32)),
            jnp.asarray(np.array([p[1] for p in pairs], np.int32)))


HEAD_PAIR = 4


def _flash_fwd(q_raw, KV, krr, tables, *, heads, name, comm=()):
    S = q_raw.shape[0]
    t = _tile(S, 512)
    nb = S // t
    qt, kt = _tri_blocks(nb, True)
    qw, vw = HEAD_PAIR * MLA_QK_PAD, HEAD_PAIR * MLA_V

    def body(qt_ref, kt_ref, q_ref, *rest):
        kn_refs, kr_ref, v_refs = rest[:HEAD_PAIR], rest[HEAD_PAIR], rest[HEAD_PAIR + 1:2 * HEAD_PAIR + 1]
        c_ref, sa_ref, sb_ref, o_ref, lse_ref, qr_ref, m_scr, l_scr, acc_scr = rest[2 * HEAD_PAIR + 1:]
        step_id = pl.program_id(1)
        qi, ki = qt_ref[step_id], kt_ref[step_id]

        @pl.when(ki == 0)
        def _():
            m_scr[...] = jnp.full_like(m_scr, NEG)
            l_scr[...] = jnp.zeros_like(l_scr)
            acc_scr[...] = jnp.zeros_like(acc_scr)
            for h in range(HEAD_PAIR):
                base = h * MLA_QK_PAD
                nope = q_ref[:, base:base + MLA_NOPE].astype(F32) * MLA_Q_PRESCALE
                rot = _rope_apply(q_ref[:, base + MLA_NOPE:base + MLA_QK_PAD].astype(F32), c_ref[...], sa_ref[...],
                                  sb_ref[...], LANES, False) * MLA_Q_PRESCALE
                qr_ref[:, base:base + MLA_NOPE] = nope.astype(qr_ref.dtype)
                qr_ref[:, base + MLA_NOPE:base + MLA_QK_PAD] = rot.astype(qr_ref.dtype)

        def step(diagonal):
            for h, (kn_ref, v_ref) in enumerate(zip(kn_refs, v_refs)):
                cols = slice(h * MLA_QK_PAD, (h + 1) * MLA_QK_PAD)
                k = jnp.concatenate([kn_ref[...], kr_ref[...]], axis=1)
                s = lax.dot_general(qr_ref[:, cols], k, NT, preferred_element_type=F32)
                if diagonal:
                    s = _causal_mask(s)
                m_prev = m_scr[h]
                m_new = jnp.maximum(m_prev, jnp.max(s, axis=1, keepdims=True))
                alpha = jnp.exp2(m_prev - m_new)
                p = jnp.exp2(s - _lane_tile(m_new, t))
                l_new = alpha * l_scr[h] + jnp.sum(p, axis=1, keepdims=True)
                acc = alpha * acc_scr[h] + jnp.dot(p.astype(BF16), v_ref[...], preferred_element_type=F32)
                if diagonal:
                    o_ref[:, h * MLA_V:(h + 1) * MLA_V] = (acc / l_new).astype(o_ref.dtype)
                    lse_ref[h] = m_new + jnp.log(l_new) * LOG2E
                else:
                    l_scr[h], acc_scr[h], m_scr[h] = l_new, acc, m_new

        pl.when(ki < qi)(lambda: step(False))
        pl.when(ki == qi)(lambda: step(True))

    def kvspec(h, half):
        return pl.BlockSpec((t, LANES), lambda hp, s, qt, kt: (kt[s], 2 * (HEAD_PAIR * hp + h) + half))

    qtab = pl.BlockSpec((t, LANES), lambda hp, s, qt, kt: (qt[s], 0))
    qrow = lambda hp, s, qt, kt: (qt[s], hp)
    return _call(
        body,
        name=name,
        grid=(heads // HEAD_PAIR, int(qt.shape[0])),
        in_specs=[pl.BlockSpec((t, qw), qrow), *[kvspec(h, 0) for h in range(HEAD_PAIR)],
                  pl.BlockSpec((t, LANES), lambda hp, s, qt, kt: (kt[s], 0)),
                  *[kvspec(h, 1) for h in range(HEAD_PAIR)], qtab, qtab, qtab],
        out_specs=[pl.BlockSpec((t, vw), qrow),
                   pl.BlockSpec((HEAD_PAIR, t, LANES), lambda hp, s, qt, kt: (hp, qt[s], 0)),
                   pl.BlockSpec((t, qw), qrow)],
        out_shape=[jax.ShapeDtypeStruct((S, heads * MLA_V), BF16),
                   jax.ShapeDtypeStruct((heads, S, LANES), F32),
                   jax.ShapeDtypeStruct((S, heads * MLA_QK_PAD), BF16)],
        scratch_shapes=[pltpu.VMEM((HEAD_PAIR, t, LANES), F32), pltpu.VMEM((HEAD_PAIR, t, LANES), F32),
                        pltpu.VMEM((HEAD_PAIR, t, MLA_V), F32)],
        sem=("parallel", "arbitrary"),
        args=(q_raw, *[KV] * HEAD_PAIR, krr, *[KV] * HEAD_PAIR, *tables),
        comm=comm,
        prefetch=(qt, kt),
    )


def _flash_bwd(Q, KV, krr, dO, O, lse, tables, *, heads, name, comm=()):
    S = Q.shape[0]
    t = _tile(S, 512)
    nb = S // t
    qt, kt = _tri_blocks(nb, False)
    n_steps = int(qt.shape[0])
    qw, vw = HEAD_PAIR * MLA_QK_PAD, HEAD_PAIR * MLA_V

    def body(qt_ref, kt_ref, q_ref, *rest):
        kn_refs, kr_ref, v_refs = rest[:HEAD_PAIR], rest[HEAD_PAIR], rest[HEAD_PAIR + 1:2 * HEAD_PAIR + 1]
        (do_ref, o_ref, lse_ref, c_ref, sa_ref, sb_ref, dq_ref, dkv_ref, dkr_ref,
         dq_scr, dk_scr, dv_scr, delta_scr) = rest[2 * HEAD_PAIR + 1:]
        step_id = pl.program_id(1)
        qi, ki = qt_ref[step_id], kt_ref[step_id]

        @pl.when(ki == 0)
        def _():
            for h in range(HEAD_PAIR):
                vc = slice(h * MLA_V, (h + 1) * MLA_V)
                d = jnp.sum(do_ref[:, vc].astype(F32) * o_ref[:, vc].astype(F32), axis=1, keepdims=True)
                delta_scr[h, qi] = jnp.broadcast_to(d, (t, LANES))

        def step(diagonal):
            for h, (kn_ref, v_ref) in enumerate(zip(kn_refs, v_refs)):
                base = h * MLA_QK_PAD
                cols = slice(base, base + MLA_QK_PAD)
                vc = slice(h * MLA_V, (h + 1) * MLA_V)
                q, do = q_ref[:, cols], do_ref[:, vc]
                k = jnp.concatenate([kn_ref[...], kr_ref[...]], axis=1)
                s = lax.dot_general(q, k, NT, preferred_element_type=F32)
                if diagonal:
                    s = _causal_mask(s)
                p = jnp.exp2(s - _lane_tile(lse_ref[h], t))
                dv = lax.dot_general(p.astype(BF16), do, TN, preferred_element_type=F32)
                dp = lax.dot_general(do, v_ref[...], NT, preferred_element_type=F32)
                ds = (p * (dp - _lane_tile(delta_scr[h, qi], t))).astype(BF16)
                dk = lax.dot_general(ds, q, TN, preferred_element_type=F32)
                dq = jnp.dot(ds, k, preferred_element_type=F32)
                if diagonal:
                    dk_scr[h], dv_scr[h] = dk, dv
                    dq = (dq_scr[qi, :, cols] + dq) * (LN2 * MLA_Q_PRESCALE)
                    rot = _rope_apply(dq[:, MLA_NOPE:], c_ref[...], sa_ref[...], sb_ref[...], LANES, True)
                    dq_ref[:, base:base + MLA_NOPE] = dq[:, :MLA_NOPE].astype(dq_ref.dtype)
                    dq_ref[:, base + MLA_NOPE:base + MLA_QK_PAD] = rot.astype(dq_ref.dtype)
                else:
                    dk_scr[h] += dk
                    dv_scr[h] += dv
                    dq_scr[qi, :, cols] += dq

        @pl.when(ki == 0)
        def _():
            dq_scr[qi] = jnp.zeros((t, qw), F32)

        pl.when(qi > ki)(lambda: step(False))
        pl.when(qi == ki)(lambda: step(True))

        @pl.when(qi == nb - 1)
        def _():
            shared = jnp.zeros((t, LANES), F32)
            for h in range(HEAD_PAIR):
                base = h * MLA_QK_PAD
                dk = dk_scr[h] * LN2
                dkv_ref[:, base:base + MLA_NOPE] = dk[:, :MLA_NOPE].astype(dkv_ref.dtype)
                dkv_ref[:, base + MLA_NOPE:base + MLA_QK_PAD] = dv_scr[h].astype(dkv_ref.dtype)
                shared = shared + dk[:, MLA_NOPE:]
            dkr_ref[0] = shared

    def kvspec(h, half):
        return pl.BlockSpec((t, LANES), lambda hp, s, qt, kt: (kt[s], 2 * (HEAD_PAIR * hp + h) + half))

    qrow = lambda hp, s, qt, kt: (qt[s], hp)
    krow = lambda hp, s, qt, kt: (kt[s], hp)
    ktab = pl.BlockSpec((t, LANES), lambda hp, s, qt, kt: (kt[s], 0))
    return _call(
        body,
        name=name,
        grid=(heads // HEAD_PAIR, n_steps),
        in_specs=[pl.BlockSpec((t, qw), qrow), *[kvspec(h, 0) for h in range(HEAD_PAIR)], ktab,
                  *[kvspec(h, 1) for h in range(HEAD_PAIR)],
                  pl.BlockSpec((t, vw), qrow), pl.BlockSpec((t, vw), qrow),
                  pl.BlockSpec((HEAD_PAIR, t, LANES), lambda hp, s, qt, kt: (hp, qt[s], 0)),
                  ktab, ktab, ktab],
        out_specs=[pl.BlockSpec((t, qw), krow), pl.BlockSpec((t, qw), krow),
                   pl.BlockSpec((1, t, LANES), lambda hp, s, qt, kt: (hp, kt[s], 0))],
        out_shape=[jax.ShapeDtypeStruct((S, heads * MLA_QK_PAD), BF16),
                   jax.ShapeDtypeStruct((S, heads * MLA_QK_PAD), BF16),
                   jax.ShapeDtypeStruct((heads // HEAD_PAIR, S, LANES), F32)],
        scratch_shapes=[pltpu.VMEM((nb, t, qw), F32), pltpu.VMEM((HEAD_PAIR, t, MLA_QK_PAD), F32),
                        pltpu.VMEM((HEAD_PAIR, t, MLA_V), F32), pltpu.VMEM((HEAD_PAIR, nb, t, LANES), F32)],
        sem=("parallel", "arbitrary"),
        args=(Q, *[KV] * HEAD_PAIR, krr, *[KV] * HEAD_PAIR, dO, O, lse, *tables),
        comm=comm,
        prefetch=(qt, kt),
    )


SWA_SCALE = SWA_HD ** -0.5


def _t5_bucket_table():
    a = np.arange(BLOCK)[:, None]
    j = np.arange(2 * BLOCK)[None, :]
    dist = BLOCK + a - j
    max_exact = REL_BUCKETS // 2
    n = np.maximum(dist, 0)
    large = max_exact + (np.log(np.maximum(n, 1).astype(np.float32) / np.float32(max_exact))
                         / np.float32(math.log(REL_MAX_DIST / max_exact))
                         * np.float32(REL_BUCKETS - max_exact)).astype(np.int32)
    large = np.minimum(large, REL_BUCKETS - 1)
    bucket = np.where(n < max_exact, n, large)
    valid = (dist >= 0) & (dist < WINDOW)
    return bucket.astype(np.int32), valid


def _heads_to_rows(x, G):
    return jnp.concatenate([x[:, g * SWA_HD:(g + 1) * SWA_HD] for g in range(G)], axis=0)


def _rows_to_heads(o_ref, x, G):
    for g in range(G):
        o_ref[:, g * SWA_HD:(g + 1) * SWA_HD] = x[g * BLOCK:(g + 1) * BLOCK].astype(o_ref.dtype)


def _swa_probs(q_ref, kp_ref, kc_ref, bias_ref, sink_ref, qb, G):
    q2 = _heads_to_rows(q_ref[...], G)
    kb = jnp.concatenate([kp_ref[0], kc_ref[0]], axis=0)
    s = lax.dot_general(kb, q2, NT, preferred_element_type=F32) * SWA_SCALE + bias_ref[0]
    keys = lax.broadcasted_iota(jnp.int32, s.shape, 0)
    s = jnp.where((keys >= BLOCK) | (qb > 0), s, NEG)
    sink = sink_ref[0]
    m = jnp.maximum(jnp.max(s, axis=0, keepdims=True), sink)
    e = jnp.exp(s - m)
    es = jnp.exp(sink - m)
    inv = 1.0 / (jnp.sum(e, axis=0, keepdims=True) + es)
    return q2, kb, e * inv, es * inv


def _swa_fwd(q, q_off, k, v, bias_t, sink, *, name, comm=()):
    S = k.shape[1]
    G = bias_t.shape[2] // BLOCK
    nb = S // BLOCK
    qcol = _col_view(G * SWA_HD, q_off)
    cur = lambda kh, qb: (kh, qb, 0)
    prev = lambda kh, qb: (kh, jnp.maximum(qb - 1, 0), 0)
    kvspec = lambda im: pl.BlockSpec((1, BLOCK, SWA_HD), im)

    def body(q_ref, kc_ref, kp_ref, vc_ref, vp_ref, bias_ref, sink_ref, o_ref):
        qb = pl.program_id(1)
        _, _, pt, _ = _swa_probs(q_ref, kp_ref, kc_ref, bias_ref, sink_ref, qb, G)
        vb = jnp.concatenate([vp_ref[0], vc_ref[0]], axis=0)
        _rows_to_heads(o_ref, lax.dot_general(pt.astype(BF16), vb, TN, preferred_element_type=F32), G)

    outs, moved = _call(
        body,
        name=name,
        grid=(SWA_KVH, nb),
        in_specs=[pl.BlockSpec((BLOCK, G * SWA_HD), lambda kh, qb: (qb, qcol + kh)),
                  kvspec(cur), kvspec(prev), kvspec(cur), kvspec(prev),
                  pl.BlockSpec((1, 2 * BLOCK, G * BLOCK), lambda kh, qb: (kh, 0, 0)),
                  pl.BlockSpec((1, 1, G * BLOCK), lambda kh, qb: (kh, 0, 0))],
        out_specs=[pl.BlockSpec((BLOCK, G * SWA_HD), lambda kh, qb: (qb, kh))],
        out_shape=[jax.ShapeDtypeStruct((S, SWA_KVH * G * SWA_HD), BF16)],
        sem=("parallel", "parallel"),
        args=(q, k, k, v, v, bias_t, sink),
        comm=comm,
    )
    return outs[0], moved


def _swa_bwd(q, q_off, k, v, bias_t, sink, do, *, name, comm=()):
    S = k.shape[1]
    G = bias_t.shape[2] // BLOCK
    nb = S // BLOCK
    qcol = _col_view(G * SWA_HD, q_off)
    cur = lambda kh, qb: (kh, jnp.minimum(qb, nb - 1), 0)
    prev = lambda kh, qb: (kh, jnp.maximum(jnp.minimum(qb, nb - 1) - 1, 0), 0)
    lag = lambda kh, qb: (kh, jnp.maximum(qb - 1, 0), 0)
    kvspec = lambda im: pl.BlockSpec((1, BLOCK, SWA_HD), im)

    def body(q_ref, kc_ref, kp_ref, vc_ref, vp_ref, bias_ref, sink_ref, do_ref,
             dq_ref, dk_ref, dv_ref, dbias_ref, dsink_ref, ck_scr, cv_scr):
        qb = pl.program_id(1)

        @pl.when(qb == 0)
        def _():
            dbias_ref[...] = jnp.zeros_like(dbias_ref)
            dsink_ref[...] = jnp.zeros_like(dsink_ref)
            ck_scr[...] = jnp.zeros_like(ck_scr)
            cv_scr[...] = jnp.zeros_like(cv_scr)

        @pl.when(qb < nb)
        def _():
            q2, kb, pt, ps = _swa_probs(q_ref, kp_ref, kc_ref, bias_ref, sink_ref, qb, G)
            vb = jnp.concatenate([vp_ref[0], vc_ref[0]], axis=0)
            do2 = _heads_to_rows(do_ref[...], G)
            dpt = lax.dot_general(vb, do2, NT, preferred_element_type=F32)
            delta = jnp.sum(dpt * pt, axis=0, keepdims=True)
            dst = pt * (dpt - delta)
            dbias_ref[0] += dst
            dsink_ref[0] += -ps * delta
            dsb = (dst * SWA_SCALE).astype(BF16)
            _rows_to_heads(dq_ref, lax.dot_general(dsb, kb, TN, preferred_element_type=F32), G)
            dkb = jnp.dot(dsb, q2, preferred_element_type=F32)
            dvb = jnp.dot(pt.astype(BF16), do2, preferred_element_type=F32)
            dk_ref[0] = (ck_scr[...] + dkb[:BLOCK]).astype(dk_ref.dtype)
            dv_ref[0] = (cv_scr[...] + dvb[:BLOCK]).astype(dv_ref.dtype)
            ck_scr[...] = dkb[BLOCK:]
            cv_scr[...] = dvb[BLOCK:]

        @pl.when(qb == nb)
        def _():
            dk_ref[0] = ck_scr[...].astype(dk_ref.dtype)
            dv_ref[0] = cv_scr[...].astype(dv_ref.dtype)

    tspec = pl.BlockSpec((BLOCK, G * SWA_HD), lambda kh, qb: (jnp.minimum(qb, nb - 1), kh))
    return _call(
        body,
        name=name,
        grid=(SWA_KVH, nb + 1),
        in_specs=[pl.BlockSpec((BLOCK, G * SWA_HD), lambda kh, qb: (jnp.minimum(qb, nb - 1), qcol + kh)),
                  kvspec(cur), kvspec(prev), kvspec(cur), kvspec(prev),
                  pl.BlockSpec((1, 2 * BLOCK, G * BLOCK), lambda kh, qb: (kh, 0, 0)),
                  pl.BlockSpec((1, 1, G * BLOCK), lambda kh, qb: (kh, 0, 0)),
                  pl.BlockSpec((BLOCK, G * SWA_HD), lambda kh, qb: (jnp.minimum(qb, nb - 1), kh))],
        out_specs=[tspec, kvspec(lag), kvspec(lag),
                   pl.BlockSpec((1, 2 * BLOCK, G * BLOCK), lambda kh, qb: (kh, 0, 0)),
                   pl.BlockSpec((1, 1, G * BLOCK), lambda kh, qb: (kh, 0, 0))],
        out_shape=[jax.ShapeDtypeStruct((S, SWA_KVH * G * SWA_HD), BF16),
                   jax.ShapeDtypeStruct((SWA_KVH, S, SWA_HD), BF16),
                   jax.ShapeDtypeStruct((SWA_KVH, S, SWA_HD), BF16),
                   jax.ShapeDtypeStruct((SWA_KVH, 2 * BLOCK, G * BLOCK), F32),
                   jax.ShapeDtypeStruct((SWA_KVH, 1, G * BLOCK), F32)],
        scratch_shapes=[pltpu.VMEM((BLOCK, SWA_HD), F32), pltpu.VMEM((BLOCK, SWA_HD), F32)],
        sem=("parallel", "arbitrary"),
        args=(q, k, k, v, v, bias_t, sink, do),
        comm=comm,
    )


def _gate_mix(z, o_a, o_b, *, D, off_a, off_b, name):
    S = z.shape[0]
    tr = _tile(S, 256, 16)
    row = pl.BlockSpec((tr, D), lambda i: (i, 0))
    ca, cb = _col_view(D, off_a), _col_view(D, off_b)

    def body(ga_ref, gb_ref, oa_ref, ob_ref, m_ref):
        m = (_sigmoid(ga_ref[...].astype(F32)) * oa_ref[...].astype(F32)
             + _sigmoid(gb_ref[...].astype(F32)) * ob_ref[...].astype(F32))
        m_ref[...] = m.astype(m_ref.dtype)

    return pl.pallas_call(
        body,
        name=name,
        grid=(S // tr,),
        in_specs=[pl.BlockSpec((tr, D), lambda i: (i, ca)), pl.BlockSpec((tr, D), lambda i: (i, cb)), row, row],
        out_specs=row,
        out_shape=jax.ShapeDtypeStruct((S, D), BF16),
        compiler_params=_params("parallel"),
    )(z, z, o_a, o_b)


def _gate_mix_bwd(dm, z, o_a, o_b, *, D, off_a, off_b, name):
    S = z.shape[0]
    tr = _tile(S, 256, 16)
    row = pl.BlockSpec((tr, D), lambda i: (i, 0))
    ca, cb = _col_view(D, off_a), _col_view(D, off_b)

    def body(dm_ref, ga_ref, gb_ref, oa_ref, ob_ref, dga_ref, dgb_ref, doa_ref, dob_ref):
        d = dm_ref[...].astype(F32)
        for g_ref, o_ref, dg_ref, do_ref in ((ga_ref, oa_ref, dga_ref, doa_ref), (gb_ref, ob_ref, dgb_ref, dob_ref)):
            sg = _sigmoid(g_ref[...].astype(F32))
            dg_ref[...] = (d * o_ref[...].astype(F32) * (sg * (1.0 - sg))).astype(dg_ref.dtype)
            do_ref[...] = (d * sg).astype(do_ref.dtype)

    return pl.pallas_call(
        body,
        name=name,
        grid=(S // tr,),
        in_specs=[row, pl.BlockSpec((tr, D), lambda i: (i, ca)), pl.BlockSpec((tr, D), lambda i: (i, cb)), row, row],
        out_specs=[row] * 4,
        out_shape=[jax.ShapeDtypeStruct((S, D), BF16)] * 4,
        compiler_params=_params("parallel"),
    )(dm, z, z, o_a, o_b)


CONV_ROWS = 256
CONV_COLS = 1408
SUBLANES = 8


def _shift_matrices(tr):
    r = np.arange(tr)[:, None]
    c = np.arange(tr)[None, :]
    back = [jnp.asarray(r == c + d, dtype=BF16) for d in (1, 2)]
    ahead = [jnp.asarray(r + d == c, dtype=BF16) for d in (1, 2)]
    return back, ahead


def _rows_before(x, halo_ref, first, b1_ref, b2_ref):
    s1 = jnp.dot(b1_ref[...], x, preferred_element_type=F32)
    s2 = jnp.dot(b2_ref[...], x, preferred_element_type=F32)
    h8 = jnp.where(first, 0.0, halo_ref[...].astype(F32)[HALO - SUBLANES:])
    rows = lax.broadcasted_iota(jnp.int32, h8.shape, 0)
    fix1 = jnp.where(rows < 1, pltpu.roll(h8, 1, 0), 0.0)
    fix2 = jnp.where(rows < 2, pltpu.roll(h8, 2, 0), 0.0)
    s1 = jnp.concatenate([s1[:SUBLANES] + fix1, s1[SUBLANES:]], axis=0)
    s2 = jnp.concatenate([s2[:SUBLANES] + fix2, s2[SUBLANES:]], axis=0)
    return s1, s2


def _conv_taps(x, s1, s2, cw_ref, cb_ref):
    return cb_ref[...] + cw_ref[0:1, :] * s2 + cw_ref[1:2, :] * s1 + cw_ref[2:3, :] * x


def _conv_gate(up, cw, cb, *, name):
    S, F2 = up.shape
    F = F2 // 2
    tr = _tile(S, CONV_ROWS, HALO)
    tc = _tile(F, CONV_COLS)
    nc = F // tc
    hb = tr // HALO
    back, _ = _shift_matrices(tr)
    mat = pl.BlockSpec((tr, tr), lambda i, j: (0, 0))

    def halo_map(shift):
        return lambda i, j: (jnp.maximum(i * hb - 1, 0), j + shift)

    def body(x1_ref, h1_ref, x2_ref, h2_ref, cw1_ref, cw2_ref, cb1_ref, cb2_ref, b1_ref, b2_ref, a_ref):
        first = pl.program_id(0) == 0
        us = []
        for x_ref, h_ref, cw_ref, cb_ref in ((x1_ref, h1_ref, cw1_ref, cb1_ref), (x2_ref, h2_ref, cw2_ref, cb2_ref)):
            x = x_ref[...]
            s1, s2 = _rows_before(x, h_ref, first, b1_ref, b2_ref)
            us.append(_conv_taps(x.astype(F32), s1, s2, cw_ref, cb_ref))
        u1, u2 = us
        a_ref[...] = (u1 * _sigmoid(u1) * u2).astype(a_ref.dtype)

    return pl.pallas_call(
        body,
        name=name,
        grid=(S // tr, nc),
        in_specs=[pl.BlockSpec((tr, tc), lambda i, j: (i, j)), pl.BlockSpec((HALO, tc), halo_map(0)),
                  pl.BlockSpec((tr, tc), lambda i, j: (i, j + nc)), pl.BlockSpec((HALO, tc), halo_map(nc)),
                  pl.BlockSpec((CONV_WIDTH, tc), lambda i, j: (0, j)),
                  pl.BlockSpec((CONV_WIDTH, tc), lambda i, j: (0, j + nc)),
                  pl.BlockSpec((1, tc), lambda i, j: (0, j)), pl.BlockSpec((1, tc), lambda i, j: (0, j + nc)),
                  mat, mat],
        out_specs=pl.BlockSpec((tr, tc), lambda i, j: (i, j)),
        out_shape=jax.ShapeDtypeStruct((S, F), BF16),
        compiler_params=_params("parallel", "parallel"),
    )(up, up, up, up, cw, cw, cb, cb, *back)


def _conv_gate_bwd(up, da, cw, cb, *, name, comm=()):
    S, F2 = up.shape
    F = F2 // 2
    tr = _tile(S, CONV_ROWS, HALO)
    tc = _tile(F, CONV_COLS)
    nc = F // tc
    hb = tr // HALO
    ni = S // tr
    back, ahead = _shift_matrices(tr)
    mat = pl.BlockSpec((tr, tr), lambda j, r: (0, 0))

    def cur(shift):
        return lambda j, r: (ni - 1 - r, j + shift)

    def before(shift):
        return lambda j, r: (jnp.maximum((ni - 1 - r) * hb - 1, 0), j + shift)

    def vec(rows, shift):
        return pl.BlockSpec((rows, tc), lambda j, r: (0, j + shift))

    def body(x1_ref, h1_ref, x2_ref, h2_ref, da_ref, cw1_ref, cw2_ref, cb1_ref, cb2_ref,
             b1_ref, b2_ref, a1_ref, a2_ref, dup_ref, dcw_ref, dcb_ref, next_du):
        r = pl.program_id(1)
        first = r == ni - 1

        @pl.when(r == 0)
        def _():
            dcw_ref[...] = jnp.zeros_like(dcw_ref)
            dcb_ref[...] = jnp.zeros_like(dcb_ref)
            next_du[...] = jnp.zeros_like(next_du)

        x1, x2 = x1_ref[...], x2_ref[...]
        x1f, x2f = x1.astype(F32), x2.astype(F32)
        s11, s12 = _rows_before(x1, h1_ref, first, b1_ref, b2_ref)
        s21, s22 = _rows_before(x2, h2_ref, first, b1_ref, b2_ref)
        u1 = _conv_taps(x1f, s11, s12, cw1_ref, cb1_ref)
        u2 = _conv_taps(x2f, s21, s22, cw2_ref, cb2_ref)
        sg = _sigmoid(u1)
        daf = da_ref[...].astype(F32)
        du1 = daf * u2 * (sg * (1.0 + u1 * (1.0 - sg)))
        du2 = daf * (u1 * sg)
        rows = lax.broadcasted_iota(jnp.int32, (SUBLANES, tc), 0)

        for half, (du, own, own1, own2, cw_ref) in enumerate(((du1, x1f, s11, s12, cw1_ref),
                                                             (du2, x2f, s21, s22, cw2_ref))):
            du_b = du.astype(BF16)
            n1 = jnp.dot(a1_ref[...], du_b, preferred_element_type=F32)
            n2 = jnp.dot(a2_ref[...], du_b, preferred_element_type=F32)
            c8 = next_du[half]
            fix1 = jnp.where(rows >= SUBLANES - 1, pltpu.roll(c8, SUBLANES - 1, 0), 0.0)
            fix2 = jnp.where(rows >= SUBLANES - 2, pltpu.roll(c8, SUBLANES - 2, 0), 0.0)
            n1 = jnp.concatenate([n1[:tr - SUBLANES], n1[tr - SUBLANES:] + fix1], axis=0)
            n2 = jnp.concatenate([n2[:tr - SUBLANES], n2[tr - SUBLANES:] + fix2], axis=0)
            dup = cw_ref[2:3, :] * du + cw_ref[1:2, :] * n1 + cw_ref[0:1, :] * n2
            dup_ref[half] = dup.astype(dup_ref.dtype)
            dcb_ref[half] += jnp.sum(du, axis=0, keepdims=True)
            for tap, shifted in enumerate((own2, own1, own)):
                dcw_ref[half, tap:tap + 1, :] += jnp.sum(du * shifted, axis=0, keepdims=True)
            next_du[half] = du[:SUBLANES].astype(BF16).astype(F32)

    return _call(
        body,
        name=name,
        grid=(nc, ni),
        in_specs=[pl.BlockSpec((tr, tc), cur(0)), pl.BlockSpec((HALO, tc), before(0)),
                  pl.BlockSpec((tr, tc), cur(nc)), pl.BlockSpec((HALO, tc), before(nc)),
                  pl.BlockSpec((tr, tc), cur(0)),
                  vec(CONV_WIDTH, 0), vec(CONV_WIDTH, nc), vec(1, 0), vec(1, nc), mat, mat, mat, mat],
        out_specs=[pl.BlockSpec((2, tr, tc), lambda j, r: (0, ni - 1 - r, j)),
                   pl.BlockSpec((2, CONV_WIDTH, tc), lambda j, r: (0, 0, j)),
                   pl.BlockSpec((2, 1, tc), lambda j, r: (0, 0, j))],
        out_shape=[jax.ShapeDtypeStruct((2, S, F), BF16), jax.ShapeDtypeStruct((2, CONV_WIDTH, F), F32),
                   jax.ShapeDtypeStruct((2, 1, F), F32)],
        scratch_shapes=[pltpu.VMEM((2, SUBLANES, tc), F32)],
        sem=("parallel", "arbitrary"),
        args=(up, up, up, up, da, cw, cw, cb, cb, *back, *ahead),
        comm=comm,
    )


def _adam_math(w, g, m, v):
    m = ADAM_B1 * m + (1.0 - ADAM_B1) * g
    v = ADAM_B2 * v + (1.0 - ADAM_B2) * (g * g)
    m_hat = m / (1.0 - ADAM_B1 ** ADAM_STEP)
    v_hat = v / (1.0 - ADAM_B2 ** ADAM_STEP)
    delta = -ADAM_LR * (m_hat / (jnp.sqrt(v_hat) + ADAM_EPS) + ADAM_WD * w)
    return delta, m, v


def _adamw(w, m, v, parts, *, name):
    R, C = w.shape
    plist = list(parts) if isinstance(parts, (list, tuple)) else [parts]
    tr = _tile(min(p.shape[1] for p in plist), 256, 16)
    assert sum(p.shape[1] for p in plist) == R and all(p.shape[1] % tr == 0 for p in plist)
    row = pl.BlockSpec((tr, C), lambda i: (i, 0))
    first, spans = 0, []
    for p in plist:
        spans.append((first, first + p.shape[1] // tr))
        first = spans[-1][1]

    def body(w_ref, m_ref, v_ref, *rest):
        p_refs, (g_ref, d_ref, m2_ref, v2_ref) = rest[:len(plist)], rest[len(plist):]
        i = pl.program_id(0)

        def update(p_ref):
            g = p_ref[0].astype(F32)
            for k in range(1, N_DEV):
                g = g + p_ref[k].astype(F32)
            g_ref[...] = g
            d_ref[...], m2_ref[...], v2_ref[...] = _adam_math(w_ref[...], g, m_ref[...], v_ref[...])

        if len(plist) == 1:
            update(p_refs[0])
        else:
            for p_ref, (lo, hi) in zip(p_refs, spans):
                pl.when((i >= lo) & (i < hi))(functools.partial(update, p_ref))

    def part_spec(lo, hi):
        return pl.BlockSpec((N_DEV, tr, C), lambda i: (0, jnp.clip(i - lo, 0, hi - lo - 1), 0))

    return pl.pallas_call(
        body,
        name=name,
        grid=(R // tr,),
        in_specs=[row, row, row] + [part_spec(lo, hi) for lo, hi in spans],
        out_specs=[row] * 4,
        out_shape=[jax.ShapeDtypeStruct((R, C), F32)] * 4,
        compiler_params=_params("parallel"),
    )(w, m, v, *plist)


def _adamw_ada(w, m, v, cact_t, dmod_cols, *, name):
    R, C = w.shape
    B = cact_t.shape[1]
    tr = _tile(R, 256, 8)
    row = pl.BlockSpec((tr, C), lambda i: (i, 0))

    def body(w_ref, m_ref, v_ref, c_ref, d_ref, g_ref, dl_ref, m2_ref, v2_ref):
        g = c_ref[:, 0:1] * d_ref[0:1, :]
        for b in range(1, B):
            g = g + c_ref[:, b:b + 1] * d_ref[b:b + 1, :]
        g_ref[...] = g
        dl_ref[...], m2_ref[...], v2_ref[...] = _adam_math(w_ref[...], g, m_ref[...], v_ref[...])

    return pl.pallas_call(
        body,
        name=name,
        grid=(R // tr,),
        in_specs=[row, row, row, pl.BlockSpec((tr, B), lambda i: (i, 0)), pl.BlockSpec((B, C), lambda i: (0, 0))],
        out_specs=[row] * 4,
        out_shape=[jax.ShapeDtypeStruct((R, C), F32)] * 4,
        compiler_params=_params("parallel"),
    )(w, m, v, cact_t, dmod_cols)


def _z_layout(D, q_rank, kv_rank):
    kv = SWA_KVH * SWA_HD
    orig = {}
    o = 0
    for nm, w in (("cq", q_rank), ("ckv", kv_rank), ("kr", MLA_ROPE), ("qs", D), ("ks", kv), ("vs", kv),
                  ("ga", D), ("gb", D)):
        orig[nm] = (o, w)
        o += w
    blockw = {"cq": q_rank, "ckv": kv_rank, "kr": LANES, "qs": D, "ks": kv, "vs": kv, "ga": D, "gb": D}
    best = None
    for perm in itertools.permutations(("cq", "ckv", "ks", "vs", "kr")):
        off, new = 0, {}
        for nm in ("ga", "gb", "qs") + perm:
            off = _round_up(off, blockw[nm])
            new[nm] = off
            off += blockw[nm]
        if best is None or off < best[0]:
            best = (off, new)
    total = _round_up(best[0], 1024 if best[0] > 4096 else 512)
    return orig, best[1], blockw, total, o


def _permute_w_in(w, lay):
    orig, new, blockw, total, _ = lay
    parts, at = [], 0
    for nm in sorted(new, key=new.get):
        if new[nm] > at:
            parts.append(jnp.zeros((w.shape[0], new[nm] - at), w.dtype))
        o, wd = orig[nm]
        parts.append(w[:, o:o + wd])
        if blockw[nm] > wd:
            parts.append(jnp.zeros((w.shape[0], blockw[nm] - wd), w.dtype))
        at = new[nm] + blockw[nm]
    if total > at:
        parts.append(jnp.zeros((w.shape[0], total - at), w.dtype))
    return jnp.concatenate(parts, axis=1)


def _unpermute_w_in(wp, lay):
    orig, new, _, _, _ = lay
    return jnp.concatenate([wp[:, new[nm]:new[nm] + orig[nm][1]] for nm in sorted(orig, key=lambda n: orig[n][0])],
                           axis=1)


def _assemble_dz(parts, lay, S):
    _, new, blockw, total, _ = lay
    names = sorted(new, key=new.get)
    tr = _tile(S, 256, 16)

    def body(*refs):
        o_ref = refs[-1]
        cols, at = [], 0
        for nm, ref in zip(names, refs):
            if new[nm] > at:
                cols.append(jnp.zeros((tr, new[nm] - at), BF16))
            cols.append(ref[...])
            at = new[nm] + blockw[nm]
        if total > at:
            cols.append(jnp.zeros((tr, total - at), BF16))
        o_ref[...] = jnp.concatenate(cols, axis=1)

    return pl.pallas_call(
        body,
        name="assemble_dz",
        grid=(S // tr,),
        in_specs=[pl.BlockSpec((tr, blockw[nm]), lambda i: (i, 0)) for nm in names],
        out_specs=pl.BlockSpec((tr, total), lambda i: (i, 0)),
        out_shape=jax.ShapeDtypeStruct((S, total), BF16),
        compiler_params=_params("parallel"),
    )(*[parts[nm] for nm in names])


def _unshard_cols(g):
    return jnp.transpose(g, (1, 0, 2)).reshape(g.shape[1], N_DEV * g.shape[2])


def _shard_cols(w):
    K, N = w.shape
    return jnp.transpose(w.reshape(K, N_DEV, N // N_DEV), (1, 0, 2))


def _pack(vecs, rows):
    flat = jnp.concatenate([v.reshape(-1) for v in vecs])
    return jnp.pad(flat, (0, rows * LANES - flat.shape[0])).reshape(rows, LANES)


def kernel(x, c, w_ada, b_ada, g_pre_mix, g_post_mix, w_in, g_q_lat, w_uq, g_kv_lat, w_ukv, rel_bias, sinks, w_o, g_pre_ffn, g_post_ffn, w_up, conv_w, conv_b, w_down, loss_target, m_w_ada, m_b_ada, m_g_pre_mix, m_g_post_mix, m_w_in, m_g_q_lat, m_w_uq, m_g_kv_lat, m_w_ukv, m_rel_bias, m_sinks, m_w_o, m_g_pre_ffn, m_g_post_ffn, m_w_up, m_conv_w, m_conv_b, m_w_down, v_w_ada, v_b_ada, v_g_pre_mix, v_g_post_mix, v_w_in, v_g_q_lat, v_w_uq, v_g_kv_lat, v_w_ukv, v_rel_bias, v_sinks, v_w_o, v_g_pre_ffn, v_g_post_ffn, v_w_up, v_conv_w, v_conv_b, v_w_down):
    S, D = x.shape[1], x.shape[2]
    Q_RANK, KV_RANK = g_q_lat.shape[1], g_kv_lat.shape[1]
    H_MLA = D // MLA_V
    H_SWA = D // SWA_HD
    G_SWA = H_SWA // SWA_KVH
    F2 = w_up.shape[2] * N_DEV
    F = F2 // 2
    ada_n = w_ada.shape[2]
    me = 4 * lax.axis_index("x") + 2 * lax.axis_index("y") + lax.axis_index("c")
    lay = _z_layout(D, Q_RANK, KV_RANK)
    _, zoff, _, NZ, in_cols = lay
    assert in_cols == w_in.shape[2] * N_DEV

    x2, tgt = x[0], loss_target[0]

    cw_n = conv_w.shape[2]
    small = jnp.concatenate([jnp.pad(c, ((0, 7), (0, 0))), jnp.pad(conv_w[0], ((0, 8 - CONV_WIDTH), (0, 0)))], axis=1)
    small_all = _all_gather(small, name="ag_cond", in_vmem=True)
    c_all = small_all[:, 0, :D]
    cw_full = _unshard_cols(small_all[:, :CONV_WIDTH, D:])
    b_cols = lax.dynamic_slice_in_dim(b_ada, me * ada_n, ada_n, axis=1)
    c_act, mod_cols = _ada_fwd(c_all, w_ada[0], b_cols, name="ada_fwd")
    mod_all = _all_gather(mod_cols, name="ag_mod", in_vmem=True)
    mod_me = lax.dynamic_index_in_dim(mod_all, me, axis=1, keepdims=False).reshape(1, N_DEV * ada_n)
    sh1, sc1, gt1, sh2, sc2, gt2 = [mod_me[:, k * D:(k + 1) * D] for k in range(6)]

    w_in_p = _permute_w_in(_unshard_cols(_all_gather(w_in[0].astype(BF16), name="ag_w_in", in_vmem=False)), lay)

    h1 = _prenorm(x2, g_pre_mix, sc1, sh1, name="prenorm_mix")
    z, (uq_g, ukv_g, o_g) = _matmul(h1, w_in_p, mode="nn", out_dtype=BF16, name="mm_in",
                                    comm=[("gather", w_uq[0].astype(BF16)), ("gather", w_ukv[0].astype(BF16)),
                                          ("gather", w_o[0].astype(BF16))])
    w_uq_p = jnp.pad(_unshard_cols(uq_g).reshape(Q_RANK, H_MLA, MLA_QK), ((0, 0), (0, 0), (0, MLA_QK_PAD - MLA_QK))
                     ).reshape(Q_RANK, H_MLA * MLA_QK_PAD)
    w_ukv_f = _unshard_cols(ukv_g)
    w_o_f = o_g.reshape(D, D)
    cqn = _prenorm(z, g_q_lat, None, None, name="norm_cq", off=zoff["cq"], width=Q_RANK)
    ckvn = _prenorm(z, g_kv_lat, None, None, name="norm_ckv", off=zoff["ckv"], width=KV_RANK)
    q_raw = _matmul(cqn, w_uq_p, mode="nn", out_dtype=BF16, name="mm_uq")
    kv = _matmul(ckvn, w_ukv_f, mode="nn", out_dtype=BF16, name="mm_ukv")
    tab_k = _rope_tables(S, LANES, 0)
    krr = _rope(z, tab_k, heads=1, width=LANES, transpose=False, name="rope_k", off=zoff["kr"])
    (o_a, lse, Qr), (up_g,) = _flash_fwd(q_raw, kv, krr, tab_k, heads=H_MLA, name="mla_fwd",
                                        comm=[("gather", w_up[0].astype(BF16))])
    w_up_f = _unshard_cols(up_g)

    bucket, valid = _t5_bucket_table()
    onehot = (jnp.asarray(bucket).reshape(-1, 1) == jnp.arange(LANES)[None, :]).astype(F32)
    rb_pad = jnp.pad(rel_bias, ((0, LANES - REL_BUCKETS), (0, LANES - H_SWA)))
    bias_t = _matmul(onehot, rb_pad, mode="nn", out_dtype=F32, name="bias_table", tm=2048, precision=HIGHEST)
    bias_full = jnp.transpose(bias_t[:, :H_SWA].reshape(BLOCK, 2 * BLOCK, H_SWA), (2, 0, 1))
    bias_full = jnp.where(jnp.asarray(valid)[None], bias_full, NEG)
    bias_full = jnp.transpose(bias_full.reshape(SWA_KVH, G_SWA, BLOCK, 2 * BLOCK), (0, 3, 1, 2)
                              ).reshape(SWA_KVH, 2 * BLOCK, G_SWA * BLOCK)
    sink_rows = jnp.broadcast_to(sinks.reshape(SWA_KVH, G_SWA, 1), (SWA_KVH, G_SWA, BLOCK)
                                 ).reshape(SWA_KVH, 1, G_SWA * BLOCK)
    kvw = SWA_KVH * SWA_HD

    def heads_first(t, n):
        return jnp.transpose(t.reshape(S, n, SWA_HD), (1, 0, 2))

    def heads_last(t):
        return jnp.transpose(t, (1, 0, 2)).reshape(S, t.shape[0] * SWA_HD)


    ks_h = heads_first(z[:, zoff["ks"]:zoff["ks"] + kvw], SWA_KVH)
    vs_h = heads_first(z[:, zoff["vs"]:zoff["vs"] + kvw], SWA_KVH)
    o_b, _ = _swa_fwd(z, zoff["qs"], ks_h, vs_h, bias_full, sink_rows, name="swa_fwd")

    mixin = _gate_mix(z, o_a, o_b, D=D, off_a=zoff["ga"], off_b=zoff["gb"], name="gate_mix")
    mix = _matmul(mixin, w_o_f, mode="nn", out_dtype=F32, name="mm_o")
    x1 = _postnorm_res(x2, mix, gt1, g_post_mix, name="postnorm_mix")

    h2 = _prenorm(x1, g_pre_ffn, sc2, sh2, name="prenorm_ffn")
    up, (down_g,) = _matmul(h2, w_up_f, mode="nn", out_dtype=BF16, name="mm_up",
                            comm=[("gather", w_down[0].astype(BF16))])
    w_down_f = down_g.reshape(F, D)
    act = _conv_gate(up, cw_full, conv_b, name="conv_gate")
    y = _matmul(act, w_down_f, mode="nn", out_dtype=F32, name="mm_down")
    loss_part, dout, dy, dgt2, dg_post_ffn = _final_loss(x1, y, tgt, gt2, g_post_ffn, name="final_loss")
    loss = lax.psum(loss_part[0, 0], ("x", "y", "c"))

    dw_down = _matmul(act, dy, mode="tn", out_dtype=BF16, name="mm_down_dw")
    dact = _matmul(dy, w_down_f, mode="nt", out_dtype=BF16, name="mm_down_dx")
    (dup, dcw, dcb), (got_down,) = _conv_gate_bwd(up, dact, cw_full, conv_b, name="conv_gate_bwd",
                                                  comm=[("scatter", dw_down.reshape(N_DEV, F // N_DEV, D))])
    dcw = jnp.transpose(dcw, (1, 0, 2)).reshape(CONV_WIDTH, F2)
    dcb = dcb.reshape(1, F2)
    dw_up = _matmul(h2, dup, mode="tn", out_dtype=BF16, name="mm_up_dw", shard_out=True, halves=True)
    dh2 = _matmul(dup, w_up_f, mode="nt", out_dtype=F32, name="mm_up_dx", halves=True)
    dx1, dg_pre_ffn, dsc2, dsh2 = _prenorm_bwd(x1, dh2, dout, g_pre_ffn, sc2, name="prenorm_ffn_bwd", out_dtype=F32)

    dmix, dgt1, dg_post_mix = _postnorm_bwd(dx1, mix, gt1, g_post_mix, name="postnorm_mix_bwd")
    dw_o = _matmul(mixin, dmix, mode="tn", out_dtype=BF16, name="mm_o_dw")
    dmixin = _matmul(dmix, w_o_f, mode="nt", out_dtype=BF16, name="mm_o_dx")
    dga, dgb, do_a, do_b = _gate_mix_bwd(dmixin, z, o_a, o_b, D=D, off_a=zoff["ga"], off_b=zoff["gb"],
                                         name="gate_mix_bwd")
    (dq_raw, dkv, dkr_parts), (got_up,) = _flash_bwd(Qr, kv, krr, do_a, o_a, lse, tab_k, heads=H_MLA, name="mla_bwd",
                                                     comm=[("scatter", dw_up)])
    dkr = _shared_rope_grad(dkr_parts, tab_k, name="rope_k_bwd")
    dw_uq_p = _matmul(cqn, dq_raw, mode="tn", out_dtype=BF16, name="mm_uq_dw")
    dcqn = _matmul(dq_raw, w_uq_p, mode="nt", out_dtype=F32, name="mm_uq_dx")
    dw_ukv = _matmul(ckvn, dkv, mode="tn", out_dtype=BF16, name="mm_ukv_dw", shard_out=True)
    dckvn = _matmul(dkv, w_ukv_f, mode="nt", out_dtype=F32, name="mm_ukv_dx")
    dw_uq = dw_uq_p.reshape(Q_RANK, H_MLA, MLA_QK_PAD)[:, :, :MLA_QK].reshape(Q_RANK, H_MLA * MLA_QK)

    dcw_parts = jnp.pad(_shard_cols(dcw), ((0, 0), (0, 16 - CONV_WIDTH), (0, 0)))
    (dqs, dks_h, dvs_h, dbias, dsink), (got_o, got_cw, got_uq, got_ukv) = _swa_bwd(
        z, zoff["qs"], ks_h, vs_h, bias_full, sink_rows, do_b, name="swa_bwd",
        comm=[("scatter", dw_o.reshape(N_DEV, D // N_DEV, D)), ("scatter", dcw_parts),
              ("scatter", _shard_cols(dw_uq)), ("scatter", dw_ukv)])
    dbias = jnp.transpose(dbias.reshape(SWA_KVH, 2 * BLOCK, G_SWA, BLOCK), (0, 2, 3, 1))
    drel_t = _matmul(dbias.reshape(H_SWA, BLOCK * 2 * BLOCK), onehot, mode="nn", out_dtype=F32, name="bias_grad",
                     tk=4096, precision=HIGHEST)
    d_rel_bias = jnp.transpose(drel_t[:, :REL_BUCKETS])
    d_sinks = jnp.sum(dsink.reshape(SWA_KVH, G_SWA, BLOCK), axis=-1).reshape(1, H_SWA)

    dcq, dg_q = _prenorm_bwd(z, dcqn, None, g_q_lat, None, name="norm_cq_bwd", out_dtype=BF16,
                             off=zoff["cq"], width=Q_RANK)
    dckv, dg_kv = _prenorm_bwd(z, dckvn, None, g_kv_lat, None, name="norm_ckv_bwd", out_dtype=BF16,
                               off=zoff["ckv"], width=KV_RANK)
    dz = _assemble_dz({"ga": dga, "gb": dgb, "qs": dqs, "cq": dcq, "ckv": dckv,
                       "ks": heads_last(dks_h), "vs": heads_last(dvs_h), "kr": dkr}, lay, S)
    dw_in_a = _matmul(h1, dz, mode="tn", out_dtype=BF16, name="mm_in_dw_a", m_range=(0, D // 2))
    dw_in_b, (got_in_a,) = _matmul(h1, dz, mode="tn", out_dtype=BF16, name="mm_in_dw_b", m_range=(D // 2, D // 2),
                                   comm=[("scatter", _shard_cols(_unpermute_w_in(dw_in_a, lay)))])
    dh1, (got_in_b,) = _matmul(dz, w_in_p, mode="nt", out_dtype=F32, name="mm_in_dx",
                               comm=[("scatter", _shard_cols(_unpermute_w_in(dw_in_b, lay)))])
    grad_x, dg_pre_mix, dsc1, dsh1 = _prenorm_bwd(x2, dh1, dx1, g_pre_mix, sc1, name="prenorm_mix_bwd",
                                                  out_dtype=F32)
    dmod = jnp.concatenate([dsh1, dsc1, dgt1, dsh2, dsc2, dgt2], axis=1)

    small_names = ["b_ada", "g_pre_mix", "g_post_mix", "g_q_lat", "g_kv_lat", "rel_bias", "sinks", "g_pre_ffn",
                   "g_post_ffn", "conv_b"]
    small_w = [b_ada, g_pre_mix, g_post_mix, g_q_lat, g_kv_lat, rel_bias, sinks, g_pre_ffn, g_post_ffn, conv_b]
    small_m = [m_b_ada, m_g_pre_mix, m_g_post_mix, m_g_q_lat, m_g_kv_lat, m_rel_bias, m_sinks, m_g_pre_ffn,
               m_g_post_ffn, m_conv_b]
    small_v = [v_b_ada, v_g_pre_mix, v_g_post_mix, v_g_q_lat, v_g_kv_lat, v_rel_bias, v_sinks, v_g_pre_ffn,
               v_g_post_ffn, v_conv_b]
    small_g = [dmod, dg_pre_mix, dg_post_mix, dg_q, dg_kv, d_rel_bias, d_sinks, dg_pre_ffn, dg_post_ffn, dcb]
    n_small = sum(int(np.prod(w.shape)) for w in small_w)
    rows = _round_up(-(-n_small // LANES), 16)
    parts_small = _all_gather(_pack(small_g, rows), name="ag_small_grads", in_vmem=True)
    sg, sd, sm, sv = _adamw(_pack(small_w, rows), _pack(small_m, rows), _pack(small_v, rows), parts_small,
                            name="adamw_small")

    def unpack(packed):
        flat, out, at = packed.reshape(-1), {}, 0
        for nm, w in zip(small_names, small_w):
            n = int(np.prod(w.shape))
            out[nm] = flat[at:at + n].reshape(w.shape)
            at += n
        return out

    small_out = [unpack(t) for t in (sg, sd, sm, sv)]

    dmod_all = parts_small.reshape(N_DEV, rows * LANES)[:, :6 * D]
    dmod_cols = lax.dynamic_slice_in_dim(dmod_all, me * ada_n, ada_n, axis=1)
    ada_out = _adamw_ada(w_ada[0], m_w_ada[0], v_w_ada[0], jnp.transpose(c_act), dmod_cols, name="adamw_w_ada")

    def owner_update(got, w, m, v, name):
        shp = w.shape
        w2, m2, v2 = (t.reshape(shp[-2], shp[-1]) for t in (w, m, v))
        return [t.reshape(shp) for t in _adamw(w2, m2, v2, got, name="adamw_" + name)]

    def pad_rows(t):
        return jnp.pad(t[0], ((0, 16 - CONV_WIDTH), (0, 0)))

    big = {
        "w_in": owner_update([got_in_a, got_in_b], w_in, m_w_in, v_w_in, "w_in"),
        "w_uq": owner_update(got_uq, w_uq, m_w_uq, v_w_uq, "w_uq"),
        "w_ukv": owner_update(got_ukv, w_ukv, m_w_ukv, v_w_ukv, "w_ukv"),
        "w_o": owner_update(got_o, w_o, m_w_o, v_w_o, "w_o"),
        "w_up": owner_update(got_up, w_up, m_w_up, v_w_up, "w_up"),
        "w_down": owner_update(got_down, w_down, m_w_down, v_w_down, "w_down"),
    }
    cw_upd = _adamw(pad_rows(conv_w), pad_rows(m_conv_w), pad_rows(v_conv_w), got_cw, name="adamw_conv_w")
    big["conv_w"] = [t[:CONV_WIDTH].reshape(conv_w.shape) for t in cw_upd]
    big["w_ada"] = [t.reshape(w_ada.shape) for t in ada_out]

    order = ["w_ada", "b_ada", "g_pre_mix", "g_post_mix", "w_in", "g_q_lat", "w_uq", "g_kv_lat", "w_ukv", "rel_bias",
             "sinks", "w_o", "g_pre_ffn", "g_post_ffn", "w_up", "conv_w", "conv_b", "w_down"]
    outs = [loss, grad_x.reshape(x.shape)]
    for kind in range(4):
        for nm in order:
            outs.append(big[nm][kind] if nm in big else small_out[kind][nm])
    return tuple(outs)
```

```python
import functools
import itertools
import math

import numpy as np

import jax
import jax.numpy as jnp
from jax import lax
from jax.experimental import pallas as pl
from jax.experimental.pallas import tpu as pltpu

F32 = jnp.float32
BF16 = jnp.bfloat16

N_DEV = 8
MLA_NOPE = 128
MLA_ROPE = 64
MLA_V = 128
MLA_QK = MLA_NOPE + MLA_ROPE
MLA_QK_PAD = 256
ROPE_HALF = MLA_ROPE // 2
ROPE_THETA = 10000.0
SWA_HD = 64
SWA_KVH = 4
WINDOW = 128
BLOCK = 128
REL_BUCKETS = 32
REL_MAX_DIST = 128
CONV_WIDTH = 3
EPS = 1e-6
NEG = -1e30
ADAM_LR = 0.001
ADAM_B1 = 0.9
ADAM_B2 = 0.999
ADAM_EPS = 1e-08
ADAM_WD = 0.01
ADAM_STEP = 10
LANES = 128
HALO = 16
MESH = pl.DeviceIdType.MESH
HIGHEST = lax.Precision.HIGHEST

NN = (((1,), (0,)), ((), ()))
NT = (((1,), (1,)), ((), ()))
TN = (((0,), (0,)), ((), ()))


def _tile(n, pref, align=LANES):
    if n <= pref:
        return n
    t = (pref // align) * align
    while t >= align:
        if n % t == 0:
            return t
        t -= align
    return n


def _round_up(n, m):
    return (n + m - 1) // m * m


def _params(*sem):
    return pltpu.CompilerParams(dimension_semantics=sem)


def _sigmoid(x):
    return 1.0 / (1.0 + jnp.exp(-x))


def _my_place():
    return lax.axis_index("x"), lax.axis_index("y"), lax.axis_index("c")


def _all_gather(x, *, name, in_vmem):
    space = pltpu.VMEM if in_vmem else pl.ANY

    def body(x_ref, out_ref, send_sems, recv_sems, local_sem):
        x_, y_, c_ = _my_place()
        me, sibling = (x_, y_, c_), (x_, y_, 1 - c_)
        chips = [(1 - x_, y_), (x_, 1 - y_), (1 - x_, 1 - y_)]

        def slot(px, py, pc):
            return out_ref.at[4 * px + 2 * py + pc]

        def copy(k, block, to, src=None):
            return pltpu.make_async_remote_copy(
                src_ref=slot(*block) if src is None else src,
                dst_ref=slot(*block),
                send_sem=send_sems.at[k],
                recv_sem=recv_sems.at[k],
                device_id=to,
                device_id_type=MESH,
            )

        mine = pltpu.make_async_copy(x_ref, slot(*me), local_sem)
        mine.start()
        first = [copy(0, me, sibling, src=x_ref)]
        first += [copy(1 + j, me, (*chip, c_), src=x_ref) for j, chip in enumerate(chips)]
        for cp in first:
            cp.start()
        passed = [copy(4 + j, (*chip, c_), sibling) for j, chip in enumerate(chips)]
        for j, chip in enumerate(chips):
            copy(1 + j, (*chip, c_), me).wait_recv()
            passed[j].start()
        copy(0, sibling, me).wait_recv()
        for j, chip in enumerate(chips):
            copy(4 + j, (*chip, 1 - c_), me).wait_recv()
        for cp in first + passed:
            cp.wait_send()
        mine.wait()

    return pl.pallas_call(
        body,
        name=name,
        out_shape=jax.ShapeDtypeStruct((N_DEV,) + x.shape, x.dtype),
        in_specs=[pl.BlockSpec(memory_space=space)],
        out_specs=pl.BlockSpec(memory_space=space),
        scratch_shapes=[
            pltpu.SemaphoreType.DMA((7,)),
            pltpu.SemaphoreType.DMA((7,)),
            pltpu.SemaphoreType.DMA,
        ],
    )(x)


class _Exchange:
    def __init__(self, kind, x_ref, out_ref, send_sems, recv_sems, local_sems, t):
        x_, y_, c_ = _my_place()
        me = 4 * x_ + 2 * y_ + c_

        def pair(k, src, dst, to):
            return pltpu.make_async_remote_copy(src_ref=src, dst_ref=dst, send_sem=send_sems.at[7 * t + k],
                                                recv_sem=recv_sems.at[7 * t + k], device_id=to, device_id_type=MESH)

        none = lambda: []
        if kind == "scatter":
            peers = [(x_ ^ ((r >> 2) & 1), y_ ^ ((r >> 1) & 1), c_ ^ (r & 1)) for r in range(1, N_DEV)]
            self.at_start = lambda: [pair(k, x_ref.at[4 * px + 2 * py + pc], out_ref.at[me], (px, py, pc))
                                     for k, (px, py, pc) in enumerate(peers)]
            self.relay_after, self.at_relay = none, none
            self.arrivals = self.at_start
            self.own = lambda: pltpu.make_async_copy(x_ref.at[me], out_ref.at[me], local_sems.at[t])
        else:
            sibling = (x_, y_, 1 - c_)
            chips = list(enumerate([(1 - x_, y_), (x_, 1 - y_), (1 - x_, 1 - y_)]))

            def slot(px, py, pc):
                return out_ref.at[4 * px + 2 * py + pc]

            mine = slot(x_, y_, c_)
            self.at_start = lambda: ([pair(0, x_ref, mine, sibling)]
                                     + [pair(1 + j, x_ref, mine, (*chip, c_)) for j, chip in chips])
            self.relay_after = lambda: [pair(1 + j, slot(*chip, c_), slot(*chip, c_), (*chip, c_)) for j, chip in chips]
            self.at_relay = lambda: [pair(4 + j, slot(*chip, c_), slot(*chip, c_), sibling) for j, chip in chips]
            self.arrivals = lambda: ([pair(0, slot(*sibling), slot(*sibling), sibling)]
                                     + [pair(4 + j, slot(*chip, 1 - c_), slot(*chip, 1 - c_), sibling)
                                        for j, chip in chips])
            self.own = lambda: pltpu.make_async_copy(x_ref, mine, local_sems.at[t])

    def start(self):
        self.own().start()
        for cp in self.at_start():
            cp.start()

    def relay(self):
        for landed, onward in zip(self.relay_after(), self.at_relay()):
            landed.wait_recv()
            onward.start()

    def finish(self):
        for cp in self.arrivals():
            cp.wait_recv()
        for cp in self.at_start() + self.at_relay():
            cp.wait_send()
        self.own().wait()


RELAY_AT = 0.85


def _call(body, *, name, grid, in_specs, out_specs, out_shape, args, scratch_shapes=(), sem=(), comm=(), prefetch=()):
    n_pf = len(prefetch)

    def launch(fn, ins, outs, shapes, scratch, semantics, operands):
        spec = pltpu.PrefetchScalarGridSpec(num_scalar_prefetch=n_pf, grid=grid, in_specs=ins, out_specs=outs,
                                            scratch_shapes=scratch)
        return pl.pallas_call(fn, name=name, grid_spec=spec, out_shape=shapes,
                              compiler_params=_params(*semantics))(*prefetch, *operands)

    if not comm:
        return list(launch(body, list(in_specs), list(out_specs), list(out_shape), list(scratch_shapes), sem, args)), []
    n_in, n_out, n_c, n_s = len(in_specs), len(out_specs), len(comm), len(scratch_shapes)
    kinds = [kind for kind, _ in comm]
    hbm = pl.BlockSpec(memory_space=pl.ANY)

    def wrapped(*refs):
        tables, refs = refs[:n_pf], refs[n_pf:]
        ins, cin = refs[:n_in], refs[n_in:n_in + n_c]
        at = n_in + n_c
        outs, cout = refs[at:at + n_out], refs[at + n_out:at + n_out + n_c]
        scr = refs[at + n_out + n_c:at + n_out + n_c + n_s]
        send, recv, local = refs[-3:]
        step = 0
        for a, g in enumerate(grid):
            step = step * g + pl.program_id(a)
        n_steps = int(np.prod(grid))

        def exchanges():
            return [_Exchange(kinds[t], cin[t], cout[t], send, recv, local, t) for t in range(n_c)]

        @pl.when(step == 0)
        def _():
            for ex in exchanges():
                ex.start()

        body(*tables, *ins, *outs, *scr)

        @pl.when(step == min(int(RELAY_AT * n_steps), n_steps - 1))
        def _():
            for ex in exchanges():
                ex.relay()

        @pl.when(step == n_steps - 1)
        def _():
            for ex in exchanges():
                ex.finish()

    c_shapes = [jax.ShapeDtypeStruct(((N_DEV,) + a.shape) if kind == "gather" else a.shape, a.dtype)
                for kind, a in comm]
    sems = [pltpu.SemaphoreType.DMA((7 * n_c,)), pltpu.SemaphoreType.DMA((7 * n_c,)), pltpu.SemaphoreType.DMA((n_c,))]
    res = launch(wrapped, list(in_specs) + [hbm] * n_c, list(out_specs) + [hbm] * n_c, list(out_shape) + c_shapes,
                 list(scratch_shapes) + sems, ["arbitrary"] * len(grid), (*args, *[a for _, a in comm]))
    return list(res[:n_out]), list(res[n_out:])


def _matmul(a, b, *, mode, out_dtype, name, tm=1024, tn=1024, tk=2816, precision=None, comm=(), shard_out=False,
            halves=False, m_range=None):
    if mode == "nn":
        (M, K), (K2, N) = a.shape, b.shape
    elif mode == "nt":
        (M, K), (N, K2) = (a.shape[1], 2 * a.shape[2]) if halves else a.shape, b.shape
    else:
        (K, M), (K2, N) = a.shape, (b.shape[1], 2 * b.shape[2]) if halves else b.shape
    assert K == K2, (a.shape, b.shape, mode)
    m_off = 0
    if m_range is not None:
        m_off, M = m_range
    tm = _tile(M, tm, LANES if mode == "tn" else 16)
    tk = _tile(K // 2 if halves and mode == "nt" else K, tk)
    tn = _tile(N // N_DEV, max(tn, 1408)) if shard_out else _tile(N // 2 if halves and mode == "tn" else N, tn)
    nk = K // tk
    m_off //= tm
    if mode == "tn":
        a_spec = pl.BlockSpec((tk, tm), lambda i, j, k: (k, i + m_off))
    elif halves:
        a_spec = pl.BlockSpec((None, tm, tk), lambda i, j, k: (k // (nk // 2), i, k % (nk // 2)))
    else:
        a_spec = pl.BlockSpec((tm, tk), lambda i, j, k: (i, k))
    if mode == "nt":
        b_spec = pl.BlockSpec((tn, tk), lambda i, j, k: (j, k))
    elif halves:
        nj = N // tn
        b_spec = pl.BlockSpec((None, tk, tn), lambda i, j, k: (j // (nj // 2), k, j % (nj // 2)))
    else:
        b_spec = pl.BlockSpec((tk, tn), lambda i, j, k: (k, j))
    dn = {"nn": NN, "nt": NT, "tn": TN}[mode]
    if shard_out:
        per = N // N_DEV // tn
        o_spec = pl.BlockSpec((None, tm, tn), lambda i, j, k: (j // per, i, j % per))
        o_shape = jax.ShapeDtypeStruct((N_DEV, M, N // N_DEV), out_dtype)
    else:
        o_spec = pl.BlockSpec((tm, tn), lambda i, j, k: (i, j))
        o_shape = jax.ShapeDtypeStruct((M, N), out_dtype)

    def product(a_ref, b_ref):
        return lax.dot_general(a_ref[...], b_ref[...], dn, preferred_element_type=F32, precision=precision)

    def body_one(a_ref, b_ref, o_ref):
        o_ref[...] = product(a_ref, b_ref).astype(o_ref.dtype)

    def body_acc(a_ref, b_ref, o_ref, acc_ref):
        k = pl.program_id(2)

        @pl.when(k == 0)
        def _():
            acc_ref[...] = product(a_ref, b_ref)

        @pl.when(k > 0)
        def _():
            acc_ref[...] += product(a_ref, b_ref)

        @pl.when(k == nk - 1)
        def _():
            o_ref[...] = acc_ref[...].astype(o_ref.dtype)

    outs, moved = _call(
        body_one if nk == 1 else body_acc,
        name=name,
        grid=(M // tm, N // tn, nk),
        in_specs=[a_spec, b_spec],
        out_specs=[o_spec],
        out_shape=[o_shape],
        scratch_shapes=[] if nk == 1 else [pltpu.VMEM((tm, tn), F32)],
        sem=("parallel", "parallel", "arbitrary"),
        args=(a, b),
        comm=comm,
    )
    return (outs[0], moved) if comm else outs[0]


def _rstd(xf):
    return lax.rsqrt(jnp.mean(xf * xf, axis=-1, keepdims=True) + EPS)


def _col_view(width, off):
    assert off % width == 0
    return off // width


def _prenorm(x, g, sc, sh, *, name, off=0, width=None, comm=()):
    S = x.shape[0]
    W = x.shape[1] if width is None else width
    cb = _col_view(W, off)
    tr = _tile(S, 512, 16)
    mod = sc is not None
    vec = pl.BlockSpec((1, W), lambda i: (0, 0))

    def body(*refs):
        if mod:
            x_ref, g_ref, sc_ref, sh_ref, o_ref = refs
        else:
            x_ref, g_ref, o_ref = refs
        xf = x_ref[...].astype(F32)
        y = xf * _rstd(xf) * g_ref[...]
        if mod:
            y = y * (1.0 + sc_ref[...]) + sh_ref[...]
        o_ref[...] = y.astype(o_ref.dtype)

    args = (x, g, sc, sh) if mod else (x, g)
    outs, moved = _call(
        body,
        name=name,
        grid=(S // tr,),
        in_specs=[pl.BlockSpec((tr, W), lambda i: (i, cb))] + [vec] * (len(args) - 1),
        out_specs=[pl.BlockSpec((tr, W), lambda i: (i, 0))],
        out_shape=[jax.ShapeDtypeStruct((S, W), BF16)],
        sem=("parallel",),
        args=args,
        comm=comm,
    )
    return (outs[0], moved) if comm else outs[0]


def _prenorm_bwd(x, dh, dres, g, sc, *, name, out_dtype, off=0, width=None):
    S = x.shape[0]
    W = x.shape[1] if width is None else width
    cb = _col_view(W, off)
    tr = _tile(S, 256, 16)
    mod = sc is not None
    res = dres is not None
    vec = pl.BlockSpec((1, W), lambda i: (0, 0))
    row = pl.BlockSpec((tr, W), lambda i: (i, 0))

    def body(*refs):
        it = iter(refs)
        x_ref, dh_ref = next(it), next(it)
        dres_ref = next(it) if res else None
        g_ref = next(it)
        sc_ref = next(it) if mod else None
        dx_ref, dg_ref = next(it), next(it)
        dsc_ref, dsh_ref = (next(it), next(it)) if mod else (None, None)
        i = pl.program_id(0)

        @pl.when(i == 0)
        def _():
            dg_ref[...] = jnp.zeros_like(dg_ref)
            if mod:
                dsc_ref[...] = jnp.zeros_like(dsc_ref)
                dsh_ref[...] = jnp.zeros_like(dsh_ref)

        xf = x_ref[...].astype(F32)
        r = _rstd(xf)
        xn = xf * r
        dhf = dh_ref[...].astype(F32)
        gv = g_ref[...]
        if mod:
            one_sc = 1.0 + sc_ref[...]
            dsh_ref[...] += jnp.sum(dhf, axis=0, keepdims=True)
            dsc_ref[...] += jnp.sum(dhf * (xn * gv), axis=0, keepdims=True)
            dg_ref[...] += jnp.sum(dhf * xn * one_sc, axis=0, keepdims=True)
            dxn = dhf * (gv * one_sc)
        else:
            dg_ref[...] += jnp.sum(dhf * xn, axis=0, keepdims=True)
            dxn = dhf * gv
        dx = r * (dxn - xn * jnp.mean(dxn * xn, axis=-1, keepdims=True))
        if res:
            dx = dx + dres_ref[...]
        dx_ref[...] = dx.astype(dx_ref.dtype)

    args = [x, dh] + ([dres] if res else []) + [g] + ([sc] if mod else [])
    in_specs = [pl.BlockSpec((tr, W), lambda i: (i, cb)), row] + ([row] if res else []) + [vec] + ([vec] if mod else [])
    n_vec = 3 if mod else 1
    outs = pl.pallas_call(
        body,
        name=name,
        grid=(S // tr,),
        in_specs=in_specs,
        out_specs=[row] + [vec] * n_vec,
        out_shape=[jax.ShapeDtypeStruct((S, W), out_dtype)] + [jax.ShapeDtypeStruct((1, W), F32)] * n_vec,
        compiler_params=_params("arbitrary"),
    )(*args)
    return outs


def _postnorm_res(x, y, gt, g, *, name):
    S, D = x.shape
    tr = _tile(S, 512, 8)
    row = pl.BlockSpec((tr, D), lambda i: (i, 0))
    vec = pl.BlockSpec((1, D), lambda i: (0, 0))

    def body(x_ref, y_ref, gt_ref, g_ref, o_ref):
        yf = y_ref[...]
        o_ref[...] = x_ref[...] + gt_ref[...] * (yf * _rstd(yf) * g_ref[...])

    return pl.pallas_call(
        body,
        name=name,
        grid=(S // tr,),
        in_specs=[row, row, vec, vec],
        out_specs=row,
        out_shape=jax.ShapeDtypeStruct((S, D), F32),
        compiler_params=_params("parallel"),
    )(x, y, gt, g)


def _postnorm_bwd(dx1, y, gt, g, *, name):
    S, D = y.shape
    tr = _tile(S, 256, 16)
    row = pl.BlockSpec((tr, D), lambda i: (i, 0))
    vec = pl.BlockSpec((1, D), lambda i: (0, 0))

    def body(dx_ref, y_ref, gt_ref, g_ref, dy_ref, dgt_ref, dg_ref):
        @pl.when(pl.program_id(0) == 0)
        def _():
            dgt_ref[...] = jnp.zeros_like(dgt_ref)
            dg_ref[...] = jnp.zeros_like(dg_ref)

        yf = y_ref[...]
        r = _rstd(yf)
        yn = yf * r
        d = dx_ref[...]
        gtv, gv = gt_ref[...], g_ref[...]
        dgt_ref[...] += jnp.sum(d * (yn * gv), axis=0, keepdims=True)
        dg_ref[...] += jnp.sum(d * gtv * yn, axis=0, keepdims=True)
        dyn = d * (gtv * gv)
        dy_ref[...] = (r * (dyn - yn * jnp.mean(dyn * yn, axis=-1, keepdims=True))).astype(dy_ref.dtype)

    return pl.pallas_call(
        body,
        name=name,
        grid=(S // tr,),
        in_specs=[row, row, vec, vec],
        out_specs=[row, vec, vec],
        out_shape=[jax.ShapeDtypeStruct((S, D), BF16), jax.ShapeDtypeStruct((1, D), F32),
                   jax.ShapeDtypeStruct((1, D), F32)],
        compiler_params=_params("arbitrary"),
    )(dx1, y, gt, g)


def _final_loss(x1, y, target, gt, g, *, name):
    S, D = y.shape
    tr = _tile(S, 256, 16)
    row = pl.BlockSpec((tr, D), lambda i: (i, 0))
    vec = pl.BlockSpec((1, D), lambda i: (0, 0))
    one = pl.BlockSpec((1, LANES), lambda i: (0, 0))

    def body(x_ref, y_ref, t_ref, gt_ref, g_ref, loss_ref, dout_ref, dy_ref, dgt_ref, dg_ref):
        @pl.when(pl.program_id(0) == 0)
        def _():
            loss_ref[...] = jnp.zeros_like(loss_ref)
            dgt_ref[...] = jnp.zeros_like(dgt_ref)
            dg_ref[...] = jnp.zeros_like(dg_ref)

        yf = y_ref[...]
        r = _rstd(yf)
        yn = yf * r
        gtv, gv = gt_ref[...], g_ref[...]
        out = x_ref[...] + gtv * (yn * gv)
        diff = out - t_ref[...]
        per_tok = jnp.mean(diff * diff, axis=-1, keepdims=True)
        loss_ref[...] += 0.5 * jnp.sum(per_tok, axis=0, keepdims=True)
        d = diff / D
        dout_ref[...] = d
        dgt_ref[...] += jnp.sum(d * (yn * gv), axis=0, keepdims=True)
        dg_ref[...] += jnp.sum(d * gtv * yn, axis=0, keepdims=True)
        dyn = d * (gtv * gv)
        dy_ref[...] = (r * (dyn - yn * jnp.mean(dyn * yn, axis=-1, keepdims=True))).astype(dy_ref.dtype)

    return pl.pallas_call(
        body,
        name=name,
        grid=(S // tr,),
        in_specs=[row, row, row, vec, vec],
        out_specs=[one, row, row, vec, vec],
        out_shape=[jax.ShapeDtypeStruct((1, LANES), F32), jax.ShapeDtypeStruct((S, D), F32),
                   jax.ShapeDtypeStruct((S, D), BF16), jax.ShapeDtypeStruct((1, D), F32),
                   jax.ShapeDtypeStruct((1, D), F32)],
        compiler_params=_params("arbitrary"),
    )(x1, y, target, gt, g)


def _ada_fwd(c_all, w_local, b_cols, *, name):
    B, D = c_all.shape
    N = w_local.shape[1]
    tn = _tile(N, 512)

    def body(c_ref, w_ref, b_ref, ca_ref, mod_ref):
        cv = c_ref[...]
        ca = cv * _sigmoid(cv)
        ca_ref[...] = ca
        mod_ref[...] = jnp.dot(ca, w_ref[...], preferred_element_type=F32, precision=HIGHEST) + b_ref[...]

    return pl.pallas_call(
        body,
        name=name,
        grid=(N // tn,),
        in_specs=[pl.BlockSpec((B, D), lambda j: (0, 0)), pl.BlockSpec((D, tn), lambda j: (0, j)),
                  pl.BlockSpec((1, tn), lambda j: (0, j))],
        out_specs=[pl.BlockSpec((B, D), lambda j: (0, 0)), pl.BlockSpec((B, tn), lambda j: (0, j))],
        out_shape=[jax.ShapeDtypeStruct((B, D), F32), jax.ShapeDtypeStruct((B, N), F32)],
        compiler_params=_params("arbitrary"),
    )(c_all, w_local, b_cols)


def _rope_tables(S, width, lane_off):
    pos = jnp.arange(S, dtype=F32)
    inv = ROPE_THETA ** (-jnp.arange(0, MLA_ROPE, 2, dtype=F32) / MLA_ROPE)
    ang = pos[:, None] * inv[None, :]
    ang = jnp.concatenate([ang, ang], axis=-1)
    cos, sin = jnp.cos(ang), jnp.sin(ang)
    first = (jnp.arange(MLA_ROPE) < ROPE_HALF)[None, :]
    sa = jnp.where(first, -sin, 0.0)
    sb = jnp.where(first, 0.0, sin)

    def place(t, fill):
        return jnp.pad(t, ((0, 0), (lane_off, width - lane_off - MLA_ROPE)), constant_values=fill)

    return place(cos, 1.0), place(sa, 0.0), place(sb, 0.0)


def _rope_apply(x, cos, sa, sb, width, transpose):
    if transpose:
        return x * cos + pltpu.roll(x * sa, ROPE_HALF, 1) + pltpu.roll(x * sb, width - ROPE_HALF, 1)
    return x * cos + pltpu.roll(x, width - ROPE_HALF, 1) * sa + pltpu.roll(x, ROPE_HALF, 1) * sb


def _rope(x, tables, *, heads, width, transpose, name, off=0, scale=1.0):
    S = x.shape[0]
    cb = _col_view(width, off)
    tr = _tile(S, 512, 16)
    tab = pl.BlockSpec((tr, width), lambda i, h: (i, 0))

    def body(x_ref, c_ref, sa_ref, sb_ref, o_ref):
        y = _rope_apply(x_ref[...].astype(F32), c_ref[...], sa_ref[...], sb_ref[...], width, transpose)
        o_ref[...] = (y if scale == 1.0 else y * scale).astype(o_ref.dtype)

    return pl.pallas_call(
        body,
        name=name,
        grid=(S // tr, heads),
        in_specs=[pl.BlockSpec((tr, width), lambda i, h: (i, cb + h)), tab, tab, tab],
        out_specs=pl.BlockSpec((tr, width), lambda i, h: (i, h)),
        out_shape=jax.ShapeDtypeStruct((S, heads * width), BF16),
        compiler_params=_params("parallel", "parallel"),
    )(x, *tables)


def _shared_rope_grad(parts, tables, *, name):
    P, S, _ = parts.shape
    tr = _tile(S, 512, 16)
    tab = pl.BlockSpec((tr, LANES), lambda i: (i, 0))

    def body(p_ref, c_ref, sa_ref, sb_ref, o_ref):
        acc = p_ref[0]
        for k in range(1, P):
            acc = acc + p_ref[k]
        o_ref[...] = _rope_apply(acc, c_ref[...], sa_ref[...], sb_ref[...], LANES, True).astype(o_ref.dtype)

    return pl.pallas_call(
        body,
        name=name,
        grid=(S // tr,),
        in_specs=[pl.BlockSpec((P, tr, LANES), lambda i: (0, i, 0)), tab, tab, tab],
        out_specs=tab,
        out_shape=jax.ShapeDtypeStruct((S, LANES), BF16),
        compiler_params=_params("parallel"),
    )(parts, *tables)


MLA_SCALE = MLA_QK ** -0.5
LOG2E = math.log2(math.e)
LN2 = math.log(2.0)
MLA_Q_PRESCALE = MLA_SCALE * LOG2E


def _lane_tile(v, n):
    return v if n == LANES else jnp.tile(v, (1, n // LANES))


def _causal_mask(s):
    rows = lax.broadcasted_iota(jnp.int32, s.shape, 0)
    cols = lax.broadcasted_iota(jnp.int32, s.shape, 1)
    return jnp.where(cols <= rows, s, NEG)


def _tri_blocks(nb, q_major):
    if q_major:
        pairs = [(q, k) for q in range(nb) for k in range(q + 1)]
    else:
        pairs = [(q, k) for k in range(nb) for q in range(k, nb)]
    return (jnp.asarray(np.array([p[0] for p in pairs], np.int32)),
            jnp.asarray(np.array([p[1] for p in pairs], np.int32)))


HEAD_PAIR = 4


def _flash_fwd(q_raw, KV, krr, tables, *, heads, name, comm=()):
    S = q_raw.shape[0]
    t = _tile(S, 512)
    nb = S // t
    qt, kt = _tri_blocks(nb, True)
    qw, vw = HEAD_PAIR * MLA_QK_PAD, HEAD_PAIR * MLA_V

    def body(qt_ref, kt_ref, q_ref, *rest):
        kn_refs, kr_ref, v_refs = rest[:HEAD_PAIR], rest[HEAD_PAIR], rest[HEAD_PAIR + 1:2 * HEAD_PAIR + 1]
        c_ref, sa_ref, sb_ref, o_ref, lse_ref, qr_ref, m_scr, l_scr, acc_scr = rest[2 * HEAD_PAIR + 1:]
        step_id = pl.program_id(1)
        qi, ki = qt_ref[step_id], kt_ref[step_id]

        @pl.when(ki == 0)
        def _():
            m_scr[...] = jnp.full_like(m_scr, NEG)
            l_scr[...] = jnp.zeros_like(l_scr)
            acc_scr[...] = jnp.zeros_like(acc_scr)
            for h in range(HEAD_PAIR):
                base = h * MLA_QK_PAD
                nope = q_ref[:, base:base + MLA_NOPE].astype(F32) * MLA_Q_PRESCALE
                rot = _rope_apply(q_ref[:, base + MLA_NOPE:base + MLA_QK_PAD].astype(F32), c_ref[...], sa_ref[...],
                                  sb_ref[...], LANES, False) * MLA_Q_PRESCALE
                qr_ref[:, base:base + MLA_NOPE] = nope.astype(qr_ref.dtype)
                qr_ref[:, base + MLA_NOPE:base + MLA_QK_PAD] = rot.astype(qr_ref.dtype)

        def step(diagonal):
            for h, (kn_ref, v_ref) in enumerate(zip(kn_refs, v_refs)):
                cols = slice(h * MLA_QK_PAD, (h + 1) * MLA_QK_PAD)
                k = jnp.concatenate([kn_ref[...], kr_ref[...]], axis=1)
                s = lax.dot_general(qr_ref[:, cols], k, NT, preferred_element_type=F32)
                if diagonal:
                    s = _causal_mask(s)
                m_prev = m_scr[h]
                m_new = jnp.maximum(m_prev, jnp.max(s, axis=1, keepdims=True))
                alpha = jnp.exp2(m_prev - m_new)
                p = jnp.exp2(s - _lane_tile(m_new, t))
                l_new = alpha * l_scr[h] + jnp.sum(p, axis=1, keepdims=True)
                acc = alpha * acc_scr[h] + jnp.dot(p.astype(BF16), v_ref[...], preferred_element_type=F32)
                if diagonal:
                    o_ref[:, h * MLA_V:(h + 1) * MLA_V] = (acc / l_new).astype(o_ref.dtype)
                    lse_ref[h] = m_new + jnp.log(l_new) * LOG2E
                else:
                    l_scr[h], acc_scr[h], m_scr[h] = l_new, acc, m_new

        pl.when(ki < qi)(lambda: step(False))
        pl.when(ki == qi)(lambda: step(True))

    def kvspec(h, half):
        return pl.BlockSpec((t, LANES), lambda hp, s, qt, kt: (kt[s], 2 * (HEAD_PAIR * hp + h) + half))

    qtab = pl.BlockSpec((t, LANES), lambda hp, s, qt, kt: (qt[s], 0))
    qrow = lambda hp, s, qt, kt: (qt[s], hp)
    return _call(
        body,
        name=name,
        grid=(heads // HEAD_PAIR, int(qt.shape[0])),
        in_specs=[pl.BlockSpec((t, qw), qrow), *[kvspec(h, 0) for h in range(HEAD_PAIR)],
                  pl.BlockSpec((t, LANES), lambda hp, s, qt, kt: (kt[s], 0)),
                  *[kvspec(h, 1) for h in range(HEAD_PAIR)], qtab, qtab, qtab],
        out_specs=[pl.BlockSpec((t, vw), qrow),
                   pl.BlockSpec((HEAD_PAIR, t, LANES), lambda hp, s, qt, kt: (hp, qt[s], 0)),
                   pl.BlockSpec((t, qw), qrow)],
        out_shape=[jax.ShapeDtypeStruct((S, heads * MLA_V), BF16),
                   jax.ShapeDtypeStruct((heads, S, LANES), F32),
                   jax.ShapeDtypeStruct((S, heads * MLA_QK_PAD), BF16)],
        scratch_shapes=[pltpu.VMEM((HEAD_PAIR, t, LANES), F32), pltpu.VMEM((HEAD_PAIR, t, LANES), F32),
                        pltpu.VMEM((HEAD_PAIR, t, MLA_V), F32)],
        sem=("parallel", "arbitrary"),
        args=(q_raw, *[KV] * HEAD_PAIR, krr, *[KV] * HEAD_PAIR, *tables),
        comm=comm,
        prefetch=(qt, kt),
    )


def _flash_bwd(Q, KV, krr, dO, O, lse, tables, *, heads, name, comm=()):
    S = Q.shape[0]
    t = _tile(S, 512)
    nb = S // t
    qt, kt = _tri_blocks(nb, False)
    n_steps = int(qt.shape[0])
    qw, vw = HEAD_PAIR * MLA_QK_PAD, HEAD_PAIR * MLA_V

    def body(qt_ref, kt_ref, q_ref, *rest):
        kn_refs, kr_ref, v_refs = rest[:HEAD_PAIR], rest[HEAD_PAIR], rest[HEAD_PAIR + 1:2 * HEAD_PAIR + 1]
        (do_ref, o_ref, lse_ref, c_ref, sa_ref, sb_ref, dq_ref, dkv_ref, dkr_ref,
         dq_scr, dk_scr, dv_scr, delta_scr) = rest[2 * HEAD_PAIR + 1:]
        step_id = pl.program_id(1)
        qi, ki = qt_ref[step_id], kt_ref[step_id]

        @pl.when(ki == 0)
        def _():
            for h in range(HEAD_PAIR):
                vc = slice(h * MLA_V, (h + 1) * MLA_V)
                d = jnp.sum(do_ref[:, vc].astype(F32) * o_ref[:, vc].astype(F32), axis=1, keepdims=True)
                delta_scr[h, qi] = jnp.broadcast_to(d, (t, LANES))

        def step(diagonal):
            for h, (kn_ref, v_ref) in enumerate(zip(kn_refs, v_refs)):
                base = h * MLA_QK_PAD
                cols = slice(base, base + MLA_QK_PAD)
                vc = slice(h * MLA_V, (h + 1) * MLA_V)
                q, do = q_ref[:, cols], do_ref[:, vc]
                k = jnp.concatenate([kn_ref[...], kr_ref[...]], axis=1)
                s = lax.dot_general(q, k, NT, preferred_element_type=F32)
                if diagonal:
                    s = _causal_mask(s)
                p = jnp.exp2(s - _lane_tile(lse_ref[h], t))
                dv = lax.dot_general(p.astype(BF16), do, TN, preferred_element_type=F32)
                dp = lax.dot_general(do, v_ref[...], NT, preferred_element_type=F32)
                ds = (p * (dp - _lane_tile(delta_scr[h, qi], t))).astype(BF16)
                dk = lax.dot_general(ds, q, TN, preferred_element_type=F32)
                dq = jnp.dot(ds, k, preferred_element_type=F32)
                if diagonal:
                    dk_scr[h], dv_scr[h] = dk, dv
                    dq = (dq_scr[qi, :, cols] + dq) * (LN2 * MLA_Q_PRESCALE)
                    rot = _rope_apply(dq[:, MLA_NOPE:], c_ref[...], sa_ref[...], sb_ref[...], LANES, True)
                    dq_ref[:, base:base + MLA_NOPE] = dq[:, :MLA_NOPE].astype(dq_ref.dtype)
                    dq_ref[:, base + MLA_NOPE:base + MLA_QK_PAD] = rot.astype(dq_ref.dtype)
                else:
                    dk_scr[h] += dk
                    dv_scr[h] += dv
                    dq_scr[qi, :, cols] += dq

        @pl.when(ki == 0)
        def _():
            dq_scr[qi] = jnp.zeros((t, qw), F32)

        pl.when(qi > ki)(lambda: step(False))
        pl.when(qi == ki)(lambda: step(True))

        @pl.when(qi == nb - 1)
        def _():
            shared = jnp.zeros((t, LANES), F32)
            for h in range(HEAD_PAIR):
                base = h * MLA_QK_PAD
                dk = dk_scr[h] * LN2
                dkv_ref[:, base:base + MLA_NOPE] = dk[:, :MLA_NOPE].astype(dkv_ref.dtype)
                dkv_ref[:, base + MLA_NOPE:base + MLA_QK_PAD] = dv_scr[h].astype(dkv_ref.dtype)
                shared = shared + dk[:, MLA_NOPE:]
            dkr_ref[0] = shared

    def kvspec(h, half):
        return pl.BlockSpec((t, LANES), lambda hp, s, qt, kt: (kt[s], 2 * (HEAD_PAIR * hp + h) + half))

    qrow = lambda hp, s, qt, kt: (qt[s], hp)
    krow = lambda hp, s, qt, kt: (kt[s], hp)
    ktab = pl.BlockSpec((t, LANES), lambda hp, s, qt, kt: (kt[s], 0))
    return _call(
        body,
        name=name,
        grid=(heads // HEAD_PAIR, n_steps),
        in_specs=[pl.BlockSpec((t, qw), qrow), *[kvspec(h, 0) for h in range(HEAD_PAIR)], ktab,
                  *[kvspec(h, 1) for h in range(HEAD_PAIR)],
                  pl.BlockSpec((t, vw), qrow), pl.BlockSpec((t, vw), qrow),
                  pl.BlockSpec((HEAD_PAIR, t, LANES), lambda hp, s, qt, kt: (hp, qt[s], 0)),
                  ktab, ktab, ktab],
        out_specs=[pl.BlockSpec((t, qw), krow), pl.BlockSpec((t, qw), krow),
                   pl.BlockSpec((1, t, LANES), lambda hp, s, qt, kt: (hp, kt[s], 0))],
        out_shape=[jax.ShapeDtypeStruct((S, heads * MLA_QK_PAD), BF16),
                   jax.ShapeDtypeStruct((S, heads * MLA_QK_PAD), BF16),
                   jax.ShapeDtypeStruct((heads // HEAD_PAIR, S, LANES), F32)],
        scratch_shapes=[pltpu.VMEM((nb, t, qw), F32), pltpu.VMEM((HEAD_PAIR, t, MLA_QK_PAD), F32),
                        pltpu.VMEM((HEAD_PAIR, t, MLA_V), F32), pltpu.VMEM((HEAD_PAIR, nb, t, LANES), F32)],
        sem=("parallel", "arbitrary"),
        args=(Q, *[KV] * HEAD_PAIR, krr, *[KV] * HEAD_PAIR, dO, O, lse, *tables),
        comm=comm,
        prefetch=(qt, kt),
    )


SWA_SCALE = SWA_HD ** -0.5


def _t5_bucket_table():
    a = np.arange(BLOCK)[:, None]
    j = np.arange(2 * BLOCK)[None, :]
    dist = BLOCK + a - j
    max_exact = REL_BUCKETS // 2
    n = np.maximum(dist, 0)
    large = max_exact + (np.log(np.maximum(n, 1).astype(np.float32) / np.float32(max_exact))
                         / np.float32(math.log(REL_MAX_DIST / max_exact))
                         * np.float32(REL_BUCKETS - max_exact)).astype(np.int32)
    large = np.minimum(large, REL_BUCKETS - 1)
    bucket = np.where(n < max_exact, n, large)
    valid = (dist >= 0) & (dist < WINDOW)
    return bucket.astype(np.int32), valid


def _heads_to_rows(x, G):
    return jnp.concatenate([x[:, g * SWA_HD:(g + 1) * SWA_HD] for g in range(G)], axis=0)


def _rows_to_heads(o_ref, x, G):
    for g in range(G):
        o_ref[:, g * SWA_HD:(g + 1) * SWA_HD] = x[g * BLOCK:(g + 1) * BLOCK].astype(o_ref.dtype)


def _swa_probs(q_ref, kp_ref, kc_ref, bias_ref, sink_ref, qb, G):
    q2 = _heads_to_rows(q_ref[...], G)
    kb = jnp.concatenate([kp_ref[0], kc_ref[0]], axis=0)
    s = lax.dot_general(kb, q2, NT, preferred_element_type=F32) * SWA_SCALE + bias_ref[0]
    keys = lax.broadcasted_iota(jnp.int32, s.shape, 0)
    s = jnp.where((keys >= BLOCK) | (qb > 0), s, NEG)
    sink = sink_ref[0]
    m = jnp.maximum(jnp.max(s, axis=0, keepdims=True), sink)
    e = jnp.exp(s - m)
    es = jnp.exp(sink - m)
    inv = 1.0 / (jnp.sum(e, axis=0, keepdims=True) + es)
    return q2, kb, e * inv, es * inv


def _swa_fwd(q, q_off, k, v, bias_t, sink, *, name, comm=()):
    S = k.shape[1]
    G = bias_t.shape[2] // BLOCK
    nb = S // BLOCK
    qcol = _col_view(G * SWA_HD, q_off)
    cur = lambda kh, qb: (kh, qb, 0)
    prev = lambda kh, qb: (kh, jnp.maximum(qb - 1, 0), 0)
    kvspec = lambda im: pl.BlockSpec((1, BLOCK, SWA_HD), im)

    def body(q_ref, kc_ref, kp_ref, vc_ref, vp_ref, bias_ref, sink_ref, o_ref):
        qb = pl.program_id(1)
        _, _, pt, _ = _swa_probs(q_ref, kp_ref, kc_ref, bias_ref, sink_ref, qb, G)
        vb = jnp.concatenate([vp_ref[0], vc_ref[0]], axis=0)
        _rows_to_heads(o_ref, lax.dot_general(pt.astype(BF16), vb, TN, preferred_element_type=F32), G)

    outs, moved = _call(
        body,
        name=name,
        grid=(SWA_KVH, nb),
        in_specs=[pl.BlockSpec((BLOCK, G * SWA_HD), lambda kh, qb: (qb, qcol + kh)),
                  kvspec(cur), kvspec(prev), kvspec(cur), kvspec(prev),
                  pl.BlockSpec((1, 2 * BLOCK, G * BLOCK), lambda kh, qb: (kh, 0, 0)),
                  pl.BlockSpec((1, 1, G * BLOCK), lambda kh, qb: (kh, 0, 0))],
        out_specs=[pl.BlockSpec((BLOCK, G * SWA_HD), lambda kh, qb: (qb, kh))],
        out_shape=[jax.ShapeDtypeStruct((S, SWA_KVH * G * SWA_HD), BF16)],
        sem=("parallel", "parallel"),
        args=(q, k, k, v, v, bias_t, sink),
        comm=comm,
    )
    return outs[0], moved


def _swa_bwd(q, q_off, k, v, bias_t, sink, do, *, name, comm=()):
    S = k.shape[1]
    G = bias_t.shape[2] // BLOCK
    nb = S // BLOCK
    qcol = _col_view(G * SWA_HD, q_off)
    cur = lambda kh, qb: (kh, jnp.minimum(qb, nb - 1), 0)
    prev = lambda kh, qb: (kh, jnp.maximum(jnp.minimum(qb, nb - 1) - 1, 0), 0)
    lag = lambda kh, qb: (kh, jnp.maximum(qb - 1, 0), 0)
    kvspec = lambda im: pl.BlockSpec((1, BLOCK, SWA_HD), im)

    def body(q_ref, kc_ref, kp_ref, vc_ref, vp_ref, bias_ref, sink_ref, do_ref,
             dq_ref, dk_ref, dv_ref, dbias_ref, dsink_ref, ck_scr, cv_scr):
        qb = pl.program_id(1)

        @pl.when(qb == 0)
        def _():
            dbias_ref[...] = jnp.zeros_like(dbias_ref)
            dsink_ref[...] = jnp.zeros_like(dsink_ref)
            ck_scr[...] = jnp.zeros_like(ck_scr)
            cv_scr[...] = jnp.zeros_like(cv_scr)

        @pl.when(qb < nb)
        def _():
            q2, kb, pt, ps = _swa_probs(q_ref, kp_ref, kc_ref, bias_ref, sink_ref, qb, G)
            vb = jnp.concatenate([vp_ref[0], vc_ref[0]], axis=0)
            do2 = _heads_to_rows(do_ref[...], G)
            dpt = lax.dot_general(vb, do2, NT, preferred_element_type=F32)
            delta = jnp.sum(dpt * pt, axis=0, keepdims=True)
            dst = pt * (dpt - delta)
            dbias_ref[0] += dst
            dsink_ref[0] += -ps * delta
            dsb = (dst * SWA_SCALE).astype(BF16)
            _rows_to_heads(dq_ref, lax.dot_general(dsb, kb, TN, preferred_element_type=F32), G)
            dkb = jnp.dot(dsb, q2, preferred_element_type=F32)
            dvb = jnp.dot(pt.astype(BF16), do2, preferred_element_type=F32)
            dk_ref[0] = (ck_scr[...] + dkb[:BLOCK]).astype(dk_ref.dtype)
            dv_ref[0] = (cv_scr[...] + dvb[:BLOCK]).astype(dv_ref.dtype)
            ck_scr[...] = dkb[BLOCK:]
            cv_scr[...] = dvb[BLOCK:]

        @pl.when(qb == nb)
        def _():
            dk_ref[0] = ck_scr[...].astype(dk_ref.dtype)
            dv_ref[0] = cv_scr[...].astype(dv_ref.dtype)

    tspec = pl.BlockSpec((BLOCK, G * SWA_HD), lambda kh, qb: (jnp.minimum(qb, nb - 1), kh))
    return _call(
        body,
        name=name,
        grid=(SWA_KVH, nb + 1),
        in_specs=[pl.BlockSpec((BLOCK, G * SWA_HD), lambda kh, qb: (jnp.minimum(qb, nb - 1), qcol + kh)),
                  kvspec(cur), kvspec(prev), kvspec(cur), kvspec(prev),
                  pl.BlockSpec((1, 2 * BLOCK, G * BLOCK), lambda kh, qb: (kh, 0, 0)),
                  pl.BlockSpec((1, 1, G * BLOCK), lambda kh, qb: (kh, 0, 0)),
                  pl.BlockSpec((BLOCK, G * SWA_HD), lambda kh, qb: (jnp.minimum(qb, nb - 1), kh))],
        out_specs=[tspec, kvspec(lag), kvspec(lag),
                   pl.BlockSpec((1, 2 * BLOCK, G * BLOCK), lambda kh, qb: (kh, 0, 0)),
                   pl.BlockSpec((1, 1, G * BLOCK), lambda kh, qb: (kh, 0, 0))],
        out_shape=[jax.ShapeDtypeStruct((S, SWA_KVH * G * SWA_HD), BF16),
                   jax.ShapeDtypeStruct((SWA_KVH, S, SWA_HD), BF16),
                   jax.ShapeDtypeStruct((SWA_KVH, S, SWA_HD), BF16),
                   jax.ShapeDtypeStruct((SWA_KVH, 2 * BLOCK, G * BLOCK), F32),
                   jax.ShapeDtypeStruct((SWA_KVH, 1, G * BLOCK), F32)],
        scratch_shapes=[pltpu.VMEM((BLOCK, SWA_HD), F32), pltpu.VMEM((BLOCK, SWA_HD), F32)],
        sem=("parallel", "arbitrary"),
        args=(q, k, k, v, v, bias_t, sink, do),
        comm=comm,
    )


def _gate_mix(z, o_a, o_b, *, D, off_a, off_b, name):
    S = z.shape[0]
    tr = _tile(S, 256, 16)
    row = pl.BlockSpec((tr, D), lambda i: (i, 0))
    ca, cb = _col_view(D, off_a), _col_view(D, off_b)

    def body(ga_ref, gb_ref, oa_ref, ob_ref, m_ref):
        m = (_sigmoid(ga_ref[...].astype(F32)) * oa_ref[...].astype(F32)
             + _sigmoid(gb_ref[...].astype(F32)) * ob_ref[...].astype(F32))
        m_ref[...] = m.astype(m_ref.dtype)

    return pl.pallas_call(
        body,
        name=name,
        grid=(S // tr,),
        in_specs=[pl.BlockSpec((tr, D), lambda i: (i, ca)), pl.BlockSpec((tr, D), lambda i: (i, cb)), row, row],
        out_specs=row,
        out_shape=jax.ShapeDtypeStruct((S, D), BF16),
        compiler_params=_params("parallel"),
    )(z, z, o_a, o_b)


def _gate_mix_bwd(dm, z, o_a, o_b, *, D, off_a, off_b, name):
    S = z.shape[0]
    tr = _tile(S, 256, 16)
    row = pl.BlockSpec((tr, D), lambda i: (i, 0))
    ca, cb = _col_view(D, off_a), _col_view(D, off_b)

    def body(dm_ref, ga_ref, gb_ref, oa_ref, ob_ref, dga_ref, dgb_ref, doa_ref, dob_ref):
        d = dm_ref[...].astype(F32)
        for g_ref, o_ref, dg_ref, do_ref in ((ga_ref, oa_ref, dga_ref, doa_ref), (gb_ref, ob_ref, dgb_ref, dob_ref)):
            sg = _sigmoid(g_ref[...].astype(F32))
            dg_ref[...] = (d * o_ref[...].astype(F32) * (sg * (1.0 - sg))).astype(dg_ref.dtype)
            do_ref[...] = (d * sg).astype(do_ref.dtype)

    return pl.pallas_call(
        body,
        name=name,
        grid=(S // tr,),
        in_specs=[row, pl.BlockSpec((tr, D), lambda i: (i, ca)), pl.BlockSpec((tr, D), lambda i: (i, cb)), row, row],
        out_specs=[row] * 4,
        out_shape=[jax.ShapeDtypeStruct((S, D), BF16)] * 4,
        compiler_params=_params("parallel"),
    )(dm, z, z, o_a, o_b)


CONV_ROWS = 256
CONV_COLS = 1408
SUBLANES = 8


def _shift_matrices(tr):
    r = np.arange(tr)[:, None]
    c = np.arange(tr)[None, :]
    back = [jnp.asarray(r == c + d, dtype=BF16) for d in (1, 2)]
    ahead = [jnp.asarray(r + d == c, dtype=BF16) for d in (1, 2)]
    return back, ahead


def _rows_before(x, halo_ref, first, b1_ref, b2_ref):
    s1 = jnp.dot(b1_ref[...], x, preferred_element_type=F32)
    s2 = jnp.dot(b2_ref[...], x, preferred_element_type=F32)
    h8 = jnp.where(first, 0.0, halo_ref[...].astype(F32)[HALO - SUBLANES:])
    rows = lax.broadcasted_iota(jnp.int32, h8.shape, 0)
    fix1 = jnp.where(rows < 1, pltpu.roll(h8, 1, 0), 0.0)
    fix2 = jnp.where(rows < 2, pltpu.roll(h8, 2, 0), 0.0)
    s1 = jnp.concatenate([s1[:SUBLANES] + fix1, s1[SUBLANES:]], axis=0)
    s2 = jnp.concatenate([s2[:SUBLANES] + fix2, s2[SUBLANES:]], axis=0)
    return s1, s2


def _conv_taps(x, s1, s2, cw_ref, cb_ref):
    return cb_ref[...] + cw_ref[0:1, :] * s2 + cw_ref[1:2, :] * s1 + cw_ref[2:3, :] * x


def _conv_gate(up, cw, cb, *, name):
    S, F2 = up.shape
    F = F2 // 2
    tr = _tile(S, CONV_ROWS, HALO)
    tc = _tile(F, CONV_COLS)
    nc = F // tc
    hb = tr // HALO
    back, _ = _shift_matrices(tr)
    mat = pl.BlockSpec((tr, tr), lambda i, j: (0, 0))

    def halo_map(shift):
        return lambda i, j: (jnp.maximum(i * hb - 1, 0), j + shift)

    def body(x1_ref, h1_ref, x2_ref, h2_ref, cw1_ref, cw2_ref, cb1_ref, cb2_ref, b1_ref, b2_ref, a_ref):
        first = pl.program_id(0) == 0
        us = []
        for x_ref, h_ref, cw_ref, cb_ref in ((x1_ref, h1_ref, cw1_ref, cb1_ref), (x2_ref, h2_ref, cw2_ref, cb2_ref)):
            x = x_ref[...]
            s1, s2 = _rows_before(x, h_ref, first, b1_ref, b2_ref)
            us.append(_conv_taps(x.astype(F32), s1, s2, cw_ref, cb_ref))
        u1, u2 = us
        a_ref[...] = (u1 * _sigmoid(u1) * u2).astype(a_ref.dtype)

    return pl.pallas_call(
        body,
        name=name,
        grid=(S // tr, nc),
        in_specs=[pl.BlockSpec((tr, tc), lambda i, j: (i, j)), pl.BlockSpec((HALO, tc), halo_map(0)),
                  pl.BlockSpec((tr, tc), lambda i, j: (i, j + nc)), pl.BlockSpec((HALO, tc), halo_map(nc)),
                  pl.BlockSpec((CONV_WIDTH, tc), lambda i, j: (0, j)),
                  pl.BlockSpec((CONV_WIDTH, tc), lambda i, j: (0, j + nc)),
                  pl.BlockSpec((1, tc), lambda i, j: (0, j)), pl.BlockSpec((1, tc), lambda i, j: (0, j + nc)),
                  mat, mat],
        out_specs=pl.BlockSpec((tr, tc), lambda i, j: (i, j)),
        out_shape=jax.ShapeDtypeStruct((S, F), BF16),
        compiler_params=_params("parallel", "parallel"),
    )(up, up, up, up, cw, cw, cb, cb, *back)


def _conv_gate_bwd(up, da, cw, cb, *, name, comm=()):
    S, F2 = up.shape
    F = F2 // 2
    tr = _tile(S, CONV_ROWS, HALO)
    tc = _tile(F, CONV_COLS)
    nc = F // tc
    hb = tr // HALO
    ni = S // tr
    back, ahead = _shift_matrices(tr)
    mat = pl.BlockSpec((tr, tr), lambda j, r: (0, 0))

    def cur(shift):
        return lambda j, r: (ni - 1 - r, j + shift)

    def before(shift):
        return lambda j, r: (jnp.maximum((ni - 1 - r) * hb - 1, 0), j + shift)

    def vec(rows, shift):
        return pl.BlockSpec((rows, tc), lambda j, r: (0, j + shift))

    def body(x1_ref, h1_ref, x2_ref, h2_ref, da_ref, cw1_ref, cw2_ref, cb1_ref, cb2_ref,
             b1_ref, b2_ref, a1_ref, a2_ref, dup_ref, dcw_ref, dcb_ref, next_du):
        r = pl.program_id(1)
        first = r == ni - 1

        @pl.when(r == 0)
        def _():
            dcw_ref[...] = jnp.zeros_like(dcw_ref)
            dcb_ref[...] = jnp.zeros_like(dcb_ref)
            next_du[...] = jnp.zeros_like(next_du)

        x1, x2 = x1_ref[...], x2_ref[...]
        x1f, x2f = x1.astype(F32), x2.astype(F32)
        s11, s12 = _rows_before(x1, h1_ref, first, b1_ref, b2_ref)
        s21, s22 = _rows_before(x2, h2_ref, first, b1_ref, b2_ref)
        u1 = _conv_taps(x1f, s11, s12, cw1_ref, cb1_ref)
        u2 = _conv_taps(x2f, s21, s22, cw2_ref, cb2_ref)
        sg = _sigmoid(u1)
        daf = da_ref[...].astype(F32)
        du1 = daf * u2 * (sg * (1.0 + u1 * (1.0 - sg)))
        du2 = daf * (u1 * sg)
        rows = lax.broadcasted_iota(jnp.int32, (SUBLANES, tc), 0)

        for half, (du, own, own1, own2, cw_ref) in enumerate(((du1, x1f, s11, s12, cw1_ref),
                                                             (du2, x2f, s21, s22, cw2_ref))):
            du_b = du.astype(BF16)
            n1 = jnp.dot(a1_ref[...], du_b, preferred_element_type=F32)
            n2 = jnp.dot(a2_ref[...], du_b, preferred_element_type=F32)
            c8 = next_du[half]
            fix1 = jnp.where(rows >= SUBLANES - 1, pltpu.roll(c8, SUBLANES - 1, 0), 0.0)
            fix2 = jnp.where(rows >= SUBLANES - 2, pltpu.roll(c8, SUBLANES - 2, 0), 0.0)
            n1 = jnp.concatenate([n1[:tr - SUBLANES], n1[tr - SUBLANES:] + fix1], axis=0)
            n2 = jnp.concatenate([n2[:tr - SUBLANES], n2[tr - SUBLANES:] + fix2], axis=0)
            dup = cw_ref[2:3, :] * du + cw_ref[1:2, :] * n1 + cw_ref[0:1, :] * n2
            dup_ref[half] = dup.astype(dup_ref.dtype)
            dcb_ref[half] += jnp.sum(du, axis=0, keepdims=True)
            for tap, shifted in enumerate((own2, own1, own)):
                dcw_ref[half, tap:tap + 1, :] += jnp.sum(du * shifted, axis=0, keepdims=True)
            next_du[half] = du[:SUBLANES].astype(BF16).astype(F32)

    return _call(
        body,
        name=name,
        grid=(nc, ni),
        in_specs=[pl.BlockSpec((tr, tc), cur(0)), pl.BlockSpec((HALO, tc), before(0)),
                  pl.BlockSpec((tr, tc), cur(nc)), pl.BlockSpec((HALO, tc), before(nc)),
                  pl.BlockSpec((tr, tc), cur(0)),
                  vec(CONV_WIDTH, 0), vec(CONV_WIDTH, nc), vec(1, 0), vec(1, nc), mat, mat, mat, mat],
        out_specs=[pl.BlockSpec((2, tr, tc), lambda j, r: (0, ni - 1 - r, j)),
                   pl.BlockSpec((2, CONV_WIDTH, tc), lambda j, r: (0, 0, j)),
                   pl.BlockSpec((2, 1, tc), lambda j, r: (0, 0, j))],
        out_shape=[jax.ShapeDtypeStruct((2, S, F), BF16), jax.ShapeDtypeStruct((2, CONV_WIDTH, F), F32),
                   jax.ShapeDtypeStruct((2, 1, F), F32)],
        scratch_shapes=[pltpu.VMEM((2, SUBLANES, tc), F32)],
        sem=("parallel", "arbitrary"),
        args=(up, up, up, up, da, cw, cw, cb, cb, *back, *ahead),
        comm=comm,
    )


def _adam_math(w, g, m, v):
    m = ADAM_B1 * m + (1.0 - ADAM_B1) * g
    v = ADAM_B2 * v + (1.0 - ADAM_B2) * (g * g)
    m_hat = m / (1.0 - ADAM_B1 ** ADAM_STEP)
    v_hat = v / (1.0 - ADAM_B2 ** ADAM_STEP)
    delta = -ADAM_LR * (m_hat / (jnp.sqrt(v_hat) + ADAM_EPS) + ADAM_WD * w)
    return delta, m, v


def _adamw(w, m, v, parts, *, name):
    R, C = w.shape
    plist = list(parts) if isinstance(parts, (list, tuple)) else [parts]
    tr = _tile(min(p.shape[1] for p in plist), 256, 16)
    assert sum(p.shape[1] for p in plist) == R and all(p.shape[1] % tr == 0 for p in plist)
    row = pl.BlockSpec((tr, C), lambda i: (i, 0))
    first, spans = 0, []
    for p in plist:
        spans.append((first, first + p.shape[1] // tr))
        first = spans[-1][1]

    def body(w_ref, m_ref, v_ref, *rest):
        p_refs, (g_ref, d_ref, m2_ref, v2_ref) = rest[:len(plist)], rest[len(plist):]
        i = pl.program_id(0)

        def update(p_ref):
            g = p_ref[0].astype(F32)
            for k in range(1, N_DEV):
                g = g + p_ref[k].astype(F32)
            g_ref[...] = g
            d_ref[...], m2_ref[...], v2_ref[...] = _adam_math(w_ref[...], g, m_ref[...], v_ref[...])

        if len(plist) == 1:
            update(p_refs[0])
        else:
            for p_ref, (lo, hi) in zip(p_refs, spans):
                pl.when((i >= lo) & (i < hi))(functools.partial(update, p_ref))

    def part_spec(lo, hi):
        return pl.BlockSpec((N_DEV, tr, C), lambda i: (0, jnp.clip(i - lo, 0, hi - lo - 1), 0))

    return pl.pallas_call(
        body,
        name=name,
        grid=(R // tr,),
        in_specs=[row, row, row] + [part_spec(lo, hi) for lo, hi in spans],
        out_specs=[row] * 4,
        out_shape=[jax.ShapeDtypeStruct((R, C), F32)] * 4,
        compiler_params=_params("parallel"),
    )(w, m, v, *plist)


def _adamw_ada(w, m, v, cact_t, dmod_cols, *, name):
    R, C = w.shape
    B = cact_t.shape[1]
    tr = _tile(R, 256, 8)
    row = pl.BlockSpec((tr, C), lambda i: (i, 0))

    def body(w_ref, m_ref, v_ref, c_ref, d_ref, g_ref, dl_ref, m2_ref, v2_ref):
        g = c_ref[:, 0:1] * d_ref[0:1, :]
        for b in range(1, B):
            g = g + c_ref[:, b:b + 1] * d_ref[b:b + 1, :]
        g_ref[...] = g
        dl_ref[...], m2_ref[...], v2_ref[...] = _adam_math(w_ref[...], g, m_ref[...], v_ref[...])

    return pl.pallas_call(
        body,
        name=name,
        grid=(R // tr,),
        in_specs=[row, row, row, pl.BlockSpec((tr, B), lambda i: (i, 0)), pl.BlockSpec((B, C), lambda i: (0, 0))],
        out_specs=[row] * 4,
        out_shape=[jax.ShapeDtypeStruct((R, C), F32)] * 4,
        compiler_params=_params("parallel"),
    )(w, m, v, cact_t, dmod_cols)


def _z_layout(D, q_rank, kv_rank):
    kv = SWA_KVH * SWA_HD
    orig = {}
    o = 0
    for nm, w in (("cq", q_rank), ("ckv", kv_rank), ("kr", MLA_ROPE), ("qs", D), ("ks", kv), ("vs", kv),
                  ("ga", D), ("gb", D)):
        orig[nm] = (o, w)
        o += w
    blockw = {"cq": q_rank, "ckv": kv_rank, "kr": LANES, "qs": D, "ks": kv, "vs": kv, "ga": D, "gb": D}
    best = None
    for perm in itertools.permutations(("cq", "ckv", "ks", "vs", "kr")):
        off, new = 0, {}
        for nm in ("ga", "gb", "qs") + perm:
            off = _round_up(off, blockw[nm])
            new[nm] = off
            off += blockw[nm]
        if best is None or off < best[0]:
            best = (off, new)
    total = _round_up(best[0], 1024 if best[0] > 4096 else 512)
    return orig, best[1], blockw, total, o


def _permute_w_in(w, lay):
    orig, new, blockw, total, _ = lay
    parts, at = [], 0
    for nm in sorted(new, key=new.get):
        if new[nm] > at:
            parts.append(jnp.zeros((w.shape[0], new[nm] - at), w.dtype))
        o, wd = orig[nm]
        parts.append(w[:, o:o + wd])
        if blockw[nm] > wd:
            parts.append(jnp.zeros((w.shape[0], blockw[nm] - wd), w.dtype))
        at = new[nm] + blockw[nm]
    if total > at:
        parts.append(jnp.zeros((w.shape[0], total - at), w.dtype))
    return jnp.concatenate(parts, axis=1)


def _unpermute_w_in(wp, lay):
    orig, new, _, _, _ = lay
    return jnp.concatenate([wp[:, new[nm]:new[nm] + orig[nm][1]] for nm in sorted(orig, key=lambda n: orig[n][0])],
                           axis=1)


def _assemble_dz(parts, lay, S):
    _, new, blockw, total, _ = lay
    names = sorted(new, key=new.get)
    tr = _tile(S, 256, 16)

    def body(*refs):
        o_ref = refs[-1]
        cols, at = [], 0
        for nm, ref in zip(names, refs):
            if new[nm] > at:
                cols.append(jnp.zeros((tr, new[nm] - at), BF16))
            cols.append(ref[...])
            at = new[nm] + blockw[nm]
        if total > at:
            cols.append(jnp.zeros((tr, total - at), BF16))
        o_ref[...] = jnp.concatenate(cols, axis=1)

    return pl.pallas_call(
        body,
        name="assemble_dz",
        grid=(S // tr,),
        in_specs=[pl.BlockSpec((tr, blockw[nm]), lambda i: (i, 0)) for nm in names],
        out_specs=pl.BlockSpec((tr, total), lambda i: (i, 0)),
        out_shape=jax.ShapeDtypeStruct((S, total), BF16),
        compiler_params=_params("parallel"),
    )(*[parts[nm] for nm in names])


def _unshard_cols(g):
    return jnp.transpose(g, (1, 0, 2)).reshape(g.shape[1], N_DEV * g.shape[2])


def _shard_cols(w):
    K, N = w.shape
    return jnp.transpose(w.reshape(K, N_DEV, N // N_DEV), (1, 0, 2))


def _pack(vecs, rows):
    flat = jnp.concatenate([v.reshape(-1) for v in vecs])
    return jnp.pad(flat, (0, rows * LANES - flat.shape[0])).reshape(rows, LANES)


def kernel(x, c, w_ada, b_ada, g_pre_mix, g_post_mix, w_in, g_q_lat, w_uq, g_kv_lat, w_ukv, rel_bias, sinks, w_o, g_pre_ffn, g_post_ffn, w_up, conv_w, conv_b, w_down, loss_target, m_w_ada, m_b_ada, m_g_pre_mix, m_g_post_mix, m_w_in, m_g_q_lat, m_w_uq, m_g_kv_lat, m_w_ukv, m_rel_bias, m_sinks, m_w_o, m_g_pre_ffn, m_g_post_ffn, m_w_up, m_conv_w, m_conv_b, m_w_down, v_w_ada, v_b_ada, v_g_pre_mix, v_g_post_mix, v_w_in, v_g_q_lat, v_w_uq, v_g_kv_lat, v_w_ukv, v_rel_bias, v_sinks, v_w_o, v_g_pre_ffn, v_g_post_ffn, v_w_up, v_conv_w, v_conv_b, v_w_down):
    S, D = x.shape[1], x.shape[2]
    Q_RANK, KV_RANK = g_q_lat.shape[1], g_kv_lat.shape[1]
    H_MLA = D // MLA_V
    H_SWA = D // SWA_HD
    G_SWA = H_SWA // SWA_KVH
    F2 = w_up.shape[2] * N_DEV
    F = F2 // 2
    ada_n = w_ada.shape[2]
    me = 4 * lax.axis_index("x") + 2 * lax.axis_index("y") + lax.axis_index("c")
    lay = _z_layout(D, Q_RANK, KV_RANK)
    _, zoff, _, NZ, in_cols = lay
    assert in_cols == w_in.shape[2] * N_DEV

    x2, tgt = x[0], loss_target[0]

    cw_n = conv_w.shape[2]
    small = jnp.concatenate([jnp.pad(c, ((0, 7), (0, 0))), jnp.pad(conv_w[0], ((0, 8 - CONV_WIDTH), (0, 0)))], axis=1)
    small_all = _all_gather(small, name="ag_cond", in_vmem=True)
    c_all = small_all[:, 0, :D]
    cw_full = _unshard_cols(small_all[:, :CONV_WIDTH, D:])
    b_cols = lax.dynamic_slice_in_dim(b_ada, me * ada_n, ada_n, axis=1)
    c_act, mod_cols = _ada_fwd(c_all, w_ada[0], b_cols, name="ada_fwd")
    mod_all = _all_gather(mod_cols, name="ag_mod", in_vmem=True)
    mod_me = lax.dynamic_index_in_dim(mod_all, me, axis=1, keepdims=False).reshape(1, N_DEV * ada_n)
    sh1, sc1, gt1, sh2, sc2, gt2 = [mod_me[:, k * D:(k + 1) * D] for k in range(6)]

    h1, (in_g,) = _prenorm(x2, g_pre_mix, sc1, sh1, name="prenorm_mix", comm=[("gather", w_in[0].astype(BF16))])
    w_in_p = _permute_w_in(_unshard_cols(in_g), lay)
    z, (uq_g, ukv_g, o_g) = _matmul(h1, w_in_p, mode="nn", out_dtype=BF16, name="mm_in",
                                    comm=[("gather", w_uq[0].astype(BF16)), ("gather", w_ukv[0].astype(BF16)),
                                          ("gather", w_o[0].astype(BF16))])
    w_uq_p = jnp.pad(_unshard_cols(uq_g).reshape(Q_RANK, H_MLA, MLA_QK), ((0, 0), (0, 0), (0, MLA_QK_PAD - MLA_QK))
                     ).reshape(Q_RANK, H_MLA * MLA_QK_PAD)
    w_ukv_f = _unshard_cols(ukv_g)
    w_o_f = o_g.reshape(D, D)
    cqn = _prenorm(z, g_q_lat, None, None, name="norm_cq", off=zoff["cq"], width=Q_RANK)
    ckvn = _prenorm(z, g_kv_lat, None, None, name="norm_ckv", off=zoff["ckv"], width=KV_RANK)
    q_raw = _matmul(cqn, w_uq_p, mode="nn", out_dtype=BF16, name="mm_uq")
    kv = _matmul(ckvn, w_ukv_f, mode="nn", out_dtype=BF16, name="mm_ukv")
    tab_k = _rope_tables(S, LANES, 0)
    krr = _rope(z, tab_k, heads=1, width=LANES, transpose=False, name="rope_k", off=zoff["kr"])
    (o_a, lse, Qr), (up_g,) = _flash_fwd(q_raw, kv, krr, tab_k, heads=H_MLA, name="mla_fwd",
                                        comm=[("gather", w_up[0].astype(BF16))])
    w_up_f = _unshard_cols(up_g)

    bucket, valid = _t5_bucket_table()
    onehot = (jnp.asarray(bucket).reshape(-1, 1) == jnp.arange(LANES)[None, :]).astype(F32)
    rb_pad = jnp.pad(rel_bias, ((0, LANES - REL_BUCKETS), (0, LANES - H_SWA)))
    bias_t = _matmul(onehot, rb_pad, mode="nn", out_dtype=F32, name="bias_table", tm=2048, precision=HIGHEST)
    bias_full = jnp.transpose(bias_t[:, :H_SWA].reshape(BLOCK, 2 * BLOCK, H_SWA), (2, 0, 1))
    bias_full = jnp.where(jnp.asarray(valid)[None], bias_full, NEG)
    bias_full = jnp.transpose(bias_full.reshape(SWA_KVH, G_SWA, BLOCK, 2 * BLOCK), (0, 3, 1, 2)
                              ).reshape(SWA_KVH, 2 * BLOCK, G_SWA * BLOCK)
    sink_rows = jnp.broadcast_to(sinks.reshape(SWA_KVH, G_SWA, 1), (SWA_KVH, G_SWA, BLOCK)
                                 ).reshape(SWA_KVH, 1, G_SWA * BLOCK)
    kvw = SWA_KVH * SWA_HD

    def heads_first(t, n):
        return jnp.transpose(t.reshape(S, n, SWA_HD), (1, 0, 2))

    def heads_last(t):
        return jnp.transpose(t, (1, 0, 2)).reshape(S, t.shape[0] * SWA_HD)


    ks_h = heads_first(z[:, zoff["ks"]:zoff["ks"] + kvw], SWA_KVH)
    vs_h = heads_first(z[:, zoff["vs"]:zoff["vs"] + kvw], SWA_KVH)
    o_b, _ = _swa_fwd(z, zoff["qs"], ks_h, vs_h, bias_full, sink_rows, name="swa_fwd")

    mixin = _gate_mix(z, o_a, o_b, D=D, off_a=zoff["ga"], off_b=zoff["gb"], name="gate_mix")
    mix = _matmul(mixin, w_o_f, mode="nn", out_dtype=F32, name="mm_o")
    x1 = _postnorm_res(x2, mix, gt1, g_post_mix, name="postnorm_mix")

    h2 = _prenorm(x1, g_pre_ffn, sc2, sh2, name="prenorm_ffn")
    up, (down_g,) = _matmul(h2, w_up_f, mode="nn", out_dtype=BF16, name="mm_up",
                            comm=[("gather", w_down[0].astype(BF16))])
    w_down_f = down_g.reshape(F, D)
    act = _conv_gate(up, cw_full, conv_b, name="conv_gate")
    y = _matmul(act, w_down_f, mode="nn", out_dtype=F32, name="mm_down")
    loss_part, dout, dy, dgt2, dg_post_ffn = _final_loss(x1, y, tgt, gt2, g_post_ffn, name="final_loss")
    loss = lax.psum(loss_part[0, 0], ("x", "y", "c"))

    dw_down = _matmul(act, dy, mode="tn", out_dtype=BF16, name="mm_down_dw")
    dact = _matmul(dy, w_down_f, mode="nt", out_dtype=BF16, name="mm_down_dx")
    (dup, dcw, dcb), (got_down,) = _conv_gate_bwd(up, dact, cw_full, conv_b, name="conv_gate_bwd",
                                                  comm=[("scatter", dw_down.reshape(N_DEV, F // N_DEV, D))])
    dcw = jnp.transpose(dcw, (1, 0, 2)).reshape(CONV_WIDTH, F2)
    dcb = dcb.reshape(1, F2)
    dw_up = _matmul(h2, dup, mode="tn", out_dtype=BF16, name="mm_up_dw", shard_out=True, halves=True)
    dh2 = _matmul(dup, w_up_f, mode="nt", out_dtype=F32, name="mm_up_dx", halves=True)
    dx1, dg_pre_ffn, dsc2, dsh2 = _prenorm_bwd(x1, dh2, dout, g_pre_ffn, sc2, name="prenorm_ffn_bwd", out_dtype=F32)

    dmix, dgt1, dg_post_mix = _postnorm_bwd(dx1, mix, gt1, g_post_mix, name="postnorm_mix_bwd")
    dw_o = _matmul(mixin, dmix, mode="tn", out_dtype=BF16, name="mm_o_dw")
    dmixin = _matmul(dmix, w_o_f, mode="nt", out_dtype=BF16, name="mm_o_dx")
    dga, dgb, do_a, do_b = _gate_mix_bwd(dmixin, z, o_a, o_b, D=D, off_a=zoff["ga"], off_b=zoff["gb"],
                                         name="gate_mix_bwd")
    (dq_raw, dkv, dkr_parts), (got_up,) = _flash_bwd(Qr, kv, krr, do_a, o_a, lse, tab_k, heads=H_MLA, name="mla_bwd",
                                                     comm=[("scatter", dw_up)])
    dkr = _shared_rope_grad(dkr_parts, tab_k, name="rope_k_bwd")
    dw_uq_p = _matmul(cqn, dq_raw, mode="tn", out_dtype=BF16, name="mm_uq_dw")
    dcqn = _matmul(dq_raw, w_uq_p, mode="nt", out_dtype=F32, name="mm_uq_dx")
    dw_ukv = _matmul(ckvn, dkv, mode="tn", out_dtype=BF16, name="mm_ukv_dw", shard_out=True)
    dckvn = _matmul(dkv, w_ukv_f, mode="nt", out_dtype=F32, name="mm_ukv_dx")
    dw_uq = dw_uq_p.reshape(Q_RANK, H_MLA, MLA_QK_PAD)[:, :, :MLA_QK].reshape(Q_RANK, H_MLA * MLA_QK)

    dcw_parts = jnp.pad(_shard_cols(dcw), ((0, 0), (0, 16 - CONV_WIDTH), (0, 0)))
    (dqs, dks_h, dvs_h, dbias, dsink), (got_o, got_cw, got_uq, got_ukv) = _swa_bwd(
        z, zoff["qs"], ks_h, vs_h, bias_full, sink_rows, do_b, name="swa_bwd",
        comm=[("scatter", dw_o.reshape(N_DEV, D // N_DEV, D)), ("scatter", dcw_parts),
              ("scatter", _shard_cols(dw_uq)), ("scatter", dw_ukv)])
    dbias = jnp.transpose(dbias.reshape(SWA_KVH, 2 * BLOCK, G_SWA, BLOCK), (0, 2, 3, 1))
    drel_t = _matmul(dbias.reshape(H_SWA, BLOCK * 2 * BLOCK), onehot, mode="nn", out_dtype=F32, name="bias_grad",
                     tk=4096, precision=HIGHEST)
    d_rel_bias = jnp.transpose(drel_t[:, :REL_BUCKETS])
    d_sinks = jnp.sum(dsink.reshape(SWA_KVH, G_SWA, BLOCK), axis=-1).reshape(1, H_SWA)

    dcq, dg_q = _prenorm_bwd(z, dcqn, None, g_q_lat, None, name="norm_cq_bwd", out_dtype=BF16,
                             off=zoff["cq"], width=Q_RANK)
    dckv, dg_kv = _prenorm_bwd(z, dckvn, None, g_kv_lat, None, name="norm_ckv_bwd", out_dtype=BF16,
                               off=zoff["ckv"], width=KV_RANK)
    dz = _assemble_dz({"ga": dga, "gb": dgb, "qs": dqs, "cq": dcq, "ckv": dckv,
                       "ks": heads_last(dks_h), "vs": heads_last(dvs_h), "kr": dkr}, lay, S)
    dw_in_a = _matmul(h1, dz, mode="tn", out_dtype=BF16, name="mm_in_dw_a", m_range=(0, D // 2))
    dw_in_b, (got_in_a,) = _matmul(h1, dz, mode="tn", out_dtype=BF16, name="mm_in_dw_b", m_range=(D // 2, D // 2),
                                   comm=[("scatter", _shard_cols(_unpermute_w_in(dw_in_a, lay)))])
    dh1, (got_in_b,) = _matmul(dz, w_in_p, mode="nt", out_dtype=F32, name="mm_in_dx",
                               comm=[("scatter", _shard_cols(_unpermute_w_in(dw_in_b, lay)))])
    grad_x, dg_pre_mix, dsc1, dsh1 = _prenorm_bwd(x2, dh1, dx1, g_pre_mix, sc1, name="prenorm_mix_bwd",
                                                  out_dtype=F32)
    dmod = jnp.concatenate([dsh1, dsc1, dgt1, dsh2, dsc2, dgt2], axis=1)

    small_names = ["b_ada", "g_pre_mix", "g_post_mix", "g_q_lat", "g_kv_lat", "rel_bias", "sinks", "g_pre_ffn",
                   "g_post_ffn", "conv_b"]
    small_w = [b_ada, g_pre_mix, g_post_mix, g_q_lat, g_kv_lat, rel_bias, sinks, g_pre_ffn, g_post_ffn, conv_b]
    small_m = [m_b_ada, m_g_pre_mix, m_g_post_mix, m_g_q_lat, m_g_kv_lat, m_rel_bias, m_sinks, m_g_pre_ffn,
               m_g_post_ffn, m_conv_b]
    small_v = [v_b_ada, v_g_pre_mix, v_g_post_mix, v_g_q_lat, v_g_kv_lat, v_rel_bias, v_sinks, v_g_pre_ffn,
               v_g_post_ffn, v_conv_b]
    small_g = [dmod, dg_pre_mix, dg_post_mix, dg_q, dg_kv, d_rel_bias, d_sinks, dg_pre_ffn, dg_post_ffn, dcb]
    n_small = sum(int(np.prod(w.shape)) for w in small_w)
    rows = _round_up(-(-n_small // LANES), 16)
    parts_small = _all_gather(_pack(small_g, rows), name="ag_small_grads", in_vmem=True)
    sg, sd, sm, sv = _adamw(_pack(small_w, rows), _pack(small_m, rows), _pack(small_v, rows), parts_small,
                            name="adamw_small")

    def unpack(packed):
        flat, out, at = packed.reshape(-1), {}, 0
        for nm, w in zip(small_names, small_w):
            n = int(np.prod(w.shape))
            out[nm] = flat[at:at + n].reshape(w.shape)
            at += n
        return out

    small_out = [unpack(t) for t in (sg, sd, sm, sv)]

    dmod_all = parts_small.reshape(N_DEV, rows * LANES)[:, :6 * D]
    dmod_cols = lax.dynamic_slice_in_dim(dmod_all, me * ada_n, ada_n, axis=1)
    ada_out = _adamw_ada(w_ada[0], m_w_ada[0], v_w_ada[0], jnp.transpose(c_act), dmod_cols, name="adamw_w_ada")

    def owner_update(got, w, m, v, name):
        shp = w.shape
        w2, m2, v2 = (t.reshape(shp[-2], shp[-1]) for t in (w, m, v))
        return [t.reshape(shp) for t in _adamw(w2, m2, v2, got, name="adamw_" + name)]

    def pad_rows(t):
        return jnp.pad(t[0], ((0, 16 - CONV_WIDTH), (0, 0)))

    big = {
        "w_in": owner_update([got_in_a, got_in_b], w_in, m_w_in, v_w_in, "w_in"),
        "w_uq": owner_update(got_uq, w_uq, m_w_uq, v_w_uq, "w_uq"),
        "w_ukv": owner_update(got_ukv, w_ukv, m_w_ukv, v_w_ukv, "w_ukv"),
        "w_o": owner_update(got_o, w_o, m_w_o, v_w_o, "w_o"),
        "w_up": owner_update(got_up, w_up, m_w_up, v_w_up, "w_up"),
        "w_down": owner_update(got_down, w_down, m_w_down, v_w_down, "w_down"),
    }
    cw_upd = _adamw(pad_rows(conv_w), pad_rows(m_conv_w), pad_rows(v_conv_w), got_cw, name="adamw_conv_w")
    big["conv_w"] = [t[:CONV_WIDTH].reshape(conv_w.shape) for t in cw_upd]
    big["w_ada"] = [t.reshape(w_ada.shape) for t in ada_out]

    order = ["w_ada", "b_ada", "g_pre_mix", "g_post_mix", "w_in", "g_q_lat", "w_uq", "g_kv_lat", "w_ukv", "rel_bias",
             "sinks", "w_o", "g_pre_ffn", "g_post_ffn", "w_up", "conv_w", "conv_b", "w_down"]
    outs = [loss, grad_x.reshape(x.shape)]
    for kind in range(4):
        for nm in order:
            outs.append(big[nm][kind] if nm in big else small_out[kind][nm])
    return tuple(outs)
```

```python
import functools
import itertools
import math

import numpy as np

import jax
import jax.numpy as jnp
from jax import lax
from jax.experimental import pallas as pl
from jax.experimental.pallas import tpu as pltpu

F32 = jnp.float32
BF16 = jnp.bfloat16

N_DEV = 8
MLA_NOPE = 128
MLA_ROPE = 64
MLA_V = 128
MLA_QK = MLA_NOPE + MLA_ROPE
MLA_QK_PAD = 256
ROPE_HALF = MLA_ROPE // 2
ROPE_THETA = 10000.0
SWA_HD = 64
SWA_KVH = 4
WINDOW = 128
BLOCK = 128
REL_BUCKETS = 32
REL_MAX_DIST = 128
CONV_WIDTH = 3
EPS = 1e-6
NEG = -1e30
ADAM_LR = 0.001
ADAM_B1 = 0.9
ADAM_B2 = 0.999
ADAM_EPS = 1e-08
ADAM_WD = 0.01
ADAM_STEP = 10
LANES = 128
HALO = 16
MESH = pl.DeviceIdType.MESH
HIGHEST = lax.Precision.HIGHEST

NN = (((1,), (0,)), ((), ()))
NT = (((1,), (1,)), ((), ()))
TN = (((0,), (0,)), ((), ()))


def _tile(n, pref, align=LANES):
    if n <= pref:
        return n
    t = (pref // align) * align
    while t >= align:
        if n % t == 0:
            return t
        t -= align
    return n


def _round_up(n, m):
    return (n + m - 1) // m * m


def _params(*sem):
    return pltpu.CompilerParams(dimension_semantics=sem)


def _sigmoid(x):
    return 1.0 / (1.0 + jnp.exp(-x))


def _my_place():
    return lax.axis_index("x"), lax.axis_index("y"), lax.axis_index("c")


def _all_gather(x, *, name, in_vmem):
    space = pltpu.VMEM if in_vmem else pl.ANY

    def body(x_ref, out_ref, send_sems, recv_sems, local_sem):
        x_, y_, c_ = _my_place()
        me, sibling = (x_, y_, c_), (x_, y_, 1 - c_)
        chips = [(1 - x_, y_), (x_, 1 - y_), (1 - x_, 1 - y_)]

        def slot(px, py, pc):
            return out_ref.at[4 * px + 2 * py + pc]

        def copy(k, block, to, src=None):
            return pltpu.make_async_remote_copy(
                src_ref=slot(*block) if src is None else src,
                dst_ref=slot(*block),
                send_sem=send_sems.at[k],
                recv_sem=recv_sems.at[k],
                device_id=to,
                device_id_type=MESH,
            )

        mine = pltpu.make_async_copy(x_ref, slot(*me), local_sem)
        mine.start()
        first = [copy(0, me, sibling, src=x_ref)]
        first += [copy(1 + j, me, (*chip, c_), src=x_ref) for j, chip in enumerate(chips)]
        for cp in first:
            cp.start()
        passed = [copy(4 + j, (*chip, c_), sibling) for j, chip in enumerate(chips)]
        for j, chip in enumerate(chips):
            copy(1 + j, (*chip, c_), me).wait_recv()
            passed[j].start()
        copy(0, sibling, me).wait_recv()
        for j, chip in enumerate(chips):
            copy(4 + j, (*chip, 1 - c_), me).wait_recv()
        for cp in first + passed:
            cp.wait_send()
        mine.wait()

    return pl.pallas_call(
        body,
        name=name,
        out_shape=jax.ShapeDtypeStruct((N_DEV,) + x.shape, x.dtype),
        in_specs=[pl.BlockSpec(memory_space=space)],
        out_specs=pl.BlockSpec(memory_space=space),
        scratch_shapes=[
            pltpu.SemaphoreType.DMA((7,)),
            pltpu.SemaphoreType.DMA((7,)),
            pltpu.SemaphoreType.DMA,
        ],
    )(x)


class _Exchange:
    def __init__(self, kind, x_ref, out_ref, send_sems, recv_sems, local_sems, t):
        x_, y_, c_ = _my_place()
        me = 4 * x_ + 2 * y_ + c_

        def pair(k, src, dst, to):
            return pltpu.make_async_remote_copy(src_ref=src, dst_ref=dst, send_sem=send_sems.at[7 * t + k],
                                                recv_sem=recv_sems.at[7 * t + k], device_id=to, device_id_type=MESH)

        none = lambda: []
        if kind == "scatter":
            peers = [(x_ ^ ((r >> 2) & 1), y_ ^ ((r >> 1) & 1), c_ ^ (r & 1)) for r in range(1, N_DEV)]
            self.at_start = lambda: [pair(k, x_ref.at[4 * px + 2 * py + pc], out_ref.at[me], (px, py, pc))
                                     for k, (px, py, pc) in enumerate(peers)]
            self.relay_after, self.at_relay = none, none
            self.arrivals = self.at_start
            self.own = lambda: pltpu.make_async_copy(x_ref.at[me], out_ref.at[me], local_sems.at[t])
        else:
            sibling = (x_, y_, 1 - c_)
            chips = list(enumerate([(1 - x_, y_), (x_, 1 - y_), (1 - x_, 1 - y_)]))

            def slot(px, py, pc):
                return out_ref.at[4 * px + 2 * py + pc]

            mine = slot(x_, y_, c_)
            self.at_start = lambda: ([pair(0, x_ref, mine, sibling)]
                                     + [pair(1 + j, x_ref, mine, (*chip, c_)) for j, chip in chips])
            self.relay_after = lambda: [pair(1 + j, slot(*chip, c_), slot(*chip, c_), (*chip, c_)) for j, chip in chips]
            self.at_relay = lambda: [pair(4 + j, slot(*chip, c_), slot(*chip, c_), sibling) for j, chip in chips]
            self.arrivals = lambda: ([pair(0, slot(*sibling), slot(*sibling), sibling)]
                                     + [pair(4 + j, slot(*chip, 1 - c_), slot(*chip, 1 - c_), sibling)
                                        for j, chip in chips])
            self.own = lambda: pltpu.make_async_copy(x_ref, mine, local_sems.at[t])

    def start(self):
        self.own().start()
        for cp in self.at_start():
            cp.start()

    def relay(self):
        for landed, onward in zip(self.relay_after(), self.at_relay()):
            landed.wait_recv()
            onward.start()

    def finish(self):
        for cp in self.arrivals():
            cp.wait_recv()
        for cp in self.at_start() + self.at_relay():
            cp.wait_send()
        self.own().wait()


RELAY_AT = 0.85


def _call(body, *, name, grid, in_specs, out_specs, out_shape, args, scratch_shapes=(), sem=(), comm=(), prefetch=()):
    n_pf = len(prefetch)

    def launch(fn, ins, outs, shapes, scratch, semantics, operands):
        spec = pltpu.PrefetchScalarGridSpec(num_scalar_prefetch=n_pf, grid=grid, in_specs=ins, out_specs=outs,
                                            scratch_shapes=scratch)
        return pl.pallas_call(fn, name=name, grid_spec=spec, out_shape=shapes,
                              compiler_params=_params(*semantics))(*prefetch, *operands)

    if not comm:
        return list(launch(body, list(in_specs), list(out_specs), list(out_shape), list(scratch_shapes), sem, args)), []
    n_in, n_out, n_c, n_s = len(in_specs), len(out_specs), len(comm), len(scratch_shapes)
    kinds = [kind for kind, _ in comm]
    hbm = pl.BlockSpec(memory_space=pl.ANY)

    def wrapped(*refs):
        tables, refs = refs[:n_pf], refs[n_pf:]
        ins, cin = refs[:n_in], refs[n_in:n_in + n_c]
        at = n_in + n_c
        outs, cout = refs[at:at + n_out], refs[at + n_out:at + n_out + n_c]
        scr = refs[at + n_out + n_c:at + n_out + n_c + n_s]
        send, recv, local = refs[-3:]
        step = 0
        for a, g in enumerate(grid):
            step = step * g + pl.program_id(a)
        n_steps = int(np.prod(grid))

        def exchanges():
            return [_Exchange(kinds[t], cin[t], cout[t], send, recv, local, t) for t in range(n_c)]

        @pl.when(step == 0)
        def _():
            for ex in exchanges():
                ex.start()

        body(*tables, *ins, *outs, *scr)

        @pl.when(step == min(int(RELAY_AT * n_steps), n_steps - 1))
        def _():
            for ex in exchanges():
                ex.relay()

        @pl.when(step == n_steps - 1)
        def _():
            for ex in exchanges():
                ex.finish()

    c_shapes = [jax.ShapeDtypeStruct(((N_DEV,) + a.shape) if kind == "gather" else a.shape, a.dtype)
                for kind, a in comm]
    sems = [pltpu.SemaphoreType.DMA((7 * n_c,)), pltpu.SemaphoreType.DMA((7 * n_c,)), pltpu.SemaphoreType.DMA((n_c,))]
    res = launch(wrapped, list(in_specs) + [hbm] * n_c, list(out_specs) + [hbm] * n_c, list(out_shape) + c_shapes,
                 list(scratch_shapes) + sems, ["arbitrary"] * len(grid), (*args, *[a for _, a in comm]))
    return list(res[:n_out]), list(res[n_out:])


def _matmul(a, b, *, mode, out_dtype, name, tm=1024, tn=1024, tk=2816, precision=None, comm=(), shard_out=False,
            halves=False, m_range=None):
    if mode == "nn":
        (M, K), (K2, N) = a.shape, b.shape
    elif mode == "nt":
        (M, K), (N, K2) = (a.shape[1], 2 * a.shape[2]) if halves else a.shape, b.shape
    else:
        (K, M), (K2, N) = a.shape, (b.shape[1], 2 * b.shape[2]) if halves else b.shape
    assert K == K2, (a.shape, b.shape, mode)
    m_off = 0
    if m_range is not None:
        m_off, M = m_range
    tm = _tile(M, tm, LANES if mode == "tn" else 16)
    tk = _tile(K // 2 if halves and mode == "nt" else K, tk)
    tn = _tile(N // N_DEV, max(tn, 1408)) if shard_out else _tile(N // 2 if halves and mode == "tn" else N, tn)
    nk = K // tk
    m_off //= tm
    if mode == "tn":
        a_spec = pl.BlockSpec((tk, tm), lambda i, j, k: (k, i + m_off))
    elif halves:
        a_spec = pl.BlockSpec((None, tm, tk), lambda i, j, k: (k // (nk // 2), i, k % (nk // 2)))
    else:
        a_spec = pl.BlockSpec((tm, tk), lambda i, j, k: (i, k))
    if mode == "nt":
        b_spec = pl.BlockSpec((tn, tk), lambda i, j, k: (j, k))
    elif halves:
        nj = N // tn
        b_spec = pl.BlockSpec((None, tk, tn), lambda i, j, k: (j // (nj // 2), k, j % (nj // 2)))
    else:
        b_spec = pl.BlockSpec((tk, tn), lambda i, j, k: (k, j))
    dn = {"nn": NN, "nt": NT, "tn": TN}[mode]
    if shard_out:
        per = N // N_DEV // tn
        o_spec = pl.BlockSpec((None, tm, tn), lambda i, j, k: (j // per, i, j % per))
        o_shape = jax.ShapeDtypeStruct((N_DEV, M, N // N_DEV), out_dtype)
    else:
        o_spec = pl.BlockSpec((tm, tn), lambda i, j, k: (i, j))
        o_shape = jax.ShapeDtypeStruct((M, N), out_dtype)

    def product(a_ref, b_ref):
        return lax.dot_general(a_ref[...], b_ref[...], dn, preferred_element_type=F32, precision=precision)

    def body_one(a_ref, b_ref, o_ref):
        o_ref[...] = product(a_ref, b_ref).astype(o_ref.dtype)

    def body_acc(a_ref, b_ref, o_ref, acc_ref):
        k = pl.program_id(2)

        @pl.when(k == 0)
        def _():
            acc_ref[...] = product(a_ref, b_ref)

        @pl.when(k > 0)
        def _():
            acc_ref[...] += product(a_ref, b_ref)

        @pl.when(k == nk - 1)
        def _():
            o_ref[...] = acc_ref[...].astype(o_ref.dtype)

    outs, moved = _call(
        body_one if nk == 1 else body_acc,
        name=name,
        grid=(M // tm, N // tn, nk),
        in_specs=[a_spec, b_spec],
        out_specs=[o_spec],
        out_shape=[o_shape],
        scratch_shapes=[] if nk == 1 else [pltpu.VMEM((tm, tn), F32)],
        sem=("parallel", "parallel", "arbitrary"),
        args=(a, b),
        comm=comm,
    )
    return (outs[0], moved) if comm else outs[0]


def _rstd(xf):
    return lax.rsqrt(jnp.mean(xf * xf, axis=-1, keepdims=True) + EPS)


def _col_view(width, off):
    assert off % width == 0
    return off // width


def _prenorm(x, g, sc, sh, *, name, off=0, width=None, comm=()):
    S = x.shape[0]
    W = x.shape[1] if width is None else width
    cb = _col_view(W, off)
    tr = _tile(S, 512, 16)
    mod = sc is not None
    vec = pl.BlockSpec((1, W), lambda i: (0, 0))

    def body(*refs):
        if mod:
            x_ref, g_ref, sc_ref, sh_ref, o_ref = refs
        else:
            x_ref, g_ref, o_ref = refs
        xf = x_ref[...].astype(F32)
        y = xf * _rstd(xf) * g_ref[...]
        if mod:
            y = y * (1.0 + sc_ref[...]) + sh_ref[...]
        o_ref[...] = y.astype(o_ref.dtype)

    args = (x, g, sc, sh) if mod else (x, g)
    outs, moved = _call(
        body,
        name=name,
        grid=(S // tr,),
        in_specs=[pl.BlockSpec((tr, W), lambda i: (i, cb))] + [vec] * (len(args) - 1),
        out_specs=[pl.BlockSpec((tr, W), lambda i: (i, 0))],
        out_shape=[jax.ShapeDtypeStruct((S, W), BF16)],
        sem=("parallel",),
        args=args,
        comm=comm,
    )
    return (outs[0], moved) if comm else outs[0]


def _prenorm_bwd(x, dh, dres, g, sc, *, name, out_dtype, off=0, width=None, comm=()):
    S = x.shape[0]
    W = x.shape[1] if width is None else width
    cb = _col_view(W, off)
    tr = _tile(S, 256, 16)
    mod = sc is not None
    res = dres is not None
    vec = pl.BlockSpec((1, W), lambda i: (0, 0))
    row = pl.BlockSpec((tr, W), lambda i: (i, 0))

    def body(*refs):
        it = iter(refs)
        x_ref, dh_ref = next(it), next(it)
        dres_ref = next(it) if res else None
        g_ref = next(it)
        sc_ref = next(it) if mod else None
        dx_ref, dg_ref = next(it), next(it)
        dsc_ref, dsh_ref = (next(it), next(it)) if mod else (None, None)
        i = pl.program_id(0)

        @pl.when(i == 0)
        def _():
            dg_ref[...] = jnp.zeros_like(dg_ref)
            if mod:
                dsc_ref[...] = jnp.zeros_like(dsc_ref)
                dsh_ref[...] = jnp.zeros_like(dsh_ref)

        xf = x_ref[...].astype(F32)
        r = _rstd(xf)
        xn = xf * r
        dhf = dh_ref[...].astype(F32)
        gv = g_ref[...]
        if mod:
            one_sc = 1.0 + sc_ref[...]
            dsh_ref[...] += jnp.sum(dhf, axis=0, keepdims=True)
            dsc_ref[...] += jnp.sum(dhf * (xn * gv), axis=0, keepdims=True)
            dg_ref[...] += jnp.sum(dhf * xn * one_sc, axis=0, keepdims=True)
            dxn = dhf * (gv * one_sc)
        else:
            dg_ref[...] += jnp.sum(dhf * xn, axis=0, keepdims=True)
            dxn = dhf * gv
        dx = r * (dxn - xn * jnp.mean(dxn * xn, axis=-1, keepdims=True))
        if res:
            dx = dx + dres_ref[...]
        dx_ref[...] = dx.astype(dx_ref.dtype)

    args = [x, dh] + ([dres] if res else []) + [g] + ([sc] if mod else [])
    in_specs = [pl.BlockSpec((tr, W), lambda i: (i, cb)), row] + ([row] if res else []) + [vec] + ([vec] if mod else [])
    n_vec = 3 if mod else 1
    outs, moved = _call(
        body,
        name=name,
        grid=(S // tr,),
        in_specs=in_specs,
        out_specs=[row] + [vec] * n_vec,
        out_shape=[jax.ShapeDtypeStruct((S, W), out_dtype)] + [jax.ShapeDtypeStruct((1, W), F32)] * n_vec,
        sem=("arbitrary",),
        args=args,
        comm=comm,
    )
    return (outs, moved) if comm else outs


def _postnorm_res(x, y, gt, g, *, name):
    S, D = x.shape
    tr = _tile(S, 512, 8)
    row = pl.BlockSpec((tr, D), lambda i: (i, 0))
    vec = pl.BlockSpec((1, D), lambda i: (0, 0))

    def body(x_ref, y_ref, gt_ref, g_ref, o_ref):
        yf = y_ref[...]
        o_ref[...] = x_ref[...] + gt_ref[...] * (yf * _rstd(yf) * g_ref[...])

    return pl.pallas_call(
        body,
        name=name,
        grid=(S // tr,),
        in_specs=[row, row, vec, vec],
        out_specs=row,
        out_shape=jax.ShapeDtypeStruct((S, D), F32),
        compiler_params=_params("parallel"),
    )(x, y, gt, g)


def _postnorm_bwd(dx1, y, gt, g, *, name):
    S, D = y.shape
    tr = _tile(S, 256, 16)
    row = pl.BlockSpec((tr, D), lambda i: (i, 0))
    vec = pl.BlockSpec((1, D), lambda i: (0, 0))

    def body(dx_ref, y_ref, gt_ref, g_ref, dy_ref, dgt_ref, dg_ref):
        @pl.when(pl.program_id(0) == 0)
        def _():
            dgt_ref[...] = jnp.zeros_like(dgt_ref)
            dg_ref[...] = jnp.zeros_like(dg_ref)

        yf = y_ref[...]
        r = _rstd(yf)
        yn = yf * r
        d = dx_ref[...]
        gtv, gv = gt_ref[...], g_ref[...]
        dgt_ref[...] += jnp.sum(d * (yn * gv), axis=0, keepdims=True)
        dg_ref[...] += jnp.sum(d * gtv * yn, axis=0, keepdims=True)
        dyn = d * (gtv * gv)
        dy_ref[...] = (r * (dyn - yn * jnp.mean(dyn * yn, axis=-1, keepdims=True))).astype(dy_ref.dtype)

    return pl.pallas_call(
        body,
        name=name,
        grid=(S // tr,),
        in_specs=[row, row, vec, vec],
        out_specs=[row, vec, vec],
        out_shape=[jax.ShapeDtypeStruct((S, D), BF16), jax.ShapeDtypeStruct((1, D), F32),
                   jax.ShapeDtypeStruct((1, D), F32)],
        compiler_params=_params("arbitrary"),
    )(dx1, y, gt, g)


def _final_loss(x1, y, target, gt, g, *, name):
    S, D = y.shape
    tr = _tile(S, 256, 16)
    row = pl.BlockSpec((tr, D), lambda i: (i, 0))
    vec = pl.BlockSpec((1, D), lambda i: (0, 0))
    one = pl.BlockSpec((1, LANES), lambda i: (0, 0))

    def body(x_ref, y_ref, t_ref, gt_ref, g_ref, loss_ref, dout_ref, dy_ref, dgt_ref, dg_ref):
        @pl.when(pl.program_id(0) == 0)
        def _():
            loss_ref[...] = jnp.zeros_like(loss_ref)
            dgt_ref[...] = jnp.zeros_like(dgt_ref)
            dg_ref[...] = jnp.zeros_like(dg_ref)

        yf = y_ref[...]
        r = _rstd(yf)
        yn = yf * r
        gtv, gv = gt_ref[...], g_ref[...]
        out = x_ref[...] + gtv * (yn * gv)
        diff = out - t_ref[...]
        per_tok = jnp.mean(diff * diff, axis=-1, keepdims=True)
        loss_ref[...] += 0.5 * jnp.sum(per_tok, axis=0, keepdims=True)
        d = diff / D
        dout_ref[...] = d
        dgt_ref[...] += jnp.sum(d * (yn * gv), axis=0, keepdims=True)
        dg_ref[...] += jnp.sum(d * gtv * yn, axis=0, keepdims=True)
        dyn = d * (gtv * gv)
        dy_ref[...] = (r * (dyn - yn * jnp.mean(dyn * yn, axis=-1, keepdims=True))).astype(dy_ref.dtype)

    return pl.pallas_call(
        body,
        name=name,
        grid=(S // tr,),
        in_specs=[row, row, row, vec, vec],
        out_specs=[one, row, row, vec, vec],
        out_shape=[jax.ShapeDtypeStruct((1, LANES), F32), jax.ShapeDtypeStruct((S, D), F32),
                   jax.ShapeDtypeStruct((S, D), BF16), jax.ShapeDtypeStruct((1, D), F32),
                   jax.ShapeDtypeStruct((1, D), F32)],
        compiler_params=_params("arbitrary"),
    )(x1, y, target, gt, g)


def _ada_fwd(c_all, w_local, b_cols, *, name):
    B, D = c_all.shape
    N = w_local.shape[1]
    tn = _tile(N, 512)

    def body(c_ref, w_ref, b_ref, ca_ref, mod_ref):
        cv = c_ref[...]
        ca = cv * _sigmoid(cv)
        ca_ref[...] = ca
        mod_ref[...] = jnp.dot(ca, w_ref[...], preferred_element_type=F32, precision=HIGHEST) + b_ref[...]

    return pl.pallas_call(
        body,
        name=name,
        grid=(N // tn,),
        in_specs=[pl.BlockSpec((B, D), lambda j: (0, 0)), pl.BlockSpec((D, tn), lambda j: (0, j)),
                  pl.BlockSpec((1, tn), lambda j: (0, j))],
        out_specs=[pl.BlockSpec((B, D), lambda j: (0, 0)), pl.BlockSpec((B, tn), lambda j: (0, j))],
        out_shape=[jax.ShapeDtypeStruct((B, D), F32), jax.ShapeDtypeStruct((B, N), F32)],
        compiler_params=_params("arbitrary"),
    )(c_all, w_local, b_cols)


def _rope_tables(S, width, lane_off):
    pos = jnp.arange(S, dtype=F32)
    inv = ROPE_THETA ** (-jnp.arange(0, MLA_ROPE, 2, dtype=F32) / MLA_ROPE)
    ang = pos[:, None] * inv[None, :]
    ang = jnp.concatenate([ang, ang], axis=-1)
    cos, sin = jnp.cos(ang), jnp.sin(ang)
    first = (jnp.arange(MLA_ROPE) < ROPE_HALF)[None, :]
    sa = jnp.where(first, -sin, 0.0)
    sb = jnp.where(first, 0.0, sin)

    def place(t, fill):
        return jnp.pad(t, ((0, 0), (lane_off, width - lane_off - MLA_ROPE)), constant_values=fill)

    return place(cos, 1.0), place(sa, 0.0), place(sb, 0.0)


def _rope_apply(x, cos, sa, sb, width, transpose):
    if transpose:
        return x * cos + pltpu.roll(x * sa, ROPE_HALF, 1) + pltpu.roll(x * sb, width - ROPE_HALF, 1)
    return x * cos + pltpu.roll(x, width - ROPE_HALF, 1) * sa + pltpu.roll(x, ROPE_HALF, 1) * sb


def _rope(x, tables, *, heads, width, transpose, name, off=0, scale=1.0):
    S = x.shape[0]
    cb = _col_view(width, off)
    tr = _tile(S, 512, 16)
    tab = pl.BlockSpec((tr, width), lambda i, h: (i, 0))

    def body(x_ref, c_ref, sa_ref, sb_ref, o_ref):
        y = _rope_apply(x_ref[...].astype(F32), c_ref[...], sa_ref[...], sb_ref[...], width, transpose)
        o_ref[...] = (y if scale == 1.0 else y * scale).astype(o_ref.dtype)

    return pl.pallas_call(
        body,
        name=name,
        grid=(S // tr, heads),
        in_specs=[pl.BlockSpec((tr, width), lambda i, h: (i, cb + h)), tab, tab, tab],
        out_specs=pl.BlockSpec((tr, width), lambda i, h: (i, h)),
        out_shape=jax.ShapeDtypeStruct((S, heads * width), BF16),
        compiler_params=_params("parallel", "parallel"),
    )(x, *tables)


def _shared_rope_grad(parts, tables, *, name):
    P, S, _ = parts.shape
    tr = _tile(S, 512, 16)
    tab = pl.BlockSpec((tr, LANES), lambda i: (i, 0))

    def body(p_ref, c_ref, sa_ref, sb_ref, o_ref):
        acc = p_ref[0]
        for k in range(1, P):
            acc = acc + p_ref[k]
        o_ref[...] = _rope_apply(acc, c_ref[...], sa_ref[...], sb_ref[...], LANES, True).astype(o_ref.dtype)

    return pl.pallas_call(
        body,
        name=name,
        grid=(S // tr,),
        in_specs=[pl.BlockSpec((P, tr, LANES), lambda i: (0, i, 0)), tab, tab, tab],
        out_specs=tab,
        out_shape=jax.ShapeDtypeStruct((S, LANES), BF16),
        compiler_params=_params("parallel"),
    )(parts, *tables)


MLA_SCALE = MLA_QK ** -0.5
LOG2E = math.log2(math.e)
LN2 = math.log(2.0)
MLA_Q_PRESCALE = MLA_SCALE * LOG2E


def _lane_tile(v, n):
    return v if n == LANES else jnp.tile(v, (1, n // LANES))


def _causal_mask(s):
    rows = lax.broadcasted_iota(jnp.int32, s.shape, 0)
    cols = lax.broadcasted_iota(jnp.int32, s.shape, 1)
    return jnp.where(cols <= rows, s, NEG)


def _tri_blocks(nb, q_major):
    if q_major:
        pairs = [(q, k) for q in range(nb) for k in range(q + 1)]
    else:
        pairs = [(q, k) for k in range(nb) for q in range(k, nb)]
    return (jnp.asarray(np.array([p[0] for p in pairs], np.int32)),
            jnp.asarray(np.array([p[1] for p in pairs], np.int32)))


HEAD_PAIR = 4


def _flash_fwd(q_raw, KV, krr, tables, *, heads, name, comm=()):
    S = q_raw.shape[0]
    t = _tile(S, 512)
    nb = S // t
    qt, kt = _tri_blocks(nb, True)
    qw, vw = HEAD_PAIR * MLA_QK_PAD, HEAD_PAIR * MLA_V

    def body(qt_ref, kt_ref, q_ref, *rest):
        kn_refs, kr_ref, v_refs = rest[:HEAD_PAIR], rest[HEAD_PAIR], rest[HEAD_PAIR + 1:2 * HEAD_PAIR + 1]
        c_ref, sa_ref, sb_ref, o_ref, lse_ref, qr_ref, m_scr, l_scr, acc_scr = rest[2 * HEAD_PAIR + 1:]
        step_id = pl.program_id(1)
        qi, ki = qt_ref[step_id], kt_ref[step_id]

        @pl.when(ki == 0)
        def _():
            m_scr[...] = jnp.full_like(m_scr, NEG)
            l_scr[...] = jnp.zeros_like(l_scr)
            acc_scr[...] = jnp.zeros_like(acc_scr)
            for h in range(HEAD_PAIR):
                base = h * MLA_QK_PAD
                nope = q_ref[:, base:base + MLA_NOPE].astype(F32) * MLA_Q_PRESCALE
                rot = _rope_apply(q_ref[:, base + MLA_NOPE:base + MLA_QK_PAD].astype(F32), c_ref[...], sa_ref[...],
                                  sb_ref[...], LANES, False) * MLA_Q_PRESCALE
                qr_ref[:, base:base + MLA_NOPE] = nope.astype(qr_ref.dtype)
                qr_ref[:, base + MLA_NOPE:base + MLA_QK_PAD] = rot.astype(qr_ref.dtype)

        def step(diagonal):
            for h, (kn_ref, v_ref) in enumerate(zip(kn_refs, v_refs)):
                cols = slice(h * MLA_QK_PAD, (h + 1) * MLA_QK_PAD)
                k = jnp.concatenate([kn_ref[...], kr_ref[...]], axis=1)
                s = lax.dot_general(qr_ref[:, cols], k, NT, preferred_element_type=F32)
                if diagonal:
                    s = _causal_mask(s)
                m_prev = m_scr[h]
                m_new = jnp.maximum(m_prev, jnp.max(s, axis=1, keepdims=True))
                alpha = jnp.exp2(m_prev - m_new)
                p = jnp.exp2(s - _lane_tile(m_new, t))
                l_new = alpha * l_scr[h] + jnp.sum(p, axis=1, keepdims=True)
                acc = alpha * acc_scr[h] + jnp.dot(p.astype(BF16), v_ref[...], preferred_element_type=F32)
                if diagonal:
                    o_ref[:, h * MLA_V:(h + 1) * MLA_V] = (acc / l_new).astype(o_ref.dtype)
                    lse_ref[h] = m_new + jnp.log(l_new) * LOG2E
                else:
                    l_scr[h], acc_scr[h], m_scr[h] = l_new, acc, m_new

        pl.when(ki < qi)(lambda: step(False))
        pl.when(ki == qi)(lambda: step(True))

    def kvspec(h, half):
        return pl.BlockSpec((t, LANES), lambda hp, s, qt, kt: (kt[s], 2 * (HEAD_PAIR * hp + h) + half))

    qtab = pl.BlockSpec((t, LANES), lambda hp, s, qt, kt: (qt[s], 0))
    qrow = lambda hp, s, qt, kt: (qt[s], hp)
    return _call(
        body,
        name=name,
        grid=(heads // HEAD_PAIR, int(qt.shape[0])),
        in_specs=[pl.BlockSpec((t, qw), qrow), *[kvspec(h, 0) for h in range(HEAD_PAIR)],
                  pl.BlockSpec((t, LANES), lambda hp, s, qt, kt: (kt[s], 0)),
                  *[kvspec(h, 1) for h in range(HEAD_PAIR)], qtab, qtab, qtab],
        out_specs=[pl.BlockSpec((t, vw), qrow),
                   pl.BlockSpec((HEAD_PAIR, t, LANES), lambda hp, s, qt, kt: (hp, qt[s], 0)),
                   pl.BlockSpec((t, qw), qrow)],
        out_shape=[jax.ShapeDtypeStruct((S, heads * MLA_V), BF16),
                   jax.ShapeDtypeStruct((heads, S, LANES), F32),
                   jax.ShapeDtypeStruct((S, heads * MLA_QK_PAD), BF16)],
        scratch_shapes=[pltpu.VMEM((HEAD_PAIR, t, LANES), F32), pltpu.VMEM((HEAD_PAIR, t, LANES), F32),
                        pltpu.VMEM((HEAD_PAIR, t, MLA_V), F32)],
        sem=("parallel", "arbitrary"),
        args=(q_raw, *[KV] * HEAD_PAIR, krr, *[KV] * HEAD_PAIR, *tables),
        comm=comm,
        prefetch=(qt, kt),
    )


def _flash_bwd(Q, KV, krr, dO, O, lse, tables, *, heads, name, comm=()):
    S = Q.shape[0]
    t = _tile(S, 512)
    nb = S // t
    qt, kt = _tri_blocks(nb, False)
    n_steps = int(qt.shape[0])
    qw, vw = HEAD_PAIR * MLA_QK_PAD, HEAD_PAIR * MLA_V

    def body(qt_ref, kt_ref, q_ref, *rest):
        kn_refs, kr_ref, v_refs = rest[:HEAD_PAIR], rest[HEAD_PAIR], rest[HEAD_PAIR + 1:2 * HEAD_PAIR + 1]
        (do_ref, o_ref, lse_ref, c_ref, sa_ref, sb_ref, dq_ref, dkv_ref, dkr_ref,
         dq_scr, dk_scr, dv_scr, delta_scr) = rest[2 * HEAD_PAIR + 1:]
        step_id = pl.program_id(1)
        qi, ki = qt_ref[step_id], kt_ref[step_id]

        @pl.when(ki == 0)
        def _():
            for h in range(HEAD_PAIR):
                vc = slice(h * MLA_V, (h + 1) * MLA_V)
                d = jnp.sum(do_ref[:, vc].astype(F32) * o_ref[:, vc].astype(F32), axis=1, keepdims=True)
                delta_scr[h, qi] = jnp.broadcast_to(d, (t, LANES))

        def step(diagonal):
            for h, (kn_ref, v_ref) in enumerate(zip(kn_refs, v_refs)):
                base = h * MLA_QK_PAD
                cols = slice(base, base + MLA_QK_PAD)
                vc = slice(h * MLA_V, (h + 1) * MLA_V)
                q, do = q_ref[:, cols], do_ref[:, vc]
                k = jnp.concatenate([kn_ref[...], kr_ref[...]], axis=1)
                s = lax.dot_general(q, k, NT, preferred_element_type=F32)
                if diagonal:
                    s = _causal_mask(s)
                p = jnp.exp2(s - _lane_tile(lse_ref[h], t))
                dv = lax.dot_general(p.astype(BF16), do, TN, preferred_element_type=F32)
                dp = lax.dot_general(do, v_ref[...], NT, preferred_element_type=F32)
                ds = (p * (dp - _lane_tile(delta_scr[h, qi], t))).astype(BF16)
                dk = lax.dot_general(ds, q, TN, preferred_element_type=F32)
                dq = jnp.dot(ds, k, preferred_element_type=F32)
                if diagonal:
                    dk_scr[h], dv_scr[h] = dk, dv
                    dq = (dq_scr[qi, :, cols] + dq) * (LN2 * MLA_Q_PRESCALE)
                    rot = _rope_apply(dq[:, MLA_NOPE:], c_ref[...], sa_ref[...], sb_ref[...], LANES, True)
                    dq_ref[:, base:base + MLA_NOPE] = dq[:, :MLA_NOPE].astype(dq_ref.dtype)
                    dq_ref[:, base + MLA_NOPE:base + MLA_QK_PAD] = rot.astype(dq_ref.dtype)
                else:
                    dk_scr[h] += dk
                    dv_scr[h] += dv
                    dq_scr[qi, :, cols] += dq

        @pl.when(ki == 0)
        def _():
            dq_scr[qi] = jnp.zeros((t, qw), F32)

        pl.when(qi > ki)(lambda: step(False))
        pl.when(qi == ki)(lambda: step(True))

        @pl.when(qi == nb - 1)
        def _():
            shared = jnp.zeros((t, LANES), F32)
            for h in range(HEAD_PAIR):
                base = h * MLA_QK_PAD
                dk = dk_scr[h] * LN2
                dkv_ref[:, base:base + MLA_NOPE] = dk[:, :MLA_NOPE].astype(dkv_ref.dtype)
                dkv_ref[:, base + MLA_NOPE:base + MLA_QK_PAD] = dv_scr[h].astype(dkv_ref.dtype)
                shared = shared + dk[:, MLA_NOPE:]
            dkr_ref[0] = shared

    def kvspec(h, half):
        return pl.BlockSpec((t, LANES), lambda hp, s, qt, kt: (kt[s], 2 * (HEAD_PAIR * hp + h) + half))

    qrow = lambda hp, s, qt, kt: (qt[s], hp)
    krow = lambda hp, s, qt, kt: (kt[s], hp)
    ktab = pl.BlockSpec((t, LANES), lambda hp, s, qt, kt: (kt[s], 0))
    return _call(
        body,
        name=name,
        grid=(heads // HEAD_PAIR, n_steps),
        in_specs=[pl.BlockSpec((t, qw), qrow), *[kvspec(h, 0) for h in range(HEAD_PAIR)], ktab,
                  *[kvspec(h, 1) for h in range(HEAD_PAIR)],
                  pl.BlockSpec((t, vw), qrow), pl.BlockSpec((t, vw), qrow),
                  pl.BlockSpec((HEAD_PAIR, t, LANES), lambda hp, s, qt, kt: (hp, qt[s], 0)),
                  ktab, ktab, ktab],
        out_specs=[pl.BlockSpec((t, qw), krow), pl.BlockSpec((t, qw), krow),
                   pl.BlockSpec((1, t, LANES), lambda hp, s, qt, kt: (hp, kt[s], 0))],
        out_shape=[jax.ShapeDtypeStruct((S, heads * MLA_QK_PAD), BF16),
                   jax.ShapeDtypeStruct((S, heads * MLA_QK_PAD), BF16),
                   jax.ShapeDtypeStruct((heads // HEAD_PAIR, S, LANES), F32)],
        scratch_shapes=[pltpu.VMEM((nb, t, qw), F32), pltpu.VMEM((HEAD_PAIR, t, MLA_QK_PAD), F32),
                        pltpu.VMEM((HEAD_PAIR, t, MLA_V), F32), pltpu.VMEM((HEAD_PAIR, nb, t, LANES), F32)],
        sem=("parallel", "arbitrary"),
        args=(Q, *[KV] * HEAD_PAIR, krr, *[KV] * HEAD_PAIR, dO, O, lse, *tables),
        comm=comm,
        prefetch=(qt, kt),
    )


SWA_SCALE = SWA_HD ** -0.5


def _t5_bucket_table():
    a = np.arange(BLOCK)[:, None]
    j = np.arange(2 * BLOCK)[None, :]
    dist = BLOCK + a - j
    max_exact = REL_BUCKETS // 2
    n = np.maximum(dist, 0)
    large = max_exact + (np.log(np.maximum(n, 1).astype(np.float32) / np.float32(max_exact))
                         / np.float32(math.log(REL_MAX_DIST / max_exact))
                         * np.float32(REL_BUCKETS - max_exact)).astype(np.int32)
    large = np.minimum(large, REL_BUCKETS - 1)
    bucket = np.where(n < max_exact, n, large)
    valid = (dist >= 0) & (dist < WINDOW)
    return bucket.astype(np.int32), valid


def _heads_to_rows(x, G):
    return jnp.concatenate([x[:, g * SWA_HD:(g + 1) * SWA_HD] for g in range(G)], axis=0)


def _rows_to_heads(o_ref, x, G):
    for g in range(G):
        o_ref[:, g * SWA_HD:(g + 1) * SWA_HD] = x[g * BLOCK:(g + 1) * BLOCK].astype(o_ref.dtype)


def _swa_probs(q_ref, kp_ref, kc_ref, bias_ref, sink_ref, qb, G):
    q2 = _heads_to_rows(q_ref[...], G)
    kb = jnp.concatenate([kp_ref[0], kc_ref[0]], axis=0)
    s = lax.dot_general(kb, q2, NT, preferred_element_type=F32) * SWA_SCALE + bias_ref[0]
    keys = lax.broadcasted_iota(jnp.int32, s.shape, 0)
    s = jnp.where((keys >= BLOCK) | (qb > 0), s, NEG)
    sink = sink_ref[0]
    m = jnp.maximum(jnp.max(s, axis=0, keepdims=True), sink)
    e = jnp.exp(s - m)
    es = jnp.exp(sink - m)
    inv = 1.0 / (jnp.sum(e, axis=0, keepdims=True) + es)
    return q2, kb, e * inv, es * inv


def _swa_fwd(q, q_off, k, v, bias_t, sink, *, name, comm=()):
    S = k.shape[1]
    G = bias_t.shape[2] // BLOCK
    nb = S // BLOCK
    qcol = _col_view(G * SWA_HD, q_off)
    cur = lambda kh, qb: (kh, qb, 0)
    prev = lambda kh, qb: (kh, jnp.maximum(qb - 1, 0), 0)
    kvspec = lambda im: pl.BlockSpec((1, BLOCK, SWA_HD), im)

    def body(q_ref, kc_ref, kp_ref, vc_ref, vp_ref, bias_ref, sink_ref, o_ref):
        qb = pl.program_id(1)
        _, _, pt, _ = _swa_probs(q_ref, kp_ref, kc_ref, bias_ref, sink_ref, qb, G)
        vb = jnp.concatenate([vp_ref[0], vc_ref[0]], axis=0)
        _rows_to_heads(o_ref, lax.dot_general(pt.astype(BF16), vb, TN, preferred_element_type=F32), G)

    outs, moved = _call(
        body,
        name=name,
        grid=(SWA_KVH, nb),
        in_specs=[pl.BlockSpec((BLOCK, G * SWA_HD), lambda kh, qb: (qb, qcol + kh)),
                  kvspec(cur), kvspec(prev), kvspec(cur), kvspec(prev),
                  pl.BlockSpec((1, 2 * BLOCK, G * BLOCK), lambda kh, qb: (kh, 0, 0)),
                  pl.BlockSpec((1, 1, G * BLOCK), lambda kh, qb: (kh, 0, 0))],
        out_specs=[pl.BlockSpec((BLOCK, G * SWA_HD), lambda kh, qb: (qb, kh))],
        out_shape=[jax.ShapeDtypeStruct((S, SWA_KVH * G * SWA_HD), BF16)],
        sem=("parallel", "parallel"),
        args=(q, k, k, v, v, bias_t, sink),
        comm=comm,
    )
    return outs[0], moved


def _swa_bwd(q, q_off, k, v, bias_t, sink, do, *, name, comm=()):
    S = k.shape[1]
    G = bias_t.shape[2] // BLOCK
    nb = S // BLOCK
    qcol = _col_view(G * SWA_HD, q_off)
    cur = lambda kh, qb: (kh, jnp.minimum(qb, nb - 1), 0)
    prev = lambda kh, qb: (kh, jnp.maximum(jnp.minimum(qb, nb - 1) - 1, 0), 0)
    lag = lambda kh, qb: (kh, jnp.maximum(qb - 1, 0), 0)
    kvspec = lambda im: pl.BlockSpec((1, BLOCK, SWA_HD), im)

    def body(q_ref, kc_ref, kp_ref, vc_ref, vp_ref, bias_ref, sink_ref, do_ref,
             dq_ref, dk_ref, dv_ref, dbias_ref, dsink_ref, ck_scr, cv_scr):
        qb = pl.program_id(1)

        @pl.when(qb == 0)
        def _():
            dbias_ref[...] = jnp.zeros_like(dbias_ref)
            dsink_ref[...] = jnp.zeros_like(dsink_ref)
            ck_scr[...] = jnp.zeros_like(ck_scr)
            cv_scr[...] = jnp.zeros_like(cv_scr)

        @pl.when(qb < nb)
        def _():
            q2, kb, pt, ps = _swa_probs(q_ref, kp_ref, kc_ref, bias_ref, sink_ref, qb, G)
            vb = jnp.concatenate([vp_ref[0], vc_ref[0]], axis=0)
            do2 = _heads_to_rows(do_ref[...], G)
            dpt = lax.dot_general(vb, do2, NT, preferred_element_type=F32)
            delta = jnp.sum(dpt * pt, axis=0, keepdims=True)
            dst = pt * (dpt - delta)
            dbias_ref[0] += dst
            dsink_ref[0] += -ps * delta
            dsb = (dst * SWA_SCALE).astype(BF16)
            _rows_to_heads(dq_ref, lax.dot_general(dsb, kb, TN, preferred_element_type=F32), G)
            dkb = jnp.dot(dsb, q2, preferred_element_type=F32)
            dvb = jnp.dot(pt.astype(BF16), do2, preferred_element_type=F32)
            dk_ref[0] = (ck_scr[...] + dkb[:BLOCK]).astype(dk_ref.dtype)
            dv_ref[0] = (cv_scr[...] + dvb[:BLOCK]).astype(dv_ref.dtype)
            ck_scr[...] = dkb[BLOCK:]
            cv_scr[...] = dvb[BLOCK:]

        @pl.when(qb == nb)
        def _():
            dk_ref[0] = ck_scr[...].astype(dk_ref.dtype)
            dv_ref[0] = cv_scr[...].astype(dv_ref.dtype)

    tspec = pl.BlockSpec((BLOCK, G * SWA_HD), lambda kh, qb: (jnp.minimum(qb, nb - 1), kh))
    return _call(
        body,
        name=name,
        grid=(SWA_KVH, nb + 1),
        in_specs=[pl.BlockSpec((BLOCK, G * SWA_HD), lambda kh, qb: (jnp.minimum(qb, nb - 1), qcol + kh)),
                  kvspec(cur), kvspec(prev), kvspec(cur), kvspec(prev),
                  pl.BlockSpec((1, 2 * BLOCK, G * BLOCK), lambda kh, qb: (kh, 0, 0)),
                  pl.BlockSpec((1, 1, G * BLOCK), lambda kh, qb: (kh, 0, 0)),
                  pl.BlockSpec((BLOCK, G * SWA_HD), lambda kh, qb: (jnp.minimum(qb, nb - 1), kh))],
        out_specs=[tspec, kvspec(lag), kvspec(lag),
                   pl.BlockSpec((1, 2 * BLOCK, G * BLOCK), lambda kh, qb: (kh, 0, 0)),
                   pl.BlockSpec((1, 1, G * BLOCK), lambda kh, qb: (kh, 0, 0))],
        out_shape=[jax.ShapeDtypeStruct((S, SWA_KVH * G * SWA_HD), BF16),
                   jax.ShapeDtypeStruct((SWA_KVH, S, SWA_HD), BF16),
                   jax.ShapeDtypeStruct((SWA_KVH, S, SWA_HD), BF16),
                   jax.ShapeDtypeStruct((SWA_KVH, 2 * BLOCK, G * BLOCK), F32),
                   jax.ShapeDtypeStruct((SWA_KVH, 1, G * BLOCK), F32)],
        scratch_shapes=[pltpu.VMEM((BLOCK, SWA_HD), F32), pltpu.VMEM((BLOCK, SWA_HD), F32)],
        sem=("parallel", "arbitrary"),
        args=(q, k, k, v, v, bias_t, sink, do),
        comm=comm,
    )


def _gate_mix(z, o_a, o_b, *, D, off_a, off_b, name):
    S = z.shape[0]
    tr = _tile(S, 256, 16)
    row = pl.BlockSpec((tr, D), lambda i: (i, 0))
    ca, cb = _col_view(D, off_a), _col_view(D, off_b)

    def body(ga_ref, gb_ref, oa_ref, ob_ref, m_ref):
        m = (_sigmoid(ga_ref[...].astype(F32)) * oa_ref[...].astype(F32)
             + _sigmoid(gb_ref[...].astype(F32)) * ob_ref[...].astype(F32))
        m_ref[...] = m.astype(m_ref.dtype)

    return pl.pallas_call(
        body,
        name=name,
        grid=(S // tr,),
        in_specs=[pl.BlockSpec((tr, D), lambda i: (i, ca)), pl.BlockSpec((tr, D), lambda i: (i, cb)), row, row],
        out_specs=row,
        out_shape=jax.ShapeDtypeStruct((S, D), BF16),
        compiler_params=_params("parallel"),
    )(z, z, o_a, o_b)


def _gate_mix_bwd(dm, z, o_a, o_b, *, D, off_a, off_b, name):
    S = z.shape[0]
    tr = _tile(S, 256, 16)
    row = pl.BlockSpec((tr, D), lambda i: (i, 0))
    ca, cb = _col_view(D, off_a), _col_view(D, off_b)

    def body(dm_ref, ga_ref, gb_ref, oa_ref, ob_ref, dga_ref, dgb_ref, doa_ref, dob_ref):
        d = dm_ref[...].astype(F32)
        for g_ref, o_ref, dg_ref, do_ref in ((ga_ref, oa_ref, dga_ref, doa_ref), (gb_ref, ob_ref, dgb_ref, dob_ref)):
            sg = _sigmoid(g_ref[...].astype(F32))
            dg_ref[...] = (d * o_ref[...].astype(F32) * (sg * (1.0 - sg))).astype(dg_ref.dtype)
            do_ref[...] = (d * sg).astype(do_ref.dtype)

    return pl.pallas_call(
        body,
        name=name,
        grid=(S // tr,),
        in_specs=[row, pl.BlockSpec((tr, D), lambda i: (i, ca)), pl.BlockSpec((tr, D), lambda i: (i, cb)), row, row],
        out_specs=[row] * 4,
        out_shape=[jax.ShapeDtypeStruct((S, D), BF16)] * 4,
        compiler_params=_params("parallel"),
    )(dm, z, z, o_a, o_b)


CONV_ROWS = 256
CONV_COLS = 1408
SUBLANES = 8


def _shift_matrices(tr):
    r = np.arange(tr)[:, None]
    c = np.arange(tr)[None, :]
    back = [jnp.asarray(r == c + d, dtype=BF16) for d in (1, 2)]
    ahead = [jnp.asarray(r + d == c, dtype=BF16) for d in (1, 2)]
    return back, ahead


def _rows_before(x, halo_ref, first, b1_ref, b2_ref):
    s1 = jnp.dot(b1_ref[...], x, preferred_element_type=F32)
    s2 = jnp.dot(b2_ref[...], x, preferred_element_type=F32)
    h8 = jnp.where(first, 0.0, halo_ref[...].astype(F32)[HALO - SUBLANES:])
    rows = lax.broadcasted_iota(jnp.int32, h8.shape, 0)
    fix1 = jnp.where(rows < 1, pltpu.roll(h8, 1, 0), 0.0)
    fix2 = jnp.where(rows < 2, pltpu.roll(h8, 2, 0), 0.0)
    s1 = jnp.concatenate([s1[:SUBLANES] + fix1, s1[SUBLANES:]], axis=0)
    s2 = jnp.concatenate([s2[:SUBLANES] + fix2, s2[SUBLANES:]], axis=0)
    return s1, s2


def _conv_taps(x, s1, s2, cw_ref, cb_ref):
    return cb_ref[...] + cw_ref[0:1, :] * s2 + cw_ref[1:2, :] * s1 + cw_ref[2:3, :] * x


def _conv_gate(up, cw, cb, *, name):
    S, F2 = up.shape
    F = F2 // 2
    tr = _tile(S, CONV_ROWS, HALO)
    tc = _tile(F, CONV_COLS)
    nc = F // tc
    hb = tr // HALO
    back, _ = _shift_matrices(tr)
    mat = pl.BlockSpec((tr, tr), lambda i, j: (0, 0))

    def halo_map(shift):
        return lambda i, j: (jnp.maximum(i * hb - 1, 0), j + shift)

    def body(x1_ref, h1_ref, x2_ref, h2_ref, cw1_ref, cw2_ref, cb1_ref, cb2_ref, b1_ref, b2_ref, a_ref):
        first = pl.program_id(0) == 0
        us = []
        for x_ref, h_ref, cw_ref, cb_ref in ((x1_ref, h1_ref, cw1_ref, cb1_ref), (x2_ref, h2_ref, cw2_ref, cb2_ref)):
            x = x_ref[...]
            s1, s2 = _rows_before(x, h_ref, first, b1_ref, b2_ref)
            us.append(_conv_taps(x.astype(F32), s1, s2, cw_ref, cb_ref))
        u1, u2 = us
        a_ref[...] = (u1 * _sigmoid(u1) * u2).astype(a_ref.dtype)

    return pl.pallas_call(
        body,
        name=name,
        grid=(S // tr, nc),
        in_specs=[pl.BlockSpec((tr, tc), lambda i, j: (i, j)), pl.BlockSpec((HALO, tc), halo_map(0)),
                  pl.BlockSpec((tr, tc), lambda i, j: (i, j + nc)), pl.BlockSpec((HALO, tc), halo_map(nc)),
                  pl.BlockSpec((CONV_WIDTH, tc), lambda i, j: (0, j)),
                  pl.BlockSpec((CONV_WIDTH, tc), lambda i, j: (0, j + nc)),
                  pl.BlockSpec((1, tc), lambda i, j: (0, j)), pl.BlockSpec((1, tc), lambda i, j: (0, j + nc)),
                  mat, mat],
        out_specs=pl.BlockSpec((tr, tc), lambda i, j: (i, j)),
        out_shape=jax.ShapeDtypeStruct((S, F), BF16),
        compiler_params=_params("parallel", "parallel"),
    )(up, up, up, up, cw, cw, cb, cb, *back)


def _conv_gate_bwd(up, da, cw, cb, *, name, comm=()):
    S, F2 = up.shape
    F = F2 // 2
    tr = _tile(S, CONV_ROWS, HALO)
    tc = _tile(F, CONV_COLS)
    nc = F // tc
    hb = tr // HALO
    ni = S // tr
    back, ahead = _shift_matrices(tr)
    mat = pl.BlockSpec((tr, tr), lambda j, r: (0, 0))

    def cur(shift):
        return lambda j, r: (ni - 1 - r, j + shift)

    def before(shift):
        return lambda j, r: (jnp.maximum((ni - 1 - r) * hb - 1, 0), j + shift)

    def vec(rows, shift):
        return pl.BlockSpec((rows, tc), lambda j, r: (0, j + shift))

    def body(x1_ref, h1_ref, x2_ref, h2_ref, da_ref, cw1_ref, cw2_ref, cb1_ref, cb2_ref,
             b1_ref, b2_ref, a1_ref, a2_ref, dup_ref, dcw_ref, dcb_ref, next_du):
        r = pl.program_id(1)
        first = r == ni - 1

        @pl.when(r == 0)
        def _():
            dcw_ref[...] = jnp.zeros_like(dcw_ref)
            dcb_ref[...] = jnp.zeros_like(dcb_ref)
            next_du[...] = jnp.zeros_like(next_du)

        x1, x2 = x1_ref[...], x2_ref[...]
        x1f, x2f = x1.astype(F32), x2.astype(F32)
        s11, s12 = _rows_before(x1, h1_ref, first, b1_ref, b2_ref)
        s21, s22 = _rows_before(x2, h2_ref, first, b1_ref, b2_ref)
        u1 = _conv_taps(x1f, s11, s12, cw1_ref, cb1_ref)
        u2 = _conv_taps(x2f, s21, s22, cw2_ref, cb2_ref)
        sg = _sigmoid(u1)
        daf = da_ref[...].astype(F32)
        du1 = daf * u2 * (sg * (1.0 + u1 * (1.0 - sg)))
        du2 = daf * (u1 * sg)
        rows = lax.broadcasted_iota(jnp.int32, (SUBLANES, tc), 0)

        for half, (du, own, own1, own2, cw_ref) in enumerate(((du1, x1f, s11, s12, cw1_ref),
                                                             (du2, x2f, s21, s22, cw2_ref))):
            du_b = du.astype(BF16)
            n1 = jnp.dot(a1_ref[...], du_b, preferred_element_type=F32)
            n2 = jnp.dot(a2_ref[...], du_b, preferred_element_type=F32)
            c8 = next_du[half]
            fix1 = jnp.where(rows >= SUBLANES - 1, pltpu.roll(c8, SUBLANES - 1, 0), 0.0)
            fix2 = jnp.where(rows >= SUBLANES - 2, pltpu.roll(c8, SUBLANES - 2, 0), 0.0)
            n1 = jnp.concatenate([n1[:tr - SUBLANES], n1[tr - SUBLANES:] + fix1], axis=0)
            n2 = jnp.concatenate([n2[:tr - SUBLANES], n2[tr - SUBLANES:] + fix2], axis=0)
            dup = cw_ref[2:3, :] * du + cw_ref[1:2, :] * n1 + cw_ref[0:1, :] * n2
            dup_ref[half] = dup.astype(dup_ref.dtype)
            dcb_ref[half] += jnp.sum(du, axis=0, keepdims=True)
            for tap, shifted in enumerate((own2, own1, own)):
                dcw_ref[half, tap:tap + 1, :] += jnp.sum(du * shifted, axis=0, keepdims=True)
            next_du[half] = du[:SUBLANES].astype(BF16).astype(F32)

    return _call(
        body,
        name=name,
        grid=(nc, ni),
        in_specs=[pl.BlockSpec((tr, tc), cur(0)), pl.BlockSpec((HALO, tc), before(0)),
                  pl.BlockSpec((tr, tc), cur(nc)), pl.BlockSpec((HALO, tc), before(nc)),
                  pl.BlockSpec((tr, tc), cur(0)),
                  vec(CONV_WIDTH, 0), vec(CONV_WIDTH, nc), vec(1, 0), vec(1, nc), mat, mat, mat, mat],
        out_specs=[pl.BlockSpec((2, tr, tc), lambda j, r: (0, ni - 1 - r, j)),
                   pl.BlockSpec((2, CONV_WIDTH, tc), lambda j, r: (0, 0, j)),
                   pl.BlockSpec((2, 1, tc), lambda j, r: (0, 0, j))],
        out_shape=[jax.ShapeDtypeStruct((2, S, F), BF16), jax.ShapeDtypeStruct((2, CONV_WIDTH, F), F32),
                   jax.ShapeDtypeStruct((2, 1, F), F32)],
        scratch_shapes=[pltpu.VMEM((2, SUBLANES, tc), F32)],
        sem=("parallel", "arbitrary"),
        args=(up, up, up, up, da, cw, cw, cb, cb, *back, *ahead),
        comm=comm,
    )


def _adam_math(w, g, m, v):
    m = ADAM_B1 * m + (1.0 - ADAM_B1) * g
    v = ADAM_B2 * v + (1.0 - ADAM_B2) * (g * g)
    m_hat = m / (1.0 - ADAM_B1 ** ADAM_STEP)
    v_hat = v / (1.0 - ADAM_B2 ** ADAM_STEP)
    delta = -ADAM_LR * (m_hat / (jnp.sqrt(v_hat) + ADAM_EPS) + ADAM_WD * w)
    return delta, m, v


def _adamw(w, m, v, parts, *, name, comm=()):
    R, C = w.shape
    plist = list(parts) if isinstance(parts, (list, tuple)) else [parts]
    tr = _tile(min(p.shape[1] for p in plist), 256, 16)
    assert sum(p.shape[1] for p in plist) == R and all(p.shape[1] % tr == 0 for p in plist)
    row = pl.BlockSpec((tr, C), lambda i: (i, 0))
    first, spans = 0, []
    for p in plist:
        spans.append((first, first + p.shape[1] // tr))
        first = spans[-1][1]

    def body(w_ref, m_ref, v_ref, *rest):
        p_refs, (g_ref, d_ref, m2_ref, v2_ref) = rest[:len(plist)], rest[len(plist):]
        i = pl.program_id(0)

        def update(p_ref):
            g = p_ref[0].astype(F32)
            for k in range(1, N_DEV):
                g = g + p_ref[k].astype(F32)
            g_ref[...] = g
            d_ref[...], m2_ref[...], v2_ref[...] = _adam_math(w_ref[...], g, m_ref[...], v_ref[...])

        if len(plist) == 1:
            update(p_refs[0])
        else:
            for p_ref, (lo, hi) in zip(p_refs, spans):
                pl.when((i >= lo) & (i < hi))(functools.partial(update, p_ref))

    def part_spec(lo, hi):
        return pl.BlockSpec((N_DEV, tr, C), lambda i: (0, jnp.clip(i - lo, 0, hi - lo - 1), 0))

    outs, moved = _call(
        body,
        name=name,
        grid=(R // tr,),
        in_specs=[row, row, row] + [part_spec(lo, hi) for lo, hi in spans],
        out_specs=[row] * 4,
        out_shape=[jax.ShapeDtypeStruct((R, C), F32)] * 4,
        sem=("parallel",),
        args=(w, m, v, *plist),
        comm=comm,
    )
    return (outs, moved) if comm else outs


def _adamw_ada(w, m, v, cact_t, dmod_cols, *, name, comm=()):
    R, C = w.shape
    B = cact_t.shape[1]
    tr = _tile(R, 256, 8)
    row = pl.BlockSpec((tr, C), lambda i: (i, 0))

    def body(w_ref, m_ref, v_ref, c_ref, d_ref, g_ref, dl_ref, m2_ref, v2_ref):
        g = c_ref[:, 0:1] * d_ref[0:1, :]
        for b in range(1, B):
            g = g + c_ref[:, b:b + 1] * d_ref[b:b + 1, :]
        g_ref[...] = g
        dl_ref[...], m2_ref[...], v2_ref[...] = _adam_math(w_ref[...], g, m_ref[...], v_ref[...])

    outs, moved = _call(
        body,
        name=name,
        grid=(R // tr,),
        in_specs=[row, row, row, pl.BlockSpec((tr, B), lambda i: (i, 0)), pl.BlockSpec((B, C), lambda i: (0, 0))],
        out_specs=[row] * 4,
        out_shape=[jax.ShapeDtypeStruct((R, C), F32)] * 4,
        sem=("parallel",),
        args=(w, m, v, cact_t, dmod_cols),
        comm=comm,
    )
    return (outs, moved) if comm else outs


def _z_layout(D, q_rank, kv_rank):
    kv = SWA_KVH * SWA_HD
    orig = {}
    o = 0
    for nm, w in (("cq", q_rank), ("ckv", kv_rank), ("kr", MLA_ROPE), ("qs", D), ("ks", kv), ("vs", kv),
                  ("ga", D), ("gb", D)):
        orig[nm] = (o, w)
        o += w
    blockw = {"cq": q_rank, "ckv": kv_rank, "kr": LANES, "qs": D, "ks": kv, "vs": kv, "ga": D, "gb": D}
    best = None
    for perm in itertools.permutations(("cq", "ckv", "ks", "vs", "kr")):
        off, new = 0, {}
        for nm in ("ga", "gb", "qs") + perm:
            off = _round_up(off, blockw[nm])
            new[nm] = off
            off += blockw[nm]
        if best is None or off < best[0]:
            best = (off, new)
    total = _round_up(best[0], 1024 if best[0] > 4096 else 512)
    return orig, best[1], blockw, total, o


def _permute_w_in(w, lay):
    orig, new, blockw, total, _ = lay
    parts, at = [], 0
    for nm in sorted(new, key=new.get):
        if new[nm] > at:
            parts.append(jnp.zeros((w.shape[0], new[nm] - at), w.dtype))
        o, wd = orig[nm]
        parts.append(w[:, o:o + wd])
        if blockw[nm] > wd:
            parts.append(jnp.zeros((w.shape[0], blockw[nm] - wd), w.dtype))
        at = new[nm] + blockw[nm]
    if total > at:
        parts.append(jnp.zeros((w.shape[0], total - at), w.dtype))
    return jnp.concatenate(parts, axis=1)


def _unpermute_w_in(wp, lay):
    orig, new, _, _, _ = lay
    return jnp.concatenate([wp[:, new[nm]:new[nm] + orig[nm][1]] for nm in sorted(orig, key=lambda n: orig[n][0])],
                           axis=1)


def _assemble_dz(parts, lay, S):
    _, new, blockw, total, _ = lay
    names = sorted(new, key=new.get)
    tr = _tile(S, 256, 16)

    def body(*refs):
        o_ref = refs[-1]
        cols, at = [], 0
        for nm, ref in zip(names, refs):
            if new[nm] > at:
                cols.append(jnp.zeros((tr, new[nm] - at), BF16))
            cols.append(ref[...])
            at = new[nm] + blockw[nm]
        if total > at:
            cols.append(jnp.zeros((tr, total - at), BF16))
        o_ref[...] = jnp.concatenate(cols, axis=1)

    return pl.pallas_call(
        body,
        name="assemble_dz",
        grid=(S // tr,),
        in_specs=[pl.BlockSpec((tr, blockw[nm]), lambda i: (i, 0)) for nm in names],
        out_specs=pl.BlockSpec((tr, total), lambda i: (i, 0)),
        out_shape=jax.ShapeDtypeStruct((S, total), BF16),
        compiler_params=_params("parallel"),
    )(*[parts[nm] for nm in names])


def _unshard_cols(g):
    return jnp.transpose(g, (1, 0, 2)).reshape(g.shape[1], N_DEV * g.shape[2])


def _shard_cols(w):
    K, N = w.shape
    return jnp.transpose(w.reshape(K, N_DEV, N // N_DEV), (1, 0, 2))


def _pack(vecs, rows):
    flat = jnp.concatenate([v.reshape(-1) for v in vecs])
    return jnp.pad(flat, (0, rows * LANES - flat.shape[0])).reshape(rows, LANES)


def kernel(x, c, w_ada, b_ada, g_pre_mix, g_post_mix, w_in, g_q_lat, w_uq, g_kv_lat, w_ukv, rel_bias, sinks, w_o, g_pre_ffn, g_post_ffn, w_up, conv_w, conv_b, w_down, loss_target, m_w_ada, m_b_ada, m_g_pre_mix, m_g_post_mix, m_w_in, m_g_q_lat, m_w_uq, m_g_kv_lat, m_w_ukv, m_rel_bias, m_sinks, m_w_o, m_g_pre_ffn, m_g_post_ffn, m_w_up, m_conv_w, m_conv_b, m_w_down, v_w_ada, v_b_ada, v_g_pre_mix, v_g_post_mix, v_w_in, v_g_q_lat, v_w_uq, v_g_kv_lat, v_w_ukv, v_rel_bias, v_sinks, v_w_o, v_g_pre_ffn, v_g_post_ffn, v_w_up, v_conv_w, v_conv_b, v_w_down):
    S, D = x.shape[1], x.shape[2]
    Q_RANK, KV_RANK = g_q_lat.shape[1], g_kv_lat.shape[1]
    H_MLA = D // MLA_V
    H_SWA = D // SWA_HD
    G_SWA = H_SWA // SWA_KVH
    F2 = w_up.shape[2] * N_DEV
    F = F2 // 2
    ada_n = w_ada.shape[2]
    me = 4 * lax.axis_index("x") + 2 * lax.axis_index("y") + lax.axis_index("c")
    lay = _z_layout(D, Q_RANK, KV_RANK)
    _, zoff, _, NZ, in_cols = lay
    assert in_cols == w_in.shape[2] * N_DEV

    x2, tgt = x[0], loss_target[0]

    cw_n = conv_w.shape[2]
    small = jnp.concatenate([jnp.pad(c, ((0, 7), (0, 0))), jnp.pad(conv_w[0], ((0, 8 - CONV_WIDTH), (0, 0)))], axis=1)
    small_all = _all_gather(small, name="ag_cond", in_vmem=True)
    c_all = small_all[:, 0, :D]
    cw_full = _unshard_cols(small_all[:, :CONV_WIDTH, D:])
    b_cols = lax.dynamic_slice_in_dim(b_ada, me * ada_n, ada_n, axis=1)
    c_act, mod_cols = _ada_fwd(c_all, w_ada[0], b_cols, name="ada_fwd")
    mod_all = _all_gather(mod_cols, name="ag_mod", in_vmem=True)
    mod_me = lax.dynamic_index_in_dim(mod_all, me, axis=1, keepdims=False).reshape(1, N_DEV * ada_n)
    sh1, sc1, gt1, sh2, sc2, gt2 = [mod_me[:, k * D:(k + 1) * D] for k in range(6)]

    h1, (in_g,) = _prenorm(x2, g_pre_mix, sc1, sh1, name="prenorm_mix", comm=[("gather", w_in[0].astype(BF16))])
    w_in_p = _permute_w_in(_unshard_cols(in_g), lay)
    z, (uq_g, ukv_g, o_g) = _matmul(h1, w_in_p, mode="nn", out_dtype=BF16, name="mm_in",
                                    comm=[("gather", w_uq[0].astype(BF16)), ("gather", w_ukv[0].astype(BF16)),
                                          ("gather", w_o[0].astype(BF16))])
    w_uq_p = jnp.pad(_unshard_cols(uq_g).reshape(Q_RANK, H_MLA, MLA_QK), ((0, 0), (0, 0), (0, MLA_QK_PAD - MLA_QK))
                     ).reshape(Q_RANK, H_MLA * MLA_QK_PAD)
    w_ukv_f = _unshard_cols(ukv_g)
    w_o_f = o_g.reshape(D, D)
    cqn = _prenorm(z, g_q_lat, None, None, name="norm_cq", off=zoff["cq"], width=Q_RANK)
    ckvn = _prenorm(z, g_kv_lat, None, None, name="norm_ckv", off=zoff["ckv"], width=KV_RANK)
    q_raw = _matmul(cqn, w_uq_p, mode="nn", out_dtype=BF16, name="mm_uq")
    kv = _matmul(ckvn, w_ukv_f, mode="nn", out_dtype=BF16, name="mm_ukv")
    tab_k = _rope_tables(S, LANES, 0)
    krr = _rope(z, tab_k, heads=1, width=LANES, transpose=False, name="rope_k", off=zoff["kr"])
    (o_a, lse, Qr), (up_g,) = _flash_fwd(q_raw, kv, krr, tab_k, heads=H_MLA, name="mla_fwd",
                                        comm=[("gather", w_up[0].astype(BF16))])
    w_up_f = _unshard_cols(up_g)

    bucket, valid = _t5_bucket_table()
    onehot = (jnp.asarray(bucket).reshape(-1, 1) == jnp.arange(LANES)[None, :]).astype(F32)
    rb_pad = jnp.pad(rel_bias, ((0, LANES - REL_BUCKETS), (0, LANES - H_SWA)))
    bias_t = _matmul(onehot, rb_pad, mode="nn", out_dtype=F32, name="bias_table", tm=2048, precision=HIGHEST)
    bias_full = jnp.transpose(bias_t[:, :H_SWA].reshape(BLOCK, 2 * BLOCK, H_SWA), (2, 0, 1))
    bias_full = jnp.where(jnp.asarray(valid)[None], bias_full, NEG)
    bias_full = jnp.transpose(bias_full.reshape(SWA_KVH, G_SWA, BLOCK, 2 * BLOCK), (0, 3, 1, 2)
                              ).reshape(SWA_KVH, 2 * BLOCK, G_SWA * BLOCK)
    sink_rows = jnp.broadcast_to(sinks.reshape(SWA_KVH, G_SWA, 1), (SWA_KVH, G_SWA, BLOCK)
                                 ).reshape(SWA_KVH, 1, G_SWA * BLOCK)
    kvw = SWA_KVH * SWA_HD

    def heads_first(t, n):
        return jnp.transpose(t.reshape(S, n, SWA_HD), (1, 0, 2))

    def heads_last(t):
        return jnp.transpose(t, (1, 0, 2)).reshape(S, t.shape[0] * SWA_HD)


    ks_h = heads_first(z[:, zoff["ks"]:zoff["ks"] + kvw], SWA_KVH)
    vs_h = heads_first(z[:, zoff["vs"]:zoff["vs"] + kvw], SWA_KVH)
    o_b, _ = _swa_fwd(z, zoff["qs"], ks_h, vs_h, bias_full, sink_rows, name="swa_fwd")

    mixin = _gate_mix(z, o_a, o_b, D=D, off_a=zoff["ga"], off_b=zoff["gb"], name="gate_mix")
    mix = _matmul(mixin, w_o_f, mode="nn", out_dtype=F32, name="mm_o")
    x1 = _postnorm_res(x2, mix, gt1, g_post_mix, name="postnorm_mix")

    h2 = _prenorm(x1, g_pre_ffn, sc2, sh2, name="prenorm_ffn")
    up, (down_g,) = _matmul(h2, w_up_f, mode="nn", out_dtype=BF16, name="mm_up",
                            comm=[("gather", w_down[0].astype(BF16))])
    w_down_f = down_g.reshape(F, D)
    act = _conv_gate(up, cw_full, conv_b, name="conv_gate")
    y = _matmul(act, w_down_f, mode="nn", out_dtype=F32, name="mm_down")
    loss_part, dout, dy, dgt2, dg_post_ffn = _final_loss(x1, y, tgt, gt2, g_post_ffn, name="final_loss")
    loss = lax.psum(loss_part[0, 0], ("x", "y", "c"))

    dw_down = _matmul(act, dy, mode="tn", out_dtype=BF16, name="mm_down_dw")
    dact = _matmul(dy, w_down_f, mode="nt", out_dtype=BF16, name="mm_down_dx")
    (dup, dcw, dcb), (got_down,) = _conv_gate_bwd(up, dact, cw_full, conv_b, name="conv_gate_bwd",
                                                  comm=[("scatter", dw_down.reshape(N_DEV, F // N_DEV, D))])
    dcw = jnp.transpose(dcw, (1, 0, 2)).reshape(CONV_WIDTH, F2)
    dcb = dcb.reshape(1, F2)
    dw_up = _matmul(h2, dup, mode="tn", out_dtype=BF16, name="mm_up_dw", shard_out=True, halves=True)
    dh2 = _matmul(dup, w_up_f, mode="nt", out_dtype=F32, name="mm_up_dx", halves=True)
    dx1, dg_pre_ffn, dsc2, dsh2 = _prenorm_bwd(x1, dh2, dout, g_pre_ffn, sc2, name="prenorm_ffn_bwd", out_dtype=F32)

    dmix, dgt1, dg_post_mix = _postnorm_bwd(dx1, mix, gt1, g_post_mix, name="postnorm_mix_bwd")
    dw_o = _matmul(mixin, dmix, mode="tn", out_dtype=BF16, name="mm_o_dw")
    dmixin = _matmul(dmix, w_o_f, mode="nt", out_dtype=BF16, name="mm_o_dx")
    dga, dgb, do_a, do_b = _gate_mix_bwd(dmixin, z, o_a, o_b, D=D, off_a=zoff["ga"], off_b=zoff["gb"],
                                         name="gate_mix_bwd")
    (dq_raw, dkv, dkr_parts), (got_up,) = _flash_bwd(Qr, kv, krr, do_a, o_a, lse, tab_k, heads=H_MLA, name="mla_bwd",
                                                     comm=[("scatter", dw_up)])
    dkr = _shared_rope_grad(dkr_parts, tab_k, name="rope_k_bwd")
    dw_uq_p = _matmul(cqn, dq_raw, mode="tn", out_dtype=BF16, name="mm_uq_dw")
    dcqn = _matmul(dq_raw, w_uq_p, mode="nt", out_dtype=F32, name="mm_uq_dx")
    dw_ukv = _matmul(ckvn, dkv, mode="tn", out_dtype=BF16, name="mm_ukv_dw", shard_out=True)
    dckvn = _matmul(dkv, w_ukv_f, mode="nt", out_dtype=F32, name="mm_ukv_dx")
    dw_uq = dw_uq_p.reshape(Q_RANK, H_MLA, MLA_QK_PAD)[:, :, :MLA_QK].reshape(Q_RANK, H_MLA * MLA_QK)

    dcw_parts = jnp.pad(_shard_cols(dcw), ((0, 0), (0, 16 - CONV_WIDTH), (0, 0)))
    (dqs, dks_h, dvs_h, dbias, dsink), (got_o, got_cw, got_uq, got_ukv) = _swa_bwd(
        z, zoff["qs"], ks_h, vs_h, bias_full, sink_rows, do_b, name="swa_bwd",
        comm=[("scatter", dw_o.reshape(N_DEV, D // N_DEV, D)), ("scatter", dcw_parts),
              ("scatter", _shard_cols(dw_uq)), ("scatter", dw_ukv)])
    dbias = jnp.transpose(dbias.reshape(SWA_KVH, 2 * BLOCK, G_SWA, BLOCK), (0, 2, 3, 1))
    drel_t = _matmul(dbias.reshape(H_SWA, BLOCK * 2 * BLOCK), onehot, mode="nn", out_dtype=F32, name="bias_grad",
                     tk=4096, precision=HIGHEST)
    d_rel_bias = jnp.transpose(drel_t[:, :REL_BUCKETS])
    d_sinks = jnp.sum(dsink.reshape(SWA_KVH, G_SWA, BLOCK), axis=-1).reshape(1, H_SWA)

    dcq, dg_q = _prenorm_bwd(z, dcqn, None, g_q_lat, None, name="norm_cq_bwd", out_dtype=BF16,
                             off=zoff["cq"], width=Q_RANK)
    dckv, dg_kv = _prenorm_bwd(z, dckvn, None, g_kv_lat, None, name="norm_ckv_bwd", out_dtype=BF16,
                               off=zoff["ckv"], width=KV_RANK)
    dz = _assemble_dz({"ga": dga, "gb": dgb, "qs": dqs, "cq": dcq, "ckv": dckv,
                       "ks": heads_last(dks_h), "vs": heads_last(dvs_h), "kr": dkr}, lay, S)
    dw_in_a = _matmul(h1, dz, mode="tn", out_dtype=BF16, name="mm_in_dw_a", m_range=(0, D // 2))
    dh1, (got_in_a,) = _matmul(dz, w_in_p, mode="nt", out_dtype=F32, name="mm_in_dx",
                               comm=[("scatter", _shard_cols(_unpermute_w_in(dw_in_a, lay)))])
    dw_in_b = _matmul(h1, dz, mode="tn", out_dtype=BF16, name="mm_in_dw_b", m_range=(D // 2, D // 2))
    late = _shard_cols(_unpermute_w_in(dw_in_b, lay))
    late = [("scatter", late[:, k * (D // 8):(k + 1) * (D // 8)]) for k in range(4)]
    (grad_x, dg_pre_mix, dsc1, dsh1), (got_in_b0,) = _prenorm_bwd(x2, dh1, dx1, g_pre_mix, sc1, name="prenorm_mix_bwd",
                                                                  out_dtype=F32, comm=late[0:1])
    dmod = jnp.concatenate([dsh1, dsc1, dgt1, dsh2, dsc2, dgt2], axis=1)

    small_names = ["b_ada", "g_pre_mix", "g_post_mix", "g_q_lat", "g_kv_lat", "rel_bias", "sinks", "g_pre_ffn",
                   "g_post_ffn", "conv_b"]
    small_w = [b_ada, g_pre_mix, g_post_mix, g_q_lat, g_kv_lat, rel_bias, sinks, g_pre_ffn, g_post_ffn, conv_b]
    small_m = [m_b_ada, m_g_pre_mix, m_g_post_mix, m_g_q_lat, m_g_kv_lat, m_rel_bias, m_sinks, m_g_pre_ffn,
               m_g_post_ffn, m_conv_b]
    small_v = [v_b_ada, v_g_pre_mix, v_g_post_mix, v_g_q_lat, v_g_kv_lat, v_rel_bias, v_sinks, v_g_pre_ffn,
               v_g_post_ffn, v_conv_b]
    small_g = [dmod, dg_pre_mix, dg_post_mix, dg_q, dg_kv, d_rel_bias, d_sinks, dg_pre_ffn, dg_post_ffn, dcb]
    n_small = sum(int(np.prod(w.shape)) for w in small_w)
    rows = _round_up(-(-n_small // LANES), 16)
    parts_small = _all_gather(_pack(small_g, rows), name="ag_small_grads", in_vmem=True)
    sg, sd, sm, sv = _adamw(_pack(small_w, rows), _pack(small_m, rows), _pack(small_v, rows), parts_small,
                            name="adamw_small")

    def unpack(packed):
        flat, out, at = packed.reshape(-1), {}, 0
        for nm, w in zip(small_names, small_w):
            n = int(np.prod(w.shape))
            out[nm] = flat[at:at + n].reshape(w.shape)
            at += n
        return out

    small_out = [unpack(t) for t in (sg, sd, sm, sv)]

    dmod_all = parts_small.reshape(N_DEV, rows * LANES)[:, :6 * D]
    dmod_cols = lax.dynamic_slice_in_dim(dmod_all, me * ada_n, ada_n, axis=1)
    ada_out, (got_in_b1,) = _adamw_ada(w_ada[0], m_w_ada[0], v_w_ada[0], jnp.transpose(c_act), dmod_cols,
                                       name="adamw_w_ada", comm=late[1:2])

    def owner_update(got, w, m, v, name, comm=()):
        shp = w.shape
        w2, m2, v2 = (t.reshape(shp[-2], shp[-1]) for t in (w, m, v))
        res = _adamw(w2, m2, v2, got, name="adamw_" + name, comm=comm)
        upd, moved = res if comm else (res, [])
        return [t.reshape(shp) for t in upd], moved

    def pad_rows(t):
        return jnp.pad(t[0], ((0, 16 - CONV_WIDTH), (0, 0)))

    big = {}
    big["w_up"], (got_in_b2,) = owner_update(got_up, w_up, m_w_up, v_w_up, "w_up", comm=late[2:3])
    big["w_down"], (got_in_b3,) = owner_update(got_down, w_down, m_w_down, v_w_down, "w_down", comm=late[3:4])
    big["w_o"], _ = owner_update(got_o, w_o, m_w_o, v_w_o, "w_o")
    big["w_uq"], _ = owner_update(got_uq, w_uq, m_w_uq, v_w_uq, "w_uq")
    big["w_ukv"], _ = owner_update(got_ukv, w_ukv, m_w_ukv, v_w_ukv, "w_ukv")
    big["w_in"], _ = owner_update([got_in_a, got_in_b0, got_in_b1, got_in_b2, got_in_b3], w_in, m_w_in, v_w_in, "w_in")
    cw_upd = _adamw(pad_rows(conv_w), pad_rows(m_conv_w), pad_rows(v_conv_w), got_cw, name="adamw_conv_w")
    big["conv_w"] = [t[:CONV_WIDTH].reshape(conv_w.shape) for t in cw_upd]
    big["w_ada"] = [t.reshape(w_ada.shape) for t in ada_out]

    order = ["w_ada", "b_ada", "g_pre_mix", "g_post_mix", "w_in", "g_q_lat", "w_uq", "g_kv_lat", "w_ukv", "rel_bias",
             "sinks", "w_o", "g_pre_ffn", "g_post_ffn", "w_up", "conv_w", "conv_b", "w_down"]
    outs = [loss, grad_x.reshape(x.shape)]
    for kind in range(4):
        for nm in order:
            outs.append(big[nm][kind] if nm in big else small_out[kind][nm])
    return tuple(outs)
```

```python
import functools
import itertools
import math

import numpy as np

import jax
import jax.numpy as jnp
from jax import lax
from jax.experimental import pallas as pl
from jax.experimental.pallas import tpu as pltpu

F32 = jnp.float32
BF16 = jnp.bfloat16

N_DEV = 8
MLA_NOPE = 128
MLA_ROPE = 64
MLA_V = 128
MLA_QK = MLA_NOPE + MLA_ROPE
MLA_QK_PAD = 256
ROPE_HALF = MLA_ROPE // 2
ROPE_THETA = 10000.0
SWA_HD = 64
SWA_KVH = 4
WINDOW = 128
BLOCK = 128
REL_BUCKETS = 32
REL_MAX_DIST = 128
CONV_WIDTH = 3
EPS = 1e-6
NEG = -1e30
ADAM_LR = 0.001
ADAM_B1 = 0.9
ADAM_B2 = 0.999
ADAM_EPS = 1e-08
ADAM_WD = 0.01
ADAM_STEP = 10
LANES = 128
HALO = 16
MESH = pl.DeviceIdType.MESH
HIGHEST = lax.Precision.HIGHEST

NN = (((1,), (0,)), ((), ()))
NT = (((1,), (1,)), ((), ()))
TN = (((0,), (0,)), ((), ()))


def _tile(n, pref, align=LANES):
    if n <= pref:
        return n
    t = (pref // align) * align
    while t >= align:
        if n % t == 0:
            return t
        t -= align
    return n


def _round_up(n, m):
    return (n + m - 1) // m * m


def _params(*sem):
    return pltpu.CompilerParams(dimension_semantics=sem)


def _sigmoid(x):
    return 1.0 / (1.0 + jnp.exp(-x))


def _my_place():
    return lax.axis_index("x"), lax.axis_index("y"), lax.axis_index("c")


def _all_gather(x, *, name, in_vmem):
    space = pltpu.VMEM if in_vmem else pl.ANY

    def body(x_ref, out_ref, send_sems, recv_sems, local_sem):
        x_, y_, c_ = _my_place()
        me, sibling = (x_, y_, c_), (x_, y_, 1 - c_)
        chips = [(1 - x_, y_), (x_, 1 - y_), (1 - x_, 1 - y_)]

        def slot(px, py, pc):
            return out_ref.at[4 * px + 2 * py + pc]

        def copy(k, block, to, src=None):
            return pltpu.make_async_remote_copy(
                src_ref=slot(*block) if src is None else src,
                dst_ref=slot(*block),
                send_sem=send_sems.at[k],
                recv_sem=recv_sems.at[k],
                device_id=to,
                device_id_type=MESH,
            )

        mine = pltpu.make_async_copy(x_ref, slot(*me), local_sem)
        mine.start()
        first = [copy(0, me, sibling, src=x_ref)]
        first += [copy(1 + j, me, (*chip, c_), src=x_ref) for j, chip in enumerate(chips)]
        for cp in first:
            cp.start()
        passed = [copy(4 + j, (*chip, c_), sibling) for j, chip in enumerate(chips)]
        for j, chip in enumerate(chips):
            copy(1 + j, (*chip, c_), me).wait_recv()
            passed[j].start()
        copy(0, sibling, me).wait_recv()
        for j, chip in enumerate(chips):
            copy(4 + j, (*chip, 1 - c_), me).wait_recv()
        for cp in first + passed:
            cp.wait_send()
        mine.wait()

    return pl.pallas_call(
        body,
        name=name,
        out_shape=jax.ShapeDtypeStruct((N_DEV,) + x.shape, x.dtype),
        in_specs=[pl.BlockSpec(memory_space=space)],
        out_specs=pl.BlockSpec(memory_space=space),
        scratch_shapes=[
            pltpu.SemaphoreType.DMA((7,)),
            pltpu.SemaphoreType.DMA((7,)),
            pltpu.SemaphoreType.DMA,
        ],
    )(x)


class _Exchange:
    def __init__(self, kind, x_ref, out_ref, send_sems, recv_sems, local_sems, t):
        x_, y_, c_ = _my_place()
        me = 4 * x_ + 2 * y_ + c_

        def pair(k, src, dst, to):
            return pltpu.make_async_remote_copy(src_ref=src, dst_ref=dst, send_sem=send_sems.at[7 * t + k],
                                                recv_sem=recv_sems.at[7 * t + k], device_id=to, device_id_type=MESH)

        none = lambda: []
        if kind == "scatter":
            peers = [(x_ ^ ((r >> 2) & 1), y_ ^ ((r >> 1) & 1), c_ ^ (r & 1)) for r in range(1, N_DEV)]
            self.at_start = lambda: [pair(k, x_ref.at[4 * px + 2 * py + pc], out_ref.at[me], (px, py, pc))
                                     for k, (px, py, pc) in enumerate(peers)]
            self.relay_after, self.at_relay, self.late_after, self.late = none, none, none, none
            self.arrivals = self.at_start
            self.own = lambda: pltpu.make_async_copy(x_ref.at[me], out_ref.at[me], local_sems.at[t])
        else:
            sibling = (x_, y_, 1 - c_)
            xn, yn, diag = (1 - x_, y_), (x_, 1 - y_), (1 - x_, 1 - y_)
            source = (x_ ^ (1 - c_), y_ ^ c_)
            target = (x_ ^ c_, y_ ^ (1 - c_))

            def slot(px, py, pc):
                return out_ref.at[4 * px + 2 * py + pc]

            def landed(k, chip, core):
                return pair(k, slot(*chip, core), slot(*chip, core), sibling)

            mine = slot(x_, y_, c_)
            self.at_start = lambda: [pair(0, x_ref, mine, sibling), pair(1, x_ref, mine, (*xn, c_)),
                                     pair(2, x_ref, mine, (*yn, c_))]
            self.relay_after = lambda: [landed(1, xn, c_), landed(2, yn, c_)]
            self.at_relay = lambda: [pair(3, slot(*source, c_), slot(*source, c_), (*target, c_)),
                                     pair(4, slot(*xn, c_), slot(*xn, c_), sibling),
                                     pair(5, slot(*yn, c_), slot(*yn, c_), sibling)]
            self.late_after = lambda: [landed(3, diag, c_)]
            self.late = lambda: [pair(6, slot(*diag, c_), slot(*diag, c_), sibling)]
            self.arrivals = lambda: [landed(0, (x_, y_), 1 - c_), landed(4, xn, 1 - c_), landed(5, yn, 1 - c_),
                                     landed(6, diag, 1 - c_)]
            self.own = lambda: pltpu.make_async_copy(x_ref, mine, local_sems.at[t])

    def start(self):
        self.own().start()
        for cp in self.at_start():
            cp.start()

    def relay(self):
        for cp in self.relay_after():
            cp.wait_recv()
        for cp in self.at_relay():
            cp.start()

    def finish(self):
        for cp in self.late_after():
            cp.wait_recv()
        for cp in self.late():
            cp.start()
        for cp in self.arrivals():
            cp.wait_recv()
        for cp in self.at_start() + self.at_relay() + self.late():
            cp.wait_send()
        self.own().wait()


RELAY_AT = 0.85


def _call(body, *, name, grid, in_specs, out_specs, out_shape, args, scratch_shapes=(), sem=(), comm=(), prefetch=()):
    n_pf = len(prefetch)

    def launch(fn, ins, outs, shapes, scratch, semantics, operands):
        spec = pltpu.PrefetchScalarGridSpec(num_scalar_prefetch=n_pf, grid=grid, in_specs=ins, out_specs=outs,
                                            scratch_shapes=scratch)
        return pl.pallas_call(fn, name=name, grid_spec=spec, out_shape=shapes,
                              compiler_params=_params(*semantics))(*prefetch, *operands)

    if not comm:
        return list(launch(body, list(in_specs), list(out_specs), list(out_shape), list(scratch_shapes), sem, args)), []
    n_in, n_out, n_c, n_s = len(in_specs), len(out_specs), len(comm), len(scratch_shapes)
    kinds = [kind for kind, _ in comm]
    hbm = pl.BlockSpec(memory_space=pl.ANY)

    def wrapped(*refs):
        tables, refs = refs[:n_pf], refs[n_pf:]
        ins, cin = refs[:n_in], refs[n_in:n_in + n_c]
        at = n_in + n_c
        outs, cout = refs[at:at + n_out], refs[at + n_out:at + n_out + n_c]
        scr = refs[at + n_out + n_c:at + n_out + n_c + n_s]
        send, recv, local = refs[-3:]
        step = 0
        for a, g in enumerate(grid):
            step = step * g + pl.program_id(a)
        n_steps = int(np.prod(grid))

        def exchanges():
            return [_Exchange(kinds[t], cin[t], cout[t], send, recv, local, t) for t in range(n_c)]

        @pl.when(step == 0)
        def _():
            for ex in exchanges():
                ex.start()

        body(*tables, *ins, *outs, *scr)

        @pl.when(step == min(int(RELAY_AT * n_steps), n_steps - 1))
        def _():
            for ex in exchanges():
                ex.relay()

        @pl.when(step == n_steps - 1)
        def _():
            for ex in exchanges():
                ex.finish()

    c_shapes = [jax.ShapeDtypeStruct(((N_DEV,) + a.shape) if kind == "gather" else a.shape, a.dtype)
                for kind, a in comm]
    sems = [pltpu.SemaphoreType.DMA((7 * n_c,)), pltpu.SemaphoreType.DMA((7 * n_c,)), pltpu.SemaphoreType.DMA((n_c,))]
    res = launch(wrapped, list(in_specs) + [hbm] * n_c, list(out_specs) + [hbm] * n_c, list(out_shape) + c_shapes,
                 list(scratch_shapes) + sems, ["arbitrary"] * len(grid), (*args, *[a for _, a in comm]))
    return list(res[:n_out]), list(res[n_out:])


def _matmul(a, b, *, mode, out_dtype, name, tm=1024, tn=1024, tk=2816, precision=None, comm=(), shard_out=False,
            halves=False, m_range=None):
    if mode == "nn":
        (M, K), (K2, N) = a.shape, b.shape
    elif mode == "nt":
        (M, K), (N, K2) = (a.shape[1], 2 * a.shape[2]) if halves else a.shape, b.shape
    else:
        (K, M), (K2, N) = a.shape, (b.shape[1], 2 * b.shape[2]) if halves else b.shape
    assert K == K2, (a.shape, b.shape, mode)
    m_off = 0
    if m_range is not None:
        m_off, M = m_range
    tm = _tile(M, tm, LANES if mode == "tn" else 16)
    tk = _tile(K // 2 if halves and mode == "nt" else K, tk)
    tn = _tile(N // N_DEV, max(tn, 1408)) if shard_out else _tile(N // 2 if halves and mode == "tn" else N, tn)
    nk = K // tk
    m_off //= tm
    if mode == "tn":
        a_spec = pl.BlockSpec((tk, tm), lambda i, j, k: (k, i + m_off))
    elif halves:
        a_spec = pl.BlockSpec((None, tm, tk), lambda i, j, k: (k // (nk // 2), i, k % (nk // 2)))
    else:
        a_spec = pl.BlockSpec((tm, tk), lambda i, j, k: (i, k))
    if mode == "nt":
        b_spec = pl.BlockSpec((tn, tk), lambda i, j, k: (j, k))
    elif halves:
        nj = N // tn
        b_spec = pl.BlockSpec((None, tk, tn), lambda i, j, k: (j // (nj // 2), k, j % (nj // 2)))
    else:
        b_spec = pl.BlockSpec((tk, tn), lambda i, j, k: (k, j))
    dn = {"nn": NN, "nt": NT, "tn": TN}[mode]
    if shard_out:
        per = N // N_DEV // tn
        o_spec = pl.BlockSpec((None, tm, tn), lambda i, j, k: (j // per, i, j % per))
        o_shape = jax.ShapeDtypeStruct((N_DEV, M, N // N_DEV), out_dtype)
    else:
        o_spec = pl.BlockSpec((tm, tn), lambda i, j, k: (i, j))
        o_shape = jax.ShapeDtypeStruct((M, N), out_dtype)

    def product(a_ref, b_ref):
        return lax.dot_general(a_ref[...], b_ref[...], dn, preferred_element_type=F32, precision=precision)

    def body_one(a_ref, b_ref, o_ref):
        o_ref[...] = product(a_ref, b_ref).astype(o_ref.dtype)

    def body_acc(a_ref, b_ref, o_ref, acc_ref):
        k = pl.program_id(2)

        @pl.when(k == 0)
        def _():
            acc_ref[...] = product(a_ref, b_ref)

        @pl.when(k > 0)
        def _():
            acc_ref[...] += product(a_ref, b_ref)

        @pl.when(k == nk - 1)
        def _():
            o_ref[...] = acc_ref[...].astype(o_ref.dtype)

    outs, moved = _call(
        body_one if nk == 1 else body_acc,
        name=name,
        grid=(M // tm, N // tn, nk),
        in_specs=[a_spec, b_spec],
        out_specs=[o_spec],
        out_shape=[o_shape],
        scratch_shapes=[] if nk == 1 else [pltpu.VMEM((tm, tn), F32)],
        sem=("parallel", "parallel", "arbitrary"),
        args=(a, b),
        comm=comm,
    )
    return (outs[0], moved) if comm else outs[0]


def _rstd(xf):
    return lax.rsqrt(jnp.mean(xf * xf, axis=-1, keepdims=True) + EPS)


def _col_view(width, off):
    assert off % width == 0
    return off // width


def _prenorm(x, g, sc, sh, *, name, off=0, width=None, comm=()):
    S = x.shape[0]
    W = x.shape[1] if width is None else width
    cb = _col_view(W, off)
    tr = _tile(S, 512, 16)
    mod = sc is not None
    vec = pl.BlockSpec((1, W), lambda i: (0, 0))

    def body(*refs):
        if mod:
            x_ref, g_ref, sc_ref, sh_ref, o_ref = refs
        else:
            x_ref, g_ref, o_ref = refs
        xf = x_ref[...].astype(F32)
        y = xf * _rstd(xf) * g_ref[...]
        if mod:
            y = y * (1.0 + sc_ref[...]) + sh_ref[...]
        o_ref[...] = y.astype(o_ref.dtype)

    args = (x, g, sc, sh) if mod else (x, g)
    outs, moved = _call(
        body,
        name=name,
        grid=(S // tr,),
        in_specs=[pl.BlockSpec((tr, W), lambda i: (i, cb))] + [vec] * (len(args) - 1),
        out_specs=[pl.BlockSpec((tr, W), lambda i: (i, 0))],
        out_shape=[jax.ShapeDtypeStruct((S, W), BF16)],
        sem=("parallel",),
        args=args,
        comm=comm,
    )
    return (outs[0], moved) if comm else outs[0]


def _prenorm_bwd(x, dh, dres, g, sc, *, name, out_dtype, off=0, width=None):
    S = x.shape[0]
    W = x.shape[1] if width is None else width
    cb = _col_view(W, off)
    tr = _tile(S, 256, 16)
    mod = sc is not None
    res = dres is not None
    vec = pl.BlockSpec((1, W), lambda i: (0, 0))
    row = pl.BlockSpec((tr, W), lambda i: (i, 0))

    def body(*refs):
        it = iter(refs)
        x_ref, dh_ref = next(it), next(it)
        dres_ref = next(it) if res else None
        g_ref = next(it)
        sc_ref = next(it) if mod else None
        dx_ref, dg_ref = next(it), next(it)
        dsc_ref, dsh_ref = (next(it), next(it)) if mod else (None, None)
        i = pl.program_id(0)

        @pl.when(i == 0)
        def _():
            dg_ref[...] = jnp.zeros_like(dg_ref)
            if mod:
                dsc_ref[...] = jnp.zeros_like(dsc_ref)
                dsh_ref[...] = jnp.zeros_like(dsh_ref)

        xf = x_ref[...].astype(F32)
        r = _rstd(xf)
        xn = xf * r
        dhf = dh_ref[...].astype(F32)
        gv = g_ref[...]
        if mod:
            one_sc = 1.0 + sc_ref[...]
            dsh_ref[...] += jnp.sum(dhf, axis=0, keepdims=True)
            dsc_ref[...] += jnp.sum(dhf * (xn * gv), axis=0, keepdims=True)
            dg_ref[...] += jnp.sum(dhf * xn * one_sc, axis=0, keepdims=True)
            dxn = dhf * (gv * one_sc)
        else:
            dg_ref[...] += jnp.sum(dhf * xn, axis=0, keepdims=True)
            dxn = dhf * gv
        dx = r * (dxn - xn * jnp.mean(dxn * xn, axis=-1, keepdims=True))
        if res:
            dx = dx + dres_ref[...]
        dx_ref[...] = dx.astype(dx_ref.dtype)

    args = [x, dh] + ([dres] if res else []) + [g] + ([sc] if mod else [])
    in_specs = [pl.BlockSpec((tr, W), lambda i: (i, cb)), row] + ([row] if res else []) + [vec] + ([vec] if mod else [])
    n_vec = 3 if mod else 1
    outs = pl.pallas_call(
        body,
        name=name,
        grid=(S // tr,),
        in_specs=in_specs,
        out_specs=[row] + [vec] * n_vec,
        out_shape=[jax.ShapeDtypeStruct((S, W), out_dtype)] + [jax.ShapeDtypeStruct((1, W), F32)] * n_vec,
        compiler_params=_params("arbitrary"),
    )(*args)
    return outs


def _postnorm_res(x, y, gt, g, *, name):
    S, D = x.shape
    tr = _tile(S, 512, 8)
    row = pl.BlockSpec((tr, D), lambda i: (i, 0))
    vec = pl.BlockSpec((1, D), lambda i: (0, 0))

    def body(x_ref, y_ref, gt_ref, g_ref, o_ref):
        yf = y_ref[...]
        o_ref[...] = x_ref[...] + gt_ref[...] * (yf * _rstd(yf) * g_ref[...])

    return pl.pallas_call(
        body,
        name=name,
        grid=(S // tr,),
        in_specs=[row, row, vec, vec],
        out_specs=row,
        out_shape=jax.ShapeDtypeStruct((S, D), F32),
        compiler_params=_params("parallel"),
    )(x, y, gt, g)


def _postnorm_bwd(dx1, y, gt, g, *, name):
    S, D = y.shape
    tr = _tile(S, 256, 16)
    row = pl.BlockSpec((tr, D), lambda i: (i, 0))
    vec = pl.BlockSpec((1, D), lambda i: (0, 0))

    def body(dx_ref, y_ref, gt_ref, g_ref, dy_ref, dgt_ref, dg_ref):
        @pl.when(pl.program_id(0) == 0)
        def _():
            dgt_ref[...] = jnp.zeros_like(dgt_ref)
            dg_ref[...] = jnp.zeros_like(dg_ref)

        yf = y_ref[...]
        r = _rstd(yf)
        yn = yf * r
        d = dx_ref[...]
        gtv, gv = gt_ref[...], g_ref[...]
        dgt_ref[...] += jnp.sum(d * (yn * gv), axis=0, keepdims=True)
        dg_ref[...] += jnp.sum(d * gtv * yn, axis=0, keepdims=True)
        dyn = d * (gtv * gv)
        dy_ref[...] = (r * (dyn - yn * jnp.mean(dyn * yn, axis=-1, keepdims=True))).astype(dy_ref.dtype)

    return pl.pallas_call(
        body,
        name=name,
        grid=(S // tr,),
        in_specs=[row, row, vec, vec],
        out_specs=[row, vec, vec],
        out_shape=[jax.ShapeDtypeStruct((S, D), BF16), jax.ShapeDtypeStruct((1, D), F32),
                   jax.ShapeDtypeStruct((1, D), F32)],
        compiler_params=_params("arbitrary"),
    )(dx1, y, gt, g)


def _final_loss(x1, y, target, gt, g, *, name):
    S, D = y.shape
    tr = _tile(S, 256, 16)
    row = pl.BlockSpec((tr, D), lambda i: (i, 0))
    vec = pl.BlockSpec((1, D), lambda i: (0, 0))
    one = pl.BlockSpec((1, LANES), lambda i: (0, 0))

    def body(x_ref, y_ref, t_ref, gt_ref, g_ref, loss_ref, dout_ref, dy_ref, dgt_ref, dg_ref):
        @pl.when(pl.program_id(0) == 0)
        def _():
            loss_ref[...] = jnp.zeros_like(loss_ref)
            dgt_ref[...] = jnp.zeros_like(dgt_ref)
            dg_ref[...] = jnp.zeros_like(dg_ref)

        yf = y_ref[...]
        r = _rstd(yf)
        yn = yf * r
        gtv, gv = gt_ref[...], g_ref[...]
        out = x_ref[...] + gtv * (yn * gv)
        diff = out - t_ref[...]
        per_tok = jnp.mean(diff * diff, axis=-1, keepdims=True)
        loss_ref[...] += 0.5 * jnp.sum(per_tok, axis=0, keepdims=True)
        d = diff / D
        dout_ref[...] = d
        dgt_ref[...] += jnp.sum(d * (yn * gv), axis=0, keepdims=True)
        dg_ref[...] += jnp.sum(d * gtv * yn, axis=0, keepdims=True)
        dyn = d * (gtv * gv)
        dy_ref[...] = (r * (dyn - yn * jnp.mean(dyn * yn, axis=-1, keepdims=True))).astype(dy_ref.dtype)

    return pl.pallas_call(
        body,
        name=name,
        grid=(S // tr,),
        in_specs=[row, row, row, vec, vec],
        out_specs=[one, row, row, vec, vec],
        out_shape=[jax.ShapeDtypeStruct((1, LANES), F32), jax.ShapeDtypeStruct((S, D), F32),
                   jax.ShapeDtypeStruct((S, D), BF16), jax.ShapeDtypeStruct((1, D), F32),
                   jax.ShapeDtypeStruct((1, D), F32)],
        compiler_params=_params("arbitrary"),
    )(x1, y, target, gt, g)


def _ada_fwd(c_all, w_local, b_cols, *, name):
    B, D = c_all.shape
    N = w_local.shape[1]
    tn = _tile(N, 512)

    def body(c_ref, w_ref, b_ref, ca_ref, mod_ref):
        cv = c_ref[...]
        ca = cv * _sigmoid(cv)
        ca_ref[...] = ca
        mod_ref[...] = jnp.dot(ca, w_ref[...], preferred_element_type=F32, precision=HIGHEST) + b_ref[...]

    return pl.pallas_call(
        body,
        name=name,
        grid=(N // tn,),
        in_specs=[pl.BlockSpec((B, D), lambda j: (0, 0)), pl.BlockSpec((D, tn), lambda j: (0, j)),
                  pl.BlockSpec((1, tn), lambda j: (0, j))],
        out_specs=[pl.BlockSpec((B, D), lambda j: (0, 0)), pl.BlockSpec((B, tn), lambda j: (0, j))],
        out_shape=[jax.ShapeDtypeStruct((B, D), F32), jax.ShapeDtypeStruct((B, N), F32)],
        compiler_params=_params("arbitrary"),
    )(c_all, w_local, b_cols)


def _rope_tables(S, width, lane_off):
    pos = jnp.arange(S, dtype=F32)
    inv = ROPE_THETA ** (-jnp.arange(0, MLA_ROPE, 2, dtype=F32) / MLA_ROPE)
    ang = pos[:, None] * inv[None, :]
    ang = jnp.concatenate([ang, ang], axis=-1)
    cos, sin = jnp.cos(ang), jnp.sin(ang)
    first = (jnp.arange(MLA_ROPE) < ROPE_HALF)[None, :]
    sa = jnp.where(first, -sin, 0.0)
    sb = jnp.where(first, 0.0, sin)

    def place(t, fill):
        return jnp.pad(t, ((0, 0), (lane_off, width - lane_off - MLA_ROPE)), constant_values=fill)

    return place(cos, 1.0), place(sa, 0.0), place(sb, 0.0)


def _rope_apply(x, cos, sa, sb, width, transpose):
    if transpose:
        return x * cos + pltpu.roll(x * sa, ROPE_HALF, 1) + pltpu.roll(x * sb, width - ROPE_HALF, 1)
    return x * cos + pltpu.roll(x, width - ROPE_HALF, 1) * sa + pltpu.roll(x, ROPE_HALF, 1) * sb


def _rope(x, tables, *, heads, width, transpose, name, off=0, scale=1.0):
    S = x.shape[0]
    cb = _col_view(width, off)
    tr = _tile(S, 512, 16)
    tab = pl.BlockSpec((tr, width), lambda i, h: (i, 0))

    def body(x_ref, c_ref, sa_ref, sb_ref, o_ref):
        y = _rope_apply(x_ref[...].astype(F32), c_ref[...], sa_ref[...], sb_ref[...], width, transpose)
        o_ref[...] = (y if scale == 1.0 else y * scale).astype(o_ref.dtype)

    return pl.pallas_call(
        body,
        name=name,
        grid=(S // tr, heads),
        in_specs=[pl.BlockSpec((tr, width), lambda i, h: (i, cb + h)), tab, tab, tab],
        out_specs=pl.BlockSpec((tr, width), lambda i, h: (i, h)),
        out_shape=jax.ShapeDtypeStruct((S, heads * width), BF16),
        compiler_params=_params("parallel", "parallel"),
    )(x, *tables)


def _shared_rope_grad(parts, tables, *, name):
    P, S, _ = parts.shape
    tr = _tile(S, 512, 16)
    tab = pl.BlockSpec((tr, LANES), lambda i: (i, 0))

    def body(p_ref, c_ref, sa_ref, sb_ref, o_ref):
        acc = p_ref[0]
        for k in range(1, P):
            acc = acc + p_ref[k]
        o_ref[...] = _rope_apply(acc, c_ref[...], sa_ref[...], sb_ref[...], LANES, True).astype(o_ref.dtype)

    return pl.pallas_call(
        body,
        name=name,
        grid=(S // tr,),
        in_specs=[pl.BlockSpec((P, tr, LANES), lambda i: (0, i, 0)), tab, tab, tab],
        out_specs=tab,
        out_shape=jax.ShapeDtypeStruct((S, LANES), BF16),
        compiler_params=_params("parallel"),
    )(parts, *tables)


MLA_SCALE = MLA_QK ** -0.5
LOG2E = math.log2(math.e)
LN2 = math.log(2.0)
MLA_Q_PRESCALE = MLA_SCALE * LOG2E


def _lane_tile(v, n):
    return v if n == LANES else jnp.tile(v, (1, n // LANES))


def _causal_mask(s):
    rows = lax.broadcasted_iota(jnp.int32, s.shape, 0)
    cols = lax.broadcasted_iota(jnp.int32, s.shape, 1)
    return jnp.where(cols <= rows, s, NEG)


def _tri_blocks(nb, q_major):
    if q_major:
        pairs = [(q, k) for q in range(nb) for k in range(q + 1)]
    else:
        pairs = [(q, k) for k in range(nb) for q in range(k, nb)]
    return (jnp.asarray(np.array([p[0] for p in pairs], np.int32)),
            jnp.asarray(np.array([p[1] for p in pairs], np.int32)))


HEAD_PAIR = 4


def _flash_fwd(q_raw, KV, krr, tables, *, heads, name, comm=()):
    S = q_raw.shape[0]
    t = _tile(S, 512)
    nb = S // t
    qt, kt = _tri_blocks(nb, True)
    qw, vw = HEAD_PAIR * MLA_QK_PAD, HEAD_PAIR * MLA_V

    def body(qt_ref, kt_ref, q_ref, *rest):
        kn_refs, kr_ref, v_refs = rest[:HEAD_PAIR], rest[HEAD_PAIR], rest[HEAD_PAIR + 1:2 * HEAD_PAIR + 1]
        c_ref, sa_ref, sb_ref, o_ref, lse_ref, qr_ref, m_scr, l_scr, acc_scr = rest[2 * HEAD_PAIR + 1:]
        step_id = pl.program_id(1)
        qi, ki = qt_ref[step_id], kt_ref[step_id]

        @pl.when(ki == 0)
        def _():
            m_scr[...] = jnp.full_like(m_scr, NEG)
            l_scr[...] = jnp.zeros_like(l_scr)
            acc_scr[...] = jnp.zeros_like(acc_scr)
            for h in range(HEAD_PAIR):
                base = h * MLA_QK_PAD
                nope = q_ref[:, base:base + MLA_NOPE].astype(F32) * MLA_Q_PRESCALE
                rot = _rope_apply(q_ref[:, base + MLA_NOPE:base + MLA_QK_PAD].astype(F32), c_ref[...], sa_ref[...],
                                  sb_ref[...], LANES, False) * MLA_Q_PRESCALE
                qr_ref[:, base:base + MLA_NOPE] = nope.astype(qr_ref.dtype)
                qr_ref[:, base + MLA_NOPE:base + MLA_QK_PAD] = rot.astype(qr_ref.dtype)

        def step(diagonal):
            for h, (kn_ref, v_ref) in enumerate(zip(kn_refs, v_refs)):
                cols = slice(h * MLA_QK_PAD, (h + 1) * MLA_QK_PAD)
                k = jnp.concatenate([kn_ref[...], kr_ref[...]], axis=1)
                s = lax.dot_general(qr_ref[:, cols], k, NT, preferred_element_type=F32)
                if diagonal:
                    s = _causal_mask(s)
                m_prev = m_scr[h]
                m_new = jnp.maximum(m_prev, jnp.max(s, axis=1, keepdims=True))
                alpha = jnp.exp2(m_prev - m_new)
                p = jnp.exp2(s - _lane_tile(m_new, t))
                l_new = alpha * l_scr[h] + jnp.sum(p, axis=1, keepdims=True)
                acc = alpha * acc_scr[h] + jnp.dot(p.astype(BF16), v_ref[...], preferred_element_type=F32)
                if diagonal:
                    o_ref[:, h * MLA_V:(h + 1) * MLA_V] = (acc / l_new).astype(o_ref.dtype)
                    lse_ref[h] = m_new + jnp.log(l_new) * LOG2E
                else:
                    l_scr[h], acc_scr[h], m_scr[h] = l_new, acc, m_new

        pl.when(ki < qi)(lambda: step(False))
        pl.when(ki == qi)(lambda: step(True))

    def kvspec(h, half):
        return pl.BlockSpec((t, LANES), lambda hp, s, qt, kt: (kt[s], 2 * (HEAD_PAIR * hp + h) + half))

    qtab = pl.BlockSpec((t, LANES), lambda hp, s, qt, kt: (qt[s], 0))
    qrow = lambda hp, s, qt, kt: (qt[s], hp)
    return _call(
        body,
        name=name,
        grid=(heads // HEAD_PAIR, int(qt.shape[0])),
        in_specs=[pl.BlockSpec((t, qw), qrow), *[kvspec(h, 0) for h in range(HEAD_PAIR)],
                  pl.BlockSpec((t, LANES), lambda hp, s, qt, kt: (kt[s], 0)),
                  *[kvspec(h, 1) for h in range(HEAD_PAIR)], qtab, qtab, qtab],
        out_specs=[pl.BlockSpec((t, vw), qrow),
                   pl.BlockSpec((HEAD_PAIR, t, LANES), lambda hp, s, qt, kt: (hp, qt[s], 0)),
                   pl.BlockSpec((t, qw), qrow)],
        out_shape=[jax.ShapeDtypeStruct((S, heads * MLA_V), BF16),
                   jax.ShapeDtypeStruct((heads, S, LANES), F32),
                   jax.ShapeDtypeStruct((S, heads * MLA_QK_PAD), BF16)],
        scratch_shapes=[pltpu.VMEM((HEAD_PAIR, t, LANES), F32), pltpu.VMEM((HEAD_PAIR, t, LANES), F32),
                        pltpu.VMEM((HEAD_PAIR, t, MLA_V), F32)],
        sem=("parallel", "arbitrary"),
        args=(q_raw, *[KV] * HEAD_PAIR, krr, *[KV] * HEAD_PAIR, *tables),
        comm=comm,
        prefetch=(qt, kt),
    )


def _flash_bwd(Q, KV, krr, dO, O, lse, tables, *, heads, name, comm=()):
    S = Q.shape[0]
    t = _tile(S, 512)
    nb = S // t
    qt, kt = _tri_blocks(nb, False)
    n_steps = int(qt.shape[0])
    qw, vw = HEAD_PAIR * MLA_QK_PAD, HEAD_PAIR * MLA_V

    def body(qt_ref, kt_ref, q_ref, *rest):
        kn_refs, kr_ref, v_refs = rest[:HEAD_PAIR], rest[HEAD_PAIR], rest[HEAD_PAIR + 1:2 * HEAD_PAIR + 1]
        (do_ref, o_ref, lse_ref, c_ref, sa_ref, sb_ref, dq_ref, dkv_ref, dkr_ref,
         dq_scr, dk_scr, dv_scr, delta_scr) = rest[2 * HEAD_PAIR + 1:]
        step_id = pl.program_id(1)
        qi, ki = qt_ref[step_id], kt_ref[step_id]

        @pl.when(ki == 0)
        def _():
            for h in range(HEAD_PAIR):
                vc = slice(h * MLA_V, (h + 1) * MLA_V)
                d = jnp.sum(do_ref[:, vc].astype(F32) * o_ref[:, vc].astype(F32), axis=1, keepdims=True)
                delta_scr[h, qi] = jnp.broadcast_to(d, (t, LANES))

        def step(diagonal):
            for h, (kn_ref, v_ref) in enumerate(zip(kn_refs, v_refs)):
                base = h * MLA_QK_PAD
                cols = slice(base, base + MLA_QK_PAD)
                vc = slice(h * MLA_V, (h + 1) * MLA_V)
                q, do = q_ref[:, cols], do_ref[:, vc]
                k = jnp.concatenate([kn_ref[...], kr_ref[...]], axis=1)
                s = lax.dot_general(q, k, NT, preferred_element_type=F32)
                if diagonal:
                    s = _causal_mask(s)
                p = jnp.exp2(s - _lane_tile(lse_ref[h], t))
                dv = lax.dot_general(p.astype(BF16), do, TN, preferred_element_type=F32)
                dp = lax.dot_general(do, v_ref[...], NT, preferred_element_type=F32)
                ds = (p * (dp - _lane_tile(delta_scr[h, qi], t))).astype(BF16)
                dk = lax.dot_general(ds, q, TN, preferred_element_type=F32)
                dq = jnp.dot(ds, k, preferred_element_type=F32)
                if diagonal:
                    dk_scr[h], dv_scr[h] = dk, dv
                    dq = (dq_scr[qi, :, cols] + dq) * (LN2 * MLA_Q_PRESCALE)
                    rot = _rope_apply(dq[:, MLA_NOPE:], c_ref[...], sa_ref[...], sb_ref[...], LANES, True)
                    dq_ref[:, base:base + MLA_NOPE] = dq[:, :MLA_NOPE].astype(dq_ref.dtype)
                    dq_ref[:, base + MLA_NOPE:base + MLA_QK_PAD] = rot.astype(dq_ref.dtype)
                else:
                    dk_scr[h] += dk
                    dv_scr[h] += dv
                    dq_scr[qi, :, cols] += dq

        @pl.when(ki == 0)
        def _():
            dq_scr[qi] = jnp.zeros((t, qw), F32)

        pl.when(qi > ki)(lambda: step(False))
        pl.when(qi == ki)(lambda: step(True))

        @pl.when(qi == nb - 1)
        def _():
            shared = jnp.zeros((t, LANES), F32)
            for h in range(HEAD_PAIR):
                base = h * MLA_QK_PAD
                dk = dk_scr[h] * LN2
                dkv_ref[:, base:base + MLA_NOPE] = dk[:, :MLA_NOPE].astype(dkv_ref.dtype)
                dkv_ref[:, base + MLA_NOPE:base + MLA_QK_PAD] = dv_scr[h].astype(dkv_ref.dtype)
                shared = shared + dk[:, MLA_NOPE:]
            dkr_ref[0] = shared

    def kvspec(h, half):
        return pl.BlockSpec((t, LANES), lambda hp, s, qt, kt: (kt[s], 2 * (HEAD_PAIR * hp + h) + half))

    qrow = lambda hp, s, qt, kt: (qt[s], hp)
    krow = lambda hp, s, qt, kt: (kt[s], hp)
    ktab = pl.BlockSpec((t, LANES), lambda hp, s, qt, kt: (kt[s], 0))
    return _call(
        body,
        name=name,
        grid=(heads // HEAD_PAIR, n_steps),
        in_specs=[pl.BlockSpec((t, qw), qrow), *[kvspec(h, 0) for h in range(HEAD_PAIR)], ktab,
                  *[kvspec(h, 1) for h in range(HEAD_PAIR)],
                  pl.BlockSpec((t, vw), qrow), pl.BlockSpec((t, vw), qrow),
                  pl.BlockSpec((HEAD_PAIR, t, LANES), lambda hp, s, qt, kt: (hp, qt[s], 0)),
                  ktab, ktab, ktab],
        out_specs=[pl.BlockSpec((t, qw), krow), pl.BlockSpec((t, qw), krow),
                   pl.BlockSpec((1, t, LANES), lambda hp, s, qt, kt: (hp, kt[s], 0))],
        out_shape=[jax.ShapeDtypeStruct((S, heads * MLA_QK_PAD), BF16),
                   jax.ShapeDtypeStruct((S, heads * MLA_QK_PAD), BF16),
                   jax.ShapeDtypeStruct((heads // HEAD_PAIR, S, LANES), F32)],
        scratch_shapes=[pltpu.VMEM((nb, t, qw), F32), pltpu.VMEM((HEAD_PAIR, t, MLA_QK_PAD), F32),
                        pltpu.VMEM((HEAD_PAIR, t, MLA_V), F32), pltpu.VMEM((HEAD_PAIR, nb, t, LANES), F32)],
        sem=("parallel", "arbitrary"),
        args=(Q, *[KV] * HEAD_PAIR, krr, *[KV] * HEAD_PAIR, dO, O, lse, *tables),
        comm=comm,
        prefetch=(qt, kt),
    )


SWA_SCALE = SWA_HD ** -0.5


def _t5_bucket_table():
    a = np.arange(BLOCK)[:, None]
    j = np.arange(2 * BLOCK)[None, :]
    dist = BLOCK + a - j
    max_exact = REL_BUCKETS // 2
    n = np.maximum(dist, 0)
    large = max_exact + (np.log(np.maximum(n, 1).astype(np.float32) / np.float32(max_exact))
                         / np.float32(math.log(REL_MAX_DIST / max_exact))
                         * np.float32(REL_BUCKETS - max_exact)).astype(np.int32)
    large = np.minimum(large, REL_BUCKETS - 1)
    bucket = np.where(n < max_exact, n, large)
    valid = (dist >= 0) & (dist < WINDOW)
    return bucket.astype(np.int32), valid


def _heads_to_rows(x, G):
    return jnp.concatenate([x[:, g * SWA_HD:(g + 1) * SWA_HD] for g in range(G)], axis=0)


def _rows_to_heads(o_ref, x, G):
    for g in range(G):
        o_ref[:, g * SWA_HD:(g + 1) * SWA_HD] = x[g * BLOCK:(g + 1) * BLOCK].astype(o_ref.dtype)


def _swa_probs(q_ref, kp_ref, kc_ref, bias_ref, sink_ref, qb, G):
    q2 = _heads_to_rows(q_ref[...], G)
    kb = jnp.concatenate([kp_ref[0], kc_ref[0]], axis=0)
    s = lax.dot_general(kb, q2, NT, preferred_element_type=F32) * SWA_SCALE + bias_ref[0]
    keys = lax.broadcasted_iota(jnp.int32, s.shape, 0)
    s = jnp.where((keys >= BLOCK) | (qb > 0), s, NEG)
    sink = sink_ref[0]
    m = jnp.maximum(jnp.max(s, axis=0, keepdims=True), sink)
    e = jnp.exp(s - m)
    es = jnp.exp(sink - m)
    inv = 1.0 / (jnp.sum(e, axis=0, keepdims=True) + es)
    return q2, kb, e * inv, es * inv


def _swa_fwd(q, q_off, k, v, bias_t, sink, *, name, comm=()):
    S = k.shape[1]
    G = bias_t.shape[2] // BLOCK
    nb = S // BLOCK
    qcol = _col_view(G * SWA_HD, q_off)
    cur = lambda kh, qb: (kh, qb, 0)
    prev = lambda kh, qb: (kh, jnp.maximum(qb - 1, 0), 0)
    kvspec = lambda im: pl.BlockSpec((1, BLOCK, SWA_HD), im)

    def body(q_ref, kc_ref, kp_ref, vc_ref, vp_ref, bias_ref, sink_ref, o_ref):
        qb = pl.program_id(1)
        _, _, pt, _ = _swa_probs(q_ref, kp_ref, kc_ref, bias_ref, sink_ref, qb, G)
        vb = jnp.concatenate([vp_ref[0], vc_ref[0]], axis=0)
        _rows_to_heads(o_ref, lax.dot_general(pt.astype(BF16), vb, TN, preferred_element_type=F32), G)

    outs, moved = _call(
        body,
        name=name,
        grid=(SWA_KVH, nb),
        in_specs=[pl.BlockSpec((BLOCK, G * SWA_HD), lambda kh, qb: (qb, qcol + kh)),
                  kvspec(cur), kvspec(prev), kvspec(cur), kvspec(prev),
                  pl.BlockSpec((1, 2 * BLOCK, G * BLOCK), lambda kh, qb: (kh, 0, 0)),
                  pl.BlockSpec((1, 1, G * BLOCK), lambda kh, qb: (kh, 0, 0))],
        out_specs=[pl.BlockSpec((BLOCK, G * SWA_HD), lambda kh, qb: (qb, kh))],
        out_shape=[jax.ShapeDtypeStruct((S, SWA_KVH * G * SWA_HD), BF16)],
        sem=("parallel", "parallel"),
        args=(q, k, k, v, v, bias_t, sink),
        comm=comm,
    )
    return outs[0], moved


def _swa_bwd(q, q_off, k, v, bias_t, sink, do, *, name, comm=()):
    S = k.shape[1]
    G = bias_t.shape[2] // BLOCK
    nb = S // BLOCK
    qcol = _col_view(G * SWA_HD, q_off)
    cur = lambda kh, qb: (kh, jnp.minimum(qb, nb - 1), 0)
    prev = lambda kh, qb: (kh, jnp.maximum(jnp.minimum(qb, nb - 1) - 1, 0), 0)
    lag = lambda kh, qb: (kh, jnp.maximum(qb - 1, 0), 0)
    kvspec = lambda im: pl.BlockSpec((1, BLOCK, SWA_HD), im)

    def body(q_ref, kc_ref, kp_ref, vc_ref, vp_ref, bias_ref, sink_ref, do_ref,
             dq_ref, dk_ref, dv_ref, dbias_ref, dsink_ref, ck_scr, cv_scr):
        qb = pl.program_id(1)

        @pl.when(qb == 0)
        def _():
            dbias_ref[...] = jnp.zeros_like(dbias_ref)
            dsink_ref[...] = jnp.zeros_like(dsink_ref)
            ck_scr[...] = jnp.zeros_like(ck_scr)
            cv_scr[...] = jnp.zeros_like(cv_scr)

        @pl.when(qb < nb)
        def _():
            q2, kb, pt, ps = _swa_probs(q_ref, kp_ref, kc_ref, bias_ref, sink_ref, qb, G)
            vb = jnp.concatenate([vp_ref[0], vc_ref[0]], axis=0)
            do2 = _heads_to_rows(do_ref[...], G)
            dpt = lax.dot_general(vb, do2, NT, preferred_element_type=F32)
            delta = jnp.sum(dpt * pt, axis=0, keepdims=True)
            dst = pt * (dpt - delta)
            dbias_ref[0] += dst
            dsink_ref[0] += -ps * delta
            dsb = (dst * SWA_SCALE).astype(BF16)
            _rows_to_heads(dq_ref, lax.dot_general(dsb, kb, TN, preferred_element_type=F32), G)
            dkb = jnp.dot(dsb, q2, preferred_element_type=F32)
            dvb = jnp.dot(pt.astype(BF16), do2, preferred_element_type=F32)
            dk_ref[0] = (ck_scr[...] + dkb[:BLOCK]).astype(dk_ref.dtype)
            dv_ref[0] = (cv_scr[...] + dvb[:BLOCK]).astype(dv_ref.dtype)
            ck_scr[...] = dkb[BLOCK:]
            cv_scr[...] = dvb[BLOCK:]

        @pl.when(qb == nb)
        def _():
            dk_ref[0] = ck_scr[...].astype(dk_ref.dtype)
            dv_ref[0] = cv_scr[...].astype(dv_ref.dtype)

    tspec = pl.BlockSpec((BLOCK, G * SWA_HD), lambda kh, qb: (jnp.minimum(qb, nb - 1), kh))
    return _call(
        body,
        name=name,
        grid=(SWA_KVH, nb + 1),
        in_specs=[pl.BlockSpec((BLOCK, G * SWA_HD), lambda kh, qb: (jnp.minimum(qb, nb - 1), qcol + kh)),
                  kvspec(cur), kvspec(prev), kvspec(cur), kvspec(prev),
                  pl.BlockSpec((1, 2 * BLOCK, G * BLOCK), lambda kh, qb: (kh, 0, 0)),
                  pl.BlockSpec((1, 1, G * BLOCK), lambda kh, qb: (kh, 0, 0)),
                  pl.BlockSpec((BLOCK, G * SWA_HD), lambda kh, qb: (jnp.minimum(qb, nb - 1), kh))],
        out_specs=[tspec, kvspec(lag), kvspec(lag),
                   pl.BlockSpec((1, 2 * BLOCK, G * BLOCK), lambda kh, qb: (kh, 0, 0)),
                   pl.BlockSpec((1, 1, G * BLOCK), lambda kh, qb: (kh, 0, 0))],
        out_shape=[jax.ShapeDtypeStruct((S, SWA_KVH * G * SWA_HD), BF16),
                   jax.ShapeDtypeStruct((SWA_KVH, S, SWA_HD), BF16),
                   jax.ShapeDtypeStruct((SWA_KVH, S, SWA_HD), BF16),
                   jax.ShapeDtypeStruct((SWA_KVH, 2 * BLOCK, G * BLOCK), F32),
                   jax.ShapeDtypeStruct((SWA_KVH, 1, G * BLOCK), F32)],
        scratch_shapes=[pltpu.VMEM((BLOCK, SWA_HD), F32), pltpu.VMEM((BLOCK, SWA_HD), F32)],
        sem=("parallel", "arbitrary"),
        args=(q, k, k, v, v, bias_t, sink, do),
        comm=comm,
    )


def _gate_mix(z, o_a, o_b, *, D, off_a, off_b, name):
    S = z.shape[0]
    tr = _tile(S, 256, 16)
    row = pl.BlockSpec((tr, D), lambda i: (i, 0))
    ca, cb = _col_view(D, off_a), _col_view(D, off_b)

    def body(ga_ref, gb_ref, oa_ref, ob_ref, m_ref):
        m = (_sigmoid(ga_ref[...].astype(F32)) * oa_ref[...].astype(F32)
             + _sigmoid(gb_ref[...].astype(F32)) * ob_ref[...].astype(F32))
        m_ref[...] = m.astype(m_ref.dtype)

    return pl.pallas_call(
        body,
        name=name,
        grid=(S // tr,),
        in_specs=[pl.BlockSpec((tr, D), lambda i: (i, ca)), pl.BlockSpec((tr, D), lambda i: (i, cb)), row, row],
        out_specs=row,
        out_shape=jax.ShapeDtypeStruct((S, D), BF16),
        compiler_params=_params("parallel"),
    )(z, z, o_a, o_b)


def _gate_mix_bwd(dm, z, o_a, o_b, *, D, off_a, off_b, name):
    S = z.shape[0]
    tr = _tile(S, 256, 16)
    row = pl.BlockSpec((tr, D), lambda i: (i, 0))
    ca, cb = _col_view(D, off_a), _col_view(D, off_b)

    def body(dm_ref, ga_ref, gb_ref, oa_ref, ob_ref, dga_ref, dgb_ref, doa_ref, dob_ref):
        d = dm_ref[...].astype(F32)
        for g_ref, o_ref, dg_ref, do_ref in ((ga_ref, oa_ref, dga_ref, doa_ref), (gb_ref, ob_ref, dgb_ref, dob_ref)):
            sg = _sigmoid(g_ref[...].astype(F32))
            dg_ref[...] = (d * o_ref[...].astype(F32) * (sg * (1.0 - sg))).astype(dg_ref.dtype)
            do_ref[...] = (d * sg).astype(do_ref.dtype)

    return pl.pallas_call(
        body,
        name=name,
        grid=(S // tr,),
        in_specs=[row, pl.BlockSpec((tr, D), lambda i: (i, ca)), pl.BlockSpec((tr, D), lambda i: (i, cb)), row, row],
        out_specs=[row] * 4,
        out_shape=[jax.ShapeDtypeStruct((S, D), BF16)] * 4,
        compiler_params=_params("parallel"),
    )(dm, z, z, o_a, o_b)


CONV_ROWS = 256
CONV_COLS = 1408
SUBLANES = 8


def _shift_matrices(tr):
    r = np.arange(tr)[:, None]
    c = np.arange(tr)[None, :]
    back = [jnp.asarray(r == c + d, dtype=BF16) for d in (1, 2)]
    ahead = [jnp.asarray(r + d == c, dtype=BF16) for d in (1, 2)]
    return back, ahead


def _rows_before(x, halo_ref, first, b1_ref, b2_ref):
    s1 = jnp.dot(b1_ref[...], x, preferred_element_type=F32)
    s2 = jnp.dot(b2_ref[...], x, preferred_element_type=F32)
    h8 = jnp.where(first, 0.0, halo_ref[...].astype(F32)[HALO - SUBLANES:])
    rows = lax.broadcasted_iota(jnp.int32, h8.shape, 0)
    fix1 = jnp.where(rows < 1, pltpu.roll(h8, 1, 0), 0.0)
    fix2 = jnp.where(rows < 2, pltpu.roll(h8, 2, 0), 0.0)
    s1 = jnp.concatenate([s1[:SUBLANES] + fix1, s1[SUBLANES:]], axis=0)
    s2 = jnp.concatenate([s2[:SUBLANES] + fix2, s2[SUBLANES:]], axis=0)
    return s1, s2


def _conv_taps(x, s1, s2, cw_ref, cb_ref):
    return cb_ref[...] + cw_ref[0:1, :] * s2 + cw_ref[1:2, :] * s1 + cw_ref[2:3, :] * x


def _conv_gate(up, cw, cb, *, name):
    S, F2 = up.shape
    F = F2 // 2
    tr = _tile(S, CONV_ROWS, HALO)
    tc = _tile(F, CONV_COLS)
    nc = F // tc
    hb = tr // HALO
    back, _ = _shift_matrices(tr)
    mat = pl.BlockSpec((tr, tr), lambda i, j: (0, 0))

    def halo_map(shift):
        return lambda i, j: (jnp.maximum(i * hb - 1, 0), j + shift)

    def body(x1_ref, h1_ref, x2_ref, h2_ref, cw1_ref, cw2_ref, cb1_ref, cb2_ref, b1_ref, b2_ref, a_ref):
        first = pl.program_id(0) == 0
        us = []
        for x_ref, h_ref, cw_ref, cb_ref in ((x1_ref, h1_ref, cw1_ref, cb1_ref), (x2_ref, h2_ref, cw2_ref, cb2_ref)):
            x = x_ref[...]
            s1, s2 = _rows_before(x, h_ref, first, b1_ref, b2_ref)
            us.append(_conv_taps(x.astype(F32), s1, s2, cw_ref, cb_ref))
        u1, u2 = us
        a_ref[...] = (u1 * _sigmoid(u1) * u2).astype(a_ref.dtype)

    return pl.pallas_call(
        body,
        name=name,
        grid=(S // tr, nc),
        in_specs=[pl.BlockSpec((tr, tc), lambda i, j: (i, j)), pl.BlockSpec((HALO, tc), halo_map(0)),
                  pl.BlockSpec((tr, tc), lambda i, j: (i, j + nc)), pl.BlockSpec((HALO, tc), halo_map(nc)),
                  pl.BlockSpec((CONV_WIDTH, tc), lambda i, j: (0, j)),
                  pl.BlockSpec((CONV_WIDTH, tc), lambda i, j: (0, j + nc)),
                  pl.BlockSpec((1, tc), lambda i, j: (0, j)), pl.BlockSpec((1, tc), lambda i, j: (0, j + nc)),
                  mat, mat],
        out_specs=pl.BlockSpec((tr, tc), lambda i, j: (i, j)),
        out_shape=jax.ShapeDtypeStruct((S, F), BF16),
        compiler_params=_params("parallel", "parallel"),
    )(up, up, up, up, cw, cw, cb, cb, *back)


def _conv_gate_bwd(up, da, cw, cb, *, name, comm=()):
    S, F2 = up.shape
    F = F2 // 2
    tr = _tile(S, CONV_ROWS, HALO)
    tc = _tile(F, CONV_COLS)
    nc = F // tc
    hb = tr // HALO
    ni = S // tr
    back, ahead = _shift_matrices(tr)
    mat = pl.BlockSpec((tr, tr), lambda j, r: (0, 0))

    def cur(shift):
        return lambda j, r: (ni - 1 - r, j + shift)

    def before(shift):
        return lambda j, r: (jnp.maximum((ni - 1 - r) * hb - 1, 0), j + shift)

    def vec(rows, shift):
        return pl.BlockSpec((rows, tc), lambda j, r: (0, j + shift))

    def body(x1_ref, h1_ref, x2_ref, h2_ref, da_ref, cw1_ref, cw2_ref, cb1_ref, cb2_ref,
             b1_ref, b2_ref, a1_ref, a2_ref, dup_ref, dcw_ref, dcb_ref, next_du):
        r = pl.program_id(1)
        first = r == ni - 1

        @pl.when(r == 0)
        def _():
            dcw_ref[...] = jnp.zeros_like(dcw_ref)
            dcb_ref[...] = jnp.zeros_like(dcb_ref)
            next_du[...] = jnp.zeros_like(next_du)

        x1, x2 = x1_ref[...], x2_ref[...]
        x1f, x2f = x1.astype(F32), x2.astype(F32)
        s11, s12 = _rows_before(x1, h1_ref, first, b1_ref, b2_ref)
        s21, s22 = _rows_before(x2, h2_ref, first, b1_ref, b2_ref)
        u1 = _conv_taps(x1f, s11, s12, cw1_ref, cb1_ref)
        u2 = _conv_taps(x2f, s21, s22, cw2_ref, cb2_ref)
        sg = _sigmoid(u1)
        daf = da_ref[...].astype(F32)
        du1 = daf * u2 * (sg * (1.0 + u1 * (1.0 - sg)))
        du2 = daf * (u1 * sg)
        rows = lax.broadcasted_iota(jnp.int32, (SUBLANES, tc), 0)

        for half, (du, own, own1, own2, cw_ref) in enumerate(((du1, x1f, s11, s12, cw1_ref),
                                                             (du2, x2f, s21, s22, cw2_ref))):
            du_b = du.astype(BF16)
            n1 = jnp.dot(a1_ref[...], du_b, preferred_element_type=F32)
            n2 = jnp.dot(a2_ref[...], du_b, preferred_element_type=F32)
            c8 = next_du[half]
            fix1 = jnp.where(rows >= SUBLANES - 1, pltpu.roll(c8, SUBLANES - 1, 0), 0.0)
            fix2 = jnp.where(rows >= SUBLANES - 2, pltpu.roll(c8, SUBLANES - 2, 0), 0.0)
            n1 = jnp.concatenate([n1[:tr - SUBLANES], n1[tr - SUBLANES:] + fix1], axis=0)
            n2 = jnp.concatenate([n2[:tr - SUBLANES], n2[tr - SUBLANES:] + fix2], axis=0)
            dup = cw_ref[2:3, :] * du + cw_ref[1:2, :] * n1 + cw_ref[0:1, :] * n2
            dup_ref[half] = dup.astype(dup_ref.dtype)
            dcb_ref[half] += jnp.sum(du, axis=0, keepdims=True)
            for tap, shifted in enumerate((own2, own1, own)):
                dcw_ref[half, tap:tap + 1, :] += jnp.sum(du * shifted, axis=0, keepdims=True)
            next_du[half] = du[:SUBLANES].astype(BF16).astype(F32)

    return _call(
        body,
        name=name,
        grid=(nc, ni),
        in_specs=[pl.BlockSpec((tr, tc), cur(0)), pl.BlockSpec((HALO, tc), before(0)),
                  pl.BlockSpec((tr, tc), cur(nc)), pl.BlockSpec((HALO, tc), before(nc)),
                  pl.BlockSpec((tr, tc), cur(0)),
                  vec(CONV_WIDTH, 0), vec(CONV_WIDTH, nc), vec(1, 0), vec(1, nc), mat, mat, mat, mat],
        out_specs=[pl.BlockSpec((2, tr, tc), lambda j, r: (0, ni - 1 - r, j)),
                   pl.BlockSpec((2, CONV_WIDTH, tc), lambda j, r: (0, 0, j)),
                   pl.BlockSpec((2, 1, tc), lambda j, r: (0, 0, j))],
        out_shape=[jax.ShapeDtypeStruct((2, S, F), BF16), jax.ShapeDtypeStruct((2, CONV_WIDTH, F), F32),
                   jax.ShapeDtypeStruct((2, 1, F), F32)],
        scratch_shapes=[pltpu.VMEM((2, SUBLANES, tc), F32)],
        sem=("parallel", "arbitrary"),
        args=(up, up, up, up, da, cw, cw, cb, cb, *back, *ahead),
        comm=comm,
    )


def _adam_math(w, g, m, v):
    m = ADAM_B1 * m + (1.0 - ADAM_B1) * g
    v = ADAM_B2 * v + (1.0 - ADAM_B2) * (g * g)
    m_hat = m / (1.0 - ADAM_B1 ** ADAM_STEP)
    v_hat = v / (1.0 - ADAM_B2 ** ADAM_STEP)
    delta = -ADAM_LR * (m_hat / (jnp.sqrt(v_hat) + ADAM_EPS) + ADAM_WD * w)
    return delta, m, v


def _adamw(w, m, v, parts, *, name):
    R, C = w.shape
    plist = list(parts) if isinstance(parts, (list, tuple)) else [parts]
    tr = _tile(min(p.shape[1] for p in plist), 256, 16)
    assert sum(p.shape[1] for p in plist) == R and all(p.shape[1] % tr == 0 for p in plist)
    row = pl.BlockSpec((tr, C), lambda i: (i, 0))
    first, spans = 0, []
    for p in plist:
        spans.append((first, first + p.shape[1] // tr))
        first = spans[-1][1]

    def body(w_ref, m_ref, v_ref, *rest):
        p_refs, (g_ref, d_ref, m2_ref, v2_ref) = rest[:len(plist)], rest[len(plist):]
        i = pl.program_id(0)

        def update(p_ref):
            g = p_ref[0].astype(F32)
            for k in range(1, N_DEV):
                g = g + p_ref[k].astype(F32)
            g_ref[...] = g
            d_ref[...], m2_ref[...], v2_ref[...] = _adam_math(w_ref[...], g, m_ref[...], v_ref[...])

        if len(plist) == 1:
            update(p_refs[0])
        else:
            for p_ref, (lo, hi) in zip(p_refs, spans):
                pl.when((i >= lo) & (i < hi))(functools.partial(update, p_ref))

    def part_spec(lo, hi):
        return pl.BlockSpec((N_DEV, tr, C), lambda i: (0, jnp.clip(i - lo, 0, hi - lo - 1), 0))

    return pl.pallas_call(
        body,
        name=name,
        grid=(R // tr,),
        in_specs=[row, row, row] + [part_spec(lo, hi) for lo, hi in spans],
        out_specs=[row] * 4,
        out_shape=[jax.ShapeDtypeStruct((R, C), F32)] * 4,
        compiler_params=_params("parallel"),
    )(w, m, v, *plist)


def _adamw_ada(w, m, v, cact_t, dmod_cols, *, name):
    R, C = w.shape
    B = cact_t.shape[1]
    tr = _tile(R, 256, 8)
    row = pl.BlockSpec((tr, C), lambda i: (i, 0))

    def body(w_ref, m_ref, v_ref, c_ref, d_ref, g_ref, dl_ref, m2_ref, v2_ref):
        g = c_ref[:, 0:1] * d_ref[0:1, :]
        for b in range(1, B):
            g = g + c_ref[:, b:b + 1] * d_ref[b:b + 1, :]
        g_ref[...] = g
        dl_ref[...], m2_ref[...], v2_ref[...] = _adam_math(w_ref[...], g, m_ref[...], v_ref[...])

    return pl.pallas_call(
        body,
        name=name,
        grid=(R // tr,),
        in_specs=[row, row, row, pl.BlockSpec((tr, B), lambda i: (i, 0)), pl.BlockSpec((B, C), lambda i: (0, 0))],
        out_specs=[row] * 4,
        out_shape=[jax.ShapeDtypeStruct((R, C), F32)] * 4,
        compiler_params=_params("parallel"),
    )(w, m, v, cact_t, dmod_cols)


def _z_layout(D, q_rank, kv_rank):
    kv = SWA_KVH * SWA_HD
    orig = {}
    o = 0
    for nm, w in (("cq", q_rank), ("ckv", kv_rank), ("kr", MLA_ROPE), ("qs", D), ("ks", kv), ("vs", kv),
                  ("ga", D), ("gb", D)):
        orig[nm] = (o, w)
        o += w
    blockw = {"cq": q_rank, "ckv": kv_rank, "kr": LANES, "qs": D, "ks": kv, "vs": kv, "ga": D, "gb": D}
    best = None
    for perm in itertools.permutations(("cq", "ckv", "ks", "vs", "kr")):
        off, new = 0, {}
        for nm in ("ga", "gb", "qs") + perm:
            off = _round_up(off, blockw[nm])
            new[nm] = off
            off += blockw[nm]
        if best is None or off < best[0]:
            best = (off, new)
    total = _round_up(best[0], 1024 if best[0] > 4096 else 512)
    return orig, best[1], blockw, total, o


def _permute_w_in(w, lay):
    orig, new, blockw, total, _ = lay
    parts, at = [], 0
    for nm in sorted(new, key=new.get):
        if new[nm] > at:
            parts.append(jnp.zeros((w.shape[0], new[nm] - at), w.dtype))
        o, wd = orig[nm]
        parts.append(w[:, o:o + wd])
        if blockw[nm] > wd:
            parts.append(jnp.zeros((w.shape[0], blockw[nm] - wd), w.dtype))
        at = new[nm] + blockw[nm]
    if total > at:
        parts.append(jnp.zeros((w.shape[0], total - at), w.dtype))
    return jnp.concatenate(parts, axis=1)


def _unpermute_w_in(wp, lay):
    orig, new, _, _, _ = lay
    return jnp.concatenate([wp[:, new[nm]:new[nm] + orig[nm][1]] for nm in sorted(orig, key=lambda n: orig[n][0])],
                           axis=1)


def _assemble_dz(parts, lay, S):
    _, new, blockw, total, _ = lay
    names = sorted(new, key=new.get)
    tr = _tile(S, 256, 16)

    def body(*refs):
        o_ref = refs[-1]
        cols, at = [], 0
        for nm, ref in zip(names, refs):
            if new[nm] > at:
                cols.append(jnp.zeros((tr, new[nm] - at), BF16))
            cols.append(ref[...])
            at = new[nm] + blockw[nm]
        if total > at:
            cols.append(jnp.zeros((tr, total - at), BF16))
        o_ref[...] = jnp.concatenate(cols, axis=1)

    return pl.pallas_call(
        body,
        name="assemble_dz",
        grid=(S // tr,),
        in_specs=[pl.BlockSpec((tr, blockw[nm]), lambda i: (i, 0)) for nm in names],
        out_specs=pl.BlockSpec((tr, total), lambda i: (i, 0)),
        out_shape=jax.ShapeDtypeStruct((S, total), BF16),
        compiler_params=_params("parallel"),
    )(*[parts[nm] for nm in names])


def _unshard_cols(g):
    return jnp.transpose(g, (1, 0, 2)).reshape(g.shape[1], N_DEV * g.shape[2])


def _shard_cols(w):
    K, N = w.shape
    return jnp.transpose(w.reshape(K, N_DEV, N // N_DEV), (1, 0, 2))


def _pack(vecs, rows):
    flat = jnp.concatenate([v.reshape(-1) for v in vecs])
    return jnp.pad(flat, (0, rows * LANES - flat.shape[0])).reshape(rows, LANES)


def kernel(x, c, w_ada, b_ada, g_pre_mix, g_post_mix, w_in, g_q_lat, w_uq, g_kv_lat, w_ukv, rel_bias, sinks, w_o, g_pre_ffn, g_post_ffn, w_up, conv_w, conv_b, w_down, loss_target, m_w_ada, m_b_ada, m_g_pre_mix, m_g_post_mix, m_w_in, m_g_q_lat, m_w_uq, m_g_kv_lat, m_w_ukv, m_rel_bias, m_sinks, m_w_o, m_g_pre_ffn, m_g_post_ffn, m_w_up, m_conv_w, m_conv_b, m_w_down, v_w_ada, v_b_ada, v_g_pre_mix, v_g_post_mix, v_w_in, v_g_q_lat, v_w_uq, v_g_kv_lat, v_w_ukv, v_rel_bias, v_sinks, v_w_o, v_g_pre_ffn, v_g_post_ffn, v_w_up, v_conv_w, v_conv_b, v_w_down):
    S, D = x.shape[1], x.shape[2]
    Q_RANK, KV_RANK = g_q_lat.shape[1], g_kv_lat.shape[1]
    H_MLA = D // MLA_V
    H_SWA = D // SWA_HD
    G_SWA = H_SWA // SWA_KVH
    F2 = w_up.shape[2] * N_DEV
    F = F2 // 2
    ada_n = w_ada.shape[2]
    me = 4 * lax.axis_index("x") + 2 * lax.axis_index("y") + lax.axis_index("c")
    lay = _z_layout(D, Q_RANK, KV_RANK)
    _, zoff, _, NZ, in_cols = lay
    assert in_cols == w_in.shape[2] * N_DEV

    x2, tgt = x[0], loss_target[0]

    cw_n = conv_w.shape[2]
    small = jnp.concatenate([jnp.pad(c, ((0, 7), (0, 0))), jnp.pad(conv_w[0], ((0, 8 - CONV_WIDTH), (0, 0)))], axis=1)
    small_all = _all_gather(small, name="ag_cond", in_vmem=True)
    c_all = small_all[:, 0, :D]
    cw_full = _unshard_cols(small_all[:, :CONV_WIDTH, D:])
    b_cols = lax.dynamic_slice_in_dim(b_ada, me * ada_n, ada_n, axis=1)
    c_act, mod_cols = _ada_fwd(c_all, w_ada[0], b_cols, name="ada_fwd")
    mod_all = _all_gather(mod_cols, name="ag_mod", in_vmem=True)
    mod_me = lax.dynamic_index_in_dim(mod_all, me, axis=1, keepdims=False).reshape(1, N_DEV * ada_n)
    sh1, sc1, gt1, sh2, sc2, gt2 = [mod_me[:, k * D:(k + 1) * D] for k in range(6)]

    h1, (in_g,) = _prenorm(x2, g_pre_mix, sc1, sh1, name="prenorm_mix", comm=[("gather", w_in[0].astype(BF16))])
    w_in_p = _permute_w_in(_unshard_cols(in_g), lay)
    z, (uq_g, ukv_g, o_g) = _matmul(h1, w_in_p, mode="nn", out_dtype=BF16, name="mm_in",
                                    comm=[("gather", w_uq[0].astype(BF16)), ("gather", w_ukv[0].astype(BF16)),
                                          ("gather", w_o[0].astype(BF16))])
    w_uq_p = jnp.pad(_unshard_cols(uq_g).reshape(Q_RANK, H_MLA, MLA_QK), ((0, 0), (0, 0), (0, MLA_QK_PAD - MLA_QK))
                     ).reshape(Q_RANK, H_MLA * MLA_QK_PAD)
    w_ukv_f = _unshard_cols(ukv_g)
    w_o_f = o_g.reshape(D, D)
    cqn = _prenorm(z, g_q_lat, None, None, name="norm_cq", off=zoff["cq"], width=Q_RANK)
    ckvn = _prenorm(z, g_kv_lat, None, None, name="norm_ckv", off=zoff["ckv"], width=KV_RANK)
    q_raw = _matmul(cqn, w_uq_p, mode="nn", out_dtype=BF16, name="mm_uq")
    kv = _matmul(ckvn, w_ukv_f, mode="nn", out_dtype=BF16, name="mm_ukv")
    tab_k = _rope_tables(S, LANES, 0)
    krr = _rope(z, tab_k, heads=1, width=LANES, transpose=False, name="rope_k", off=zoff["kr"])
    (o_a, lse, Qr), (up_g,) = _flash_fwd(q_raw, kv, krr, tab_k, heads=H_MLA, name="mla_fwd",
                                        comm=[("gather", w_up[0].astype(BF16))])
    w_up_f = _unshard_cols(up_g)

    bucket, valid = _t5_bucket_table()
    onehot = (jnp.asarray(bucket).reshape(-1, 1) == jnp.arange(LANES)[None, :]).astype(F32)
    rb_pad = jnp.pad(rel_bias, ((0, LANES - REL_BUCKETS), (0, LANES - H_SWA)))
    bias_t = _matmul(onehot, rb_pad, mode="nn", out_dtype=F32, name="bias_table", tm=2048, precision=HIGHEST)
    bias_full = jnp.transpose(bias_t[:, :H_SWA].reshape(BLOCK, 2 * BLOCK, H_SWA), (2, 0, 1))
    bias_full = jnp.where(jnp.asarray(valid)[None], bias_full, NEG)
    bias_full = jnp.transpose(bias_full.reshape(SWA_KVH, G_SWA, BLOCK, 2 * BLOCK), (0, 3, 1, 2)
                              ).reshape(SWA_KVH, 2 * BLOCK, G_SWA * BLOCK)
    sink_rows = jnp.broadcast_to(sinks.reshape(SWA_KVH, G_SWA, 1), (SWA_KVH, G_SWA, BLOCK)
                                 ).reshape(SWA_KVH, 1, G_SWA * BLOCK)
    kvw = SWA_KVH * SWA_HD

    def heads_first(t, n):
        return jnp.transpose(t.reshape(S, n, SWA_HD), (1, 0, 2))

    def heads_last(t):
        return jnp.transpose(t, (1, 0, 2)).reshape(S, t.shape[0] * SWA_HD)


    ks_h = heads_first(z[:, zoff["ks"]:zoff["ks"] + kvw], SWA_KVH)
    vs_h = heads_first(z[:, zoff["vs"]:zoff["vs"] + kvw], SWA_KVH)
    o_b, _ = _swa_fwd(z, zoff["qs"], ks_h, vs_h, bias_full, sink_rows, name="swa_fwd")

    mixin = _gate_mix(z, o_a, o_b, D=D, off_a=zoff["ga"], off_b=zoff["gb"], name="gate_mix")
    mix = _matmul(mixin, w_o_f, mode="nn", out_dtype=F32, name="mm_o")
    x1 = _postnorm_res(x2, mix, gt1, g_post_mix, name="postnorm_mix")

    h2 = _prenorm(x1, g_pre_ffn, sc2, sh2, name="prenorm_ffn")
    up, (down_g,) = _matmul(h2, w_up_f, mode="nn", out_dtype=BF16, name="mm_up",
                            comm=[("gather", w_down[0].astype(BF16))])
    w_down_f = down_g.reshape(F, D)
    act = _conv_gate(up, cw_full, conv_b, name="conv_gate")
    y = _matmul(act, w_down_f, mode="nn", out_dtype=F32, name="mm_down")
    loss_part, dout, dy, dgt2, dg_post_ffn = _final_loss(x1, y, tgt, gt2, g_post_ffn, name="final_loss")
    loss = lax.psum(loss_part[0, 0], ("x", "y", "c"))

    dw_down = _matmul(act, dy, mode="tn", out_dtype=BF16, name="mm_down_dw")
    dact = _matmul(dy, w_down_f, mode="nt", out_dtype=BF16, name="mm_down_dx")
    (dup, dcw, dcb), (got_down,) = _conv_gate_bwd(up, dact, cw_full, conv_b, name="conv_gate_bwd",
                                                  comm=[("scatter", dw_down.reshape(N_DEV, F // N_DEV, D))])
    dcw = jnp.transpose(dcw, (1, 0, 2)).reshape(CONV_WIDTH, F2)
    dcb = dcb.reshape(1, F2)
    dw_up = _matmul(h2, dup, mode="tn", out_dtype=BF16, name="mm_up_dw", shard_out=True, halves=True)
    dh2 = _matmul(dup, w_up_f, mode="nt", out_dtype=F32, name="mm_up_dx", halves=True)
    dx1, dg_pre_ffn, dsc2, dsh2 = _prenorm_bwd(x1, dh2, dout, g_pre_ffn, sc2, name="prenorm_ffn_bwd", out_dtype=F32)

    dmix, dgt1, dg_post_mix = _postnorm_bwd(dx1, mix, gt1, g_post_mix, name="postnorm_mix_bwd")
    dw_o = _matmul(mixin, dmix, mode="tn", out_dtype=BF16, name="mm_o_dw")
    dmixin = _matmul(dmix, w_o_f, mode="nt", out_dtype=BF16, name="mm_o_dx")
    dga, dgb, do_a, do_b = _gate_mix_bwd(dmixin, z, o_a, o_b, D=D, off_a=zoff["ga"], off_b=zoff["gb"],
                                         name="gate_mix_bwd")
    (dq_raw, dkv, dkr_parts), (got_up,) = _flash_bwd(Qr, kv, krr, do_a, o_a, lse, tab_k, heads=H_MLA, name="mla_bwd",
                                                     comm=[("scatter", dw_up)])
    dkr = _shared_rope_grad(dkr_parts, tab_k, name="rope_k_bwd")
    dw_uq_p = _matmul(cqn, dq_raw, mode="tn", out_dtype=BF16, name="mm_uq_dw")
    dcqn = _matmul(dq_raw, w_uq_p, mode="nt", out_dtype=F32, name="mm_uq_dx")
    dw_ukv = _matmul(ckvn, dkv, mode="tn", out_dtype=BF16, name="mm_ukv_dw", shard_out=True)
    dckvn = _matmul(dkv, w_ukv_f, mode="nt", out_dtype=F32, name="mm_ukv_dx")
    dw_uq = dw_uq_p.reshape(Q_RANK, H_MLA, MLA_QK_PAD)[:, :, :MLA_QK].reshape(Q_RANK, H_MLA * MLA_QK)

    dcw_parts = jnp.pad(_shard_cols(dcw), ((0, 0), (0, 16 - CONV_WIDTH), (0, 0)))
    (dqs, dks_h, dvs_h, dbias, dsink), (got_o, got_cw, got_uq, got_ukv) = _swa_bwd(
        z, zoff["qs"], ks_h, vs_h, bias_full, sink_rows, do_b, name="swa_bwd",
        comm=[("scatter", dw_o.reshape(N_DEV, D // N_DEV, D)), ("scatter", dcw_parts),
              ("scatter", _shard_cols(dw_uq)), ("scatter", dw_ukv)])
    dbias = jnp.transpose(dbias.reshape(SWA_KVH, 2 * BLOCK, G_SWA, BLOCK), (0, 2, 3, 1))
    drel_t = _matmul(dbias.reshape(H_SWA, BLOCK * 2 * BLOCK), onehot, mode="nn", out_dtype=F32, name="bias_grad",
                     tk=4096, precision=HIGHEST)
    d_rel_bias = jnp.transpose(drel_t[:, :REL_BUCKETS])
    d_sinks = jnp.sum(dsink.reshape(SWA_KVH, G_SWA, BLOCK), axis=-1).reshape(1, H_SWA)

    dcq, dg_q = _prenorm_bwd(z, dcqn, None, g_q_lat, None, name="norm_cq_bwd", out_dtype=BF16,
                             off=zoff["cq"], width=Q_RANK)
    dckv, dg_kv = _prenorm_bwd(z, dckvn, None, g_kv_lat, None, name="norm_ckv_bwd", out_dtype=BF16,
                               off=zoff["ckv"], width=KV_RANK)
    dz = _assemble_dz({"ga": dga, "gb": dgb, "qs": dqs, "cq": dcq, "ckv": dckv,
                       "ks": heads_last(dks_h), "vs": heads_last(dvs_h), "kr": dkr}, lay, S)
    dw_in_a = _matmul(h1, dz, mode="tn", out_dtype=BF16, name="mm_in_dw_a", m_range=(0, D // 2))
    dw_in_b, (got_in_a,) = _matmul(h1, dz, mode="tn", out_dtype=BF16, name="mm_in_dw_b", m_range=(D // 2, D // 2),
                                   comm=[("scatter", _shard_cols(_unpermute_w_in(dw_in_a, lay)))])
    dh1, (got_in_b,) = _matmul(dz, w_in_p, mode="nt", out_dtype=F32, name="mm_in_dx",
                               comm=[("scatter", _shard_cols(_unpermute_w_in(dw_in_b, lay)))])
    grad_x, dg_pre_mix, dsc1, dsh1 = _prenorm_bwd(x2, dh1, dx1, g_pre_mix, sc1, name="prenorm_mix_bwd",
                                                  out_dtype=F32)
    dmod = jnp.concatenate([dsh1, dsc1, dgt1, dsh2, dsc2, dgt2], axis=1)

    small_names = ["b_ada", "g_pre_mix", "g_post_mix", "g_q_lat", "g_kv_lat", "rel_bias", "sinks", "g_pre_ffn",
                   "g_post_ffn", "conv_b"]
    small_w = [b_ada, g_pre_mix, g_post_mix, g_q_lat, g_kv_lat, rel_bias, sinks, g_pre_ffn, g_post_ffn, conv_b]
    small_m = [m_b_ada, m_g_pre_mix, m_g_post_mix, m_g_q_lat, m_g_kv_lat, m_rel_bias, m_sinks, m_g_pre_ffn,
               m_g_post_ffn, m_conv_b]
    small_v = [v_b_ada, v_g_pre_mix, v_g_post_mix, v_g_q_lat, v_g_kv_lat, v_rel_bias, v_sinks, v_g_pre_ffn,
               v_g_post_ffn, v_conv_b]
    small_g = [dmod, dg_pre_mix, dg_post_mix, dg_q, dg_kv, d_rel_bias, d_sinks, dg_pre_ffn, dg_post_ffn, dcb]
    n_small = sum(int(np.prod(w.shape)) for w in small_w)
    rows = _round_up(-(-n_small // LANES), 16)
    parts_small = _all_gather(_pack(small_g, rows), name="ag_small_grads", in_vmem=True)
    sg, sd, sm, sv = _adamw(_pack(small_w, rows), _pack(small_m, rows), _pack(small_v, rows), parts_small,
                            name="adamw_small")

    def unpack(packed):
        flat, out, at = packed.reshape(-1), {}, 0
        for nm, w in zip(small_names, small_w):
            n = int(np.prod(w.shape))
            out[nm] = flat[at:at + n].reshape(w.shape)
            at += n
        return out

    small_out = [unpack(t) for t in (sg, sd, sm, sv)]

    dmod_all = parts_small.reshape(N_DEV, rows * LANES)[:, :6 * D]
    dmod_cols = lax.dynamic_slice_in_dim(dmod_all, me * ada_n, ada_n, axis=1)
    ada_out = _adamw_ada(w_ada[0], m_w_ada[0], v_w_ada[0], jnp.transpose(c_act), dmod_cols, name="adamw_w_ada")

    def owner_update(got, w, m, v, name):
        shp = w.shape
        w2, m2, v2 = (t.reshape(shp[-2], shp[-1]) for t in (w, m, v))
        return [t.reshape(shp) for t in _adamw(w2, m2, v2, got, name="adamw_" + name)]

    def pad_rows(t):
        return jnp.pad(t[0], ((0, 16 - CONV_WIDTH), (0, 0)))

    big = {
        "w_in": owner_update([got_in_a, got_in_b], w_in, m_w_in, v_w_in, "w_in"),
        "w_uq": owner_update(got_uq, w_uq, m_w_uq, v_w_uq, "w_uq"),
        "w_ukv": owner_update(got_ukv, w_ukv, m_w_ukv, v_w_ukv, "w_ukv"),
        "w_o": owner_update(got_o, w_o, m_w_o, v_w_o, "w_o"),
        "w_up": owner_update(got_up, w_up, m_w_up, v_w_up, "w_up"),
        "w_down": owner_update(got_down, w_down, m_w_down, v_w_down, "w_down"),
    }
    cw_upd = _adamw(pad_rows(conv_w), pad_rows(m_conv_w), pad_rows(v_conv_w), got_cw, name="adamw_conv_w")
    big["conv_w"] = [t[:CONV_WIDTH].reshape(conv_w.shape) for t in cw_upd]
    big["w_ada"] = [t.reshape(w_ada.shape) for t in ada_out]

    order = ["w_ada", "b_ada", "g_pre_mix", "g_post_mix", "w_in", "g_q_lat", "w_uq", "g_kv_lat", "w_ukv", "rel_bias",
             "sinks", "w_o", "g_pre_ffn", "g_post_ffn", "w_up", "conv_w", "conv_b", "w_down"]
    outs = [loss, grad_x.reshape(x.shape)]
    for kind in range(4):
        for nm in order:
            outs.append(big[nm][kind] if nm in big else small_out[kind][nm])
    return tuple(outs)
```

```python
import functools
import itertools
import math

import numpy as np

import jax
import jax.numpy as jnp
from jax import lax
from jax.experimental import pallas as pl
from jax.experimental.pallas import tpu as pltpu

F32 = jnp.float32
BF16 = jnp.bfloat16

N_DEV = 8
MLA_NOPE = 128
MLA_ROPE = 64
MLA_V = 128
MLA_QK = MLA_NOPE + MLA_ROPE
MLA_QK_PAD = 256
ROPE_HALF = MLA_ROPE // 2
ROPE_THETA = 10000.0
SWA_HD = 64
SWA_KVH = 4
WINDOW = 128
BLOCK = 128
REL_BUCKETS = 32
REL_MAX_DIST = 128
CONV_WIDTH = 3
EPS = 1e-6
NEG = -1e30
ADAM_LR = 0.001
ADAM_B1 = 0.9
ADAM_B2 = 0.999
ADAM_EPS = 1e-08
ADAM_WD = 0.01
ADAM_STEP = 10
LANES = 128
HALO = 16
MESH = pl.DeviceIdType.MESH
HIGHEST = lax.Precision.HIGHEST

NN = (((1,), (0,)), ((), ()))
NT = (((1,), (1,)), ((), ()))
TN = (((0,), (0,)), ((), ()))


def _tile(n, pref, align=LANES):
    if n <= pref:
        return n
    t = (pref // align) * align
    while t >= align:
        if n % t == 0:
            return t
        t -= align
    return n


def _round_up(n, m):
    return (n + m - 1) // m * m


def _params(*sem):
    return pltpu.CompilerParams(dimension_semantics=sem)


def _sigmoid(x):
    return 1.0 / (1.0 + jnp.exp(-x))


def _my_place():
    return lax.axis_index("x"), lax.axis_index("y"), lax.axis_index("c")


def _all_gather(x, *, name, in_vmem):
    space = pltpu.VMEM if in_vmem else pl.ANY

    def body(x_ref, out_ref, send_sems, recv_sems, local_sem):
        x_, y_, c_ = _my_place()
        me, sibling = (x_, y_, c_), (x_, y_, 1 - c_)
        chips = [(1 - x_, y_), (x_, 1 - y_), (1 - x_, 1 - y_)]

        def slot(px, py, pc):
            return out_ref.at[4 * px + 2 * py + pc]

        def copy(k, block, to, src=None):
            return pltpu.make_async_remote_copy(
                src_ref=slot(*block) if src is None else src,
                dst_ref=slot(*block),
                send_sem=send_sems.at[k],
                recv_sem=recv_sems.at[k],
                device_id=to,
                device_id_type=MESH,
            )

        mine = pltpu.make_async_copy(x_ref, slot(*me), local_sem)
        mine.start()
        first = [copy(0, me, sibling, src=x_ref)]
        first += [copy(1 + j, me, (*chip, c_), src=x_ref) for j, chip in enumerate(chips)]
        for cp in first:
            cp.start()
        passed = [copy(4 + j, (*chip, c_), sibling) for j, chip in enumerate(chips)]
        for j, chip in enumerate(chips):
            copy(1 + j, (*chip, c_), me).wait_recv()
            passed[j].start()
        copy(0, sibling, me).wait_recv()
        for j, chip in enumerate(chips):
            copy(4 + j, (*chip, 1 - c_), me).wait_recv()
        for cp in first + passed:
            cp.wait_send()
        mine.wait()

    return pl.pallas_call(
        body,
        name=name,
        out_shape=jax.ShapeDtypeStruct((N_DEV,) + x.shape, x.dtype),
        in_specs=[pl.BlockSpec(memory_space=space)],
        out_specs=pl.BlockSpec(memory_space=space),
        scratch_shapes=[
            pltpu.SemaphoreType.DMA((7,)),
            pltpu.SemaphoreType.DMA((7,)),
            pltpu.SemaphoreType.DMA,
        ],
    )(x)


class _Exchange:
    def __init__(self, kind, x_ref, out_ref, send_sems, recv_sems, local_sems, t):
        x_, y_, c_ = _my_place()
        me = 4 * x_ + 2 * y_ + c_

        def pair(k, src, dst, to):
            return pltpu.make_async_remote_copy(src_ref=src, dst_ref=dst, send_sem=send_sems.at[7 * t + k],
                                                recv_sem=recv_sems.at[7 * t + k], device_id=to, device_id_type=MESH)

        none = lambda: []
        if kind == "scatter":
            peers = [(x_ ^ ((r >> 2) & 1), y_ ^ ((r >> 1) & 1), c_ ^ (r & 1)) for r in range(1, N_DEV)]
            self.at_start = lambda: [pair(k, x_ref.at[4 * px + 2 * py + pc], out_ref.at[me], (px, py, pc))
                                     for k, (px, py, pc) in enumerate(peers)]
            self.relay_after, self.at_relay, self.late_after, self.late = none, none, none, none
            self.arrivals = self.at_start
            self.own = lambda: pltpu.make_async_copy(x_ref.at[me], out_ref.at[me], local_sems.at[t])
        elif kind == "gather":
            sibling = (x_, y_, 1 - c_)
            chips = list(enumerate([(1 - x_, y_), (x_, 1 - y_), (1 - x_, 1 - y_)]))

            def slot(px, py, pc):
                return out_ref.at[4 * px + 2 * py + pc]

            mine = slot(x_, y_, c_)
            self.at_start = lambda: ([pair(0, x_ref, mine, sibling)]
                                     + [pair(1 + j, x_ref, mine, (*chip, c_)) for j, chip in chips])
            self.relay_after = lambda: [pair(1 + j, slot(*chip, c_), slot(*chip, c_), (*chip, c_)) for j, chip in chips]
            self.at_relay = lambda: [pair(4 + j, slot(*chip, c_), slot(*chip, c_), sibling) for j, chip in chips]
            self.late_after, self.late = none, none
            self.arrivals = lambda: ([pair(0, slot(*sibling), slot(*sibling), sibling)]
                                     + [pair(4 + j, slot(*chip, 1 - c_), slot(*chip, 1 - c_), sibling)
                                        for j, chip in chips])
            self.own = lambda: pltpu.make_async_copy(x_ref, mine, local_sems.at[t])
        else:
            sibling = (x_, y_, 1 - c_)
            xn, yn, diag = (1 - x_, y_), (x_, 1 - y_), (1 - x_, 1 - y_)
            source = (x_ ^ (1 - c_), y_ ^ c_)
            target = (x_ ^ c_, y_ ^ (1 - c_))

            def slot(px, py, pc):
                return out_ref.at[4 * px + 2 * py + pc]

            def landed(k, chip, core):
                return pair(k, slot(*chip, core), slot(*chip, core), sibling)

            mine = slot(x_, y_, c_)
            self.at_start = lambda: [pair(0, x_ref, mine, sibling), pair(1, x_ref, mine, (*xn, c_)),
                                     pair(2, x_ref, mine, (*yn, c_))]
            self.relay_after = lambda: [landed(1, xn, c_), landed(2, yn, c_)]
            self.at_relay = lambda: [pair(3, slot(*source, c_), slot(*source, c_), (*target, c_)),
                                     pair(4, slot(*xn, c_), slot(*xn, c_), sibling),
                                     pair(5, slot(*yn, c_), slot(*yn, c_), sibling)]
            self.late_after = lambda: [landed(3, diag, c_)]
            self.late = lambda: [pair(6, slot(*diag, c_), slot(*diag, c_), sibling)]
            self.arrivals = lambda: [landed(0, (x_, y_), 1 - c_), landed(4, xn, 1 - c_), landed(5, yn, 1 - c_),
                                     landed(6, diag, 1 - c_)]
            self.own = lambda: pltpu.make_async_copy(x_ref, mine, local_sems.at[t])

    def start(self):
        self.own().start()
        for cp in self.at_start():
            cp.start()

    def relay(self):
        for cp in self.relay_after():
            cp.wait_recv()
        for cp in self.at_relay():
            cp.start()

    def finish(self):
        for cp in self.late_after():
            cp.wait_recv()
        for cp in self.late():
            cp.start()
        for cp in self.arrivals():
            cp.wait_recv()
        for cp in self.at_start() + self.at_relay() + self.late():
            cp.wait_send()
        self.own().wait()


RELAY_AT = 0.85


def _call(body, *, name, grid, in_specs, out_specs, out_shape, args, scratch_shapes=(), sem=(), comm=(), prefetch=()):
    n_pf = len(prefetch)

    def launch(fn, ins, outs, shapes, scratch, semantics, operands):
        spec = pltpu.PrefetchScalarGridSpec(num_scalar_prefetch=n_pf, grid=grid, in_specs=ins, out_specs=outs,
                                            scratch_shapes=scratch)
        return pl.pallas_call(fn, name=name, grid_spec=spec, out_shape=shapes,
                              compiler_params=_params(*semantics))(*prefetch, *operands)

    if not comm:
        return list(launch(body, list(in_specs), list(out_specs), list(out_shape), list(scratch_shapes), sem, args)), []
    n_in, n_out, n_c, n_s = len(in_specs), len(out_specs), len(comm), len(scratch_shapes)
    kinds = [kind for kind, _ in comm]
    hbm = pl.BlockSpec(memory_space=pl.ANY)

    def wrapped(*refs):
        tables, refs = refs[:n_pf], refs[n_pf:]
        ins, cin = refs[:n_in], refs[n_in:n_in + n_c]
        at = n_in + n_c
        outs, cout = refs[at:at + n_out], refs[at + n_out:at + n_out + n_c]
        scr = refs[at + n_out + n_c:at + n_out + n_c + n_s]
        send, recv, local = refs[-3:]
        step = 0
        for a, g in enumerate(grid):
            step = step * g + pl.program_id(a)
        n_steps = int(np.prod(grid))

        def exchanges():
            return [_Exchange(kinds[t], cin[t], cout[t], send, recv, local, t) for t in range(n_c)]

        @pl.when(step == 0)
        def _():
            for ex in exchanges():
                ex.start()

        body(*tables, *ins, *outs, *scr)

        @pl.when(step == min(int(RELAY_AT * n_steps), n_steps - 1))
        def _():
            for ex in exchanges():
                ex.relay()

        @pl.when(step == n_steps - 1)
        def _():
            for ex in exchanges():
                ex.finish()

    c_shapes = [jax.ShapeDtypeStruct(((N_DEV,) + a.shape) if kind != "scatter" else a.shape, a.dtype)
                for kind, a in comm]
    sems = [pltpu.SemaphoreType.DMA((7 * n_c,)), pltpu.SemaphoreType.DMA((7 * n_c,)), pltpu.SemaphoreType.DMA((n_c,))]
    res = launch(wrapped, list(in_specs) + [hbm] * n_c, list(out_specs) + [hbm] * n_c, list(out_shape) + c_shapes,
                 list(scratch_shapes) + sems, ["arbitrary"] * len(grid), (*args, *[a for _, a in comm]))
    return list(res[:n_out]), list(res[n_out:])


def _matmul(a, b, *, mode, out_dtype, name, tm=1024, tn=1024, tk=2816, precision=None, comm=(), shard_out=False,
            halves=False, m_range=None):
    if mode == "nn":
        (M, K), (K2, N) = a.shape, b.shape
    elif mode == "nt":
        (M, K), (N, K2) = (a.shape[1], 2 * a.shape[2]) if halves else a.shape, b.shape
    else:
        (K, M), (K2, N) = a.shape, (b.shape[1], 2 * b.shape[2]) if halves else b.shape
    assert K == K2, (a.shape, b.shape, mode)
    m_off = 0
    if m_range is not None:
        m_off, M = m_range
    tm = _tile(M, tm, LANES if mode == "tn" else 16)
    tk = _tile(K // 2 if halves and mode == "nt" else K, tk)
    tn = _tile(N // N_DEV, max(tn, 1408)) if shard_out else _tile(N // 2 if halves and mode == "tn" else N, tn)
    nk = K // tk
    m_off //= tm
    if mode == "tn":
        a_spec = pl.BlockSpec((tk, tm), lambda i, j, k: (k, i + m_off))
    elif halves:
        a_spec = pl.BlockSpec((None, tm, tk), lambda i, j, k: (k // (nk // 2), i, k % (nk // 2)))
    else:
        a_spec = pl.BlockSpec((tm, tk), lambda i, j, k: (i, k))
    if mode == "nt":
        b_spec = pl.BlockSpec((tn, tk), lambda i, j, k: (j, k))
    elif halves:
        nj = N // tn
        b_spec = pl.BlockSpec((None, tk, tn), lambda i, j, k: (j // (nj // 2), k, j % (nj // 2)))
    else:
        b_spec = pl.BlockSpec((tk, tn), lambda i, j, k: (k, j))
    dn = {"nn": NN, "nt": NT, "tn": TN}[mode]
    if shard_out:
        per = N // N_DEV // tn
        o_spec = pl.BlockSpec((None, tm, tn), lambda i, j, k: (j // per, i, j % per))
        o_shape = jax.ShapeDtypeStruct((N_DEV, M, N // N_DEV), out_dtype)
    else:
        o_spec = pl.BlockSpec((tm, tn), lambda i, j, k: (i, j))
        o_shape = jax.ShapeDtypeStruct((M, N), out_dtype)

    def product(a_ref, b_ref):
        return lax.dot_general(a_ref[...], b_ref[...], dn, preferred_element_type=F32, precision=precision)

    def body_one(a_ref, b_ref, o_ref):
        o_ref[...] = product(a_ref, b_ref).astype(o_ref.dtype)

    def body_acc(a_ref, b_ref, o_ref, acc_ref):
        k = pl.program_id(2)

        @pl.when(k == 0)
        def _():
            acc_ref[...] = product(a_ref, b_ref)

        @pl.when(k > 0)
        def _():
            acc_ref[...] += product(a_ref, b_ref)

        @pl.when(k == nk - 1)
        def _():
            o_ref[...] = acc_ref[...].astype(o_ref.dtype)

    outs, moved = _call(
        body_one if nk == 1 else body_acc,
        name=name,
        grid=(M // tm, N // tn, nk),
        in_specs=[a_spec, b_spec],
        out_specs=[o_spec],
        out_shape=[o_shape],
        scratch_shapes=[] if nk == 1 else [pltpu.VMEM((tm, tn), F32)],
        sem=("parallel", "parallel", "arbitrary"),
        args=(a, b),
        comm=comm,
    )
    return (outs[0], moved) if comm else outs[0]


def _rstd(xf):
    return lax.rsqrt(jnp.mean(xf * xf, axis=-1, keepdims=True) + EPS)


def _col_view(width, off):
    assert off % width == 0
    return off // width


def _prenorm(x, g, sc, sh, *, name, off=0, width=None, comm=()):
    S = x.shape[0]
    W = x.shape[1] if width is None else width
    cb = _col_view(W, off)
    tr = _tile(S, 512, 16)
    mod = sc is not None
    vec = pl.BlockSpec((1, W), lambda i: (0, 0))

    def body(*refs):
        if mod:
            x_ref, g_ref, sc_ref, sh_ref, o_ref = refs
        else:
            x_ref, g_ref, o_ref = refs
        xf = x_ref[...].astype(F32)
        y = xf * _rstd(xf) * g_ref[...]
        if mod:
            y = y * (1.0 + sc_ref[...]) + sh_ref[...]
        o_ref[...] = y.astype(o_ref.dtype)

    args = (x, g, sc, sh) if mod else (x, g)
    outs, moved = _call(
        body,
        name=name,
        grid=(S // tr,),
        in_specs=[pl.BlockSpec((tr, W), lambda i: (i, cb))] + [vec] * (len(args) - 1),
        out_specs=[pl.BlockSpec((tr, W), lambda i: (i, 0))],
        out_shape=[jax.ShapeDtypeStruct((S, W), BF16)],
        sem=("parallel",),
        args=args,
        comm=comm,
    )
    return (outs[0], moved) if comm else outs[0]


def _prenorm_bwd(x, dh, dres, g, sc, *, name, out_dtype, off=0, width=None):
    S = x.shape[0]
    W = x.shape[1] if width is None else width
    cb = _col_view(W, off)
    tr = _tile(S, 256, 16)
    mod = sc is not None
    res = dres is not None
    vec = pl.BlockSpec((1, W), lambda i: (0, 0))
    row = pl.BlockSpec((tr, W), lambda i: (i, 0))

    def body(*refs):
        it = iter(refs)
        x_ref, dh_ref = next(it), next(it)
        dres_ref = next(it) if res else None
        g_ref = next(it)
        sc_ref = next(it) if mod else None
        dx_ref, dg_ref = next(it), next(it)
        dsc_ref, dsh_ref = (next(it), next(it)) if mod else (None, None)
        i = pl.program_id(0)

        @pl.when(i == 0)
        def _():
            dg_ref[...] = jnp.zeros_like(dg_ref)
            if mod:
                dsc_ref[...] = jnp.zeros_like(dsc_ref)
                dsh_ref[...] = jnp.zeros_like(dsh_ref)

        xf = x_ref[...].astype(F32)
        r = _rstd(xf)
        xn = xf * r
        dhf = dh_ref[...].astype(F32)
        gv = g_ref[...]
        if mod:
            one_sc = 1.0 + sc_ref[...]
            dsh_ref[...] += jnp.sum(dhf, axis=0, keepdims=True)
            dsc_ref[...] += jnp.sum(dhf * (xn * gv), axis=0, keepdims=True)
            dg_ref[...] += jnp.sum(dhf * xn * one_sc, axis=0, keepdims=True)
            dxn = dhf * (gv * one_sc)
        else:
            dg_ref[...] += jnp.sum(dhf * xn, axis=0, keepdims=True)
            dxn = dhf * gv
        dx = r * (dxn - xn * jnp.mean(dxn * xn, axis=-1, keepdims=True))
        if res:
            dx = dx + dres_ref[...]
        dx_ref[...] = dx.astype(dx_ref.dtype)

    args = [x, dh] + ([dres] if res else []) + [g] + ([sc] if mod else [])
    in_specs = [pl.BlockSpec((tr, W), lambda i: (i, cb)), row] + ([row] if res else []) + [vec] + ([vec] if mod else [])
    n_vec = 3 if mod else 1
    outs = pl.pallas_call(
        body,
        name=name,
        grid=(S // tr,),
        in_specs=in_specs,
        out_specs=[row] + [vec] * n_vec,
        out_shape=[jax.ShapeDtypeStruct((S, W), out_dtype)] + [jax.ShapeDtypeStruct((1, W), F32)] * n_vec,
        compiler_params=_params("arbitrary"),
    )(*args)
    return outs


def _postnorm_res(x, y, gt, g, *, name):
    S, D = x.shape
    tr = _tile(S, 512, 8)
    row = pl.BlockSpec((tr, D), lambda i: (i, 0))
    vec = pl.BlockSpec((1, D), lambda i: (0, 0))

    def body(x_ref, y_ref, gt_ref, g_ref, o_ref):
        yf = y_ref[...]
        o_ref[...] = x_ref[...] + gt_ref[...] * (yf * _rstd(yf) * g_ref[...])

    return pl.pallas_call(
        body,
        name=name,
        grid=(S // tr,),
        in_specs=[row, row, vec, vec],
        out_specs=row,
        out_shape=jax.ShapeDtypeStruct((S, D), F32),
        compiler_params=_params("parallel"),
    )(x, y, gt, g)


def _postnorm_bwd(dx1, y, gt, g, *, name):
    S, D = y.shape
    tr = _tile(S, 256, 16)
    row = pl.BlockSpec((tr, D), lambda i: (i, 0))
    vec = pl.BlockSpec((1, D), lambda i: (0, 0))

    def body(dx_ref, y_ref, gt_ref, g_ref, dy_ref, dgt_ref, dg_ref):
        @pl.when(pl.program_id(0) == 0)
        def _():
            dgt_ref[...] = jnp.zeros_like(dgt_ref)
            dg_ref[...] = jnp.zeros_like(dg_ref)

        yf = y_ref[...]
        r = _rstd(yf)
        yn = yf * r
        d = dx_ref[...]
        gtv, gv = gt_ref[...], g_ref[...]
        dgt_ref[...] += jnp.sum(d * (yn * gv), axis=0, keepdims=True)
        dg_ref[...] += jnp.sum(d * gtv * yn, axis=0, keepdims=True)
        dyn = d * (gtv * gv)
        dy_ref[...] = (r * (dyn - yn * jnp.mean(dyn * yn, axis=-1, keepdims=True))).astype(dy_ref.dtype)

    return pl.pallas_call(
        body,
        name=name,
        grid=(S // tr,),
        in_specs=[row, row, vec, vec],
        out_specs=[row, vec, vec],
        out_shape=[jax.ShapeDtypeStruct((S, D), BF16), jax.ShapeDtypeStruct((1, D), F32),
                   jax.ShapeDtypeStruct((1, D), F32)],
        compiler_params=_params("arbitrary"),
    )(dx1, y, gt, g)


def _final_loss(x1, y, target, gt, g, *, name):
    S, D = y.shape
    tr = _tile(S, 256, 16)
    row = pl.BlockSpec((tr, D), lambda i: (i, 0))
    vec = pl.BlockSpec((1, D), lambda i: (0, 0))
    one = pl.BlockSpec((1, LANES), lambda i: (0, 0))

    def body(x_ref, y_ref, t_ref, gt_ref, g_ref, loss_ref, dout_ref, dy_ref, dgt_ref, dg_ref):
        @pl.when(pl.program_id(0) == 0)
        def _():
            loss_ref[...] = jnp.zeros_like(loss_ref)
            dgt_ref[...] = jnp.zeros_like(dgt_ref)
            dg_ref[...] = jnp.zeros_like(dg_ref)

        yf = y_ref[...]
        r = _rstd(yf)
        yn = yf * r
        gtv, gv = gt_ref[...], g_ref[...]
        out = x_ref[...] + gtv * (yn * gv)
        diff = out - t_ref[...]
        per_tok = jnp.mean(diff * diff, axis=-1, keepdims=True)
        loss_ref[...] += 0.5 * jnp.sum(per_tok, axis=0, keepdims=True)
        d = diff / D
        dout_ref[...] = d
        dgt_ref[...] += jnp.sum(d * (yn * gv), axis=0, keepdims=True)
        dg_ref[...] += jnp.sum(d * gtv * yn, axis=0, keepdims=True)
        dyn = d * (gtv * gv)
        dy_ref[...] = (r * (dyn - yn * jnp.mean(dyn * yn, axis=-1, keepdims=True))).astype(dy_ref.dtype)

    return pl.pallas_call(
        body,
        name=name,
        grid=(S // tr,),
        in_specs=[row, row, row, vec, vec],
        out_specs=[one, row, row, vec, vec],
        out_shape=[jax.ShapeDtypeStruct((1, LANES), F32), jax.ShapeDtypeStruct((S, D), F32),
                   jax.ShapeDtypeStruct((S, D), BF16), jax.ShapeDtypeStruct((1, D), F32),
                   jax.ShapeDtypeStruct((1, D), F32)],
        compiler_params=_params("arbitrary"),
    )(x1, y, target, gt, g)


def _ada_fwd(c_all, w_local, b_cols, *, name):
    B, D = c_all.shape
    N = w_local.shape[1]
    tn = _tile(N, 512)

    def body(c_ref, w_ref, b_ref, ca_ref, mod_ref):
        cv = c_ref[...]
        ca = cv * _sigmoid(cv)
        ca_ref[...] = ca
        mod_ref[...] = jnp.dot(ca, w_ref[...], preferred_element_type=F32, precision=HIGHEST) + b_ref[...]

    return pl.pallas_call(
        body,
        name=name,
        grid=(N // tn,),
        in_specs=[pl.BlockSpec((B, D), lambda j: (0, 0)), pl.BlockSpec((D, tn), lambda j: (0, j)),
                  pl.BlockSpec((1, tn), lambda j: (0, j))],
        out_specs=[pl.BlockSpec((B, D), lambda j: (0, 0)), pl.BlockSpec((B, tn), lambda j: (0, j))],
        out_shape=[jax.ShapeDtypeStruct((B, D), F32), jax.ShapeDtypeStruct((B, N), F32)],
        compiler_params=_params("arbitrary"),
    )(c_all, w_local, b_cols)


def _rope_tables(S, width, lane_off):
    pos = jnp.arange(S, dtype=F32)
    inv = ROPE_THETA ** (-jnp.arange(0, MLA_ROPE, 2, dtype=F32) / MLA_ROPE)
    ang = pos[:, None] * inv[None, :]
    ang = jnp.concatenate([ang, ang], axis=-1)
    cos, sin = jnp.cos(ang), jnp.sin(ang)
    first = (jnp.arange(MLA_ROPE) < ROPE_HALF)[None, :]
    sa = jnp.where(first, -sin, 0.0)
    sb = jnp.where(first, 0.0, sin)

    def place(t, fill):
        return jnp.pad(t, ((0, 0), (lane_off, width - lane_off - MLA_ROPE)), constant_values=fill)

    return place(cos, 1.0), place(sa, 0.0), place(sb, 0.0)


def _rope_apply(x, cos, sa, sb, width, transpose):
    if transpose:
        return x * cos + pltpu.roll(x * sa, ROPE_HALF, 1) + pltpu.roll(x * sb, width - ROPE_HALF, 1)
    return x * cos + pltpu.roll(x, width - ROPE_HALF, 1) * sa + pltpu.roll(x, ROPE_HALF, 1) * sb


def _rope(x, tables, *, heads, width, transpose, name, off=0, scale=1.0):
    S = x.shape[0]
    cb = _col_view(width, off)
    tr = _tile(S, 512, 16)
    tab = pl.BlockSpec((tr, width), lambda i, h: (i, 0))

    def body(x_ref, c_ref, sa_ref, sb_ref, o_ref):
        y = _rope_apply(x_ref[...].astype(F32), c_ref[...], sa_ref[...], sb_ref[...], width, transpose)
        o_ref[...] = (y if scale == 1.0 else y * scale).astype(o_ref.dtype)

    return pl.pallas_call(
        body,
        name=name,
        grid=(S // tr, heads),
        in_specs=[pl.BlockSpec((tr, width), lambda i, h: (i, cb + h)), tab, tab, tab],
        out_specs=pl.BlockSpec((tr, width), lambda i, h: (i, h)),
        out_shape=jax.ShapeDtypeStruct((S, heads * width), BF16),
        compiler_params=_params("parallel", "parallel"),
    )(x, *tables)


def _shared_rope_grad(parts, tables, *, name):
    P, S, _ = parts.shape
    tr = _tile(S, 512, 16)
    tab = pl.BlockSpec((tr, LANES), lambda i: (i, 0))

    def body(p_ref, c_ref, sa_ref, sb_ref, o_ref):
        acc = p_ref[0]
        for k in range(1, P):
            acc = acc + p_ref[k]
        o_ref[...] = _rope_apply(acc, c_ref[...], sa_ref[...], sb_ref[...], LANES, True).astype(o_ref.dtype)

    return pl.pallas_call(
        body,
        name=name,
        grid=(S // tr,),
        in_specs=[pl.BlockSpec((P, tr, LANES), lambda i: (0, i, 0)), tab, tab, tab],
        out_specs=tab,
        out_shape=jax.ShapeDtypeStruct((S, LANES), BF16),
        compiler_params=_params("parallel"),
    )(parts, *tables)


MLA_SCALE = MLA_QK ** -0.5
LOG2E = math.log2(math.e)
LN2 = math.log(2.0)
MLA_Q_PRESCALE = MLA_SCALE * LOG2E


def _lane_tile(v, n):
    return v if n == LANES else jnp.tile(v, (1, n // LANES))


def _causal_mask(s):
    rows = lax.broadcasted_iota(jnp.int32, s.shape, 0)
    cols = lax.broadcasted_iota(jnp.int32, s.shape, 1)
    return jnp.where(cols <= rows, s, NEG)


def _tri_blocks(nb, q_major):
    if q_major:
        pairs = [(q, k) for q in range(nb) for k in range(q + 1)]
    else:
        pairs = [(q, k) for k in range(nb) for q in range(k, nb)]
    return (jnp.asarray(np.array([p[0] for p in pairs], np.int32)),
            jnp.asarray(np.array([p[1] for p in pairs], np.int32)))


HEAD_PAIR = 4


def _flash_fwd(q_raw, KV, krr, tables, *, heads, name, comm=()):
    S = q_raw.shape[0]
    t = _tile(S, 512)
    nb = S // t
    qt, kt = _tri_blocks(nb, True)
    qw, vw = HEAD_PAIR * MLA_QK_PAD, HEAD_PAIR * MLA_V

    def body(qt_ref, kt_ref, q_ref, *rest):
        kn_refs, kr_ref, v_refs = rest[:HEAD_PAIR], rest[HEAD_PAIR], rest[HEAD_PAIR + 1:2 * HEAD_PAIR + 1]
        c_ref, sa_ref, sb_ref, o_ref, lse_ref, qr_ref, m_scr, l_scr, acc_scr = rest[2 * HEAD_PAIR + 1:]
        step_id = pl.program_id(1)
        qi, ki = qt_ref[step_id], kt_ref[step_id]

        @pl.when(ki == 0)
        def _():
            m_scr[...] = jnp.full_like(m_scr, NEG)
            l_scr[...] = jnp.zeros_like(l_scr)
            acc_scr[...] = jnp.zeros_like(acc_scr)
            for h in range(HEAD_PAIR):
                base = h * MLA_QK_PAD
                nope = q_ref[:, base:base + MLA_NOPE].astype(F32) * MLA_Q_PRESCALE
                rot = _rope_apply(q_ref[:, base + MLA_NOPE:base + MLA_QK_PAD].astype(F32), c_ref[...], sa_ref[...],
                                  sb_ref[...], LANES, False) * MLA_Q_PRESCALE
                qr_ref[:, base:base + MLA_NOPE] = nope.astype(qr_ref.dtype)
                qr_ref[:, base + MLA_NOPE:base + MLA_QK_PAD] = rot.astype(qr_ref.dtype)

        def step(diagonal):
            for h, (kn_ref, v_ref) in enumerate(zip(kn_refs, v_refs)):
                cols = slice(h * MLA_QK_PAD, (h + 1) * MLA_QK_PAD)
                k = jnp.concatenate([kn_ref[...], kr_ref[...]], axis=1)
                s = lax.dot_general(qr_ref[:, cols], k, NT, preferred_element_type=F32)
                if diagonal:
                    s = _causal_mask(s)
                m_prev = m_scr[h]
                m_new = jnp.maximum(m_prev, jnp.max(s, axis=1, keepdims=True))
                alpha = jnp.exp2(m_prev - m_new)
                p = jnp.exp2(s - _lane_tile(m_new, t))
                l_new = alpha * l_scr[h] + jnp.sum(p, axis=1, keepdims=True)
                acc = alpha * acc_scr[h] + jnp.dot(p.astype(BF16), v_ref[...], preferred_element_type=F32)
                if diagonal:
                    o_ref[:, h * MLA_V:(h + 1) * MLA_V] = (acc / l_new).astype(o_ref.dtype)
                    lse_ref[h] = m_new + jnp.log(l_new) * LOG2E
                else:
                    l_scr[h], acc_scr[h], m_scr[h] = l_new, acc, m_new

        pl.when(ki < qi)(lambda: step(False))
        pl.when(ki == qi)(lambda: step(True))

    def kvspec(h, half):
        return pl.BlockSpec((t, LANES), lambda hp, s, qt, kt: (kt[s], 2 * (HEAD_PAIR * hp + h) + half))

    qtab = pl.BlockSpec((t, LANES), lambda hp, s, qt, kt: (qt[s], 0))
    qrow = lambda hp, s, qt, kt: (qt[s], hp)
    return _call(
        body,
        name=name,
        grid=(heads // HEAD_PAIR, int(qt.shape[0])),
        in_specs=[pl.BlockSpec((t, qw), qrow), *[kvspec(h, 0) for h in range(HEAD_PAIR)],
                  pl.BlockSpec((t, LANES), lambda hp, s, qt, kt: (kt[s], 0)),
                  *[kvspec(h, 1) for h in range(HEAD_PAIR)], qtab, qtab, qtab],
        out_specs=[pl.BlockSpec((t, vw), qrow),
                   pl.BlockSpec((HEAD_PAIR, t, LANES), lambda hp, s, qt, kt: (hp, qt[s], 0)),
                   pl.BlockSpec((t, qw), qrow)],
        out_shape=[jax.ShapeDtypeStruct((S, heads * MLA_V), BF16),
                   jax.ShapeDtypeStruct((heads, S, LANES), F32),
                   jax.ShapeDtypeStruct((S, heads * MLA_QK_PAD), BF16)],
        scratch_shapes=[pltpu.VMEM((HEAD_PAIR, t, LANES), F32), pltpu.VMEM((HEAD_PAIR, t, LANES), F32),
                        pltpu.VMEM((HEAD_PAIR, t, MLA_V), F32)],
        sem=("parallel", "arbitrary"),
        args=(q_raw, *[KV] * HEAD_PAIR, krr, *[KV] * HEAD_PAIR, *tables),
        comm=comm,
        prefetch=(qt, kt),
    )


def _flash_bwd(Q, KV, krr, dO, O, lse, tables, *, heads, name, comm=()):
    S = Q.shape[0]
    t = _tile(S, 512)
    nb = S // t
    qt, kt = _tri_blocks(nb, False)
    n_steps = int(qt.shape[0])
    qw, vw = HEAD_PAIR * MLA_QK_PAD, HEAD_PAIR * MLA_V

    def body(qt_ref, kt_ref, q_ref, *rest):
        kn_refs, kr_ref, v_refs = rest[:HEAD_PAIR], rest[HEAD_PAIR], rest[HEAD_PAIR + 1:2 * HEAD_PAIR + 1]
        (do_ref, o_ref, lse_ref, c_ref, sa_ref, sb_ref, dq_ref, dkv_ref, dkr_ref,
         dq_scr, dk_scr, dv_scr, delta_scr) = rest[2 * HEAD_PAIR + 1:]
        step_id = pl.program_id(1)
        qi, ki = qt_ref[step_id], kt_ref[step_id]

        @pl.when(ki == 0)
        def _():
            for h in range(HEAD_PAIR):
                vc = slice(h * MLA_V, (h + 1) * MLA_V)
                d = jnp.sum(do_ref[:, vc].astype(F32) * o_ref[:, vc].astype(F32), axis=1, keepdims=True)
                delta_scr[h, qi] = jnp.broadcast_to(d, (t, LANES))

        def step(diagonal):
            for h, (kn_ref, v_ref) in enumerate(zip(kn_refs, v_refs)):
                base = h * MLA_QK_PAD
                cols = slice(base, base + MLA_QK_PAD)
                vc = slice(h * MLA_V, (h + 1) * MLA_V)
                q, do = q_ref[:, cols], do_ref[:, vc]
                k = jnp.concatenate([kn_ref[...], kr_ref[...]], axis=1)
                s = lax.dot_general(q, k, NT, preferred_element_type=F32)
                if diagonal:
                    s = _causal_mask(s)
                p = jnp.exp2(s - _lane_tile(lse_ref[h], t))
                dv = lax.dot_general(p.astype(BF16), do, TN, preferred_element_type=F32)
                dp = lax.dot_general(do, v_ref[...], NT, preferred_element_type=F32)
                ds = (p * (dp - _lane_tile(delta_scr[h, qi], t))).astype(BF16)
                dk = lax.dot_general(ds, q, TN, preferred_element_type=F32)
                dq = jnp.dot(ds, k, preferred_element_type=F32)
                if diagonal:
                    dk_scr[h], dv_scr[h] = dk, dv
                    dq = (dq_scr[qi, :, cols] + dq) * (LN2 * MLA_Q_PRESCALE)
                    rot = _rope_apply(dq[:, MLA_NOPE:], c_ref[...], sa_ref[...], sb_ref[...], LANES, True)
                    dq_ref[:, base:base + MLA_NOPE] = dq[:, :MLA_NOPE].astype(dq_ref.dtype)
                    dq_ref[:, base + MLA_NOPE:base + MLA_QK_PAD] = rot.astype(dq_ref.dtype)
                else:
                    dk_scr[h] += dk
                    dv_scr[h] += dv
                    dq_scr[qi, :, cols] += dq

        @pl.when(ki == 0)
        def _():
            dq_scr[qi] = jnp.zeros((t, qw), F32)

        pl.when(qi > ki)(lambda: step(False))
        pl.when(qi == ki)(lambda: step(True))

        @pl.when(qi == nb - 1)
        def _():
            shared = jnp.zeros((t, LANES), F32)
            for h in range(HEAD_PAIR):
                base = h * MLA_QK_PAD
                dk = dk_scr[h] * LN2
                dkv_ref[:, base:base + MLA_NOPE] = dk[:, :MLA_NOPE].astype(dkv_ref.dtype)
                dkv_ref[:, base + MLA_NOPE:base + MLA_QK_PAD] = dv_scr[h].astype(dkv_ref.dtype)
                shared = shared + dk[:, MLA_NOPE:]
            dkr_ref[0] = shared

    def kvspec(h, half):
        return pl.BlockSpec((t, LANES), lambda hp, s, qt, kt: (kt[s], 2 * (HEAD_PAIR * hp + h) + half))

    qrow = lambda hp, s, qt, kt: (qt[s], hp)
    krow = lambda hp, s, qt, kt: (kt[s], hp)
    ktab = pl.BlockSpec((t, LANES), lambda hp, s, qt, kt: (kt[s], 0))
    return _call(
        body,
        name=name,
        grid=(heads // HEAD_PAIR, n_steps),
        in_specs=[pl.BlockSpec((t, qw), qrow), *[kvspec(h, 0) for h in range(HEAD_PAIR)], ktab,
                  *[kvspec(h, 1) for h in range(HEAD_PAIR)],
                  pl.BlockSpec((t, vw), qrow), pl.BlockSpec((t, vw), qrow),
                  pl.BlockSpec((HEAD_PAIR, t, LANES), lambda hp, s, qt, kt: (hp, qt[s], 0)),
                  ktab, ktab, ktab],
        out_specs=[pl.BlockSpec((t, qw), krow), pl.BlockSpec((t, qw), krow),
                   pl.BlockSpec((1, t, LANES), lambda hp, s, qt, kt: (hp, kt[s], 0))],
        out_shape=[jax.ShapeDtypeStruct((S, heads * MLA_QK_PAD), BF16),
                   jax.ShapeDtypeStruct((S, heads * MLA_QK_PAD), BF16),
                   jax.ShapeDtypeStruct((heads // HEAD_PAIR, S, LANES), F32)],
        scratch_shapes=[pltpu.VMEM((nb, t, qw), F32), pltpu.VMEM((HEAD_PAIR, t, MLA_QK_PAD), F32),
                        pltpu.VMEM((HEAD_PAIR, t, MLA_V), F32), pltpu.VMEM((HEAD_PAIR, nb, t, LANES), F32)],
        sem=("parallel", "arbitrary"),
        args=(Q, *[KV] * HEAD_PAIR, krr, *[KV] * HEAD_PAIR, dO, O, lse, *tables),
        comm=comm,
        prefetch=(qt, kt),
    )


SWA_SCALE = SWA_HD ** -0.5


def _t5_bucket_table():
    a = np.arange(BLOCK)[:, None]
    j = np.arange(2 * BLOCK)[None, :]
    dist = BLOCK + a - j
    max_exact = REL_BUCKETS // 2
    n = np.maximum(dist, 0)
    large = max_exact + (np.log(np.maximum(n, 1).astype(np.float32) / np.float32(max_exact))
                         / np.float32(math.log(REL_MAX_DIST / max_exact))
                         * np.float32(REL_BUCKETS - max_exact)).astype(np.int32)
    large = np.minimum(large, REL_BUCKETS - 1)
    bucket = np.where(n < max_exact, n, large)
    valid = (dist >= 0) & (dist < WINDOW)
    return bucket.astype(np.int32), valid


def _heads_to_rows(x, G):
    return jnp.concatenate([x[:, g * SWA_HD:(g + 1) * SWA_HD] for g in range(G)], axis=0)


def _rows_to_heads(o_ref, x, G):
    for g in range(G):
        o_ref[:, g * SWA_HD:(g + 1) * SWA_HD] = x[g * BLOCK:(g + 1) * BLOCK].astype(o_ref.dtype)


def _swa_probs(q_ref, kp_ref, kc_ref, bias_ref, sink_ref, qb, G):
    q2 = _heads_to_rows(q_ref[...], G)
    kb = jnp.concatenate([kp_ref[0], kc_ref[0]], axis=0)
    s = lax.dot_general(kb, q2, NT, preferred_element_type=F32) * SWA_SCALE + bias_ref[0]
    keys = lax.broadcasted_iota(jnp.int32, s.shape, 0)
    s = jnp.where((keys >= BLOCK) | (qb > 0), s, NEG)
    sink = sink_ref[0]
    m = jnp.maximum(jnp.max(s, axis=0, keepdims=True), sink)
    e = jnp.exp(s - m)
    es = jnp.exp(sink - m)
    inv = 1.0 / (jnp.sum(e, axis=0, keepdims=True) + es)
    return q2, kb, e * inv, es * inv


def _swa_fwd(q, q_off, k, v, bias_t, sink, *, name, comm=()):
    S = k.shape[1]
    G = bias_t.shape[2] // BLOCK
    nb = S // BLOCK
    qcol = _col_view(G * SWA_HD, q_off)
    cur = lambda kh, qb: (kh, qb, 0)
    prev = lambda kh, qb: (kh, jnp.maximum(qb - 1, 0), 0)
    kvspec = lambda im: pl.BlockSpec((1, BLOCK, SWA_HD), im)

    def body(q_ref, kc_ref, kp_ref, vc_ref, vp_ref, bias_ref, sink_ref, o_ref):
        qb = pl.program_id(1)
        _, _, pt, _ = _swa_probs(q_ref, kp_ref, kc_ref, bias_ref, sink_ref, qb, G)
        vb = jnp.concatenate([vp_ref[0], vc_ref[0]], axis=0)
        _rows_to_heads(o_ref, lax.dot_general(pt.astype(BF16), vb, TN, preferred_element_type=F32), G)

    outs, moved = _call(
        body,
        name=name,
        grid=(SWA_KVH, nb),
        in_specs=[pl.BlockSpec((BLOCK, G * SWA_HD), lambda kh, qb: (qb, qcol + kh)),
                  kvspec(cur), kvspec(prev), kvspec(cur), kvspec(prev),
                  pl.BlockSpec((1, 2 * BLOCK, G * BLOCK), lambda kh, qb: (kh, 0, 0)),
                  pl.BlockSpec((1, 1, G * BLOCK), lambda kh, qb: (kh, 0, 0))],
        out_specs=[pl.BlockSpec((BLOCK, G * SWA_HD), lambda kh, qb: (qb, kh))],
        out_shape=[jax.ShapeDtypeStruct((S, SWA_KVH * G * SWA_HD), BF16)],
        sem=("parallel", "parallel"),
        args=(q, k, k, v, v, bias_t, sink),
        comm=comm,
    )
    return outs[0], moved


def _swa_bwd(q, q_off, k, v, bias_t, sink, do, *, name, comm=()):
    S = k.shape[1]
    G = bias_t.shape[2] // BLOCK
    nb = S // BLOCK
    qcol = _col_view(G * SWA_HD, q_off)
    cur = lambda kh, qb: (kh, jnp.minimum(qb, nb - 1), 0)
    prev = lambda kh, qb: (kh, jnp.maximum(jnp.minimum(qb, nb - 1) - 1, 0), 0)
    lag = lambda kh, qb: (kh, jnp.maximum(qb - 1, 0), 0)
    kvspec = lambda im: pl.BlockSpec((1, BLOCK, SWA_HD), im)

    def body(q_ref, kc_ref, kp_ref, vc_ref, vp_ref, bias_ref, sink_ref, do_ref,
             dq_ref, dk_ref, dv_ref, dbias_ref, dsink_ref, ck_scr, cv_scr):
        qb = pl.program_id(1)

        @pl.when(qb == 0)
        def _():
            dbias_ref[...] = jnp.zeros_like(dbias_ref)
            dsink_ref[...] = jnp.zeros_like(dsink_ref)
            ck_scr[...] = jnp.zeros_like(ck_scr)
            cv_scr[...] = jnp.zeros_like(cv_scr)

        @pl.when(qb < nb)
        def _():
            q2, kb, pt, ps = _swa_probs(q_ref, kp_ref, kc_ref, bias_ref, sink_ref, qb, G)
            vb = jnp.concatenate([vp_ref[0], vc_ref[0]], axis=0)
            do2 = _heads_to_rows(do_ref[...], G)
            dpt = lax.dot_general(vb, do2, NT, preferred_element_type=F32)
            delta = jnp.sum(dpt * pt, axis=0, keepdims=True)
            dst = pt * (dpt - delta)
            dbias_ref[0] += dst
            dsink_ref[0] += -ps * delta
            dsb = (dst * SWA_SCALE).astype(BF16)
            _rows_to_heads(dq_ref, lax.dot_general(dsb, kb, TN, preferred_element_type=F32), G)
            dkb = jnp.dot(dsb, q2, preferred_element_type=F32)
            dvb = jnp.dot(pt.astype(BF16), do2, preferred_element_type=F32)
            dk_ref[0] = (ck_scr[...] + dkb[:BLOCK]).astype(dk_ref.dtype)
            dv_ref[0] = (cv_scr[...] + dvb[:BLOCK]).astype(dv_ref.dtype)
            ck_scr[...] = dkb[BLOCK:]
            cv_scr[...] = dvb[BLOCK:]

        @pl.when(qb == nb)
        def _():
            dk_ref[0] = ck_scr[...].astype(dk_ref.dtype)
            dv_ref[0] = cv_scr[...].astype(dv_ref.dtype)

    tspec = pl.BlockSpec((BLOCK, G * SWA_HD), lambda kh, qb: (jnp.minimum(qb, nb - 1), kh))
    return _call(
        body,
        name=name,
        grid=(SWA_KVH, nb + 1),
        in_specs=[pl.BlockSpec((BLOCK, G * SWA_HD), lambda kh, qb: (jnp.minimum(qb, nb - 1), qcol + kh)),
                  kvspec(cur), kvspec(prev), kvspec(cur), kvspec(prev),
                  pl.BlockSpec((1, 2 * BLOCK, G * BLOCK), lambda kh, qb: (kh, 0, 0)),
                  pl.BlockSpec((1, 1, G * BLOCK), lambda kh, qb: (kh, 0, 0)),
                  pl.BlockSpec((BLOCK, G * SWA_HD), lambda kh, qb: (jnp.minimum(qb, nb - 1), kh))],
        out_specs=[tspec, kvspec(lag), kvspec(lag),
                   pl.BlockSpec((1, 2 * BLOCK, G * BLOCK), lambda kh, qb: (kh, 0, 0)),
                   pl.BlockSpec((1, 1, G * BLOCK), lambda kh, qb: (kh, 0, 0))],
        out_shape=[jax.ShapeDtypeStruct((S, SWA_KVH * G * SWA_HD), BF16),
                   jax.ShapeDtypeStruct((SWA_KVH, S, SWA_HD), BF16),
                   jax.ShapeDtypeStruct((SWA_KVH, S, SWA_HD), BF16),
                   jax.ShapeDtypeStruct((SWA_KVH, 2 * BLOCK, G * BLOCK), F32),
                   jax.ShapeDtypeStruct((SWA_KVH, 1, G * BLOCK), F32)],
        scratch_shapes=[pltpu.VMEM((BLOCK, SWA_HD), F32), pltpu.VMEM((BLOCK, SWA_HD), F32)],
        sem=("parallel", "arbitrary"),
        args=(q, k, k, v, v, bias_t, sink, do),
        comm=comm,
    )


def _gate_mix(z, o_a, o_b, *, D, off_a, off_b, name):
    S = z.shape[0]
    tr = _tile(S, 256, 16)
    row = pl.BlockSpec((tr, D), lambda i: (i, 0))
    ca, cb = _col_view(D, off_a), _col_view(D, off_b)

    def body(ga_ref, gb_ref, oa_ref, ob_ref, m_ref):
        m = (_sigmoid(ga_ref[...].astype(F32)) * oa_ref[...].astype(F32)
             + _sigmoid(gb_ref[...].astype(F32)) * ob_ref[...].astype(F32))
        m_ref[...] = m.astype(m_ref.dtype)

    return pl.pallas_call(
        body,
        name=name,
        grid=(S // tr,),
        in_specs=[pl.BlockSpec((tr, D), lambda i: (i, ca)), pl.BlockSpec((tr, D), lambda i: (i, cb)), row, row],
        out_specs=row,
        out_shape=jax.ShapeDtypeStruct((S, D), BF16),
        compiler_params=_params("parallel"),
    )(z, z, o_a, o_b)


def _gate_mix_bwd(dm, z, o_a, o_b, *, D, off_a, off_b, name):
    S = z.shape[0]
    tr = _tile(S, 256, 16)
    row = pl.BlockSpec((tr, D), lambda i: (i, 0))
    ca, cb = _col_view(D, off_a), _col_view(D, off_b)

    def body(dm_ref, ga_ref, gb_ref, oa_ref, ob_ref, dga_ref, dgb_ref, doa_ref, dob_ref):
        d = dm_ref[...].astype(F32)
        for g_ref, o_ref, dg_ref, do_ref in ((ga_ref, oa_ref, dga_ref, doa_ref), (gb_ref, ob_ref, dgb_ref, dob_ref)):
            sg = _sigmoid(g_ref[...].astype(F32))
            dg_ref[...] = (d * o_ref[...].astype(F32) * (sg * (1.0 - sg))).astype(dg_ref.dtype)
            do_ref[...] = (d * sg).astype(do_ref.dtype)

    return pl.pallas_call(
        body,
        name=name,
        grid=(S // tr,),
        in_specs=[row, pl.BlockSpec((tr, D), lambda i: (i, ca)), pl.BlockSpec((tr, D), lambda i: (i, cb)), row, row],
        out_specs=[row] * 4,
        out_shape=[jax.ShapeDtypeStruct((S, D), BF16)] * 4,
        compiler_params=_params("parallel"),
    )(dm, z, z, o_a, o_b)


CONV_ROWS = 256
CONV_COLS = 1408
SUBLANES = 8


def _shift_matrices(tr):
    r = np.arange(tr)[:, None]
    c = np.arange(tr)[None, :]
    back = [jnp.asarray(r == c + d, dtype=BF16) for d in (1, 2)]
    ahead = [jnp.asarray(r + d == c, dtype=BF16) for d in (1, 2)]
    return back, ahead


def _rows_before(x, halo_ref, first, b1_ref, b2_ref):
    s1 = jnp.dot(b1_ref[...], x, preferred_element_type=F32)
    s2 = jnp.dot(b2_ref[...], x, preferred_element_type=F32)
    h8 = jnp.where(first, 0.0, halo_ref[...].astype(F32)[HALO - SUBLANES:])
    rows = lax.broadcasted_iota(jnp.int32, h8.shape, 0)
    fix1 = jnp.where(rows < 1, pltpu.roll(h8, 1, 0), 0.0)
    fix2 = jnp.where(rows < 2, pltpu.roll(h8, 2, 0), 0.0)
    s1 = jnp.concatenate([s1[:SUBLANES] + fix1, s1[SUBLANES:]], axis=0)
    s2 = jnp.concatenate([s2[:SUBLANES] + fix2, s2[SUBLANES:]], axis=0)
    return s1, s2


def _conv_taps(x, s1, s2, cw_ref, cb_ref):
    return cb_ref[...] + cw_ref[0:1, :] * s2 + cw_ref[1:2, :] * s1 + cw_ref[2:3, :] * x


def _conv_gate(up, cw, cb, *, name):
    S, F2 = up.shape
    F = F2 // 2
    tr = _tile(S, CONV_ROWS, HALO)
    tc = _tile(F, CONV_COLS)
    nc = F // tc
    hb = tr // HALO
    back, _ = _shift_matrices(tr)
    mat = pl.BlockSpec((tr, tr), lambda i, j: (0, 0))

    def halo_map(shift):
        return lambda i, j: (jnp.maximum(i * hb - 1, 0), j + shift)

    def body(x1_ref, h1_ref, x2_ref, h2_ref, cw1_ref, cw2_ref, cb1_ref, cb2_ref, b1_ref, b2_ref, a_ref):
        first = pl.program_id(0) == 0
        us = []
        for x_ref, h_ref, cw_ref, cb_ref in ((x1_ref, h1_ref, cw1_ref, cb1_ref), (x2_ref, h2_ref, cw2_ref, cb2_ref)):
            x = x_ref[...]
            s1, s2 = _rows_before(x, h_ref, first, b1_ref, b2_ref)
            us.append(_conv_taps(x.astype(F32), s1, s2, cw_ref, cb_ref))
        u1, u2 = us
        a_ref[...] = (u1 * _sigmoid(u1) * u2).astype(a_ref.dtype)

    return pl.pallas_call(
        body,
        name=name,
        grid=(S // tr, nc),
        in_specs=[pl.BlockSpec((tr, tc), lambda i, j: (i, j)), pl.BlockSpec((HALO, tc), halo_map(0)),
                  pl.BlockSpec((tr, tc), lambda i, j: (i, j + nc)), pl.BlockSpec((HALO, tc), halo_map(nc)),
                  pl.BlockSpec((CONV_WIDTH, tc), lambda i, j: (0, j)),
                  pl.BlockSpec((CONV_WIDTH, tc), lambda i, j: (0, j + nc)),
                  pl.BlockSpec((1, tc), lambda i, j: (0, j)), pl.BlockSpec((1, tc), lambda i, j: (0, j + nc)),
                  mat, mat],
        out_specs=pl.BlockSpec((tr, tc), lambda i, j: (i, j)),
        out_shape=jax.ShapeDtypeStruct((S, F), BF16),
        compiler_params=_params("parallel", "parallel"),
    )(up, up, up, up, cw, cw, cb, cb, *back)


def _conv_gate_bwd(up, da, cw, cb, *, name, comm=()):
    S, F2 = up.shape
    F = F2 // 2
    tr = _tile(S, CONV_ROWS, HALO)
    tc = _tile(F, CONV_COLS)
    nc = F // tc
    hb = tr // HALO
    ni = S // tr
    back, ahead = _shift_matrices(tr)
    mat = pl.BlockSpec((tr, tr), lambda j, r: (0, 0))

    def cur(shift):
        return lambda j, r: (ni - 1 - r, j + shift)

    def before(shift):
        return lambda j, r: (jnp.maximum((ni - 1 - r) * hb - 1, 0), j + shift)

    def vec(rows, shift):
        return pl.BlockSpec((rows, tc), lambda j, r: (0, j + shift))

    def body(x1_ref, h1_ref, x2_ref, h2_ref, da_ref, cw1_ref, cw2_ref, cb1_ref, cb2_ref,
             b1_ref, b2_ref, a1_ref, a2_ref, dup_ref, dcw_ref, dcb_ref, next_du):
        r = pl.program_id(1)
        first = r == ni - 1

        @pl.when(r == 0)
        def _():
            dcw_ref[...] = jnp.zeros_like(dcw_ref)
            dcb_ref[...] = jnp.zeros_like(dcb_ref)
            next_du[...] = jnp.zeros_like(next_du)

        x1, x2 = x1_ref[...], x2_ref[...]
        x1f, x2f = x1.astype(F32), x2.astype(F32)
        s11, s12 = _rows_before(x1, h1_ref, first, b1_ref, b2_ref)
        s21, s22 = _rows_before(x2, h2_ref, first, b1_ref, b2_ref)
        u1 = _conv_taps(x1f, s11, s12, cw1_ref, cb1_ref)
        u2 = _conv_taps(x2f, s21, s22, cw2_ref, cb2_ref)
        sg = _sigmoid(u1)
        daf = da_ref[...].astype(F32)
        du1 = daf * u2 * (sg * (1.0 + u1 * (1.0 - sg)))
        du2 = daf * (u1 * sg)
        rows = lax.broadcasted_iota(jnp.int32, (SUBLANES, tc), 0)

        for half, (du, own, own1, own2, cw_ref) in enumerate(((du1, x1f, s11, s12, cw1_ref),
                                                             (du2, x2f, s21, s22, cw2_ref))):
            du_b = du.astype(BF16)
            n1 = jnp.dot(a1_ref[...], du_b, preferred_element_type=F32)
            n2 = jnp.dot(a2_ref[...], du_b, preferred_element_type=F32)
            c8 = next_du[half]
            fix1 = jnp.where(rows >= SUBLANES - 1, pltpu.roll(c8, SUBLANES - 1, 0), 0.0)
            fix2 = jnp.where(rows >= SUBLANES - 2, pltpu.roll(c8, SUBLANES - 2, 0), 0.0)
            n1 = jnp.concatenate([n1[:tr - SUBLANES], n1[tr - SUBLANES:] + fix1], axis=0)
            n2 = jnp.concatenate([n2[:tr - SUBLANES], n2[tr - SUBLANES:] + fix2], axis=0)
            dup = cw_ref[2:3, :] * du + cw_ref[1:2, :] * n1 + cw_ref[0:1, :] * n2
            dup_ref[half] = dup.astype(dup_ref.dtype)
            dcb_ref[half] += jnp.sum(du, axis=0, keepdims=True)
            for tap, shifted in enumerate((own2, own1, own)):
                dcw_ref[half, tap:tap + 1, :] += jnp.sum(du * shifted, axis=0, keepdims=True)
            next_du[half] = du[:SUBLANES].astype(BF16).astype(F32)

    return _call(
        body,
        name=name,
        grid=(nc, ni),
        in_specs=[pl.BlockSpec((tr, tc), cur(0)), pl.BlockSpec((HALO, tc), before(0)),
                  pl.BlockSpec((tr, tc), cur(nc)), pl.BlockSpec((HALO, tc), before(nc)),
                  pl.BlockSpec((tr, tc), cur(0)),
                  vec(CONV_WIDTH, 0), vec(CONV_WIDTH, nc), vec(1, 0), vec(1, nc), mat, mat, mat, mat],
        out_specs=[pl.BlockSpec((2, tr, tc), lambda j, r: (0, ni - 1 - r, j)),
                   pl.BlockSpec((2, CONV_WIDTH, tc), lambda j, r: (0, 0, j)),
                   pl.BlockSpec((2, 1, tc), lambda j, r: (0, 0, j))],
        out_shape=[jax.ShapeDtypeStruct((2, S, F), BF16), jax.ShapeDtypeStruct((2, CONV_WIDTH, F), F32),
                   jax.ShapeDtypeStruct((2, 1, F), F32)],
        scratch_shapes=[pltpu.VMEM((2, SUBLANES, tc), F32)],
        sem=("parallel", "arbitrary"),
        args=(up, up, up, up, da, cw, cw, cb, cb, *back, *ahead),
        comm=comm,
    )


def _adam_math(w, g, m, v):
    m = ADAM_B1 * m + (1.0 - ADAM_B1) * g
    v = ADAM_B2 * v + (1.0 - ADAM_B2) * (g * g)
    m_hat = m / (1.0 - ADAM_B1 ** ADAM_STEP)
    v_hat = v / (1.0 - ADAM_B2 ** ADAM_STEP)
    delta = -ADAM_LR * (m_hat / (jnp.sqrt(v_hat) + ADAM_EPS) + ADAM_WD * w)
    return delta, m, v


def _adamw(w, m, v, parts, *, name):
    R, C = w.shape
    plist = list(parts) if isinstance(parts, (list, tuple)) else [parts]
    tr = _tile(min(p.shape[1] for p in plist), 256, 16)
    assert sum(p.shape[1] for p in plist) == R and all(p.shape[1] % tr == 0 for p in plist)
    row = pl.BlockSpec((tr, C), lambda i: (i, 0))
    first, spans = 0, []
    for p in plist:
        spans.append((first, first + p.shape[1] // tr))
        first = spans[-1][1]

    def body(w_ref, m_ref, v_ref, *rest):
        p_refs, (g_ref, d_ref, m2_ref, v2_ref) = rest[:len(plist)], rest[len(plist):]
        i = pl.program_id(0)

        def update(p_ref):
            g = p_ref[0].astype(F32)
            for k in range(1, N_DEV):
                g = g + p_ref[k].astype(F32)
            g_ref[...] = g
            d_ref[...], m2_ref[...], v2_ref[...] = _adam_math(w_ref[...], g, m_ref[...], v_ref[...])

        if len(plist) == 1:
            update(p_refs[0])
        else:
            for p_ref, (lo, hi) in zip(p_refs, spans):
                pl.when((i >= lo) & (i < hi))(functools.partial(update, p_ref))

    def part_spec(lo, hi):
        return pl.BlockSpec((N_DEV, tr, C), lambda i: (0, jnp.clip(i - lo, 0, hi - lo - 1), 0))

    return pl.pallas_call(
        body,
        name=name,
        grid=(R // tr,),
        in_specs=[row, row, row] + [part_spec(lo, hi) for lo, hi in spans],
        out_specs=[row] * 4,
        out_shape=[jax.ShapeDtypeStruct((R, C), F32)] * 4,
        compiler_params=_params("parallel"),
    )(w, m, v, *plist)


def _adamw_ada(w, m, v, cact_t, dmod_cols, *, name):
    R, C = w.shape
    B = cact_t.shape[1]
    tr = _tile(R, 256, 8)
    row = pl.BlockSpec((tr, C), lambda i: (i, 0))

    def body(w_ref, m_ref, v_ref, c_ref, d_ref, g_ref, dl_ref, m2_ref, v2_ref):
        g = c_ref[:, 0:1] * d_ref[0:1, :]
        for b in range(1, B):
            g = g + c_ref[:, b:b + 1] * d_ref[b:b + 1, :]
        g_ref[...] = g
        dl_ref[...], m2_ref[...], v2_ref[...] = _adam_math(w_ref[...], g, m_ref[...], v_ref[...])

    return pl.pallas_call(
        body,
        name=name,
        grid=(R // tr,),
        in_specs=[row, row, row, pl.BlockSpec((tr, B), lambda i: (i, 0)), pl.BlockSpec((B, C), lambda i: (0, 0))],
        out_specs=[row] * 4,
        out_shape=[jax.ShapeDtypeStruct((R, C), F32)] * 4,
        compiler_params=_params("parallel"),
    )(w, m, v, cact_t, dmod_cols)


def _z_layout(D, q_rank, kv_rank):
    kv = SWA_KVH * SWA_HD
    orig = {}
    o = 0
    for nm, w in (("cq", q_rank), ("ckv", kv_rank), ("kr", MLA_ROPE), ("qs", D), ("ks", kv), ("vs", kv),
                  ("ga", D), ("gb", D)):
        orig[nm] = (o, w)
        o += w
    blockw = {"cq": q_rank, "ckv": kv_rank, "kr": LANES, "qs": D, "ks": kv, "vs": kv, "ga": D, "gb": D}
    best = None
    for perm in itertools.permutations(("cq", "ckv", "ks", "vs", "kr")):
        off, new = 0, {}
        for nm in ("ga", "gb", "qs") + perm:
            off = _round_up(off, blockw[nm])
            new[nm] = off
            off += blockw[nm]
        if best is None or off < best[0]:
            best = (off, new)
    total = _round_up(best[0], 1024 if best[0] > 4096 else 512)
    return orig, best[1], blockw, total, o


def _permute_w_in(w, lay):
    orig, new, blockw, total, _ = lay
    parts, at = [], 0
    for nm in sorted(new, key=new.get):
        if new[nm] > at:
            parts.append(jnp.zeros((w.shape[0], new[nm] - at), w.dtype))
        o, wd = orig[nm]
        parts.append(w[:, o:o + wd])
        if blockw[nm] > wd:
            parts.append(jnp.zeros((w.shape[0], blockw[nm] - wd), w.dtype))
        at = new[nm] + blockw[nm]
    if total > at:
        parts.append(jnp.zeros((w.shape[0], total - at), w.dtype))
    return jnp.concatenate(parts, axis=1)


def _unpermute_w_in(wp, lay):
    orig, new, _, _, _ = lay
    return jnp.concatenate([wp[:, new[nm]:new[nm] + orig[nm][1]] for nm in sorted(orig, key=lambda n: orig[n][0])],
                           axis=1)


def _assemble_dz(parts, lay, S):
    _, new, blockw, total, _ = lay
    names = sorted(new, key=new.get)
    tr = _tile(S, 256, 16)

    def body(*refs):
        o_ref = refs[-1]
        cols, at = [], 0
        for nm, ref in zip(names, refs):
            if new[nm] > at:
                cols.append(jnp.zeros((tr, new[nm] - at), BF16))
            cols.append(ref[...])
            at = new[nm] + blockw[nm]
        if total > at:
            cols.append(jnp.zeros((tr, total - at), BF16))
        o_ref[...] = jnp.concatenate(cols, axis=1)

    return pl.pallas_call(
        body,
        name="assemble_dz",
        grid=(S // tr,),
        in_specs=[pl.BlockSpec((tr, blockw[nm]), lambda i: (i, 0)) for nm in names],
        out_specs=pl.BlockSpec((tr, total), lambda i: (i, 0)),
        out_shape=jax.ShapeDtypeStruct((S, total), BF16),
        compiler_params=_params("parallel"),
    )(*[parts[nm] for nm in names])


def _unshard_cols(g):
    return jnp.transpose(g, (1, 0, 2)).reshape(g.shape[1], N_DEV * g.shape[2])


def _shard_cols(w):
    K, N = w.shape
    return jnp.transpose(w.reshape(K, N_DEV, N // N_DEV), (1, 0, 2))


def _pack(vecs, rows):
    flat = jnp.concatenate([v.reshape(-1) for v in vecs])
    return jnp.pad(flat, (0, rows * LANES - flat.shape[0])).reshape(rows, LANES)


def kernel(x, c, w_ada, b_ada, g_pre_mix, g_post_mix, w_in, g_q_lat, w_uq, g_kv_lat, w_ukv, rel_bias, sinks, w_o, g_pre_ffn, g_post_ffn, w_up, conv_w, conv_b, w_down, loss_target, m_w_ada, m_b_ada, m_g_pre_mix, m_g_post_mix, m_w_in, m_g_q_lat, m_w_uq, m_g_kv_lat, m_w_ukv, m_rel_bias, m_sinks, m_w_o, m_g_pre_ffn, m_g_post_ffn, m_w_up, m_conv_w, m_conv_b, m_w_down, v_w_ada, v_b_ada, v_g_pre_mix, v_g_post_mix, v_w_in, v_g_q_lat, v_w_uq, v_g_kv_lat, v_w_ukv, v_rel_bias, v_sinks, v_w_o, v_g_pre_ffn, v_g_post_ffn, v_w_up, v_conv_w, v_conv_b, v_w_down):
    S, D = x.shape[1], x.shape[2]
    Q_RANK, KV_RANK = g_q_lat.shape[1], g_kv_lat.shape[1]
    H_MLA = D // MLA_V
    H_SWA = D // SWA_HD
    G_SWA = H_SWA // SWA_KVH
    F2 = w_up.shape[2] * N_DEV
    F = F2 // 2
    ada_n = w_ada.shape[2]
    me = 4 * lax.axis_index("x") + 2 * lax.axis_index("y") + lax.axis_index("c")
    lay = _z_layout(D, Q_RANK, KV_RANK)
    _, zoff, _, NZ, in_cols = lay
    assert in_cols == w_in.shape[2] * N_DEV

    x2, tgt = x[0], loss_target[0]

    cw_n = conv_w.shape[2]
    small = jnp.concatenate([jnp.pad(c, ((0, 7), (0, 0))), jnp.pad(conv_w[0], ((0, 8 - CONV_WIDTH), (0, 0)))], axis=1)
    small_all = _all_gather(small, name="ag_cond", in_vmem=True)
    c_all = small_all[:, 0, :D]
    cw_full = _unshard_cols(small_all[:, :CONV_WIDTH, D:])
    b_cols = lax.dynamic_slice_in_dim(b_ada, me * ada_n, ada_n, axis=1)
    c_act, mod_cols = _ada_fwd(c_all, w_ada[0], b_cols, name="ada_fwd")
    mod_all = _all_gather(mod_cols, name="ag_mod", in_vmem=True)
    mod_me = lax.dynamic_index_in_dim(mod_all, me, axis=1, keepdims=False).reshape(1, N_DEV * ada_n)
    sh1, sc1, gt1, sh2, sc2, gt2 = [mod_me[:, k * D:(k + 1) * D] for k in range(6)]

    h1, (in_g,) = _prenorm(x2, g_pre_mix, sc1, sh1, name="prenorm_mix", comm=[("gather_hops", w_in[0].astype(BF16))])
    w_in_p = _permute_w_in(_unshard_cols(in_g), lay)
    z, (uq_g, ukv_g, o_g) = _matmul(h1, w_in_p, mode="nn", out_dtype=BF16, name="mm_in",
                                    comm=[("gather", w_uq[0].astype(BF16)), ("gather", w_ukv[0].astype(BF16)),
                                          ("gather", w_o[0].astype(BF16))])
    w_uq_p = jnp.pad(_unshard_cols(uq_g).reshape(Q_RANK, H_MLA, MLA_QK), ((0, 0), (0, 0), (0, MLA_QK_PAD - MLA_QK))
                     ).reshape(Q_RANK, H_MLA * MLA_QK_PAD)
    w_ukv_f = _unshard_cols(ukv_g)
    w_o_f = o_g.reshape(D, D)
    cqn = _prenorm(z, g_q_lat, None, None, name="norm_cq", off=zoff["cq"], width=Q_RANK)
    ckvn = _prenorm(z, g_kv_lat, None, None, name="norm_ckv", off=zoff["ckv"], width=KV_RANK)
    q_raw = _matmul(cqn, w_uq_p, mode="nn", out_dtype=BF16, name="mm_uq")
    kv = _matmul(ckvn, w_ukv_f, mode="nn", out_dtype=BF16, name="mm_ukv")
    tab_k = _rope_tables(S, LANES, 0)
    krr = _rope(z, tab_k, heads=1, width=LANES, transpose=False, name="rope_k", off=zoff["kr"])
    (o_a, lse, Qr), (up_g,) = _flash_fwd(q_raw, kv, krr, tab_k, heads=H_MLA, name="mla_fwd",
                                        comm=[("gather", w_up[0].astype(BF16))])
    w_up_f = _unshard_cols(up_g)

    bucket, valid = _t5_bucket_table()
    onehot = (jnp.asarray(bucket).reshape(-1, 1) == jnp.arange(LANES)[None, :]).astype(F32)
    rb_pad = jnp.pad(rel_bias, ((0, LANES - REL_BUCKETS), (0, LANES - H_SWA)))
    bias_t = _matmul(onehot, rb_pad, mode="nn", out_dtype=F32, name="bias_table", tm=2048, precision=HIGHEST)
    bias_full = jnp.transpose(bias_t[:, :H_SWA].reshape(BLOCK, 2 * BLOCK, H_SWA), (2, 0, 1))
    bias_full = jnp.where(jnp.asarray(valid)[None], bias_full, NEG)
    bias_full = jnp.transpose(bias_full.reshape(SWA_KVH, G_SWA, BLOCK, 2 * BLOCK), (0, 3, 1, 2)
                              ).reshape(SWA_KVH, 2 * BLOCK, G_SWA * BLOCK)
    sink_rows = jnp.broadcast_to(sinks.reshape(SWA_KVH, G_SWA, 1), (SWA_KVH, G_SWA, BLOCK)
                                 ).reshape(SWA_KVH, 1, G_SWA * BLOCK)
    kvw = SWA_KVH * SWA_HD

    def heads_first(t, n):
        return jnp.transpose(t.reshape(S, n, SWA_HD), (1, 0, 2))

    def heads_last(t):
        return jnp.transpose(t, (1, 0, 2)).reshape(S, t.shape[0] * SWA_HD)


    ks_h = heads_first(z[:, zoff["ks"]:zoff["ks"] + kvw], SWA_KVH)
    vs_h = heads_first(z[:, zoff["vs"]:zoff["vs"] + kvw], SWA_KVH)
    o_b, _ = _swa_fwd(z, zoff["qs"], ks_h, vs_h, bias_full, sink_rows, name="swa_fwd")

    mixin = _gate_mix(z, o_a, o_b, D=D, off_a=zoff["ga"], off_b=zoff["gb"], name="gate_mix")
    mix = _matmul(mixin, w_o_f, mode="nn", out_dtype=F32, name="mm_o")
    x1 = _postnorm_res(x2, mix, gt1, g_post_mix, name="postnorm_mix")

    h2 = _prenorm(x1, g_pre_ffn, sc2, sh2, name="prenorm_ffn")
    up, (down_g,) = _matmul(h2, w_up_f, mode="nn", out_dtype=BF16, name="mm_up",
                            comm=[("gather", w_down[0].astype(BF16))])
    w_down_f = down_g.reshape(F, D)
    act = _conv_gate(up, cw_full, conv_b, name="conv_gate")
    y = _matmul(act, w_down_f, mode="nn", out_dtype=F32, name="mm_down")
    loss_part, dout, dy, dgt2, dg_post_ffn = _final_loss(x1, y, tgt, gt2, g_post_ffn, name="final_loss")
    loss = lax.psum(loss_part[0, 0], ("x", "y", "c"))

    dw_down = _matmul(act, dy, mode="tn", out_dtype=BF16, name="mm_down_dw")
    dact = _matmul(dy, w_down_f, mode="nt", out_dtype=BF16, name="mm_down_dx")
    (dup, dcw, dcb), (got_down,) = _conv_gate_bwd(up, dact, cw_full, conv_b, name="conv_gate_bwd",
                                                  comm=[("scatter", dw_down.reshape(N_DEV, F // N_DEV, D))])
    dcw = jnp.transpose(dcw, (1, 0, 2)).reshape(CONV_WIDTH, F2)
    dcb = dcb.reshape(1, F2)
    dw_up = _matmul(h2, dup, mode="tn", out_dtype=BF16, name="mm_up_dw", shard_out=True, halves=True)
    dh2 = _matmul(dup, w_up_f, mode="nt", out_dtype=F32, name="mm_up_dx", halves=True)
    dx1, dg_pre_ffn, dsc2, dsh2 = _prenorm_bwd(x1, dh2, dout, g_pre_ffn, sc2, name="prenorm_ffn_bwd", out_dtype=F32)

    dmix, dgt1, dg_post_mix = _postnorm_bwd(dx1, mix, gt1, g_post_mix, name="postnorm_mix_bwd")
    dw_o = _matmul(mixin, dmix, mode="tn", out_dtype=BF16, name="mm_o_dw")
    dmixin = _matmul(dmix, w_o_f, mode="nt", out_dtype=BF16, name="mm_o_dx")
    dga, dgb, do_a, do_b = _gate_mix_bwd(dmixin, z, o_a, o_b, D=D, off_a=zoff["ga"], off_b=zoff["gb"],
                                         name="gate_mix_bwd")
    (dq_raw, dkv, dkr_parts), (got_up,) = _flash_bwd(Qr, kv, krr, do_a, o_a, lse, tab_k, heads=H_MLA, name="mla_bwd",
                                                     comm=[("scatter", dw_up)])
    dkr = _shared_rope_grad(dkr_parts, tab_k, name="rope_k_bwd")
    dw_uq_p = _matmul(cqn, dq_raw, mode="tn", out_dtype=BF16, name="mm_uq_dw")
    dcqn = _matmul(dq_raw, w_uq_p, mode="nt", out_dtype=F32, name="mm_uq_dx")
    dw_ukv = _matmul(ckvn, dkv, mode="tn", out_dtype=BF16, name="mm_ukv_dw", shard_out=True)
    dckvn = _matmul(dkv, w_ukv_f, mode="nt", out_dtype=F32, name="mm_ukv_dx")
    dw_uq = dw_uq_p.reshape(Q_RANK, H_MLA, MLA_QK_PAD)[:, :, :MLA_QK].reshape(Q_RANK, H_MLA * MLA_QK)

    dcw_parts = jnp.pad(_shard_cols(dcw), ((0, 0), (0, 16 - CONV_WIDTH), (0, 0)))
    (dqs, dks_h, dvs_h, dbias, dsink), (got_o, got_cw, got_uq, got_ukv) = _swa_bwd(
        z, zoff["qs"], ks_h, vs_h, bias_full, sink_rows, do_b, name="swa_bwd",
        comm=[("scatter", dw_o.reshape(N_DEV, D // N_DEV, D)), ("scatter", dcw_parts),
              ("scatter", _shard_cols(dw_uq)), ("scatter", dw_ukv)])
    dbias = jnp.transpose(dbias.reshape(SWA_KVH, 2 * BLOCK, G_SWA, BLOCK), (0, 2, 3, 1))
    drel_t = _matmul(dbias.reshape(H_SWA, BLOCK * 2 * BLOCK), onehot, mode="nn", out_dtype=F32, name="bias_grad",
                     tk=4096, precision=HIGHEST)
    d_rel_bias = jnp.transpose(drel_t[:, :REL_BUCKETS])
    d_sinks = jnp.sum(dsink.reshape(SWA_KVH, G_SWA, BLOCK), axis=-1).reshape(1, H_SWA)

    dcq, dg_q = _prenorm_bwd(z, dcqn, None, g_q_lat, None, name="norm_cq_bwd", out_dtype=BF16,
                             off=zoff["cq"], width=Q_RANK)
    dckv, dg_kv = _prenorm_bwd(z, dckvn, None, g_kv_lat, None, name="norm_ckv_bwd", out_dtype=BF16,
                               off=zoff["ckv"], width=KV_RANK)
    dz = _assemble_dz({"ga": dga, "gb": dgb, "qs": dqs, "cq": dcq, "ckv": dckv,
                       "ks": heads_last(dks_h), "vs": heads_last(dvs_h), "kr": dkr}, lay, S)
    dw_in_a = _matmul(h1, dz, mode="tn", out_dtype=BF16, name="mm_in_dw_a", m_range=(0, D // 2))
    dw_in_b, (got_in_a,) = _matmul(h1, dz, mode="tn", out_dtype=BF16, name="mm_in_dw_b", m_range=(D // 2, D // 2),
                                   comm=[("scatter", _shard_cols(_unpermute_w_in(dw_in_a, lay)))])
    dh1, (got_in_b,) = _matmul(dz, w_in_p, mode="nt", out_dtype=F32, name="mm_in_dx",
                               comm=[("scatter", _shard_cols(_unpermute_w_in(dw_in_b, lay)))])
    grad_x, dg_pre_mix, dsc1, dsh1 = _prenorm_bwd(x2, dh1, dx1, g_pre_mix, sc1, name="prenorm_mix_bwd",
                                                  out_dtype=F32)
    dmod = jnp.concatenate([dsh1, dsc1, dgt1, dsh2, dsc2, dgt2], axis=1)

    small_names = ["b_ada", "g_pre_mix", "g_post_mix", "g_q_lat", "g_kv_lat", "rel_bias", "sinks", "g_pre_ffn",
                   "g_post_ffn", "conv_b"]
    small_w = [b_ada, g_pre_mix, g_post_mix, g_q_lat, g_kv_lat, rel_bias, sinks, g_pre_ffn, g_post_ffn, conv_b]
    small_m = [m_b_ada, m_g_pre_mix, m_g_post_mix, m_g_q_lat, m_g_kv_lat, m_rel_bias, m_sinks, m_g_pre_ffn,
               m_g_post_ffn, m_conv_b]
    small_v = [v_b_ada, v_g_pre_mix, v_g_post_mix, v_g_q_lat, v_g_kv_lat, v_rel_bias, v_sinks, v_g_pre_ffn,
               v_g_post_ffn, v_conv_b]
    small_g = [dmod, dg_pre_mix, dg_post_mix, dg_q, dg_kv, d_rel_bias, d_sinks, dg_pre_ffn, dg_post_ffn, dcb]
    n_small = sum(int(np.prod(w.shape)) for w in small_w)
    rows = _round_up(-(-n_small // LANES), 16)
    parts_small = _all_gather(_pack(small_g, rows), name="ag_small_grads", in_vmem=True)
    sg, sd, sm, sv = _adamw(_pack(small_w, rows), _pack(small_m, rows), _pack(small_v, rows), parts_small,
                            name="adamw_small")

    def unpack(packed):
        flat, out, at = packed.reshape(-1), {}, 0
        for nm, w in zip(small_names, small_w):
            n = int(np.prod(w.shape))
            out[nm] = flat[at:at + n].reshape(w.shape)
            at += n
        return out

    small_out = [unpack(t) for t in (sg, sd, sm, sv)]

    dmod_all = parts_small.reshape(N_DEV, rows * LANES)[:, :6 * D]
    dmod_cols = lax.dynamic_slice_in_dim(dmod_all, me * ada_n, ada_n, axis=1)
    ada_out = _adamw_ada(w_ada[0], m_w_ada[0], v_w_ada[0], jnp.transpose(c_act), dmod_cols, name="adamw_w_ada")

    def owner_update(got, w, m, v, name):
        shp = w.shape
        w2, m2, v2 = (t.reshape(shp[-2], shp[-1]) for t in (w, m, v))
        return [t.reshape(shp) for t in _adamw(w2, m2, v2, got, name="adamw_" + name)]

    def pad_rows(t):
        return jnp.pad(t[0], ((0, 16 - CONV_WIDTH), (0, 0)))

    big = {
        "w_in": owner_update([got_in_a, got_in_b], w_in, m_w_in, v_w_in, "w_in"),
        "w_uq": owner_update(got_uq, w_uq, m_w_uq, v_w_uq, "w_uq"),
        "w_ukv": owner_update(got_ukv, w_ukv, m_w_ukv, v_w_ukv, "w_ukv"),
        "w_o": owner_update(got_o, w_o, m_w_o, v_w_o, "w_o"),
        "w_up": owner_update(got_up, w_up, m_w_up, v_w_up, "w_up"),
        "w_down": owner_update(got_down, w_down, m_w_down, v_w_down, "w_down"),
    }
    cw_upd = _adamw(pad_rows(conv_w), pad_rows(m_conv_w), pad_rows(v_conv_w), got_cw, name="adamw_conv_w")
    big["conv_w"] = [t[:CONV_WIDTH].reshape(conv_w.shape) for t in cw_upd]
    big["w_ada"] = [t.reshape(w_ada.shape) for t in ada_out]

    order = ["w_ada", "b_ada", "g_pre_mix", "g_post_mix", "w_in", "g_q_lat", "w_uq", "g_kv_lat", "w_ukv", "rel_bias",
             "sinks", "w_o", "g_pre_ffn", "g_post_ffn", "w_up", "conv_w", "conv_b", "w_down"]
    outs = [loss, grad_x.reshape(x.shape)]
    for kind in range(4):
        for nm in order:
            outs.append(big[nm][kind] if nm in big else small_out[kind][nm])
    return tuple(outs)
```
